```python
import jax
import jax.numpy as jnp
from jax import lax
import numpy as np

D_MODEL = 2048
BATCH = 8
SEQ = 4096
DEPTH = 1

D_MIX = D_MODEL
V_HEAD_DIM = 128
MLA_WIDTH = D_MIX // 2
MLA_HEADS = MLA_WIDTH // V_HEAD_DIM
QK_NOPE_DIM = 128
QK_ROPE_DIM = 64
QK_HEAD_DIM = QK_NOPE_DIM + QK_ROPE_DIM
Q_LORA_RANK = 512
KV_LORA_RANK = 512
ROPE_THETA = 10000.0
Q_BLOCK = 128
SSD_WIDTH = D_MIX - MLA_WIDTH
SSD_HEAD_DIM = 64
SSD_HEADS = SSD_WIDTH // SSD_HEAD_DIM
SSD_GROUPS = 2
SSD_HEADS_PER_GROUP = SSD_HEADS // SSD_GROUPS
SSD_STATE = 128
SSD_CONV = 4
SSD_CHUNK = 128
SSD_CONV_DIM = SSD_WIDTH + 2 * SSD_GROUPS * SSD_STATE
D_FF = -(-8 * D_MODEL // (3 * 256)) * 256
IN_SIZES = (Q_LORA_RANK, KV_LORA_RANK, QK_ROPE_DIM, SSD_WIDTH, SSD_CONV_DIM, SSD_HEADS)
D_IN = Q_LORA_RANK + KV_LORA_RANK + QK_ROPE_DIM + SSD_WIDTH + SSD_CONV_DIM + SSD_HEADS
EPS = 1e-6

kernel_name = "hymba_mla_ssd_sandwich_layer"


def rms_norm(t, w):
    tf = t.astype(jnp.float32)
    y = tf * lax.rsqrt(jnp.mean(tf * tf, axis=-1, keepdims=True) + EPS)
    return (y * w.astype(jnp.float32)).astype(t.dtype)


def split_cols(t, sizes):
    offsets = np.cumsum(np.array(sizes))[:-1].tolist()
    return jnp.split(t, offsets, axis=-1)


def rope_tables(positions):
    inv_freq = ROPE_THETA ** (-jnp.arange(0, QK_ROPE_DIM, 2, dtype=jnp.float32) / QK_ROPE_DIM)
    ang = positions.astype(jnp.float32)[..., None] * inv_freq
    return jnp.cos(ang), jnp.sin(ang)


def apply_rope(t, cos, sin):
    t1, t2 = jnp.split(t.astype(jnp.float32), 2, axis=-1)
    return jnp.concatenate([t1 * cos - t2 * sin, t2 * cos + t1 * sin], axis=-1).astype(t.dtype)


def mla_group(c_q, c_kv, k_rope, cos, sin, q_norm_w, w_uq, kv_norm_w, w_ukv):
    b, s, _ = c_q.shape
    q = (rms_norm(c_q, q_norm_w) @ w_uq).reshape(b, s, MLA_HEADS, QK_HEAD_DIM)
    q_nope, q_rope = q[..., :QK_NOPE_DIM], q[..., QK_NOPE_DIM:]
    kv = (rms_norm(c_kv, kv_norm_w) @ w_ukv).reshape(b, s, MLA_HEADS, QK_NOPE_DIM + V_HEAD_DIM)
    k_nope, v = kv[..., :QK_NOPE_DIM], kv[..., QK_NOPE_DIM:]
    q_rope = apply_rope(q_rope, cos[:, :, None, :], sin[:, :, None, :])
    k_rope = apply_rope(k_rope, cos, sin)
    scale = QK_HEAD_DIM ** -0.5
    n_blk = s // Q_BLOCK
    qn_blocks = jnp.moveaxis(q_nope.reshape(b, n_blk, Q_BLOCK, MLA_HEADS, QK_NOPE_DIM), 1, 0)
    qr_blocks = jnp.moveaxis(q_rope.reshape(b, n_blk, Q_BLOCK, MLA_HEADS, QK_ROPE_DIM), 1, 0)
    key_idx = jnp.arange(s)

    def attend(args):
        blk, qn, qr = args
        sc = (jnp.einsum('bqhd,bkhd->bhqk', qn, k_nope, preferred_element_type=jnp.float32)
              + jnp.einsum('bqhr,bkr->bhqk', qr, k_rope, preferred_element_type=jnp.float32)) * scale
        q_idx = blk * Q_BLOCK + jnp.arange(Q_BLOCK)
        causal = key_idx[None, :] <= q_idx[:, None]
        p = jax.nn.softmax(jnp.where(causal, sc, -jnp.inf), axis=-1).astype(v.dtype)
        return jnp.einsum('bhqk,bkhd->bqhd', p, v)

    o = lax.map(attend, (jnp.arange(n_blk), qn_blocks, qr_blocks))
    return jnp.moveaxis(o, 0, 1).reshape(b, s, MLA_HEADS * V_HEAD_DIM)


def causal_depthwise_conv(t, w, bias):
    y = lax.conv_general_dilated(t, w[:, None, :], window_strides=(1,), padding=[(SSD_CONV - 1, 0)],
                                 dimension_numbers=('NWC', 'WIO', 'NWC'), feature_group_count=t.shape[-1])
    return y + bias


def ssd_group(z, xbc, dt_raw, conv_w, conv_b, dt_bias, a_log, d_skip, norm_w):
    b, s, _ = z.shape
    G, E, P, N, T = SSD_GROUPS, SSD_HEADS_PER_GROUP, SSD_HEAD_DIM, SSD_STATE, SSD_CHUNK
    c = s // T
    xbc = jax.nn.silu(causal_depthwise_conv(xbc, conv_w, conv_b))
    xs, bm, cm = split_cols(xbc, (SSD_WIDTH, G * N, G * N))
    dt = jax.nn.softplus(dt_raw.astype(jnp.float32) + dt_bias.astype(jnp.float32))
    a_neg = -jnp.exp(a_log.astype(jnp.float32)).reshape(G, E)
    x = xs.astype(jnp.float32).reshape(b, c, T, G, E, P)
    dt_c = dt.reshape(b, c, T, G, E)
    bc = bm.astype(jnp.float32).reshape(b, c, T, G, N)
    cc = cm.astype(jnp.float32).reshape(b, c, T, G, N)
    xdt = x * dt_c[..., None]
    a_cum = jnp.cumsum(jnp.transpose(dt_c * a_neg, (0, 1, 3, 4, 2)), axis=-1)
    seg = a_cum[..., :, None] - a_cum[..., None, :]
    tri = jnp.tril(jnp.ones((T, T), dtype=bool))
    decay = jnp.exp(jnp.where(tri, seg, -jnp.inf))
    cb = jnp.einsum('bclgn,bcsgn->bcgls', cc, bc)
    y_diag = jnp.einsum('bcgels,bcsgep->bclgep', cb[:, :, :, None] * decay, xdt)
    decay_states = jnp.exp(a_cum[..., -1:] - a_cum)
    states = jnp.einsum('bcsgn,bcsgep->bcgepn', bc, xdt * jnp.transpose(decay_states, (0, 1, 4, 2, 3))[..., None])
    chunk_decay = jnp.exp(a_cum[..., -1])

    def step(h, inp):
        st, dec = inp
        return h * dec[..., None, None] + st, h

    h0 = jnp.zeros((b, G, E, P, N), jnp.float32)
    _, prev = lax.scan(step, h0, (jnp.moveaxis(states, 1, 0), jnp.moveaxis(chunk_decay, 1, 0)))
    prev = jnp.moveaxis(prev, 0, 1)
    y_off = jnp.einsum('bclgn,bcgepn->bclgep', cc, prev) * jnp.transpose(jnp.exp(a_cum), (0, 1, 4, 2, 3))[..., None]
    y = (y_diag + y_off + x * d_skip.astype(jnp.float32).reshape(G, E, 1)).reshape(b, s, SSD_WIDTH)
    g = (y * jax.nn.silu(z.astype(jnp.float32))).reshape(b, s, G, SSD_WIDTH // G)
    g = g * lax.rsqrt(jnp.mean(g * g, axis=-1, keepdims=True) + EPS)
    return (g.reshape(b, s, SSD_WIDTH) * norm_w.astype(jnp.float32)).astype(z.dtype)


def _fwd_setup_inputs(seed: int = 0) -> dict:
    key = jax.random.key(seed)
    ks = jax.random.split(key, 24)
    f32 = jnp.float32
    L = DEPTH

    def dense(k, fan_in, fan_out):
        return jax.random.normal(k, (L, fan_in, fan_out), f32) * fan_in ** -0.5

    def gain(k, n):
        return 1.0 + 0.02 * jax.random.normal(k, (L, n), f32)

    x = jax.random.normal(ks[0], (BATCH, SEQ, D_MODEL), f32)
    positions = jnp.arange(SEQ, dtype=jnp.int32)[None, :] + jax.random.randint(ks[1], (BATCH, 1), 0, SEQ, dtype=jnp.int32)
    dt0 = jnp.exp(jax.random.uniform(ks[2], (L, SSD_HEADS), f32, float(np.log(1e-3)), float(np.log(1e-1))))
    dt_bias = dt0 + jnp.log(-jnp.expm1(-dt0))
    a_log = jnp.log(jax.random.uniform(ks[3], (L, SSD_HEADS), f32, 1.0, 16.0))
    return {
        "x": x,
        "positions": positions,
        "w_in": dense(ks[4], D_MODEL, D_IN),
        "q_norm_w": gain(ks[5], Q_LORA_RANK),
        "w_uq": dense(ks[6], Q_LORA_RANK, MLA_HEADS * QK_HEAD_DIM),
        "kv_norm_w": gain(ks[7], KV_LORA_RANK),
        "w_ukv": dense(ks[8], KV_LORA_RANK, MLA_HEADS * (QK_NOPE_DIM + V_HEAD_DIM)),
        "conv_w": jax.random.normal(ks[9], (L, SSD_CONV, SSD_CONV_DIM), f32) * SSD_CONV ** -0.5,
        "conv_b": 0.02 * jax.random.normal(ks[10], (L, SSD_CONV_DIM), f32),
        "dt_bias": dt_bias,
        "a_log": a_log,
        "d_skip": gain(ks[11], SSD_HEADS),
        "ssd_norm_w": gain(ks[12], SSD_WIDTH),
        "attn_out_norm_w": gain(ks[13], MLA_WIDTH),
        "w_out": dense(ks[14], D_MIX, D_MODEL),
        "pre_mix_norm_w": gain(ks[15], D_MODEL),
        "post_mix_norm_w": gain(ks[16], D_MODEL),
        "pre_ffn_norm_w": gain(ks[17], D_MODEL),
        "post_ffn_norm_w": gain(ks[18], D_MODEL),
        "w_gate": dense(ks[19], D_MODEL, D_FF),
        "w_up": dense(ks[20], D_MODEL, D_FF),
        "w_down": dense(ks[21], D_FF, D_MODEL),
    }


def _fwd_reference(x, positions, w_in, q_norm_w, w_uq, kv_norm_w, w_ukv, conv_w, conv_b, dt_bias, a_log,
              d_skip, ssd_norm_w, attn_out_norm_w, w_out, pre_mix_norm_w, post_mix_norm_w,
              pre_ffn_norm_w, post_ffn_norm_w, w_gate, w_up, w_down):
    cos, sin = rope_tables(positions)
    h = x
    for l in range(DEPTH):
        u = rms_norm(h, pre_mix_norm_w[l])
        c_q, c_kv, k_rope, z, xbc, dt_raw = split_cols(u @ w_in[l], IN_SIZES)
        attn = rms_norm(mla_group(c_q, c_kv, k_rope, cos, sin, q_norm_w[l], w_uq[l], kv_norm_w[l], w_ukv[l]),
                        attn_out_norm_w[l])
        ssm = ssd_group(z, xbc, dt_raw, conv_w[l], conv_b[l], dt_bias[l], a_log[l], d_skip[l], ssd_norm_w[l])
        mix = jnp.concatenate([attn, ssm], axis=-1) @ w_out[l]
        h = h + rms_norm(mix, post_mix_norm_w[l])
        v = rms_norm(h, pre_ffn_norm_w[l])
        ffn = (jax.nn.silu(v @ w_gate[l]) * (v @ w_up[l])) @ w_down[l]
        h = h + rms_norm(ffn, post_ffn_norm_w[l])
    return h


import jax as _jax
import jax.numpy as _jnp

TWIN_FORMAT = 'train_step'
FWD_PARAMS = ['x', 'positions', 'w_in', 'q_norm_w', 'w_uq', 'kv_norm_w', 'w_ukv', 'conv_w', 'conv_b', 'dt_bias', 'a_log', 'd_skip', 'ssd_norm_w', 'attn_out_norm_w', 'w_out', 'pre_mix_norm_w', 'post_mix_norm_w', 'pre_ffn_norm_w', 'post_ffn_norm_w', 'w_gate', 'w_up', 'w_down']
TWIN_WEIGHTS = ['w_in', 'q_norm_w', 'w_uq', 'kv_norm_w', 'w_ukv', 'conv_w', 'conv_b', 'dt_bias', 'a_log', 'd_skip', 'ssd_norm_w', 'attn_out_norm_w', 'w_out', 'pre_mix_norm_w', 'post_mix_norm_w', 'pre_ffn_norm_w', 'post_ffn_norm_w', 'w_gate', 'w_up', 'w_down']
TWIN_DIFF_INPUT = 'x'
TWIN_INPUTS = ['x', 'positions', 'w_in', 'q_norm_w', 'w_uq', 'kv_norm_w', 'w_ukv', 'conv_w', 'conv_b', 'dt_bias', 'a_log', 'd_skip', 'ssd_norm_w', 'attn_out_norm_w', 'w_out', 'pre_mix_norm_w', 'post_mix_norm_w', 'pre_ffn_norm_w', 'post_ffn_norm_w', 'w_gate', 'w_up', 'w_down', 'loss_target', 'm_w_in', 'm_q_norm_w', 'm_w_uq', 'm_kv_norm_w', 'm_w_ukv', 'm_conv_w', 'm_conv_b', 'm_dt_bias', 'm_a_log', 'm_d_skip', 'm_ssd_norm_w', 'm_attn_out_norm_w', 'm_w_out', 'm_pre_mix_norm_w', 'm_post_mix_norm_w', 'm_pre_ffn_norm_w', 'm_post_ffn_norm_w', 'm_w_gate', 'm_w_up', 'm_w_down', 'v_w_in', 'v_q_norm_w', 'v_w_uq', 'v_kv_norm_w', 'v_w_ukv', 'v_conv_w', 'v_conv_b', 'v_dt_bias', 'v_a_log', 'v_d_skip', 'v_ssd_norm_w', 'v_attn_out_norm_w', 'v_w_out', 'v_pre_mix_norm_w', 'v_post_mix_norm_w', 'v_pre_ffn_norm_w', 'v_post_ffn_norm_w', 'v_w_gate', 'v_w_up', 'v_w_down']
TWIN_OUTPUTS = ['loss', 'grad_x', 'grad_w_in', 'grad_q_norm_w', 'grad_w_uq', 'grad_kv_norm_w', 'grad_w_ukv', 'grad_conv_w', 'grad_conv_b', 'grad_dt_bias', 'grad_a_log', 'grad_d_skip', 'grad_ssd_norm_w', 'grad_attn_out_norm_w', 'grad_w_out', 'grad_pre_mix_norm_w', 'grad_post_mix_norm_w', 'grad_pre_ffn_norm_w', 'grad_post_ffn_norm_w', 'grad_w_gate', 'grad_w_up', 'grad_w_down', 'delta_w_in', 'delta_q_norm_w', 'delta_w_uq', 'delta_kv_norm_w', 'delta_w_ukv', 'delta_conv_w', 'delta_conv_b', 'delta_dt_bias', 'delta_a_log', 'delta_d_skip', 'delta_ssd_norm_w', 'delta_attn_out_norm_w', 'delta_w_out', 'delta_pre_mix_norm_w', 'delta_post_mix_norm_w', 'delta_pre_ffn_norm_w', 'delta_post_ffn_norm_w', 'delta_w_gate', 'delta_w_up', 'delta_w_down', 'new_m_w_in', 'new_m_q_norm_w', 'new_m_w_uq', 'new_m_kv_norm_w', 'new_m_w_ukv', 'new_m_conv_w', 'new_m_conv_b', 'new_m_dt_bias', 'new_m_a_log', 'new_m_d_skip', 'new_m_ssd_norm_w', 'new_m_attn_out_norm_w', 'new_m_w_out', 'new_m_pre_mix_norm_w', 'new_m_post_mix_norm_w', 'new_m_pre_ffn_norm_w', 'new_m_post_ffn_norm_w', 'new_m_w_gate', 'new_m_w_up', 'new_m_w_down', 'new_v_w_in', 'new_v_q_norm_w', 'new_v_w_uq', 'new_v_kv_norm_w', 'new_v_w_ukv', 'new_v_conv_w', 'new_v_conv_b', 'new_v_dt_bias', 'new_v_a_log', 'new_v_d_skip', 'new_v_ssd_norm_w', 'new_v_attn_out_norm_w', 'new_v_w_out', 'new_v_pre_mix_norm_w', 'new_v_post_mix_norm_w', 'new_v_pre_ffn_norm_w', 'new_v_post_ffn_norm_w', 'new_v_w_gate', 'new_v_w_up', 'new_v_w_down']
TWIN_LEAF_KINDS = {'loss': 'loss', 'grad_x': 'grad_x', 'grad_w_in': 'grad_w', 'grad_q_norm_w': 'grad_w', 'grad_w_uq': 'grad_w', 'grad_kv_norm_w': 'grad_w', 'grad_w_ukv': 'grad_w', 'grad_conv_w': 'grad_w', 'grad_conv_b': 'grad_w', 'grad_dt_bias': 'grad_w', 'grad_a_log': 'grad_w', 'grad_d_skip': 'grad_w', 'grad_ssd_norm_w': 'grad_w', 'grad_attn_out_norm_w': 'grad_w', 'grad_w_out': 'grad_w', 'grad_pre_mix_norm_w': 'grad_w', 'grad_post_mix_norm_w': 'grad_w', 'grad_pre_ffn_norm_w': 'grad_w', 'grad_post_ffn_norm_w': 'grad_w', 'grad_w_gate': 'grad_w', 'grad_w_up': 'grad_w', 'grad_w_down': 'grad_w', 'delta_w_in': 'delta_w', 'delta_q_norm_w': 'delta_w', 'delta_w_uq': 'delta_w', 'delta_kv_norm_w': 'delta_w', 'delta_w_ukv': 'delta_w', 'delta_conv_w': 'delta_w', 'delta_conv_b': 'delta_w', 'delta_dt_bias': 'delta_w', 'delta_a_log': 'delta_w', 'delta_d_skip': 'delta_w', 'delta_ssd_norm_w': 'delta_w', 'delta_attn_out_norm_w': 'delta_w', 'delta_w_out': 'delta_w', 'delta_pre_mix_norm_w': 'delta_w', 'delta_post_mix_norm_w': 'delta_w', 'delta_pre_ffn_norm_w': 'delta_w', 'delta_post_ffn_norm_w': 'delta_w', 'delta_w_gate': 'delta_w', 'delta_w_up': 'delta_w', 'delta_w_down': 'delta_w', 'new_m_w_in': 'new_m', 'new_m_q_norm_w': 'new_m', 'new_m_w_uq': 'new_m', 'new_m_kv_norm_w': 'new_m', 'new_m_w_ukv': 'new_m', 'new_m_conv_w': 'new_m', 'new_m_conv_b': 'new_m', 'new_m_dt_bias': 'new_m', 'new_m_a_log': 'new_m', 'new_m_d_skip': 'new_m', 'new_m_ssd_norm_w': 'new_m', 'new_m_attn_out_norm_w': 'new_m', 'new_m_w_out': 'new_m', 'new_m_pre_mix_norm_w': 'new_m', 'new_m_post_mix_norm_w': 'new_m', 'new_m_pre_ffn_norm_w': 'new_m', 'new_m_post_ffn_norm_w': 'new_m', 'new_m_w_gate': 'new_m', 'new_m_w_up': 'new_m', 'new_m_w_down': 'new_m', 'new_v_w_in': 'new_v', 'new_v_q_norm_w': 'new_v', 'new_v_w_uq': 'new_v', 'new_v_kv_norm_w': 'new_v', 'new_v_w_ukv': 'new_v', 'new_v_conv_w': 'new_v', 'new_v_conv_b': 'new_v', 'new_v_dt_bias': 'new_v', 'new_v_a_log': 'new_v', 'new_v_d_skip': 'new_v', 'new_v_ssd_norm_w': 'new_v', 'new_v_attn_out_norm_w': 'new_v', 'new_v_w_out': 'new_v', 'new_v_pre_mix_norm_w': 'new_v', 'new_v_post_mix_norm_w': 'new_v', 'new_v_pre_ffn_norm_w': 'new_v', 'new_v_post_ffn_norm_w': 'new_v', 'new_v_w_gate': 'new_v', 'new_v_w_up': 'new_v', 'new_v_w_down': 'new_v'}


def _forward(args):
    return _fwd_reference(*[args[k] for k in FWD_PARAMS])


def _output_shape():
    out = _jax.eval_shape(lambda: _forward(_fwd_setup_inputs(0)))
    return out.shape, out.dtype

N_MICROBATCH = 1
ADAM_LR = 0.001
ADAM_B1 = 0.9
ADAM_B2 = 0.999
ADAM_EPS = 1e-08
ADAM_WD = 0.01
ADAM_STEP = 10
PER_EXAMPLE_BATCH_AXIS = {'x': 0, 'positions': 0, 'loss_target': 0}
SHARED_INPUTS = []
_WEIGHT_DTYPES = {'w_in': _jnp.float32, 'q_norm_w': _jnp.float32, 'w_uq': _jnp.float32, 'kv_norm_w': _jnp.float32, 'w_ukv': _jnp.float32, 'conv_w': _jnp.float32, 'conv_b': _jnp.float32, 'dt_bias': _jnp.float32, 'a_log': _jnp.float32, 'd_skip': _jnp.float32, 'ssd_norm_w': _jnp.float32, 'attn_out_norm_w': _jnp.float32, 'w_out': _jnp.float32, 'pre_mix_norm_w': _jnp.float32, 'post_mix_norm_w': _jnp.float32, 'pre_ffn_norm_w': _jnp.float32, 'post_ffn_norm_w': _jnp.float32, 'w_gate': _jnp.float32, 'w_up': _jnp.float32, 'w_down': _jnp.float32}
MOMENT_SCALE = {'w_in': 2.941031e-01, 'q_norm_w': 3.650768e-01, 'w_uq': 2.067330e-01, 'kv_norm_w': 6.900135e-01, 'w_ukv': 3.197927e-01, 'conv_w': 2.355752e-01, 'conv_b': 5.783368e-01, 'dt_bias': 5.616195e-01, 'a_log': 1.922125e+00, 'd_skip': 1.531581e+00, 'ssd_norm_w': 3.486621e-01, 'attn_out_norm_w': 4.014557e-01, 'w_out': 3.702638e-01, 'pre_mix_norm_w': 3.930488e-01, 'post_mix_norm_w': 1.599574e+01, 'pre_ffn_norm_w': 3.429417e-01, 'post_ffn_norm_w': 1.597011e+01, 'w_gate': 1.148831e-01, 'w_up': 1.628090e-01, 'w_down': 2.687829e-01}


def _to_microbatches(a, axis):
    t = _jnp.moveaxis(a, axis, 0)
    t = t.reshape((N_MICROBATCH, t.shape[0] // N_MICROBATCH) + t.shape[1:])
    return _jnp.moveaxis(t, 1, axis + 1)


def setup_inputs(seed: int = 0) -> dict:
    inp = _fwd_setup_inputs(seed)
    key = _jax.random.fold_in(_jax.random.key(seed), 7919)
    shape, _ = _output_shape()
    out = dict(inp)
    out["loss_target"] = _jax.random.normal(_jax.random.fold_in(key, 0), shape, _jnp.float32)
    for i, name in enumerate(TWIN_WEIGHTS):
        w = inp[name].astype(_jnp.float32)
        if MOMENT_SCALE is None:
            s = _jnp.sqrt(_jnp.mean(_jnp.square(w)) + 1e-30)
        else:
            s = MOMENT_SCALE[name]
        km, kv = _jax.random.split(_jax.random.fold_in(key, i + 1))
        out[name] = w
        out["m_" + name] = s * _jax.random.normal(km, w.shape, _jnp.float32)
        out["v_" + name] = (s * s) * _jax.random.uniform(kv, w.shape, _jnp.float32, 0.5, 1.5)
    if N_MICROBATCH > 1:
        for name, axis in PER_EXAMPLE_BATCH_AXIS.items():
            out[name] = _to_microbatches(out[name], axis)
    return {'x': out['x'], 'positions': out['positions'], 'w_in': out['w_in'], 'q_norm_w': out['q_norm_w'], 'w_uq': out['w_uq'], 'kv_norm_w': out['kv_norm_w'], 'w_ukv': out['w_ukv'], 'conv_w': out['conv_w'], 'conv_b': out['conv_b'], 'dt_bias': out['dt_bias'], 'a_log': out['a_log'], 'd_skip': out['d_skip'], 'ssd_norm_w': out['ssd_norm_w'], 'attn_out_norm_w': out['attn_out_norm_w'], 'w_out': out['w_out'], 'pre_mix_norm_w': out['pre_mix_norm_w'], 'post_mix_norm_w': out['post_mix_norm_w'], 'pre_ffn_norm_w': out['pre_ffn_norm_w'], 'post_ffn_norm_w': out['post_ffn_norm_w'], 'w_gate': out['w_gate'], 'w_up': out['w_up'], 'w_down': out['w_down'], 'loss_target': out['loss_target'], 'm_w_in': out['m_w_in'], 'm_q_norm_w': out['m_q_norm_w'], 'm_w_uq': out['m_w_uq'], 'm_kv_norm_w': out['m_kv_norm_w'], 'm_w_ukv': out['m_w_ukv'], 'm_conv_w': out['m_conv_w'], 'm_conv_b': out['m_conv_b'], 'm_dt_bias': out['m_dt_bias'], 'm_a_log': out['m_a_log'], 'm_d_skip': out['m_d_skip'], 'm_ssd_norm_w': out['m_ssd_norm_w'], 'm_attn_out_norm_w': out['m_attn_out_norm_w'], 'm_w_out': out['m_w_out'], 'm_pre_mix_norm_w': out['m_pre_mix_norm_w'], 'm_post_mix_norm_w': out['m_post_mix_norm_w'], 'm_pre_ffn_norm_w': out['m_pre_ffn_norm_w'], 'm_post_ffn_norm_w': out['m_post_ffn_norm_w'], 'm_w_gate': out['m_w_gate'], 'm_w_up': out['m_w_up'], 'm_w_down': out['m_w_down'], 'v_w_in': out['v_w_in'], 'v_q_norm_w': out['v_q_norm_w'], 'v_w_uq': out['v_w_uq'], 'v_kv_norm_w': out['v_kv_norm_w'], 'v_w_ukv': out['v_w_ukv'], 'v_conv_w': out['v_conv_w'], 'v_conv_b': out['v_conv_b'], 'v_dt_bias': out['v_dt_bias'], 'v_a_log': out['v_a_log'], 'v_d_skip': out['v_d_skip'], 'v_ssd_norm_w': out['v_ssd_norm_w'], 'v_attn_out_norm_w': out['v_attn_out_norm_w'], 'v_w_out': out['v_w_out'], 'v_pre_mix_norm_w': out['v_pre_mix_norm_w'], 'v_post_mix_norm_w': out['v_post_mix_norm_w'], 'v_pre_ffn_norm_w': out['v_pre_ffn_norm_w'], 'v_post_ffn_norm_w': out['v_post_ffn_norm_w'], 'v_w_gate': out['v_w_gate'], 'v_w_up': out['v_w_up'], 'v_w_down': out['v_w_down']}


def _loss(weights, diff, rest, loss_target):
    with _jax.named_scope("forward"):
        args = {**rest, TWIN_DIFF_INPUT: diff, **{k: w.astype(_WEIGHT_DTYPES[k]) for k, w in weights.items()}}
        y = _forward(args)
    with _jax.named_scope("loss_head"):
        err = _jnp.square(y.astype(_jnp.float32) - loss_target)
        return 0.5 * _jnp.sum(_jnp.mean(err, axis=-1)) if err.ndim else 0.5 * err


def _adamw(w, g, m, v):
    m = ADAM_B1 * m + (1.0 - ADAM_B1) * g
    v = ADAM_B2 * v + (1.0 - ADAM_B2) * _jnp.square(g)
    m_hat = m / (1.0 - ADAM_B1 ** ADAM_STEP)
    v_hat = v / (1.0 - ADAM_B2 ** ADAM_STEP)
    delta = -ADAM_LR * (m_hat / (_jnp.sqrt(v_hat) + ADAM_EPS) + ADAM_WD * w)
    return delta, m, v


def reference(x, positions, w_in, q_norm_w, w_uq, kv_norm_w, w_ukv, conv_w, conv_b, dt_bias, a_log, d_skip, ssd_norm_w, attn_out_norm_w, w_out, pre_mix_norm_w, post_mix_norm_w, pre_ffn_norm_w, post_ffn_norm_w, w_gate, w_up, w_down, loss_target, m_w_in, m_q_norm_w, m_w_uq, m_kv_norm_w, m_w_ukv, m_conv_w, m_conv_b, m_dt_bias, m_a_log, m_d_skip, m_ssd_norm_w, m_attn_out_norm_w, m_w_out, m_pre_mix_norm_w, m_post_mix_norm_w, m_pre_ffn_norm_w, m_post_ffn_norm_w, m_w_gate, m_w_up, m_w_down, v_w_in, v_q_norm_w, v_w_uq, v_kv_norm_w, v_w_ukv, v_conv_w, v_conv_b, v_dt_bias, v_a_log, v_d_skip, v_ssd_norm_w, v_attn_out_norm_w, v_w_out, v_pre_mix_norm_w, v_post_mix_norm_w, v_pre_ffn_norm_w, v_post_ffn_norm_w, v_w_gate, v_w_up, v_w_down):
    given = dict(x=x, positions=positions, w_in=w_in, q_norm_w=q_norm_w, w_uq=w_uq, kv_norm_w=kv_norm_w, w_ukv=w_ukv, conv_w=conv_w, conv_b=conv_b, dt_bias=dt_bias, a_log=a_log, d_skip=d_skip, ssd_norm_w=ssd_norm_w, attn_out_norm_w=attn_out_norm_w, w_out=w_out, pre_mix_norm_w=pre_mix_norm_w, post_mix_norm_w=post_mix_norm_w, pre_ffn_norm_w=pre_ffn_norm_w, post_ffn_norm_w=post_ffn_norm_w, w_gate=w_gate, w_up=w_up, w_down=w_down, loss_target=loss_target, m_w_in=m_w_in, m_q_norm_w=m_q_norm_w, m_w_uq=m_w_uq, m_kv_norm_w=m_kv_norm_w, m_w_ukv=m_w_ukv, m_conv_w=m_conv_w, m_conv_b=m_conv_b, m_dt_bias=m_dt_bias, m_a_log=m_a_log, m_d_skip=m_d_skip, m_ssd_norm_w=m_ssd_norm_w, m_attn_out_norm_w=m_attn_out_norm_w, m_w_out=m_w_out, m_pre_mix_norm_w=m_pre_mix_norm_w, m_post_mix_norm_w=m_post_mix_norm_w, m_pre_ffn_norm_w=m_pre_ffn_norm_w, m_post_ffn_norm_w=m_post_ffn_norm_w, m_w_gate=m_w_gate, m_w_up=m_w_up, m_w_down=m_w_down, v_w_in=v_w_in, v_q_norm_w=v_q_norm_w, v_w_uq=v_w_uq, v_kv_norm_w=v_kv_norm_w, v_w_ukv=v_w_ukv, v_conv_w=v_conv_w, v_conv_b=v_conv_b, v_dt_bias=v_dt_bias, v_a_log=v_a_log, v_d_skip=v_d_skip, v_ssd_norm_w=v_ssd_norm_w, v_attn_out_norm_w=v_attn_out_norm_w, v_w_out=v_w_out, v_pre_mix_norm_w=v_pre_mix_norm_w, v_post_mix_norm_w=v_post_mix_norm_w, v_pre_ffn_norm_w=v_pre_ffn_norm_w, v_post_ffn_norm_w=v_post_ffn_norm_w, v_w_gate=v_w_gate, v_w_up=v_w_up, v_w_down=v_w_down)
    weights = {n: given[n] for n in TWIN_WEIGHTS}
    shared = {n: given[n] for n in SHARED_INPUTS}
    per_example = {n: given[n] for n in ['x', 'positions']}
    grad_fn = _jax.value_and_grad(_loss, argnums=(0, 1))

    def one_microbatch(ex, loss_target):
        ex = dict(ex)
        diff = ex.pop(TWIN_DIFF_INPUT)
        return grad_fn(weights, diff, {**shared, **ex}, loss_target)

    if N_MICROBATCH == 1:
        loss, (grad_w, grad_x) = one_microbatch(per_example, given["loss_target"])
    else:
        def body(carry, xs):
            loss_sum, grad_sum = carry
            l_k, (gw_k, gx_k) = one_microbatch(xs[0], xs[1])
            with _jax.named_scope("update"):
                return (loss_sum + l_k, _jax.tree.map(_jnp.add, grad_sum, gw_k)), gx_k

        init = (_jnp.zeros((), _jnp.float32), _jax.tree.map(_jnp.zeros_like, weights))
        (loss, grad_w), grad_x = _jax.lax.scan(body, init, (per_example, given["loss_target"]))
    with _jax.named_scope("update"):
        delta_w, new_m, new_v = {}, {}, {}
        for n in TWIN_WEIGHTS:
            delta_w[n], new_m[n], new_v[n] = _adamw(weights[n], grad_w[n], given["m_" + n], given["v_" + n])
    return (loss, grad_x, *[grad_w[n] for n in TWIN_WEIGHTS], *[delta_w[n] for n in TWIN_WEIGHTS],
            *[new_m[n] for n in TWIN_WEIGHTS], *[new_v[n] for n in TWIN_WEIGHTS])
```

```python
import functools

import numpy as np
import jax
import jax.numpy as jnp
from jax import lax
from jax.experimental import pallas as pl
from jax.experimental.pallas import tpu as pltpu

f32, bf16 = jnp.float32, jnp.bfloat16

EPS = 1e-6
V_HEAD = 128
NOPE = 128
ROPE = 64
QK_PAD = 256
Q_RANK = 512
KV_RANK = 512
ROPE_THETA = 10000.0
SSD_P = 64
SSD_G = 2
SSD_N = 128
SSD_K = 4
CHUNK = 128
ADAM_LR, ADAM_B1, ADAM_B2, ADAM_EPS, ADAM_WD, ADAM_STEP = 0.001, 0.9, 0.999, 1e-08, 0.01, 10

VMEM_LIMIT_BYTES = 48 * 1024 * 1024
LANES = 128
ATT_TILE = 512
MM_TM, MM_TN, MM_TK = 1024, 1024, 512
ROW_TILE = 256

NN = (((1,), (0,)), ((), ()))
NT = (((1,), (1,)), ((), ()))
TN = (((0,), (0,)), ((), ()))
MESH = pl.DeviceIdType.MESH
ANY = pl.BlockSpec(memory_space=pl.ANY)


def _tile(dim, cap, align=LANES):
    if dim <= cap:
        return dim
    t = (cap // align) * align
    while t >= align:
        if dim % t == 0:
            return t
        t -= align
    raise ValueError(f"no tile for {dim} under {cap}")


def _round_up(n, m):
    return -(-n // m) * m


def _params(sem):
    return pltpu.CompilerParams(dimension_semantics=sem, vmem_limit_bytes=VMEM_LIMIT_BYTES)


def _dot(a, b, dims):
    return lax.dot_general(a.astype(bf16), b.astype(bf16), dims, preferred_element_type=f32)


def _matmul(pairs, mode, out_dtype, name):
    a0, b0 = pairs[0]
    if mode == "nn":
        (m, k), n = a0.shape, b0.shape[1]
    elif mode == "nt":
        (m, k), n = a0.shape, b0.shape[0]
    else:
        (k, m), n = a0.shape, b0.shape[1]
    tm, tn, tk = _tile(m, MM_TM), _tile(n, MM_TN), _tile(k, MM_TK)
    nk = k // tk
    if mode == "nn":
        a_spec = pl.BlockSpec((tm, tk), lambda i, j, kk: (i, kk))
        b_spec = pl.BlockSpec((tk, tn), lambda i, j, kk: (kk, j))
        dims = NN
    elif mode == "nt":
        a_spec = pl.BlockSpec((tm, tk), lambda i, j, kk: (i, kk))
        b_spec = pl.BlockSpec((tn, tk), lambda i, j, kk: (j, kk))
        dims = NT
    else:
        a_spec = pl.BlockSpec((tk, tm), lambda i, j, kk: (kk, i))
        b_spec = pl.BlockSpec((tk, tn), lambda i, j, kk: (kk, j))
        dims = TN
    npair = len(pairs)

    def body(*refs):
        o_ref, acc = refs[2 * npair], refs[2 * npair + 1]
        kk = pl.program_id(2)

        @pl.when(kk == 0)
        def _():
            acc[...] = jnp.zeros_like(acc)

        part = _dot(refs[0][...], refs[1][...], dims)
        for p in range(1, npair):
            part = part + _dot(refs[2 * p][...], refs[2 * p + 1][...], dims)
        acc[...] += part

        @pl.when(kk == nk - 1)
        def _():
            o_ref[...] = acc[...].astype(out_dtype)

    args = [t for pr in pairs for t in pr]
    return pl.pallas_call(
        body, name=name, out_shape=jax.ShapeDtypeStruct((m, n), out_dtype),
        grid=(m // tm, n // tn, nk), in_specs=[a_spec, b_spec] * npair,
        out_specs=pl.BlockSpec((tm, tn), lambda i, j, kk: (i, j)),
        scratch_shapes=[pltpu.VMEM((tm, tn), f32)],
        compiler_params=_params(("parallel", "parallel", "arbitrary")),
    )(*args)


def _sigmoid(x):
    return 1.0 / (1.0 + jnp.exp(-x))


def _ffn_up(v, wg, wu):
    m, k = v.shape
    n = wg.shape[1]
    tm, tn, tk = _tile(m, MM_TM), _tile(n, 512), _tile(k, MM_TK)
    nk = k // tk

    def body(v_ref, wg_ref, wu_ref, g_ref, u_ref, act_ref, accg, accu):
        kk = pl.program_id(2)

        @pl.when(kk == 0)
        def _():
            accg[...] = jnp.zeros_like(accg)
            accu[...] = jnp.zeros_like(accu)

        vb = v_ref[...]
        accg[...] += _dot(vb, wg_ref[...], NN)
        accu[...] += _dot(vb, wu_ref[...], NN)

        @pl.when(kk == nk - 1)
        def _():
            g, u = accg[...], accu[...]
            g_ref[...] = g.astype(bf16)
            u_ref[...] = u.astype(bf16)
            act_ref[...] = (g * _sigmoid(g) * u).astype(bf16)

    out = jax.ShapeDtypeStruct((m, n), bf16)
    o_spec = pl.BlockSpec((tm, tn), lambda i, j, kk: (i, j))
    return pl.pallas_call(
        body, name="ffn_up", out_shape=(out, out, out), grid=(m // tm, n // tn, nk),
        in_specs=[pl.BlockSpec((tm, tk), lambda i, j, kk: (i, kk)),
                  pl.BlockSpec((tk, tn), lambda i, j, kk: (kk, j)),
                  pl.BlockSpec((tk, tn), lambda i, j, kk: (kk, j))],
        out_specs=(o_spec, o_spec, o_spec),
        scratch_shapes=[pltpu.VMEM((tm, tn), f32), pltpu.VMEM((tm, tn), f32)],
        compiler_params=_params(("parallel", "parallel", "arbitrary")),
    )(v, wg, wu)


def _ffn_down_bwd(dffn, wd, g, u):
    m, k = dffn.shape
    n = wd.shape[0]
    tm, tn, tk = _tile(m, MM_TM), _tile(n, 512), _tile(k, MM_TK)
    nk = k // tk

    def body(d_ref, w_ref, g_ref, u_ref, dg_ref, du_ref, acc):
        kk = pl.program_id(2)

        @pl.when(kk == 0)
        def _():
            acc[...] = jnp.zeros_like(acc)

        acc[...] += _dot(d_ref[...], w_ref[...], NT)

        @pl.when(kk == nk - 1)
        def _():
            dact = acc[...]
            gg, uu = g_ref[...].astype(f32), u_ref[...].astype(f32)
            sg = _sigmoid(gg)
            du_ref[...] = (dact * gg * sg).astype(bf16)
            dg_ref[...] = (dact * uu * (sg * (1.0 + gg * (1.0 - sg)))).astype(bf16)

    out = jax.ShapeDtypeStruct((m, n), bf16)
    o_spec = pl.BlockSpec((tm, tn), lambda i, j, kk: (i, j))
    return pl.pallas_call(
        body, name="ffn_down_bwd", out_shape=(out, out), grid=(m // tm, n // tn, nk),
        in_specs=[pl.BlockSpec((tm, tk), lambda i, j, kk: (i, kk)),
                  pl.BlockSpec((tn, tk), lambda i, j, kk: (j, kk)), o_spec, o_spec],
        out_specs=(o_spec, o_spec),
        scratch_shapes=[pltpu.VMEM((tm, tn), f32)],
        compiler_params=_params(("parallel", "parallel", "arbitrary")),
    )(dffn, wd, g, u)


def _blocked(fn, grid, ins, outs, accs, name, sp=None):
    n_in, n_out, n_acc = len(ins), len(outs), len(accs)
    nsp = 0 if sp is None else 1

    def body(*refs):
        refs = refs[nsp:]
        tiles = [r[...] for r in refs[:n_in]]
        res = fn(*tiles)
        if not isinstance(res, (tuple, list)):
            res = (res,)
        for r, val in zip(refs[n_in:n_in + n_out], res[:n_out]):
            r[...] = val.astype(r.dtype)
        if n_acc:
            first = pl.program_id(0) == 0
            for d in range(1, len(grid)):
                first = jnp.logical_and(first, pl.program_id(d) == 0)

            @pl.when(first)
            def _():
                for r in refs[n_in + n_out:]:
                    r[...] = jnp.zeros_like(r)

            for r, val in zip(refs[n_in + n_out:], res[n_out:]):
                r[...] += val

    def acc_map(shape):
        zeros = (0,) * len(shape)
        return lambda *idx: zeros

    in_specs = [pl.BlockSpec(bs, im) for _, bs, im in ins]
    out_specs = [pl.BlockSpec(bs, im) for _, _, bs, im in outs] + [pl.BlockSpec(s, acc_map(s)) for s in accs]
    out_shape = [jax.ShapeDtypeStruct(s, d) for s, d, _, _ in outs] + [jax.ShapeDtypeStruct(s, f32) for s in accs]
    sem = ("arbitrary",) * len(grid) if n_acc else ("parallel",) * len(grid)
    args = [a for a, _, _ in ins]
    if sp is None:
        res = pl.pallas_call(body, name=name, out_shape=out_shape, grid=grid, in_specs=in_specs,
                             out_specs=out_specs, compiler_params=_params(sem))(*args)
    else:
        gs = pltpu.PrefetchScalarGridSpec(num_scalar_prefetch=1, grid=grid, in_specs=in_specs, out_specs=out_specs)
        res = pl.pallas_call(body, name=name, out_shape=out_shape, grid_spec=gs,
                             compiler_params=_params(sem))(sp, *args)
    return res


def _rows(a, tm, cols=None, cb=0):
    w = a.shape[1] if cols is None else cols
    return (a, (tm, w), lambda i: (i, cb))


def _par(a):
    zeros = (0,) * a.ndim
    return (a, a.shape, lambda i: zeros)


def _rout(t, w, dtype, tm):
    return ((t, w), dtype, (tm, w), lambda i: (i, 0))


def _rowwise(fn, t, tm, ins, outs, accs, name):
    return _blocked(fn, (t // tm,), ins, outs, accs, name)


def _rms(x, w):
    r = lax.rsqrt(jnp.mean(x * x, axis=-1, keepdims=True) + EPS)
    return x * r * w


def _rms_bwd(x, w, dy):
    r = lax.rsqrt(jnp.mean(x * x, axis=-1, keepdims=True) + EPS)
    xh = x * r
    dyw = dy * w
    dx = r * (dyw - xh * jnp.mean(dyw * xh, axis=-1, keepdims=True))
    return dx, jnp.sum(dy * xh, axis=0, keepdims=True)


def _silu_grad(x):
    s = _sigmoid(x)
    return s * (1.0 + x * (1.0 - s))


def _rope(blk, cosp, sina, sinb):
    return blk * cosp + pltpu.roll(blk, 96, 1) * sina + pltpu.roll(blk, 32, 1) * sinb


def _rope_bwd(dy, cosp, sina, sinb):
    return dy * cosp + pltpu.roll(dy * sina, 32, 1) + pltpu.roll(dy * sinb, 96, 1)


def _causal_mask(s, diag):
    row = lax.broadcasted_iota(jnp.int32, s.shape, 0)
    col = lax.broadcasted_iota(jnp.int32, s.shape, 1)
    return jnp.where(jnp.logical_or(jnp.logical_not(diag), row >= col), s, -jnp.inf)


def _flash_fwd(q, kv, kr, nheads, scale):
    t = q.shape[0]
    tq = _tile(t, ATT_TILE)
    nq = t // tq

    def body(q_ref, kn_ref, kr_ref, v_ref, o_ref, lse_ref, m_sc, l_sc, acc_sc):
        qi, ki = pl.program_id(1), pl.program_id(2)

        @pl.when(ki == 0)
        def _():
            m_sc[...] = jnp.full_like(m_sc, -jnp.inf)
            l_sc[...] = jnp.zeros_like(l_sc)
            acc_sc[...] = jnp.zeros_like(acc_sc)

        @pl.when(ki <= qi)
        def _():
            k = jnp.concatenate([kn_ref[...], kr_ref[...]], axis=1)
            s = lax.dot_general(q_ref[...], k, NT, preferred_element_type=f32) * scale
            s = _causal_mask(s, ki == qi)
            m_old = m_sc[...]
            m_new = jnp.maximum(m_old, jnp.max(s, axis=1, keepdims=True))
            alpha = jnp.exp(m_old - m_new)
            p = jnp.exp(s - m_new)
            l_sc[...] = alpha * l_sc[...] + jnp.sum(p, axis=1, keepdims=True)
            acc_sc[...] = alpha * acc_sc[...] + lax.dot_general(p.astype(bf16), v_ref[...], NN,
                                                                preferred_element_type=f32)
            m_sc[...] = m_new

        @pl.when(ki == qi)
        def _():
            l = l_sc[...]
            o_ref[...] = acc_sc[...] / l
            lse_ref[...] = jnp.broadcast_to(m_sc[...] + jnp.log(l), lse_ref.shape)

    kc = lambda qi, ki: jnp.minimum(ki, qi)
    o_spec = pl.BlockSpec((tq, V_HEAD), lambda h, qi, ki: (qi, h))
    out = jax.ShapeDtypeStruct((t, nheads * V_HEAD), f32)
    return pl.pallas_call(
        body, name="flash_fwd", out_shape=(out, out), grid=(nheads, nq, nq),
        in_specs=[pl.BlockSpec((tq, QK_PAD), lambda h, qi, ki: (qi, h)),
                  pl.BlockSpec((tq, NOPE), lambda h, qi, ki: (kc(qi, ki), 2 * h)),
                  pl.BlockSpec((tq, LANES), lambda h, qi, ki: (kc(qi, ki), 0)),
                  pl.BlockSpec((tq, V_HEAD), lambda h, qi, ki: (kc(qi, ki), 2 * h + 1))],
        out_specs=(o_spec, o_spec),
        scratch_shapes=[pltpu.VMEM((tq, 1), f32), pltpu.VMEM((tq, 1), f32), pltpu.VMEM((tq, V_HEAD), f32)],
        compiler_params=_params(("parallel", "parallel", "arbitrary")),
    )(q, kv, kr, kv)


def _flash_bwd_dkv(q, kv, kr, do, lse, delta, nheads, scale):
    t = q.shape[0]
    tq = _tile(t, ATT_TILE)
    nq = t // tq

    def body(q_ref, kn_ref, kr_ref, v_ref, do_ref, lse_ref, dl_ref, dkv_ref, dkr_ref, dk_sc, dv_sc):
        ki, qi = pl.program_id(1), pl.program_id(2)

        @pl.when(qi == 0)
        def _():
            dk_sc[...] = jnp.zeros_like(dk_sc)
            dv_sc[...] = jnp.zeros_like(dv_sc)

        @pl.when(qi >= ki)
        def _():
            qb = q_ref[...]
            dob = do_ref[...]
            k = jnp.concatenate([kn_ref[...], kr_ref[...]], axis=1)
            s = lax.dot_general(qb, k, NT, preferred_element_type=f32) * scale
            s = _causal_mask(s, ki == qi)
            p = jnp.exp(s - lse_ref[:, 0:1])
            dv_sc[...] += lax.dot_general(p.astype(bf16), dob, TN, preferred_element_type=f32)
            dp = lax.dot_general(dob, v_ref[...], NT, preferred_element_type=f32)
            ds = p * (dp - dl_ref[:, 0:1]) * scale
            dk_sc[...] += lax.dot_general(ds.astype(bf16), qb, TN, preferred_element_type=f32)

        @pl.when(qi == nq - 1)
        def _():
            dk = dk_sc[...]
            dkv_ref[...] = jnp.concatenate([dk[:, :NOPE], dv_sc[...]], axis=1).astype(bf16)
            dkr_ref[...] = dk[:, NOPE:]

    qc = lambda ki, qi: jnp.maximum(ki, qi)
    hspec = pl.BlockSpec((tq, V_HEAD), lambda h, ki, qi: (qc(ki, qi), h))
    return pl.pallas_call(
        body, name="flash_bwd_dkv",
        out_shape=(jax.ShapeDtypeStruct((t, nheads * QK_PAD), bf16), jax.ShapeDtypeStruct((t, nheads * LANES), f32)),
        grid=(nheads, nq, nq),
        in_specs=[pl.BlockSpec((tq, QK_PAD), lambda h, ki, qi: (qc(ki, qi), h)),
                  pl.BlockSpec((tq, NOPE), lambda h, ki, qi: (ki, 2 * h)),
                  pl.BlockSpec((tq, LANES), lambda h, ki, qi: (ki, 0)),
                  pl.BlockSpec((tq, V_HEAD), lambda h, ki, qi: (ki, 2 * h + 1)),
                  hspec, hspec, hspec],
        out_specs=(pl.BlockSpec((tq, QK_PAD), lambda h, ki, qi: (ki, h)),
                   pl.BlockSpec((tq, LANES), lambda h, ki, qi: (ki, h))),
        scratch_shapes=[pltpu.VMEM((tq, QK_PAD), f32), pltpu.VMEM((tq, V_HEAD), f32)],
        compiler_params=_params(("parallel", "parallel", "arbitrary")),
    )(q, kv, kr, kv, do, lse, delta)


def _flash_bwd_dq(q, kv, kr, do, lse, delta, nheads, scale):
    t = q.shape[0]
    tq = _tile(t, ATT_TILE)
    nq = t // tq

    def body(q_ref, kn_ref, kr_ref, v_ref, do_ref, lse_ref, dl_ref, dq_ref, dq_sc):
        qi, ki = pl.program_id(1), pl.program_id(2)

        @pl.when(ki == 0)
        def _():
            dq_sc[...] = jnp.zeros_like(dq_sc)

        @pl.when(ki <= qi)
        def _():
            dob = do_ref[...]
            k = jnp.concatenate([kn_ref[...], kr_ref[...]], axis=1)
            s = lax.dot_general(q_ref[...], k, NT, preferred_element_type=f32) * scale
            s = _causal_mask(s, ki == qi)
            p = jnp.exp(s - lse_ref[:, 0:1])
            dp = lax.dot_general(dob, v_ref[...], NT, preferred_element_type=f32)
            ds = p * (dp - dl_ref[:, 0:1]) * scale
            dq_sc[...] += lax.dot_general(ds.astype(bf16), k, NN, preferred_element_type=f32)

        @pl.when(ki == qi)
        def _():
            dq_ref[...] = dq_sc[...]

    kc = lambda qi, ki: jnp.minimum(ki, qi)
    hspec = pl.BlockSpec((tq, V_HEAD), lambda h, qi, ki: (qi, h))
    return pl.pallas_call(
        body, name="flash_bwd_dq", out_shape=jax.ShapeDtypeStruct((t, nheads * QK_PAD), f32),
        grid=(nheads, nq, nq),
        in_specs=[pl.BlockSpec((tq, QK_PAD), lambda h, qi, ki: (qi, h)),
                  pl.BlockSpec((tq, NOPE), lambda h, qi, ki: (kc(qi, ki), 2 * h)),
                  pl.BlockSpec((tq, LANES), lambda h, qi, ki: (kc(qi, ki), 0)),
                  pl.BlockSpec((tq, V_HEAD), lambda h, qi, ki: (kc(qi, ki), 2 * h + 1)),
                  hspec, hspec, hspec],
        out_specs=pl.BlockSpec((tq, QK_PAD), lambda h, qi, ki: (qi, h)),
        scratch_shapes=[pltpu.VMEM((tq, QK_PAD), f32)],
        compiler_params=_params(("parallel", "parallel", "arbitrary")),
    )(q, kv, kr, kv, do, lse, delta)


def _split3(a):
    hi = a.astype(bf16)
    r1 = a - hi.astype(f32)
    mid = r1.astype(bf16)
    lo = (r1 - mid.astype(f32)).astype(bf16)
    return hi, mid, lo


def _ones_dot_left(tri, a):
    hi, mid, lo = _split3(a)
    d = lambda v: lax.dot_general(tri, v, NN, preferred_element_type=f32)
    return d(hi) + d(mid) + d(lo)


def _ones_dot_right(a, tri):
    hi, mid, lo = _split3(a)
    d = lambda v: lax.dot_general(v, tri, NN, preferred_element_type=f32)
    return d(hi) + d(mid) + d(lo)


def _softplus(x):
    return jnp.maximum(x, 0.0) + jnp.log(1.0 + jnp.exp(-jnp.abs(x)))


def _ssd_common(dt_ref, dtT_ref, dtb_ref, dtbT_ref, alog_ref, alogT_ref):
    ii = lax.broadcasted_iota(jnp.int32, (CHUNK, CHUNK), 0)
    jj = lax.broadcasted_iota(jnp.int32, (CHUNK, CHUNK), 1)
    tri = ii >= jj
    raw = dt_ref[...] + dtb_ref[...]
    dt = _softplus(raw)
    a_neg = -jnp.exp(alog_ref[...])
    cum = _ones_dot_left(tri.astype(bf16), dt * a_neg)
    dt_t = _softplus(dtT_ref[...] + dtbT_ref[...])
    cum_t = _ones_dot_right(dt_t * (-jnp.exp(alogT_ref[...])), (ii <= jj).astype(bf16))
    return tri, raw, dt, a_neg, cum, cum_t


def _ssd_fwd(xbc, dt, dt_t, dtb, dtb_t, alog, alog_t, dskip, width):
    t, cdim = xbc.shape
    hs = dt.shape[1]
    nc = t // CHUNK
    epg = hs // SSD_G
    gn = SSD_G * SSD_N

    def body(x_ref, dt_ref, dtT_ref, dtb_ref, dtbT_ref, alog_ref, alogT_ref, d_ref, y_ref, hp_ref, h_sc):
        @pl.when(pl.program_id(0) == 0)
        def _():
            h_sc[...] = jnp.zeros_like(h_sc)

        tri, _, dtv, _, cum, cum_t = _ssd_common(dt_ref, dtT_ref, dtb_ref, dtbT_ref, alog_ref, alogT_ref)
        dsk = d_ref[...]
        for g in range(SSD_G):
            bb = x_ref[:, width + g * SSD_N: width + (g + 1) * SSD_N].astype(bf16)
            cb_ = x_ref[:, width + gn + g * SSD_N: width + gn + (g + 1) * SSD_N].astype(bf16)
            cbm = lax.dot_general(cb_, bb, NT, preferred_element_type=f32)
            for e in range(g * epg, (g + 1) * epg):
                xe = x_ref[:, e * SSD_P:(e + 1) * SSD_P]
                dtc = dtv[:, e:e + 1]
                cc = cum[:, e:e + 1]
                clast = cum[CHUNK - 1:CHUNK, e:e + 1]
                seg = cc - cum_t[e:e + 1, :]
                lmat = jnp.exp(jnp.where(tri, seg, -jnp.inf))
                xdt = xe * dtc
                h = h_sc[e]
                hp_ref[0, e] = h
                y = _dot(cbm * lmat, xdt, NN)
                y = y + _dot(cb_, h, NT) * jnp.exp(cc)
                y_ref[:, e * SSD_P:(e + 1) * SSD_P] = y + xe * dsk[:, e:e + 1]
                st = _dot(xdt * jnp.exp(clast - cc), bb, TN)
                h_sc[e] = h * jnp.exp(clast) + st

    par = lambda a: pl.BlockSpec(a.shape, lambda i: (0,) * a.ndim)
    return pl.pallas_call(
        body, name="ssd_fwd",
        out_shape=(jax.ShapeDtypeStruct((t, width), f32), jax.ShapeDtypeStruct((nc, hs, SSD_P, SSD_N), f32)),
        grid=(nc,),
        in_specs=[pl.BlockSpec((CHUNK, cdim), lambda i: (i, 0)), pl.BlockSpec((CHUNK, hs), lambda i: (i, 0)),
                  pl.BlockSpec((hs, CHUNK), lambda i: (0, i)), par(dtb), par(dtb_t), par(alog), par(alog_t), par(dskip)],
        out_specs=(pl.BlockSpec((CHUNK, width), lambda i: (i, 0)),
                   pl.BlockSpec((1, hs, SSD_P, SSD_N), lambda i: (i, 0, 0, 0))),
        scratch_shapes=[pltpu.VMEM((hs, SSD_P, SSD_N), f32)],
        compiler_params=_params(("arbitrary",)),
    )(xbc, dt, dt_t, dtb, dtb_t, alog, alog_t, dskip)


def _ssd_bwd(xbc, dt, dt_t, dtb, dtb_t, alog, alog_t, dskip, hprev, dy, width):
    t, cdim = xbc.shape
    hs = dt.shape[1]
    nc = t // CHUNK
    epg = hs // SSD_G
    gn = SSD_G * SSD_N

    def body(x_ref, dt_ref, dtT_ref, dtb_ref, dtbT_ref, alog_ref, alogT_ref, d_ref, hp_ref, dy_ref,
             dx_ref, ddt_ref, dalog_ref, ddsk_ref, ddtb_ref, dh_sc):
        @pl.when(pl.program_id(0) == 0)
        def _():
            dh_sc[...] = jnp.zeros_like(dh_sc)
            dalog_ref[...] = jnp.zeros_like(dalog_ref)
            ddsk_ref[...] = jnp.zeros_like(ddsk_ref)
            ddtb_ref[...] = jnp.zeros_like(ddtb_ref)

        tri, raw, dtv, a_neg, cum, cum_t = _ssd_common(dt_ref, dtT_ref, dtb_ref, dtbT_ref, alog_ref, alogT_ref)
        dsk = d_ref[...]
        head_iota = lax.broadcasted_iota(jnp.int32, (1, hs), 1)
        last_row = (lax.broadcasted_iota(jnp.int32, (CHUNK, 1), 0) == CHUNK - 1).astype(f32)
        dcum_all = jnp.zeros((CHUNK, hs), f32)
        ddtx_all = jnp.zeros((CHUNK, hs), f32)
        ddsk = jnp.zeros((1, hs), f32)
        for g in range(SSD_G):
            bsl = slice(width + g * SSD_N, width + (g + 1) * SSD_N)
            csl = slice(width + gn + g * SSD_N, width + gn + (g + 1) * SSD_N)
            bb = x_ref[:, bsl].astype(bf16)
            cb_ = x_ref[:, csl].astype(bf16)
            cbm = lax.dot_general(cb_, bb, NT, preferred_element_type=f32)
            dbg = jnp.zeros((CHUNK, SSD_N), f32)
            dcg = jnp.zeros((CHUNK, SSD_N), f32)
            for e in range(g * epg, (g + 1) * epg):
                onehot = (head_iota == e).astype(f32)
                esl = slice(e * SSD_P, (e + 1) * SSD_P)
                xe = x_ref[:, esl]
                dye = dy_ref[:, esl]
                dtc = dtv[:, e:e + 1]
                cc = cum[:, e:e + 1]
                clast = cum[CHUNK - 1:CHUNK, e:e + 1]
                lmat = jnp.exp(jnp.where(tri, cc - cum_t[e:e + 1, :], -jnp.inf))
                gmat = cbm * lmat
                xdt = xe * dtc
                ee = jnp.exp(cc)
                ff = jnp.exp(clast - cc)
                dec = jnp.exp(clast)
                h = hp_ref[0, e]
                dh = dh_sc[e]
                ch = _dot(cb_, h, NT)
                dye_e = dye * ee
                dcum = jnp.sum(dye * ch, axis=1, keepdims=True) * ee
                dcg = dcg + _dot(dye_e, h, NN)
                dh_new = _dot(dye_e, cb_, TN)
                dbg = dbg + _dot(xdt * ff, dh, NN)
                bds = _dot(bb, dh, NT)
                dxdt = bds * ff
                dff = jnp.sum(bds * xdt, axis=1, keepdims=True) * ff
                dcum = dcum - dff
                dclast = jnp.sum(dff, axis=0, keepdims=True) + dec * jnp.sum(
                    jnp.sum(dh * h, axis=1, keepdims=True), axis=0, keepdims=True)
                dgm = _dot(dye, xdt, NT)
                dxdt = dxdt + _dot(gmat, dye, TN)
                dcb = dgm * lmat
                dcg = dcg + _dot(dcb, bb, NN)
                dbg = dbg + _dot(dcb, cb_, TN)
                mm = dgm * gmat
                dcum = dcum + jnp.sum(mm, axis=1, keepdims=True) - jnp.sum(mm.T, axis=1, keepdims=True)
                dcum = dcum + dclast * last_row
                dx_ref[:, esl] = dxdt * dtc + dye * dsk[:, e:e + 1]
                ddtx_all = ddtx_all + jnp.sum(dxdt * xe, axis=1, keepdims=True) * onehot
                dcum_all = dcum_all + dcum * onehot
                ddsk = ddsk + jnp.sum(jnp.sum(dye * xe, axis=1, keepdims=True), axis=0, keepdims=True) * onehot
                dh_sc[e] = dh_new + dec * dh
            dx_ref[:, bsl] = dbg
            dx_ref[:, csl] = dcg
        ii = lax.broadcasted_iota(jnp.int32, (CHUNK, CHUNK), 0)
        jj = lax.broadcasted_iota(jnp.int32, (CHUNK, CHUNK), 1)
        da = _ones_dot_left((jj >= ii).astype(bf16), dcum_all)
        ddt = da * a_neg + ddtx_all
        dalog_ref[...] += jnp.sum(da * dtv, axis=0, keepdims=True) * a_neg
        draw = ddt * _sigmoid(raw)
        ddt_ref[...] = draw
        ddtb_ref[...] += jnp.sum(draw, axis=0, keepdims=True)
        ddsk_ref[...] += ddsk

    rev = lambda i: nc - 1 - i
    par = lambda a: pl.BlockSpec(a.shape, lambda i: (0,) * a.ndim)
    acc = pl.BlockSpec((1, hs), lambda i: (0, 0))
    acc_shape = jax.ShapeDtypeStruct((1, hs), f32)
    return pl.pallas_call(
        body, name="ssd_bwd",
        out_shape=(jax.ShapeDtypeStruct((t, cdim), f32), jax.ShapeDtypeStruct((t, hs), f32),
                   acc_shape, acc_shape, acc_shape),
        grid=(nc,),
        in_specs=[pl.BlockSpec((CHUNK, cdim), lambda i: (rev(i), 0)), pl.BlockSpec((CHUNK, hs), lambda i: (rev(i), 0)),
                  pl.BlockSpec((hs, CHUNK), lambda i: (0, rev(i))), par(dtb), par(dtb_t), par(alog), par(alog_t),
                  par(dskip), pl.BlockSpec((1, hs, SSD_P, SSD_N), lambda i: (rev(i), 0, 0, 0)),
                  pl.BlockSpec((CHUNK, width), lambda i: (rev(i), 0))],
        out_specs=(pl.BlockSpec((CHUNK, cdim), lambda i: (rev(i), 0)), pl.BlockSpec((CHUNK, hs), lambda i: (rev(i), 0)),
                   acc, acc, acc),
        scratch_shapes=[pltpu.VMEM((hs, SSD_P, SSD_N), f32)],
        compiler_params=_params(("arbitrary",)),
    )(xbc, dt, dt_t, dtb, dtb_t, alog, alog_t, dskip, hprev, dy)


def _where_am_i():
    x, y, c = lax.axis_index("x"), lax.axis_index("y"), lax.axis_index("c")
    chips = [(1 - x, y), (x, 1 - y), (1 - x, 1 - y)]
    return x, y, c, chips


def _remote(src, dst, send_sems, recv_sems, k, to):
    return pltpu.make_async_remote_copy(src_ref=src, dst_ref=dst, send_sem=send_sems.at[k], recv_sem=recv_sems.at[k],
                                        device_id=to, device_id_type=MESH)


def _comm_call(body, out_shape, nsem, name, *args):
    return pl.pallas_call(
        body, name=name, out_shape=out_shape, in_specs=[ANY] * len(args),
        out_specs=ANY if not isinstance(out_shape, (tuple, list)) else [ANY] * len(out_shape),
        scratch_shapes=[pltpu.SemaphoreType.DMA((nsem,)), pltpu.SemaphoreType.DMA((nsem,)), pltpu.SemaphoreType.DMA],
        compiler_params=pltpu.CompilerParams(has_side_effects=True),
    )(*args)


def _gather_shards(src, name):
    def body(src_ref, out_ref, send_sems, recv_sems, local_sem):
        x, y, c, chips = _where_am_i()
        sib = (x, y, 1 - c)
        slot = lambda px, py, pc: out_ref.at[2 * px + py, pc]
        mine = pltpu.make_async_copy(src_ref, out_ref.at[2 * x + y], local_sem)
        mine.start()
        first = [_remote(src_ref.at[c], slot(x, y, c), send_sems, recv_sems, j, (*chip, c))
                 for j, chip in enumerate(chips)]
        for cp in first:
            cp.start()
        passed = [_remote(slot(*chip, c), slot(*chip, c), send_sems, recv_sems, 3 + j, sib)
                  for j, chip in enumerate(chips)]
        for j, chip in enumerate(chips):
            _remote(slot(*chip, c), slot(*chip, c), send_sems, recv_sems, j, (*chip, c)).wait_recv()
            passed[j].start()
        for j, chip in enumerate(chips):
            _remote(slot(*chip, 1 - c), slot(*chip, 1 - c), send_sems, recv_sems, 3 + j, sib).wait_recv()
        for cp in first + passed:
            cp.wait_send()
        mine.wait()

    return _comm_call(body, jax.ShapeDtypeStruct((4,) + src.shape, src.dtype), 6, name, src)


def _sibling_send_halves(a):
    def body(a_ref, out_ref, send_sems, recv_sems, local_sem):
        x, y, c, _ = _where_am_i()
        sib = (x, y, 1 - c)
        cps = [_remote(a_ref.at[k, 1 - c], out_ref.at[k], send_sems, recv_sems, k, sib) for k in range(4)]
        for cp in cps:
            cp.start()
        for cp in cps:
            cp.wait()

    return _comm_call(body, jax.ShapeDtypeStruct((4,) + a.shape[2:], a.dtype), 4, "sibling_send_halves", a)


def _chips_exchange(p):
    def body(p_ref, out_ref, send_sems, recv_sems, local_sem):
        x, y, c, chips = _where_am_i()
        me = 2 * x + y
        cps = [_remote(p_ref.at[2 * cx + cy], out_ref.at[me], send_sems, recv_sems, j, (cx, cy, c))
               for j, (cx, cy) in enumerate(chips)]
        for cp in cps:
            cp.start()
        for j, (cx, cy) in enumerate(chips):
            _remote(p_ref.at[me], out_ref.at[2 * cx + cy], send_sems, recv_sems, j, (cx, cy, c)).wait_recv()
        for cp in cps:
            cp.wait_send()

    return _comm_call(body, jax.ShapeDtypeStruct(p.shape, p.dtype), 3, "chips_exchange", p)


def _sibling_share(r):
    def body(r_ref, out_ref, send_sems, recv_sems, local_sem):
        x, y, c, _ = _where_am_i()
        sib = (x, y, 1 - c)
        mine = pltpu.make_async_copy(r_ref, out_ref.at[c], local_sem)
        mine.start()
        cp = _remote(r_ref, out_ref.at[c], send_sems, recv_sems, 0, sib)
        cp.start()
        _remote(r_ref, out_ref.at[1 - c], send_sems, recv_sems, 0, sib).wait_recv()
        cp.wait_send()
        mine.wait()

    return _comm_call(body, jax.ShapeDtypeStruct((2,) + r.shape, r.dtype), 1, "sibling_share", r)


def _gather_all(v):
    rows = v.shape[0]

    def body(x_ref, out_ref, send_sems, recv_sems, local_sem):
        x, y, c, chips = _where_am_i()
        me, sib = (x, y, c), (x, y, 1 - c)
        blk = lambda px, py, pc: out_ref.at[4 * px + 2 * py + pc]
        mine = pltpu.make_async_copy(x_ref, blk(*me), local_sem)
        mine.start()
        first = [_remote(x_ref, blk(*me), send_sems, recv_sems, 0, sib)]
        first += [_remote(x_ref, blk(*me), send_sems, recv_sems, 1 + j, (*chip, c)) for j, chip in enumerate(chips)]
        for cp in first:
            cp.start()
        passed = [_remote(blk(*chip, c), blk(*chip, c), send_sems, recv_sems, 4 + j, sib) for j, chip in enumerate(chips)]
        for j, chip in enumerate(chips):
            _remote(blk(*chip, c), blk(*chip, c), send_sems, recv_sems, 1 + j, me).wait_recv()
            passed[j].start()
        _remote(blk(*sib), blk(*sib), send_sems, recv_sems, 0, me).wait_recv()
        for j, chip in enumerate(chips):
            _remote(blk(*chip, 1 - c), blk(*chip, 1 - c), send_sems, recv_sems, 4 + j, me).wait_recv()
        for cp in first + passed:
            cp.wait_send()
        mine.wait()

    vm = pl.BlockSpec(memory_space=pltpu.VMEM)
    return pl.pallas_call(
        body, name="gather_all", out_shape=jax.ShapeDtypeStruct((8, rows, LANES), v.dtype),
        in_specs=[vm], out_specs=vm,
        scratch_shapes=[pltpu.SemaphoreType.DMA((7,)), pltpu.SemaphoreType.DMA((7,)), pltpu.SemaphoreType.DMA],
    )(v)


def _flat_pad(parts, total):
    v = jnp.concatenate([p.reshape(-1) for p in parts])
    return jnp.pad(v, (0, total - v.shape[0]))


def _adamw(w, g, m, v):
    m = ADAM_B1 * m + (1.0 - ADAM_B1) * g
    v = ADAM_B2 * v + (1.0 - ADAM_B2) * jnp.square(g)
    m_hat = m / (1.0 - ADAM_B1 ** ADAM_STEP)
    v_hat = v / (1.0 - ADAM_B2 ** ADAM_STEP)
    delta = -ADAM_LR * (m_hat / (jnp.sqrt(v_hat) + ADAM_EPS) + ADAM_WD * w)
    return delta, m, v


def _adamw_call(w, g, m, v, name):
    r, cdim = w.shape
    tm = _tile(r, ROW_TILE, 8)
    o = _rout(r, cdim, f32, tm)
    return _rowwise(_adamw, r, tm, [_rows(w, tm), _rows(g, tm), _rows(m, tm), _rows(v, tm)], [o, o, o], [], name)


BIG = ("w_in", "w_uq", "w_ukv", "w_out", "w_gate", "w_up", "w_down")
COL_SHARDED = ("w_in", "w_uq", "w_ukv", "w_gate", "w_up")
SMALL = ("q_norm_w", "kv_norm_w", "conv_w", "conv_b", "dt_bias", "a_log", "d_skip", "ssd_norm_w", "attn_out_norm_w",
         "pre_mix_norm_w", "post_mix_norm_w", "pre_ffn_norm_w", "post_ffn_norm_w")
ORDER = ("w_in", "q_norm_w", "w_uq", "kv_norm_w", "w_ukv", "conv_w", "conv_b", "dt_bias", "a_log", "d_skip",
         "ssd_norm_w", "attn_out_norm_w", "w_out", "pre_mix_norm_w", "post_mix_norm_w", "pre_ffn_norm_w",
         "post_ffn_norm_w", "w_gate", "w_up", "w_down")


def kernel(x, positions, w_in, q_norm_w, w_uq, kv_norm_w, w_ukv, conv_w, conv_b, dt_bias, a_log, d_skip, ssd_norm_w, attn_out_norm_w, w_out, pre_mix_norm_w, post_mix_norm_w, pre_ffn_norm_w, post_ffn_norm_w, w_gate, w_up, w_down, loss_target, m_w_in, m_q_norm_w, m_w_uq, m_kv_norm_w, m_w_ukv, m_conv_w, m_conv_b, m_dt_bias, m_a_log, m_d_skip, m_ssd_norm_w, m_attn_out_norm_w, m_w_out, m_pre_mix_norm_w, m_post_mix_norm_w, m_pre_ffn_norm_w, m_post_ffn_norm_w, m_w_gate, m_w_up, m_w_down, v_w_in, v_q_norm_w, v_w_uq, v_kv_norm_w, v_w_ukv, v_conv_w, v_conv_b, v_dt_bias, v_a_log, v_d_skip, v_ssd_norm_w, v_attn_out_norm_w, v_w_out, v_pre_mix_norm_w, v_post_mix_norm_w, v_pre_ffn_norm_w, v_post_ffn_norm_w, v_w_gate, v_w_up, v_w_down):
    local = dict(locals())
    wts = {n: local[n][0] for n in ORDER}
    mom_m = {n: local["m_" + n][0] for n in ORDER}
    mom_v = {n: local["v_" + n][0] for n in ORDER}
    xs = x[0]
    tgt = loss_target[0]
    t, d = xs.shape
    nchip = 4
    my_x, my_y, my_c = lax.axis_index("x"), lax.axis_index("y"), lax.axis_index("c")
    my_chip = 2 * my_x + my_y

    mla_w = d // 2
    nh = mla_w // V_HEAD
    width = d - mla_w
    hs = width // SSD_P
    gn = SSD_G * SSD_N
    cdim = width + 2 * gn
    in_sizes = (Q_RANK, KV_RANK, ROPE, width, cdim, hs)
    d_in = sum(in_sizes)
    tail = LANES
    d_in_p = _round_up(Q_RANK + KV_RANK + width + cdim + tail, 256)
    off_z = Q_RANK + KV_RANK
    off_xbc = off_z + width
    off_tail = off_xbc + cdim
    qk_head = NOPE + ROPE
    scale = qk_head ** -0.5
    tm = _tile(t, ROW_TILE, 8)
    tmw = _tile(t, ROW_TILE // 2, 8)

    shard_sizes = [int(np.prod(wts[n].shape)) for n in BIG]
    ls = sum(shard_sizes)
    cols = 1024
    rows_half = _round_up(-(-ls // (2 * cols)), ROW_TILE)
    ls_pad = 2 * rows_half * cols
    mine = _flat_pad([wts[n].astype(bf16) for n in BIG], ls_pad).reshape(2, rows_half, cols)
    gathered = _gather_shards(mine, "gather_weights").reshape(nchip, ls_pad)
    full = {}
    off = 0
    for n, sz in zip(BIG, shard_sizes):
        r, cs = wts[n].shape
        blk = gathered[:, off:off + sz].reshape(nchip, r, cs)
        off += sz
        full[n] = (jnp.transpose(blk, (1, 0, 2)).reshape(r, nchip * cs) if n in COL_SHARDED
                   else blk.reshape(nchip * r, cs))
    ck, ccs = wts["conv_w"].shape
    conv_rows = _round_up(-(-(ck * ccs) // (2 * LANES)), 8)
    conv_mine = _flat_pad([wts["conv_w"]], 2 * conv_rows * LANES).reshape(2, conv_rows, LANES)
    conv_g = _gather_shards(conv_mine, "gather_conv_w").reshape(nchip, -1)[:, :ck * ccs].reshape(nchip, ck, ccs)
    conv_full = jnp.transpose(conv_g, (1, 0, 2)).reshape(ck, nchip * ccs)

    wi = full["w_in"]
    o = np.cumsum((0,) + in_sizes)
    seg = lambda i: wi[:, o[i]:o[i + 1]]
    w_in_p = jnp.concatenate([seg(0), seg(1), seg(3), seg(4), seg(2), seg(5),
                              jnp.zeros((d, d_in_p - off_tail - ROPE - hs), bf16)], axis=1)
    w_uq_p = jnp.pad(full["w_uq"].reshape(Q_RANK, nh, qk_head), ((0, 0), (0, 0), (0, QK_PAD - qk_head))
                     ).reshape(Q_RANK, nh * QK_PAD)
    w_ukv_f, w_out_f = full["w_ukv"], full["w_out"]
    w_gate_f, w_up_f, w_down_f = full["w_gate"], full["w_up"], full["w_down"]

    inv_freq = ROPE_THETA ** (-jnp.arange(0, ROPE, 2, dtype=f32) / ROPE)
    ang = positions[0].astype(f32)[:, None] * inv_freq
    cos, sin = jnp.cos(ang), jnp.sin(ang)
    z32, z64, z96 = jnp.zeros((t, 32), f32), jnp.zeros((t, 64), f32), jnp.zeros((t, 96), f32)
    cosp = jnp.concatenate([cos, cos, z64], axis=1)
    sina = jnp.concatenate([-sin, z96], axis=1)
    sinb = jnp.concatenate([z32, sin, z64], axis=1)

    row = lambda a: a.reshape(1, -1)
    w_pre_mix, w_post_mix = row(wts["pre_mix_norm_w"]), row(wts["post_mix_norm_w"])
    w_pre_ffn, w_post_ffn = row(wts["pre_ffn_norm_w"]), row(wts["post_ffn_norm_w"])
    w_qn, w_kvn = row(wts["q_norm_w"]), row(wts["kv_norm_w"])
    w_attn_n, w_ssd_n = row(wts["attn_out_norm_w"]), row(wts["ssd_norm_w"])
    conv_b_r = row(wts["conv_b"])
    dtb, alog, dskip = row(wts["dt_bias"]), row(wts["a_log"]), row(wts["d_skip"])
    dtb_t, alog_t = dtb.reshape(hs, 1), alog.reshape(hs, 1)

    u = _rowwise(lambda a, w: _rms(a, w), t, tm, [_rows(xs, tm), _par(w_pre_mix)], [_rout(t, d, bf16, tm)], [],
                 "pre_mix_norm")[0]
    proj = _matmul([(u, w_in_p)], "nn", f32, "in_proj")
    cqn = _rowwise(lambda a, w: _rms(a, w), t, tm, [_rows(proj, tm, Q_RANK, 0), _par(w_qn)],
                   [_rout(t, Q_RANK, bf16, tm)], [], "q_norm")[0]
    ckvn = _rowwise(lambda a, w: _rms(a, w), t, tm, [_rows(proj, tm, KV_RANK, 1), _par(w_kvn)],
                    [_rout(t, KV_RANK, bf16, tm)], [], "kv_norm")[0]
    q_raw = _matmul([(cqn, w_uq_p)], "nn", f32, "q_up")
    kv = _matmul([(ckvn, w_ukv_f)], "nn", bf16, "kv_up")

    def q_rope_fn(qt, cp, sa, sb):
        parts = []
        for h in range(nh):
            parts.append(qt[:, h * QK_PAD: h * QK_PAD + NOPE])
            parts.append(_rope(qt[:, h * QK_PAD + NOPE:(h + 1) * QK_PAD], cp, sa, sb))
        return jnp.concatenate(parts, axis=1)

    tail_cb = off_tail // LANES
    q2 = _rowwise(q_rope_fn, t, tm, [_rows(q_raw, tm), _rows(cosp, tm), _rows(sina, tm), _rows(sinb, tm)],
                  [_rout(t, nh * QK_PAD, bf16, tm)], [], "q_rope")[0]
    kr2 = _rowwise(_rope, t, tm, [_rows(proj, tm, LANES, tail_cb), _rows(cosp, tm), _rows(sina, tm), _rows(sinb, tm)],
                   [_rout(t, LANES, bf16, tm)], [], "k_rope")[0]
    o_att, lse = _flash_fwd(q2, kv, kr2, nh, scale)
    attn = _rowwise(lambda a, w: _rms(a, w), t, tm, [_rows(o_att, tm), _par(w_attn_n)],
                    [_rout(t, mla_w, bf16, tm)], [], "attn_out_norm")[0]

    xbc_raw = proj[:, off_xbc:off_xbc + cdim]
    shifted = [jnp.pad(xbc_raw, ((SSD_K - 1 - k, 0), (0, 0)))[:t] for k in range(SSD_K)]

    def conv_pre(x0, x1, x2, x3, w, b):
        return x0 * w[0:1] + x1 * w[1:2] + x2 * w[2:3] + x3 * w[3:4] + b

    def conv_fn(x0, x1, x2, x3, w, b):
        pre = conv_pre(x0, x1, x2, x3, w, b)
        return pre * _sigmoid(pre)

    xbc_act = _rowwise(conv_fn, t, tm, [_rows(s, tm) for s in shifted] + [_par(conv_full), _par(conv_b_r)],
                       [_rout(t, cdim, f32, tm)], [], "conv_silu")[0]
    dt_raw = proj[:, off_tail + ROPE: off_tail + ROPE + hs]
    dt_raw_t = dt_raw.T
    y_ssd, hprev = _ssd_fwd(xbc_act, dt_raw, dt_raw_t, dtb, dtb_t, alog, alog_t, dskip, width)
    gw = width // SSD_G

    def gate_norm_fn(yv, zv, w):
        gg = yv * (zv * _sigmoid(zv))
        outs = [_rms(gg[:, i * gw:(i + 1) * gw], w[:, i * gw:(i + 1) * gw]) for i in range(SSD_G)]
        return jnp.concatenate(outs, axis=1)

    z_cb = off_z // width if off_z % width == 0 else None
    z_in = _rows(proj, tm, width, z_cb) if z_cb is not None else _rows(proj[:, off_z:off_z + width], tm)
    ssm = _rowwise(gate_norm_fn, t, tm, [_rows(y_ssd, tm), z_in, _par(w_ssd_n)],
                   [_rout(t, width, bf16, tm)], [], "ssd_gate_norm")[0]

    cat = jnp.concatenate([attn, ssm], axis=1)
    mix = _matmul([(cat, w_out_f)], "nn", f32, "out_proj")

    def post_mix_fn(mx, xv, w1, w2):
        h1v = xv + _rms(mx, w1)
        return h1v, _rms(h1v, w2)

    h1, v_in = _rowwise(post_mix_fn, t, tmw, [_rows(mix, tmw), _rows(xs, tmw), _par(w_post_mix), _par(w_pre_ffn)],
                        [_rout(t, d, f32, tmw), _rout(t, d, bf16, tmw)], [], "post_mix_pre_ffn_norm")
    g_ff, u_ff, act = _ffn_up(v_in, w_gate_f, w_up_f)
    ffn = _matmul([(act, w_down_f)], "nn", f32, "ffn_down")

    def final_fn(fv, h1v, tg, w):
        h2 = h1v + _rms(fv, w)
        err = h2 - tg
        lpart = 0.5 * jnp.sum(jnp.sum(err * err, axis=1, keepdims=True), axis=0, keepdims=True) / d
        dh2 = err / d
        dff, dw = _rms_bwd(fv, w, dh2)
        return dff, dh2, jnp.broadcast_to(lpart, (1, LANES)), dw

    dffn, dh2, loss_acc, g_post_ffn = _rowwise(
        final_fn, t, tmw, [_rows(ffn, tmw), _rows(h1, tmw), _rows(tgt, tmw), _par(w_post_ffn)],
        [_rout(t, d, bf16, tmw), _rout(t, d, f32, tmw)], [(1, LANES), (1, d)], "loss_post_ffn_norm_bwd")
    loss = lax.psum(loss_acc[0, 0], ("x", "y", "c"))

    dg_ff, du_ff = _ffn_down_bwd(dffn, w_down_f, g_ff, u_ff)
    gw_down = _matmul([(act, dffn)], "tn", f32, "grad_w_down")
    gw_gate = _matmul([(v_in, dg_ff)], "tn", f32, "grad_w_gate")
    gw_up = _matmul([(v_in, du_ff)], "tn", f32, "grad_w_up")
    dv_in = _matmul([(dg_ff, w_gate_f), (du_ff, w_up_f)], "nt", f32, "ffn_up_bwd")

    def mid_bwd_fn(h1v, dvv, dh2v, mx, w_pf, w_pm):
        dxn, dw_pf = _rms_bwd(h1v, w_pf, dvv)
        dh1v = dh2v + dxn
        dmx, dw_pm = _rms_bwd(mx, w_pm, dh1v)
        return dh1v, dmx, dw_pf, dw_pm

    dh1, dmix, g_pre_ffn, g_post_mix = _rowwise(
        mid_bwd_fn, t, tmw, [_rows(h1, tmw), _rows(dv_in, tmw), _rows(dh2, tmw), _rows(mix, tmw),
                             _par(w_pre_ffn), _par(w_post_mix)],
        [_rout(t, d, f32, tmw), _rout(t, d, bf16, tmw)], [(1, d), (1, d)], "pre_ffn_post_mix_norm_bwd")
    dcat = _matmul([(dmix, w_out_f)], "nt", f32, "out_proj_bwd")
    gw_out = _matmul([(cat, dmix)], "tn", f32, "grad_w_out")

    def attn_norm_bwd_fn(ov, dav, w):
        dov, dw = _rms_bwd(ov, w, dav)
        dl = [jnp.broadcast_to(jnp.sum(dov[:, h * V_HEAD:(h + 1) * V_HEAD] * ov[:, h * V_HEAD:(h + 1) * V_HEAD],
                                       axis=1, keepdims=True), (ov.shape[0], V_HEAD)) for h in range(nh)]
        return dov, jnp.concatenate(dl, axis=1), dw

    do_att, delta, g_attn_n = _rowwise(
        attn_norm_bwd_fn, t, tm, [_rows(o_att, tm), _rows(dcat, tm, mla_w, 0), _par(w_attn_n)],
        [_rout(t, mla_w, bf16, tm), _rout(t, mla_w, f32, tm)], [(1, mla_w)], "attn_out_norm_bwd")
    dkv, dkr_h = _flash_bwd_dkv(q2, kv, kr2, do_att, lse, delta, nh, scale)
    dq2 = _flash_bwd_dq(q2, kv, kr2, do_att, lse, delta, nh, scale)

    def q_rope_bwd_fn(dqt, cp, sa, sb):
        parts = []
        for h in range(nh):
            parts.append(dqt[:, h * QK_PAD: h * QK_PAD + NOPE])
            parts.append(_rope_bwd(dqt[:, h * QK_PAD + NOPE:(h + 1) * QK_PAD], cp, sa, sb))
        return jnp.concatenate(parts, axis=1)

    dq_raw = _rowwise(q_rope_bwd_fn, t, tm, [_rows(dq2, tm), _rows(cosp, tm), _rows(sina, tm), _rows(sinb, tm)],
                      [_rout(t, nh * QK_PAD, bf16, tm)], [], "q_rope_bwd")[0]

    def k_rope_bwd_fn(dk, cp, sa, sb):
        tot = dk[:, 0:LANES]
        for h in range(1, nh):
            tot = tot + dk[:, h * LANES:(h + 1) * LANES]
        return _rope_bwd(tot, cp, sa, sb)

    dkr = _rowwise(k_rope_bwd_fn, t, tm, [_rows(dkr_h, tm), _rows(cosp, tm), _rows(sina, tm), _rows(sinb, tm)],
                   [_rout(t, LANES, f32, tm)], [], "k_rope_bwd")[0]
    gw_uq_p = _matmul([(cqn, dq_raw)], "tn", f32, "grad_w_uq")
    gw_ukv = _matmul([(ckvn, dkv)], "tn", f32, "grad_w_ukv")
    dcqn = _matmul([(dq_raw, w_uq_p)], "nt", f32, "q_up_bwd")
    dckvn = _matmul([(dkv, w_ukv_f)], "nt", f32, "kv_up_bwd")

    def lat_norm_bwd_fn(a, w, dyv):
        return _rms_bwd(a, w, dyv)

    dcq, g_qn = _rowwise(lat_norm_bwd_fn, t, tm, [_rows(proj, tm, Q_RANK, 0), _par(w_qn), _rows(dcqn, tm)],
                         [_rout(t, Q_RANK, bf16, tm)], [(1, Q_RANK)], "q_norm_bwd")
    dckv, g_kvn = _rowwise(lat_norm_bwd_fn, t, tm, [_rows(proj, tm, KV_RANK, 1), _par(w_kvn), _rows(dckvn, tm)],
                           [_rout(t, KV_RANK, bf16, tm)], [(1, KV_RANK)], "kv_norm_bwd")

    def gate_norm_bwd_fn(yv, zv, dsv, w):
        sg = _sigmoid(zv)
        sz = zv * sg
        gg = yv * sz
        dgs, dws = [], []
        for i in range(SSD_G):
            sl = slice(i * gw, (i + 1) * gw)
            dgi, dwi = _rms_bwd(gg[:, sl], w[:, sl], dsv[:, sl])
            dgs.append(dgi)
            dws.append(dwi)
        dgg = jnp.concatenate(dgs, axis=1)
        return dgg * sz, dgg * yv * (sg * (1.0 + zv * (1.0 - sg))), jnp.concatenate(dws, axis=1)

    d_ssm_cb = mla_w // width if mla_w % width == 0 else None
    dssm_in = (_rows(dcat, tm, width, d_ssm_cb) if d_ssm_cb is not None else _rows(dcat[:, mla_w:], tm))
    dy_ssd, dz, g_ssd_n = _rowwise(
        gate_norm_bwd_fn, t, tm, [_rows(y_ssd, tm), z_in, dssm_in, _par(w_ssd_n)],
        [_rout(t, width, f32, tm), _rout(t, width, bf16, tm)], [(1, width)], "ssd_gate_norm_bwd")
    dxbc_act, ddt_raw, g_alog, g_dskip, g_dtb = _ssd_bwd(
        xbc_act, dt_raw, dt_raw_t, dtb, dtb_t, alog, alog_t, dskip, hprev, dy_ssd, width)

    def conv_bwd_fn(x0, x1, x2, x3, w, b, dact):
        pre = conv_pre(x0, x1, x2, x3, w, b)
        dpre = dact * _silu_grad(pre)
        s = lambda a: jnp.sum(a, axis=0, keepdims=True)
        return dpre, s(dpre * x0), s(dpre * x1), s(dpre * x2), s(dpre * x3), s(dpre)

    dpre, gcw0, gcw1, gcw2, gcw3, g_conv_b = _rowwise(
        conv_bwd_fn, t, tm, [_rows(s, tm) for s in shifted] + [_par(conv_full), _par(conv_b_r), _rows(dxbc_act, tm)],
        [_rout(t, cdim, f32, tm)], [(1, cdim)] * 5, "conv_silu_bwd")
    g_conv_w = jnp.concatenate([gcw0, gcw1, gcw2, gcw3], axis=0)
    dshift = [jnp.pad(dpre, ((0, SSD_K - 1 - k), (0, 0)))[SSD_K - 1 - k:] for k in range(SSD_K)]

    def conv_dx_fn(d0, d1, d2, d3, w):
        return d0 * w[0:1] + d1 * w[1:2] + d2 * w[2:3] + d3 * w[3:4]

    dxbc = _rowwise(conv_dx_fn, t, tm, [_rows(s, tm) for s in dshift] + [_par(conv_full)],
                    [_rout(t, cdim, bf16, tm)], [], "conv_bwd_dx")[0]

    dtail = jnp.concatenate([dkr[:, :ROPE], ddt_raw, jnp.zeros((t, d_in_p - off_tail - ROPE - hs), f32)],
                            axis=1).astype(bf16)
    dproj = jnp.concatenate([dcq, dckv, dz, dxbc, dtail], axis=1)
    du_in = _matmul([(dproj, w_in_p)], "nt", f32, "in_proj_bwd")
    gw_in_p = _matmul([(u, dproj)], "tn", f32, "grad_w_in")

    def first_bwd_fn(xv, duv, dh1v, w):
        dxn, dw = _rms_bwd(xv, w, duv)
        return dh1v + dxn, dw

    grad_x, g_pre_mix = _rowwise(first_bwd_fn, t, tmw, [_rows(xs, tmw), _rows(du_in, tmw), _rows(dh1, tmw),
                                                         _par(w_pre_mix)],
                                 [_rout(t, d, f32, tmw)], [(1, d)], "pre_mix_norm_bwd")

    gseg = lambda a, b: gw_in_p[:, a:b]
    gw_in = jnp.concatenate([gseg(0, Q_RANK), gseg(Q_RANK, off_z), gseg(off_tail, off_tail + ROPE),
                             gseg(off_z, off_xbc), gseg(off_xbc, off_tail),
                             gseg(off_tail + ROPE, off_tail + ROPE + hs)], axis=1)
    gw_uq = gw_uq_p.reshape(Q_RANK, nh, QK_PAD)[:, :, :qk_head].reshape(Q_RANK, nh * qk_head)
    gfull = {"w_in": gw_in, "w_uq": gw_uq, "w_ukv": gw_ukv, "w_out": gw_out, "w_gate": gw_gate, "w_up": gw_up,
             "w_down": gw_down}
    by_chip = []
    for n in BIG:
        r, cs = wts[n].shape
        gf = gfull[n]
        by_chip.append(jnp.transpose(gf.reshape(r, nchip, cs), (1, 0, 2)).reshape(nchip, r * cs) if n in COL_SHARDED
                       else gf.reshape(nchip, r * cs))
    gsend = jnp.pad(jnp.concatenate(by_chip, axis=1), ((0, 0), (0, ls_pad - ls))).reshape(nchip, 2, rows_half, cols)
    from_sib = _sibling_send_halves(gsend)
    sp = jnp.stack([my_chip, my_c]).astype(jnp.int32)
    tr = _tile(rows_half, ROW_TILE, 8)
    pair = _blocked(
        lambda a, b: a + b, (nchip, rows_half // tr),
        [(gsend, (None, None, tr, cols), lambda k, i, s: (k, s[1], i, 0)),
         (from_sib, (None, tr, cols), lambda k, i, s: (k, i, 0))],
        [((nchip, rows_half, cols), bf16, (None, tr, cols), lambda k, i, s: (k, i, 0))], [], "pair_sum", sp=sp)[0]
    from_chips = _chips_exchange(pair)
    reduced_half = _blocked(
        lambda a, b, r1, r2, r3: ((a + b) + r1.astype(f32)) + r2.astype(f32) + r3.astype(f32), (rows_half // tr,),
        [(gsend, (None, None, tr, cols), lambda i, s: (s[0], s[1], i, 0)),
         (from_sib, (None, tr, cols), lambda i, s: (s[0], i, 0)),
         (from_chips, (None, tr, cols), lambda i, s: (s[0] ^ 1, i, 0)),
         (from_chips, (None, tr, cols), lambda i, s: (s[0] ^ 2, i, 0)),
         (from_chips, (None, tr, cols), lambda i, s: (s[0] ^ 3, i, 0))],
        [((rows_half, cols), f32, (tr, cols), lambda i, s: (i, 0))], [], "chip_sum", sp=sp)[0]
    reduced = _sibling_share(reduced_half).reshape(ls_pad)
    gshard = {}
    off = 0
    for n, sz in zip(BIG, shard_sizes):
        gshard[n] = reduced[off:off + sz].reshape(wts[n].shape)
        off += sz

    gsmall = {"q_norm_w": g_qn, "kv_norm_w": g_kvn, "conv_w": g_conv_w, "conv_b": g_conv_b, "dt_bias": g_dtb,
              "a_log": g_alog, "d_skip": g_dskip, "ssd_norm_w": g_ssd_n, "attn_out_norm_w": g_attn_n,
              "pre_mix_norm_w": g_pre_mix, "post_mix_norm_w": g_post_mix, "pre_ffn_norm_w": g_pre_ffn,
              "post_ffn_norm_w": g_post_ffn}
    small_sizes = [int(np.prod(gsmall[n].shape)) for n in SMALL]
    srows = _round_up(-(-sum(small_sizes) // LANES), 8)
    spart = _flat_pad([gsmall[n] for n in SMALL], srows * LANES).reshape(srows, LANES)
    sall = _gather_all(spart)

    def sum8_fn(a):
        tot = a[0]
        for k in range(1, 8):
            tot = tot + a[k]
        return tot

    ssum = _blocked(sum8_fn, (1,), [(sall, sall.shape, lambda i: (0, 0, 0))],
                    [((srows, LANES), f32, (srows, LANES), lambda i: (0, 0))], [], "small_grad_sum")[0].reshape(-1)
    gred = {}
    off = 0
    for n, sz in zip(SMALL, small_sizes):
        gred[n] = ssum[off:off + sz].reshape(gsmall[n].shape)
        off += sz
    gshard["conv_w"] = lax.dynamic_slice_in_dim(gred["conv_w"], my_chip * ccs, ccs, axis=1)
    for n in SMALL:
        if n != "conv_w":
            gshard[n] = gred[n].reshape(wts[n].shape)

    delta, new_m, new_v = {}, {}, {}
    for n in BIG:
        delta[n], new_m[n], new_v[n] = _adamw_call(wts[n], gshard[n], mom_m[n], mom_v[n], "adamw_" + n)
    pack = lambda src: _flat_pad([src[n] for n in SMALL], srows * LANES).reshape(srows, LANES)
    sd, sm, sv = _adamw_call(pack(wts), pack(gshard), pack(mom_m), pack(mom_v), "adamw_small")
    off = 0
    for n in SMALL:
        sz = int(np.prod(wts[n].shape))
        for dst, src in ((delta, sd), (new_m, sm), (new_v, sv)):
            dst[n] = src.reshape(-1)[off:off + sz].reshape(wts[n].shape)
        off += sz

    lead = lambda a: a[None]
    return (loss, grad_x[None], *[lead(gshard[n]) for n in ORDER], *[lead(delta[n]) for n in ORDER],
            *[lead(new_m[n]) for n in ORDER], *[lead(new_v[n]) for n in ORDER])
```

```python
import functools

import numpy as np
import jax
import jax.numpy as jnp
from jax import lax
from jax.experimental import pallas as pl
from jax.experimental.pallas import tpu as pltpu

f32, bf16 = jnp.float32, jnp.bfloat16

EPS = 1e-6
V_HEAD = 128
NOPE = 128
ROPE = 64
QK_PAD = 256
Q_RANK = 512
KV_RANK = 512
ROPE_THETA = 10000.0
SSD_P = 64
SSD_G = 2
SSD_N = 128
SSD_K = 4
CHUNK = 128
ADAM_LR, ADAM_B1, ADAM_B2, ADAM_EPS, ADAM_WD, ADAM_STEP = 0.001, 0.9, 0.999, 1e-08, 0.01, 10

VMEM_LIMIT_BYTES = 48 * 1024 * 1024
LANES = 128
ATT_TILE = 512
MM_TM, MM_TN, MM_TK = 1024, 1024, 512
CHUNK_WHOLE_MAX = 1536
ROW_TILE = 256

NN = (((1,), (0,)), ((), ()))
NT = (((1,), (1,)), ((), ()))
TN = (((0,), (0,)), ((), ()))
MESH = pl.DeviceIdType.MESH
ANY = pl.BlockSpec(memory_space=pl.ANY)


def _tile(dim, cap, align=LANES):
    if dim <= cap:
        return dim
    t = (cap // align) * align
    while t >= align:
        if dim % t == 0:
            return t
        t -= align
    raise ValueError(f"no tile for {dim} under {cap}")


def _round_up(n, m):
    return -(-n // m) * m


def _params(sem):
    return pltpu.CompilerParams(dimension_semantics=sem, vmem_limit_bytes=VMEM_LIMIT_BYTES)


def _dot(a, b, dims):
    return lax.dot_general(a.astype(bf16), b.astype(bf16), dims, preferred_element_type=f32)


def _chunk_tile(cs, cap):
    return cs if cs <= CHUNK_WHOLE_MAX else _tile(cs, cap)


def _matmul(pairs, mode, out_dtype, name, out_chunks=None):
    a0, b0 = pairs[0]
    chunked = b0.ndim == 3
    cs = b0.shape[2] if chunked else None
    bcols = b0.shape[0] * b0.shape[2] if chunked else b0.shape[1]
    brows = b0.shape[1] if chunked else b0.shape[0]
    if mode == "nn":
        (m, k), n = a0.shape, bcols
    elif mode == "nt":
        (m, k), n = a0.shape, brows
    else:
        (k, m), n = a0.shape, bcols
    tm = _tile(m, MM_TM)
    if mode == "nt":
        tn = _tile(n, MM_TN)
        tk = _chunk_tile(cs, MM_TK) if chunked else _tile(k, MM_TK)
    else:
        tk = _tile(k, MM_TK)
        if chunked:
            tn = _chunk_tile(cs, MM_TN)
        elif out_chunks:
            tn = _chunk_tile(n // out_chunks, MM_TN)
        else:
            tn = _tile(n, MM_TN)
    nk = k // tk
    if mode == "nn":
        a_spec = pl.BlockSpec((tm, tk), lambda i, j, kk: (i, kk))
        if chunked:
            q = cs // tn
            b_spec = pl.BlockSpec((None, tk, tn), lambda i, j, kk: (j // q, kk, j % q))
        else:
            b_spec = pl.BlockSpec((tk, tn), lambda i, j, kk: (kk, j))
        dims = NN
    elif mode == "nt":
        a_spec = pl.BlockSpec((tm, tk), lambda i, j, kk: (i, kk))
        if chunked:
            q = cs // tk
            b_spec = pl.BlockSpec((None, tn, tk), lambda i, j, kk: (kk // q, j, kk % q))
        else:
            b_spec = pl.BlockSpec((tn, tk), lambda i, j, kk: (j, kk))
        dims = NT
    else:
        a_spec = pl.BlockSpec((tk, tm), lambda i, j, kk: (kk, i))
        b_spec = pl.BlockSpec((tk, tn), lambda i, j, kk: (kk, j))
        dims = TN
    if out_chunks:
        qo = (n // out_chunks) // tn
        out_shape = jax.ShapeDtypeStruct((out_chunks, m, n // out_chunks), out_dtype)
        o_spec = pl.BlockSpec((None, tm, tn), lambda i, j, kk: (j // qo, i, j % qo))
    else:
        out_shape = jax.ShapeDtypeStruct((m, n), out_dtype)
        o_spec = pl.BlockSpec((tm, tn), lambda i, j, kk: (i, j))
    npair = len(pairs)

    def body(*refs):
        o_ref, acc = refs[2 * npair], refs[2 * npair + 1]
        kk = pl.program_id(2)

        @pl.when(kk == 0)
        def _():
            acc[...] = jnp.zeros_like(acc)

        part = _dot(refs[0][...], refs[1][...], dims)
        for p in range(1, npair):
            part = part + _dot(refs[2 * p][...], refs[2 * p + 1][...], dims)
        acc[...] += part

        @pl.when(kk == nk - 1)
        def _():
            o_ref[...] = acc[...].astype(out_dtype)

    args = [t for pr in pairs for t in pr]
    return pl.pallas_call(
        body, name=name, out_shape=out_shape,
        grid=(m // tm, n // tn, nk), in_specs=[a_spec, b_spec] * npair,
        out_specs=o_spec,
        scratch_shapes=[pltpu.VMEM((tm, tn), f32)],
        compiler_params=_params(("parallel", "parallel", "arbitrary")),
    )(*args)


def _sigmoid(x):
    return 1.0 / (1.0 + jnp.exp(-x))


def _ffn_up(v, wg, wu):
    m, k = v.shape
    nchunk, _, cs = wg.shape
    n = nchunk * cs
    tm, tn, tk = _tile(m, 512), _chunk_tile(cs, 512), _tile(k, MM_TK)
    nk = k // tk
    q = cs // tn
    w_spec = pl.BlockSpec((None, tk, tn), lambda i, j, kk: (j // q, kk, j % q))

    def body(v_ref, wg_ref, wu_ref, g_ref, u_ref, act_ref, accg, accu):
        kk = pl.program_id(2)

        @pl.when(kk == 0)
        def _():
            accg[...] = jnp.zeros_like(accg)
            accu[...] = jnp.zeros_like(accu)

        vb = v_ref[...]
        accg[...] += _dot(vb, wg_ref[...], NN)
        accu[...] += _dot(vb, wu_ref[...], NN)

        @pl.when(kk == nk - 1)
        def _():
            g, u = accg[...], accu[...]
            g_ref[...] = g.astype(bf16)
            u_ref[...] = u.astype(bf16)
            act_ref[...] = (g * _sigmoid(g) * u).astype(bf16)

    out = jax.ShapeDtypeStruct((m, n), bf16)
    o_spec = pl.BlockSpec((tm, tn), lambda i, j, kk: (i, j))
    return pl.pallas_call(
        body, name="ffn_up", out_shape=(out, out, out), grid=(m // tm, n // tn, nk),
        in_specs=[pl.BlockSpec((tm, tk), lambda i, j, kk: (i, kk)), w_spec, w_spec],
        out_specs=(o_spec, o_spec, o_spec),
        scratch_shapes=[pltpu.VMEM((tm, tn), f32), pltpu.VMEM((tm, tn), f32)],
        compiler_params=_params(("parallel", "parallel", "arbitrary")),
    )(v, wg, wu)


def _ffn_down_bwd(dffn, wd, g, u):
    m, k = dffn.shape
    n = wd.shape[0]
    tm, tn, tk = _tile(m, MM_TM), _tile(n, 512), _tile(k, MM_TK)
    nk = k // tk

    def body(d_ref, w_ref, g_ref, u_ref, dg_ref, du_ref, acc):
        kk = pl.program_id(2)

        @pl.when(kk == 0)
        def _():
            acc[...] = jnp.zeros_like(acc)

        acc[...] += _dot(d_ref[...], w_ref[...], NT)

        @pl.when(kk == nk - 1)
        def _():
            dact = acc[...]
            gg, uu = g_ref[...].astype(f32), u_ref[...].astype(f32)
            sg = _sigmoid(gg)
            du_ref[...] = (dact * gg * sg).astype(bf16)
            dg_ref[...] = (dact * uu * (sg * (1.0 + gg * (1.0 - sg)))).astype(bf16)

    out = jax.ShapeDtypeStruct((m, n), bf16)
    o_spec = pl.BlockSpec((tm, tn), lambda i, j, kk: (i, j))
    return pl.pallas_call(
        body, name="ffn_down_bwd", out_shape=(out, out), grid=(m // tm, n // tn, nk),
        in_specs=[pl.BlockSpec((tm, tk), lambda i, j, kk: (i, kk)),
                  pl.BlockSpec((tn, tk), lambda i, j, kk: (j, kk)), o_spec, o_spec],
        out_specs=(o_spec, o_spec),
        scratch_shapes=[pltpu.VMEM((tm, tn), f32)],
        compiler_params=_params(("parallel", "parallel", "arbitrary")),
    )(dffn, wd, g, u)


def _blocked(fn, grid, ins, outs, accs, name, sp=None):
    n_in, n_out, n_acc = len(ins), len(outs), len(accs)
    nsp = 0 if sp is None else 1

    def body(*refs):
        refs = refs[nsp:]
        tiles = [r[...] for r in refs[:n_in]]
        res = fn(*tiles)
        if not isinstance(res, (tuple, list)):
            res = (res,)
        for r, val in zip(refs[n_in:n_in + n_out], res[:n_out]):
            r[...] = val.astype(r.dtype)
        if n_acc:
            first = pl.program_id(0) == 0
            for d in range(1, len(grid)):
                first = jnp.logical_and(first, pl.program_id(d) == 0)

            @pl.when(first)
            def _():
                for r in refs[n_in + n_out:]:
                    r[...] = jnp.zeros_like(r)

            for r, val in zip(refs[n_in + n_out:], res[n_out:]):
                r[...] += val

    def acc_map(shape):
        zeros = (0,) * len(shape)
        return lambda *idx: zeros

    in_specs = [pl.BlockSpec(bs, im) for _, bs, im in ins]
    out_specs = [pl.BlockSpec(bs, im) for _, _, bs, im in outs] + [pl.BlockSpec(s, acc_map(s)) for s in accs]
    out_shape = [jax.ShapeDtypeStruct(s, d) for s, d, _, _ in outs] + [jax.ShapeDtypeStruct(s, f32) for s in accs]
    sem = ("arbitrary",) * len(grid) if n_acc else ("parallel",) * len(grid)
    args = [a for a, _, _ in ins]
    if sp is None:
        res = pl.pallas_call(body, name=name, out_shape=out_shape, grid=grid, in_specs=in_specs,
                             out_specs=out_specs, compiler_params=_params(sem))(*args)
    else:
        gs = pltpu.PrefetchScalarGridSpec(num_scalar_prefetch=1, grid=grid, in_specs=in_specs, out_specs=out_specs)
        res = pl.pallas_call(body, name=name, out_shape=out_shape, grid_spec=gs,
                             compiler_params=_params(sem))(sp, *args)
    return res


def _rows(a, tm, cols=None, cb=0):
    w = a.shape[1] if cols is None else cols
    return (a, (tm, w), lambda i: (i, cb))


def _par(a):
    zeros = (0,) * a.ndim
    return (a, a.shape, lambda i: zeros)


def _rout(t, w, dtype, tm):
    return ((t, w), dtype, (tm, w), lambda i: (i, 0))


def _rowwise(fn, t, tm, ins, outs, accs, name):
    return _blocked(fn, (t // tm,), ins, outs, accs, name)


def _rms(x, w):
    r = lax.rsqrt(jnp.mean(x * x, axis=-1, keepdims=True) + EPS)
    return x * r * w


def _rms_bwd(x, w, dy):
    r = lax.rsqrt(jnp.mean(x * x, axis=-1, keepdims=True) + EPS)
    xh = x * r
    dyw = dy * w
    dx = r * (dyw - xh * jnp.mean(dyw * xh, axis=-1, keepdims=True))
    return dx, jnp.sum(dy * xh, axis=0, keepdims=True)


def _silu_grad(x):
    s = _sigmoid(x)
    return s * (1.0 + x * (1.0 - s))


def _rope(blk, cosp, sina, sinb):
    return blk * cosp + pltpu.roll(blk, 96, 1) * sina + pltpu.roll(blk, 32, 1) * sinb


def _rope_bwd(dy, cosp, sina, sinb):
    return dy * cosp + pltpu.roll(dy * sina, 32, 1) + pltpu.roll(dy * sinb, 96, 1)


HALO = 8


def _conv_taps(buf, w, tm, base):
    acc = buf[base:base + tm, :] * w[0:1]
    for k in range(1, SSD_K):
        acc = acc + buf[base + k:base + k + tm, :] * w[k:k + 1]
    return acc


def _conv_specs(t, tm, cdim):
    cur = pl.BlockSpec((tm, cdim), lambda i: (i, 0))
    prev = pl.BlockSpec((HALO, cdim), lambda i: (jnp.maximum(i * (tm // HALO) - 1, 0), 0))
    nxt = pl.BlockSpec((HALO, cdim), lambda i: (jnp.minimum((i + 1) * (tm // HALO), t // HALO - 1), 0))
    return cur, prev, nxt


def _conv_fwd(src, w, b, cdim, tm):
    t = src.shape[0]
    cur, prev, _ = _conv_specs(t, tm, cdim)

    def body(x_ref, p_ref, w_ref, b_ref, o_ref, buf):
        buf[0:HALO, :] = jnp.where(pl.program_id(0) > 0, p_ref[...], 0.0)
        buf[HALO:HALO + tm, :] = x_ref[...]
        pre = _conv_taps(buf, w_ref[...], tm, HALO - (SSD_K - 1)) + b_ref[...]
        o_ref[...] = pre * _sigmoid(pre)

    par = lambda a: pl.BlockSpec(a.shape, lambda i: (0, 0))
    return pl.pallas_call(
        body, name="conv_silu", out_shape=jax.ShapeDtypeStruct((t, cdim), f32), grid=(t // tm,),
        in_specs=[cur, prev, par(w), par(b)], out_specs=cur,
        scratch_shapes=[pltpu.VMEM((tm + HALO, cdim), f32)], compiler_params=_params(("parallel",)),
    )(src, src, w, b)


def _conv_bwd_pre(src, w, b, dact, cdim, tm):
    t = src.shape[0]
    cur, prev, _ = _conv_specs(t, tm, cdim)

    def body(x_ref, p_ref, w_ref, b_ref, d_ref, dpre_ref, dw0, dw1, dw2, dw3, db, buf):
        @pl.when(pl.program_id(0) == 0)
        def _():
            for r in (dw0, dw1, dw2, dw3, db):
                r[...] = jnp.zeros_like(r)

        buf[0:HALO, :] = jnp.where(pl.program_id(0) > 0, p_ref[...], 0.0)
        buf[HALO:HALO + tm, :] = x_ref[...]
        base = HALO - (SSD_K - 1)
        pre = _conv_taps(buf, w_ref[...], tm, base) + b_ref[...]
        dpre = d_ref[...] * _silu_grad(pre)
        dpre_ref[...] = dpre
        for k, r in enumerate((dw0, dw1, dw2, dw3)):
            r[...] += jnp.sum(dpre * buf[base + k:base + k + tm, :], axis=0, keepdims=True)
        db[...] += jnp.sum(dpre, axis=0, keepdims=True)

    par = lambda a: pl.BlockSpec(a.shape, lambda i: (0, 0))
    acc = pl.BlockSpec((1, cdim), lambda i: (0, 0))
    acc_shape = jax.ShapeDtypeStruct((1, cdim), f32)
    return pl.pallas_call(
        body, name="conv_silu_bwd", out_shape=[jax.ShapeDtypeStruct((t, cdim), f32)] + [acc_shape] * 5,
        grid=(t // tm,), in_specs=[cur, prev, par(w), par(b), cur], out_specs=[cur] + [acc] * 5,
        scratch_shapes=[pltpu.VMEM((tm + HALO, cdim), f32)], compiler_params=_params(("arbitrary",)),
    )(src, src, w, b, dact)


def _conv_bwd_dx(dpre, w, tm):
    t, cdim = dpre.shape
    cur, _, nxt = _conv_specs(t, tm, cdim)
    last = t // tm - 1

    def body(d_ref, n_ref, w_ref, o_ref, buf):
        buf[0:tm, :] = d_ref[...]
        buf[tm:tm + HALO, :] = jnp.where(pl.program_id(0) < last, n_ref[...], 0.0)
        wv = w_ref[...]
        acc = buf[0:tm, :] * wv[SSD_K - 1:SSD_K]
        for k in range(SSD_K - 1):
            s = SSD_K - 1 - k
            acc = acc + buf[s:s + tm, :] * wv[k:k + 1]
        o_ref[...] = acc.astype(bf16)

    return pl.pallas_call(
        body, name="conv_bwd_dx", out_shape=jax.ShapeDtypeStruct((t, cdim), bf16), grid=(t // tm,),
        in_specs=[cur, nxt, pl.BlockSpec(w.shape, lambda i: (0, 0))], out_specs=cur,
        scratch_shapes=[pltpu.VMEM((tm + HALO, cdim), f32)], compiler_params=_params(("parallel",)),
    )(dpre, dpre, w)


def _causal_mask(s, diag):
    row = lax.broadcasted_iota(jnp.int32, s.shape, 0)
    col = lax.broadcasted_iota(jnp.int32, s.shape, 1)
    return jnp.where(jnp.logical_or(jnp.logical_not(diag), row >= col), s, -jnp.inf)


def _flash_fwd(q, kv, kr, nheads, scale):
    t = q.shape[0]
    tq = _tile(t, ATT_TILE)
    nq = t // tq

    def body(q_ref, kn_ref, kr_ref, v_ref, o_ref, lse_ref, m_sc, l_sc, acc_sc):
        qi, ki = pl.program_id(1), pl.program_id(2)

        @pl.when(ki == 0)
        def _():
            m_sc[...] = jnp.full_like(m_sc, -jnp.inf)
            l_sc[...] = jnp.zeros_like(l_sc)
            acc_sc[...] = jnp.zeros_like(acc_sc)

        @pl.when(ki <= qi)
        def _():
            k = jnp.concatenate([kn_ref[...], kr_ref[...]], axis=1)
            s = lax.dot_general(q_ref[...], k, NT, preferred_element_type=f32) * scale
            s = _causal_mask(s, ki == qi)
            m_old = m_sc[...]
            m_new = jnp.maximum(m_old, jnp.max(s, axis=1, keepdims=True))
            alpha = jnp.exp(m_old - m_new)
            p = jnp.exp(s - m_new)
            l_sc[...] = alpha * l_sc[...] + jnp.sum(p, axis=1, keepdims=True)
            acc_sc[...] = alpha * acc_sc[...] + lax.dot_general(p.astype(bf16), v_ref[...], NN,
                                                                preferred_element_type=f32)
            m_sc[...] = m_new

        @pl.when(ki == qi)
        def _():
            l = l_sc[...]
            o_ref[...] = acc_sc[...] / l
            lse_ref[...] = jnp.broadcast_to(m_sc[...] + jnp.log(l), lse_ref.shape)

    kc = lambda qi, ki: jnp.minimum(ki, qi)
    o_spec = pl.BlockSpec((tq, V_HEAD), lambda h, qi, ki: (qi, h))
    out = jax.ShapeDtypeStruct((t, nheads * V_HEAD), f32)
    return pl.pallas_call(
        body, name="flash_fwd", out_shape=(out, out), grid=(nheads, nq, nq),
        in_specs=[pl.BlockSpec((tq, QK_PAD), lambda h, qi, ki: (qi, h)),
                  pl.BlockSpec((tq, NOPE), lambda h, qi, ki: (kc(qi, ki), 2 * h)),
                  pl.BlockSpec((tq, LANES), lambda h, qi, ki: (kc(qi, ki), 0)),
                  pl.BlockSpec((tq, V_HEAD), lambda h, qi, ki: (kc(qi, ki), 2 * h + 1))],
        out_specs=(o_spec, o_spec),
        scratch_shapes=[pltpu.VMEM((tq, 1), f32), pltpu.VMEM((tq, 1), f32), pltpu.VMEM((tq, V_HEAD), f32)],
        compiler_params=_params(("parallel", "parallel", "arbitrary")),
    )(q, kv, kr, kv)


def _flash_bwd_dkv(q, kv, kr, do, lse, delta, nheads, scale):
    t = q.shape[0]
    tq = _tile(t, ATT_TILE)
    nq = t // tq

    def body(q_ref, kn_ref, kr_ref, v_ref, do_ref, lse_ref, dl_ref, dkv_ref, dkr_ref, dk_sc, dv_sc):
        ki, qi = pl.program_id(1), pl.program_id(2)

        @pl.when(qi == 0)
        def _():
            dk_sc[...] = jnp.zeros_like(dk_sc)
            dv_sc[...] = jnp.zeros_like(dv_sc)

        @pl.when(qi >= ki)
        def _():
            qb = q_ref[...]
            dob = do_ref[...]
            k = jnp.concatenate([kn_ref[...], kr_ref[...]], axis=1)
            s = lax.dot_general(qb, k, NT, preferred_element_type=f32) * scale
            s = _causal_mask(s, ki == qi)
            p = jnp.exp(s - lse_ref[:, 0:1])
            dv_sc[...] += lax.dot_general(p.astype(bf16), dob, TN, preferred_element_type=f32)
            dp = lax.dot_general(dob, v_ref[...], NT, preferred_element_type=f32)
            ds = p * (dp - dl_ref[:, 0:1]) * scale
            dk_sc[...] += lax.dot_general(ds.astype(bf16), qb, TN, preferred_element_type=f32)

        @pl.when(qi == nq - 1)
        def _():
            dk = dk_sc[...]
            dkv_ref[...] = jnp.concatenate([dk[:, :NOPE], dv_sc[...]], axis=1).astype(bf16)
            dkr_ref[...] = dk[:, NOPE:]

    qc = lambda ki, qi: jnp.maximum(ki, qi)
    hspec = pl.BlockSpec((tq, V_HEAD), lambda h, ki, qi: (qc(ki, qi), h))
    return pl.pallas_call(
        body, name="flash_bwd_dkv",
        out_shape=(jax.ShapeDtypeStruct((t, nheads * QK_PAD), bf16), jax.ShapeDtypeStruct((t, nheads * LANES), f32)),
        grid=(nheads, nq, nq),
        in_specs=[pl.BlockSpec((tq, QK_PAD), lambda h, ki, qi: (qc(ki, qi), h)),
                  pl.BlockSpec((tq, NOPE), lambda h, ki, qi: (ki, 2 * h)),
                  pl.BlockSpec((tq, LANES), lambda h, ki, qi: (ki, 0)),
                  pl.BlockSpec((tq, V_HEAD), lambda h, ki, qi: (ki, 2 * h + 1)),
                  hspec, hspec, hspec],
        out_specs=(pl.BlockSpec((tq, QK_PAD), lambda h, ki, qi: (ki, h)),
                   pl.BlockSpec((tq, LANES), lambda h, ki, qi: (ki, h))),
        scratch_shapes=[pltpu.VMEM((tq, QK_PAD), f32), pltpu.VMEM((tq, V_HEAD), f32)],
        compiler_params=_params(("parallel", "parallel", "arbitrary")),
    )(q, kv, kr, kv, do, lse, delta)


def _flash_bwd_dq(q, kv, kr, do, lse, delta, nheads, scale):
    t = q.shape[0]
    tq = _tile(t, ATT_TILE)
    nq = t // tq

    def body(q_ref, kn_ref, kr_ref, v_ref, do_ref, lse_ref, dl_ref, dq_ref, dq_sc):
        qi, ki = pl.program_id(1), pl.program_id(2)

        @pl.when(ki == 0)
        def _():
            dq_sc[...] = jnp.zeros_like(dq_sc)

        @pl.when(ki <= qi)
        def _():
            dob = do_ref[...]
            k = jnp.concatenate([kn_ref[...], kr_ref[...]], axis=1)
            s = lax.dot_general(q_ref[...], k, NT, preferred_element_type=f32) * scale
            s = _causal_mask(s, ki == qi)
            p = jnp.exp(s - lse_ref[:, 0:1])
            dp = lax.dot_general(dob, v_ref[...], NT, preferred_element_type=f32)
            ds = p * (dp - dl_ref[:, 0:1]) * scale
            dq_sc[...] += lax.dot_general(ds.astype(bf16), k, NN, preferred_element_type=f32)

        @pl.when(ki == qi)
        def _():
            dq_ref[...] = dq_sc[...]

    kc = lambda qi, ki: jnp.minimum(ki, qi)
    hspec = pl.BlockSpec((tq, V_HEAD), lambda h, qi, ki: (qi, h))
    return pl.pallas_call(
        body, name="flash_bwd_dq", out_shape=jax.ShapeDtypeStruct((t, nheads * QK_PAD), f32),
        grid=(nheads, nq, nq),
        in_specs=[pl.BlockSpec((tq, QK_PAD), lambda h, qi, ki: (qi, h)),
                  pl.BlockSpec((tq, NOPE), lambda h, qi, ki: (kc(qi, ki), 2 * h)),
                  pl.BlockSpec((tq, LANES), lambda h, qi, ki: (kc(qi, ki), 0)),
                  pl.BlockSpec((tq, V_HEAD), lambda h, qi, ki: (kc(qi, ki), 2 * h + 1)),
                  hspec, hspec, hspec],
        out_specs=pl.BlockSpec((tq, QK_PAD), lambda h, qi, ki: (qi, h)),
        scratch_shapes=[pltpu.VMEM((tq, QK_PAD), f32)],
        compiler_params=_params(("parallel", "parallel", "arbitrary")),
    )(q, kv, kr, kv, do, lse, delta)


def _split3(a):
    hi = a.astype(bf16)
    r1 = a - hi.astype(f32)
    mid = r1.astype(bf16)
    lo = (r1 - mid.astype(f32)).astype(bf16)
    return hi, mid, lo


def _ones_dot_left(tri, a):
    hi, mid, lo = _split3(a)
    d = lambda v: lax.dot_general(tri, v, NN, preferred_element_type=f32)
    return d(hi) + d(mid) + d(lo)


def _ones_dot_right(a, tri):
    hi, mid, lo = _split3(a)
    d = lambda v: lax.dot_general(v, tri, NN, preferred_element_type=f32)
    return d(hi) + d(mid) + d(lo)


def _softplus(x):
    return jnp.maximum(x, 0.0) + jnp.log(1.0 + jnp.exp(-jnp.abs(x)))


def _ssd_common(dt_ref, dtT_ref, dtb_ref, dtbT_ref, alog_ref, alogT_ref):
    ii = lax.broadcasted_iota(jnp.int32, (CHUNK, CHUNK), 0)
    jj = lax.broadcasted_iota(jnp.int32, (CHUNK, CHUNK), 1)
    tri = ii >= jj
    raw = dt_ref[...] + dtb_ref[...]
    dt = _softplus(raw)
    a_neg = -jnp.exp(alog_ref[...])
    cum = _ones_dot_left(tri.astype(bf16), dt * a_neg)
    dt_t = _softplus(dtT_ref[...] + dtbT_ref[...])
    cum_t = _ones_dot_right(dt_t * (-jnp.exp(alogT_ref[...])), (ii <= jj).astype(bf16))
    return tri, raw, dt, a_neg, cum, cum_t


def _ssd_fwd(xbc, dt, dt_t, dtb, dtb_t, alog, alog_t, dskip, width):
    t, cdim = xbc.shape
    hs = dt.shape[1]
    nc = t // CHUNK
    epg = hs // SSD_G
    gn = SSD_G * SSD_N

    def body(x_ref, dt_ref, dtT_ref, dtb_ref, dtbT_ref, alog_ref, alogT_ref, d_ref, y_ref, hp_ref, h_sc):
        @pl.when(pl.program_id(0) == 0)
        def _():
            h_sc[...] = jnp.zeros_like(h_sc)

        tri, _, dtv, _, cum, cum_t = _ssd_common(dt_ref, dtT_ref, dtb_ref, dtbT_ref, alog_ref, alogT_ref)
        dsk = d_ref[...]
        for g in range(SSD_G):
            bb = x_ref[:, width + g * SSD_N: width + (g + 1) * SSD_N].astype(bf16)
            cb_ = x_ref[:, width + gn + g * SSD_N: width + gn + (g + 1) * SSD_N].astype(bf16)
            cbm = lax.dot_general(cb_, bb, NT, preferred_element_type=f32)
            for e in range(g * epg, (g + 1) * epg):
                xe = x_ref[:, e * SSD_P:(e + 1) * SSD_P]
                dtc = dtv[:, e:e + 1]
                cc = cum[:, e:e + 1]
                clast = cum[CHUNK - 1:CHUNK, e:e + 1]
                seg = cc - cum_t[e:e + 1, :]
                lmat = jnp.exp(jnp.where(tri, seg, -jnp.inf))
                xdt = xe * dtc
                h = h_sc[e]
                hp_ref[0, e] = h
                y = _dot(cbm * lmat, xdt, NN)
                y = y + _dot(cb_, h, NT) * jnp.exp(cc)
                y_ref[:, e * SSD_P:(e + 1) * SSD_P] = y + xe * dsk[:, e:e + 1]
                st = _dot(xdt * jnp.exp(clast - cc), bb, TN)
                h_sc[e] = h * jnp.exp(clast) + st

    par = lambda a: pl.BlockSpec(a.shape, lambda i: (0,) * a.ndim)
    return pl.pallas_call(
        body, name="ssd_fwd",
        out_shape=(jax.ShapeDtypeStruct((t, width), f32), jax.ShapeDtypeStruct((nc, hs, SSD_P, SSD_N), f32)),
        grid=(nc,),
        in_specs=[pl.BlockSpec((CHUNK, cdim), lambda i: (i, 0)), pl.BlockSpec((CHUNK, hs), lambda i: (i, 0)),
                  pl.BlockSpec((hs, CHUNK), lambda i: (0, i)), par(dtb), par(dtb_t), par(alog), par(alog_t), par(dskip)],
        out_specs=(pl.BlockSpec((CHUNK, width), lambda i: (i, 0)),
                   pl.BlockSpec((1, hs, SSD_P, SSD_N), lambda i: (i, 0, 0, 0))),
        scratch_shapes=[pltpu.VMEM((hs, SSD_P, SSD_N), f32)],
        compiler_params=_params(("arbitrary",)),
    )(xbc, dt, dt_t, dtb, dtb_t, alog, alog_t, dskip)


def _ssd_bwd(xbc, dt, dt_t, dtb, dtb_t, alog, alog_t, dskip, hprev, dy, width):
    t, cdim = xbc.shape
    hs = dt.shape[1]
    nc = t // CHUNK
    epg = hs // SSD_G
    gn = SSD_G * SSD_N

    def body(x_ref, dt_ref, dtT_ref, dtb_ref, dtbT_ref, alog_ref, alogT_ref, d_ref, hp_ref, dy_ref,
             dx_ref, ddt_ref, dalog_ref, ddsk_ref, ddtb_ref, dh_sc):
        @pl.when(pl.program_id(0) == 0)
        def _():
            dh_sc[...] = jnp.zeros_like(dh_sc)
            dalog_ref[...] = jnp.zeros_like(dalog_ref)
            ddsk_ref[...] = jnp.zeros_like(ddsk_ref)
            ddtb_ref[...] = jnp.zeros_like(ddtb_ref)

        tri, raw, dtv, a_neg, cum, cum_t = _ssd_common(dt_ref, dtT_ref, dtb_ref, dtbT_ref, alog_ref, alogT_ref)
        dsk = d_ref[...]
        head_iota = lax.broadcasted_iota(jnp.int32, (1, hs), 1)
        last_row = (lax.broadcasted_iota(jnp.int32, (CHUNK, 1), 0) == CHUNK - 1).astype(f32)
        dcum_all = jnp.zeros((CHUNK, hs), f32)
        ddtx_all = jnp.zeros((CHUNK, hs), f32)
        ddsk = jnp.zeros((1, hs), f32)
        for g in range(SSD_G):
            bsl = slice(width + g * SSD_N, width + (g + 1) * SSD_N)
            csl = slice(width + gn + g * SSD_N, width + gn + (g + 1) * SSD_N)
            bb = x_ref[:, bsl].astype(bf16)
            cb_ = x_ref[:, csl].astype(bf16)
            cbm = lax.dot_general(cb_, bb, NT, preferred_element_type=f32)
            dbg = jnp.zeros((CHUNK, SSD_N), f32)
            dcg = jnp.zeros((CHUNK, SSD_N), f32)
            for e in range(g * epg, (g + 1) * epg):
                onehot = (head_iota == e).astype(f32)
                esl = slice(e * SSD_P, (e + 1) * SSD_P)
                xe = x_ref[:, esl]
                dye = dy_ref[:, esl]
                dtc = dtv[:, e:e + 1]
                cc = cum[:, e:e + 1]
                clast = cum[CHUNK - 1:CHUNK, e:e + 1]
                lmat = jnp.exp(jnp.where(tri, cc - cum_t[e:e + 1, :], -jnp.inf))
                gmat = cbm * lmat
                xdt = xe * dtc
                ee = jnp.exp(cc)
                ff = jnp.exp(clast - cc)
                dec = jnp.exp(clast)
                h = hp_ref[0, e]
                dh = dh_sc[e]
                ch = _dot(cb_, h, NT)
                dye_e = dye * ee
                dcum = jnp.sum(dye * ch, axis=1, keepdims=True) * ee
                dcg = dcg + _dot(dye_e, h, NN)
                dh_new = _dot(dye_e, cb_, TN)
                dbg = dbg + _dot(xdt * ff, dh, NN)
                bds = _dot(bb, dh, NT)
                dxdt = bds * ff
                dff = jnp.sum(bds * xdt, axis=1, keepdims=True) * ff
                dcum = dcum - dff
                dclast = jnp.sum(dff, axis=0, keepdims=True) + dec * jnp.sum(
                    jnp.sum(dh * h, axis=1, keepdims=True), axis=0, keepdims=True)
                dgm = _dot(dye, xdt, NT)
                dxdt = dxdt + _dot(gmat, dye, TN)
                dcb = dgm * lmat
                dcg = dcg + _dot(dcb, bb, NN)
                dbg = dbg + _dot(dcb, cb_, TN)
                mm = dgm * gmat
                dcum = dcum + jnp.sum(mm, axis=1, keepdims=True) - jnp.sum(mm.T, axis=1, keepdims=True)
                dcum = dcum + dclast * last_row
                dx_ref[:, esl] = dxdt * dtc + dye * dsk[:, e:e + 1]
                ddtx_all = ddtx_all + jnp.sum(dxdt * xe, axis=1, keepdims=True) * onehot
                dcum_all = dcum_all + dcum * onehot
                ddsk = ddsk + jnp.sum(jnp.sum(dye * xe, axis=1, keepdims=True), axis=0, keepdims=True) * onehot
                dh_sc[e] = dh_new + dec * dh
            dx_ref[:, bsl] = dbg
            dx_ref[:, csl] = dcg
        ii = lax.broadcasted_iota(jnp.int32, (CHUNK, CHUNK), 0)
        jj = lax.broadcasted_iota(jnp.int32, (CHUNK, CHUNK), 1)
        da = _ones_dot_left((jj >= ii).astype(bf16), dcum_all)
        ddt = da * a_neg + ddtx_all
        dalog_ref[...] += jnp.sum(da * dtv, axis=0, keepdims=True) * a_neg
        draw = ddt * _sigmoid(raw)
        ddt_ref[...] = draw
        ddtb_ref[...] += jnp.sum(draw, axis=0, keepdims=True)
        ddsk_ref[...] += ddsk

    rev = lambda i: nc - 1 - i
    par = lambda a: pl.BlockSpec(a.shape, lambda i: (0,) * a.ndim)
    acc = pl.BlockSpec((1, hs), lambda i: (0, 0))
    acc_shape = jax.ShapeDtypeStruct((1, hs), f32)
    return pl.pallas_call(
        body, name="ssd_bwd",
        out_shape=(jax.ShapeDtypeStruct((t, cdim), f32), jax.ShapeDtypeStruct((t, hs), f32),
                   acc_shape, acc_shape, acc_shape),
        grid=(nc,),
        in_specs=[pl.BlockSpec((CHUNK, cdim), lambda i: (rev(i), 0)), pl.BlockSpec((CHUNK, hs), lambda i: (rev(i), 0)),
                  pl.BlockSpec((hs, CHUNK), lambda i: (0, rev(i))), par(dtb), par(dtb_t), par(alog), par(alog_t),
                  par(dskip), pl.BlockSpec((1, hs, SSD_P, SSD_N), lambda i: (rev(i), 0, 0, 0)),
                  pl.BlockSpec((CHUNK, width), lambda i: (rev(i), 0))],
        out_specs=(pl.BlockSpec((CHUNK, cdim), lambda i: (rev(i), 0)), pl.BlockSpec((CHUNK, hs), lambda i: (rev(i), 0)),
                   acc, acc, acc),
        scratch_shapes=[pltpu.VMEM((hs, SSD_P, SSD_N), f32)],
        compiler_params=_params(("arbitrary",)),
    )(xbc, dt, dt_t, dtb, dtb_t, alog, alog_t, dskip, hprev, dy)


def _where_am_i():
    x, y, c = lax.axis_index("x"), lax.axis_index("y"), lax.axis_index("c")
    chips = [(1 - x, y), (x, 1 - y), (1 - x, 1 - y)]
    return x, y, c, chips


def _remote(src, dst, send_sems, recv_sems, k, to):
    return pltpu.make_async_remote_copy(src_ref=src, dst_ref=dst, send_sem=send_sems.at[k], recv_sem=recv_sems.at[k],
                                        device_id=to, device_id_type=MESH)


def _comm_call(body, out_shapes, nsem, name, args):
    return pl.pallas_call(
        body, name=name, out_shape=list(out_shapes), in_specs=[ANY] * len(args), out_specs=[ANY] * len(out_shapes),
        scratch_shapes=[pltpu.SemaphoreType.DMA((nsem,)), pltpu.SemaphoreType.DMA((nsem,))],
    )(*args)


def _half(ref, c, r2):
    return ref.at[pl.ds(c * r2, r2)]


def _gather_weights(shards, wholes):
    ns, nw = len(shards), len(wholes)
    per = 7

    def body(*refs):
        srcs, outs = refs[:ns + nw], refs[ns + nw:2 * (ns + nw)]
        send_sems, recv_sems = refs[2 * (ns + nw):]
        x, y, c, chips = _where_am_i()
        me, sib = 2 * x + y, (x, y, 1 - c)
        started = []

        def go(cp):
            cp.start()
            started.append(cp)

        for w in range(ns + nw):
            src, out, base = srcs[w], outs[w], per * w
            halved = w < ns
            r2 = src.shape[0] // 2
            piece = _half(src, c, r2) if halved else src
            for j, (cx, cy) in enumerate(chips):
                dst = _half(out.at[me], c, r2) if halved else out.at[me]
                go(_remote(piece, dst, send_sems, recv_sems, base + j, (cx, cy, c)))
            go(_remote(src, out.at[me], send_sems, recv_sems, base + 6, sib))
        for w in range(ns + nw):
            src, out, base = srcs[w], outs[w], per * w
            halved = w < ns
            r2 = src.shape[0] // 2
            for j, (cx, cy) in enumerate(chips):
                got = _half(out.at[2 * cx + cy], c, r2) if halved else out.at[2 * cx + cy]
                _remote(got, got, send_sems, recv_sems, base + j, sib).wait_recv()
                if halved:
                    go(_remote(got, got, send_sems, recv_sems, base + 3 + j, sib))
        for w in range(ns + nw):
            src, out, base = srcs[w], outs[w], per * w
            r2 = src.shape[0] // 2
            if w < ns:
                for j, (cx, cy) in enumerate(chips):
                    got = _half(out.at[2 * cx + cy], 1 - c, r2)
                    _remote(got, got, send_sems, recv_sems, base + 3 + j, sib).wait_recv()
            _remote(src, out.at[me], send_sems, recv_sems, base + 6, sib).wait_recv()
        for cp in started:
            cp.wait_send()

    args = list(shards) + list(wholes)
    outs = [jax.ShapeDtypeStruct((4,) + a.shape, a.dtype) for a in args]
    return _comm_call(body, outs, per * len(args), "gather_weights", args)


def _sibling_send_halves(gs):
    n = len(gs)

    def body(*refs):
        srcs, outs, send_sems, recv_sems = refs[:n], refs[n:2 * n], refs[2 * n], refs[2 * n + 1]
        x, y, c, _ = _where_am_i()
        sib = (x, y, 1 - c)
        cps = []
        for w in range(n):
            r2 = srcs[w].shape[1] // 2
            for k in range(4):
                cps.append(_remote(_half(srcs[w].at[k], 1 - c, r2), outs[w].at[k], send_sems, recv_sems, 4 * w + k, sib))
        for cp in cps:
            cp.start()
        for cp in cps:
            cp.wait()

    outs = [jax.ShapeDtypeStruct((4, g.shape[1] // 2, g.shape[2]), g.dtype) for g in gs]
    return _comm_call(body, outs, 4 * n, "sibling_send_halves", list(gs))


def _chips_exchange(ps):
    n = len(ps)

    def body(*refs):
        srcs, outs, send_sems, recv_sems = refs[:n], refs[n:2 * n], refs[2 * n], refs[2 * n + 1]
        x, y, c, chips = _where_am_i()
        me = 2 * x + y
        cps = []
        for w in range(n):
            for j, (cx, cy) in enumerate(chips):
                cps.append(_remote(srcs[w].at[2 * cx + cy], outs[w].at[me], send_sems, recv_sems, 3 * w + j, (cx, cy, c)))
        for cp in cps:
            cp.start()
        for w in range(n):
            for j, (cx, cy) in enumerate(chips):
                got = outs[w].at[2 * cx + cy]
                _remote(got, got, send_sems, recv_sems, 3 * w + j, (cx, cy, c)).wait_recv()
        for cp in cps:
            cp.wait_send()

    outs = [jax.ShapeDtypeStruct(p.shape, p.dtype) for p in ps]
    return _comm_call(body, outs, 3 * n, "chips_exchange", list(ps))


def _sibling_swap(rs):
    n = len(rs)

    def body(*refs):
        srcs, outs, send_sems, recv_sems = refs[:n], refs[n:2 * n], refs[2 * n], refs[2 * n + 1]
        x, y, c, _ = _where_am_i()
        cps = [_remote(srcs[w], outs[w], send_sems, recv_sems, w, (x, y, 1 - c)) for w in range(n)]
        for cp in cps:
            cp.start()
        for cp in cps:
            cp.wait()

    outs = [jax.ShapeDtypeStruct(r.shape, r.dtype) for r in rs]
    return _comm_call(body, outs, n, "sibling_swap", list(rs))


def _gather_all(v):
    rows = v.shape[0]

    def body(x_ref, out_ref, send_sems, recv_sems, local_sem):
        x, y, c, chips = _where_am_i()
        me, sib = (x, y, c), (x, y, 1 - c)
        blk = lambda px, py, pc: out_ref.at[4 * px + 2 * py + pc]
        mine = pltpu.make_async_copy(x_ref, blk(*me), local_sem)
        mine.start()
        first = [_remote(x_ref, blk(*me), send_sems, recv_sems, 0, sib)]
        first += [_remote(x_ref, blk(*me), send_sems, recv_sems, 1 + j, (*chip, c)) for j, chip in enumerate(chips)]
        for cp in first:
            cp.start()
        passed = [_remote(blk(*chip, c), blk(*chip, c), send_sems, recv_sems, 4 + j, sib) for j, chip in enumerate(chips)]
        for j, chip in enumerate(chips):
            _remote(blk(*chip, c), blk(*chip, c), send_sems, recv_sems, 1 + j, me).wait_recv()
            passed[j].start()
        _remote(blk(*sib), blk(*sib), send_sems, recv_sems, 0, me).wait_recv()
        for j, chip in enumerate(chips):
            _remote(blk(*chip, 1 - c), blk(*chip, 1 - c), send_sems, recv_sems, 4 + j, me).wait_recv()
        for cp in first + passed:
            cp.wait_send()
        mine.wait()

    vm = pl.BlockSpec(memory_space=pltpu.VMEM)
    return pl.pallas_call(
        body, name="gather_all", out_shape=jax.ShapeDtypeStruct((8, rows, LANES), v.dtype),
        in_specs=[vm], out_specs=vm,
        scratch_shapes=[pltpu.SemaphoreType.DMA((7,)), pltpu.SemaphoreType.DMA((7,)), pltpu.SemaphoreType.DMA],
    )(v)


def _flat_pad(parts, total):
    v = jnp.concatenate([p.reshape(-1) for p in parts])
    return jnp.pad(v, (0, total - v.shape[0]))


def _adamw(w, g, m, v):
    m = ADAM_B1 * m + (1.0 - ADAM_B1) * g
    v = ADAM_B2 * v + (1.0 - ADAM_B2) * jnp.square(g)
    m_hat = m / (1.0 - ADAM_B1 ** ADAM_STEP)
    v_hat = v / (1.0 - ADAM_B2 ** ADAM_STEP)
    delta = -ADAM_LR * (m_hat / (jnp.sqrt(v_hat) + ADAM_EPS) + ADAM_WD * w)
    return delta, m, v


def _adamw_call(w, g, m, v, name):
    r, cdim = w.shape
    tm = _tile(r, ROW_TILE, 8)
    o = _rout(r, cdim, f32, tm)
    return _rowwise(_adamw, r, tm, [_rows(w, tm), _rows(g, tm), _rows(m, tm), _rows(v, tm)], [o, o, o], [], name)


def _adamw_halves(w, m, v, g_mine, g_sib, sp, name):
    r, cdim = w.shape
    r2 = r // 2
    tr = _tile(r2, ROW_TILE, 8)
    nb = r2 // tr

    def body(sp_ref, w_ref, m_ref, v_ref, ga_ref, gb_ref, g_out, d_out, m_out, v_out):
        g = jnp.where(pl.program_id(0) == sp_ref[1], ga_ref[...], gb_ref[...])
        delta, mn, vn = _adamw(w_ref[...], g, m_ref[...], v_ref[...])
        g_out[...] = g
        d_out[...] = delta
        m_out[...] = mn
        v_out[...] = vn

    full = pl.BlockSpec((tr, cdim), lambda h, i, s: (h * nb + i, 0))
    mine = pl.BlockSpec((tr, cdim), lambda h, i, s: (jnp.where(h == s[1], i, 0), 0))
    sib = pl.BlockSpec((tr, cdim), lambda h, i, s: (jnp.where(h == s[1], 0, i), 0))
    out = jax.ShapeDtypeStruct((r, cdim), f32)
    gs = pltpu.PrefetchScalarGridSpec(num_scalar_prefetch=1, grid=(2, nb), in_specs=[full, full, full, mine, sib],
                                      out_specs=[full, full, full, full])
    return pl.pallas_call(body, name=name, out_shape=[out, out, out, out], grid_spec=gs,
                          compiler_params=_params(("parallel", "parallel")))(sp, w, m, v, g_mine, g_sib)


BIG = ("w_in", "w_uq", "w_ukv", "w_out", "w_gate", "w_up", "w_down")
COL_SHARDED = ("w_in", "w_uq", "w_ukv", "w_gate", "w_up")
SMALL = ("q_norm_w", "kv_norm_w", "conv_w", "conv_b", "dt_bias", "a_log", "d_skip", "ssd_norm_w", "attn_out_norm_w",
         "pre_mix_norm_w", "post_mix_norm_w", "pre_ffn_norm_w", "post_ffn_norm_w")
ORDER = ("w_in", "q_norm_w", "w_uq", "kv_norm_w", "w_ukv", "conv_w", "conv_b", "dt_bias", "a_log", "d_skip",
         "ssd_norm_w", "attn_out_norm_w", "w_out", "pre_mix_norm_w", "post_mix_norm_w", "pre_ffn_norm_w",
         "post_ffn_norm_w", "w_gate", "w_up", "w_down")


def kernel(x, positions, w_in, q_norm_w, w_uq, kv_norm_w, w_ukv, conv_w, conv_b, dt_bias, a_log, d_skip, ssd_norm_w, attn_out_norm_w, w_out, pre_mix_norm_w, post_mix_norm_w, pre_ffn_norm_w, post_ffn_norm_w, w_gate, w_up, w_down, loss_target, m_w_in, m_q_norm_w, m_w_uq, m_kv_norm_w, m_w_ukv, m_conv_w, m_conv_b, m_dt_bias, m_a_log, m_d_skip, m_ssd_norm_w, m_attn_out_norm_w, m_w_out, m_pre_mix_norm_w, m_post_mix_norm_w, m_pre_ffn_norm_w, m_post_ffn_norm_w, m_w_gate, m_w_up, m_w_down, v_w_in, v_q_norm_w, v_w_uq, v_kv_norm_w, v_w_ukv, v_conv_w, v_conv_b, v_dt_bias, v_a_log, v_d_skip, v_ssd_norm_w, v_attn_out_norm_w, v_w_out, v_pre_mix_norm_w, v_post_mix_norm_w, v_pre_ffn_norm_w, v_post_ffn_norm_w, v_w_gate, v_w_up, v_w_down):
    local = dict(locals())
    wts = {n: local[n][0] for n in ORDER}
    mom_m = {n: local["m_" + n][0] for n in ORDER}
    mom_v = {n: local["v_" + n][0] for n in ORDER}
    xs = x[0]
    tgt = loss_target[0]
    t, d = xs.shape
    nchip = 4
    my_x, my_y, my_c = lax.axis_index("x"), lax.axis_index("y"), lax.axis_index("c")
    my_chip = 2 * my_x + my_y

    mla_w = d // 2
    nh = mla_w // V_HEAD
    width = d - mla_w
    hs = width // SSD_P
    gn = SSD_G * SSD_N
    cdim = width + 2 * gn
    in_sizes = (Q_RANK, KV_RANK, ROPE, width, cdim, hs)
    d_in = sum(in_sizes)
    tail = LANES
    off_xbc = 0
    off_cq = cdim
    off_ckv = off_cq + Q_RANK
    off_z = off_ckv + KV_RANK
    off_tail = off_z + width
    d_in_p = _round_up(off_tail + tail, 256)
    gw = width // SSD_G
    assert off_cq % Q_RANK == 0 and off_ckv % KV_RANK == 0 and off_z % gw == 0 and off_tail % LANES == 0
    qk_head = NOPE + ROPE
    scale = qk_head ** -0.5
    tm = _tile(t, ROW_TILE, 8)
    tmw = _tile(t, ROW_TILE // 2, 8)

    ck, ccs = wts["conv_w"].shape
    gathered = _gather_weights([wts[n].astype(bf16) for n in BIG], [wts["conv_w"]])
    gat = dict(zip(BIG, gathered[:len(BIG)]))
    cat_cols = lambda g: jnp.concatenate([g[k] for k in range(nchip)], axis=1)
    conv_full = cat_cols(gathered[len(BIG)])

    wi = cat_cols(gat["w_in"])
    o = np.cumsum((0,) + in_sizes)
    seg = lambda i: wi[:, o[i]:o[i + 1]]
    w_in_p = jnp.concatenate([seg(4), seg(0), seg(1), seg(3), seg(2), seg(5),
                              jnp.zeros((d, d_in_p - off_tail - ROPE - hs), bf16)], axis=1)
    w_uq_p = jnp.pad(cat_cols(gat["w_uq"]).reshape(Q_RANK, nh, qk_head), ((0, 0), (0, 0), (0, QK_PAD - qk_head))
                     ).reshape(Q_RANK, nh * QK_PAD)
    w_ukv_f, w_gate_f, w_up_f = gat["w_ukv"], gat["w_gate"], gat["w_up"]
    w_out_f = gat["w_out"].reshape(-1, gat["w_out"].shape[2])
    w_down_f = gat["w_down"].reshape(-1, gat["w_down"].shape[2])

    inv_freq = ROPE_THETA ** (-jnp.arange(0, ROPE, 2, dtype=f32) / ROPE)
    ang = positions[0].astype(f32)[:, None] * inv_freq
    cos, sin = jnp.cos(ang), jnp.sin(ang)
    z32, z64, z96 = jnp.zeros((t, 32), f32), jnp.zeros((t, 64), f32), jnp.zeros((t, 96), f32)
    cosp = jnp.concatenate([cos, cos, z64], axis=1)
    sina = jnp.concatenate([-sin, z96], axis=1)
    sinb = jnp.concatenate([z32, sin, z64], axis=1)

    row = lambda a: a.reshape(1, -1)
    w_pre_mix, w_post_mix = row(wts["pre_mix_norm_w"]), row(wts["post_mix_norm_w"])
    w_pre_ffn, w_post_ffn = row(wts["pre_ffn_norm_w"]), row(wts["post_ffn_norm_w"])
    w_qn, w_kvn = row(wts["q_norm_w"]), row(wts["kv_norm_w"])
    w_attn_n, w_ssd_n = row(wts["attn_out_norm_w"]), row(wts["ssd_norm_w"])
    conv_b_r = row(wts["conv_b"])
    dtb, alog, dskip = row(wts["dt_bias"]), row(wts["a_log"]), row(wts["d_skip"])
    dtb_t, alog_t = dtb.reshape(hs, 1), alog.reshape(hs, 1)

    u = _rowwise(lambda a, w: _rms(a, w), t, tm, [_rows(xs, tm), _par(w_pre_mix)], [_rout(t, d, bf16, tm)], [],
                 "pre_mix_norm")[0]
    proj = _matmul([(u, w_in_p)], "nn", f32, "in_proj")
    cq_in = _rows(proj, tm, Q_RANK, off_cq // Q_RANK)
    ckv_in = _rows(proj, tm, KV_RANK, off_ckv // KV_RANK)
    cqn = _rowwise(lambda a, w: _rms(a, w), t, tm, [cq_in, _par(w_qn)], [_rout(t, Q_RANK, bf16, tm)], [], "q_norm")[0]
    ckvn = _rowwise(lambda a, w: _rms(a, w), t, tm, [ckv_in, _par(w_kvn)], [_rout(t, KV_RANK, bf16, tm)], [],
                    "kv_norm")[0]
    q_raw = _matmul([(cqn, w_uq_p)], "nn", f32, "q_up")
    kv = _matmul([(ckvn, w_ukv_f)], "nn", bf16, "kv_up")

    def q_rope_fn(qt, cp, sa, sb):
        parts = []
        for h in range(nh):
            parts.append(qt[:, h * QK_PAD: h * QK_PAD + NOPE])
            parts.append(_rope(qt[:, h * QK_PAD + NOPE:(h + 1) * QK_PAD], cp, sa, sb))
        return jnp.concatenate(parts, axis=1)

    tail_cb = off_tail // LANES
    q2 = _rowwise(q_rope_fn, t, tm, [_rows(q_raw, tm), _rows(cosp, tm), _rows(sina, tm), _rows(sinb, tm)],
                  [_rout(t, nh * QK_PAD, bf16, tm)], [], "q_rope")[0]
    kr2 = _rowwise(_rope, t, tm, [_rows(proj, tm, LANES, tail_cb), _rows(cosp, tm), _rows(sina, tm), _rows(sinb, tm)],
                   [_rout(t, LANES, bf16, tm)], [], "k_rope")[0]
    o_att, lse = _flash_fwd(q2, kv, kr2, nh, scale)

    xbc_act = _conv_fwd(proj, conv_full, conv_b_r, cdim, tm)
    dt_raw = proj[:, off_tail + ROPE: off_tail + ROPE + hs]
    dt_raw_t = dt_raw.T
    y_ssd, hprev = _ssd_fwd(xbc_act, dt_raw, dt_raw_t, dtb, dtb_t, alog, alog_t, dskip, width)
    z_ins = [_rows(proj, tm, gw, off_z // gw + i) for i in range(SSD_G)]

    def mix_norms_fn(ov, yv, *rest):
        zs, wa, ws = rest[:SSD_G], rest[SSD_G], rest[SSD_G + 1]
        outs = [_rms(ov, wa)]
        for i in range(SSD_G):
            sl = slice(i * gw, (i + 1) * gw)
            outs.append(_rms(yv[:, sl] * (zs[i] * _sigmoid(zs[i])), ws[:, sl]))
        return jnp.concatenate(outs, axis=1)

    cat = _rowwise(mix_norms_fn, t, tm, [_rows(o_att, tm), _rows(y_ssd, tm)] + z_ins + [_par(w_attn_n), _par(w_ssd_n)],
                   [_rout(t, d, bf16, tm)], [], "attn_ssd_out_norms")[0]
    mix = _matmul([(cat, w_out_f)], "nn", f32, "out_proj")

    def post_mix_fn(mx, xv, w1, w2):
        h1v = xv + _rms(mx, w1)
        return h1v, _rms(h1v, w2)

    h1, v_in = _rowwise(post_mix_fn, t, tmw, [_rows(mix, tmw), _rows(xs, tmw), _par(w_post_mix), _par(w_pre_ffn)],
                        [_rout(t, d, f32, tmw), _rout(t, d, bf16, tmw)], [], "post_mix_pre_ffn_norm")
    g_ff, u_ff, act = _ffn_up(v_in, w_gate_f, w_up_f)
    ffn = _matmul([(act, w_down_f)], "nn", f32, "ffn_down")

    def final_fn(fv, h1v, tg, w):
        h2 = h1v + _rms(fv, w)
        err = h2 - tg
        lpart = 0.5 * jnp.sum(jnp.sum(err * err, axis=1, keepdims=True), axis=0, keepdims=True) / d
        dh2 = err / d
        dff, dw = _rms_bwd(fv, w, dh2)
        return dff, dh2, jnp.broadcast_to(lpart, (1, LANES)), dw

    dffn, dh2, loss_acc, g_post_ffn = _rowwise(
        final_fn, t, tmw, [_rows(ffn, tmw), _rows(h1, tmw), _rows(tgt, tmw), _par(w_post_ffn)],
        [_rout(t, d, bf16, tmw), _rout(t, d, f32, tmw)], [(1, LANES), (1, d)], "loss_post_ffn_norm_bwd")
    loss = lax.psum(loss_acc[0, 0], ("x", "y", "c"))

    dg_ff, du_ff = _ffn_down_bwd(dffn, w_down_f, g_ff, u_ff)
    gw_down = _matmul([(act, dffn)], "tn", f32, "grad_w_down")
    gw_gate = _matmul([(v_in, dg_ff)], "tn", f32, "grad_w_gate", out_chunks=nchip)
    gw_up = _matmul([(v_in, du_ff)], "tn", f32, "grad_w_up", out_chunks=nchip)
    dv_in = _matmul([(dg_ff, w_gate_f), (du_ff, w_up_f)], "nt", f32, "ffn_up_bwd")

    def mid_bwd_fn(h1v, dvv, dh2v, mx, w_pf, w_pm):
        dxn, dw_pf = _rms_bwd(h1v, w_pf, dvv)
        dh1v = dh2v + dxn
        dmx, dw_pm = _rms_bwd(mx, w_pm, dh1v)
        return dh1v, dmx, dw_pf, dw_pm

    dh1, dmix, g_pre_ffn, g_post_mix = _rowwise(
        mid_bwd_fn, t, tmw, [_rows(h1, tmw), _rows(dv_in, tmw), _rows(dh2, tmw), _rows(mix, tmw),
                             _par(w_pre_ffn), _par(w_post_mix)],
        [_rout(t, d, f32, tmw), _rout(t, d, bf16, tmw)], [(1, d), (1, d)], "pre_ffn_post_mix_norm_bwd")
    dcat = _matmul([(dmix, w_out_f)], "nt", f32, "out_proj_bwd")
    gw_out = _matmul([(cat, dmix)], "tn", f32, "grad_w_out")

    def mix_norms_bwd_fn(ov, yv, *rest):
        zs, dcv, wa, ws = rest[:SSD_G], rest[SSD_G], rest[SSD_G + 1], rest[SSD_G + 2]
        dov, dwa = _rms_bwd(ov, wa, dcv[:, :mla_w])
        dl = [jnp.broadcast_to(jnp.sum(dov[:, h * V_HEAD:(h + 1) * V_HEAD] * ov[:, h * V_HEAD:(h + 1) * V_HEAD],
                                       axis=1, keepdims=True), (ov.shape[0], V_HEAD)) for h in range(nh)]
        dys, dzs, dws = [], [], []
        for i in range(SSD_G):
            sl = slice(i * gw, (i + 1) * gw)
            zv, yi = zs[i], yv[:, sl]
            sg = _sigmoid(zv)
            sz = zv * sg
            dgi, dwi = _rms_bwd(yi * sz, ws[:, sl], dcv[:, mla_w + i * gw: mla_w + (i + 1) * gw])
            dys.append(dgi * sz)
            dzs.append(dgi * yi * (sg * (1.0 + zv * (1.0 - sg))))
            dws.append(dwi)
        cc = lambda a: jnp.concatenate(a, axis=1)
        return dov, cc(dl), cc(dys), cc(dzs), dwa, cc(dws)

    do_att, delta, dy_ssd, dz, g_attn_n, g_ssd_n = _rowwise(
        mix_norms_bwd_fn, t, tm,
        [_rows(o_att, tm), _rows(y_ssd, tm)] + z_ins + [_rows(dcat, tm), _par(w_attn_n), _par(w_ssd_n)],
        [_rout(t, mla_w, bf16, tm), _rout(t, mla_w, f32, tm), _rout(t, width, f32, tm), _rout(t, width, bf16, tm)],
        [(1, mla_w), (1, width)], "attn_ssd_out_norms_bwd")
    dkv, dkr_h = _flash_bwd_dkv(q2, kv, kr2, do_att, lse, delta, nh, scale)
    dq2 = _flash_bwd_dq(q2, kv, kr2, do_att, lse, delta, nh, scale)

    def q_rope_bwd_fn(dqt, cp, sa, sb):
        parts = []
        for h in range(nh):
            parts.append(dqt[:, h * QK_PAD: h * QK_PAD + NOPE])
            parts.append(_rope_bwd(dqt[:, h * QK_PAD + NOPE:(h + 1) * QK_PAD], cp, sa, sb))
        return jnp.concatenate(parts, axis=1)

    dq_raw = _rowwise(q_rope_bwd_fn, t, tm, [_rows(dq2, tm), _rows(cosp, tm), _rows(sina, tm), _rows(sinb, tm)],
                      [_rout(t, nh * QK_PAD, bf16, tm)], [], "q_rope_bwd")[0]

    def k_rope_bwd_fn(dk, cp, sa, sb):
        tot = dk[:, 0:LANES]
        for h in range(1, nh):
            tot = tot + dk[:, h * LANES:(h + 1) * LANES]
        return _rope_bwd(tot, cp, sa, sb)

    dkr = _rowwise(k_rope_bwd_fn, t, tm, [_rows(dkr_h, tm), _rows(cosp, tm), _rows(sina, tm), _rows(sinb, tm)],
                   [_rout(t, LANES, f32, tm)], [], "k_rope_bwd")[0]
    gw_uq_p = _matmul([(cqn, dq_raw)], "tn", f32, "grad_w_uq")
    gw_ukv = _matmul([(ckvn, dkv)], "tn", f32, "grad_w_ukv", out_chunks=nchip)
    dcqn = _matmul([(dq_raw, w_uq_p)], "nt", f32, "q_up_bwd")
    dckvn = _matmul([(dkv, w_ukv_f)], "nt", f32, "kv_up_bwd")

    def lat_norm_bwd_fn(a, w, dyv):
        return _rms_bwd(a, w, dyv)

    dcq, g_qn = _rowwise(lat_norm_bwd_fn, t, tm, [cq_in, _par(w_qn), _rows(dcqn, tm)],
                         [_rout(t, Q_RANK, bf16, tm)], [(1, Q_RANK)], "q_norm_bwd")
    dckv, g_kvn = _rowwise(lat_norm_bwd_fn, t, tm, [ckv_in, _par(w_kvn), _rows(dckvn, tm)],
                           [_rout(t, KV_RANK, bf16, tm)], [(1, KV_RANK)], "kv_norm_bwd")

    dxbc_act, ddt_raw, g_alog, g_dskip, g_dtb = _ssd_bwd(
        xbc_act, dt_raw, dt_raw_t, dtb, dtb_t, alog, alog_t, dskip, hprev, dy_ssd, width)
    dpre, gcw0, gcw1, gcw2, gcw3, g_conv_b = _conv_bwd_pre(proj, conv_full, conv_b_r, dxbc_act, cdim, tm)
    g_conv_w = jnp.concatenate([gcw0, gcw1, gcw2, gcw3], axis=0)
    dxbc = _conv_bwd_dx(dpre, conv_full, tm)

    dtail = jnp.concatenate([dkr[:, :ROPE], ddt_raw, jnp.zeros((t, d_in_p - off_tail - ROPE - hs), f32)],
                            axis=1).astype(bf16)
    dproj = jnp.concatenate([dxbc, dcq, dckv, dz, dtail], axis=1)
    du_in = _matmul([(dproj, w_in_p)], "nt", f32, "in_proj_bwd")
    gw_in_p = _matmul([(u, dproj)], "tn", f32, "grad_w_in")

    def first_bwd_fn(xv, duv, dh1v, w):
        dxn, dw = _rms_bwd(xv, w, duv)
        return dh1v + dxn, dw

    grad_x, g_pre_mix = _rowwise(first_bwd_fn, t, tmw, [_rows(xs, tmw), _rows(du_in, tmw), _rows(dh1, tmw),
                                                         _par(w_pre_mix)],
                                 [_rout(t, d, f32, tmw)], [(1, d)], "pre_mix_norm_bwd")

    gseg = lambda a, b: gw_in_p[:, a:b]
    gw_in = jnp.concatenate([gseg(off_cq, off_ckv), gseg(off_ckv, off_z), gseg(off_tail, off_tail + ROPE),
                             gseg(off_z, off_tail), gseg(off_xbc, off_cq),
                             gseg(off_tail + ROPE, off_tail + ROPE + hs)], axis=1)
    gw_uq = gw_uq_p.reshape(Q_RANK, nh, QK_PAD)[:, :, :qk_head].reshape(Q_RANK, nh * qk_head)
    split_cols = lambda gf: jnp.stack(jnp.split(gf, nchip, axis=1))
    split_rows = lambda gf: gf.reshape(nchip, gf.shape[0] // nchip, gf.shape[1])
    gch = {"w_in": split_cols(gw_in), "w_uq": split_cols(gw_uq), "w_ukv": gw_ukv, "w_out": split_rows(gw_out),
           "w_gate": gw_gate, "w_up": gw_up, "w_down": split_rows(gw_down)}
    glist = [gch[n] for n in BIG]
    from_sib = _sibling_send_halves(glist)
    sp = jnp.stack([my_chip, my_c]).astype(jnp.int32)
    pair_sums = []
    for n, g, fs in zip(BIG, glist, from_sib):
        _, r2, cs = fs.shape
        tr = _tile(r2, ROW_TILE, 16)
        nb = r2 // tr
        pair_sums.append(_blocked(
            lambda a, b: a + b, (nchip, nb),
            [(g, (None, tr, cs), lambda k, i, s, nb=nb: (k, s[1] * nb + i, 0)),
             (fs, (None, tr, cs), lambda k, i, s: (k, i, 0))],
            [((nchip, r2, cs), bf16, (None, tr, cs), lambda k, i, s: (k, i, 0))], [], "pair_sum_" + n, sp=sp)[0])
    from_chips = _chips_exchange(pair_sums)
    halves = []
    for n, g, fs, fc in zip(BIG, glist, from_sib, from_chips):
        _, r2, cs = fs.shape
        tr = _tile(r2, ROW_TILE, 16)
        nb = r2 // tr
        halves.append(_blocked(
            lambda a, b, r1, r2_, r3: ((a + b) + r1.astype(f32)) + r2_.astype(f32) + r3.astype(f32), (nb,),
            [(g, (None, tr, cs), lambda i, s, nb=nb: (s[0], s[1] * nb + i, 0)),
             (fs, (None, tr, cs), lambda i, s: (s[0], i, 0)),
             (fc, (None, tr, cs), lambda i, s: (s[0] ^ 1, i, 0)),
             (fc, (None, tr, cs), lambda i, s: (s[0] ^ 2, i, 0)),
             (fc, (None, tr, cs), lambda i, s: (s[0] ^ 3, i, 0))],
            [((r2, cs), f32, (tr, cs), lambda i, s: (i, 0))], [], "chip_sum_" + n, sp=sp)[0])
    sib_halves = _sibling_swap(halves)
    gshard = {}

    gsmall = {"q_norm_w": g_qn, "kv_norm_w": g_kvn, "conv_w": g_conv_w, "conv_b": g_conv_b, "dt_bias": g_dtb,
              "a_log": g_alog, "d_skip": g_dskip, "ssd_norm_w": g_ssd_n, "attn_out_norm_w": g_attn_n,
              "pre_mix_norm_w": g_pre_mix, "post_mix_norm_w": g_post_mix, "pre_ffn_norm_w": g_pre_ffn,
              "post_ffn_norm_w": g_post_ffn}
    small_sizes = [int(np.prod(gsmall[n].shape)) for n in SMALL]
    srows = _round_up(-(-sum(small_sizes) // LANES), 8)
    spart = _flat_pad([gsmall[n] for n in SMALL], srows * LANES).reshape(srows, LANES)
    sall = _gather_all(spart)

    def sum8_fn(a):
        tot = a[0]
        for k in range(1, 8):
            tot = tot + a[k]
        return tot

    ssum = _blocked(sum8_fn, (1,), [(sall, sall.shape, lambda i: (0, 0, 0))],
                    [((srows, LANES), f32, (srows, LANES), lambda i: (0, 0))], [], "small_grad_sum")[0].reshape(-1)
    gred = {}
    off = 0
    for n, sz in zip(SMALL, small_sizes):
        gred[n] = ssum[off:off + sz].reshape(gsmall[n].shape)
        off += sz
    gshard["conv_w"] = lax.dynamic_slice_in_dim(gred["conv_w"], my_chip * ccs, ccs, axis=1)
    for n in SMALL:
        if n != "conv_w":
            gshard[n] = gred[n].reshape(wts[n].shape)

    delta, new_m, new_v = {}, {}, {}
    for n, mine_h, sib_h in zip(BIG, halves, sib_halves):
        gshard[n], delta[n], new_m[n], new_v[n] = _adamw_halves(wts[n], mom_m[n], mom_v[n], mine_h, sib_h, sp,
                                                                "adamw_" + n)
    pack = lambda src: _flat_pad([src[n] for n in SMALL], srows * LANES).reshape(srows, LANES)
    sd, sm, sv = _adamw_call(pack(wts), pack(gshard), pack(mom_m), pack(mom_v), "adamw_small")
    off = 0
    for n in SMALL:
        sz = int(np.prod(wts[n].shape))
        for dst, src in ((delta, sd), (new_m, sm), (new_v, sv)):
            dst[n] = src.reshape(-1)[off:off + sz].reshape(wts[n].shape)
        off += sz

    lead = lambda a: a[None]
    return (loss, grad_x[None], *[lead(gshard[n]) for n in ORDER], *[lead(delta[n]) for n in ORDER],
            *[lead(new_m[n]) for n in ORDER], *[lead(new_v[n]) for n in ORDER])
```

```python
import functools

import numpy as np
import jax
import jax.numpy as jnp
from jax import lax
from jax.experimental import pallas as pl
from jax.experimental.pallas import tpu as pltpu

f32, bf16 = jnp.float32, jnp.bfloat16

EPS = 1e-6
V_HEAD = 128
NOPE = 128
ROPE = 64
QK_PAD = 256
Q_RANK = 512
KV_RANK = 512
ROPE_THETA = 10000.0
SSD_P = 64
SSD_G = 2
SSD_N = 128
SSD_K = 4
CHUNK = 128
ADAM_LR, ADAM_B1, ADAM_B2, ADAM_EPS, ADAM_WD, ADAM_STEP = 0.001, 0.9, 0.999, 1e-08, 0.01, 10

VMEM_LIMIT_BYTES = 48 * 1024 * 1024
LANES = 128
ATT_TILE = 512
MM_TM, MM_TN, MM_TK = 1024, 1024, 1024
CHUNK_WHOLE_MAX = 1536
ROW_TILE = 256

NN = (((1,), (0,)), ((), ()))
NT = (((1,), (1,)), ((), ()))
TN = (((0,), (0,)), ((), ()))
MESH = pl.DeviceIdType.MESH
ANY = pl.BlockSpec(memory_space=pl.ANY)


def _tile(dim, cap, align=LANES):
    if dim <= cap:
        return dim
    t = (cap // align) * align
    while t >= align:
        if dim % t == 0:
            return t
        t -= align
    raise ValueError(f"no tile for {dim} under {cap}")


def _round_up(n, m):
    return -(-n // m) * m


def _params(sem):
    return pltpu.CompilerParams(dimension_semantics=sem, vmem_limit_bytes=VMEM_LIMIT_BYTES)


def _dot(a, b, dims):
    return lax.dot_general(a.astype(bf16), b.astype(bf16), dims, preferred_element_type=f32)


def _call(body, name, out_shape, grid, in_specs, out_specs, scratch_shapes, sem, args, comm=None, at=None):
    if comm is None:
        res = pl.pallas_call(body, name=name, out_shape=list(out_shape), grid=grid, in_specs=list(in_specs),
                             out_specs=list(out_specs), scratch_shapes=list(scratch_shapes),
                             compiler_params=_params(sem))(*args)
        return list(res), []
    n_in, n_out, n_sc = len(args), len(out_shape), len(scratch_shapes)
    na, no = len(comm.args), len(comm.outs)
    steps = int(np.prod(grid))

    def full(*refs):
        cin = refs[n_in:n_in + na]
        o0 = n_in + na
        cout = refs[o0 + n_out:o0 + n_out + no]
        s0 = o0 + n_out + no
        send_sems, recv_sems = refs[s0 + n_sc], refs[s0 + n_sc + 1]
        lin = pl.program_id(0)
        for dim in range(1, len(grid)):
            lin = lin * grid[dim] + pl.program_id(dim)
        for p in range(comm.nphase - 1):
            @pl.when(lin == int(round(at[p] * (steps - 1))))
            def _(p=p):
                comm.run(p, cin, cout, send_sems, recv_sems)
        body(*refs[:n_in], *refs[o0:o0 + n_out], *refs[s0:s0 + n_sc])

        @pl.when(lin == steps - 1)
        def _():
            comm.run(comm.nphase - 1, cin, cout, send_sems, recv_sems)

    res = pl.pallas_call(
        full, name=name, out_shape=list(out_shape) + comm.outs, grid=grid,
        in_specs=list(in_specs) + [ANY] * na, out_specs=list(out_specs) + [ANY] * no,
        scratch_shapes=list(scratch_shapes) + comm.sems(),
        compiler_params=_params(("arbitrary",) * len(grid)))(*args, *comm.args)
    return list(res[:n_out]), list(res[n_out:])


def _chunk_tile(cs, cap):
    return cs if cs <= CHUNK_WHOLE_MAX else _tile(cs, cap)


def _matmul(pairs, mode, out_dtype, name, out_chunks=None, comm=None, at=None):
    a0, b0 = pairs[0]
    chunked = b0.ndim == 3
    cs = b0.shape[2] if chunked else None
    bcols = b0.shape[0] * b0.shape[2] if chunked else b0.shape[1]
    brows = b0.shape[1] if chunked else b0.shape[0]
    if mode == "nn":
        (m, k), n = a0.shape, bcols
    elif mode == "nt":
        (m, k), n = a0.shape, brows
    else:
        (k, m), n = a0.shape, bcols
    tm = _tile(m, MM_TM)
    if mode == "nt":
        tn = _tile(n, MM_TN)
        tk = _chunk_tile(cs, MM_TK) if chunked else _tile(k, MM_TK)
    else:
        tk = _tile(k, MM_TK)
        if chunked:
            tn = _chunk_tile(cs, MM_TN)
        elif out_chunks:
            tn = _chunk_tile(n // out_chunks, MM_TN)
        else:
            tn = _tile(n, MM_TN)
    nk = k // tk
    if mode == "nn":
        a_spec = pl.BlockSpec((tm, tk), lambda i, j, kk: (i, kk))
        if chunked:
            q = cs // tn
            b_spec = pl.BlockSpec((None, tk, tn), lambda i, j, kk: (j // q, kk, j % q))
        else:
            b_spec = pl.BlockSpec((tk, tn), lambda i, j, kk: (kk, j))
        dims = NN
    elif mode == "nt":
        a_spec = pl.BlockSpec((tm, tk), lambda i, j, kk: (i, kk))
        if chunked:
            q = cs // tk
            b_spec = pl.BlockSpec((None, tn, tk), lambda i, j, kk: (kk // q, j, kk % q))
        else:
            b_spec = pl.BlockSpec((tn, tk), lambda i, j, kk: (j, kk))
        dims = NT
    else:
        a_spec = pl.BlockSpec((tk, tm), lambda i, j, kk: (kk, i))
        b_spec = pl.BlockSpec((tk, tn), lambda i, j, kk: (kk, j))
        dims = TN
    if out_chunks:
        qo = (n // out_chunks) // tn
        out_shape = jax.ShapeDtypeStruct((out_chunks, m, n // out_chunks), out_dtype)
        o_spec = pl.BlockSpec((None, tm, tn), lambda i, j, kk: (j // qo, i, j % qo))
    else:
        out_shape = jax.ShapeDtypeStruct((m, n), out_dtype)
        o_spec = pl.BlockSpec((tm, tn), lambda i, j, kk: (i, j))
    npair = len(pairs)

    def body(*refs):
        o_ref, acc = refs[2 * npair], refs[2 * npair + 1]
        kk = pl.program_id(2)

        @pl.when(kk == 0)
        def _():
            acc[...] = jnp.zeros_like(acc)

        part = _dot(refs[0][...], refs[1][...], dims)
        for p in range(1, npair):
            part = part + _dot(refs[2 * p][...], refs[2 * p + 1][...], dims)
        acc[...] += part

        @pl.when(kk == nk - 1)
        def _():
            o_ref[...] = acc[...].astype(out_dtype)

    args = [t for pr in pairs for t in pr]
    res, cres = _call(body, name, [out_shape], (m // tm, n // tn, nk), [a_spec, b_spec] * npair, [o_spec],
                      [pltpu.VMEM((tm, tn), f32)], ("parallel", "parallel", "arbitrary"), args, comm, at)
    return res[0] if comm is None else (res[0], cres)


def _sigmoid(x):
    return 1.0 / (1.0 + jnp.exp(-x))


def _ffn_up(v, wg, wu):
    m, k = v.shape
    nchunk, _, cs = wg.shape
    n = nchunk * cs
    tm, tn, tk = _tile(m, 512), _chunk_tile(cs, 512), _tile(k, MM_TK)
    nk = k // tk
    q = cs // tn
    w_spec = pl.BlockSpec((None, tk, tn), lambda i, j, kk: (j // q, kk, j % q))

    def body(v_ref, wg_ref, wu_ref, g_ref, u_ref, act_ref, accg, accu):
        kk = pl.program_id(2)

        @pl.when(kk == 0)
        def _():
            accg[...] = jnp.zeros_like(accg)
            accu[...] = jnp.zeros_like(accu)

        vb = v_ref[...]
        accg[...] += _dot(vb, wg_ref[...], NN)
        accu[...] += _dot(vb, wu_ref[...], NN)

        @pl.when(kk == nk - 1)
        def _():
            g, u = accg[...], accu[...]
            g_ref[...] = g.astype(bf16)
            u_ref[...] = u.astype(bf16)
            act_ref[...] = (g * _sigmoid(g) * u).astype(bf16)

    out = jax.ShapeDtypeStruct((m, n), bf16)
    o_spec = pl.BlockSpec((tm, tn), lambda i, j, kk: (i, j))
    return pl.pallas_call(
        body, name="ffn_up", out_shape=(out, out, out), grid=(m // tm, n // tn, nk),
        in_specs=[pl.BlockSpec((tm, tk), lambda i, j, kk: (i, kk)), w_spec, w_spec],
        out_specs=(o_spec, o_spec, o_spec),
        scratch_shapes=[pltpu.VMEM((tm, tn), f32), pltpu.VMEM((tm, tn), f32)],
        compiler_params=_params(("parallel", "parallel", "arbitrary")),
    )(v, wg, wu)


def _ffn_down_bwd(dffn, wd, g, u):
    m, k = dffn.shape
    n = wd.shape[0]
    tm, tn, tk = _tile(m, MM_TM), _tile(n, 512), _tile(k, MM_TK)
    nk = k // tk

    def body(d_ref, w_ref, g_ref, u_ref, dg_ref, du_ref, acc):
        kk = pl.program_id(2)

        @pl.when(kk == 0)
        def _():
            acc[...] = jnp.zeros_like(acc)

        acc[...] += _dot(d_ref[...], w_ref[...], NT)

        @pl.when(kk == nk - 1)
        def _():
            dact = acc[...]
            gg, uu = g_ref[...].astype(f32), u_ref[...].astype(f32)
            sg = _sigmoid(gg)
            du_ref[...] = (dact * gg * sg).astype(bf16)
            dg_ref[...] = (dact * uu * (sg * (1.0 + gg * (1.0 - sg)))).astype(bf16)

    out = jax.ShapeDtypeStruct((m, n), bf16)
    o_spec = pl.BlockSpec((tm, tn), lambda i, j, kk: (i, j))
    return pl.pallas_call(
        body, name="ffn_down_bwd", out_shape=(out, out), grid=(m // tm, n // tn, nk),
        in_specs=[pl.BlockSpec((tm, tk), lambda i, j, kk: (i, kk)),
                  pl.BlockSpec((tn, tk), lambda i, j, kk: (j, kk)), o_spec, o_spec],
        out_specs=(o_spec, o_spec),
        scratch_shapes=[pltpu.VMEM((tm, tn), f32)],
        compiler_params=_params(("parallel", "parallel", "arbitrary")),
    )(dffn, wd, g, u)


def _blocked(fn, grid, ins, outs, accs, name, sp=None):
    n_in, n_out, n_acc = len(ins), len(outs), len(accs)
    nsp = 0 if sp is None else 1

    def body(*refs):
        refs = refs[nsp:]
        tiles = [r[...] for r in refs[:n_in]]
        res = fn(*tiles)
        if not isinstance(res, (tuple, list)):
            res = (res,)
        for r, val in zip(refs[n_in:n_in + n_out], res[:n_out]):
            r[...] = val.astype(r.dtype)
        if n_acc:
            first = pl.program_id(0) == 0
            for d in range(1, len(grid)):
                first = jnp.logical_and(first, pl.program_id(d) == 0)

            @pl.when(first)
            def _():
                for r in refs[n_in + n_out:]:
                    r[...] = jnp.zeros_like(r)

            for r, val in zip(refs[n_in + n_out:], res[n_out:]):
                r[...] += val

    def acc_map(shape):
        zeros = (0,) * len(shape)
        return lambda *idx: zeros

    in_specs = [pl.BlockSpec(bs, im) for _, bs, im in ins]
    out_specs = [pl.BlockSpec(bs, im) for _, _, bs, im in outs] + [pl.BlockSpec(s, acc_map(s)) for s in accs]
    out_shape = [jax.ShapeDtypeStruct(s, d) for s, d, _, _ in outs] + [jax.ShapeDtypeStruct(s, f32) for s in accs]
    sem = ("arbitrary",) * len(grid) if n_acc else ("parallel",) * len(grid)
    args = [a for a, _, _ in ins]
    if sp is None:
        res = pl.pallas_call(body, name=name, out_shape=out_shape, grid=grid, in_specs=in_specs,
                             out_specs=out_specs, compiler_params=_params(sem))(*args)
    else:
        gs = pltpu.PrefetchScalarGridSpec(num_scalar_prefetch=1, grid=grid, in_specs=in_specs, out_specs=out_specs)
        res = pl.pallas_call(body, name=name, out_shape=out_shape, grid_spec=gs,
                             compiler_params=_params(sem))(sp, *args)
    return res


def _rows(a, tm, cols=None, cb=0):
    w = a.shape[1] if cols is None else cols
    return (a, (tm, w), lambda i: (i, cb))


def _par(a):
    zeros = (0,) * a.ndim
    return (a, a.shape, lambda i: zeros)


def _rout(t, w, dtype, tm):
    return ((t, w), dtype, (tm, w), lambda i: (i, 0))


def _rowwise(fn, t, tm, ins, outs, accs, name):
    return _blocked(fn, (t // tm,), ins, outs, accs, name)


def _rms(x, w):
    r = lax.rsqrt(jnp.mean(x * x, axis=-1, keepdims=True) + EPS)
    return x * r * w


def _rms_bwd(x, w, dy):
    r = lax.rsqrt(jnp.mean(x * x, axis=-1, keepdims=True) + EPS)
    xh = x * r
    dyw = dy * w
    dx = r * (dyw - xh * jnp.mean(dyw * xh, axis=-1, keepdims=True))
    return dx, jnp.sum(dy * xh, axis=0, keepdims=True)


def _silu_grad(x):
    s = _sigmoid(x)
    return s * (1.0 + x * (1.0 - s))


def _rope(blk, cosp, sina, sinb):
    return blk * cosp + pltpu.roll(blk, 96, 1) * sina + pltpu.roll(blk, 32, 1) * sinb


def _rope_bwd(dy, cosp, sina, sinb):
    return dy * cosp + pltpu.roll(dy * sina, 32, 1) + pltpu.roll(dy * sinb, 96, 1)


HALO = 8


def _conv_taps(buf, w, tm, base):
    acc = buf[base:base + tm, :] * w[0:1]
    for k in range(1, SSD_K):
        acc = acc + buf[base + k:base + k + tm, :] * w[k:k + 1]
    return acc


def _conv_specs(t, tm, cdim):
    cur = pl.BlockSpec((tm, cdim), lambda i: (i, 0))
    prev = pl.BlockSpec((HALO, cdim), lambda i: (jnp.maximum(i * (tm // HALO) - 1, 0), 0))
    nxt = pl.BlockSpec((HALO, cdim), lambda i: (jnp.minimum((i + 1) * (tm // HALO), t // HALO - 1), 0))
    return cur, prev, nxt


def _conv_fwd(src, w, b, cdim, tm):
    t = src.shape[0]
    cur, prev, _ = _conv_specs(t, tm, cdim)

    def body(x_ref, p_ref, w_ref, b_ref, o_ref, buf):
        buf[0:HALO, :] = jnp.where(pl.program_id(0) > 0, p_ref[...], 0.0)
        buf[HALO:HALO + tm, :] = x_ref[...]
        pre = _conv_taps(buf, w_ref[...], tm, HALO - (SSD_K - 1)) + b_ref[...]
        o_ref[...] = pre * _sigmoid(pre)

    par = lambda a: pl.BlockSpec(a.shape, lambda i: (0, 0))
    return pl.pallas_call(
        body, name="conv_silu", out_shape=jax.ShapeDtypeStruct((t, cdim), f32), grid=(t // tm,),
        in_specs=[cur, prev, par(w), par(b)], out_specs=cur,
        scratch_shapes=[pltpu.VMEM((tm + HALO, cdim), f32)], compiler_params=_params(("parallel",)),
    )(src, src, w, b)


def _conv_bwd_pre(src, w, b, dact, cdim, tm):
    t = src.shape[0]
    cur, prev, _ = _conv_specs(t, tm, cdim)

    def body(x_ref, p_ref, w_ref, b_ref, d_ref, dpre_ref, dw0, dw1, dw2, dw3, db, buf):
        @pl.when(pl.program_id(0) == 0)
        def _():
            for r in (dw0, dw1, dw2, dw3, db):
                r[...] = jnp.zeros_like(r)

        buf[0:HALO, :] = jnp.where(pl.program_id(0) > 0, p_ref[...], 0.0)
        buf[HALO:HALO + tm, :] = x_ref[...]
        base = HALO - (SSD_K - 1)
        pre = _conv_taps(buf, w_ref[...], tm, base) + b_ref[...]
        dpre = d_ref[...] * _silu_grad(pre)
        dpre_ref[...] = dpre
        for k, r in enumerate((dw0, dw1, dw2, dw3)):
            r[...] += jnp.sum(dpre * buf[base + k:base + k + tm, :], axis=0, keepdims=True)
        db[...] += jnp.sum(dpre, axis=0, keepdims=True)

    par = lambda a: pl.BlockSpec(a.shape, lambda i: (0, 0))
    acc = pl.BlockSpec((1, cdim), lambda i: (0, 0))
    acc_shape = jax.ShapeDtypeStruct((1, cdim), f32)
    return pl.pallas_call(
        body, name="conv_silu_bwd", out_shape=[jax.ShapeDtypeStruct((t, cdim), f32)] + [acc_shape] * 5,
        grid=(t // tm,), in_specs=[cur, prev, par(w), par(b), cur], out_specs=[cur] + [acc] * 5,
        scratch_shapes=[pltpu.VMEM((tm + HALO, cdim), f32)], compiler_params=_params(("arbitrary",)),
    )(src, src, w, b, dact)


def _conv_bwd_dx(dpre, w, tm):
    t, cdim = dpre.shape
    cur, _, nxt = _conv_specs(t, tm, cdim)
    last = t // tm - 1

    def body(d_ref, n_ref, w_ref, o_ref, buf):
        buf[0:tm, :] = d_ref[...]
        buf[tm:tm + HALO, :] = jnp.where(pl.program_id(0) < last, n_ref[...], 0.0)
        wv = w_ref[...]
        acc = buf[0:tm, :] * wv[SSD_K - 1:SSD_K]
        for k in range(SSD_K - 1):
            s = SSD_K - 1 - k
            acc = acc + buf[s:s + tm, :] * wv[k:k + 1]
        o_ref[...] = acc.astype(bf16)

    return pl.pallas_call(
        body, name="conv_bwd_dx", out_shape=jax.ShapeDtypeStruct((t, cdim), bf16), grid=(t // tm,),
        in_specs=[cur, nxt, pl.BlockSpec(w.shape, lambda i: (0, 0))], out_specs=cur,
        scratch_shapes=[pltpu.VMEM((tm + HALO, cdim), f32)], compiler_params=_params(("parallel",)),
    )(dpre, dpre, w)


def _causal_mask(s):
    row = lax.broadcasted_iota(jnp.int32, s.shape, 0)
    col = lax.broadcasted_iota(jnp.int32, s.shape, 1)
    return jnp.where(row >= col, s, -jnp.inf)


def _flash_fwd(q, kv, kr, nheads, scale, comm=None, at=None):
    t = q.shape[0]
    tq = _tile(t, ATT_TILE)
    nq = t // tq

    def body(q_ref, kn_ref, kr_ref, v_ref, o_ref, lse_ref, m_sc, l_sc, acc_sc):
        qi, ki = pl.program_id(1), pl.program_id(2)

        @pl.when(ki == 0)
        def _():
            m_sc[...] = jnp.full_like(m_sc, -jnp.inf)
            l_sc[...] = jnp.zeros_like(l_sc)
            acc_sc[...] = jnp.zeros_like(acc_sc)

        def step(masked):
            k = jnp.concatenate([kn_ref[...], kr_ref[...]], axis=1)
            s = lax.dot_general(q_ref[...], k, NT, preferred_element_type=f32) * scale
            if masked:
                s = _causal_mask(s)
            m_old = m_sc[...]
            m_new = jnp.maximum(m_old, jnp.max(s, axis=1, keepdims=True))
            alpha = jnp.exp(m_old - m_new)
            p = jnp.exp(s - m_new)
            l_sc[...] = alpha * l_sc[...] + jnp.sum(p, axis=1, keepdims=True)
            acc_sc[...] = alpha * acc_sc[...] + lax.dot_general(p.astype(bf16), v_ref[...], NN,
                                                                preferred_element_type=f32)
            m_sc[...] = m_new

        @pl.when(ki < qi)
        def _():
            step(False)

        @pl.when(ki == qi)
        def _():
            step(True)
            l = l_sc[...]
            o_ref[...] = acc_sc[...] / l
            lse_ref[...] = jnp.broadcast_to(m_sc[...] + jnp.log(l), lse_ref.shape)

    kc = lambda qi, ki: jnp.minimum(ki, qi)
    o_spec = pl.BlockSpec((tq, V_HEAD), lambda h, qi, ki: (qi, h))
    out = jax.ShapeDtypeStruct((t, nheads * V_HEAD), f32)
    (o, lse), cres = _call(
        body, "flash_fwd", [out, out], (nheads, nq, nq),
        [pl.BlockSpec((tq, QK_PAD), lambda h, qi, ki: (qi, h)),
         pl.BlockSpec((tq, NOPE), lambda h, qi, ki: (kc(qi, ki), 2 * h)),
         pl.BlockSpec((tq, LANES), lambda h, qi, ki: (kc(qi, ki), 0)),
         pl.BlockSpec((tq, V_HEAD), lambda h, qi, ki: (kc(qi, ki), 2 * h + 1))],
        [o_spec, o_spec],
        [pltpu.VMEM((tq, 1), f32), pltpu.VMEM((tq, 1), f32), pltpu.VMEM((tq, V_HEAD), f32)],
        ("parallel", "parallel", "arbitrary"), [q, kv, kr, kv], comm, at)
    return o, lse, cres


def _flash_bwd_dkv(q, kv, kr, do, lse, delta, nheads, scale, comm=None, at=None):
    t = q.shape[0]
    tq = _tile(t, ATT_TILE)
    nq = t // tq

    def body(q_ref, kn_ref, kr_ref, v_ref, do_ref, lse_ref, dl_ref, dkv_ref, dkr_ref, dk_sc, dv_sc):
        ki, qi = pl.program_id(1), pl.program_id(2)

        @pl.when(qi == 0)
        def _():
            dk_sc[...] = jnp.zeros_like(dk_sc)
            dv_sc[...] = jnp.zeros_like(dv_sc)

        def step(masked):
            qb = q_ref[...]
            dob = do_ref[...]
            k = jnp.concatenate([kn_ref[...], kr_ref[...]], axis=1)
            s = lax.dot_general(qb, k, NT, preferred_element_type=f32) * scale
            if masked:
                s = _causal_mask(s)
            p = jnp.exp(s - lse_ref[:, 0:1])
            dv_sc[...] += lax.dot_general(p.astype(bf16), dob, TN, preferred_element_type=f32)
            dp = lax.dot_general(dob, v_ref[...], NT, preferred_element_type=f32)
            ds = p * (dp - dl_ref[:, 0:1]) * scale
            dk_sc[...] += lax.dot_general(ds.astype(bf16), qb, TN, preferred_element_type=f32)

        @pl.when(qi > ki)
        def _():
            step(False)

        @pl.when(qi == ki)
        def _():
            step(True)

        @pl.when(qi == nq - 1)
        def _():
            dk = dk_sc[...]
            dkv_ref[...] = jnp.concatenate([dk[:, :NOPE], dv_sc[...]], axis=1).astype(bf16)
            dkr_ref[...] = dk[:, NOPE:]

    qc = lambda ki, qi: jnp.maximum(ki, qi)
    hspec = pl.BlockSpec((tq, V_HEAD), lambda h, ki, qi: (qc(ki, qi), h))
    (dkv, dkr), cres = _call(
        body, "flash_bwd_dkv",
        [jax.ShapeDtypeStruct((t, nheads * QK_PAD), bf16), jax.ShapeDtypeStruct((t, nheads * LANES), f32)],
        (nheads, nq, nq),
        [pl.BlockSpec((tq, QK_PAD), lambda h, ki, qi: (qc(ki, qi), h)),
         pl.BlockSpec((tq, NOPE), lambda h, ki, qi: (ki, 2 * h)),
         pl.BlockSpec((tq, LANES), lambda h, ki, qi: (ki, 0)),
         pl.BlockSpec((tq, V_HEAD), lambda h, ki, qi: (ki, 2 * h + 1)),
         hspec, hspec, hspec],
        [pl.BlockSpec((tq, QK_PAD), lambda h, ki, qi: (ki, h)), pl.BlockSpec((tq, LANES), lambda h, ki, qi: (ki, h))],
        [pltpu.VMEM((tq, QK_PAD), f32), pltpu.VMEM((tq, V_HEAD), f32)],
        ("parallel", "parallel", "arbitrary"), [q, kv, kr, kv, do, lse, delta], comm, at)
    return dkv, dkr, cres


def _flash_bwd_dq(q, kv, kr, do, lse, delta, nheads, scale, comm=None, at=None):
    t = q.shape[0]
    tq = _tile(t, ATT_TILE)
    nq = t // tq

    def body(q_ref, kn_ref, kr_ref, v_ref, do_ref, lse_ref, dl_ref, dq_ref, dq_sc):
        qi, ki = pl.program_id(1), pl.program_id(2)

        @pl.when(ki == 0)
        def _():
            dq_sc[...] = jnp.zeros_like(dq_sc)

        def step(masked):
            dob = do_ref[...]
            k = jnp.concatenate([kn_ref[...], kr_ref[...]], axis=1)
            s = lax.dot_general(q_ref[...], k, NT, preferred_element_type=f32) * scale
            if masked:
                s = _causal_mask(s)
            p = jnp.exp(s - lse_ref[:, 0:1])
            dp = lax.dot_general(dob, v_ref[...], NT, preferred_element_type=f32)
            ds = p * (dp - dl_ref[:, 0:1]) * scale
            dq_sc[...] += lax.dot_general(ds.astype(bf16), k, NN, preferred_element_type=f32)

        @pl.when(ki < qi)
        def _():
            step(False)

        @pl.when(ki == qi)
        def _():
            step(True)
            dq_ref[...] = dq_sc[...]

    kc = lambda qi, ki: jnp.minimum(ki, qi)
    hspec = pl.BlockSpec((tq, V_HEAD), lambda h, qi, ki: (qi, h))
    (dq,), cres = _call(
        body, "flash_bwd_dq", [jax.ShapeDtypeStruct((t, nheads * QK_PAD), f32)], (nheads, nq, nq),
        [pl.BlockSpec((tq, QK_PAD), lambda h, qi, ki: (qi, h)),
         pl.BlockSpec((tq, NOPE), lambda h, qi, ki: (kc(qi, ki), 2 * h)),
         pl.BlockSpec((tq, LANES), lambda h, qi, ki: (kc(qi, ki), 0)),
         pl.BlockSpec((tq, V_HEAD), lambda h, qi, ki: (kc(qi, ki), 2 * h + 1)),
         hspec, hspec, hspec],
        [pl.BlockSpec((tq, QK_PAD), lambda h, qi, ki: (qi, h))],
        [pltpu.VMEM((tq, QK_PAD), f32)],
        ("parallel", "parallel", "arbitrary"), [q, kv, kr, kv, do, lse, delta], comm, at)
    return dq, cres


def _split3(a):
    hi = a.astype(bf16)
    r1 = a - hi.astype(f32)
    mid = r1.astype(bf16)
    lo = (r1 - mid.astype(f32)).astype(bf16)
    return hi, mid, lo


def _ones_dot_left(tri, a):
    hi, mid, lo = _split3(a)
    d = lambda v: lax.dot_general(tri, v, NN, preferred_element_type=f32)
    return d(hi) + d(mid) + d(lo)


def _ones_dot_right(a, tri):
    hi, mid, lo = _split3(a)
    d = lambda v: lax.dot_general(v, tri, NN, preferred_element_type=f32)
    return d(hi) + d(mid) + d(lo)


def _softplus(x):
    return jnp.maximum(x, 0.0) + jnp.log(1.0 + jnp.exp(-jnp.abs(x)))


def _ssd_common(dt_ref, dtT_ref, dtb_ref, dtbT_ref, alog_ref, alogT_ref):
    ii = lax.broadcasted_iota(jnp.int32, (CHUNK, CHUNK), 0)
    jj = lax.broadcasted_iota(jnp.int32, (CHUNK, CHUNK), 1)
    tri = ii >= jj
    raw = dt_ref[...] + dtb_ref[...]
    dt = _softplus(raw)
    a_neg = -jnp.exp(alog_ref[...])
    cum = _ones_dot_left(tri.astype(bf16), dt * a_neg)
    dt_t = _softplus(dtT_ref[...] + dtbT_ref[...])
    cum_t = _ones_dot_right(dt_t * (-jnp.exp(alogT_ref[...])), (ii <= jj).astype(bf16))
    return tri, raw, dt, a_neg, cum, cum_t


def _ssd_fwd(xbc, dt, dt_t, dtb, dtb_t, alog, alog_t, dskip, width):
    t, cdim = xbc.shape
    hs = dt.shape[1]
    nc = t // CHUNK
    epg = hs // SSD_G
    gn = SSD_G * SSD_N

    def body(x_ref, dt_ref, dtT_ref, dtb_ref, dtbT_ref, alog_ref, alogT_ref, d_ref, y_ref, hp_ref, h_sc):
        @pl.when(pl.program_id(0) == 0)
        def _():
            h_sc[...] = jnp.zeros_like(h_sc)

        tri, _, dtv, _, cum, cum_t = _ssd_common(dt_ref, dtT_ref, dtb_ref, dtbT_ref, alog_ref, alogT_ref)
        dsk = d_ref[...]
        for g in range(SSD_G):
            bb = x_ref[:, width + g * SSD_N: width + (g + 1) * SSD_N].astype(bf16)
            cb_ = x_ref[:, width + gn + g * SSD_N: width + gn + (g + 1) * SSD_N].astype(bf16)
            cbm = lax.dot_general(cb_, bb, NT, preferred_element_type=f32)
            for e in range(g * epg, (g + 1) * epg):
                xe = x_ref[:, e * SSD_P:(e + 1) * SSD_P]
                dtc = dtv[:, e:e + 1]
                cc = cum[:, e:e + 1]
                clast = cum[CHUNK - 1:CHUNK, e:e + 1]
                seg = cc - cum_t[e:e + 1, :]
                lmat = jnp.exp(jnp.where(tri, seg, -jnp.inf))
                xdt = xe * dtc
                h = h_sc[e]
                hp_ref[0, e] = h
                y = _dot(cbm * lmat, xdt, NN)
                y = y + _dot(cb_, h, NT) * jnp.exp(cc)
                y_ref[:, e * SSD_P:(e + 1) * SSD_P] = y + xe * dsk[:, e:e + 1]
                st = _dot(xdt * jnp.exp(clast - cc), bb, TN)
                h_sc[e] = h * jnp.exp(clast) + st

    par = lambda a: pl.BlockSpec(a.shape, lambda i: (0,) * a.ndim)
    return pl.pallas_call(
        body, name="ssd_fwd",
        out_shape=(jax.ShapeDtypeStruct((t, width), f32), jax.ShapeDtypeStruct((nc, hs, SSD_P, SSD_N), f32)),
        grid=(nc,),
        in_specs=[pl.BlockSpec((CHUNK, cdim), lambda i: (i, 0)), pl.BlockSpec((CHUNK, hs), lambda i: (i, 0)),
                  pl.BlockSpec((hs, CHUNK), lambda i: (0, i)), par(dtb), par(dtb_t), par(alog), par(alog_t), par(dskip)],
        out_specs=(pl.BlockSpec((CHUNK, width), lambda i: (i, 0)),
                   pl.BlockSpec((1, hs, SSD_P, SSD_N), lambda i: (i, 0, 0, 0))),
        scratch_shapes=[pltpu.VMEM((hs, SSD_P, SSD_N), f32)],
        compiler_params=_params(("arbitrary",)),
    )(xbc, dt, dt_t, dtb, dtb_t, alog, alog_t, dskip)


def _ssd_bwd(xbc, dt, dt_t, dtb, dtb_t, alog, alog_t, dskip, hprev, dy, width):
    t, cdim = xbc.shape
    hs = dt.shape[1]
    nc = t // CHUNK
    epg = hs // SSD_G
    gn = SSD_G * SSD_N

    def body(x_ref, dt_ref, dtT_ref, dtb_ref, dtbT_ref, alog_ref, alogT_ref, d_ref, hp_ref, dy_ref,
             dx_ref, ddt_ref, dalog_ref, ddsk_ref, ddtb_ref, dh_sc):
        @pl.when(pl.program_id(0) == 0)
        def _():
            dh_sc[...] = jnp.zeros_like(dh_sc)
            dalog_ref[...] = jnp.zeros_like(dalog_ref)
            ddsk_ref[...] = jnp.zeros_like(ddsk_ref)
            ddtb_ref[...] = jnp.zeros_like(ddtb_ref)

        tri, raw, dtv, a_neg, cum, cum_t = _ssd_common(dt_ref, dtT_ref, dtb_ref, dtbT_ref, alog_ref, alogT_ref)
        dsk = d_ref[...]
        head_iota = lax.broadcasted_iota(jnp.int32, (1, hs), 1)
        last_row = (lax.broadcasted_iota(jnp.int32, (CHUNK, 1), 0) == CHUNK - 1).astype(f32)
        dcum_all = jnp.zeros((CHUNK, hs), f32)
        ddtx_all = jnp.zeros((CHUNK, hs), f32)
        ddsk = jnp.zeros((1, hs), f32)
        for g in range(SSD_G):
            bsl = slice(width + g * SSD_N, width + (g + 1) * SSD_N)
            csl = slice(width + gn + g * SSD_N, width + gn + (g + 1) * SSD_N)
            bb = x_ref[:, bsl].astype(bf16)
            cb_ = x_ref[:, csl].astype(bf16)
            cbm = lax.dot_general(cb_, bb, NT, preferred_element_type=f32)
            dbg = jnp.zeros((CHUNK, SSD_N), f32)
            dcg = jnp.zeros((CHUNK, SSD_N), f32)
            for e in range(g * epg, (g + 1) * epg):
                onehot = (head_iota == e).astype(f32)
                esl = slice(e * SSD_P, (e + 1) * SSD_P)
                xe = x_ref[:, esl]
                dye = dy_ref[:, esl]
                dtc = dtv[:, e:e + 1]
                cc = cum[:, e:e + 1]
                clast = cum[CHUNK - 1:CHUNK, e:e + 1]
                lmat = jnp.exp(jnp.where(tri, cc - cum_t[e:e + 1, :], -jnp.inf))
                gmat = cbm * lmat
                xdt = xe * dtc
                ee = jnp.exp(cc)
                ff = jnp.exp(clast - cc)
                dec = jnp.exp(clast)
                h = hp_ref[0, e]
                dh = dh_sc[e]
                ch = _dot(cb_, h, NT)
                dye_e = dye * ee
                dcum = jnp.sum(dye * ch, axis=1, keepdims=True) * ee
                dcg = dcg + _dot(dye_e, h, NN)
                dh_new = _dot(dye_e, cb_, TN)
                dbg = dbg + _dot(xdt * ff, dh, NN)
                bds = _dot(bb, dh, NT)
                dxdt = bds * ff
                dff = jnp.sum(bds * xdt, axis=1, keepdims=True) * ff
                dcum = dcum - dff
                dclast = jnp.sum(dff, axis=0, keepdims=True) + dec * jnp.sum(
                    jnp.sum(dh * h, axis=1, keepdims=True), axis=0, keepdims=True)
                dgm = _dot(dye, xdt, NT)
                dxdt = dxdt + _dot(gmat, dye, TN)
                dcb = dgm * lmat
                dcg = dcg + _dot(dcb, bb, NN)
                dbg = dbg + _dot(dcb, cb_, TN)
                mm = dgm * gmat
                dcum = dcum + jnp.sum(mm, axis=1, keepdims=True) - jnp.sum(mm.T, axis=1, keepdims=True)
                dcum = dcum + dclast * last_row
                dx_ref[:, esl] = dxdt * dtc + dye * dsk[:, e:e + 1]
                ddtx_all = ddtx_all + jnp.sum(dxdt * xe, axis=1, keepdims=True) * onehot
                dcum_all = dcum_all + dcum * onehot
                ddsk = ddsk + jnp.sum(jnp.sum(dye * xe, axis=1, keepdims=True), axis=0, keepdims=True) * onehot
                dh_sc[e] = dh_new + dec * dh
            dx_ref[:, bsl] = dbg
            dx_ref[:, csl] = dcg
        ii = lax.broadcasted_iota(jnp.int32, (CHUNK, CHUNK), 0)
        jj = lax.broadcasted_iota(jnp.int32, (CHUNK, CHUNK), 1)
        da = _ones_dot_left((jj >= ii).astype(bf16), dcum_all)
        ddt = da * a_neg + ddtx_all
        dalog_ref[...] += jnp.sum(da * dtv, axis=0, keepdims=True) * a_neg
        draw = ddt * _sigmoid(raw)
        ddt_ref[...] = draw
        ddtb_ref[...] += jnp.sum(draw, axis=0, keepdims=True)
        ddsk_ref[...] += ddsk

    rev = lambda i: nc - 1 - i
    par = lambda a: pl.BlockSpec(a.shape, lambda i: (0,) * a.ndim)
    acc = pl.BlockSpec((1, hs), lambda i: (0, 0))
    acc_shape = jax.ShapeDtypeStruct((1, hs), f32)
    return pl.pallas_call(
        body, name="ssd_bwd",
        out_shape=(jax.ShapeDtypeStruct((t, cdim), f32), jax.ShapeDtypeStruct((t, hs), f32),
                   acc_shape, acc_shape, acc_shape),
        grid=(nc,),
        in_specs=[pl.BlockSpec((CHUNK, cdim), lambda i: (rev(i), 0)), pl.BlockSpec((CHUNK, hs), lambda i: (rev(i), 0)),
                  pl.BlockSpec((hs, CHUNK), lambda i: (0, rev(i))), par(dtb), par(dtb_t), par(alog), par(alog_t),
                  par(dskip), pl.BlockSpec((1, hs, SSD_P, SSD_N), lambda i: (rev(i), 0, 0, 0)),
                  pl.BlockSpec((CHUNK, width), lambda i: (rev(i), 0))],
        out_specs=(pl.BlockSpec((CHUNK, cdim), lambda i: (rev(i), 0)), pl.BlockSpec((CHUNK, hs), lambda i: (rev(i), 0)),
                   acc, acc, acc),
        scratch_shapes=[pltpu.VMEM((hs, SSD_P, SSD_N), f32)],
        compiler_params=_params(("arbitrary",)),
    )(xbc, dt, dt_t, dtb, dtb_t, alog, alog_t, dskip, hprev, dy)


def _where_am_i():
    x, y, c = lax.axis_index("x"), lax.axis_index("y"), lax.axis_index("c")
    chips = [(1 - x, y), (x, 1 - y), (1 - x, 1 - y)]
    return x, y, c, chips


def _remote(src, dst, send_sems, recv_sems, k, to):
    return pltpu.make_async_remote_copy(src_ref=src, dst_ref=dst, send_sem=send_sems.at[k], recv_sem=recv_sems.at[k],
                                        device_id=to, device_id_type=MESH)


class _Comm:
    def __init__(self, args, outs, nsem, nphase, run):
        self.args, self.outs, self.nsem, self.nphase, self.run = list(args), list(outs), nsem, nphase, run

    def sems(self):
        return [pltpu.SemaphoreType.DMA((self.nsem,)), pltpu.SemaphoreType.DMA((self.nsem,))]


def _comm_call(comm, name):
    na, no = len(comm.args), len(comm.outs)

    def body(*refs):
        for phase in range(comm.nphase):
            comm.run(phase, refs[:na], refs[na:na + no], refs[na + no], refs[na + no + 1])

    return list(pl.pallas_call(body, name=name, out_shape=comm.outs, in_specs=[ANY] * na, out_specs=[ANY] * no,
                               scratch_shapes=comm.sems())(*comm.args))


def _half(ref, c, r2):
    return ref.at[pl.ds(c * r2, r2)]


def _gather_weights(shards, wholes):
    ns, nw = len(shards), len(wholes)
    per = 7

    def run(phase, srcs, outs, send_sems, recv_sems):
        x, y, c, chips = _where_am_i()
        me, sib = 2 * x + y, (x, y, 1 - c)
        def first():
            cps = []
            for w in range(ns + nw):
                src, out, base = srcs[w], outs[w], per * w
                halved = w < ns
                r2 = src.shape[0] // 2
                piece = _half(src, c, r2) if halved else src
                for j, (cx, cy) in enumerate(chips):
                    dst = _half(out.at[me], c, r2) if halved else out.at[me]
                    cps.append(_remote(piece, dst, send_sems, recv_sems, base + j, (cx, cy, c)))
                cps.append(_remote(src, out.at[me], send_sems, recv_sems, base + 6, sib))
            return cps

        def passed():
            cps = []
            for w in range(ns):
                r2 = srcs[w].shape[0] // 2
                for j, (cx, cy) in enumerate(chips):
                    got = _half(outs[w].at[2 * cx + cy], c, r2)
                    cps.append(_remote(got, got, send_sems, recv_sems, per * w + 3 + j, sib))
            return cps

        if phase == 0:
            for cp in first():
                cp.start()
        elif phase == 1:
            it = iter(passed())
            for w in range(ns + nw):
                src, out, base = srcs[w], outs[w], per * w
                r2 = src.shape[0] // 2
                for j, (cx, cy) in enumerate(chips):
                    got = _half(out.at[2 * cx + cy], c, r2) if w < ns else out.at[2 * cx + cy]
                    _remote(got, got, send_sems, recv_sems, base + j, sib).wait_recv()
                    if w < ns:
                        next(it).start()
        else:
            for w in range(ns + nw):
                src, out, base = srcs[w], outs[w], per * w
                r2 = src.shape[0] // 2
                if w < ns:
                    for j, (cx, cy) in enumerate(chips):
                        got = _half(out.at[2 * cx + cy], 1 - c, r2)
                        _remote(got, got, send_sems, recv_sems, base + 3 + j, sib).wait_recv()
                _remote(src, out.at[me], send_sems, recv_sems, base + 6, sib).wait_recv()
            for cp in first() + passed():
                cp.wait_send()

    args = list(shards) + list(wholes)
    outs = [jax.ShapeDtypeStruct((4,) + a.shape, a.dtype) for a in args]
    return _Comm(args, outs, per * len(args), 3, run)


def _sibling_send_halves(gs):
    n = len(gs)

    def run(phase, srcs, outs, send_sems, recv_sems):
        x, y, c, _ = _where_am_i()
        sib = (x, y, 1 - c)
        cps = []
        for w in range(n):
            r2 = srcs[w].shape[1] // 2
            for k in range(4):
                cps.append(_remote(_half(srcs[w].at[k], 1 - c, r2), outs[w].at[k], send_sems, recv_sems, 4 * w + k, sib))
        for cp in cps:
            cp.start() if phase == 0 else cp.wait()

    outs = [jax.ShapeDtypeStruct((4, g.shape[1] // 2, g.shape[2]), g.dtype) for g in gs]
    return _Comm(gs, outs, 4 * n, 2, run)


def _chips_exchange(ps):
    n = len(ps)

    def run(phase, srcs, outs, send_sems, recv_sems):
        x, y, c, chips = _where_am_i()
        me = 2 * x + y
        cps = []
        for w in range(n):
            for j, (cx, cy) in enumerate(chips):
                cps.append(_remote(srcs[w].at[2 * cx + cy], outs[w].at[me], send_sems, recv_sems, 3 * w + j, (cx, cy, c)))
        if phase == 0:
            for cp in cps:
                cp.start()
        else:
            for w in range(n):
                for j, (cx, cy) in enumerate(chips):
                    got = outs[w].at[2 * cx + cy]
                    _remote(got, got, send_sems, recv_sems, 3 * w + j, (cx, cy, c)).wait_recv()
            for cp in cps:
                cp.wait_send()

    outs = [jax.ShapeDtypeStruct(p.shape, p.dtype) for p in ps]
    return _Comm(ps, outs, 3 * n, 2, run)


def _sibling_swap(rs):
    n = len(rs)

    def run(phase, srcs, outs, send_sems, recv_sems):
        x, y, c, _ = _where_am_i()
        for w in range(n):
            cp = _remote(srcs[w], outs[w], send_sems, recv_sems, w, (x, y, 1 - c))
            cp.start() if phase == 0 else cp.wait()

    outs = [jax.ShapeDtypeStruct(r.shape, r.dtype) for r in rs]
    return _Comm(rs, outs, n, 2, run)


def _gather_all(v):
    rows = v.shape[0]

    def body(x_ref, out_ref, send_sems, recv_sems, local_sem):
        x, y, c, chips = _where_am_i()
        me, sib = (x, y, c), (x, y, 1 - c)
        blk = lambda px, py, pc: out_ref.at[4 * px + 2 * py + pc]
        mine = pltpu.make_async_copy(x_ref, blk(*me), local_sem)
        mine.start()
        first = [_remote(x_ref, blk(*me), send_sems, recv_sems, 0, sib)]
        first += [_remote(x_ref, blk(*me), send_sems, recv_sems, 1 + j, (*chip, c)) for j, chip in enumerate(chips)]
        for cp in first:
            cp.start()
        passed = [_remote(blk(*chip, c), blk(*chip, c), send_sems, recv_sems, 4 + j, sib) for j, chip in enumerate(chips)]
        for j, chip in enumerate(chips):
            _remote(blk(*chip, c), blk(*chip, c), send_sems, recv_sems, 1 + j, me).wait_recv()
            passed[j].start()
        _remote(blk(*sib), blk(*sib), send_sems, recv_sems, 0, me).wait_recv()
        for j, chip in enumerate(chips):
            _remote(blk(*chip, 1 - c), blk(*chip, 1 - c), send_sems, recv_sems, 4 + j, me).wait_recv()
        for cp in first + passed:
            cp.wait_send()
        mine.wait()

    vm = pl.BlockSpec(memory_space=pltpu.VMEM)
    return pl.pallas_call(
        body, name="gather_all", out_shape=jax.ShapeDtypeStruct((8, rows, LANES), v.dtype),
        in_specs=[vm], out_specs=vm,
        scratch_shapes=[pltpu.SemaphoreType.DMA((7,)), pltpu.SemaphoreType.DMA((7,)), pltpu.SemaphoreType.DMA],
    )(v)


def _flat_pad(parts, total):
    v = jnp.concatenate([p.reshape(-1) for p in parts])
    return jnp.pad(v, (0, total - v.shape[0]))


def _adamw(w, g, m, v):
    m = ADAM_B1 * m + (1.0 - ADAM_B1) * g
    v = ADAM_B2 * v + (1.0 - ADAM_B2) * jnp.square(g)
    m_hat = m / (1.0 - ADAM_B1 ** ADAM_STEP)
    v_hat = v / (1.0 - ADAM_B2 ** ADAM_STEP)
    delta = -ADAM_LR * (m_hat / (jnp.sqrt(v_hat) + ADAM_EPS) + ADAM_WD * w)
    return delta, m, v


def _adamw_call(w, g, m, v, name):
    r, cdim = w.shape
    tm = _tile(r, ROW_TILE, 8)
    o = _rout(r, cdim, f32, tm)
    return _rowwise(_adamw, r, tm, [_rows(w, tm), _rows(g, tm), _rows(m, tm), _rows(v, tm)], [o, o, o], [], name)


def _adamw_halves(w, m, v, g_mine, g_sib, sp, name):
    r, cdim = w.shape
    r2 = r // 2
    tr = _tile(r2, ROW_TILE, 8)
    nb = r2 // tr

    def body(sp_ref, w_ref, m_ref, v_ref, ga_ref, gb_ref, g_out, d_out, m_out, v_out):
        g = jnp.where(pl.program_id(0) == sp_ref[1], ga_ref[...], gb_ref[...])
        delta, mn, vn = _adamw(w_ref[...], g, m_ref[...], v_ref[...])
        g_out[...] = g
        d_out[...] = delta
        m_out[...] = mn
        v_out[...] = vn

    full = pl.BlockSpec((tr, cdim), lambda h, i, s: (h * nb + i, 0))
    mine = pl.BlockSpec((tr, cdim), lambda h, i, s: (jnp.where(h == s[1], i, 0), 0))
    sib = pl.BlockSpec((tr, cdim), lambda h, i, s: (jnp.where(h == s[1], 0, i), 0))
    out = jax.ShapeDtypeStruct((r, cdim), f32)
    gs = pltpu.PrefetchScalarGridSpec(num_scalar_prefetch=1, grid=(2, nb), in_specs=[full, full, full, mine, sib],
                                      out_specs=[full, full, full, full])
    return pl.pallas_call(body, name=name, out_shape=[out, out, out, out], grid_spec=gs,
                          compiler_params=_params(("parallel", "parallel")))(sp, w, m, v, g_mine, g_sib)


MIXER = ("w_in", "w_uq", "w_ukv", "w_out")
FFN = ("w_gate", "w_up", "w_down")
BIG = MIXER + FFN
SMALL = ("q_norm_w", "kv_norm_w", "conv_w", "conv_b", "dt_bias", "a_log", "d_skip", "ssd_norm_w", "attn_out_norm_w",
         "pre_mix_norm_w", "post_mix_norm_w", "pre_ffn_norm_w", "post_ffn_norm_w")
ORDER = ("w_in", "q_norm_w", "w_uq", "kv_norm_w", "w_ukv", "conv_w", "conv_b", "dt_bias", "a_log", "d_skip",
         "ssd_norm_w", "attn_out_norm_w", "w_out", "pre_mix_norm_w", "post_mix_norm_w", "pre_ffn_norm_w",
         "post_ffn_norm_w", "w_gate", "w_up", "w_down")


def kernel(x, positions, w_in, q_norm_w, w_uq, kv_norm_w, w_ukv, conv_w, conv_b, dt_bias, a_log, d_skip, ssd_norm_w, attn_out_norm_w, w_out, pre_mix_norm_w, post_mix_norm_w, pre_ffn_norm_w, post_ffn_norm_w, w_gate, w_up, w_down, loss_target, m_w_in, m_q_norm_w, m_w_uq, m_kv_norm_w, m_w_ukv, m_conv_w, m_conv_b, m_dt_bias, m_a_log, m_d_skip, m_ssd_norm_w, m_attn_out_norm_w, m_w_out, m_pre_mix_norm_w, m_post_mix_norm_w, m_pre_ffn_norm_w, m_post_ffn_norm_w, m_w_gate, m_w_up, m_w_down, v_w_in, v_q_norm_w, v_w_uq, v_kv_norm_w, v_w_ukv, v_conv_w, v_conv_b, v_dt_bias, v_a_log, v_d_skip, v_ssd_norm_w, v_attn_out_norm_w, v_w_out, v_pre_mix_norm_w, v_post_mix_norm_w, v_pre_ffn_norm_w, v_post_ffn_norm_w, v_w_gate, v_w_up, v_w_down):
    local = dict(locals())
    wts = {n: local[n][0] for n in ORDER}
    mom_m = {n: local["m_" + n][0] for n in ORDER}
    mom_v = {n: local["v_" + n][0] for n in ORDER}
    xs = x[0]
    tgt = loss_target[0]
    t, d = xs.shape
    nchip = 4
    my_x, my_y, my_c = lax.axis_index("x"), lax.axis_index("y"), lax.axis_index("c")
    my_chip = 2 * my_x + my_y

    mla_w = d // 2
    nh = mla_w // V_HEAD
    width = d - mla_w
    hs = width // SSD_P
    gn = SSD_G * SSD_N
    cdim = width + 2 * gn
    in_sizes = (Q_RANK, KV_RANK, ROPE, width, cdim, hs)
    d_in = sum(in_sizes)
    tail = LANES
    off_xbc = 0
    off_cq = cdim
    off_ckv = off_cq + Q_RANK
    off_z = off_ckv + KV_RANK
    off_tail = off_z + width
    d_in_p = _round_up(off_tail + tail, 256)
    gw = width // SSD_G
    assert off_cq % Q_RANK == 0 and off_ckv % KV_RANK == 0 and off_z % gw == 0 and off_tail % LANES == 0
    qk_head = NOPE + ROPE
    scale = qk_head ** -0.5
    tm = _tile(t, ROW_TILE, 8)
    tmw = _tile(t, ROW_TILE // 2, 8)

    sp = jnp.stack([my_chip, my_c]).astype(jnp.int32)

    def pair_sums(names, gl, from_sib):
        res = []
        for n, g, fs in zip(names, gl, from_sib):
            _, r2, cs = fs.shape
            tr = _tile(r2, ROW_TILE, 16)
            nb = r2 // tr
            res.append(_blocked(
                lambda a, b: a + b, (nchip, nb),
                [(g, (None, tr, cs), lambda k, i, s, nb=nb: (k, s[1] * nb + i, 0)),
                 (fs, (None, tr, cs), lambda k, i, s: (k, i, 0))],
                [((nchip, r2, cs), bf16, (None, tr, cs), lambda k, i, s: (k, i, 0))], [], "pair_sum_" + n, sp=sp)[0])
        return res

    def chip_sums(names, gl, from_sib, from_chips):
        res = []
        for n, g, fs, fc in zip(names, gl, from_sib, from_chips):
            _, r2, cs = fs.shape
            tr = _tile(r2, ROW_TILE, 16)
            nb = r2 // tr
            res.append(_blocked(
                lambda a, b, r1, r2_, r3: ((a + b) + r1.astype(f32)) + r2_.astype(f32) + r3.astype(f32), (nb,),
                [(g, (None, tr, cs), lambda i, s, nb=nb: (s[0], s[1] * nb + i, 0)),
                 (fs, (None, tr, cs), lambda i, s: (s[0], i, 0)),
                 (fc, (None, tr, cs), lambda i, s: (s[0] ^ 1, i, 0)),
                 (fc, (None, tr, cs), lambda i, s: (s[0] ^ 2, i, 0)),
                 (fc, (None, tr, cs), lambda i, s: (s[0] ^ 3, i, 0))],
                [((r2, cs), f32, (tr, cs), lambda i, s: (i, 0))], [], "chip_sum_" + n, sp=sp)[0])
        return res

    ck, ccs = wts["conv_w"].shape
    gathered = _comm_call(_gather_weights([wts[n].astype(bf16) for n in MIXER], [wts["conv_w"]]), "gather_mixer_weights")
    gat = dict(zip(MIXER, gathered[:len(MIXER)]))
    cat_cols = lambda g: jnp.concatenate([g[k] for k in range(nchip)], axis=1)
    conv_full = cat_cols(gathered[len(MIXER)])
    ffn_gather = _gather_weights([wts[n].astype(bf16) for n in FFN], [])

    wi = cat_cols(gat["w_in"])
    o = np.cumsum((0,) + in_sizes)
    seg = lambda i: wi[:, o[i]:o[i + 1]]
    w_in_p = jnp.concatenate([seg(4), seg(0), seg(1), seg(3), seg(2), seg(5),
                              jnp.zeros((d, d_in_p - off_tail - ROPE - hs), bf16)], axis=1)
    w_uq_p = jnp.pad(cat_cols(gat["w_uq"]).reshape(Q_RANK, nh, qk_head), ((0, 0), (0, 0), (0, QK_PAD - qk_head))
                     ).reshape(Q_RANK, nh * QK_PAD)
    w_ukv_f = gat["w_ukv"]
    w_out_f = gat["w_out"].reshape(-1, gat["w_out"].shape[2])

    inv_freq = ROPE_THETA ** (-jnp.arange(0, ROPE, 2, dtype=f32) / ROPE)
    ang = positions[0].astype(f32)[:, None] * inv_freq
    cos, sin = jnp.cos(ang), jnp.sin(ang)
    z32, z64, z96 = jnp.zeros((t, 32), f32), jnp.zeros((t, 64), f32), jnp.zeros((t, 96), f32)
    cosp = jnp.concatenate([cos, cos, z64], axis=1)
    sina = jnp.concatenate([-sin, z96], axis=1)
    sinb = jnp.concatenate([z32, sin, z64], axis=1)

    row = lambda a: a.reshape(1, -1)
    w_pre_mix, w_post_mix = row(wts["pre_mix_norm_w"]), row(wts["post_mix_norm_w"])
    w_pre_ffn, w_post_ffn = row(wts["pre_ffn_norm_w"]), row(wts["post_ffn_norm_w"])
    w_qn, w_kvn = row(wts["q_norm_w"]), row(wts["kv_norm_w"])
    w_attn_n, w_ssd_n = row(wts["attn_out_norm_w"]), row(wts["ssd_norm_w"])
    conv_b_r = row(wts["conv_b"])
    dtb, alog, dskip = row(wts["dt_bias"]), row(wts["a_log"]), row(wts["d_skip"])
    dtb_t, alog_t = dtb.reshape(hs, 1), alog.reshape(hs, 1)

    u = _rowwise(lambda a, w: _rms(a, w), t, tm, [_rows(xs, tm), _par(w_pre_mix)], [_rout(t, d, bf16, tm)], [],
                 "pre_mix_norm")[0]
    proj = _matmul([(u, w_in_p)], "nn", f32, "in_proj")
    cq_in = _rows(proj, tm, Q_RANK, off_cq // Q_RANK)
    ckv_in = _rows(proj, tm, KV_RANK, off_ckv // KV_RANK)
    cqn = _rowwise(lambda a, w: _rms(a, w), t, tm, [cq_in, _par(w_qn)], [_rout(t, Q_RANK, bf16, tm)], [], "q_norm")[0]
    ckvn = _rowwise(lambda a, w: _rms(a, w), t, tm, [ckv_in, _par(w_kvn)], [_rout(t, KV_RANK, bf16, tm)], [],
                    "kv_norm")[0]
    q_raw = _matmul([(cqn, w_uq_p)], "nn", f32, "q_up")
    kv = _matmul([(ckvn, w_ukv_f)], "nn", bf16, "kv_up")

    def q_rope_fn(qt, cp, sa, sb):
        parts = []
        for h in range(nh):
            parts.append(qt[:, h * QK_PAD: h * QK_PAD + NOPE])
            parts.append(_rope(qt[:, h * QK_PAD + NOPE:(h + 1) * QK_PAD], cp, sa, sb))
        return jnp.concatenate(parts, axis=1)

    tail_cb = off_tail // LANES
    q2 = _rowwise(q_rope_fn, t, tm, [_rows(q_raw, tm), _rows(cosp, tm), _rows(sina, tm), _rows(sinb, tm)],
                  [_rout(t, nh * QK_PAD, bf16, tm)], [], "q_rope")[0]
    kr2 = _rowwise(_rope, t, tm, [_rows(proj, tm, LANES, tail_cb), _rows(cosp, tm), _rows(sina, tm), _rows(sinb, tm)],
                   [_rout(t, LANES, bf16, tm)], [], "k_rope")[0]
    o_att, lse, ffn_w = _flash_fwd(q2, kv, kr2, nh, scale, ffn_gather, (0.0, 0.75))
    w_gate_f, w_up_f = ffn_w[0], ffn_w[1]
    w_down_f = ffn_w[2].reshape(-1, ffn_w[2].shape[2])

    xbc_act = _conv_fwd(proj, conv_full, conv_b_r, cdim, tm)
    dt_raw = proj[:, off_tail + ROPE: off_tail + ROPE + hs]
    dt_raw_t = dt_raw.T
    y_ssd, hprev = _ssd_fwd(xbc_act, dt_raw, dt_raw_t, dtb, dtb_t, alog, alog_t, dskip, width)
    z_ins = [_rows(proj, tm, gw, off_z // gw + i) for i in range(SSD_G)]

    def mix_norms_fn(ov, yv, *rest):
        zs, wa, ws = rest[:SSD_G], rest[SSD_G], rest[SSD_G + 1]
        outs = [_rms(ov, wa)]
        for i in range(SSD_G):
            sl = slice(i * gw, (i + 1) * gw)
            outs.append(_rms(yv[:, sl] * (zs[i] * _sigmoid(zs[i])), ws[:, sl]))
        return jnp.concatenate(outs, axis=1)

    cat = _rowwise(mix_norms_fn, t, tm, [_rows(o_att, tm), _rows(y_ssd, tm)] + z_ins + [_par(w_attn_n), _par(w_ssd_n)],
                   [_rout(t, d, bf16, tm)], [], "attn_ssd_out_norms")[0]
    mix = _matmul([(cat, w_out_f)], "nn", f32, "out_proj")

    def post_mix_fn(mx, xv, w1, w2):
        h1v = xv + _rms(mx, w1)
        return h1v, _rms(h1v, w2)

    h1, v_in = _rowwise(post_mix_fn, t, tmw, [_rows(mix, tmw), _rows(xs, tmw), _par(w_post_mix), _par(w_pre_ffn)],
                        [_rout(t, d, f32, tmw), _rout(t, d, bf16, tmw)], [], "post_mix_pre_ffn_norm")
    g_ff, u_ff, act = _ffn_up(v_in, w_gate_f, w_up_f)
    ffn = _matmul([(act, w_down_f)], "nn", f32, "ffn_down")

    def final_fn(fv, h1v, tg, w):
        h2 = h1v + _rms(fv, w)
        err = h2 - tg
        lpart = 0.5 * jnp.sum(jnp.sum(err * err, axis=1, keepdims=True), axis=0, keepdims=True) / d
        dh2 = err / d
        dff, dw = _rms_bwd(fv, w, dh2)
        return dff, dh2, jnp.broadcast_to(lpart, (1, LANES)), dw

    dffn, dh2, loss_acc, g_post_ffn = _rowwise(
        final_fn, t, tmw, [_rows(ffn, tmw), _rows(h1, tmw), _rows(tgt, tmw), _par(w_post_ffn)],
        [_rout(t, d, bf16, tmw), _rout(t, d, f32, tmw)], [(1, LANES), (1, d)], "loss_post_ffn_norm_bwd")
    loss = lax.psum(loss_acc[0, 0], ("x", "y", "c"))

    dg_ff, du_ff = _ffn_down_bwd(dffn, w_down_f, g_ff, u_ff)
    gw_down = _matmul([(act, dffn)], "tn", f32, "grad_w_down")
    gw_gate = _matmul([(v_in, dg_ff)], "tn", f32, "grad_w_gate", out_chunks=nchip)
    gw_up = _matmul([(v_in, du_ff)], "tn", f32, "grad_w_up", out_chunks=nchip)
    ffn_g = [gw_gate, gw_up, gw_down.reshape(nchip, -1, d)]
    dv_in, ffn_from_sib = _matmul([(dg_ff, w_gate_f), (du_ff, w_up_f)], "nt", f32, "ffn_up_bwd",
                                  comm=_sibling_send_halves(ffn_g), at=(0.0,))
    ffn_pairs = pair_sums(FFN, ffn_g, ffn_from_sib)

    def mid_bwd_fn(h1v, dvv, dh2v, mx, w_pf, w_pm):
        dxn, dw_pf = _rms_bwd(h1v, w_pf, dvv)
        dh1v = dh2v + dxn
        dmx, dw_pm = _rms_bwd(mx, w_pm, dh1v)
        return dh1v, dmx, dw_pf, dw_pm

    dh1, dmix, g_pre_ffn, g_post_mix = _rowwise(
        mid_bwd_fn, t, tmw, [_rows(h1, tmw), _rows(dv_in, tmw), _rows(dh2, tmw), _rows(mix, tmw),
                             _par(w_pre_ffn), _par(w_post_mix)],
        [_rout(t, d, f32, tmw), _rout(t, d, bf16, tmw)], [(1, d), (1, d)], "pre_ffn_post_mix_norm_bwd")
    dcat = _matmul([(dmix, w_out_f)], "nt", f32, "out_proj_bwd")
    gw_out = _matmul([(cat, dmix)], "tn", f32, "grad_w_out")

    def mix_norms_bwd_fn(ov, yv, *rest):
        zs, dcv, wa, ws = rest[:SSD_G], rest[SSD_G], rest[SSD_G + 1], rest[SSD_G + 2]
        dov, dwa = _rms_bwd(ov, wa, dcv[:, :mla_w])
        dl = [jnp.broadcast_to(jnp.sum(dov[:, h * V_HEAD:(h + 1) * V_HEAD] * ov[:, h * V_HEAD:(h + 1) * V_HEAD],
                                       axis=1, keepdims=True), (ov.shape[0], V_HEAD)) for h in range(nh)]
        dys, dzs, dws = [], [], []
        for i in range(SSD_G):
            sl = slice(i * gw, (i + 1) * gw)
            zv, yi = zs[i], yv[:, sl]
            sg = _sigmoid(zv)
            sz = zv * sg
            dgi, dwi = _rms_bwd(yi * sz, ws[:, sl], dcv[:, mla_w + i * gw: mla_w + (i + 1) * gw])
            dys.append(dgi * sz)
            dzs.append(dgi * yi * (sg * (1.0 + zv * (1.0 - sg))))
            dws.append(dwi)
        cc = lambda a: jnp.concatenate(a, axis=1)
        return dov, cc(dl), cc(dys), cc(dzs), dwa, cc(dws)

    do_att, delta, dy_ssd, dz, g_attn_n, g_ssd_n = _rowwise(
        mix_norms_bwd_fn, t, tm,
        [_rows(o_att, tm), _rows(y_ssd, tm)] + z_ins + [_rows(dcat, tm), _par(w_attn_n), _par(w_ssd_n)],
        [_rout(t, mla_w, bf16, tm), _rout(t, mla_w, f32, tm), _rout(t, width, f32, tm), _rout(t, width, bf16, tm)],
        [(1, mla_w), (1, width)], "attn_ssd_out_norms_bwd")
    dkv, dkr_h, ffn_from_chips = _flash_bwd_dkv(q2, kv, kr2, do_att, lse, delta, nh, scale,
                                                _chips_exchange(ffn_pairs), (0.0,))
    ffn_halves = chip_sums(FFN, ffn_g, ffn_from_sib, ffn_from_chips)
    dq2, ffn_sib_halves = _flash_bwd_dq(q2, kv, kr2, do_att, lse, delta, nh, scale, _sibling_swap(ffn_halves), (0.0,))

    def q_rope_bwd_fn(dqt, cp, sa, sb):
        parts = []
        for h in range(nh):
            parts.append(dqt[:, h * QK_PAD: h * QK_PAD + NOPE])
            parts.append(_rope_bwd(dqt[:, h * QK_PAD + NOPE:(h + 1) * QK_PAD], cp, sa, sb))
        return jnp.concatenate(parts, axis=1)

    dq_raw = _rowwise(q_rope_bwd_fn, t, tm, [_rows(dq2, tm), _rows(cosp, tm), _rows(sina, tm), _rows(sinb, tm)],
                      [_rout(t, nh * QK_PAD, bf16, tm)], [], "q_rope_bwd")[0]

    def k_rope_bwd_fn(dk, cp, sa, sb):
        tot = dk[:, 0:LANES]
        for h in range(1, nh):
            tot = tot + dk[:, h * LANES:(h + 1) * LANES]
        return _rope_bwd(tot, cp, sa, sb)

    dkr = _rowwise(k_rope_bwd_fn, t, tm, [_rows(dkr_h, tm), _rows(cosp, tm), _rows(sina, tm), _rows(sinb, tm)],
                   [_rout(t, LANES, f32, tm)], [], "k_rope_bwd")[0]
    gw_uq_p = _matmul([(cqn, dq_raw)], "tn", f32, "grad_w_uq")
    gw_ukv = _matmul([(ckvn, dkv)], "tn", f32, "grad_w_ukv", out_chunks=nchip)
    dcqn = _matmul([(dq_raw, w_uq_p)], "nt", f32, "q_up_bwd")
    dckvn = _matmul([(dkv, w_ukv_f)], "nt", f32, "kv_up_bwd")

    def lat_norm_bwd_fn(a, w, dyv):
        return _rms_bwd(a, w, dyv)

    dcq, g_qn = _rowwise(lat_norm_bwd_fn, t, tm, [cq_in, _par(w_qn), _rows(dcqn, tm)],
                         [_rout(t, Q_RANK, bf16, tm)], [(1, Q_RANK)], "q_norm_bwd")
    dckv, g_kvn = _rowwise(lat_norm_bwd_fn, t, tm, [ckv_in, _par(w_kvn), _rows(dckvn, tm)],
                           [_rout(t, KV_RANK, bf16, tm)], [(1, KV_RANK)], "kv_norm_bwd")

    dxbc_act, ddt_raw, g_alog, g_dskip, g_dtb = _ssd_bwd(
        xbc_act, dt_raw, dt_raw_t, dtb, dtb_t, alog, alog_t, dskip, hprev, dy_ssd, width)
    dpre, gcw0, gcw1, gcw2, gcw3, g_conv_b = _conv_bwd_pre(proj, conv_full, conv_b_r, dxbc_act, cdim, tm)
    g_conv_w = jnp.concatenate([gcw0, gcw1, gcw2, gcw3], axis=0)
    dxbc = _conv_bwd_dx(dpre, conv_full, tm)

    dtail = jnp.concatenate([dkr[:, :ROPE], ddt_raw, jnp.zeros((t, d_in_p - off_tail - ROPE - hs), f32)],
                            axis=1).astype(bf16)
    dproj = jnp.concatenate([dxbc, dcq, dckv, dz, dtail], axis=1)
    du_in = _matmul([(dproj, w_in_p)], "nt", f32, "in_proj_bwd")
    gw_in_p = _matmul([(u, dproj)], "tn", f32, "grad_w_in")

    def first_bwd_fn(xv, duv, dh1v, w):
        dxn, dw = _rms_bwd(xv, w, duv)
        return dh1v + dxn, dw

    grad_x, g_pre_mix = _rowwise(first_bwd_fn, t, tmw, [_rows(xs, tmw), _rows(du_in, tmw), _rows(dh1, tmw),
                                                         _par(w_pre_mix)],
                                 [_rout(t, d, f32, tmw)], [(1, d)], "pre_mix_norm_bwd")

    gseg = lambda a, b: gw_in_p[:, a:b]
    gw_in = jnp.concatenate([gseg(off_cq, off_ckv), gseg(off_ckv, off_z), gseg(off_tail, off_tail + ROPE),
                             gseg(off_z, off_tail), gseg(off_xbc, off_cq),
                             gseg(off_tail + ROPE, off_tail + ROPE + hs)], axis=1)
    gw_uq = gw_uq_p.reshape(Q_RANK, nh, QK_PAD)[:, :, :qk_head].reshape(Q_RANK, nh * qk_head)
    split_cols = lambda gf: jnp.stack(jnp.split(gf, nchip, axis=1))
    split_rows = lambda gf: gf.reshape(nchip, gf.shape[0] // nchip, gf.shape[1])
    mix_g = [split_cols(gw_in), split_cols(gw_uq), gw_ukv, split_rows(gw_out)]
    mix_from_sib = _comm_call(_sibling_send_halves(mix_g), "sibling_send_halves")
    mix_pairs = pair_sums(MIXER, mix_g, mix_from_sib)
    mix_from_chips = _comm_call(_chips_exchange(mix_pairs), "chips_exchange")
    mix_halves = chip_sums(MIXER, mix_g, mix_from_sib, mix_from_chips)
    mix_sib_halves = _comm_call(_sibling_swap(mix_halves), "sibling_swap")
    halves = mix_halves + ffn_halves
    sib_halves = mix_sib_halves + ffn_sib_halves
    gshard = {}

    gsmall = {"q_norm_w": g_qn, "kv_norm_w": g_kvn, "conv_w": g_conv_w, "conv_b": g_conv_b, "dt_bias": g_dtb,
              "a_log": g_alog, "d_skip": g_dskip, "ssd_norm_w": g_ssd_n, "attn_out_norm_w": g_attn_n,
              "pre_mix_norm_w": g_pre_mix, "post_mix_norm_w": g_post_mix, "pre_ffn_norm_w": g_pre_ffn,
              "post_ffn_norm_w": g_post_ffn}
    small_sizes = [int(np.prod(gsmall[n].shape)) for n in SMALL]
    srows = _round_up(-(-sum(small_sizes) // LANES), 8)
    spart = _flat_pad([gsmall[n] for n in SMALL], srows * LANES).reshape(srows, LANES)
    sall = _gather_all(spart)

    def sum8_fn(a):
        tot = a[0]
        for k in range(1, 8):
            tot = tot + a[k]
        return tot

    ssum = _blocked(sum8_fn, (1,), [(sall, sall.shape, lambda i: (0, 0, 0))],
                    [((srows, LANES), f32, (srows, LANES), lambda i: (0, 0))], [], "small_grad_sum")[0].reshape(-1)
    gred = {}
    off = 0
    for n, sz in zip(SMALL, small_sizes):
        gred[n] = ssum[off:off + sz].reshape(gsmall[n].shape)
        off += sz
    gshard["conv_w"] = lax.dynamic_slice_in_dim(gred["conv_w"], my_chip * ccs, ccs, axis=1)
    for n in SMALL:
        if n != "conv_w":
            gshard[n] = gred[n].reshape(wts[n].shape)

    delta, new_m, new_v = {}, {}, {}
    for n, mine_h, sib_h in zip(BIG, halves, sib_halves):
        gshard[n], delta[n], new_m[n], new_v[n] = _adamw_halves(wts[n], mom_m[n], mom_v[n], mine_h, sib_h, sp,
                                                                "adamw_" + n)
    pack = lambda src: _flat_pad([src[n] for n in SMALL], srows * LANES).reshape(srows, LANES)
    sd, sm, sv = _adamw_call(pack(wts), pack(gshard), pack(mom_m), pack(mom_v), "adamw_small")
    off = 0
    for n in SMALL:
        sz = int(np.prod(wts[n].shape))
        for dst, src in ((delta, sd), (new_m, sm), (new_v, sv)):
            dst[n] = src.reshape(-1)[off:off + sz].reshape(wts[n].shape)
        off += sz

    lead = lambda a: a[None]
    return (loss, grad_x[None], *[lead(gshard[n]) for n in ORDER], *[lead(delta[n]) for n in ORDER],
            *[lead(new_m[n]) for n in ORDER], *[lead(new_v[n]) for n in ORDER])
```

```python
import functools

import numpy as np
import jax
import jax.numpy as jnp
from jax import lax
from jax.experimental import pallas as pl
from jax.experimental.pallas import tpu as pltpu

f32, bf16 = jnp.float32, jnp.bfloat16

EPS = 1e-6
V_HEAD = 128
NOPE = 128
ROPE = 64
QK_PAD = 256
Q_RANK = 512
KV_RANK = 512
ROPE_THETA = 10000.0
SSD_P = 64
SSD_G = 2
SSD_N = 128
SSD_K = 4
CHUNK = 128
ADAM_LR, ADAM_B1, ADAM_B2, ADAM_EPS, ADAM_WD, ADAM_STEP = 0.001, 0.9, 0.999, 1e-08, 0.01, 10

VMEM_LIMIT_BYTES = 48 * 1024 * 1024
LANES = 128
ATT_TILE = 512
MM_TM, MM_TN, MM_TK = 1024, 1024, 1024
CHUNK_WHOLE_MAX = 1536
ROW_TILE = 256

NN = (((1,), (0,)), ((), ()))
NT = (((1,), (1,)), ((), ()))
TN = (((0,), (0,)), ((), ()))
MESH = pl.DeviceIdType.MESH
ANY = pl.BlockSpec(memory_space=pl.ANY)


def _tile(dim, cap, align=LANES):
    if dim <= cap:
        return dim
    t = (cap // align) * align
    while t >= align:
        if dim % t == 0:
            return t
        t -= align
    raise ValueError(f"no tile for {dim} under {cap}")


def _round_up(n, m):
    return -(-n // m) * m


def _params(sem):
    return pltpu.CompilerParams(dimension_semantics=sem, vmem_limit_bytes=VMEM_LIMIT_BYTES)


def _dot(a, b, dims):
    return lax.dot_general(a.astype(bf16), b.astype(bf16), dims, preferred_element_type=f32)


def _call(body, name, out_shape, grid, in_specs, out_specs, scratch_shapes, sem, args, comm=None, at=None,
          prefetch=()):
    npf = len(prefetch)
    if comm is None:
        gs = pltpu.PrefetchScalarGridSpec(num_scalar_prefetch=npf, grid=grid, in_specs=list(in_specs),
                                          out_specs=list(out_specs), scratch_shapes=list(scratch_shapes))
        res = pl.pallas_call(body, name=name, out_shape=list(out_shape), grid_spec=gs,
                             compiler_params=_params(sem))(*prefetch, *args)
        return list(res), []
    n_in, n_out, n_sc = len(args), len(out_shape), len(scratch_shapes)
    na, no = len(comm.args), len(comm.outs)
    steps = int(np.prod(grid))

    def full(*allrefs):
        pf, refs = allrefs[:npf], allrefs[npf:]
        cin = refs[n_in:n_in + na]
        o0 = n_in + na
        cout = refs[o0 + n_out:o0 + n_out + no]
        s0 = o0 + n_out + no
        send_sems, recv_sems = refs[s0 + n_sc], refs[s0 + n_sc + 1]
        lin = pl.program_id(0)
        for dim in range(1, len(grid)):
            lin = lin * grid[dim] + pl.program_id(dim)
        for p in range(comm.nphase - 1):
            @pl.when(lin == int(round(at[p] * (steps - 1))))
            def _(p=p):
                comm.run(p, cin, cout, send_sems, recv_sems)
        body(*pf, *refs[:n_in], *refs[o0:o0 + n_out], *refs[s0:s0 + n_sc])

        @pl.when(lin == steps - 1)
        def _():
            comm.run(comm.nphase - 1, cin, cout, send_sems, recv_sems)

    gs = pltpu.PrefetchScalarGridSpec(
        num_scalar_prefetch=npf, grid=grid, in_specs=list(in_specs) + [ANY] * na,
        out_specs=list(out_specs) + [ANY] * no, scratch_shapes=list(scratch_shapes) + comm.sems())
    res = pl.pallas_call(full, name=name, out_shape=list(out_shape) + comm.outs, grid_spec=gs,
                         compiler_params=_params(("arbitrary",) * len(grid)))(*prefetch, *args, *comm.args)
    return list(res[:n_out]), list(res[n_out:])


def _chunk_tile(cs, cap):
    return cs if cs <= CHUNK_WHOLE_MAX else _tile(cs, cap)


def _matmul(pairs, mode, out_dtype, name, out_chunks=None, comm=None, at=None):
    a0, b0 = pairs[0]
    chunked = b0.ndim == 3
    cs = b0.shape[2] if chunked else None
    bcols = b0.shape[0] * b0.shape[2] if chunked else b0.shape[1]
    brows = b0.shape[1] if chunked else b0.shape[0]
    if mode == "nn":
        (m, k), n = a0.shape, bcols
    elif mode == "nt":
        (m, k), n = a0.shape, brows
    else:
        (k, m), n = a0.shape, bcols
    tm = _tile(m, MM_TM)
    if mode == "nt":
        tn = _tile(n, MM_TN)
        tk = _chunk_tile(cs, MM_TK) if chunked else _tile(k, MM_TK)
    else:
        tk = _tile(k, MM_TK)
        if chunked:
            tn = _chunk_tile(cs, MM_TN)
        elif out_chunks:
            tn = _chunk_tile(n // out_chunks, MM_TN)
        else:
            tn = _tile(n, MM_TN)
    nk = k // tk
    if mode == "nn":
        a_spec = pl.BlockSpec((tm, tk), lambda i, j, kk: (i, kk))
        if chunked:
            q = cs // tn
            b_spec = pl.BlockSpec((None, tk, tn), lambda i, j, kk: (j // q, kk, j % q))
        else:
            b_spec = pl.BlockSpec((tk, tn), lambda i, j, kk: (kk, j))
        dims = NN
    elif mode == "nt":
        a_spec = pl.BlockSpec((tm, tk), lambda i, j, kk: (i, kk))
        if chunked:
            q = cs // tk
            b_spec = pl.BlockSpec((None, tn, tk), lambda i, j, kk: (kk // q, j, kk % q))
        else:
            b_spec = pl.BlockSpec((tn, tk), lambda i, j, kk: (j, kk))
        dims = NT
    else:
        a_spec = pl.BlockSpec((tk, tm), lambda i, j, kk: (kk, i))
        b_spec = pl.BlockSpec((tk, tn), lambda i, j, kk: (kk, j))
        dims = TN
    if out_chunks:
        qo = (n // out_chunks) // tn
        out_shape = jax.ShapeDtypeStruct((out_chunks, m, n // out_chunks), out_dtype)
        o_spec = pl.BlockSpec((None, tm, tn), lambda i, j, kk: (j // qo, i, j % qo))
    else:
        out_shape = jax.ShapeDtypeStruct((m, n), out_dtype)
        o_spec = pl.BlockSpec((tm, tn), lambda i, j, kk: (i, j))
    npair = len(pairs)

    def body(*refs):
        o_ref, acc = refs[2 * npair], refs[2 * npair + 1]
        kk = pl.program_id(2)

        @pl.when(kk == 0)
        def _():
            acc[...] = jnp.zeros_like(acc)

        part = _dot(refs[0][...], refs[1][...], dims)
        for p in range(1, npair):
            part = part + _dot(refs[2 * p][...], refs[2 * p + 1][...], dims)
        acc[...] += part

        @pl.when(kk == nk - 1)
        def _():
            o_ref[...] = acc[...].astype(out_dtype)

    args = [t for pr in pairs for t in pr]
    res, cres = _call(body, name, [out_shape], (m // tm, n // tn, nk), [a_spec, b_spec] * npair, [o_spec],
                      [pltpu.VMEM((tm, tn), f32)], ("parallel", "parallel", "arbitrary"), args, comm, at)
    return res[0] if comm is None else (res[0], cres)


def _sigmoid(x):
    return 1.0 / (1.0 + jnp.exp(-x))


def _ffn_up(v, wg, wu):
    m, k = v.shape
    nchunk, _, cs = wg.shape
    n = nchunk * cs
    tm, tn, tk = _tile(m, 512), _chunk_tile(cs, 512), _tile(k, MM_TK)
    nk = k // tk
    q = cs // tn
    w_spec = pl.BlockSpec((None, tk, tn), lambda i, j, kk: (j // q, kk, j % q))

    def body(v_ref, wg_ref, wu_ref, g_ref, u_ref, act_ref, accg, accu):
        kk = pl.program_id(2)

        @pl.when(kk == 0)
        def _():
            accg[...] = jnp.zeros_like(accg)
            accu[...] = jnp.zeros_like(accu)

        vb = v_ref[...]
        accg[...] += _dot(vb, wg_ref[...], NN)
        accu[...] += _dot(vb, wu_ref[...], NN)

        @pl.when(kk == nk - 1)
        def _():
            g, u = accg[...], accu[...]
            g_ref[...] = g.astype(bf16)
            u_ref[...] = u.astype(bf16)
            act_ref[...] = (g * _sigmoid(g) * u).astype(bf16)

    out = jax.ShapeDtypeStruct((m, n), bf16)
    o_spec = pl.BlockSpec((tm, tn), lambda i, j, kk: (i, j))
    return pl.pallas_call(
        body, name="ffn_up", out_shape=(out, out, out), grid=(m // tm, n // tn, nk),
        in_specs=[pl.BlockSpec((tm, tk), lambda i, j, kk: (i, kk)), w_spec, w_spec],
        out_specs=(o_spec, o_spec, o_spec),
        scratch_shapes=[pltpu.VMEM((tm, tn), f32), pltpu.VMEM((tm, tn), f32)],
        compiler_params=_params(("parallel", "parallel", "arbitrary")),
    )(v, wg, wu)


def _ffn_down_bwd(dffn, wd, g, u):
    m, k = dffn.shape
    n = wd.shape[0]
    tm, tn, tk = _tile(m, MM_TM), _tile(n, 512), _tile(k, MM_TK)
    nk = k // tk

    def body(d_ref, w_ref, g_ref, u_ref, dg_ref, du_ref, acc):
        kk = pl.program_id(2)

        @pl.when(kk == 0)
        def _():
            acc[...] = jnp.zeros_like(acc)

        acc[...] += _dot(d_ref[...], w_ref[...], NT)

        @pl.when(kk == nk - 1)
        def _():
            dact = acc[...]
            gg, uu = g_ref[...].astype(f32), u_ref[...].astype(f32)
            sg = _sigmoid(gg)
            du_ref[...] = (dact * gg * sg).astype(bf16)
            dg_ref[...] = (dact * uu * (sg * (1.0 + gg * (1.0 - sg)))).astype(bf16)

    out = jax.ShapeDtypeStruct((m, n), bf16)
    o_spec = pl.BlockSpec((tm, tn), lambda i, j, kk: (i, j))
    return pl.pallas_call(
        body, name="ffn_down_bwd", out_shape=(out, out), grid=(m // tm, n // tn, nk),
        in_specs=[pl.BlockSpec((tm, tk), lambda i, j, kk: (i, kk)),
                  pl.BlockSpec((tn, tk), lambda i, j, kk: (j, kk)), o_spec, o_spec],
        out_specs=(o_spec, o_spec),
        scratch_shapes=[pltpu.VMEM((tm, tn), f32)],
        compiler_params=_params(("parallel", "parallel", "arbitrary")),
    )(dffn, wd, g, u)


def _blocked(fn, grid, ins, outs, accs, name, sp=None):
    n_in, n_out, n_acc = len(ins), len(outs), len(accs)
    nsp = 0 if sp is None else 1

    def body(*refs):
        refs = refs[nsp:]
        tiles = [r[...] for r in refs[:n_in]]
        res = fn(*tiles)
        if not isinstance(res, (tuple, list)):
            res = (res,)
        for r, val in zip(refs[n_in:n_in + n_out], res[:n_out]):
            r[...] = val.astype(r.dtype)
        if n_acc:
            first = pl.program_id(0) == 0
            for d in range(1, len(grid)):
                first = jnp.logical_and(first, pl.program_id(d) == 0)

            @pl.when(first)
            def _():
                for r in refs[n_in + n_out:]:
                    r[...] = jnp.zeros_like(r)

            for r, val in zip(refs[n_in + n_out:], res[n_out:]):
                r[...] += val

    def acc_map(shape):
        zeros = (0,) * len(shape)
        return lambda *idx: zeros

    in_specs = [pl.BlockSpec(bs, im) for _, bs, im in ins]
    out_specs = [pl.BlockSpec(bs, im) for _, _, bs, im in outs] + [pl.BlockSpec(s, acc_map(s)) for s in accs]
    out_shape = [jax.ShapeDtypeStruct(s, d) for s, d, _, _ in outs] + [jax.ShapeDtypeStruct(s, f32) for s in accs]
    sem = ("arbitrary",) * len(grid) if n_acc else ("parallel",) * len(grid)
    args = [a for a, _, _ in ins]
    if sp is None:
        res = pl.pallas_call(body, name=name, out_shape=out_shape, grid=grid, in_specs=in_specs,
                             out_specs=out_specs, compiler_params=_params(sem))(*args)
    else:
        gs = pltpu.PrefetchScalarGridSpec(num_scalar_prefetch=1, grid=grid, in_specs=in_specs, out_specs=out_specs)
        res = pl.pallas_call(body, name=name, out_shape=out_shape, grid_spec=gs,
                             compiler_params=_params(sem))(sp, *args)
    return res


def _rows(a, tm, cols=None, cb=0):
    w = a.shape[1] if cols is None else cols
    return (a, (tm, w), lambda i: (i, cb))


def _par(a):
    zeros = (0,) * a.ndim
    return (a, a.shape, lambda i: zeros)


def _rout(t, w, dtype, tm):
    return ((t, w), dtype, (tm, w), lambda i: (i, 0))


def _rowwise(fn, t, tm, ins, outs, accs, name):
    return _blocked(fn, (t // tm,), ins, outs, accs, name)


def _rms(x, w):
    r = lax.rsqrt(jnp.mean(x * x, axis=-1, keepdims=True) + EPS)
    return x * r * w


def _rms_bwd(x, w, dy):
    r = lax.rsqrt(jnp.mean(x * x, axis=-1, keepdims=True) + EPS)
    xh = x * r
    dyw = dy * w
    dx = r * (dyw - xh * jnp.mean(dyw * xh, axis=-1, keepdims=True))
    return dx, jnp.sum(dy * xh, axis=0, keepdims=True)


def _silu_grad(x):
    s = _sigmoid(x)
    return s * (1.0 + x * (1.0 - s))


def _rope(blk, cosp, sina, sinb):
    return blk * cosp + pltpu.roll(blk, 96, 1) * sina + pltpu.roll(blk, 32, 1) * sinb


def _rope_bwd(dy, cosp, sina, sinb):
    return dy * cosp + pltpu.roll(dy * sina, 32, 1) + pltpu.roll(dy * sinb, 96, 1)


HALO = 8


def _conv_taps(buf, w, tm, base):
    acc = buf[base:base + tm, :] * w[0:1]
    for k in range(1, SSD_K):
        acc = acc + buf[base + k:base + k + tm, :] * w[k:k + 1]
    return acc


def _conv_specs(t, tm, cdim):
    cur = pl.BlockSpec((tm, cdim), lambda i: (i, 0))
    prev = pl.BlockSpec((HALO, cdim), lambda i: (jnp.maximum(i * (tm // HALO) - 1, 0), 0))
    nxt = pl.BlockSpec((HALO, cdim), lambda i: (jnp.minimum((i + 1) * (tm // HALO), t // HALO - 1), 0))
    return cur, prev, nxt


def _conv_fwd(src, w, b, cdim, tm):
    t = src.shape[0]
    cur, prev, _ = _conv_specs(t, tm, cdim)

    def body(x_ref, p_ref, w_ref, b_ref, o_ref, buf):
        buf[0:HALO, :] = jnp.where(pl.program_id(0) > 0, p_ref[...], 0.0)
        buf[HALO:HALO + tm, :] = x_ref[...]
        pre = _conv_taps(buf, w_ref[...], tm, HALO - (SSD_K - 1)) + b_ref[...]
        o_ref[...] = pre * _sigmoid(pre)

    par = lambda a: pl.BlockSpec(a.shape, lambda i: (0, 0))
    return pl.pallas_call(
        body, name="conv_silu", out_shape=jax.ShapeDtypeStruct((t, cdim), f32), grid=(t // tm,),
        in_specs=[cur, prev, par(w), par(b)], out_specs=cur,
        scratch_shapes=[pltpu.VMEM((tm + HALO, cdim), f32)], compiler_params=_params(("parallel",)),
    )(src, src, w, b)


def _conv_bwd_pre(src, w, b, dact, cdim, tm):
    t = src.shape[0]
    cur, prev, _ = _conv_specs(t, tm, cdim)

    def body(x_ref, p_ref, w_ref, b_ref, d_ref, dpre_ref, dw0, dw1, dw2, dw3, db, buf):
        @pl.when(pl.program_id(0) == 0)
        def _():
            for r in (dw0, dw1, dw2, dw3, db):
                r[...] = jnp.zeros_like(r)

        buf[0:HALO, :] = jnp.where(pl.program_id(0) > 0, p_ref[...], 0.0)
        buf[HALO:HALO + tm, :] = x_ref[...]
        base = HALO - (SSD_K - 1)
        pre = _conv_taps(buf, w_ref[...], tm, base) + b_ref[...]
        dpre = d_ref[...] * _silu_grad(pre)
        dpre_ref[...] = dpre
        for k, r in enumerate((dw0, dw1, dw2, dw3)):
            r[...] += jnp.sum(dpre * buf[base + k:base + k + tm, :], axis=0, keepdims=True)
        db[...] += jnp.sum(dpre, axis=0, keepdims=True)

    par = lambda a: pl.BlockSpec(a.shape, lambda i: (0, 0))
    acc = pl.BlockSpec((1, cdim), lambda i: (0, 0))
    acc_shape = jax.ShapeDtypeStruct((1, cdim), f32)
    return pl.pallas_call(
        body, name="conv_silu_bwd", out_shape=[jax.ShapeDtypeStruct((t, cdim), f32)] + [acc_shape] * 5,
        grid=(t // tm,), in_specs=[cur, prev, par(w), par(b), cur], out_specs=[cur] + [acc] * 5,
        scratch_shapes=[pltpu.VMEM((tm + HALO, cdim), f32)], compiler_params=_params(("arbitrary",)),
    )(src, src, w, b, dact)


def _conv_bwd_dx(dpre, w, tm):
    t, cdim = dpre.shape
    cur, _, nxt = _conv_specs(t, tm, cdim)
    last = t // tm - 1

    def body(d_ref, n_ref, w_ref, o_ref, buf):
        buf[0:tm, :] = d_ref[...]
        buf[tm:tm + HALO, :] = jnp.where(pl.program_id(0) < last, n_ref[...], 0.0)
        wv = w_ref[...]
        acc = buf[0:tm, :] * wv[SSD_K - 1:SSD_K]
        for k in range(SSD_K - 1):
            s = SSD_K - 1 - k
            acc = acc + buf[s:s + tm, :] * wv[k:k + 1]
        o_ref[...] = acc.astype(bf16)

    return pl.pallas_call(
        body, name="conv_bwd_dx", out_shape=jax.ShapeDtypeStruct((t, cdim), bf16), grid=(t // tm,),
        in_specs=[cur, nxt, pl.BlockSpec(w.shape, lambda i: (0, 0))], out_specs=cur,
        scratch_shapes=[pltpu.VMEM((tm + HALO, cdim), f32)], compiler_params=_params(("parallel",)),
    )(dpre, dpre, w)


def _causal_mask(s):
    row = lax.broadcasted_iota(jnp.int32, s.shape, 0)
    col = lax.broadcasted_iota(jnp.int32, s.shape, 1)
    return jnp.where(row >= col, s, -jnp.inf)


def _causal_pairs(nq, q_major):
    pairs = [(qi, ki) for qi in range(nq) for ki in range(qi + 1)]
    if not q_major:
        pairs.sort(key=lambda p: (p[1], p[0]))
    return (jnp.asarray([p[0] for p in pairs], jnp.int32), jnp.asarray([p[1] for p in pairs], jnp.int32))


def _flash_fwd(q, kv, kr, nheads, scale, comm=None, at=None):
    t = q.shape[0]
    tq = _tile(t, ATT_TILE)
    nq = t // tq

    qtab, ktab = _causal_pairs(nq, q_major=True)

    def body(qt, kt, q_ref, kn_ref, kr_ref, v_ref, o_ref, lse_ref, m_sc, l_sc, acc_sc):
        qi, ki = qt[pl.program_id(1)], kt[pl.program_id(1)]

        @pl.when(ki == 0)
        def _():
            m_sc[...] = jnp.full_like(m_sc, -jnp.inf)
            l_sc[...] = jnp.zeros_like(l_sc)
            acc_sc[...] = jnp.zeros_like(acc_sc)

        def step(masked):
            k = jnp.concatenate([kn_ref[...], kr_ref[...]], axis=1)
            s = lax.dot_general(q_ref[...], k, NT, preferred_element_type=f32) * scale
            if masked:
                s = _causal_mask(s)
            m_old = m_sc[...]
            m_new = jnp.maximum(m_old, jnp.max(s, axis=1, keepdims=True))
            alpha = jnp.exp(m_old - m_new)
            p = jnp.exp(s - m_new)
            l_sc[...] = alpha * l_sc[...] + jnp.sum(p, axis=1, keepdims=True)
            acc_sc[...] = alpha * acc_sc[...] + lax.dot_general(p.astype(bf16), v_ref[...], NN,
                                                                preferred_element_type=f32)
            m_sc[...] = m_new

        @pl.when(ki < qi)
        def _():
            step(False)

        @pl.when(ki == qi)
        def _():
            step(True)
            l = l_sc[...]
            o_ref[...] = acc_sc[...] / l
            lse_ref[...] = jnp.broadcast_to(m_sc[...] + jnp.log(l), lse_ref.shape)

    o_spec = pl.BlockSpec((tq, V_HEAD), lambda h, s, qt, kt: (qt[s], h))
    out = jax.ShapeDtypeStruct((t, nheads * V_HEAD), f32)
    (o, lse), cres = _call(
        body, "flash_fwd", [out, out], (nheads, len(qtab)),
        [pl.BlockSpec((tq, QK_PAD), lambda h, s, qt, kt: (qt[s], h)),
         pl.BlockSpec((tq, NOPE), lambda h, s, qt, kt: (kt[s], 2 * h)),
         pl.BlockSpec((tq, LANES), lambda h, s, qt, kt: (kt[s], 0)),
         pl.BlockSpec((tq, V_HEAD), lambda h, s, qt, kt: (kt[s], 2 * h + 1))],
        [o_spec, o_spec],
        [pltpu.VMEM((tq, 1), f32), pltpu.VMEM((tq, 1), f32), pltpu.VMEM((tq, V_HEAD), f32)],
        ("parallel", "arbitrary"), [q, kv, kr, kv], comm, at, prefetch=(qtab, ktab))
    return o, lse, cres


def _flash_bwd(q, kv, kr, do, lse, delta, nheads, scale, comm=None, at=None):
    t = q.shape[0]
    tq = _tile(t, ATT_TILE)
    nq = t // tq
    qtab, ktab = _causal_pairs(nq, q_major=False)

    def body(qt, kt, q_ref, kn_ref, kr_ref, v_ref, do_ref, lse_ref, dl_ref, dkv_ref, dkr_ref, dq_ref, dk_sc, dv_sc):
        qi, ki = qt[pl.program_id(1)], kt[pl.program_id(1)]

        @pl.when(pl.program_id(1) == 0)
        def _():
            dq_ref[...] = jnp.zeros_like(dq_ref)

        @pl.when(qi == ki)
        def _():
            dk_sc[...] = jnp.zeros_like(dk_sc)
            dv_sc[...] = jnp.zeros_like(dv_sc)

        def step(masked):
            qb = q_ref[...]
            dob = do_ref[...]
            k = jnp.concatenate([kn_ref[...], kr_ref[...]], axis=1)
            s = lax.dot_general(qb, k, NT, preferred_element_type=f32) * scale
            if masked:
                s = _causal_mask(s)
            p = jnp.exp(s - lse_ref[:, 0:1])
            dv_sc[...] += lax.dot_general(p.astype(bf16), dob, TN, preferred_element_type=f32)
            dp = lax.dot_general(dob, v_ref[...], NT, preferred_element_type=f32)
            ds = (p * (dp - dl_ref[:, 0:1]) * scale).astype(bf16)
            dk_sc[...] += lax.dot_general(ds, qb, TN, preferred_element_type=f32)
            rows = pl.ds(pl.multiple_of(qi * tq, tq), tq)
            dq_ref[rows, :] += lax.dot_general(ds, k, NN, preferred_element_type=f32)

        @pl.when(qi > ki)
        def _():
            step(False)

        @pl.when(qi == ki)
        def _():
            step(True)

        @pl.when(qi == nq - 1)
        def _():
            dk = dk_sc[...]
            dkv_ref[...] = jnp.concatenate([dk[:, :NOPE], dv_sc[...]], axis=1).astype(bf16)
            dkr_ref[...] = dk[:, NOPE:]

    hspec = pl.BlockSpec((tq, V_HEAD), lambda h, s, qt, kt: (qt[s], h))
    (dkv, dkr, dq), cres = _call(
        body, "flash_bwd",
        [jax.ShapeDtypeStruct((t, nheads * QK_PAD), bf16), jax.ShapeDtypeStruct((t, nheads * LANES), f32),
         jax.ShapeDtypeStruct((t, nheads * QK_PAD), f32)],
        (nheads, len(qtab)),
        [pl.BlockSpec((tq, QK_PAD), lambda h, s, qt, kt: (qt[s], h)),
         pl.BlockSpec((tq, NOPE), lambda h, s, qt, kt: (kt[s], 2 * h)),
         pl.BlockSpec((tq, LANES), lambda h, s, qt, kt: (kt[s], 0)),
         pl.BlockSpec((tq, V_HEAD), lambda h, s, qt, kt: (kt[s], 2 * h + 1)),
         hspec, hspec, hspec],
        [pl.BlockSpec((tq, QK_PAD), lambda h, s, qt, kt: (kt[s], h)),
         pl.BlockSpec((tq, LANES), lambda h, s, qt, kt: (kt[s], h)),
         pl.BlockSpec((t, QK_PAD), lambda h, s, qt, kt: (0, h))],
        [pltpu.VMEM((tq, QK_PAD), f32), pltpu.VMEM((tq, V_HEAD), f32)],
        ("parallel", "arbitrary"), [q, kv, kr, kv, do, lse, delta], comm, at, prefetch=(qtab, ktab))
    return dkv, dkr, dq, cres


def _split3(a):
    hi = a.astype(bf16)
    r1 = a - hi.astype(f32)
    mid = r1.astype(bf16)
    lo = (r1 - mid.astype(f32)).astype(bf16)
    return hi, mid, lo


def _ones_dot_left(tri, a):
    hi, mid, lo = _split3(a)
    d = lambda v: lax.dot_general(tri, v, NN, preferred_element_type=f32)
    return d(hi) + d(mid) + d(lo)


def _ones_dot_right(a, tri):
    hi, mid, lo = _split3(a)
    d = lambda v: lax.dot_general(v, tri, NN, preferred_element_type=f32)
    return d(hi) + d(mid) + d(lo)


def _softplus(x):
    return jnp.maximum(x, 0.0) + jnp.log(1.0 + jnp.exp(-jnp.abs(x)))


def _ssd_common(dt_ref, dtT_ref, dtb_ref, dtbT_ref, alog_ref, alogT_ref):
    ii = lax.broadcasted_iota(jnp.int32, (CHUNK, CHUNK), 0)
    jj = lax.broadcasted_iota(jnp.int32, (CHUNK, CHUNK), 1)
    tri = ii >= jj
    raw = dt_ref[...] + dtb_ref[...]
    dt = _softplus(raw)
    a_neg = -jnp.exp(alog_ref[...])
    cum = _ones_dot_left(tri.astype(bf16), dt * a_neg)
    dt_t = _softplus(dtT_ref[...] + dtbT_ref[...])
    cum_t = _ones_dot_right(dt_t * (-jnp.exp(alogT_ref[...])), (ii <= jj).astype(bf16))
    return tri, raw, dt, a_neg, cum, cum_t


def _ssd_fwd(xbc, dt, dt_t, dtb, dtb_t, alog, alog_t, dskip, width):
    t, cdim = xbc.shape
    hs = dt.shape[1]
    nc = t // CHUNK
    epg = hs // SSD_G
    gn = SSD_G * SSD_N

    def body(x_ref, dt_ref, dtT_ref, dtb_ref, dtbT_ref, alog_ref, alogT_ref, d_ref, y_ref, hp_ref, h_sc):
        @pl.when(pl.program_id(0) == 0)
        def _():
            h_sc[...] = jnp.zeros_like(h_sc)

        tri, _, dtv, _, cum, cum_t = _ssd_common(dt_ref, dtT_ref, dtb_ref, dtbT_ref, alog_ref, alogT_ref)
        dsk = d_ref[...]
        for g in range(SSD_G):
            bb = x_ref[:, width + g * SSD_N: width + (g + 1) * SSD_N].astype(bf16)
            cb_ = x_ref[:, width + gn + g * SSD_N: width + gn + (g + 1) * SSD_N].astype(bf16)
            cbm = lax.dot_general(cb_, bb, NT, preferred_element_type=f32)
            for e in range(g * epg, (g + 1) * epg):
                xe = x_ref[:, e * SSD_P:(e + 1) * SSD_P]
                dtc = dtv[:, e:e + 1]
                cc = cum[:, e:e + 1]
                clast = cum[CHUNK - 1:CHUNK, e:e + 1]
                seg = cc - cum_t[e:e + 1, :]
                lmat = jnp.exp(jnp.where(tri, seg, -jnp.inf))
                xdt = xe * dtc
                h = h_sc[e]
                hp_ref[0, e] = h
                y = _dot(cbm * lmat, xdt, NN)
                y = y + _dot(cb_, h, NT) * jnp.exp(cc)
                y_ref[:, e * SSD_P:(e + 1) * SSD_P] = y + xe * dsk[:, e:e + 1]
                st = _dot(xdt * jnp.exp(clast - cc), bb, TN)
                h_sc[e] = h * jnp.exp(clast) + st

    par = lambda a: pl.BlockSpec(a.shape, lambda i: (0,) * a.ndim)
    return pl.pallas_call(
        body, name="ssd_fwd",
        out_shape=(jax.ShapeDtypeStruct((t, width), f32), jax.ShapeDtypeStruct((nc, hs, SSD_P, SSD_N), f32)),
        grid=(nc,),
        in_specs=[pl.BlockSpec((CHUNK, cdim), lambda i: (i, 0)), pl.BlockSpec((CHUNK, hs), lambda i: (i, 0)),
                  pl.BlockSpec((hs, CHUNK), lambda i: (0, i)), par(dtb), par(dtb_t), par(alog), par(alog_t), par(dskip)],
        out_specs=(pl.BlockSpec((CHUNK, width), lambda i: (i, 0)),
                   pl.BlockSpec((1, hs, SSD_P, SSD_N), lambda i: (i, 0, 0, 0))),
        scratch_shapes=[pltpu.VMEM((hs, SSD_P, SSD_N), f32)],
        compiler_params=_params(("arbitrary",)),
    )(xbc, dt, dt_t, dtb, dtb_t, alog, alog_t, dskip)


def _ssd_bwd(xbc, dt, dt_t, dtb, dtb_t, alog, alog_t, dskip, hprev, dy, width, comm=None, at=None):
    t, cdim = xbc.shape
    hs = dt.shape[1]
    nc = t // CHUNK
    epg = hs // SSD_G
    gn = SSD_G * SSD_N

    def body(x_ref, dt_ref, dtT_ref, dtb_ref, dtbT_ref, alog_ref, alogT_ref, d_ref, hp_ref, dy_ref,
             dx_ref, ddt_ref, dalog_ref, ddsk_ref, ddtb_ref, dh_sc):
        @pl.when(pl.program_id(0) == 0)
        def _():
            dh_sc[...] = jnp.zeros_like(dh_sc)
            dalog_ref[...] = jnp.zeros_like(dalog_ref)
            ddsk_ref[...] = jnp.zeros_like(ddsk_ref)
            ddtb_ref[...] = jnp.zeros_like(ddtb_ref)

        tri, raw, dtv, a_neg, cum, cum_t = _ssd_common(dt_ref, dtT_ref, dtb_ref, dtbT_ref, alog_ref, alogT_ref)
        dsk = d_ref[...]
        head_iota = lax.broadcasted_iota(jnp.int32, (1, hs), 1)
        last_row = (lax.broadcasted_iota(jnp.int32, (CHUNK, 1), 0) == CHUNK - 1).astype(f32)
        dcum_all = jnp.zeros((CHUNK, hs), f32)
        ddtx_all = jnp.zeros((CHUNK, hs), f32)
        ddsk = jnp.zeros((1, hs), f32)
        for g in range(SSD_G):
            bsl = slice(width + g * SSD_N, width + (g + 1) * SSD_N)
            csl = slice(width + gn + g * SSD_N, width + gn + (g + 1) * SSD_N)
            bb = x_ref[:, bsl].astype(bf16)
            cb_ = x_ref[:, csl].astype(bf16)
            cbm = lax.dot_general(cb_, bb, NT, preferred_element_type=f32)
            dbg = jnp.zeros((CHUNK, SSD_N), f32)
            dcg = jnp.zeros((CHUNK, SSD_N), f32)
            for e in range(g * epg, (g + 1) * epg):
                onehot = (head_iota == e).astype(f32)
                esl = slice(e * SSD_P, (e + 1) * SSD_P)
                xe = x_ref[:, esl]
                dye = dy_ref[:, esl]
                dtc = dtv[:, e:e + 1]
                cc = cum[:, e:e + 1]
                clast = cum[CHUNK - 1:CHUNK, e:e + 1]
                lmat = jnp.exp(jnp.where(tri, cc - cum_t[e:e + 1, :], -jnp.inf))
                gmat = cbm * lmat
                xdt = xe * dtc
                ee = jnp.exp(cc)
                ff = jnp.exp(clast - cc)
                dec = jnp.exp(clast)
                h = hp_ref[0, e]
                dh = dh_sc[e]
                ch = _dot(cb_, h, NT)
                dye_e = dye * ee
                dcum = jnp.sum(dye * ch, axis=1, keepdims=True) * ee
                dcg = dcg + _dot(dye_e, h, NN)
                dh_new = _dot(dye_e, cb_, TN)
                dbg = dbg + _dot(xdt * ff, dh, NN)
                bds = _dot(bb, dh, NT)
                dxdt = bds * ff
                dff = jnp.sum(bds * xdt, axis=1, keepdims=True) * ff
                dcum = dcum - dff
                dclast = jnp.sum(dff, axis=0, keepdims=True) + dec * jnp.sum(
                    jnp.sum(dh * h, axis=1, keepdims=True), axis=0, keepdims=True)
                dgm = _dot(dye, xdt, NT)
                dxdt = dxdt + _dot(gmat, dye, TN)
                dcb = dgm * lmat
                dcg = dcg + _dot(dcb, bb, NN)
                dbg = dbg + _dot(dcb, cb_, TN)
                mm = dgm * gmat
                dcum = dcum + jnp.sum(mm, axis=1, keepdims=True) - jnp.sum(mm.T, axis=1, keepdims=True)
                dcum = dcum + dclast * last_row
                dx_ref[:, esl] = dxdt * dtc + dye * dsk[:, e:e + 1]
                ddtx_all = ddtx_all + jnp.sum(dxdt * xe, axis=1, keepdims=True) * onehot
                dcum_all = dcum_all + dcum * onehot
                ddsk = ddsk + jnp.sum(jnp.sum(dye * xe, axis=1, keepdims=True), axis=0, keepdims=True) * onehot
                dh_sc[e] = dh_new + dec * dh
            dx_ref[:, bsl] = dbg
            dx_ref[:, csl] = dcg
        ii = lax.broadcasted_iota(jnp.int32, (CHUNK, CHUNK), 0)
        jj = lax.broadcasted_iota(jnp.int32, (CHUNK, CHUNK), 1)
        da = _ones_dot_left((jj >= ii).astype(bf16), dcum_all)
        ddt = da * a_neg + ddtx_all
        dalog_ref[...] += jnp.sum(da * dtv, axis=0, keepdims=True) * a_neg
        draw = ddt * _sigmoid(raw)
        ddt_ref[...] = draw
        ddtb_ref[...] += jnp.sum(draw, axis=0, keepdims=True)
        ddsk_ref[...] += ddsk

    rev = lambda i: nc - 1 - i
    par = lambda a: pl.BlockSpec(a.shape, lambda i: (0,) * a.ndim)
    acc = pl.BlockSpec((1, hs), lambda i: (0, 0))
    acc_shape = jax.ShapeDtypeStruct((1, hs), f32)
    res, cres = _call(
        body, "ssd_bwd",
        [jax.ShapeDtypeStruct((t, cdim), f32), jax.ShapeDtypeStruct((t, hs), f32), acc_shape, acc_shape, acc_shape],
        (nc,),
        [pl.BlockSpec((CHUNK, cdim), lambda i: (rev(i), 0)), pl.BlockSpec((CHUNK, hs), lambda i: (rev(i), 0)),
         pl.BlockSpec((hs, CHUNK), lambda i: (0, rev(i))), par(dtb), par(dtb_t), par(alog), par(alog_t),
         par(dskip), pl.BlockSpec((1, hs, SSD_P, SSD_N), lambda i: (rev(i), 0, 0, 0)),
         pl.BlockSpec((CHUNK, width), lambda i: (rev(i), 0))],
        [pl.BlockSpec((CHUNK, cdim), lambda i: (rev(i), 0)), pl.BlockSpec((CHUNK, hs), lambda i: (rev(i), 0)),
         acc, acc, acc],
        [pltpu.VMEM((hs, SSD_P, SSD_N), f32)], ("arbitrary",),
        [xbc, dt, dt_t, dtb, dtb_t, alog, alog_t, dskip, hprev, dy], comm, at)
    return (*res, cres)


def _where_am_i():
    x, y, c = lax.axis_index("x"), lax.axis_index("y"), lax.axis_index("c")
    chips = [(1 - x, y), (x, 1 - y), (1 - x, 1 - y)]
    return x, y, c, chips


def _remote(src, dst, send_sems, recv_sems, k, to):
    return pltpu.make_async_remote_copy(src_ref=src, dst_ref=dst, send_sem=send_sems.at[k], recv_sem=recv_sems.at[k],
                                        device_id=to, device_id_type=MESH)


class _Comm:
    def __init__(self, args, outs, nsem, nphase, run):
        self.args, self.outs, self.nsem, self.nphase, self.run = list(args), list(outs), nsem, nphase, run

    def sems(self):
        return [pltpu.SemaphoreType.DMA((self.nsem,)), pltpu.SemaphoreType.DMA((self.nsem,))]


def _comm_call(comm, name):
    na, no = len(comm.args), len(comm.outs)

    def body(*refs):
        for phase in range(comm.nphase):
            comm.run(phase, refs[:na], refs[na:na + no], refs[na + no], refs[na + no + 1])

    return list(pl.pallas_call(body, name=name, out_shape=comm.outs, in_specs=[ANY] * na, out_specs=[ANY] * no,
                               scratch_shapes=comm.sems())(*comm.args))


def _half(ref, c, r2):
    return ref.at[pl.ds(c * r2, r2)]


def _gather_weights(shards, wholes):
    ns, nw = len(shards), len(wholes)
    per = 7

    def run(phase, srcs, outs, send_sems, recv_sems):
        x, y, c, chips = _where_am_i()
        me, sib = 2 * x + y, (x, y, 1 - c)
        def first():
            cps = []
            for w in range(ns + nw):
                src, out, base = srcs[w], outs[w], per * w
                halved = w < ns
                r2 = src.shape[0] // 2
                piece = _half(src, c, r2) if halved else src
                for j, (cx, cy) in enumerate(chips):
                    dst = _half(out.at[me], c, r2) if halved else out.at[me]
                    cps.append(_remote(piece, dst, send_sems, recv_sems, base + j, (cx, cy, c)))
                cps.append(_remote(src, out.at[me], send_sems, recv_sems, base + 6, sib))
            return cps

        def passed():
            cps = []
            for w in range(ns):
                r2 = srcs[w].shape[0] // 2
                for j, (cx, cy) in enumerate(chips):
                    got = _half(outs[w].at[2 * cx + cy], c, r2)
                    cps.append(_remote(got, got, send_sems, recv_sems, per * w + 3 + j, sib))
            return cps

        if phase == 0:
            for cp in first():
                cp.start()
        elif phase == 1:
            it = iter(passed())
            for w in range(ns + nw):
                src, out, base = srcs[w], outs[w], per * w
                r2 = src.shape[0] // 2
                for j, (cx, cy) in enumerate(chips):
                    got = _half(out.at[2 * cx + cy], c, r2) if w < ns else out.at[2 * cx + cy]
                    _remote(got, got, send_sems, recv_sems, base + j, sib).wait_recv()
                    if w < ns:
                        next(it).start()
        else:
            for w in range(ns + nw):
                src, out, base = srcs[w], outs[w], per * w
                r2 = src.shape[0] // 2
                if w < ns:
                    for j, (cx, cy) in enumerate(chips):
                        got = _half(out.at[2 * cx + cy], 1 - c, r2)
                        _remote(got, got, send_sems, recv_sems, base + 3 + j, sib).wait_recv()
                _remote(src, out.at[me], send_sems, recv_sems, base + 6, sib).wait_recv()
            for cp in first() + passed():
                cp.wait_send()

    args = list(shards) + list(wholes)
    outs = [jax.ShapeDtypeStruct((4,) + a.shape, a.dtype) for a in args]
    return _Comm(args, outs, per * len(args), 3, run)


def _sibling_send_halves(gs):
    n = len(gs)

    def run(phase, srcs, outs, send_sems, recv_sems):
        x, y, c, _ = _where_am_i()
        sib = (x, y, 1 - c)
        cps = []
        for w in range(n):
            r2 = srcs[w].shape[1] // 2
            for k in range(4):
                cps.append(_remote(_half(srcs[w].at[k], 1 - c, r2), outs[w].at[k], send_sems, recv_sems, 4 * w + k, sib))
        for cp in cps:
            cp.start() if phase == 0 else cp.wait()

    outs = [jax.ShapeDtypeStruct((4, g.shape[1] // 2, g.shape[2]), g.dtype) for g in gs]
    return _Comm(gs, outs, 4 * n, 2, run)


def _chips_exchange(ps):
    n = len(ps)

    def run(phase, srcs, outs, send_sems, recv_sems):
        x, y, c, chips = _where_am_i()
        me = 2 * x + y
        cps = []
        for w in range(n):
            for j, (cx, cy) in enumerate(chips):
                cps.append(_remote(srcs[w].at[2 * cx + cy], outs[w].at[me], send_sems, recv_sems, 3 * w + j, (cx, cy, c)))
        if phase == 0:
            for cp in cps:
                cp.start()
        else:
            for w in range(n):
                for j, (cx, cy) in enumerate(chips):
                    got = outs[w].at[2 * cx + cy]
                    _remote(got, got, send_sems, recv_sems, 3 * w + j, (cx, cy, c)).wait_recv()
            for cp in cps:
                cp.wait_send()

    outs = [jax.ShapeDtypeStruct(p.shape, p.dtype) for p in ps]
    return _Comm(ps, outs, 3 * n, 2, run)


def _sibling_swap(rs):
    n = len(rs)

    def run(phase, srcs, outs, send_sems, recv_sems):
        x, y, c, _ = _where_am_i()
        for w in range(n):
            cp = _remote(srcs[w], outs[w], send_sems, recv_sems, w, (x, y, 1 - c))
            cp.start() if phase == 0 else cp.wait()

    outs = [jax.ShapeDtypeStruct(r.shape, r.dtype) for r in rs]
    return _Comm(rs, outs, n, 2, run)


def _gather_all(v):
    rows = v.shape[0]

    def body(x_ref, out_ref, send_sems, recv_sems, local_sem):
        x, y, c, chips = _where_am_i()
        me, sib = (x, y, c), (x, y, 1 - c)
        blk = lambda px, py, pc: out_ref.at[4 * px + 2 * py + pc]
        mine = pltpu.make_async_copy(x_ref, blk(*me), local_sem)
        mine.start()
        first = [_remote(x_ref, blk(*me), send_sems, recv_sems, 0, sib)]
        first += [_remote(x_ref, blk(*me), send_sems, recv_sems, 1 + j, (*chip, c)) for j, chip in enumerate(chips)]
        for cp in first:
            cp.start()
        passed = [_remote(blk(*chip, c), blk(*chip, c), send_sems, recv_sems, 4 + j, sib) for j, chip in enumerate(chips)]
        for j, chip in enumerate(chips):
            _remote(blk(*chip, c), blk(*chip, c), send_sems, recv_sems, 1 + j, me).wait_recv()
            passed[j].start()
        _remote(blk(*sib), blk(*sib), send_sems, recv_sems, 0, me).wait_recv()
        for j, chip in enumerate(chips):
            _remote(blk(*chip, 1 - c), blk(*chip, 1 - c), send_sems, recv_sems, 4 + j, me).wait_recv()
        for cp in first + passed:
            cp.wait_send()
        mine.wait()

    vm = pl.BlockSpec(memory_space=pltpu.VMEM)
    return pl.pallas_call(
        body, name="gather_all", out_shape=jax.ShapeDtypeStruct((8, rows, LANES), v.dtype),
        in_specs=[vm], out_specs=vm,
        scratch_shapes=[pltpu.SemaphoreType.DMA((7,)), pltpu.SemaphoreType.DMA((7,)), pltpu.SemaphoreType.DMA],
    )(v)


def _flat_pad(parts, total):
    v = jnp.concatenate([p.reshape(-1) for p in parts])
    return jnp.pad(v, (0, total - v.shape[0]))


def _adamw(w, g, m, v):
    m = ADAM_B1 * m + (1.0 - ADAM_B1) * g
    v = ADAM_B2 * v + (1.0 - ADAM_B2) * jnp.square(g)
    m_hat = m / (1.0 - ADAM_B1 ** ADAM_STEP)
    v_hat = v / (1.0 - ADAM_B2 ** ADAM_STEP)
    delta = -ADAM_LR * (m_hat / (jnp.sqrt(v_hat) + ADAM_EPS) + ADAM_WD * w)
    return delta, m, v


def _adamw_call(w, g, m, v, name):
    r, cdim = w.shape
    tm = _tile(r, ROW_TILE, 8)
    o = _rout(r, cdim, f32, tm)
    return _rowwise(_adamw, r, tm, [_rows(w, tm), _rows(g, tm), _rows(m, tm), _rows(v, tm)], [o, o, o], [], name)


def _adamw_halves(w, m, v, g_mine, g_sib, sp, name):
    r, cdim = w.shape
    r2 = r // 2
    tr = _tile(r2, ROW_TILE, 8)
    nb = r2 // tr

    def body(sp_ref, w_ref, m_ref, v_ref, ga_ref, gb_ref, g_out, d_out, m_out, v_out):
        g = jnp.where(pl.program_id(0) == sp_ref[1], ga_ref[...], gb_ref[...])
        delta, mn, vn = _adamw(w_ref[...], g, m_ref[...], v_ref[...])
        g_out[...] = g
        d_out[...] = delta
        m_out[...] = mn
        v_out[...] = vn

    full = pl.BlockSpec((tr, cdim), lambda h, i, s: (h * nb + i, 0))
    mine = pl.BlockSpec((tr, cdim), lambda h, i, s: (jnp.where(h == s[1], i, 0), 0))
    sib = pl.BlockSpec((tr, cdim), lambda h, i, s: (jnp.where(h == s[1], 0, i), 0))
    out = jax.ShapeDtypeStruct((r, cdim), f32)
    gs = pltpu.PrefetchScalarGridSpec(num_scalar_prefetch=1, grid=(2, nb), in_specs=[full, full, full, mine, sib],
                                      out_specs=[full, full, full, full])
    return pl.pallas_call(body, name=name, out_shape=[out, out, out, out], grid_spec=gs,
                          compiler_params=_params(("parallel", "parallel")))(sp, w, m, v, g_mine, g_sib)


MIXER = ("w_in", "w_uq", "w_ukv", "w_out")
FFN = ("w_gate", "w_up", "w_down")
BIG = MIXER + FFN
SMALL = ("q_norm_w", "kv_norm_w", "conv_w", "conv_b", "dt_bias", "a_log", "d_skip", "ssd_norm_w", "attn_out_norm_w",
         "pre_mix_norm_w", "post_mix_norm_w", "pre_ffn_norm_w", "post_ffn_norm_w")
ORDER = ("w_in", "q_norm_w", "w_uq", "kv_norm_w", "w_ukv", "conv_w", "conv_b", "dt_bias", "a_log", "d_skip",
         "ssd_norm_w", "attn_out_norm_w", "w_out", "pre_mix_norm_w", "post_mix_norm_w", "pre_ffn_norm_w",
         "post_ffn_norm_w", "w_gate", "w_up", "w_down")


def kernel(x, positions, w_in, q_norm_w, w_uq, kv_norm_w, w_ukv, conv_w, conv_b, dt_bias, a_log, d_skip, ssd_norm_w, attn_out_norm_w, w_out, pre_mix_norm_w, post_mix_norm_w, pre_ffn_norm_w, post_ffn_norm_w, w_gate, w_up, w_down, loss_target, m_w_in, m_q_norm_w, m_w_uq, m_kv_norm_w, m_w_ukv, m_conv_w, m_conv_b, m_dt_bias, m_a_log, m_d_skip, m_ssd_norm_w, m_attn_out_norm_w, m_w_out, m_pre_mix_norm_w, m_post_mix_norm_w, m_pre_ffn_norm_w, m_post_ffn_norm_w, m_w_gate, m_w_up, m_w_down, v_w_in, v_q_norm_w, v_w_uq, v_kv_norm_w, v_w_ukv, v_conv_w, v_conv_b, v_dt_bias, v_a_log, v_d_skip, v_ssd_norm_w, v_attn_out_norm_w, v_w_out, v_pre_mix_norm_w, v_post_mix_norm_w, v_pre_ffn_norm_w, v_post_ffn_norm_w, v_w_gate, v_w_up, v_w_down):
    local = dict(locals())
    wts = {n: local[n][0] for n in ORDER}
    mom_m = {n: local["m_" + n][0] for n in ORDER}
    mom_v = {n: local["v_" + n][0] for n in ORDER}
    xs = x[0]
    tgt = loss_target[0]
    t, d = xs.shape
    nchip = 4
    my_x, my_y, my_c = lax.axis_index("x"), lax.axis_index("y"), lax.axis_index("c")
    my_chip = 2 * my_x + my_y

    mla_w = d // 2
    nh = mla_w // V_HEAD
    width = d - mla_w
    hs = width // SSD_P
    gn = SSD_G * SSD_N
    cdim = width + 2 * gn
    in_sizes = (Q_RANK, KV_RANK, ROPE, width, cdim, hs)
    d_in = sum(in_sizes)
    tail = LANES
    off_xbc = 0
    off_cq = cdim
    off_ckv = off_cq + Q_RANK
    off_z = off_ckv + KV_RANK
    off_tail = off_z + width
    d_in_p = _round_up(off_tail + tail, 256)
    gw = width // SSD_G
    assert off_cq % Q_RANK == 0 and off_ckv % KV_RANK == 0 and off_z % gw == 0 and off_tail % LANES == 0
    qk_head = NOPE + ROPE
    scale = qk_head ** -0.5
    tm = _tile(t, ROW_TILE, 8)
    tmw = _tile(t, ROW_TILE // 2, 8)

    sp = jnp.stack([my_chip, my_c]).astype(jnp.int32)

    def pair_sums(names, gl, from_sib):
        res = []
        for n, g, fs in zip(names, gl, from_sib):
            _, r2, cs = fs.shape
            tr = _tile(r2, ROW_TILE, 16)
            nb = r2 // tr
            res.append(_blocked(
                lambda a, b: a + b, (nchip, nb),
                [(g, (None, tr, cs), lambda k, i, s, nb=nb: (k, s[1] * nb + i, 0)),
                 (fs, (None, tr, cs), lambda k, i, s: (k, i, 0))],
                [((nchip, r2, cs), bf16, (None, tr, cs), lambda k, i, s: (k, i, 0))], [], "pair_sum_" + n, sp=sp)[0])
        return res

    def chip_sums(names, gl, from_sib, from_chips):
        res = []
        for n, g, fs, fc in zip(names, gl, from_sib, from_chips):
            _, r2, cs = fs.shape
            tr = _tile(r2, ROW_TILE, 16)
            nb = r2 // tr
            res.append(_blocked(
                lambda a, b, r1, r2_, r3: ((a + b) + r1.astype(f32)) + r2_.astype(f32) + r3.astype(f32), (nb,),
                [(g, (None, tr, cs), lambda i, s, nb=nb: (s[0], s[1] * nb + i, 0)),
                 (fs, (None, tr, cs), lambda i, s: (s[0], i, 0)),
                 (fc, (None, tr, cs), lambda i, s: (s[0] ^ 1, i, 0)),
                 (fc, (None, tr, cs), lambda i, s: (s[0] ^ 2, i, 0)),
                 (fc, (None, tr, cs), lambda i, s: (s[0] ^ 3, i, 0))],
                [((r2, cs), f32, (tr, cs), lambda i, s: (i, 0))], [], "chip_sum_" + n, sp=sp)[0])
        return res

    ck, ccs = wts["conv_w"].shape
    w_in_g, conv_g = _comm_call(_gather_weights([wts["w_in"].astype(bf16)], [wts["conv_w"]]), "gather_w_in")
    cat_cols = lambda g: jnp.concatenate([g[k] for k in range(nchip)], axis=1)
    conv_full = cat_cols(conv_g)
    mixer_gather = _gather_weights([wts[n].astype(bf16) for n in MIXER[1:]], [])
    ffn_gather = _gather_weights([wts[n].astype(bf16) for n in FFN], [])

    wi = cat_cols(w_in_g)
    o = np.cumsum((0,) + in_sizes)
    seg = lambda i: wi[:, o[i]:o[i + 1]]
    w_in_p = jnp.concatenate([seg(4), seg(0), seg(1), seg(3), seg(2), seg(5),
                              jnp.zeros((d, d_in_p - off_tail - ROPE - hs), bf16)], axis=1)

    inv_freq = ROPE_THETA ** (-jnp.arange(0, ROPE, 2, dtype=f32) / ROPE)
    ang = positions[0].astype(f32)[:, None] * inv_freq
    cos, sin = jnp.cos(ang), jnp.sin(ang)
    z32, z64, z96 = jnp.zeros((t, 32), f32), jnp.zeros((t, 64), f32), jnp.zeros((t, 96), f32)
    cosp = jnp.concatenate([cos, cos, z64], axis=1)
    sina = jnp.concatenate([-sin, z96], axis=1)
    sinb = jnp.concatenate([z32, sin, z64], axis=1)

    row = lambda a: a.reshape(1, -1)
    w_pre_mix, w_post_mix = row(wts["pre_mix_norm_w"]), row(wts["post_mix_norm_w"])
    w_pre_ffn, w_post_ffn = row(wts["pre_ffn_norm_w"]), row(wts["post_ffn_norm_w"])
    w_qn, w_kvn = row(wts["q_norm_w"]), row(wts["kv_norm_w"])
    w_attn_n, w_ssd_n = row(wts["attn_out_norm_w"]), row(wts["ssd_norm_w"])
    conv_b_r = row(wts["conv_b"])
    dtb, alog, dskip = row(wts["dt_bias"]), row(wts["a_log"]), row(wts["d_skip"])
    dtb_t, alog_t = dtb.reshape(hs, 1), alog.reshape(hs, 1)

    u = _rowwise(lambda a, w: _rms(a, w), t, tm, [_rows(xs, tm), _par(w_pre_mix)], [_rout(t, d, bf16, tm)], [],
                 "pre_mix_norm")[0]
    proj, (w_uq_g, w_ukv_f, w_out_g) = _matmul([(u, w_in_p)], "nn", f32, "in_proj", comm=mixer_gather,
                                               at=(0.0, 0.6))
    w_uq_p = jnp.pad(cat_cols(w_uq_g).reshape(Q_RANK, nh, qk_head), ((0, 0), (0, 0), (0, QK_PAD - qk_head))
                     ).reshape(Q_RANK, nh * QK_PAD)
    w_out_f = w_out_g.reshape(-1, w_out_g.shape[2])
    cq_in = _rows(proj, tm, Q_RANK, off_cq // Q_RANK)
    ckv_in = _rows(proj, tm, KV_RANK, off_ckv // KV_RANK)
    cqn = _rowwise(lambda a, w: _rms(a, w), t, tm, [cq_in, _par(w_qn)], [_rout(t, Q_RANK, bf16, tm)], [], "q_norm")[0]
    ckvn = _rowwise(lambda a, w: _rms(a, w), t, tm, [ckv_in, _par(w_kvn)], [_rout(t, KV_RANK, bf16, tm)], [],
                    "kv_norm")[0]
    q_raw = _matmul([(cqn, w_uq_p)], "nn", f32, "q_up")
    kv = _matmul([(ckvn, w_ukv_f)], "nn", bf16, "kv_up")

    def q_rope_fn(qt, cp, sa, sb):
        parts = []
        for h in range(nh):
            parts.append(qt[:, h * QK_PAD: h * QK_PAD + NOPE])
            parts.append(_rope(qt[:, h * QK_PAD + NOPE:(h + 1) * QK_PAD], cp, sa, sb))
        return jnp.concatenate(parts, axis=1)

    tail_cb = off_tail // LANES
    q2 = _rowwise(q_rope_fn, t, tm, [_rows(q_raw, tm), _rows(cosp, tm), _rows(sina, tm), _rows(sinb, tm)],
                  [_rout(t, nh * QK_PAD, bf16, tm)], [], "q_rope")[0]
    kr2 = _rowwise(_rope, t, tm, [_rows(proj, tm, LANES, tail_cb), _rows(cosp, tm), _rows(sina, tm), _rows(sinb, tm)],
                   [_rout(t, LANES, bf16, tm)], [], "k_rope")[0]
    o_att, lse, ffn_w = _flash_fwd(q2, kv, kr2, nh, scale, ffn_gather, (0.0, 0.75))
    w_gate_f, w_up_f = ffn_w[0], ffn_w[1]
    w_down_f = ffn_w[2].reshape(-1, ffn_w[2].shape[2])

    xbc_act = _conv_fwd(proj, conv_full, conv_b_r, cdim, tm)
    dt_raw = proj[:, off_tail + ROPE: off_tail + ROPE + hs]
    dt_raw_t = dt_raw.T
    y_ssd, hprev = _ssd_fwd(xbc_act, dt_raw, dt_raw_t, dtb, dtb_t, alog, alog_t, dskip, width)
    z_ins = [_rows(proj, tm, gw, off_z // gw + i) for i in range(SSD_G)]

    def mix_norms_fn(ov, yv, *rest):
        zs, wa, ws = rest[:SSD_G], rest[SSD_G], rest[SSD_G + 1]
        outs = [_rms(ov, wa)]
        for i in range(SSD_G):
            sl = slice(i * gw, (i + 1) * gw)
            outs.append(_rms(yv[:, sl] * (zs[i] * _sigmoid(zs[i])), ws[:, sl]))
        return jnp.concatenate(outs, axis=1)

    cat = _rowwise(mix_norms_fn, t, tm, [_rows(o_att, tm), _rows(y_ssd, tm)] + z_ins + [_par(w_attn_n), _par(w_ssd_n)],
                   [_rout(t, d, bf16, tm)], [], "attn_ssd_out_norms")[0]
    mix = _matmul([(cat, w_out_f)], "nn", f32, "out_proj")

    def post_mix_fn(mx, xv, w1, w2):
        h1v = xv + _rms(mx, w1)
        return h1v, _rms(h1v, w2)

    h1, v_in = _rowwise(post_mix_fn, t, tmw, [_rows(mix, tmw), _rows(xs, tmw), _par(w_post_mix), _par(w_pre_ffn)],
                        [_rout(t, d, f32, tmw), _rout(t, d, bf16, tmw)], [], "post_mix_pre_ffn_norm")
    g_ff, u_ff, act = _ffn_up(v_in, w_gate_f, w_up_f)
    ffn = _matmul([(act, w_down_f)], "nn", f32, "ffn_down")

    def final_fn(fv, h1v, tg, w):
        h2 = h1v + _rms(fv, w)
        err = h2 - tg
        lpart = 0.5 * jnp.sum(jnp.sum(err * err, axis=1, keepdims=True), axis=0, keepdims=True) / d
        dh2 = err / d
        dff, dw = _rms_bwd(fv, w, dh2)
        return dff, dh2, jnp.broadcast_to(lpart, (1, LANES)), dw

    dffn, dh2, loss_acc, g_post_ffn = _rowwise(
        final_fn, t, tmw, [_rows(ffn, tmw), _rows(h1, tmw), _rows(tgt, tmw), _par(w_post_ffn)],
        [_rout(t, d, bf16, tmw), _rout(t, d, f32, tmw)], [(1, LANES), (1, d)], "loss_post_ffn_norm_bwd")
    loss = lax.psum(loss_acc[0, 0], ("x", "y", "c"))

    dg_ff, du_ff = _ffn_down_bwd(dffn, w_down_f, g_ff, u_ff)
    gw_down = _matmul([(act, dffn)], "tn", f32, "grad_w_down")
    gw_gate = _matmul([(v_in, dg_ff)], "tn", f32, "grad_w_gate", out_chunks=nchip)
    gw_up = _matmul([(v_in, du_ff)], "tn", f32, "grad_w_up", out_chunks=nchip)
    ffn_g = [gw_gate, gw_up, gw_down.reshape(nchip, -1, d)]
    dv_in, ffn_from_sib = _matmul([(dg_ff, w_gate_f), (du_ff, w_up_f)], "nt", f32, "ffn_up_bwd",
                                  comm=_sibling_send_halves(ffn_g), at=(0.0,))
    ffn_pairs = pair_sums(FFN, ffn_g, ffn_from_sib)

    def mid_bwd_fn(h1v, dvv, dh2v, mx, w_pf, w_pm):
        dxn, dw_pf = _rms_bwd(h1v, w_pf, dvv)
        dh1v = dh2v + dxn
        dmx, dw_pm = _rms_bwd(mx, w_pm, dh1v)
        return dh1v, dmx, dw_pf, dw_pm

    dh1, dmix, g_pre_ffn, g_post_mix = _rowwise(
        mid_bwd_fn, t, tmw, [_rows(h1, tmw), _rows(dv_in, tmw), _rows(dh2, tmw), _rows(mix, tmw),
                             _par(w_pre_ffn), _par(w_post_mix)],
        [_rout(t, d, f32, tmw), _rout(t, d, bf16, tmw)], [(1, d), (1, d)], "pre_ffn_post_mix_norm_bwd")
    dcat = _matmul([(dmix, w_out_f)], "nt", f32, "out_proj_bwd")
    gw_out = _matmul([(cat, dmix)], "tn", f32, "grad_w_out")

    def mix_norms_bwd_fn(ov, yv, *rest):
        zs, dcv, wa, ws = rest[:SSD_G], rest[SSD_G], rest[SSD_G + 1], rest[SSD_G + 2]
        dov, dwa = _rms_bwd(ov, wa, dcv[:, :mla_w])
        dl = [jnp.broadcast_to(jnp.sum(dov[:, h * V_HEAD:(h + 1) * V_HEAD] * ov[:, h * V_HEAD:(h + 1) * V_HEAD],
                                       axis=1, keepdims=True), (ov.shape[0], V_HEAD)) for h in range(nh)]
        dys, dzs, dws = [], [], []
        for i in range(SSD_G):
            sl = slice(i * gw, (i + 1) * gw)
            zv, yi = zs[i], yv[:, sl]
            sg = _sigmoid(zv)
            sz = zv * sg
            dgi, dwi = _rms_bwd(yi * sz, ws[:, sl], dcv[:, mla_w + i * gw: mla_w + (i + 1) * gw])
            dys.append(dgi * sz)
            dzs.append(dgi * yi * (sg * (1.0 + zv * (1.0 - sg))))
            dws.append(dwi)
        cc = lambda a: jnp.concatenate(a, axis=1)
        return dov, cc(dl), cc(dys), cc(dzs), dwa, cc(dws)

    do_att, delta, dy_ssd, dz, g_attn_n, g_ssd_n = _rowwise(
        mix_norms_bwd_fn, t, tm,
        [_rows(o_att, tm), _rows(y_ssd, tm)] + z_ins + [_rows(dcat, tm), _par(w_attn_n), _par(w_ssd_n)],
        [_rout(t, mla_w, bf16, tm), _rout(t, mla_w, f32, tm), _rout(t, width, f32, tm), _rout(t, width, bf16, tm)],
        [(1, mla_w), (1, width)], "attn_ssd_out_norms_bwd")
    dkv, dkr_h, dq2, ffn_from_chips = _flash_bwd(q2, kv, kr2, do_att, lse, delta, nh, scale,
                                                 _chips_exchange(ffn_pairs), (0.0,))
    ffn_halves = chip_sums(FFN, ffn_g, ffn_from_sib, ffn_from_chips)

    def q_rope_bwd_fn(dqt, cp, sa, sb):
        parts = []
        for h in range(nh):
            parts.append(dqt[:, h * QK_PAD: h * QK_PAD + NOPE])
            parts.append(_rope_bwd(dqt[:, h * QK_PAD + NOPE:(h + 1) * QK_PAD], cp, sa, sb))
        return jnp.concatenate(parts, axis=1)

    dq_raw = _rowwise(q_rope_bwd_fn, t, tm, [_rows(dq2, tm), _rows(cosp, tm), _rows(sina, tm), _rows(sinb, tm)],
                      [_rout(t, nh * QK_PAD, bf16, tm)], [], "q_rope_bwd")[0]

    def k_rope_bwd_fn(dk, cp, sa, sb):
        tot = dk[:, 0:LANES]
        for h in range(1, nh):
            tot = tot + dk[:, h * LANES:(h + 1) * LANES]
        return _rope_bwd(tot, cp, sa, sb)

    dkr = _rowwise(k_rope_bwd_fn, t, tm, [_rows(dkr_h, tm), _rows(cosp, tm), _rows(sina, tm), _rows(sinb, tm)],
                   [_rout(t, LANES, f32, tm)], [], "k_rope_bwd")[0]
    gw_uq_p = _matmul([(cqn, dq_raw)], "tn", f32, "grad_w_uq")
    gw_ukv = _matmul([(ckvn, dkv)], "tn", f32, "grad_w_ukv", out_chunks=nchip)
    dcqn = _matmul([(dq_raw, w_uq_p)], "nt", f32, "q_up_bwd")
    dckvn = _matmul([(dkv, w_ukv_f)], "nt", f32, "kv_up_bwd")

    def lat_norm_bwd_fn(a, w, dyv):
        return _rms_bwd(a, w, dyv)

    dcq, g_qn = _rowwise(lat_norm_bwd_fn, t, tm, [cq_in, _par(w_qn), _rows(dcqn, tm)],
                         [_rout(t, Q_RANK, bf16, tm)], [(1, Q_RANK)], "q_norm_bwd")
    dckv, g_kvn = _rowwise(lat_norm_bwd_fn, t, tm, [ckv_in, _par(w_kvn), _rows(dckvn, tm)],
                           [_rout(t, KV_RANK, bf16, tm)], [(1, KV_RANK)], "kv_norm_bwd")

    dxbc_act, ddt_raw, g_alog, g_dskip, g_dtb, ffn_sib_halves = _ssd_bwd(
        xbc_act, dt_raw, dt_raw_t, dtb, dtb_t, alog, alog_t, dskip, hprev, dy_ssd, width,
        _sibling_swap(ffn_halves), (0.0,))
    dpre, gcw0, gcw1, gcw2, gcw3, g_conv_b = _conv_bwd_pre(proj, conv_full, conv_b_r, dxbc_act, cdim, tm)
    g_conv_w = jnp.concatenate([gcw0, gcw1, gcw2, gcw3], axis=0)
    dxbc = _conv_bwd_dx(dpre, conv_full, tm)

    dtail = jnp.concatenate([dkr[:, :ROPE], ddt_raw, jnp.zeros((t, d_in_p - off_tail - ROPE - hs), f32)],
                            axis=1).astype(bf16)
    dproj = jnp.concatenate([dxbc, dcq, dckv, dz, dtail], axis=1)
    split_cols = lambda gf: jnp.stack(jnp.split(gf, nchip, axis=1))
    split_rows = lambda gf: gf.reshape(nchip, gf.shape[0] // nchip, gf.shape[1])
    gw_uq = gw_uq_p.reshape(Q_RANK, nh, QK_PAD)[:, :, :qk_head].reshape(Q_RANK, nh * qk_head)
    rest_g = [split_cols(gw_uq), gw_ukv, split_rows(gw_out)]
    du_in, rest_from_sib = _matmul([(dproj, w_in_p)], "nt", f32, "in_proj_bwd",
                                   comm=_sibling_send_halves(rest_g), at=(0.0,))
    rest_pairs = pair_sums(MIXER[1:], rest_g, rest_from_sib)
    gw_in_p, rest_from_chips = _matmul([(u, dproj)], "tn", f32, "grad_w_in",
                                       comm=_chips_exchange(rest_pairs), at=(0.0,))
    rest_halves = chip_sums(MIXER[1:], rest_g, rest_from_sib, rest_from_chips)

    def first_bwd_fn(xv, duv, dh1v, w):
        dxn, dw = _rms_bwd(xv, w, duv)
        return dh1v + dxn, dw

    grad_x, g_pre_mix = _rowwise(first_bwd_fn, t, tmw, [_rows(xs, tmw), _rows(du_in, tmw), _rows(dh1, tmw),
                                                         _par(w_pre_mix)],
                                 [_rout(t, d, f32, tmw)], [(1, d)], "pre_mix_norm_bwd")

    gseg = lambda a, b: gw_in_p[:, a:b]
    gw_in = jnp.concatenate([gseg(off_cq, off_ckv), gseg(off_ckv, off_z), gseg(off_tail, off_tail + ROPE),
                             gseg(off_z, off_tail), gseg(off_xbc, off_cq),
                             gseg(off_tail + ROPE, off_tail + ROPE + hs)], axis=1)
    in_g = [split_cols(gw_in)]
    in_from_sib = _comm_call(_sibling_send_halves(in_g), "sibling_send_halves")
    in_pairs = pair_sums(MIXER[:1], in_g, in_from_sib)
    in_from_chips = _comm_call(_chips_exchange(in_pairs), "chips_exchange")
    mix_halves = chip_sums(MIXER[:1], in_g, in_from_sib, in_from_chips) + rest_halves
    mix_sib_halves = _comm_call(_sibling_swap(mix_halves), "sibling_swap")
    halves = mix_halves + ffn_halves
    sib_halves = mix_sib_halves + ffn_sib_halves
    gshard = {}

    gsmall = {"q_norm_w": g_qn, "kv_norm_w": g_kvn, "conv_w": g_conv_w, "conv_b": g_conv_b, "dt_bias": g_dtb,
              "a_log": g_alog, "d_skip": g_dskip, "ssd_norm_w": g_ssd_n, "attn_out_norm_w": g_attn_n,
              "pre_mix_norm_w": g_pre_mix, "post_mix_norm_w": g_post_mix, "pre_ffn_norm_w": g_pre_ffn,
              "post_ffn_norm_w": g_post_ffn}
    small_sizes = [int(np.prod(gsmall[n].shape)) for n in SMALL]
    srows = _round_up(-(-sum(small_sizes) // LANES), 8)
    spart = _flat_pad([gsmall[n] for n in SMALL], srows * LANES).reshape(srows, LANES)
    sall = _gather_all(spart)

    def sum8_fn(a):
        tot = a[0]
        for k in range(1, 8):
            tot = tot + a[k]
        return tot

    ssum = _blocked(sum8_fn, (1,), [(sall, sall.shape, lambda i: (0, 0, 0))],
                    [((srows, LANES), f32, (srows, LANES), lambda i: (0, 0))], [], "small_grad_sum")[0].reshape(-1)
    gred = {}
    off = 0
    for n, sz in zip(SMALL, small_sizes):
        gred[n] = ssum[off:off + sz].reshape(gsmall[n].shape)
        off += sz
    gshard["conv_w"] = lax.dynamic_slice_in_dim(gred["conv_w"], my_chip * ccs, ccs, axis=1)
    for n in SMALL:
        if n != "conv_w":
            gshard[n] = gred[n].reshape(wts[n].shape)

    delta, new_m, new_v = {}, {}, {}
    for n, mine_h, sib_h in zip(BIG, halves, sib_halves):
        gshard[n], delta[n], new_m[n], new_v[n] = _adamw_halves(wts[n], mom_m[n], mom_v[n], mine_h, sib_h, sp,
                                                                "adamw_" + n)
    pack = lambda src: _flat_pad([src[n] for n in SMALL], srows * LANES).reshape(srows, LANES)
    sd, sm, sv = _adamw_call(pack(wts), pack(gshard), pack(mom_m), pack(mom_v), "adamw_small")
    off = 0
    for n in SMALL:
        sz = int(np.prod(wts[n].shape))
        for dst, src in ((delta, sd), (new_m, sm), (new_v, sv)):
            dst[n] = src.reshape(-1)[off:off + sz].reshape(wts[n].shape)
        off += sz

    lead = lambda a: a[None]
    return (loss, grad_x[None], *[lead(gshard[n]) for n in ORDER], *[lead(delta[n]) for n in ORDER],
            *[lead(new_m[n]) for n in ORDER], *[lead(new_v[n]) for n in ORDER])
```

```python
import functools

import numpy as np
import jax
import jax.numpy as jnp
from jax import lax
from jax.experimental import pallas as pl
from jax.experimental.pallas import tpu as pltpu

f32, bf16 = jnp.float32, jnp.bfloat16

EPS = 1e-6
V_HEAD = 128
NOPE = 128
ROPE = 64
QK_PAD = 256
Q_RANK = 512
KV_RANK = 512
ROPE_THETA = 10000.0
SSD_P = 64
SSD_G = 2
SSD_N = 128
SSD_K = 4
CHUNK = 128
ADAM_LR, ADAM_B1, ADAM_B2, ADAM_EPS, ADAM_WD, ADAM_STEP = 0.001, 0.9, 0.999, 1e-08, 0.01, 10

VMEM_LIMIT_BYTES = 48 * 1024 * 1024
LANES = 128
ATT_TILE = 512
MM_TM, MM_TN, MM_TK = 1024, 1024, 1024
CHUNK_WHOLE_MAX = 1536
ROW_TILE = 256

NN = (((1,), (0,)), ((), ()))
NT = (((1,), (1,)), ((), ()))
TN = (((0,), (0,)), ((), ()))
MESH = pl.DeviceIdType.MESH
ANY = pl.BlockSpec(memory_space=pl.ANY)


def _tile(dim, cap, align=LANES):
    if dim <= cap:
        return dim
    t = (cap // align) * align
    while t >= align:
        if dim % t == 0:
            return t
        t -= align
    raise ValueError(f"no tile for {dim} under {cap}")


def _round_up(n, m):
    return -(-n // m) * m


def _params(sem):
    return pltpu.CompilerParams(dimension_semantics=sem, vmem_limit_bytes=VMEM_LIMIT_BYTES)


def _dot(a, b, dims):
    return lax.dot_general(a.astype(bf16), b.astype(bf16), dims, preferred_element_type=f32)


def _call(body, name, out_shape, grid, in_specs, out_specs, scratch_shapes, sem, args, comm=None, at=None,
          prefetch=()):
    npf = len(prefetch)
    if comm is None:
        gs = pltpu.PrefetchScalarGridSpec(num_scalar_prefetch=npf, grid=grid, in_specs=list(in_specs),
                                          out_specs=list(out_specs), scratch_shapes=list(scratch_shapes))
        res = pl.pallas_call(body, name=name, out_shape=list(out_shape), grid_spec=gs,
                             compiler_params=_params(sem))(*prefetch, *args)
        return list(res), []
    n_in, n_out, n_sc = len(args), len(out_shape), len(scratch_shapes)
    na, no = len(comm.args), len(comm.outs)
    steps = int(np.prod(grid))

    def full(*allrefs):
        pf, refs = allrefs[:npf], allrefs[npf:]
        cin = refs[n_in:n_in + na]
        o0 = n_in + na
        cout = refs[o0 + n_out:o0 + n_out + no]
        s0 = o0 + n_out + no
        send_sems, recv_sems = refs[s0 + n_sc], refs[s0 + n_sc + 1]
        lin = pl.program_id(0)
        for dim in range(1, len(grid)):
            lin = lin * grid[dim] + pl.program_id(dim)
        for p in range(comm.nphase - 1):
            @pl.when(lin == int(round(at[p] * (steps - 1))))
            def _(p=p):
                comm.run(p, cin, cout, send_sems, recv_sems)
        body(*pf, *refs[:n_in], *refs[o0:o0 + n_out], *refs[s0:s0 + n_sc])

        @pl.when(lin == steps - 1)
        def _():
            comm.run(comm.nphase - 1, cin, cout, send_sems, recv_sems)

    gs = pltpu.PrefetchScalarGridSpec(
        num_scalar_prefetch=npf, grid=grid, in_specs=list(in_specs) + [ANY] * na,
        out_specs=list(out_specs) + [ANY] * no, scratch_shapes=list(scratch_shapes) + comm.sems())
    res = pl.pallas_call(full, name=name, out_shape=list(out_shape) + comm.outs, grid_spec=gs,
                         compiler_params=_params(("arbitrary",) * len(grid)))(*prefetch, *args, *comm.args)
    return list(res[:n_out]), list(res[n_out:])


def _chunk_tile(cs, cap):
    return cs if cs <= CHUNK_WHOLE_MAX else _tile(cs, cap)


def _matmul(pairs, mode, out_dtype, name, out_chunks=None, comm=None, at=None):
    a0, b0 = pairs[0]
    chunked = b0.ndim == 3
    cs = b0.shape[2] if chunked else None
    bcols = b0.shape[0] * b0.shape[2] if chunked else b0.shape[1]
    brows = b0.shape[1] if chunked else b0.shape[0]
    if mode == "nn":
        (m, k), n = a0.shape, bcols
    elif mode == "nt":
        (m, k), n = a0.shape, brows
    else:
        (k, m), n = a0.shape, bcols
    tm = _tile(m, MM_TM)
    if mode == "nt":
        tn = _tile(n, MM_TN)
        tk = _chunk_tile(cs, MM_TK) if chunked else _tile(k, MM_TK)
    else:
        tk = _tile(k, MM_TK)
        if chunked:
            tn = _chunk_tile(cs, MM_TN)
        elif out_chunks:
            tn = _chunk_tile(n // out_chunks, MM_TN)
        else:
            tn = _tile(n, MM_TN)
    nk = k // tk
    if mode == "nn":
        a_spec = pl.BlockSpec((tm, tk), lambda i, j, kk: (i, kk))
        if chunked:
            q = cs // tn
            b_spec = pl.BlockSpec((None, tk, tn), lambda i, j, kk: (j // q, kk, j % q))
        else:
            b_spec = pl.BlockSpec((tk, tn), lambda i, j, kk: (kk, j))
        dims = NN
    elif mode == "nt":
        a_spec = pl.BlockSpec((tm, tk), lambda i, j, kk: (i, kk))
        if chunked:
            q = cs // tk
            b_spec = pl.BlockSpec((None, tn, tk), lambda i, j, kk: (kk // q, j, kk % q))
        else:
            b_spec = pl.BlockSpec((tn, tk), lambda i, j, kk: (j, kk))
        dims = NT
    else:
        a_spec = pl.BlockSpec((tk, tm), lambda i, j, kk: (kk, i))
        b_spec = pl.BlockSpec((tk, tn), lambda i, j, kk: (kk, j))
        dims = TN
    if out_chunks:
        qo = (n // out_chunks) // tn
        out_shape = jax.ShapeDtypeStruct((out_chunks, m, n // out_chunks), out_dtype)
        o_spec = pl.BlockSpec((None, tm, tn), lambda i, j, kk: (j // qo, i, j % qo))
    else:
        out_shape = jax.ShapeDtypeStruct((m, n), out_dtype)
        o_spec = pl.BlockSpec((tm, tn), lambda i, j, kk: (i, j))
    npair = len(pairs)

    def body(*refs):
        o_ref, acc = refs[2 * npair], refs[2 * npair + 1]
        kk = pl.program_id(2)

        @pl.when(kk == 0)
        def _():
            acc[...] = jnp.zeros_like(acc)

        part = _dot(refs[0][...], refs[1][...], dims)
        for p in range(1, npair):
            part = part + _dot(refs[2 * p][...], refs[2 * p + 1][...], dims)
        acc[...] += part

        @pl.when(kk == nk - 1)
        def _():
            o_ref[...] = acc[...].astype(out_dtype)

    args = [t for pr in pairs for t in pr]
    res, cres = _call(body, name, [out_shape], (m // tm, n // tn, nk), [a_spec, b_spec] * npair, [o_spec],
                      [pltpu.VMEM((tm, tn), f32)], ("parallel", "parallel", "arbitrary"), args, comm, at)
    return res[0] if comm is None else (res[0], cres)


def _sigmoid(x):
    return 1.0 / (1.0 + jnp.exp(-x))


def _ffn_up(v, wg, wu):
    m, k = v.shape
    nchunk, _, cs = wg.shape
    n = nchunk * cs
    tm, tn, tk = _tile(m, 512), _chunk_tile(cs, 512), _tile(k, MM_TK)
    nk = k // tk
    q = cs // tn
    w_spec = pl.BlockSpec((None, tk, tn), lambda i, j, kk: (j // q, kk, j % q))

    def body(v_ref, wg_ref, wu_ref, g_ref, u_ref, act_ref, accg, accu):
        kk = pl.program_id(2)

        @pl.when(kk == 0)
        def _():
            accg[...] = jnp.zeros_like(accg)
            accu[...] = jnp.zeros_like(accu)

        vb = v_ref[...]
        accg[...] += _dot(vb, wg_ref[...], NN)
        accu[...] += _dot(vb, wu_ref[...], NN)

        @pl.when(kk == nk - 1)
        def _():
            g, u = accg[...], accu[...]
            g_ref[...] = g.astype(bf16)
            u_ref[...] = u.astype(bf16)
            act_ref[...] = (g * _sigmoid(g) * u).astype(bf16)

    out = jax.ShapeDtypeStruct((m, n), bf16)
    o_spec = pl.BlockSpec((tm, tn), lambda i, j, kk: (i, j))
    return pl.pallas_call(
        body, name="ffn_up", out_shape=(out, out, out), grid=(m // tm, n // tn, nk),
        in_specs=[pl.BlockSpec((tm, tk), lambda i, j, kk: (i, kk)), w_spec, w_spec],
        out_specs=(o_spec, o_spec, o_spec),
        scratch_shapes=[pltpu.VMEM((tm, tn), f32), pltpu.VMEM((tm, tn), f32)],
        compiler_params=_params(("parallel", "parallel", "arbitrary")),
    )(v, wg, wu)


def _ffn_down_bwd(dffn, wd, g, u):
    m, k = dffn.shape
    n = wd.shape[0]
    tm, tn, tk = _tile(m, MM_TM), _tile(n, 512), _tile(k, MM_TK)
    nk = k // tk

    def body(d_ref, w_ref, g_ref, u_ref, dg_ref, du_ref, acc):
        kk = pl.program_id(2)

        @pl.when(kk == 0)
        def _():
            acc[...] = jnp.zeros_like(acc)

        acc[...] += _dot(d_ref[...], w_ref[...], NT)

        @pl.when(kk == nk - 1)
        def _():
            dact = acc[...]
            gg, uu = g_ref[...].astype(f32), u_ref[...].astype(f32)
            sg = _sigmoid(gg)
            du_ref[...] = (dact * gg * sg).astype(bf16)
            dg_ref[...] = (dact * uu * (sg * (1.0 + gg * (1.0 - sg)))).astype(bf16)

    out = jax.ShapeDtypeStruct((m, n), bf16)
    o_spec = pl.BlockSpec((tm, tn), lambda i, j, kk: (i, j))
    return pl.pallas_call(
        body, name="ffn_down_bwd", out_shape=(out, out), grid=(m // tm, n // tn, nk),
        in_specs=[pl.BlockSpec((tm, tk), lambda i, j, kk: (i, kk)),
                  pl.BlockSpec((tn, tk), lambda i, j, kk: (j, kk)), o_spec, o_spec],
        out_specs=(o_spec, o_spec),
        scratch_shapes=[pltpu.VMEM((tm, tn), f32)],
        compiler_params=_params(("parallel", "parallel", "arbitrary")),
    )(dffn, wd, g, u)


def _blocked(fn, grid, ins, outs, accs, name, sp=None):
    n_in, n_out, n_acc = len(ins), len(outs), len(accs)
    nsp = 0 if sp is None else 1

    def body(*refs):
        refs = refs[nsp:]
        tiles = [r[...] for r in refs[:n_in]]
        res = fn(*tiles)
        if not isinstance(res, (tuple, list)):
            res = (res,)
        for r, val in zip(refs[n_in:n_in + n_out], res[:n_out]):
            r[...] = val.astype(r.dtype)
        if n_acc:
            first = pl.program_id(0) == 0
            for d in range(1, len(grid)):
                first = jnp.logical_and(first, pl.program_id(d) == 0)

            @pl.when(first)
            def _():
                for r in refs[n_in + n_out:]:
                    r[...] = jnp.zeros_like(r)

            for r, val in zip(refs[n_in + n_out:], res[n_out:]):
                r[...] += val

    def acc_map(shape):
        zeros = (0,) * len(shape)
        return lambda *idx: zeros

    in_specs = [pl.BlockSpec(bs, im) for _, bs, im in ins]
    out_specs = [pl.BlockSpec(bs, im) for _, _, bs, im in outs] + [pl.BlockSpec(s, acc_map(s)) for s in accs]
    out_shape = [jax.ShapeDtypeStruct(s, d) for s, d, _, _ in outs] + [jax.ShapeDtypeStruct(s, f32) for s in accs]
    sem = ("arbitrary",) * len(grid) if n_acc else ("parallel",) * len(grid)
    args = [a for a, _, _ in ins]
    if sp is None:
        res = pl.pallas_call(body, name=name, out_shape=out_shape, grid=grid, in_specs=in_specs,
                             out_specs=out_specs, compiler_params=_params(sem))(*args)
    else:
        gs = pltpu.PrefetchScalarGridSpec(num_scalar_prefetch=1, grid=grid, in_specs=in_specs, out_specs=out_specs)
        res = pl.pallas_call(body, name=name, out_shape=out_shape, grid_spec=gs,
                             compiler_params=_params(sem))(sp, *args)
    return res


def _rows(a, tm, cols=None, cb=0):
    w = a.shape[1] if cols is None else cols
    return (a, (tm, w), lambda i: (i, cb))


def _par(a):
    zeros = (0,) * a.ndim
    return (a, a.shape, lambda i: zeros)


def _rout(t, w, dtype, tm):
    return ((t, w), dtype, (tm, w), lambda i: (i, 0))


def _rowwise(fn, t, tm, ins, outs, accs, name):
    return _blocked(fn, (t // tm,), ins, outs, accs, name)


def _rms(x, w):
    r = lax.rsqrt(jnp.mean(x * x, axis=-1, keepdims=True) + EPS)
    return x * r * w


def _rms_bwd(x, w, dy):
    r = lax.rsqrt(jnp.mean(x * x, axis=-1, keepdims=True) + EPS)
    xh = x * r
    dyw = dy * w
    dx = r * (dyw - xh * jnp.mean(dyw * xh, axis=-1, keepdims=True))
    return dx, jnp.sum(dy * xh, axis=0, keepdims=True)


def _silu_grad(x):
    s = _sigmoid(x)
    return s * (1.0 + x * (1.0 - s))


def _rope(blk, cosp, sina, sinb):
    return blk * cosp + pltpu.roll(blk, 96, 1) * sina + pltpu.roll(blk, 32, 1) * sinb


def _rope_bwd(dy, cosp, sina, sinb):
    return dy * cosp + pltpu.roll(dy * sina, 32, 1) + pltpu.roll(dy * sinb, 96, 1)


HALO = 8


def _conv_taps(buf, w, tm, base):
    acc = buf[base:base + tm, :] * w[0:1]
    for k in range(1, SSD_K):
        acc = acc + buf[base + k:base + k + tm, :] * w[k:k + 1]
    return acc


def _conv_specs(t, tm, cdim):
    cur = pl.BlockSpec((tm, cdim), lambda i: (i, 0))
    prev = pl.BlockSpec((HALO, cdim), lambda i: (jnp.maximum(i * (tm // HALO) - 1, 0), 0))
    nxt = pl.BlockSpec((HALO, cdim), lambda i: (jnp.minimum((i + 1) * (tm // HALO), t // HALO - 1), 0))
    return cur, prev, nxt


def _conv_fwd(src, w, b, cdim, tm):
    t = src.shape[0]
    cur, prev, _ = _conv_specs(t, tm, cdim)

    def body(x_ref, p_ref, w_ref, b_ref, o_ref, buf):
        buf[0:HALO, :] = jnp.where(pl.program_id(0) > 0, p_ref[...], 0.0)
        buf[HALO:HALO + tm, :] = x_ref[...]
        pre = _conv_taps(buf, w_ref[...], tm, HALO - (SSD_K - 1)) + b_ref[...]
        o_ref[...] = pre * _sigmoid(pre)

    par = lambda a: pl.BlockSpec(a.shape, lambda i: (0, 0))
    return pl.pallas_call(
        body, name="conv_silu", out_shape=jax.ShapeDtypeStruct((t, cdim), f32), grid=(t // tm,),
        in_specs=[cur, prev, par(w), par(b)], out_specs=cur,
        scratch_shapes=[pltpu.VMEM((tm + HALO, cdim), f32)], compiler_params=_params(("parallel",)),
    )(src, src, w, b)


def _conv_bwd_pre(src, w, b, dact, cdim, tm):
    t = src.shape[0]
    cur, prev, _ = _conv_specs(t, tm, cdim)

    def body(x_ref, p_ref, w_ref, b_ref, d_ref, dpre_ref, dw0, dw1, dw2, dw3, db, buf):
        @pl.when(pl.program_id(0) == 0)
        def _():
            for r in (dw0, dw1, dw2, dw3, db):
                r[...] = jnp.zeros_like(r)

        buf[0:HALO, :] = jnp.where(pl.program_id(0) > 0, p_ref[...], 0.0)
        buf[HALO:HALO + tm, :] = x_ref[...]
        base = HALO - (SSD_K - 1)
        pre = _conv_taps(buf, w_ref[...], tm, base) + b_ref[...]
        dpre = d_ref[...] * _silu_grad(pre)
        dpre_ref[...] = dpre
        for k, r in enumerate((dw0, dw1, dw2, dw3)):
            r[...] += jnp.sum(dpre * buf[base + k:base + k + tm, :], axis=0, keepdims=True)
        db[...] += jnp.sum(dpre, axis=0, keepdims=True)

    par = lambda a: pl.BlockSpec(a.shape, lambda i: (0, 0))
    acc = pl.BlockSpec((1, cdim), lambda i: (0, 0))
    acc_shape = jax.ShapeDtypeStruct((1, cdim), f32)
    return pl.pallas_call(
        body, name="conv_silu_bwd", out_shape=[jax.ShapeDtypeStruct((t, cdim), f32)] + [acc_shape] * 5,
        grid=(t // tm,), in_specs=[cur, prev, par(w), par(b), cur], out_specs=[cur] + [acc] * 5,
        scratch_shapes=[pltpu.VMEM((tm + HALO, cdim), f32)], compiler_params=_params(("arbitrary",)),
    )(src, src, w, b, dact)


def _conv_bwd_dx(dpre, w, tm):
    t, cdim = dpre.shape
    cur, _, nxt = _conv_specs(t, tm, cdim)
    last = t // tm - 1

    def body(d_ref, n_ref, w_ref, o_ref, buf):
        buf[0:tm, :] = d_ref[...]
        buf[tm:tm + HALO, :] = jnp.where(pl.program_id(0) < last, n_ref[...], 0.0)
        wv = w_ref[...]
        acc = buf[0:tm, :] * wv[SSD_K - 1:SSD_K]
        for k in range(SSD_K - 1):
            s = SSD_K - 1 - k
            acc = acc + buf[s:s + tm, :] * wv[k:k + 1]
        o_ref[...] = acc.astype(bf16)

    return pl.pallas_call(
        body, name="conv_bwd_dx", out_shape=jax.ShapeDtypeStruct((t, cdim), bf16), grid=(t // tm,),
        in_specs=[cur, nxt, pl.BlockSpec(w.shape, lambda i: (0, 0))], out_specs=cur,
        scratch_shapes=[pltpu.VMEM((tm + HALO, cdim), f32)], compiler_params=_params(("parallel",)),
    )(dpre, dpre, w)


def _causal_mask(s):
    row = lax.broadcasted_iota(jnp.int32, s.shape, 0)
    col = lax.broadcasted_iota(jnp.int32, s.shape, 1)
    return jnp.where(row >= col, s, -jnp.inf)


def _causal_pairs(nq, q_major):
    pairs = [(qi, ki) for qi in range(nq) for ki in range(qi + 1)]
    if not q_major:
        pairs.sort(key=lambda p: (p[1], p[0]))
    return (jnp.asarray([p[0] for p in pairs], jnp.int32), jnp.asarray([p[1] for p in pairs], jnp.int32))


def _flash_fwd(q, kv, kr, nheads, scale, comm=None, at=None):
    t = q.shape[0]
    tq = _tile(t, ATT_TILE)
    nq = t // tq

    qtab, ktab = _causal_pairs(nq, q_major=True)
    hp = 2 if nheads % 2 == 0 else 1

    def body(qt, kt, q_ref, kv_ref, kr_ref, o_ref, lse_ref, m_sc, l_sc, acc_sc):
        qi, ki = qt[pl.program_id(1)], kt[pl.program_id(1)]

        @pl.when(ki == 0)
        def _():
            m_sc[...] = jnp.full_like(m_sc, -jnp.inf)
            l_sc[...] = jnp.zeros_like(l_sc)
            acc_sc[...] = jnp.zeros_like(acc_sc)

        def step(masked, last):
            krb = kr_ref[...]
            for j in range(hp):
                kn = kv_ref[:, j * QK_PAD: j * QK_PAD + NOPE]
                v = kv_ref[:, j * QK_PAD + NOPE:(j + 1) * QK_PAD]
                k = jnp.concatenate([kn, krb], axis=1)
                s = lax.dot_general(q_ref[:, j * QK_PAD:(j + 1) * QK_PAD], k, NT, preferred_element_type=f32) * scale
                if masked:
                    s = _causal_mask(s)
                m_old = m_sc[j]
                m_new = jnp.maximum(m_old, jnp.max(s, axis=1, keepdims=True))
                alpha = jnp.exp(m_old - m_new)
                p = jnp.exp(s - m_new)
                l = alpha * l_sc[j] + jnp.sum(p, axis=1, keepdims=True)
                acc = alpha * acc_sc[j] + lax.dot_general(p.astype(bf16), v, NN, preferred_element_type=f32)
                if last:
                    o_ref[:, j * V_HEAD:(j + 1) * V_HEAD] = acc / l
                    lse_ref[:, j * V_HEAD:(j + 1) * V_HEAD] = jnp.broadcast_to(m_new + jnp.log(l), (tq, V_HEAD))
                else:
                    l_sc[j] = l
                    acc_sc[j] = acc
                    m_sc[j] = m_new

        @pl.when(ki < qi)
        def _():
            step(False, False)

        @pl.when(ki == qi)
        def _():
            step(True, True)

    o_spec = pl.BlockSpec((tq, hp * V_HEAD), lambda h, s, qt, kt: (qt[s], h))
    out = jax.ShapeDtypeStruct((t, nheads * V_HEAD), f32)
    (o, lse), cres = _call(
        body, "flash_fwd", [out, out], (nheads // hp, len(qtab)),
        [pl.BlockSpec((tq, hp * QK_PAD), lambda h, s, qt, kt: (qt[s], h)),
         pl.BlockSpec((tq, hp * QK_PAD), lambda h, s, qt, kt: (kt[s], h)),
         pl.BlockSpec((tq, LANES), lambda h, s, qt, kt: (kt[s], 0))],
        [o_spec, o_spec],
        [pltpu.VMEM((hp, tq, 1), f32), pltpu.VMEM((hp, tq, 1), f32), pltpu.VMEM((hp, tq, V_HEAD), f32)],
        ("parallel", "arbitrary"), [q, kv, kr], comm, at, prefetch=(qtab, ktab))
    return o, lse, cres


def _flash_bwd(q, kv, kr, do, lse, delta, nheads, scale, comm=None, at=None):
    t = q.shape[0]
    tq = _tile(t, ATT_TILE)
    nq = t // tq
    qtab, ktab = _causal_pairs(nq, q_major=False)

    def body(qt, kt, q_ref, kn_ref, kr_ref, v_ref, do_ref, lse_ref, dl_ref, dkv_ref, dkr_ref, dq_ref, dk_sc, dv_sc):
        qi, ki = qt[pl.program_id(1)], kt[pl.program_id(1)]

        @pl.when(pl.program_id(1) == 0)
        def _():
            dq_ref[...] = jnp.zeros_like(dq_ref)

        @pl.when(qi == ki)
        def _():
            dk_sc[...] = jnp.zeros_like(dk_sc)
            dv_sc[...] = jnp.zeros_like(dv_sc)

        def step(masked):
            qb = q_ref[...]
            dob = do_ref[...]
            k = jnp.concatenate([kn_ref[...], kr_ref[...]], axis=1)
            s = lax.dot_general(qb, k, NT, preferred_element_type=f32) * scale
            if masked:
                s = _causal_mask(s)
            p = jnp.exp(s - lse_ref[:, 0:1])
            dv_sc[...] += lax.dot_general(p.astype(bf16), dob, TN, preferred_element_type=f32)
            dp = lax.dot_general(dob, v_ref[...], NT, preferred_element_type=f32)
            ds = (p * (dp - dl_ref[:, 0:1]) * scale).astype(bf16)
            dk_sc[...] += lax.dot_general(ds, qb, TN, preferred_element_type=f32)
            rows = pl.ds(pl.multiple_of(qi * tq, tq), tq)
            dq_ref[rows, :] += lax.dot_general(ds, k, NN, preferred_element_type=f32)

        @pl.when(qi > ki)
        def _():
            step(False)

        @pl.when(qi == ki)
        def _():
            step(True)

        @pl.when(qi == nq - 1)
        def _():
            dk = dk_sc[...]
            dkv_ref[...] = jnp.concatenate([dk[:, :NOPE], dv_sc[...]], axis=1).astype(bf16)
            dkr_ref[...] = dk[:, NOPE:]

    hspec = pl.BlockSpec((tq, V_HEAD), lambda h, s, qt, kt: (qt[s], h))
    (dkv, dkr, dq), cres = _call(
        body, "flash_bwd",
        [jax.ShapeDtypeStruct((t, nheads * QK_PAD), bf16), jax.ShapeDtypeStruct((t, nheads * LANES), f32),
         jax.ShapeDtypeStruct((t, nheads * QK_PAD), f32)],
        (nheads, len(qtab)),
        [pl.BlockSpec((tq, QK_PAD), lambda h, s, qt, kt: (qt[s], h)),
         pl.BlockSpec((tq, NOPE), lambda h, s, qt, kt: (kt[s], 2 * h)),
         pl.BlockSpec((tq, LANES), lambda h, s, qt, kt: (kt[s], 0)),
         pl.BlockSpec((tq, V_HEAD), lambda h, s, qt, kt: (kt[s], 2 * h + 1)),
         hspec, hspec, hspec],
        [pl.BlockSpec((tq, QK_PAD), lambda h, s, qt, kt: (kt[s], h)),
         pl.BlockSpec((tq, LANES), lambda h, s, qt, kt: (kt[s], h)),
         pl.BlockSpec((t, QK_PAD), lambda h, s, qt, kt: (0, h))],
        [pltpu.VMEM((tq, QK_PAD), f32), pltpu.VMEM((tq, V_HEAD), f32)],
        ("parallel", "arbitrary"), [q, kv, kr, kv, do, lse, delta], comm, at, prefetch=(qtab, ktab))
    return dkv, dkr, dq, cres


def _split3(a):
    hi = a.astype(bf16)
    r1 = a - hi.astype(f32)
    mid = r1.astype(bf16)
    lo = (r1 - mid.astype(f32)).astype(bf16)
    return hi, mid, lo


def _split2(a):
    hi = a.astype(bf16)
    return hi, (a - hi.astype(f32)).astype(bf16)


def _ones_dot_left(tri, a):
    hi, mid, lo = _split3(a)
    d = lambda v: lax.dot_general(tri, v, NN, preferred_element_type=f32)
    return d(hi) + d(mid) + d(lo)


def _ones_dot_right(a, tri):
    hi, mid, lo = _split3(a)
    d = lambda v: lax.dot_general(v, tri, NN, preferred_element_type=f32)
    return d(hi) + d(mid) + d(lo)


def _softplus(x):
    return jnp.maximum(x, 0.0) + jnp.log(1.0 + jnp.exp(-jnp.abs(x)))


def _head_spread(hs, width):
    shift = SSD_P.bit_length() - 1
    return (lax.broadcasted_iota(jnp.int32, (hs, width), 0)
            == lax.shift_right_logical(lax.broadcasted_iota(jnp.int32, (hs, width), 1), shift)).astype(bf16)


def _ssd_common(dt_ref, dtT_ref, dtb_ref, dtbT_ref, alog_ref, alogT_ref):
    ii = lax.broadcasted_iota(jnp.int32, (CHUNK, CHUNK), 0)
    jj = lax.broadcasted_iota(jnp.int32, (CHUNK, CHUNK), 1)
    tri = ii >= jj
    raw = dt_ref[...] + dtb_ref[...]
    dt = _softplus(raw)
    a_neg = -jnp.exp(alog_ref[...])
    cum = _ones_dot_left(tri.astype(bf16), dt * a_neg)
    dt_t = _softplus(dtT_ref[...] + dtbT_ref[...])
    cum_t = _ones_dot_right(dt_t * (-jnp.exp(alogT_ref[...])), (ii <= jj).astype(bf16))
    return tri, raw, dt, a_neg, cum, cum_t


def _ssd_fwd(xbc, dt, dt_t, dtb, dtb_t, alog, alog_t, dskip, width):
    t, cdim = xbc.shape
    hs = dt.shape[1]
    nc = t // CHUNK
    epg = hs // SSD_G
    gn = SSD_G * SSD_N

    def body(x_ref, dt_ref, dtT_ref, dtb_ref, dtbT_ref, alog_ref, alogT_ref, d_ref, y_ref, hp_ref, h_sc, yd_sc):
        @pl.when(pl.program_id(0) == 0)
        def _():
            h_sc[...] = jnp.zeros_like(h_sc)

        tri, _, dtv, _, cum, cum_t = _ssd_common(dt_ref, dtT_ref, dtb_ref, dtbT_ref, alog_ref, alogT_ref)
        spread = _head_spread(hs, width)
        clast = cum[CHUNK - 1:CHUNK, :]
        dec = jnp.exp(clast)
        wide = _ones_dot_right(jnp.concatenate([dtv, jnp.exp(cum), jnp.exp(clast - cum),
                                                jnp.broadcast_to(d_ref[...], (CHUNK, hs))], axis=0), spread)
        dt_x, ee_x, ff_x, dsk_x = (wide[i * CHUNK:(i + 1) * CHUNK] for i in range(4))
        xs = x_ref[:, :width]
        xdt = xs * dt_x
        xf = xdt * ff_x
        h_all = h_sc[...]
        hp_ref[0] = h_all
        h_all = h_all.reshape(hs * SSD_P, SSD_N)
        ch_parts = []
        for g in range(SSD_G):
            gsl = slice(g * epg * SSD_P, (g + 1) * epg * SSD_P)
            bb = x_ref[:, width + g * SSD_N: width + (g + 1) * SSD_N].astype(bf16)
            cb_ = x_ref[:, width + gn + g * SSD_N: width + gn + (g + 1) * SSD_N].astype(bf16)
            cbm = lax.dot_general(cb_, bb, NT, preferred_element_type=f32)
            ch_parts.append(_dot(cb_, h_all[gsl], NT))
            st = _dot(xf[:, gsl], bb, TN)
            for e in range(g * epg, (g + 1) * epg):
                esl = slice(e * SSD_P, (e + 1) * SSD_P)
                lmat = jnp.exp(jnp.where(tri, cum[:, e:e + 1] - cum_t[e:e + 1, :], -jnp.inf))
                yd_sc[:, esl] = _dot(cbm * lmat, xdt[:, esl], NN)
                j = e - g * epg
                h_sc[e] = h_sc[e] * dec[:, e:e + 1] + st[j * SSD_P:(j + 1) * SSD_P, :]
        y_ref[...] = yd_sc[...] + jnp.concatenate(ch_parts, axis=1) * ee_x + xs * dsk_x

    par = lambda a: pl.BlockSpec(a.shape, lambda i: (0,) * a.ndim)
    return pl.pallas_call(
        body, name="ssd_fwd",
        out_shape=(jax.ShapeDtypeStruct((t, width), f32), jax.ShapeDtypeStruct((nc, hs, SSD_P, SSD_N), f32)),
        grid=(nc,),
        in_specs=[pl.BlockSpec((CHUNK, cdim), lambda i: (i, 0)), pl.BlockSpec((CHUNK, hs), lambda i: (i, 0)),
                  pl.BlockSpec((hs, CHUNK), lambda i: (0, i)), par(dtb), par(dtb_t), par(alog), par(alog_t), par(dskip)],
        out_specs=(pl.BlockSpec((CHUNK, width), lambda i: (i, 0)),
                   pl.BlockSpec((1, hs, SSD_P, SSD_N), lambda i: (i, 0, 0, 0))),
        scratch_shapes=[pltpu.VMEM((hs, SSD_P, SSD_N), f32), pltpu.VMEM((CHUNK, width), f32)],
        compiler_params=_params(("arbitrary",)),
    )(xbc, dt, dt_t, dtb, dtb_t, alog, alog_t, dskip)


def _ssd_bwd(xbc, dt, dt_t, dtb, dtb_t, alog, alog_t, dskip, hprev, dy, width, comm=None, at=None):
    t, cdim = xbc.shape
    hs = dt.shape[1]
    nc = t // CHUNK
    epg = hs // SSD_G
    gn = SSD_G * SSD_N

    def body(x_ref, dt_ref, dtT_ref, dtb_ref, dtbT_ref, alog_ref, alogT_ref, d_ref, hp_ref, dy_ref,
             dx_ref, ddt_ref, dalog_ref, ddsk_ref, ddtb_ref, dh_sc, dxd_sc):
        @pl.when(pl.program_id(0) == 0)
        def _():
            dh_sc[...] = jnp.zeros_like(dh_sc)
            dalog_ref[...] = jnp.zeros_like(dalog_ref)
            ddsk_ref[...] = jnp.zeros_like(ddsk_ref)
            ddtb_ref[...] = jnp.zeros_like(ddtb_ref)

        tri, raw, dtv, a_neg, cum, cum_t = _ssd_common(dt_ref, dtT_ref, dtb_ref, dtbT_ref, alog_ref, alogT_ref)
        dsk = d_ref[...]
        head_row = lax.broadcasted_iota(jnp.int32, (1, hs), 1)
        head_col = lax.broadcasted_iota(jnp.int32, (hs, 1), 0)
        last_row = (lax.broadcasted_iota(jnp.int32, (CHUNK, 1), 0) == CHUNK - 1).astype(f32)
        shift = SSD_P.bit_length() - 1
        spread = _head_spread(hs, width)
        gather = (lax.shift_right_logical(lax.broadcasted_iota(jnp.int32, (width, hs), 0), shift)
                  == lax.broadcasted_iota(jnp.int32, (width, hs), 1)).astype(bf16)
        clast = cum[CHUNK - 1:CHUNK, :]
        ee = jnp.exp(cum)
        ff = jnp.exp(clast - cum)
        dec = jnp.exp(clast)
        wide = _ones_dot_right(jnp.concatenate([dtv, ee, ff, jnp.broadcast_to(dsk, (CHUNK, hs))], axis=0), spread)
        dt_x, ee_x, ff_x, dsk_x = (wide[i * CHUNK:(i + 1) * CHUNK] for i in range(4))
        xs = x_ref[:, :width]
        dyv = dy_ref[...]
        xdt = xs * dt_x
        dye = dyv * ee_x
        xf = xdt * ff_x
        h_all = hp_ref[0].reshape(hs * SSD_P, SSD_N)
        dh_all = dh_sc[...].reshape(hs * SSD_P, SSD_N)
        hi, lo = _split2(_ones_dot_left(spread, dh_all * h_all))
        ones8 = jnp.ones((8, SSD_N), bf16)
        hh = (lax.dot_general(ones8, hi, NT, preferred_element_type=f32)
              + lax.dot_general(ones8, lo, NT, preferred_element_type=f32))[0:1]
        rowsum_m = jnp.zeros((CHUNK, hs), f32)
        colsum_m = jnp.zeros((hs, CHUNK), f32)
        ch_parts, bds_parts = [], []
        for g in range(SSD_G):
            bsl = slice(width + g * SSD_N, width + (g + 1) * SSD_N)
            csl = slice(width + gn + g * SSD_N, width + gn + (g + 1) * SSD_N)
            gsl = slice(g * epg * SSD_P, (g + 1) * epg * SSD_P)
            bb = x_ref[:, bsl].astype(bf16)
            cb_ = x_ref[:, csl].astype(bf16)
            cbm = lax.dot_general(cb_, bb, NT, preferred_element_type=f32)
            hg = h_all[gsl].astype(bf16)
            dhg = dh_all[gsl].astype(bf16)
            ch_parts.append(_dot(cb_, hg, NT))
            bds_parts.append(_dot(bb, dhg, NT))
            dcg = _dot(dye[:, gsl], hg, NN)
            dbg = _dot(xf[:, gsl], dhg, NN)
            dh_new = _dot(dye[:, gsl], cb_, TN)
            dcb = jnp.zeros((CHUNK, CHUNK), f32)
            for e in range(g * epg, (g + 1) * epg):
                esl = slice(e * SSD_P, (e + 1) * SSD_P)
                lmat = jnp.exp(jnp.where(tri, cum[:, e:e + 1] - cum_t[e:e + 1, :], -jnp.inf))
                gmat = cbm * lmat
                dy_e = dyv[:, esl].astype(bf16)
                dgm = _dot(dy_e, xdt[:, esl], NT)
                dxd_sc[:, esl] = _dot(gmat, dy_e, TN)
                dcb = dcb + dgm * lmat
                mm = dgm * gmat
                rowsum_m = rowsum_m + jnp.sum(mm, axis=1, keepdims=True) * (head_row == e).astype(f32)
                colsum_m = colsum_m + (head_col == e).astype(f32) * jnp.sum(mm, axis=0, keepdims=True)
                j = e - g * epg
                dh_sc[e] = dh_new[j * SSD_P:(j + 1) * SSD_P, :] + dec[:, e:e + 1] * dh_sc[e]
            dx_ref[:, bsl] = dbg + _dot(dcb, cb_, TN)
            dx_ref[:, csl] = dcg + _dot(dcb, bb, NN)
        ch_all = jnp.concatenate(ch_parts, axis=1)
        bds_all = jnp.concatenate(bds_parts, axis=1)
        dxdt = bds_all * ff_x + dxd_sc[...]
        dx_ref[:, :width] = dxdt * dt_x + dyv * dsk_x
        sums = _ones_dot_right(jnp.concatenate([dxdt * xs, dyv * ch_all, bds_all * xdt, dyv * xs], axis=0), gather)
        ddtx_all = sums[0:CHUNK]
        dff = sums[2 * CHUNK:3 * CHUNK] * ff
        ddsk = jnp.sum(sums[3 * CHUNK:4 * CHUNK], axis=0, keepdims=True)
        dclast = jnp.sum(dff, axis=0, keepdims=True) + dec * hh
        eye = (lax.broadcasted_iota(jnp.int32, (hs, hs), 0) == lax.broadcasted_iota(jnp.int32, (hs, hs), 1)).astype(bf16)
        colsum_t = sum(lax.dot_general(v, eye, TN, preferred_element_type=f32) for v in _split3(colsum_m))
        dcum_all = sums[CHUNK:2 * CHUNK] * ee - dff + rowsum_m - colsum_t + dclast * last_row
        ii = lax.broadcasted_iota(jnp.int32, (CHUNK, CHUNK), 0)
        jj = lax.broadcasted_iota(jnp.int32, (CHUNK, CHUNK), 1)
        da = _ones_dot_left((jj >= ii).astype(bf16), dcum_all)
        ddt = da * a_neg + ddtx_all
        dalog_ref[...] += jnp.sum(da * dtv, axis=0, keepdims=True) * a_neg
        draw = ddt * _sigmoid(raw)
        ddt_ref[...] = draw
        ddtb_ref[...] += jnp.sum(draw, axis=0, keepdims=True)
        ddsk_ref[...] += ddsk

    rev = lambda i: nc - 1 - i
    par = lambda a: pl.BlockSpec(a.shape, lambda i: (0,) * a.ndim)
    acc = pl.BlockSpec((1, hs), lambda i: (0, 0))
    acc_shape = jax.ShapeDtypeStruct((1, hs), f32)
    res, cres = _call(
        body, "ssd_bwd",
        [jax.ShapeDtypeStruct((t, cdim), f32), jax.ShapeDtypeStruct((t, hs), f32), acc_shape, acc_shape, acc_shape],
        (nc,),
        [pl.BlockSpec((CHUNK, cdim), lambda i: (rev(i), 0)), pl.BlockSpec((CHUNK, hs), lambda i: (rev(i), 0)),
         pl.BlockSpec((hs, CHUNK), lambda i: (0, rev(i))), par(dtb), par(dtb_t), par(alog), par(alog_t),
         par(dskip), pl.BlockSpec((1, hs, SSD_P, SSD_N), lambda i: (rev(i), 0, 0, 0)),
         pl.BlockSpec((CHUNK, width), lambda i: (rev(i), 0))],
        [pl.BlockSpec((CHUNK, cdim), lambda i: (rev(i), 0)), pl.BlockSpec((CHUNK, hs), lambda i: (rev(i), 0)),
         acc, acc, acc],
        [pltpu.VMEM((hs, SSD_P, SSD_N), f32), pltpu.VMEM((CHUNK, width), f32)], ("arbitrary",),
        [xbc, dt, dt_t, dtb, dtb_t, alog, alog_t, dskip, hprev, dy], comm, at)
    return (*res, cres)


def _where_am_i():
    x, y, c = lax.axis_index("x"), lax.axis_index("y"), lax.axis_index("c")
    chips = [(1 - x, y), (x, 1 - y), (1 - x, 1 - y)]
    return x, y, c, chips


def _remote(src, dst, send_sems, recv_sems, k, to):
    return pltpu.make_async_remote_copy(src_ref=src, dst_ref=dst, send_sem=send_sems.at[k], recv_sem=recv_sems.at[k],
                                        device_id=to, device_id_type=MESH)


class _Comm:
    def __init__(self, args, outs, nsem, nphase, run):
        self.args, self.outs, self.nsem, self.nphase, self.run = list(args), list(outs), nsem, nphase, run

    def sems(self):
        return [pltpu.SemaphoreType.DMA((self.nsem,)), pltpu.SemaphoreType.DMA((self.nsem,))]


def _comm_call(comm, name):
    na, no = len(comm.args), len(comm.outs)

    def body(*refs):
        for phase in range(comm.nphase):
            comm.run(phase, refs[:na], refs[na:na + no], refs[na + no], refs[na + no + 1])

    return list(pl.pallas_call(body, name=name, out_shape=comm.outs, in_specs=[ANY] * na, out_specs=[ANY] * no,
                               scratch_shapes=comm.sems())(*comm.args))


def _half(ref, c, r2):
    return ref.at[pl.ds(c * r2, r2)]


def _gather_weights(shards, wholes):
    ns, nw = len(shards), len(wholes)
    per = 7

    def run(phase, srcs, outs, send_sems, recv_sems):
        x, y, c, chips = _where_am_i()
        me, sib = 2 * x + y, (x, y, 1 - c)
        def first():
            cps = []
            for w in range(ns + nw):
                src, out, base = srcs[w], outs[w], per * w
                halved = w < ns
                r2 = src.shape[0] // 2
                piece = _half(src, c, r2) if halved else src
                for j, (cx, cy) in enumerate(chips):
                    dst = _half(out.at[me], c, r2) if halved else out.at[me]
                    cps.append(_remote(piece, dst, send_sems, recv_sems, base + j, (cx, cy, c)))
                cps.append(_remote(src, out.at[me], send_sems, recv_sems, base + 6, sib))
            return cps

        def passed():
            cps = []
            for w in range(ns):
                r2 = srcs[w].shape[0] // 2
                for j, (cx, cy) in enumerate(chips):
                    got = _half(outs[w].at[2 * cx + cy], c, r2)
                    cps.append(_remote(got, got, send_sems, recv_sems, per * w + 3 + j, sib))
            return cps

        if phase == 0:
            for cp in first():
                cp.start()
        elif phase == 1:
            it = iter(passed())
            for w in range(ns + nw):
                src, out, base = srcs[w], outs[w], per * w
                r2 = src.shape[0] // 2
                for j, (cx, cy) in enumerate(chips):
                    got = _half(out.at[2 * cx + cy], c, r2) if w < ns else out.at[2 * cx + cy]
                    _remote(got, got, send_sems, recv_sems, base + j, sib).wait_recv()
                    if w < ns:
                        next(it).start()
        else:
            for w in range(ns + nw):
                src, out, base = srcs[w], outs[w], per * w
                r2 = src.shape[0] // 2
                if w < ns:
                    for j, (cx, cy) in enumerate(chips):
                        got = _half(out.at[2 * cx + cy], 1 - c, r2)
                        _remote(got, got, send_sems, recv_sems, base + 3 + j, sib).wait_recv()
                _remote(src, out.at[me], send_sems, recv_sems, base + 6, sib).wait_recv()
            for cp in first() + passed():
                cp.wait_send()

    args = list(shards) + list(wholes)
    outs = [jax.ShapeDtypeStruct((4,) + a.shape, a.dtype) for a in args]
    return _Comm(args, outs, per * len(args), 3, run)


def _sibling_send_halves(gs):
    n = len(gs)

    def run(phase, srcs, outs, send_sems, recv_sems):
        x, y, c, _ = _where_am_i()
        sib = (x, y, 1 - c)
        cps = []
        for w in range(n):
            r2 = srcs[w].shape[1] // 2
            for k in range(4):
                cps.append(_remote(_half(srcs[w].at[k], 1 - c, r2), outs[w].at[k], send_sems, recv_sems, 4 * w + k, sib))
        for cp in cps:
            cp.start() if phase == 0 else cp.wait()

    outs = [jax.ShapeDtypeStruct((4, g.shape[1] // 2, g.shape[2]), g.dtype) for g in gs]
    return _Comm(gs, outs, 4 * n, 2, run)


def _chips_exchange(ps):
    n = len(ps)

    def run(phase, srcs, outs, send_sems, recv_sems):
        x, y, c, chips = _where_am_i()
        me = 2 * x + y
        cps = []
        for w in range(n):
            for j, (cx, cy) in enumerate(chips):
                cps.append(_remote(srcs[w].at[2 * cx + cy], outs[w].at[me], send_sems, recv_sems, 3 * w + j, (cx, cy, c)))
        if phase == 0:
            for cp in cps:
                cp.start()
        else:
            for w in range(n):
                for j, (cx, cy) in enumerate(chips):
                    got = outs[w].at[2 * cx + cy]
                    _remote(got, got, send_sems, recv_sems, 3 * w + j, (cx, cy, c)).wait_recv()
            for cp in cps:
                cp.wait_send()

    outs = [jax.ShapeDtypeStruct(p.shape, p.dtype) for p in ps]
    return _Comm(ps, outs, 3 * n, 2, run)


def _sibling_swap(rs):
    n = len(rs)

    def run(phase, srcs, outs, send_sems, recv_sems):
        x, y, c, _ = _where_am_i()
        for w in range(n):
            cp = _remote(srcs[w], outs[w], send_sems, recv_sems, w, (x, y, 1 - c))
            cp.start() if phase == 0 else cp.wait()

    outs = [jax.ShapeDtypeStruct(r.shape, r.dtype) for r in rs]
    return _Comm(rs, outs, n, 2, run)


def _gather_all(v):
    rows = v.shape[0]

    def body(x_ref, out_ref, send_sems, recv_sems, local_sem):
        x, y, c, chips = _where_am_i()
        me, sib = (x, y, c), (x, y, 1 - c)
        blk = lambda px, py, pc: out_ref.at[4 * px + 2 * py + pc]
        mine = pltpu.make_async_copy(x_ref, blk(*me), local_sem)
        mine.start()
        first = [_remote(x_ref, blk(*me), send_sems, recv_sems, 0, sib)]
        first += [_remote(x_ref, blk(*me), send_sems, recv_sems, 1 + j, (*chip, c)) for j, chip in enumerate(chips)]
        for cp in first:
            cp.start()
        passed = [_remote(blk(*chip, c), blk(*chip, c), send_sems, recv_sems, 4 + j, sib) for j, chip in enumerate(chips)]
        for j, chip in enumerate(chips):
            _remote(blk(*chip, c), blk(*chip, c), send_sems, recv_sems, 1 + j, me).wait_recv()
            passed[j].start()
        _remote(blk(*sib), blk(*sib), send_sems, recv_sems, 0, me).wait_recv()
        for j, chip in enumerate(chips):
            _remote(blk(*chip, 1 - c), blk(*chip, 1 - c), send_sems, recv_sems, 4 + j, me).wait_recv()
        for cp in first + passed:
            cp.wait_send()
        mine.wait()

    vm = pl.BlockSpec(memory_space=pltpu.VMEM)
    return pl.pallas_call(
        body, name="gather_all", out_shape=jax.ShapeDtypeStruct((8, rows, LANES), v.dtype),
        in_specs=[vm], out_specs=vm,
        scratch_shapes=[pltpu.SemaphoreType.DMA((7,)), pltpu.SemaphoreType.DMA((7,)), pltpu.SemaphoreType.DMA],
    )(v)


def _flat_pad(parts, total):
    v = jnp.concatenate([p.reshape(-1) for p in parts])
    return jnp.pad(v, (0, total - v.shape[0]))


def _adamw(w, g, m, v):
    m = ADAM_B1 * m + (1.0 - ADAM_B1) * g
    v = ADAM_B2 * v + (1.0 - ADAM_B2) * jnp.square(g)
    m_hat = m / (1.0 - ADAM_B1 ** ADAM_STEP)
    v_hat = v / (1.0 - ADAM_B2 ** ADAM_STEP)
    delta = -ADAM_LR * (m_hat / (jnp.sqrt(v_hat) + ADAM_EPS) + ADAM_WD * w)
    return delta, m, v


def _adamw_call(w, g, m, v, name):
    r, cdim = w.shape
    tm = _tile(r, ROW_TILE, 8)
    o = _rout(r, cdim, f32, tm)
    return _rowwise(_adamw, r, tm, [_rows(w, tm), _rows(g, tm), _rows(m, tm), _rows(v, tm)], [o, o, o], [], name)


def _adamw_halves(w, m, v, g_mine, g_sib, sp, name):
    _, r, cdim = w.shape
    r2 = r // 2
    tr = _tile(r2, ROW_TILE, 8)
    nb = r2 // tr

    def body(sp_ref, w_ref, m_ref, v_ref, ga_ref, gb_ref, g_out, d_out, m_out, v_out):
        g = jnp.where(pl.program_id(0) == sp_ref[1], ga_ref[...], gb_ref[...])
        delta, mn, vn = _adamw(w_ref[...], g, m_ref[...], v_ref[...])
        g_out[...] = g
        d_out[...] = delta
        m_out[...] = mn
        v_out[...] = vn

    full = pl.BlockSpec((None, tr, cdim), lambda h, i, s: (0, h * nb + i, 0))
    mine = pl.BlockSpec((tr, cdim), lambda h, i, s: (jnp.where(h == s[1], i, 0), 0))
    sib = pl.BlockSpec((tr, cdim), lambda h, i, s: (jnp.where(h == s[1], 0, i), 0))
    out = jax.ShapeDtypeStruct((1, r, cdim), f32)
    gs = pltpu.PrefetchScalarGridSpec(num_scalar_prefetch=1, grid=(2, nb), in_specs=[full, full, full, mine, sib],
                                      out_specs=[full, full, full, full])
    return pl.pallas_call(body, name=name, out_shape=[out, out, out, out], grid_spec=gs,
                          compiler_params=_params(("parallel", "parallel")))(sp, w, m, v, g_mine, g_sib)


MIXER = ("w_in", "w_uq", "w_ukv", "w_out")
FFN = ("w_gate", "w_up", "w_down")
BIG = MIXER + FFN
SMALL = ("q_norm_w", "kv_norm_w", "conv_w", "conv_b", "dt_bias", "a_log", "d_skip", "ssd_norm_w", "attn_out_norm_w",
         "pre_mix_norm_w", "post_mix_norm_w", "pre_ffn_norm_w", "post_ffn_norm_w")
ORDER = ("w_in", "q_norm_w", "w_uq", "kv_norm_w", "w_ukv", "conv_w", "conv_b", "dt_bias", "a_log", "d_skip",
         "ssd_norm_w", "attn_out_norm_w", "w_out", "pre_mix_norm_w", "post_mix_norm_w", "pre_ffn_norm_w",
         "post_ffn_norm_w", "w_gate", "w_up", "w_down")


def kernel(x, positions, w_in, q_norm_w, w_uq, kv_norm_w, w_ukv, conv_w, conv_b, dt_bias, a_log, d_skip, ssd_norm_w, attn_out_norm_w, w_out, pre_mix_norm_w, post_mix_norm_w, pre_ffn_norm_w, post_ffn_norm_w, w_gate, w_up, w_down, loss_target, m_w_in, m_q_norm_w, m_w_uq, m_kv_norm_w, m_w_ukv, m_conv_w, m_conv_b, m_dt_bias, m_a_log, m_d_skip, m_ssd_norm_w, m_attn_out_norm_w, m_w_out, m_pre_mix_norm_w, m_post_mix_norm_w, m_pre_ffn_norm_w, m_post_ffn_norm_w, m_w_gate, m_w_up, m_w_down, v_w_in, v_q_norm_w, v_w_uq, v_kv_norm_w, v_w_ukv, v_conv_w, v_conv_b, v_dt_bias, v_a_log, v_d_skip, v_ssd_norm_w, v_attn_out_norm_w, v_w_out, v_pre_mix_norm_w, v_post_mix_norm_w, v_pre_ffn_norm_w, v_post_ffn_norm_w, v_w_gate, v_w_up, v_w_down):
    local = dict(locals())
    wts = {n: local[n][0] for n in ORDER}
    mom_m = {n: local["m_" + n][0] for n in ORDER}
    mom_v = {n: local["v_" + n][0] for n in ORDER}
    xs = x[0]
    tgt = loss_target[0]
    t, d = xs.shape
    nchip = 4
    my_x, my_y, my_c = lax.axis_index("x"), lax.axis_index("y"), lax.axis_index("c")
    my_chip = 2 * my_x + my_y

    mla_w = d // 2
    nh = mla_w // V_HEAD
    width = d - mla_w
    hs = width // SSD_P
    gn = SSD_G * SSD_N
    cdim = width + 2 * gn
    in_sizes = (Q_RANK, KV_RANK, ROPE, width, cdim, hs)
    d_in = sum(in_sizes)
    tail = LANES
    off_xbc = 0
    off_cq = cdim
    off_ckv = off_cq + Q_RANK
    off_z = off_ckv + KV_RANK
    off_tail = off_z + width
    d_in_p = _round_up(off_tail + tail, 256)
    gw = width // SSD_G
    assert off_cq % Q_RANK == 0 and off_ckv % KV_RANK == 0 and off_z % gw == 0 and off_tail % LANES == 0
    qk_head = NOPE + ROPE
    scale = qk_head ** -0.5
    tm = _tile(t, ROW_TILE, 8)
    tmw = _tile(t, ROW_TILE // 2, 8)

    sp = jnp.stack([my_chip, my_c]).astype(jnp.int32)

    def pair_sums(names, gl, from_sib):
        res = []
        for n, g, fs in zip(names, gl, from_sib):
            _, r2, cs = fs.shape
            tr = _tile(r2, ROW_TILE, 16)
            nb = r2 // tr
            res.append(_blocked(
                lambda a, b: a + b, (nchip, nb),
                [(g, (None, tr, cs), lambda k, i, s, nb=nb: (k, s[1] * nb + i, 0)),
                 (fs, (None, tr, cs), lambda k, i, s: (k, i, 0))],
                [((nchip, r2, cs), bf16, (None, tr, cs), lambda k, i, s: (k, i, 0))], [], "pair_sum_" + n, sp=sp)[0])
        return res

    def chip_sums(names, gl, from_sib, from_chips):
        res = []
        for n, g, fs, fc in zip(names, gl, from_sib, from_chips):
            _, r2, cs = fs.shape
            tr = _tile(r2, ROW_TILE, 16)
            nb = r2 // tr
            res.append(_blocked(
                lambda a, b, r1, r2_, r3: ((a + b) + r1.astype(f32)) + r2_.astype(f32) + r3.astype(f32), (nb,),
                [(g, (None, tr, cs), lambda i, s, nb=nb: (s[0], s[1] * nb + i, 0)),
                 (fs, (None, tr, cs), lambda i, s: (s[0], i, 0)),
                 (fc, (None, tr, cs), lambda i, s: (s[0] ^ 1, i, 0)),
                 (fc, (None, tr, cs), lambda i, s: (s[0] ^ 2, i, 0)),
                 (fc, (None, tr, cs), lambda i, s: (s[0] ^ 3, i, 0))],
                [((r2, cs), f32, (tr, cs), lambda i, s: (i, 0))], [], "chip_sum_" + n, sp=sp)[0])
        return res

    ck, ccs = wts["conv_w"].shape
    w_in_g, conv_g = _comm_call(_gather_weights([wts["w_in"].astype(bf16)], [wts["conv_w"]]), "gather_w_in")
    cat_cols = lambda g: jnp.concatenate([g[k] for k in range(nchip)], axis=1)
    conv_full = cat_cols(conv_g)
    mixer_gather = _gather_weights([wts[n].astype(bf16) for n in MIXER[1:]], [])
    ffn_gather = _gather_weights([wts[n].astype(bf16) for n in FFN], [])

    wi = cat_cols(w_in_g)
    o = np.cumsum((0,) + in_sizes)
    seg = lambda i: wi[:, o[i]:o[i + 1]]
    w_in_p = jnp.concatenate([seg(4), seg(0), seg(1), seg(3), seg(2), seg(5),
                              jnp.zeros((d, d_in_p - off_tail - ROPE - hs), bf16)], axis=1)

    inv_freq = ROPE_THETA ** (-jnp.arange(0, ROPE, 2, dtype=f32) / ROPE)
    ang = positions[0].astype(f32)[:, None] * inv_freq
    cos, sin = jnp.cos(ang), jnp.sin(ang)
    z32, z64, z96 = jnp.zeros((t, 32), f32), jnp.zeros((t, 64), f32), jnp.zeros((t, 96), f32)
    cosp = jnp.concatenate([cos, cos, z64], axis=1)
    sina = jnp.concatenate([-sin, z96], axis=1)
    sinb = jnp.concatenate([z32, sin, z64], axis=1)

    row = lambda a: a.reshape(1, -1)
    w_pre_mix, w_post_mix = row(wts["pre_mix_norm_w"]), row(wts["post_mix_norm_w"])
    w_pre_ffn, w_post_ffn = row(wts["pre_ffn_norm_w"]), row(wts["post_ffn_norm_w"])
    w_qn, w_kvn = row(wts["q_norm_w"]), row(wts["kv_norm_w"])
    w_attn_n, w_ssd_n = row(wts["attn_out_norm_w"]), row(wts["ssd_norm_w"])
    conv_b_r = row(wts["conv_b"])
    dtb, alog, dskip = row(wts["dt_bias"]), row(wts["a_log"]), row(wts["d_skip"])
    dtb_t, alog_t = dtb.reshape(hs, 1), alog.reshape(hs, 1)

    u = _rowwise(lambda a, w: _rms(a, w), t, tm, [_rows(xs, tm), _par(w_pre_mix)], [_rout(t, d, bf16, tm)], [],
                 "pre_mix_norm")[0]
    proj, (w_uq_g, w_ukv_f, w_out_g) = _matmul([(u, w_in_p)], "nn", f32, "in_proj", comm=mixer_gather,
                                               at=(0.0, 0.6))
    w_uq_p = jnp.pad(cat_cols(w_uq_g).reshape(Q_RANK, nh, qk_head), ((0, 0), (0, 0), (0, QK_PAD - qk_head))
                     ).reshape(Q_RANK, nh * QK_PAD)
    w_out_f = w_out_g.reshape(-1, w_out_g.shape[2])
    cq_in = _rows(proj, tm, Q_RANK, off_cq // Q_RANK)
    ckv_in = _rows(proj, tm, KV_RANK, off_ckv // KV_RANK)
    cqn = _rowwise(lambda a, w: _rms(a, w), t, tm, [cq_in, _par(w_qn)], [_rout(t, Q_RANK, bf16, tm)], [], "q_norm")[0]
    ckvn = _rowwise(lambda a, w: _rms(a, w), t, tm, [ckv_in, _par(w_kvn)], [_rout(t, KV_RANK, bf16, tm)], [],
                    "kv_norm")[0]
    q_raw = _matmul([(cqn, w_uq_p)], "nn", f32, "q_up")
    kv = _matmul([(ckvn, w_ukv_f)], "nn", bf16, "kv_up")

    def q_rope_fn(qt, cp, sa, sb):
        parts = []
        for h in range(nh):
            parts.append(qt[:, h * QK_PAD: h * QK_PAD + NOPE])
            parts.append(_rope(qt[:, h * QK_PAD + NOPE:(h + 1) * QK_PAD], cp, sa, sb))
        return jnp.concatenate(parts, axis=1)

    tail_cb = off_tail // LANES
    q2 = _rowwise(q_rope_fn, t, tm, [_rows(q_raw, tm), _rows(cosp, tm), _rows(sina, tm), _rows(sinb, tm)],
                  [_rout(t, nh * QK_PAD, bf16, tm)], [], "q_rope")[0]
    kr2 = _rowwise(_rope, t, tm, [_rows(proj, tm, LANES, tail_cb), _rows(cosp, tm), _rows(sina, tm), _rows(sinb, tm)],
                   [_rout(t, LANES, bf16, tm)], [], "k_rope")[0]
    o_att, lse, ffn_w = _flash_fwd(q2, kv, kr2, nh, scale, ffn_gather, (0.0, 0.75))
    w_gate_f, w_up_f = ffn_w[0], ffn_w[1]
    w_down_f = ffn_w[2].reshape(-1, ffn_w[2].shape[2])

    xbc_act = _conv_fwd(proj, conv_full, conv_b_r, cdim, tm)
    dt_raw = lax.optimization_barrier(proj[:, off_tail + ROPE: off_tail + ROPE + hs])
    dt_raw_t = dt_raw.T
    y_ssd, hprev = _ssd_fwd(xbc_act, dt_raw, dt_raw_t, dtb, dtb_t, alog, alog_t, dskip, width)
    z_ins = [_rows(proj, tm, gw, off_z // gw + i) for i in range(SSD_G)]

    def mix_norms_fn(ov, yv, *rest):
        zs, wa, ws = rest[:SSD_G], rest[SSD_G], rest[SSD_G + 1]
        outs = [_rms(ov, wa)]
        for i in range(SSD_G):
            sl = slice(i * gw, (i + 1) * gw)
            outs.append(_rms(yv[:, sl] * (zs[i] * _sigmoid(zs[i])), ws[:, sl]))
        return jnp.concatenate(outs, axis=1)

    cat = _rowwise(mix_norms_fn, t, tm, [_rows(o_att, tm), _rows(y_ssd, tm)] + z_ins + [_par(w_attn_n), _par(w_ssd_n)],
                   [_rout(t, d, bf16, tm)], [], "attn_ssd_out_norms")[0]
    mix = _matmul([(cat, w_out_f)], "nn", f32, "out_proj")

    def post_mix_fn(mx, xv, w1, w2):
        h1v = xv + _rms(mx, w1)
        return h1v, _rms(h1v, w2)

    h1, v_in = _rowwise(post_mix_fn, t, tmw, [_rows(mix, tmw), _rows(xs, tmw), _par(w_post_mix), _par(w_pre_ffn)],
                        [_rout(t, d, f32, tmw), _rout(t, d, bf16, tmw)], [], "post_mix_pre_ffn_norm")
    g_ff, u_ff, act = _ffn_up(v_in, w_gate_f, w_up_f)
    ffn = _matmul([(act, w_down_f)], "nn", f32, "ffn_down")

    def final_fn(fv, h1v, tg, w):
        h2 = h1v + _rms(fv, w)
        err = h2 - tg
        lpart = 0.5 * jnp.sum(jnp.sum(err * err, axis=1, keepdims=True), axis=0, keepdims=True) / d
        dh2 = err / d
        dff, dw = _rms_bwd(fv, w, dh2)
        return dff, dh2, jnp.broadcast_to(lpart, (1, LANES)), dw

    dffn, dh2, loss_acc, g_post_ffn = _rowwise(
        final_fn, t, tmw, [_rows(ffn, tmw), _rows(h1, tmw), _rows(tgt, tmw), _par(w_post_ffn)],
        [_rout(t, d, bf16, tmw), _rout(t, d, f32, tmw)], [(1, LANES), (1, d)], "loss_post_ffn_norm_bwd")
    loss = lax.psum(loss_acc[0, 0], ("x", "y", "c"))

    dg_ff, du_ff = _ffn_down_bwd(dffn, w_down_f, g_ff, u_ff)
    gw_down = _matmul([(act, dffn)], "tn", f32, "grad_w_down")
    gw_gate = _matmul([(v_in, dg_ff)], "tn", f32, "grad_w_gate", out_chunks=nchip)
    gw_up = _matmul([(v_in, du_ff)], "tn", f32, "grad_w_up", out_chunks=nchip)
    ffn_g = [gw_gate, gw_up, gw_down.reshape(nchip, -1, d)]
    dv_in, ffn_from_sib = _matmul([(dg_ff, w_gate_f), (du_ff, w_up_f)], "nt", f32, "ffn_up_bwd",
                                  comm=_sibling_send_halves(ffn_g), at=(0.0,))
    ffn_pairs = pair_sums(FFN, ffn_g, ffn_from_sib)

    def mid_bwd_fn(h1v, dvv, dh2v, mx, w_pf, w_pm):
        dxn, dw_pf = _rms_bwd(h1v, w_pf, dvv)
        dh1v = dh2v + dxn
        dmx, dw_pm = _rms_bwd(mx, w_pm, dh1v)
        return dh1v, dmx, dw_pf, dw_pm

    dh1, dmix, g_pre_ffn, g_post_mix = _rowwise(
        mid_bwd_fn, t, tmw, [_rows(h1, tmw), _rows(dv_in, tmw), _rows(dh2, tmw), _rows(mix, tmw),
                             _par(w_pre_ffn), _par(w_post_mix)],
        [_rout(t, d, f32, tmw), _rout(t, d, bf16, tmw)], [(1, d), (1, d)], "pre_ffn_post_mix_norm_bwd")
    dcat = _matmul([(dmix, w_out_f)], "nt", f32, "out_proj_bwd")
    gw_out = _matmul([(cat, dmix)], "tn", f32, "grad_w_out")

    def mix_norms_bwd_fn(ov, yv, *rest):
        zs, dcv, wa, ws = rest[:SSD_G], rest[SSD_G], rest[SSD_G + 1], rest[SSD_G + 2]
        dov, dwa = _rms_bwd(ov, wa, dcv[:, :mla_w])
        dl = [jnp.broadcast_to(jnp.sum(dov[:, h * V_HEAD:(h + 1) * V_HEAD] * ov[:, h * V_HEAD:(h + 1) * V_HEAD],
                                       axis=1, keepdims=True), (ov.shape[0], V_HEAD)) for h in range(nh)]
        dys, dzs, dws = [], [], []
        for i in range(SSD_G):
            sl = slice(i * gw, (i + 1) * gw)
            zv, yi = zs[i], yv[:, sl]
            sg = _sigmoid(zv)
            sz = zv * sg
            dgi, dwi = _rms_bwd(yi * sz, ws[:, sl], dcv[:, mla_w + i * gw: mla_w + (i + 1) * gw])
            dys.append(dgi * sz)
            dzs.append(dgi * yi * (sg * (1.0 + zv * (1.0 - sg))))
            dws.append(dwi)
        cc = lambda a: jnp.concatenate(a, axis=1)
        return dov, cc(dl), cc(dys), cc(dzs), dwa, cc(dws)

    do_att, delta, dy_ssd, dz, g_attn_n, g_ssd_n = _rowwise(
        mix_norms_bwd_fn, t, tm,
        [_rows(o_att, tm), _rows(y_ssd, tm)] + z_ins + [_rows(dcat, tm), _par(w_attn_n), _par(w_ssd_n)],
        [_rout(t, mla_w, bf16, tm), _rout(t, mla_w, f32, tm), _rout(t, width, f32, tm), _rout(t, width, bf16, tm)],
        [(1, mla_w), (1, width)], "attn_ssd_out_norms_bwd")
    dkv, dkr_h, dq2, ffn_from_chips = _flash_bwd(q2, kv, kr2, do_att, lse, delta, nh, scale,
                                                 _chips_exchange(ffn_pairs), (0.0,))
    ffn_halves = chip_sums(FFN, ffn_g, ffn_from_sib, ffn_from_chips)

    def q_rope_bwd_fn(dqt, cp, sa, sb):
        parts = []
        for h in range(nh):
            parts.append(dqt[:, h * QK_PAD: h * QK_PAD + NOPE])
            parts.append(_rope_bwd(dqt[:, h * QK_PAD + NOPE:(h + 1) * QK_PAD], cp, sa, sb))
        return jnp.concatenate(parts, axis=1)

    dq_raw = _rowwise(q_rope_bwd_fn, t, tm, [_rows(dq2, tm), _rows(cosp, tm), _rows(sina, tm), _rows(sinb, tm)],
                      [_rout(t, nh * QK_PAD, bf16, tm)], [], "q_rope_bwd")[0]

    def k_rope_bwd_fn(dk, cp, sa, sb):
        tot = dk[:, 0:LANES]
        for h in range(1, nh):
            tot = tot + dk[:, h * LANES:(h + 1) * LANES]
        return _rope_bwd(tot, cp, sa, sb)

    dkr = _rowwise(k_rope_bwd_fn, t, tm, [_rows(dkr_h, tm), _rows(cosp, tm), _rows(sina, tm), _rows(sinb, tm)],
                   [_rout(t, LANES, f32, tm)], [], "k_rope_bwd")[0]
    gw_uq_p = _matmul([(cqn, dq_raw)], "tn", f32, "grad_w_uq")
    gw_ukv = _matmul([(ckvn, dkv)], "tn", f32, "grad_w_ukv", out_chunks=nchip)
    dcqn = _matmul([(dq_raw, w_uq_p)], "nt", f32, "q_up_bwd")
    dckvn = _matmul([(dkv, w_ukv_f)], "nt", f32, "kv_up_bwd")

    def lat_norm_bwd_fn(a, w, dyv):
        return _rms_bwd(a, w, dyv)

    dcq, g_qn = _rowwise(lat_norm_bwd_fn, t, tm, [cq_in, _par(w_qn), _rows(dcqn, tm)],
                         [_rout(t, Q_RANK, bf16, tm)], [(1, Q_RANK)], "q_norm_bwd")
    dckv, g_kvn = _rowwise(lat_norm_bwd_fn, t, tm, [ckv_in, _par(w_kvn), _rows(dckvn, tm)],
                           [_rout(t, KV_RANK, bf16, tm)], [(1, KV_RANK)], "kv_norm_bwd")

    dxbc_act, ddt_raw, g_alog, g_dskip, g_dtb, ffn_sib_halves = _ssd_bwd(
        xbc_act, dt_raw, dt_raw_t, dtb, dtb_t, alog, alog_t, dskip, hprev, dy_ssd, width,
        _sibling_swap(ffn_halves), (0.0,))
    dpre, gcw0, gcw1, gcw2, gcw3, g_conv_b = _conv_bwd_pre(proj, conv_full, conv_b_r, dxbc_act, cdim, tm)
    g_conv_w = jnp.concatenate([gcw0, gcw1, gcw2, gcw3], axis=0)
    dxbc = _conv_bwd_dx(dpre, conv_full, tm)

    dtail = jnp.concatenate([dkr[:, :ROPE], ddt_raw, jnp.zeros((t, d_in_p - off_tail - ROPE - hs), f32)],
                            axis=1).astype(bf16)
    dproj = jnp.concatenate([dxbc, dcq, dckv, dz, dtail], axis=1)
    split_cols = lambda gf: jnp.stack(jnp.split(gf, nchip, axis=1))
    split_rows = lambda gf: gf.reshape(nchip, gf.shape[0] // nchip, gf.shape[1])
    gw_uq = gw_uq_p.reshape(Q_RANK, nh, QK_PAD)[:, :, :qk_head].reshape(Q_RANK, nh * qk_head)
    rest_g = [split_cols(gw_uq), gw_ukv, split_rows(gw_out)]
    du_in, rest_from_sib = _matmul([(dproj, w_in_p)], "nt", f32, "in_proj_bwd",
                                   comm=_sibling_send_halves(rest_g), at=(0.0,))
    rest_pairs = pair_sums(MIXER[1:], rest_g, rest_from_sib)
    gw_in_p, rest_from_chips = _matmul([(u, dproj)], "tn", f32, "grad_w_in",
                                       comm=_chips_exchange(rest_pairs), at=(0.0,))
    rest_halves = chip_sums(MIXER[1:], rest_g, rest_from_sib, rest_from_chips)

    def first_bwd_fn(xv, duv, dh1v, w):
        dxn, dw = _rms_bwd(xv, w, duv)
        return dh1v + dxn, dw

    grad_x, g_pre_mix = _rowwise(first_bwd_fn, t, tmw, [_rows(xs, tmw), _rows(du_in, tmw), _rows(dh1, tmw),
                                                         _par(w_pre_mix)],
                                 [_rout(t, d, f32, tmw)], [(1, d)], "pre_mix_norm_bwd")

    gseg = lambda a, b: gw_in_p[:, a:b]
    gw_in = jnp.concatenate([gseg(off_cq, off_ckv), gseg(off_ckv, off_z), gseg(off_tail, off_tail + ROPE),
                             gseg(off_z, off_tail), gseg(off_xbc, off_cq),
                             gseg(off_tail + ROPE, off_tail + ROPE + hs)], axis=1)
    in_g = [split_cols(gw_in)]
    in_from_sib = _comm_call(_sibling_send_halves(in_g), "sibling_send_halves")
    in_pairs = pair_sums(MIXER[:1], in_g, in_from_sib)
    in_from_chips = _comm_call(_chips_exchange(in_pairs), "chips_exchange")
    mix_halves = chip_sums(MIXER[:1], in_g, in_from_sib, in_from_chips) + rest_halves
    mix_sib_halves = _comm_call(_sibling_swap(mix_halves), "sibling_swap")
    halves = mix_halves + ffn_halves
    sib_halves = mix_sib_halves + ffn_sib_halves
    gshard = {}

    gsmall = {"q_norm_w": g_qn, "kv_norm_w": g_kvn, "conv_w": g_conv_w, "conv_b": g_conv_b, "dt_bias": g_dtb,
              "a_log": g_alog, "d_skip": g_dskip, "ssd_norm_w": g_ssd_n, "attn_out_norm_w": g_attn_n,
              "pre_mix_norm_w": g_pre_mix, "post_mix_norm_w": g_post_mix, "pre_ffn_norm_w": g_pre_ffn,
              "post_ffn_norm_w": g_post_ffn}
    small_sizes = [int(np.prod(gsmall[n].shape)) for n in SMALL]
    srows = _round_up(-(-sum(small_sizes) // LANES), 8)
    spart = _flat_pad([gsmall[n] for n in SMALL], srows * LANES).reshape(srows, LANES)
    sall = _gather_all(spart)

    def sum8_fn(a):
        tot = a[0]
        for k in range(1, 8):
            tot = tot + a[k]
        return tot

    ssum = _blocked(sum8_fn, (1,), [(sall, sall.shape, lambda i: (0, 0, 0))],
                    [((srows, LANES), f32, (srows, LANES), lambda i: (0, 0))], [], "small_grad_sum")[0].reshape(-1)
    gred = {}
    off = 0
    for n, sz in zip(SMALL, small_sizes):
        gred[n] = ssum[off:off + sz].reshape(gsmall[n].shape)
        off += sz
    gshard["conv_w"] = lax.dynamic_slice_in_dim(gred["conv_w"], my_chip * ccs, ccs, axis=1)
    for n in SMALL:
        if n != "conv_w":
            gshard[n] = gred[n].reshape(wts[n].shape)

    delta, new_m, new_v = {}, {}, {}
    big_out = {}
    for n, mine_h, sib_h in zip(BIG, halves, sib_halves):
        big_out[n] = _adamw_halves(local[n], local["m_" + n], local["v_" + n], mine_h, sib_h, sp, "adamw_" + n)
    pack = lambda src: _flat_pad([src[n] for n in SMALL], srows * LANES).reshape(srows, LANES)
    sd, sm, sv = _adamw_call(pack(wts), pack(gshard), pack(mom_m), pack(mom_v), "adamw_small")
    off = 0
    for n in SMALL:
        sz = int(np.prod(wts[n].shape))
        for dst, src in ((delta, sd), (new_m, sm), (new_v, sv)):
            dst[n] = src.reshape(-1)[off:off + sz].reshape(wts[n].shape)
        off += sz

    small_out = (gshard, delta, new_m, new_v)
    pick = lambda k: [big_out[n][k] if n in big_out else small_out[k][n][None] for n in ORDER]
    return (loss, grad_x[None], *pick(0), *pick(1), *pick(2), *pick(3))
```

```python
import functools

import numpy as np
import jax
import jax.numpy as jnp
from jax import lax
from jax.experimental import pallas as pl
from jax.experimental.pallas import tpu as pltpu

f32, bf16 = jnp.float32, jnp.bfloat16

EPS = 1e-6
V_HEAD = 128
NOPE = 128
ROPE = 64
QK_PAD = 256
Q_RANK = 512
KV_RANK = 512
ROPE_THETA = 10000.0
SSD_P = 64
SSD_G = 2
SSD_N = 128
SSD_K = 4
CHUNK = 128
ADAM_LR, ADAM_B1, ADAM_B2, ADAM_EPS, ADAM_WD, ADAM_STEP = 0.001, 0.9, 0.999, 1e-08, 0.01, 10

VMEM_LIMIT_BYTES = 48 * 1024 * 1024
LANES = 128
ATT_TILE = 512
ATT_TILE_FWD = 1024
MM_TM, MM_TN, MM_TK = 1408, 1024, 1408
CHUNK_WHOLE_MAX = 1536
ROW_TILE = 256

NN = (((1,), (0,)), ((), ()))
NT = (((1,), (1,)), ((), ()))
TN = (((0,), (0,)), ((), ()))
MESH = pl.DeviceIdType.MESH
ANY = pl.BlockSpec(memory_space=pl.ANY)


def _tile(dim, cap, align=LANES):
    if dim <= cap:
        return dim
    t = (cap // align) * align
    while t >= align:
        if dim % t == 0:
            return t
        t -= align
    raise ValueError(f"no tile for {dim} under {cap}")


def _round_up(n, m):
    return -(-n // m) * m


def _params(sem):
    return pltpu.CompilerParams(dimension_semantics=sem, vmem_limit_bytes=VMEM_LIMIT_BYTES)


def _dot(a, b, dims):
    return lax.dot_general(a.astype(bf16), b.astype(bf16), dims, preferred_element_type=f32)


def _call(body, name, out_shape, grid, in_specs, out_specs, scratch_shapes, sem, args, comm=None, at=None,
          prefetch=()):
    npf = len(prefetch)
    if comm is None:
        gs = pltpu.PrefetchScalarGridSpec(num_scalar_prefetch=npf, grid=grid, in_specs=list(in_specs),
                                          out_specs=list(out_specs), scratch_shapes=list(scratch_shapes))
        res = pl.pallas_call(body, name=name, out_shape=list(out_shape), grid_spec=gs,
                             compiler_params=_params(sem))(*prefetch, *args)
        return list(res), []
    n_in, n_out, n_sc = len(args), len(out_shape), len(scratch_shapes)
    na, no = len(comm.args), len(comm.outs)
    steps = int(np.prod(grid))

    def full(*allrefs):
        pf, refs = allrefs[:npf], allrefs[npf:]
        cin = refs[n_in:n_in + na]
        o0 = n_in + na
        cout = refs[o0 + n_out:o0 + n_out + no]
        s0 = o0 + n_out + no
        send_sems, recv_sems = refs[s0 + n_sc], refs[s0 + n_sc + 1]
        lin = pl.program_id(0)
        for dim in range(1, len(grid)):
            lin = lin * grid[dim] + pl.program_id(dim)
        for p in range(comm.nphase - 1):
            @pl.when(lin == int(round(at[p] * (steps - 1))))
            def _(p=p):
                comm.run(p, cin, cout, send_sems, recv_sems)
        body(*pf, *refs[:n_in], *refs[o0:o0 + n_out], *refs[s0:s0 + n_sc])

        @pl.when(lin == steps - 1)
        def _():
            comm.run(comm.nphase - 1, cin, cout, send_sems, recv_sems)

    gs = pltpu.PrefetchScalarGridSpec(
        num_scalar_prefetch=npf, grid=grid, in_specs=list(in_specs) + [ANY] * na,
        out_specs=list(out_specs) + [ANY] * no, scratch_shapes=list(scratch_shapes) + comm.sems())
    res = pl.pallas_call(full, name=name, out_shape=list(out_shape) + comm.outs, grid_spec=gs,
                         compiler_params=_params(("arbitrary",) * len(grid)))(*prefetch, *args, *comm.args)
    return list(res[:n_out]), list(res[n_out:])


def _chunk_tile(cs, cap):
    return cs if cs <= CHUNK_WHOLE_MAX else _tile(cs, cap)


def _matmul(pairs, mode, out_dtype, name, out_chunks=None, comm=None, at=None):
    a0, b0 = pairs[0]
    chunked = b0.ndim == 3
    cs = b0.shape[2] if chunked else None
    bcols = b0.shape[0] * b0.shape[2] if chunked else b0.shape[1]
    brows = b0.shape[1] if chunked else b0.shape[0]
    if mode == "nn":
        (m, k), n = a0.shape, bcols
    elif mode == "nt":
        (m, k), n = a0.shape, brows
    else:
        (k, m), n = a0.shape, bcols
    tm = _tile(m, MM_TM)
    if mode == "nt":
        tn = _tile(n, MM_TN)
        tk = _chunk_tile(cs, MM_TK) if chunked else _tile(k, MM_TK)
    else:
        tk = _tile(k, MM_TK)
        if chunked:
            tn = _chunk_tile(cs, MM_TN)
        elif out_chunks:
            tn = _chunk_tile(n // out_chunks, MM_TN)
        else:
            tn = _tile(n, MM_TN)
    nk = k // tk
    if mode == "nn":
        a_spec = pl.BlockSpec((tm, tk), lambda i, j, kk: (i, kk))
        if chunked:
            q = cs // tn
            b_spec = pl.BlockSpec((None, tk, tn), lambda i, j, kk: (j // q, kk, j % q))
        else:
            b_spec = pl.BlockSpec((tk, tn), lambda i, j, kk: (kk, j))
        dims = NN
    elif mode == "nt":
        a_spec = pl.BlockSpec((tm, tk), lambda i, j, kk: (i, kk))
        if chunked:
            q = cs // tk
            b_spec = pl.BlockSpec((None, tn, tk), lambda i, j, kk: (kk // q, j, kk % q))
        else:
            b_spec = pl.BlockSpec((tn, tk), lambda i, j, kk: (j, kk))
        dims = NT
    else:
        a_spec = pl.BlockSpec((tk, tm), lambda i, j, kk: (kk, i))
        b_spec = pl.BlockSpec((tk, tn), lambda i, j, kk: (kk, j))
        dims = TN
    if out_chunks:
        qo = (n // out_chunks) // tn
        out_shape = jax.ShapeDtypeStruct((out_chunks, m, n // out_chunks), out_dtype)
        o_spec = pl.BlockSpec((None, tm, tn), lambda i, j, kk: (j // qo, i, j % qo))
    else:
        out_shape = jax.ShapeDtypeStruct((m, n), out_dtype)
        o_spec = pl.BlockSpec((tm, tn), lambda i, j, kk: (i, j))
    npair = len(pairs)

    def body(*refs):
        o_ref, acc = refs[2 * npair], refs[2 * npair + 1]
        kk = pl.program_id(2)

        @pl.when(kk == 0)
        def _():
            acc[...] = jnp.zeros_like(acc)

        part = _dot(refs[0][...], refs[1][...], dims)
        for p in range(1, npair):
            part = part + _dot(refs[2 * p][...], refs[2 * p + 1][...], dims)
        acc[...] += part

        @pl.when(kk == nk - 1)
        def _():
            o_ref[...] = acc[...].astype(out_dtype)

    args = [t for pr in pairs for t in pr]
    res, cres = _call(body, name, [out_shape], (m // tm, n // tn, nk), [a_spec, b_spec] * npair, [o_spec],
                      [pltpu.VMEM((tm, tn), f32)], ("parallel", "parallel", "arbitrary"), args, comm, at)
    return res[0] if comm is None else (res[0], cres)


def _sigmoid(x):
    return 1.0 / (1.0 + jnp.exp(-x))


def _ffn_up(v, wg, wu):
    m, k = v.shape
    nchunk, _, cs = wg.shape
    n = nchunk * cs
    tm, tn = _tile(m, 512), _chunk_tile(cs, 512)
    q = cs // tn
    w_spec = pl.BlockSpec((None, k, tn), lambda j, i: (j // q, 0, j % q))

    def body(v_ref, wg_ref, wu_ref, g_ref, u_ref, act_ref):
        vb = v_ref[...]
        g = _dot(vb, wg_ref[...], NN)
        u = _dot(vb, wu_ref[...], NN)
        g_ref[...] = g.astype(bf16)
        u_ref[...] = u.astype(bf16)
        act_ref[...] = (g * _sigmoid(g) * u).astype(bf16)

    out = jax.ShapeDtypeStruct((m, n), bf16)
    o_spec = pl.BlockSpec((tm, tn), lambda j, i: (i, j))
    return pl.pallas_call(
        body, name="ffn_up", out_shape=(out, out, out), grid=(n // tn, m // tm),
        in_specs=[pl.BlockSpec((tm, k), lambda j, i: (i, 0)), w_spec, w_spec],
        out_specs=(o_spec, o_spec, o_spec),
        compiler_params=_params(("parallel", "parallel")),
    )(v, wg, wu)


def _ffn_down_bwd(dffn, wd, g, u):
    m, k = dffn.shape
    n = wd.shape[0]
    tm, tn, tk = _tile(m, MM_TM), _tile(n, 512), _tile(k, MM_TK)
    nk = k // tk

    def body(d_ref, w_ref, g_ref, u_ref, dg_ref, du_ref, acc):
        kk = pl.program_id(2)

        @pl.when(kk == 0)
        def _():
            acc[...] = jnp.zeros_like(acc)

        acc[...] += _dot(d_ref[...], w_ref[...], NT)

        @pl.when(kk == nk - 1)
        def _():
            dact = acc[...]
            gg, uu = g_ref[...].astype(f32), u_ref[...].astype(f32)
            sg = _sigmoid(gg)
            du_ref[...] = (dact * gg * sg).astype(bf16)
            dg_ref[...] = (dact * uu * (sg * (1.0 + gg * (1.0 - sg)))).astype(bf16)

    out = jax.ShapeDtypeStruct((m, n), bf16)
    o_spec = pl.BlockSpec((tm, tn), lambda i, j, kk: (i, j))
    return pl.pallas_call(
        body, name="ffn_down_bwd", out_shape=(out, out), grid=(m // tm, n // tn, nk),
        in_specs=[pl.BlockSpec((tm, tk), lambda i, j, kk: (i, kk)),
                  pl.BlockSpec((tn, tk), lambda i, j, kk: (j, kk)), o_spec, o_spec],
        out_specs=(o_spec, o_spec),
        scratch_shapes=[pltpu.VMEM((tm, tn), f32)],
        compiler_params=_params(("parallel", "parallel", "arbitrary")),
    )(dffn, wd, g, u)


def _blocked(fn, grid, ins, outs, accs, name, sp=None):
    n_in, n_out, n_acc = len(ins), len(outs), len(accs)
    nsp = 0 if sp is None else 1

    def body(*refs):
        refs = refs[nsp:]
        tiles = [r[...] for r in refs[:n_in]]
        res = fn(*tiles)
        if not isinstance(res, (tuple, list)):
            res = (res,)
        for r, val in zip(refs[n_in:n_in + n_out], res[:n_out]):
            r[...] = val.astype(r.dtype)
        if n_acc:
            first = pl.program_id(0) == 0
            for d in range(1, len(grid)):
                first = jnp.logical_and(first, pl.program_id(d) == 0)

            @pl.when(first)
            def _():
                for r in refs[n_in + n_out:]:
                    r[...] = jnp.zeros_like(r)

            for r, val in zip(refs[n_in + n_out:], res[n_out:]):
                r[...] += val

    def acc_map(shape):
        zeros = (0,) * len(shape)
        return lambda *idx: zeros

    in_specs = [pl.BlockSpec(bs, im) for _, bs, im in ins]
    out_specs = [pl.BlockSpec(bs, im) for _, _, bs, im in outs] + [pl.BlockSpec(s, acc_map(s)) for s in accs]
    out_shape = [jax.ShapeDtypeStruct(s, d) for s, d, _, _ in outs] + [jax.ShapeDtypeStruct(s, f32) for s in accs]
    sem = ("arbitrary",) * len(grid) if n_acc else ("parallel",) * len(grid)
    args = [a for a, _, _ in ins]
    if sp is None:
        res = pl.pallas_call(body, name=name, out_shape=out_shape, grid=grid, in_specs=in_specs,
                             out_specs=out_specs, compiler_params=_params(sem))(*args)
    else:
        gs = pltpu.PrefetchScalarGridSpec(num_scalar_prefetch=1, grid=grid, in_specs=in_specs, out_specs=out_specs)
        res = pl.pallas_call(body, name=name, out_shape=out_shape, grid_spec=gs,
                             compiler_params=_params(sem))(sp, *args)
    return res


def _rows(a, tm, cols=None, cb=0):
    w = a.shape[1] if cols is None else cols
    return (a, (tm, w), lambda i: (i, cb))


def _par(a):
    zeros = (0,) * a.ndim
    return (a, a.shape, lambda i: zeros)


def _rout(t, w, dtype, tm):
    return ((t, w), dtype, (tm, w), lambda i: (i, 0))


def _rowwise(fn, t, tm, ins, outs, accs, name):
    return _blocked(fn, (t // tm,), ins, outs, accs, name)


def _rms(x, w):
    r = lax.rsqrt(jnp.mean(x * x, axis=-1, keepdims=True) + EPS)
    return x * r * w


def _rms_bwd(x, w, dy):
    r = lax.rsqrt(jnp.mean(x * x, axis=-1, keepdims=True) + EPS)
    xh = x * r
    dyw = dy * w
    dx = r * (dyw - xh * jnp.mean(dyw * xh, axis=-1, keepdims=True))
    return dx, jnp.sum(dy * xh, axis=0, keepdims=True)


def _silu_grad(x):
    s = _sigmoid(x)
    return s * (1.0 + x * (1.0 - s))


def _rope(blk, cosp, sina, sinb):
    return blk * cosp + pltpu.roll(blk, 96, 1) * sina + pltpu.roll(blk, 32, 1) * sinb


def _rope_bwd(dy, cosp, sina, sinb):
    return dy * cosp + pltpu.roll(dy * sina, 32, 1) + pltpu.roll(dy * sinb, 96, 1)


HALO = 8


def _conv_taps(buf, w, tm, base):
    acc = buf[base:base + tm, :] * w[0:1]
    for k in range(1, SSD_K):
        acc = acc + buf[base + k:base + k + tm, :] * w[k:k + 1]
    return acc


def _conv_specs(t, tm, cdim):
    cur = pl.BlockSpec((tm, cdim), lambda i: (i, 0))
    prev = pl.BlockSpec((HALO, cdim), lambda i: (jnp.maximum(i * (tm // HALO) - 1, 0), 0))
    nxt = pl.BlockSpec((HALO, cdim), lambda i: (jnp.minimum((i + 1) * (tm // HALO), t // HALO - 1), 0))
    return cur, prev, nxt


def _conv_fwd(src, w, b, cdim, tm):
    t = src.shape[0]
    cur, prev, _ = _conv_specs(t, tm, cdim)

    def body(x_ref, p_ref, w_ref, b_ref, o_ref, buf):
        buf[0:HALO, :] = jnp.where(pl.program_id(0) > 0, p_ref[...], 0.0)
        buf[HALO:HALO + tm, :] = x_ref[...]
        pre = _conv_taps(buf, w_ref[...], tm, HALO - (SSD_K - 1)) + b_ref[...]
        o_ref[...] = pre * _sigmoid(pre)

    par = lambda a: pl.BlockSpec(a.shape, lambda i: (0, 0))
    return pl.pallas_call(
        body, name="conv_silu", out_shape=jax.ShapeDtypeStruct((t, cdim), f32), grid=(t // tm,),
        in_specs=[cur, prev, par(w), par(b)], out_specs=cur,
        scratch_shapes=[pltpu.VMEM((tm + HALO, cdim), f32)], compiler_params=_params(("parallel",)),
    )(src, src, w, b)


def _conv_bwd_pre(src, w, b, dact, cdim, tm):
    t = src.shape[0]
    cur, prev, _ = _conv_specs(t, tm, cdim)

    def body(x_ref, p_ref, w_ref, b_ref, d_ref, dpre_ref, dw0, dw1, dw2, dw3, db, buf):
        @pl.when(pl.program_id(0) == 0)
        def _():
            for r in (dw0, dw1, dw2, dw3, db):
                r[...] = jnp.zeros_like(r)

        buf[0:HALO, :] = jnp.where(pl.program_id(0) > 0, p_ref[...], 0.0)
        buf[HALO:HALO + tm, :] = x_ref[...]
        base = HALO - (SSD_K - 1)
        pre = _conv_taps(buf, w_ref[...], tm, base) + b_ref[...]
        dpre = d_ref[...] * _silu_grad(pre)
        dpre_ref[...] = dpre
        for k, r in enumerate((dw0, dw1, dw2, dw3)):
            r[...] += jnp.sum(dpre * buf[base + k:base + k + tm, :], axis=0, keepdims=True)
        db[...] += jnp.sum(dpre, axis=0, keepdims=True)

    par = lambda a: pl.BlockSpec(a.shape, lambda i: (0, 0))
    acc = pl.BlockSpec((1, cdim), lambda i: (0, 0))
    acc_shape = jax.ShapeDtypeStruct((1, cdim), f32)
    return pl.pallas_call(
        body, name="conv_silu_bwd", out_shape=[jax.ShapeDtypeStruct((t, cdim), f32)] + [acc_shape] * 5,
        grid=(t // tm,), in_specs=[cur, prev, par(w), par(b), cur], out_specs=[cur] + [acc] * 5,
        scratch_shapes=[pltpu.VMEM((tm + HALO, cdim), f32)], compiler_params=_params(("arbitrary",)),
    )(src, src, w, b, dact)


def _conv_bwd_dx(dpre, w, tm):
    t, cdim = dpre.shape
    cur, _, nxt = _conv_specs(t, tm, cdim)
    last = t // tm - 1

    def body(d_ref, n_ref, w_ref, o_ref, buf):
        buf[0:tm, :] = d_ref[...]
        buf[tm:tm + HALO, :] = jnp.where(pl.program_id(0) < last, n_ref[...], 0.0)
        wv = w_ref[...]
        acc = buf[0:tm, :] * wv[SSD_K - 1:SSD_K]
        for k in range(SSD_K - 1):
            s = SSD_K - 1 - k
            acc = acc + buf[s:s + tm, :] * wv[k:k + 1]
        o_ref[...] = acc.astype(bf16)

    return pl.pallas_call(
        body, name="conv_bwd_dx", out_shape=jax.ShapeDtypeStruct((t, cdim), bf16), grid=(t // tm,),
        in_specs=[cur, nxt, pl.BlockSpec(w.shape, lambda i: (0, 0))], out_specs=cur,
        scratch_shapes=[pltpu.VMEM((tm + HALO, cdim), f32)], compiler_params=_params(("parallel",)),
    )(dpre, dpre, w)


def _causal_mask(s):
    row = lax.broadcasted_iota(jnp.int32, s.shape, 0)
    col = lax.broadcasted_iota(jnp.int32, s.shape, 1)
    return jnp.where(row >= col, s, -jnp.inf)


def _causal_pairs(nq, q_major):
    pairs = [(qi, ki) for qi in range(nq) for ki in range(qi + 1)]
    if not q_major:
        pairs.sort(key=lambda p: (p[1], p[0]))
    return (jnp.asarray([p[0] for p in pairs], jnp.int32), jnp.asarray([p[1] for p in pairs], jnp.int32))


def _flash_fwd(q, kv, kr, nheads, scale, comm=None, at=None):
    t = q.shape[0]
    tq = _tile(t, ATT_TILE_FWD)
    nq = t // tq

    qtab, ktab = _causal_pairs(nq, q_major=True)
    hp = 2 if nheads % 2 == 0 else 1

    def body(qt, kt, q_ref, kv_ref, kr_ref, o_ref, lse_ref, m_sc, l_sc, acc_sc):
        qi, ki = qt[pl.program_id(1)], kt[pl.program_id(1)]

        @pl.when(ki == 0)
        def _():
            m_sc[...] = jnp.full_like(m_sc, -jnp.inf)
            l_sc[...] = jnp.zeros_like(l_sc)
            acc_sc[...] = jnp.zeros_like(acc_sc)

        def step(masked, last):
            krb = kr_ref[...]
            for j in range(hp):
                kn = kv_ref[:, j * QK_PAD: j * QK_PAD + NOPE]
                v = kv_ref[:, j * QK_PAD + NOPE:(j + 1) * QK_PAD]
                k = jnp.concatenate([kn, krb], axis=1)
                s = lax.dot_general(q_ref[:, j * QK_PAD:(j + 1) * QK_PAD], k, NT, preferred_element_type=f32) * scale
                if masked:
                    s = _causal_mask(s)
                m_old = m_sc[j]
                m_new = jnp.maximum(m_old, jnp.max(s, axis=1, keepdims=True))
                alpha = jnp.exp(m_old - m_new)
                p = jnp.exp(s - m_new)
                l = alpha * l_sc[j] + jnp.sum(p, axis=1, keepdims=True)
                acc = alpha * acc_sc[j] + lax.dot_general(p.astype(bf16), v, NN, preferred_element_type=f32)
                if last:
                    o_ref[:, j * V_HEAD:(j + 1) * V_HEAD] = acc / l
                    lse_ref[:, j * V_HEAD:(j + 1) * V_HEAD] = jnp.broadcast_to(m_new + jnp.log(l), (tq, V_HEAD))
                else:
                    l_sc[j] = l
                    acc_sc[j] = acc
                    m_sc[j] = m_new

        @pl.when(ki < qi)
        def _():
            step(False, False)

        @pl.when(ki == qi)
        def _():
            step(True, True)

    o_spec = pl.BlockSpec((tq, hp * V_HEAD), lambda h, s, qt, kt: (qt[s], h))
    out = jax.ShapeDtypeStruct((t, nheads * V_HEAD), f32)
    (o, lse), cres = _call(
        body, "flash_fwd", [out, out], (nheads // hp, len(qtab)),
        [pl.BlockSpec((tq, hp * QK_PAD), lambda h, s, qt, kt: (qt[s], h)),
         pl.BlockSpec((tq, hp * QK_PAD), lambda h, s, qt, kt: (kt[s], h)),
         pl.BlockSpec((tq, LANES), lambda h, s, qt, kt: (kt[s], 0))],
        [o_spec, o_spec],
        [pltpu.VMEM((hp, tq, 1), f32), pltpu.VMEM((hp, tq, 1), f32), pltpu.VMEM((hp, tq, V_HEAD), f32)],
        ("parallel", "arbitrary"), [q, kv, kr], comm, at, prefetch=(qtab, ktab))
    return o, lse, cres


def _flash_bwd(q, kv, kr, do, lse, delta, nheads, scale, comm=None, at=None):
    t = q.shape[0]
    tq = _tile(t, ATT_TILE)
    nq = t // tq
    qtab, ktab = _causal_pairs(nq, q_major=False)

    def body(qt, kt, q_ref, kn_ref, kr_ref, v_ref, do_ref, lse_ref, dl_ref, dkv_ref, dkr_ref, dq_ref, dk_sc, dv_sc):
        qi, ki = qt[pl.program_id(1)], kt[pl.program_id(1)]

        @pl.when(pl.program_id(1) == 0)
        def _():
            dq_ref[...] = jnp.zeros_like(dq_ref)

        @pl.when(qi == ki)
        def _():
            dk_sc[...] = jnp.zeros_like(dk_sc)
            dv_sc[...] = jnp.zeros_like(dv_sc)

        def step(masked):
            qb = q_ref[...]
            dob = do_ref[...]
            k = jnp.concatenate([kn_ref[...], kr_ref[...]], axis=1)
            s = lax.dot_general(qb, k, NT, preferred_element_type=f32) * scale
            if masked:
                s = _causal_mask(s)
            p = jnp.exp(s - lse_ref[:, 0:1])
            dv_sc[...] += lax.dot_general(p.astype(bf16), dob, TN, preferred_element_type=f32)
            dp = lax.dot_general(dob, v_ref[...], NT, preferred_element_type=f32)
            ds = (p * (dp - dl_ref[:, 0:1]) * scale).astype(bf16)
            dk_sc[...] += lax.dot_general(ds, qb, TN, preferred_element_type=f32)
            rows = pl.ds(pl.multiple_of(qi * tq, tq), tq)
            dq_ref[rows, :] += lax.dot_general(ds, k, NN, preferred_element_type=f32)

        @pl.when(qi > ki)
        def _():
            step(False)

        @pl.when(qi == ki)
        def _():
            step(True)

        @pl.when(qi == nq - 1)
        def _():
            dk = dk_sc[...]
            dkv_ref[...] = jnp.concatenate([dk[:, :NOPE], dv_sc[...]], axis=1).astype(bf16)
            dkr_ref[...] = dk[:, NOPE:]

    hspec = pl.BlockSpec((tq, V_HEAD), lambda h, s, qt, kt: (qt[s], h))
    (dkv, dkr, dq), cres = _call(
        body, "flash_bwd",
        [jax.ShapeDtypeStruct((t, nheads * QK_PAD), bf16), jax.ShapeDtypeStruct((t, nheads * LANES), f32),
         jax.ShapeDtypeStruct((t, nheads * QK_PAD), f32)],
        (nheads, len(qtab)),
        [pl.BlockSpec((tq, QK_PAD), lambda h, s, qt, kt: (qt[s], h)),
         pl.BlockSpec((tq, NOPE), lambda h, s, qt, kt: (kt[s], 2 * h)),
         pl.BlockSpec((tq, LANES), lambda h, s, qt, kt: (kt[s], 0)),
         pl.BlockSpec((tq, V_HEAD), lambda h, s, qt, kt: (kt[s], 2 * h + 1)),
         hspec, hspec, hspec],
        [pl.BlockSpec((tq, QK_PAD), lambda h, s, qt, kt: (kt[s], h)),
         pl.BlockSpec((tq, LANES), lambda h, s, qt, kt: (kt[s], h)),
         pl.BlockSpec((t, QK_PAD), lambda h, s, qt, kt: (0, h))],
        [pltpu.VMEM((tq, QK_PAD), f32), pltpu.VMEM((tq, V_HEAD), f32)],
        ("parallel", "arbitrary"), [q, kv, kr, kv, do, lse, delta], comm, at, prefetch=(qtab, ktab))
    return dkv, dkr, dq, cres


def _split3(a):
    hi = a.astype(bf16)
    r1 = a - hi.astype(f32)
    mid = r1.astype(bf16)
    lo = (r1 - mid.astype(f32)).astype(bf16)
    return hi, mid, lo


def _split2(a):
    hi = a.astype(bf16)
    return hi, (a - hi.astype(f32)).astype(bf16)


def _ones_dot_left(tri, a):
    hi, mid, lo = _split3(a)
    d = lambda v: lax.dot_general(tri, v, NN, preferred_element_type=f32)
    return d(hi) + d(mid) + d(lo)


def _ones_dot_right(a, tri):
    hi, mid, lo = _split3(a)
    d = lambda v: lax.dot_general(v, tri, NN, preferred_element_type=f32)
    return d(hi) + d(mid) + d(lo)


def _softplus(x):
    return jnp.maximum(x, 0.0) + jnp.log(1.0 + jnp.exp(-jnp.abs(x)))


def _head_spread(hs, width):
    shift = SSD_P.bit_length() - 1
    return (lax.broadcasted_iota(jnp.int32, (hs, width), 0)
            == lax.shift_right_logical(lax.broadcasted_iota(jnp.int32, (hs, width), 1), shift)).astype(bf16)


def _ssd_common(dt_ref, dtT_ref, dtb_ref, dtbT_ref, alog_ref, alogT_ref):
    ii = lax.broadcasted_iota(jnp.int32, (CHUNK, CHUNK), 0)
    jj = lax.broadcasted_iota(jnp.int32, (CHUNK, CHUNK), 1)
    tri = ii >= jj
    raw = dt_ref[...] + dtb_ref[...]
    dt = _softplus(raw)
    a_neg = -jnp.exp(alog_ref[...])
    cum = _ones_dot_left(tri.astype(bf16), dt * a_neg)
    dt_t = _softplus(dtT_ref[...] + dtbT_ref[...])
    cum_t = _ones_dot_right(dt_t * (-jnp.exp(alogT_ref[...])), (ii <= jj).astype(bf16))
    return tri, raw, dt, a_neg, cum, cum_t


def _ssd_fwd(xbc, dt, dt_t, dtb, dtb_t, alog, alog_t, dskip, width):
    t, cdim = xbc.shape
    hs = dt.shape[1]
    nc = t // CHUNK
    epg = hs // SSD_G
    gn = SSD_G * SSD_N

    def body(x_ref, dt_ref, dtT_ref, dtb_ref, dtbT_ref, alog_ref, alogT_ref, d_ref, y_ref, hp_ref, h_sc, yd_sc):
        @pl.when(pl.program_id(0) == 0)
        def _():
            h_sc[...] = jnp.zeros_like(h_sc)

        tri, _, dtv, _, cum, cum_t = _ssd_common(dt_ref, dtT_ref, dtb_ref, dtbT_ref, alog_ref, alogT_ref)
        spread = _head_spread(hs, width)
        clast = cum[CHUNK - 1:CHUNK, :]
        dec = jnp.exp(clast)
        wide = _ones_dot_right(jnp.concatenate([dtv, jnp.exp(cum), jnp.exp(clast - cum),
                                                jnp.broadcast_to(d_ref[...], (CHUNK, hs))], axis=0), spread)
        dt_x, ee_x, ff_x, dsk_x = (wide[i * CHUNK:(i + 1) * CHUNK] for i in range(4))
        xs = x_ref[:, :width]
        xdt = xs * dt_x
        xf = xdt * ff_x
        h_all = h_sc[...]
        hp_ref[0] = h_all
        h_all = h_all.reshape(hs * SSD_P, SSD_N)
        ch_parts = []
        for g in range(SSD_G):
            gsl = slice(g * epg * SSD_P, (g + 1) * epg * SSD_P)
            bb = x_ref[:, width + g * SSD_N: width + (g + 1) * SSD_N].astype(bf16)
            cb_ = x_ref[:, width + gn + g * SSD_N: width + gn + (g + 1) * SSD_N].astype(bf16)
            cbm = lax.dot_general(cb_, bb, NT, preferred_element_type=f32)
            ch_parts.append(_dot(cb_, h_all[gsl], NT))
            st = _dot(xf[:, gsl], bb, TN)
            for e in range(g * epg, (g + 1) * epg):
                esl = slice(e * SSD_P, (e + 1) * SSD_P)
                lmat = jnp.exp(jnp.where(tri, cum[:, e:e + 1] - cum_t[e:e + 1, :], -jnp.inf))
                yd_sc[:, esl] = _dot(cbm * lmat, xdt[:, esl], NN)
                j = e - g * epg
                h_sc[e] = h_sc[e] * dec[:, e:e + 1] + st[j * SSD_P:(j + 1) * SSD_P, :]
        y_ref[...] = yd_sc[...] + jnp.concatenate(ch_parts, axis=1) * ee_x + xs * dsk_x

    par = lambda a: pl.BlockSpec(a.shape, lambda i: (0,) * a.ndim)
    return pl.pallas_call(
        body, name="ssd_fwd",
        out_shape=(jax.ShapeDtypeStruct((t, width), f32), jax.ShapeDtypeStruct((nc, hs, SSD_P, SSD_N), f32)),
        grid=(nc,),
        in_specs=[pl.BlockSpec((CHUNK, cdim), lambda i: (i, 0)), pl.BlockSpec((CHUNK, hs), lambda i: (i, 0)),
                  pl.BlockSpec((hs, CHUNK), lambda i: (0, i)), par(dtb), par(dtb_t), par(alog), par(alog_t), par(dskip)],
        out_specs=(pl.BlockSpec((CHUNK, width), lambda i: (i, 0)),
                   pl.BlockSpec((1, hs, SSD_P, SSD_N), lambda i: (i, 0, 0, 0))),
        scratch_shapes=[pltpu.VMEM((hs, SSD_P, SSD_N), f32), pltpu.VMEM((CHUNK, width), f32)],
        compiler_params=_params(("arbitrary",)),
    )(xbc, dt, dt_t, dtb, dtb_t, alog, alog_t, dskip)


def _ssd_bwd(xbc, dt, dt_t, dtb, dtb_t, alog, alog_t, dskip, hprev, dy, width, comm=None, at=None):
    t, cdim = xbc.shape
    hs = dt.shape[1]
    nc = t // CHUNK
    epg = hs // SSD_G
    gn = SSD_G * SSD_N

    def body(x_ref, dt_ref, dtT_ref, dtb_ref, dtbT_ref, alog_ref, alogT_ref, d_ref, hp_ref, dy_ref,
             dx_ref, ddt_ref, dalog_ref, ddsk_ref, ddtb_ref, dh_sc, dxd_sc):
        @pl.when(pl.program_id(0) == 0)
        def _():
            dh_sc[...] = jnp.zeros_like(dh_sc)
            dalog_ref[...] = jnp.zeros_like(dalog_ref)
            ddsk_ref[...] = jnp.zeros_like(ddsk_ref)
            ddtb_ref[...] = jnp.zeros_like(ddtb_ref)

        tri, raw, dtv, a_neg, cum, cum_t = _ssd_common(dt_ref, dtT_ref, dtb_ref, dtbT_ref, alog_ref, alogT_ref)
        dsk = d_ref[...]
        head_row = lax.broadcasted_iota(jnp.int32, (1, hs), 1)
        head_col = lax.broadcasted_iota(jnp.int32, (hs, 1), 0)
        last_row = (lax.broadcasted_iota(jnp.int32, (CHUNK, 1), 0) == CHUNK - 1).astype(f32)
        shift = SSD_P.bit_length() - 1
        spread = _head_spread(hs, width)
        gather = (lax.shift_right_logical(lax.broadcasted_iota(jnp.int32, (width, hs), 0), shift)
                  == lax.broadcasted_iota(jnp.int32, (width, hs), 1)).astype(bf16)
        clast = cum[CHUNK - 1:CHUNK, :]
        ee = jnp.exp(cum)
        ff = jnp.exp(clast - cum)
        dec = jnp.exp(clast)
        wide = _ones_dot_right(jnp.concatenate([dtv, ee, ff, jnp.broadcast_to(dsk, (CHUNK, hs))], axis=0), spread)
        dt_x, ee_x, ff_x, dsk_x = (wide[i * CHUNK:(i + 1) * CHUNK] for i in range(4))
        xs = x_ref[:, :width]
        dyv = dy_ref[...]
        xdt = xs * dt_x
        dye = dyv * ee_x
        xf = xdt * ff_x
        h_all = hp_ref[0].reshape(hs * SSD_P, SSD_N)
        dh_all = dh_sc[...].reshape(hs * SSD_P, SSD_N)
        hi, lo = _split2(_ones_dot_left(spread, dh_all * h_all))
        ones8 = jnp.ones((8, SSD_N), bf16)
        hh = (lax.dot_general(ones8, hi, NT, preferred_element_type=f32)
              + lax.dot_general(ones8, lo, NT, preferred_element_type=f32))[0:1]
        rowsum_m = jnp.zeros((CHUNK, hs), f32)
        colsum_m = jnp.zeros((hs, CHUNK), f32)
        ch_parts, bds_parts = [], []
        for g in range(SSD_G):
            bsl = slice(width + g * SSD_N, width + (g + 1) * SSD_N)
            csl = slice(width + gn + g * SSD_N, width + gn + (g + 1) * SSD_N)
            gsl = slice(g * epg * SSD_P, (g + 1) * epg * SSD_P)
            bb = x_ref[:, bsl].astype(bf16)
            cb_ = x_ref[:, csl].astype(bf16)
            cbm = lax.dot_general(cb_, bb, NT, preferred_element_type=f32)
            hg = h_all[gsl].astype(bf16)
            dhg = dh_all[gsl].astype(bf16)
            ch_parts.append(_dot(cb_, hg, NT))
            bds_parts.append(_dot(bb, dhg, NT))
            dcg = _dot(dye[:, gsl], hg, NN)
            dbg = _dot(xf[:, gsl], dhg, NN)
            dh_new = _dot(dye[:, gsl], cb_, TN)
            dcb = jnp.zeros((CHUNK, CHUNK), f32)
            for e in range(g * epg, (g + 1) * epg):
                esl = slice(e * SSD_P, (e + 1) * SSD_P)
                lmat = jnp.exp(jnp.where(tri, cum[:, e:e + 1] - cum_t[e:e + 1, :], -jnp.inf))
                gmat = cbm * lmat
                dy_e = dyv[:, esl].astype(bf16)
                dgm = _dot(dy_e, xdt[:, esl], NT)
                dxd_sc[:, esl] = _dot(gmat, dy_e, TN)
                dcb = dcb + dgm * lmat
                mm = dgm * gmat
                rowsum_m = rowsum_m + jnp.sum(mm, axis=1, keepdims=True) * (head_row == e).astype(f32)
                colsum_m = colsum_m + (head_col == e).astype(f32) * jnp.sum(mm, axis=0, keepdims=True)
                j = e - g * epg
                dh_sc[e] = dh_new[j * SSD_P:(j + 1) * SSD_P, :] + dec[:, e:e + 1] * dh_sc[e]
            dx_ref[:, bsl] = dbg + _dot(dcb, cb_, TN)
            dx_ref[:, csl] = dcg + _dot(dcb, bb, NN)
        ch_all = jnp.concatenate(ch_parts, axis=1)
        bds_all = jnp.concatenate(bds_parts, axis=1)
        dxdt = bds_all * ff_x + dxd_sc[...]
        dx_ref[:, :width] = dxdt * dt_x + dyv * dsk_x
        sums = _ones_dot_right(jnp.concatenate([dxdt * xs, dyv * ch_all, bds_all * xdt, dyv * xs], axis=0), gather)
        ddtx_all = sums[0:CHUNK]
        dff = sums[2 * CHUNK:3 * CHUNK] * ff
        ddsk = jnp.sum(sums[3 * CHUNK:4 * CHUNK], axis=0, keepdims=True)
        dclast = jnp.sum(dff, axis=0, keepdims=True) + dec * hh
        eye = (lax.broadcasted_iota(jnp.int32, (hs, hs), 0) == lax.broadcasted_iota(jnp.int32, (hs, hs), 1)).astype(bf16)
        colsum_t = sum(lax.dot_general(v, eye, TN, preferred_element_type=f32) for v in _split3(colsum_m))
        dcum_all = sums[CHUNK:2 * CHUNK] * ee - dff + rowsum_m - colsum_t + dclast * last_row
        ii = lax.broadcasted_iota(jnp.int32, (CHUNK, CHUNK), 0)
        jj = lax.broadcasted_iota(jnp.int32, (CHUNK, CHUNK), 1)
        da = _ones_dot_left((jj >= ii).astype(bf16), dcum_all)
        ddt = da * a_neg + ddtx_all
        dalog_ref[...] += jnp.sum(da * dtv, axis=0, keepdims=True) * a_neg
        draw = ddt * _sigmoid(raw)
        place = (lax.broadcasted_iota(jnp.int32, (hs, LANES), 0) + ROPE
                 == lax.broadcasted_iota(jnp.int32, (hs, LANES), 1)).astype(bf16)
        ddt_ref[...] = _ones_dot_right(draw, place)
        ddtb_ref[...] += jnp.sum(draw, axis=0, keepdims=True)
        ddsk_ref[...] += ddsk

    rev = lambda i: nc - 1 - i
    par = lambda a: pl.BlockSpec(a.shape, lambda i: (0,) * a.ndim)
    acc = pl.BlockSpec((1, hs), lambda i: (0, 0))
    acc_shape = jax.ShapeDtypeStruct((1, hs), f32)
    res, cres = _call(
        body, "ssd_bwd",
        [jax.ShapeDtypeStruct((t, cdim), f32), jax.ShapeDtypeStruct((t, LANES), f32), acc_shape, acc_shape, acc_shape],
        (nc,),
        [pl.BlockSpec((CHUNK, cdim), lambda i: (rev(i), 0)), pl.BlockSpec((CHUNK, hs), lambda i: (rev(i), 0)),
         pl.BlockSpec((hs, CHUNK), lambda i: (0, rev(i))), par(dtb), par(dtb_t), par(alog), par(alog_t),
         par(dskip), pl.BlockSpec((1, hs, SSD_P, SSD_N), lambda i: (rev(i), 0, 0, 0)),
         pl.BlockSpec((CHUNK, width), lambda i: (rev(i), 0))],
        [pl.BlockSpec((CHUNK, cdim), lambda i: (rev(i), 0)), pl.BlockSpec((CHUNK, LANES), lambda i: (rev(i), 0)),
         acc, acc, acc],
        [pltpu.VMEM((hs, SSD_P, SSD_N), f32), pltpu.VMEM((CHUNK, width), f32)], ("arbitrary",),
        [xbc, dt, dt_t, dtb, dtb_t, alog, alog_t, dskip, hprev, dy], comm, at)
    return (*res, cres)


def _where_am_i():
    x, y, c = lax.axis_index("x"), lax.axis_index("y"), lax.axis_index("c")
    chips = [(1 - x, y), (x, 1 - y), (1 - x, 1 - y)]
    return x, y, c, chips


def _remote(src, dst, send_sems, recv_sems, k, to):
    return pltpu.make_async_remote_copy(src_ref=src, dst_ref=dst, send_sem=send_sems.at[k], recv_sem=recv_sems.at[k],
                                        device_id=to, device_id_type=MESH)


class _Comm:
    def __init__(self, args, outs, nsem, nphase, run):
        self.args, self.outs, self.nsem, self.nphase, self.run = list(args), list(outs), nsem, nphase, run

    def sems(self):
        return [pltpu.SemaphoreType.DMA((self.nsem,)), pltpu.SemaphoreType.DMA((self.nsem,))]


def _comm_call(comm, name):
    na, no = len(comm.args), len(comm.outs)

    def body(*refs):
        for phase in range(comm.nphase):
            comm.run(phase, refs[:na], refs[na:na + no], refs[na + no], refs[na + no + 1])

    return list(pl.pallas_call(body, name=name, out_shape=comm.outs, in_specs=[ANY] * na, out_specs=[ANY] * no,
                               scratch_shapes=comm.sems())(*comm.args))


def _half(ref, c, r2):
    return ref.at[pl.ds(c * r2, r2)]


def _gather_weights(shards, wholes):
    ns, nw = len(shards), len(wholes)
    per = 7

    def run(phase, srcs, outs, send_sems, recv_sems):
        x, y, c, chips = _where_am_i()
        me, sib = 2 * x + y, (x, y, 1 - c)
        def first():
            cps = []
            for w in range(ns + nw):
                src, out, base = srcs[w], outs[w], per * w
                halved = w < ns
                r2 = src.shape[0] // 2
                piece = _half(src, c, r2) if halved else src
                for j, (cx, cy) in enumerate(chips):
                    dst = _half(out.at[me], c, r2) if halved else out.at[me]
                    cps.append(_remote(piece, dst, send_sems, recv_sems, base + j, (cx, cy, c)))
                cps.append(_remote(src, out.at[me], send_sems, recv_sems, base + 6, sib))
            return cps

        def passed():
            cps = []
            for w in range(ns):
                r2 = srcs[w].shape[0] // 2
                for j, (cx, cy) in enumerate(chips):
                    got = _half(outs[w].at[2 * cx + cy], c, r2)
                    cps.append(_remote(got, got, send_sems, recv_sems, per * w + 3 + j, sib))
            return cps

        if phase == 0:
            for cp in first():
                cp.start()
        elif phase == 1:
            it = iter(passed())
            for w in range(ns + nw):
                src, out, base = srcs[w], outs[w], per * w
                r2 = src.shape[0] // 2
                for j, (cx, cy) in enumerate(chips):
                    got = _half(out.at[2 * cx + cy], c, r2) if w < ns else out.at[2 * cx + cy]
                    _remote(got, got, send_sems, recv_sems, base + j, sib).wait_recv()
                    if w < ns:
                        next(it).start()
        else:
            for w in range(ns + nw):
                src, out, base = srcs[w], outs[w], per * w
                r2 = src.shape[0] // 2
                if w < ns:
                    for j, (cx, cy) in enumerate(chips):
                        got = _half(out.at[2 * cx + cy], 1 - c, r2)
                        _remote(got, got, send_sems, recv_sems, base + 3 + j, sib).wait_recv()
                _remote(src, out.at[me], send_sems, recv_sems, base + 6, sib).wait_recv()
            for cp in first() + passed():
                cp.wait_send()

    args = list(shards) + list(wholes)
    outs = [jax.ShapeDtypeStruct((4,) + a.shape, a.dtype) for a in args]
    return _Comm(args, outs, per * len(args), 3, run)


def _sibling_send_halves(gs):
    n = len(gs)

    def run(phase, srcs, outs, send_sems, recv_sems):
        x, y, c, _ = _where_am_i()
        sib = (x, y, 1 - c)
        cps = []
        for w in range(n):
            r2 = srcs[w].shape[1] // 2
            for k in range(4):
                cps.append(_remote(_half(srcs[w].at[k], 1 - c, r2), outs[w].at[k], send_sems, recv_sems, 4 * w + k, sib))
        for cp in cps:
            cp.start() if phase == 0 else cp.wait()

    outs = [jax.ShapeDtypeStruct((4, g.shape[1] // 2, g.shape[2]), g.dtype) for g in gs]
    return _Comm(gs, outs, 4 * n, 2, run)


def _chips_exchange(ps):
    n = len(ps)

    def run(phase, srcs, outs, send_sems, recv_sems):
        x, y, c, chips = _where_am_i()
        me = 2 * x + y
        cps = []
        for w in range(n):
            for j, (cx, cy) in enumerate(chips):
                cps.append(_remote(srcs[w].at[2 * cx + cy], outs[w].at[me], send_sems, recv_sems, 3 * w + j, (cx, cy, c)))
        if phase == 0:
            for cp in cps:
                cp.start()
        else:
            for w in range(n):
                for j, (cx, cy) in enumerate(chips):
                    got = outs[w].at[2 * cx + cy]
                    _remote(got, got, send_sems, recv_sems, 3 * w + j, (cx, cy, c)).wait_recv()
            for cp in cps:
                cp.wait_send()

    outs = [jax.ShapeDtypeStruct(p.shape, p.dtype) for p in ps]
    return _Comm(ps, outs, 3 * n, 2, run)


def _sibling_swap(rs):
    n = len(rs)

    def run(phase, srcs, outs, send_sems, recv_sems):
        x, y, c, _ = _where_am_i()
        for w in range(n):
            cp = _remote(srcs[w], outs[w], send_sems, recv_sems, w, (x, y, 1 - c))
            cp.start() if phase == 0 else cp.wait()

    outs = [jax.ShapeDtypeStruct(r.shape, r.dtype) for r in rs]
    return _Comm(rs, outs, n, 2, run)


def _gather_all(v):
    rows = v.shape[0]

    def body(x_ref, out_ref, send_sems, recv_sems, local_sem):
        x, y, c, chips = _where_am_i()
        me, sib = (x, y, c), (x, y, 1 - c)
        blk = lambda px, py, pc: out_ref.at[4 * px + 2 * py + pc]
        mine = pltpu.make_async_copy(x_ref, blk(*me), local_sem)
        mine.start()
        first = [_remote(x_ref, blk(*me), send_sems, recv_sems, 0, sib)]
        first += [_remote(x_ref, blk(*me), send_sems, recv_sems, 1 + j, (*chip, c)) for j, chip in enumerate(chips)]
        for cp in first:
            cp.start()
        passed = [_remote(blk(*chip, c), blk(*chip, c), send_sems, recv_sems, 4 + j, sib) for j, chip in enumerate(chips)]
        for j, chip in enumerate(chips):
            _remote(blk(*chip, c), blk(*chip, c), send_sems, recv_sems, 1 + j, me).wait_recv()
            passed[j].start()
        _remote(blk(*sib), blk(*sib), send_sems, recv_sems, 0, me).wait_recv()
        for j, chip in enumerate(chips):
            _remote(blk(*chip, 1 - c), blk(*chip, 1 - c), send_sems, recv_sems, 4 + j, me).wait_recv()
        for cp in first + passed:
            cp.wait_send()
        mine.wait()

    vm = pl.BlockSpec(memory_space=pltpu.VMEM)
    return pl.pallas_call(
        body, name="gather_all", out_shape=jax.ShapeDtypeStruct((8, rows, LANES), v.dtype),
        in_specs=[vm], out_specs=vm,
        scratch_shapes=[pltpu.SemaphoreType.DMA((7,)), pltpu.SemaphoreType.DMA((7,)), pltpu.SemaphoreType.DMA],
    )(v)


def _flat_pad(parts, total):
    v = jnp.concatenate([p.reshape(-1) for p in parts])
    return jnp.pad(v, (0, total - v.shape[0]))


def _adamw(w, g, m, v):
    m = ADAM_B1 * m + (1.0 - ADAM_B1) * g
    v = ADAM_B2 * v + (1.0 - ADAM_B2) * jnp.square(g)
    m_hat = m / (1.0 - ADAM_B1 ** ADAM_STEP)
    v_hat = v / (1.0 - ADAM_B2 ** ADAM_STEP)
    delta = -ADAM_LR * (m_hat / (jnp.sqrt(v_hat) + ADAM_EPS) + ADAM_WD * w)
    return delta, m, v


def _adamw_call(w, g, m, v, name):
    r, cdim = w.shape
    tm = _tile(r, ROW_TILE, 8)
    o = _rout(r, cdim, f32, tm)
    return _rowwise(_adamw, r, tm, [_rows(w, tm), _rows(g, tm), _rows(m, tm), _rows(v, tm)], [o, o, o], [], name)


def _adamw_halves(w, m, v, g_mine, g_sib, sp, name):
    _, r, cdim = w.shape
    r2 = r // 2
    tr = _tile(r2, ROW_TILE, 8)
    nb = r2 // tr

    def body(sp_ref, w_ref, m_ref, v_ref, ga_ref, gb_ref, g_out, d_out, m_out, v_out):
        g = jnp.where(pl.program_id(0) == sp_ref[1], ga_ref[...], gb_ref[...])
        delta, mn, vn = _adamw(w_ref[...], g, m_ref[...], v_ref[...])
        g_out[...] = g
        d_out[...] = delta
        m_out[...] = mn
        v_out[...] = vn

    full = pl.BlockSpec((None, tr, cdim), lambda h, i, s: (0, h * nb + i, 0))
    mine = pl.BlockSpec((tr, cdim), lambda h, i, s: (jnp.where(h == s[1], i, 0), 0))
    sib = pl.BlockSpec((tr, cdim), lambda h, i, s: (jnp.where(h == s[1], 0, i), 0))
    out = jax.ShapeDtypeStruct((1, r, cdim), f32)
    gs = pltpu.PrefetchScalarGridSpec(num_scalar_prefetch=1, grid=(2, nb), in_specs=[full, full, full, mine, sib],
                                      out_specs=[full, full, full, full])
    return pl.pallas_call(body, name=name, out_shape=[out, out, out, out], grid_spec=gs,
                          compiler_params=_params(("parallel", "parallel")))(sp, w, m, v, g_mine, g_sib)


MIXER = ("w_in", "w_uq", "w_ukv", "w_out")
FFN = ("w_gate", "w_up", "w_down")
BIG = MIXER + FFN
SMALL = ("q_norm_w", "kv_norm_w", "conv_w", "conv_b", "dt_bias", "a_log", "d_skip", "ssd_norm_w", "attn_out_norm_w",
         "pre_mix_norm_w", "post_mix_norm_w", "pre_ffn_norm_w", "post_ffn_norm_w")
ORDER = ("w_in", "q_norm_w", "w_uq", "kv_norm_w", "w_ukv", "conv_w", "conv_b", "dt_bias", "a_log", "d_skip",
         "ssd_norm_w", "attn_out_norm_w", "w_out", "pre_mix_norm_w", "post_mix_norm_w", "pre_ffn_norm_w",
         "post_ffn_norm_w", "w_gate", "w_up", "w_down")


def kernel(x, positions, w_in, q_norm_w, w_uq, kv_norm_w, w_ukv, conv_w, conv_b, dt_bias, a_log, d_skip, ssd_norm_w, attn_out_norm_w, w_out, pre_mix_norm_w, post_mix_norm_w, pre_ffn_norm_w, post_ffn_norm_w, w_gate, w_up, w_down, loss_target, m_w_in, m_q_norm_w, m_w_uq, m_kv_norm_w, m_w_ukv, m_conv_w, m_conv_b, m_dt_bias, m_a_log, m_d_skip, m_ssd_norm_w, m_attn_out_norm_w, m_w_out, m_pre_mix_norm_w, m_post_mix_norm_w, m_pre_ffn_norm_w, m_post_ffn_norm_w, m_w_gate, m_w_up, m_w_down, v_w_in, v_q_norm_w, v_w_uq, v_kv_norm_w, v_w_ukv, v_conv_w, v_conv_b, v_dt_bias, v_a_log, v_d_skip, v_ssd_norm_w, v_attn_out_norm_w, v_w_out, v_pre_mix_norm_w, v_post_mix_norm_w, v_pre_ffn_norm_w, v_post_ffn_norm_w, v_w_gate, v_w_up, v_w_down):
    local = dict(locals())
    wts = {n: local[n][0] for n in ORDER}
    mom_m = {n: local["m_" + n][0] for n in ORDER}
    mom_v = {n: local["v_" + n][0] for n in ORDER}
    xs = x[0]
    tgt = loss_target[0]
    t, d = xs.shape
    nchip = 4
    my_x, my_y, my_c = lax.axis_index("x"), lax.axis_index("y"), lax.axis_index("c")
    my_chip = 2 * my_x + my_y

    mla_w = d // 2
    nh = mla_w // V_HEAD
    width = d - mla_w
    hs = width // SSD_P
    gn = SSD_G * SSD_N
    cdim = width + 2 * gn
    in_sizes = (Q_RANK, KV_RANK, ROPE, width, cdim, hs)
    d_in = sum(in_sizes)
    tail = LANES
    off_xbc = 0
    off_cq = cdim
    off_ckv = off_cq + Q_RANK
    off_z = off_ckv + KV_RANK
    off_tail = off_z + width
    d_in_p = _round_up(off_tail + tail, 256)
    gw = width // SSD_G
    assert off_cq % Q_RANK == 0 and off_ckv % KV_RANK == 0 and off_z % gw == 0 and off_tail % LANES == 0
    qk_head = NOPE + ROPE
    scale = qk_head ** -0.5
    tm = _tile(t, ROW_TILE, 8)
    tmw = _tile(t, ROW_TILE // 2, 8)

    sp = jnp.stack([my_chip, my_c]).astype(jnp.int32)

    def pair_sums(names, gl, from_sib):
        res = []
        for n, g, fs in zip(names, gl, from_sib):
            _, r2, cs = fs.shape
            tr = _tile(r2, ROW_TILE, 16)
            nb = r2 // tr
            res.append(_blocked(
                lambda a, b: a + b, (nchip, nb),
                [(g, (None, tr, cs), lambda k, i, s, nb=nb: (k, s[1] * nb + i, 0)),
                 (fs, (None, tr, cs), lambda k, i, s: (k, i, 0))],
                [((nchip, r2, cs), bf16, (None, tr, cs), lambda k, i, s: (k, i, 0))], [], "pair_sum_" + n, sp=sp)[0])
        return res

    def chip_sums(names, gl, from_sib, from_chips):
        res = []
        for n, g, fs, fc in zip(names, gl, from_sib, from_chips):
            _, r2, cs = fs.shape
            tr = _tile(r2, ROW_TILE, 16)
            nb = r2 // tr
            res.append(_blocked(
                lambda a, b, r1, r2_, r3: ((a + b) + r1.astype(f32)) + r2_.astype(f32) + r3.astype(f32), (nb,),
                [(g, (None, tr, cs), lambda i, s, nb=nb: (s[0], s[1] * nb + i, 0)),
                 (fs, (None, tr, cs), lambda i, s: (s[0], i, 0)),
                 (fc, (None, tr, cs), lambda i, s: (s[0] ^ 1, i, 0)),
                 (fc, (None, tr, cs), lambda i, s: (s[0] ^ 2, i, 0)),
                 (fc, (None, tr, cs), lambda i, s: (s[0] ^ 3, i, 0))],
                [((r2, cs), f32, (tr, cs), lambda i, s: (i, 0))], [], "chip_sum_" + n, sp=sp)[0])
        return res

    ck, ccs = wts["conv_w"].shape
    w_in_g, conv_g = _comm_call(_gather_weights([wts["w_in"].astype(bf16)], [wts["conv_w"]]), "gather_w_in")
    cat_cols = lambda g: jnp.concatenate([g[k] for k in range(nchip)], axis=1)
    conv_full = cat_cols(conv_g)
    mixer_gather = _gather_weights([wts[n].astype(bf16) for n in MIXER[1:]], [])
    ffn_gather = _gather_weights([wts[n].astype(bf16) for n in FFN], [])

    wi = cat_cols(w_in_g)
    o = np.cumsum((0,) + in_sizes)
    seg = lambda i: wi[:, o[i]:o[i + 1]]
    w_in_p = jnp.concatenate([seg(4), seg(0), seg(1), seg(3), seg(2), seg(5),
                              jnp.zeros((d, d_in_p - off_tail - ROPE - hs), bf16)], axis=1)

    inv_freq = ROPE_THETA ** (-jnp.arange(0, ROPE, 2, dtype=f32) / ROPE)
    ang = positions[0].astype(f32)[:, None] * inv_freq
    cos, sin = jnp.cos(ang), jnp.sin(ang)
    z32, z64, z96 = jnp.zeros((t, 32), f32), jnp.zeros((t, 64), f32), jnp.zeros((t, 96), f32)
    cosp = jnp.concatenate([cos, cos, z64], axis=1)
    sina = jnp.concatenate([-sin, z96], axis=1)
    sinb = jnp.concatenate([z32, sin, z64], axis=1)

    row = lambda a: a.reshape(1, -1)
    w_pre_mix, w_post_mix = row(wts["pre_mix_norm_w"]), row(wts["post_mix_norm_w"])
    w_pre_ffn, w_post_ffn = row(wts["pre_ffn_norm_w"]), row(wts["post_ffn_norm_w"])
    w_qn, w_kvn = row(wts["q_norm_w"]), row(wts["kv_norm_w"])
    w_attn_n, w_ssd_n = row(wts["attn_out_norm_w"]), row(wts["ssd_norm_w"])
    conv_b_r = row(wts["conv_b"])
    dtb, alog, dskip = row(wts["dt_bias"]), row(wts["a_log"]), row(wts["d_skip"])
    dtb_t, alog_t = dtb.reshape(hs, 1), alog.reshape(hs, 1)

    u = _rowwise(lambda a, w: _rms(a, w), t, tm, [_rows(xs, tm), _par(w_pre_mix)], [_rout(t, d, bf16, tm)], [],
                 "pre_mix_norm")[0]
    proj, (w_uq_g, w_ukv_f, w_out_g) = _matmul([(u, w_in_p)], "nn", f32, "in_proj", comm=mixer_gather,
                                               at=(0.0, 0.6))
    w_uq_p = jnp.pad(cat_cols(w_uq_g).reshape(Q_RANK, nh, qk_head), ((0, 0), (0, 0), (0, QK_PAD - qk_head))
                     ).reshape(Q_RANK, nh * QK_PAD)
    w_out_f = w_out_g.reshape(-1, w_out_g.shape[2])
    cq_in = _rows(proj, tm, Q_RANK, off_cq // Q_RANK)
    ckv_in = _rows(proj, tm, KV_RANK, off_ckv // KV_RANK)
    cqn = _rowwise(lambda a, w: _rms(a, w), t, tm, [cq_in, _par(w_qn)], [_rout(t, Q_RANK, bf16, tm)], [], "q_norm")[0]
    ckvn = _rowwise(lambda a, w: _rms(a, w), t, tm, [ckv_in, _par(w_kvn)], [_rout(t, KV_RANK, bf16, tm)], [],
                    "kv_norm")[0]
    q_raw = _matmul([(cqn, w_uq_p)], "nn", f32, "q_up")
    kv = _matmul([(ckvn, w_ukv_f)], "nn", bf16, "kv_up")

    def q_rope_fn(qt, cp, sa, sb):
        parts = []
        for h in range(nh):
            parts.append(qt[:, h * QK_PAD: h * QK_PAD + NOPE])
            parts.append(_rope(qt[:, h * QK_PAD + NOPE:(h + 1) * QK_PAD], cp, sa, sb))
        return jnp.concatenate(parts, axis=1)

    tail_cb = off_tail // LANES
    q2 = _rowwise(q_rope_fn, t, tm, [_rows(q_raw, tm), _rows(cosp, tm), _rows(sina, tm), _rows(sinb, tm)],
                  [_rout(t, nh * QK_PAD, bf16, tm)], [], "q_rope")[0]
    kr2 = _rowwise(_rope, t, tm, [_rows(proj, tm, LANES, tail_cb), _rows(cosp, tm), _rows(sina, tm), _rows(sinb, tm)],
                   [_rout(t, LANES, bf16, tm)], [], "k_rope")[0]
    o_att, lse, ffn_w = _flash_fwd(q2, kv, kr2, nh, scale, ffn_gather, (0.0, 0.75))
    w_gate_f, w_up_f = ffn_w[0], ffn_w[1]
    w_down_f = ffn_w[2].reshape(-1, ffn_w[2].shape[2])

    xbc_act = _conv_fwd(proj, conv_full, conv_b_r, cdim, tm)
    dt_raw, dt_raw_t = _blocked(
        lambda blk: (blk[:, ROPE:ROPE + hs], blk.T[ROPE:ROPE + hs, :]), (t // tm,),
        [_rows(proj, tm, LANES, tail_cb)],
        [((t, hs), f32, (tm, hs), lambda i: (i, 0)), ((hs, t), f32, (hs, tm), lambda i: (0, i))], [], "dt_split")
    y_ssd, hprev = _ssd_fwd(xbc_act, dt_raw, dt_raw_t, dtb, dtb_t, alog, alog_t, dskip, width)
    z_ins = [_rows(proj, tm, gw, off_z // gw + i) for i in range(SSD_G)]

    def mix_norms_fn(ov, yv, *rest):
        zs, wa, ws = rest[:SSD_G], rest[SSD_G], rest[SSD_G + 1]
        outs = [_rms(ov, wa)]
        for i in range(SSD_G):
            sl = slice(i * gw, (i + 1) * gw)
            outs.append(_rms(yv[:, sl] * (zs[i] * _sigmoid(zs[i])), ws[:, sl]))
        return jnp.concatenate(outs, axis=1)

    cat = _rowwise(mix_norms_fn, t, tm, [_rows(o_att, tm), _rows(y_ssd, tm)] + z_ins + [_par(w_attn_n), _par(w_ssd_n)],
                   [_rout(t, d, bf16, tm)], [], "attn_ssd_out_norms")[0]
    mix = _matmul([(cat, w_out_f)], "nn", f32, "out_proj")

    def post_mix_fn(mx, xv, w1, w2):
        h1v = xv + _rms(mx, w1)
        return h1v, _rms(h1v, w2)

    h1, v_in = _rowwise(post_mix_fn, t, tmw, [_rows(mix, tmw), _rows(xs, tmw), _par(w_post_mix), _par(w_pre_ffn)],
                        [_rout(t, d, f32, tmw), _rout(t, d, bf16, tmw)], [], "post_mix_pre_ffn_norm")
    g_ff, u_ff, act = _ffn_up(v_in, w_gate_f, w_up_f)
    ffn = _matmul([(act, w_down_f)], "nn", f32, "ffn_down")

    def final_fn(fv, h1v, tg, w):
        h2 = h1v + _rms(fv, w)
        err = h2 - tg
        lpart = 0.5 * jnp.sum(jnp.sum(err * err, axis=1, keepdims=True), axis=0, keepdims=True) / d
        dh2 = err / d
        dff, dw = _rms_bwd(fv, w, dh2)
        return dff, dh2, jnp.broadcast_to(lpart, (1, LANES)), dw

    dffn, dh2, loss_acc, g_post_ffn = _rowwise(
        final_fn, t, tmw, [_rows(ffn, tmw), _rows(h1, tmw), _rows(tgt, tmw), _par(w_post_ffn)],
        [_rout(t, d, bf16, tmw), _rout(t, d, f32, tmw)], [(1, LANES), (1, d)], "loss_post_ffn_norm_bwd")
    loss = lax.psum(loss_acc[0, 0], ("x", "y", "c"))

    dg_ff, du_ff = _ffn_down_bwd(dffn, w_down_f, g_ff, u_ff)
    gw_down = _matmul([(act, dffn)], "tn", f32, "grad_w_down")
    gw_gate = _matmul([(v_in, dg_ff)], "tn", f32, "grad_w_gate", out_chunks=nchip)
    gw_up = _matmul([(v_in, du_ff)], "tn", f32, "grad_w_up", out_chunks=nchip)
    ffn_g = [gw_gate, gw_up, gw_down.reshape(nchip, -1, d)]
    dv_in, ffn_from_sib = _matmul([(dg_ff, w_gate_f), (du_ff, w_up_f)], "nt", f32, "ffn_up_bwd",
                                  comm=_sibling_send_halves(ffn_g), at=(0.0,))
    ffn_pairs = pair_sums(FFN, ffn_g, ffn_from_sib)

    def mid_bwd_fn(h1v, dvv, dh2v, mx, w_pf, w_pm):
        dxn, dw_pf = _rms_bwd(h1v, w_pf, dvv)
        dh1v = dh2v + dxn
        dmx, dw_pm = _rms_bwd(mx, w_pm, dh1v)
        return dh1v, dmx, dw_pf, dw_pm

    dh1, dmix, g_pre_ffn, g_post_mix = _rowwise(
        mid_bwd_fn, t, tmw, [_rows(h1, tmw), _rows(dv_in, tmw), _rows(dh2, tmw), _rows(mix, tmw),
                             _par(w_pre_ffn), _par(w_post_mix)],
        [_rout(t, d, f32, tmw), _rout(t, d, bf16, tmw)], [(1, d), (1, d)], "pre_ffn_post_mix_norm_bwd")
    dcat = _matmul([(dmix, w_out_f)], "nt", f32, "out_proj_bwd")
    gw_out = _matmul([(cat, dmix)], "tn", f32, "grad_w_out")

    def mix_norms_bwd_fn(ov, yv, *rest):
        zs, dcv, wa, ws = rest[:SSD_G], rest[SSD_G], rest[SSD_G + 1], rest[SSD_G + 2]
        dov, dwa = _rms_bwd(ov, wa, dcv[:, :mla_w])
        dl = [jnp.broadcast_to(jnp.sum(dov[:, h * V_HEAD:(h + 1) * V_HEAD] * ov[:, h * V_HEAD:(h + 1) * V_HEAD],
                                       axis=1, keepdims=True), (ov.shape[0], V_HEAD)) for h in range(nh)]
        dys, dzs, dws = [], [], []
        for i in range(SSD_G):
            sl = slice(i * gw, (i + 1) * gw)
            zv, yi = zs[i], yv[:, sl]
            sg = _sigmoid(zv)
            sz = zv * sg
            dgi, dwi = _rms_bwd(yi * sz, ws[:, sl], dcv[:, mla_w + i * gw: mla_w + (i + 1) * gw])
            dys.append(dgi * sz)
            dzs.append(dgi * yi * (sg * (1.0 + zv * (1.0 - sg))))
            dws.append(dwi)
        cc = lambda a: jnp.concatenate(a, axis=1)
        return dov, cc(dl), cc(dys), cc(dzs), dwa, cc(dws)

    do_att, delta, dy_ssd, dz, g_attn_n, g_ssd_n = _rowwise(
        mix_norms_bwd_fn, t, tm,
        [_rows(o_att, tm), _rows(y_ssd, tm)] + z_ins + [_rows(dcat, tm), _par(w_attn_n), _par(w_ssd_n)],
        [_rout(t, mla_w, bf16, tm), _rout(t, mla_w, f32, tm), _rout(t, width, f32, tm), _rout(t, width, bf16, tm)],
        [(1, mla_w), (1, width)], "attn_ssd_out_norms_bwd")
    dkv, dkr_h, dq2, ffn_from_chips = _flash_bwd(q2, kv, kr2, do_att, lse, delta, nh, scale,
                                                 _chips_exchange(ffn_pairs), (0.0,))
    ffn_halves = chip_sums(FFN, ffn_g, ffn_from_sib, ffn_from_chips)

    def q_rope_bwd_fn(dqt, cp, sa, sb):
        parts = []
        for h in range(nh):
            parts.append(dqt[:, h * QK_PAD: h * QK_PAD + NOPE])
            parts.append(_rope_bwd(dqt[:, h * QK_PAD + NOPE:(h + 1) * QK_PAD], cp, sa, sb))
        return jnp.concatenate(parts, axis=1)

    dq_raw = _rowwise(q_rope_bwd_fn, t, tm, [_rows(dq2, tm), _rows(cosp, tm), _rows(sina, tm), _rows(sinb, tm)],
                      [_rout(t, nh * QK_PAD, bf16, tm)], [], "q_rope_bwd")[0]

    def k_rope_bwd_fn(dk, cp, sa, sb):
        tot = dk[:, 0:LANES]
        for h in range(1, nh):
            tot = tot + dk[:, h * LANES:(h + 1) * LANES]
        return _rope_bwd(tot, cp, sa, sb)

    dkr = _rowwise(k_rope_bwd_fn, t, tm, [_rows(dkr_h, tm), _rows(cosp, tm), _rows(sina, tm), _rows(sinb, tm)],
                   [_rout(t, LANES, f32, tm)], [], "k_rope_bwd")[0]
    gw_uq_p = _matmul([(cqn, dq_raw)], "tn", f32, "grad_w_uq")
    gw_ukv = _matmul([(ckvn, dkv)], "tn", f32, "grad_w_ukv", out_chunks=nchip)
    dcqn = _matmul([(dq_raw, w_uq_p)], "nt", f32, "q_up_bwd")
    dckvn = _matmul([(dkv, w_ukv_f)], "nt", f32, "kv_up_bwd")

    def lat_norm_bwd_fn(a, w, dyv):
        return _rms_bwd(a, w, dyv)

    dcq, g_qn = _rowwise(lat_norm_bwd_fn, t, tm, [cq_in, _par(w_qn), _rows(dcqn, tm)],
                         [_rout(t, Q_RANK, bf16, tm)], [(1, Q_RANK)], "q_norm_bwd")
    dckv, g_kvn = _rowwise(lat_norm_bwd_fn, t, tm, [ckv_in, _par(w_kvn), _rows(dckvn, tm)],
                           [_rout(t, KV_RANK, bf16, tm)], [(1, KV_RANK)], "kv_norm_bwd")

    dxbc_act, ddt_raw, g_alog, g_dskip, g_dtb, ffn_sib_halves = _ssd_bwd(
        xbc_act, dt_raw, dt_raw_t, dtb, dtb_t, alog, alog_t, dskip, hprev, dy_ssd, width,
        _sibling_swap(ffn_halves), (0.0,))
    dpre, gcw0, gcw1, gcw2, gcw3, g_conv_b = _conv_bwd_pre(proj, conv_full, conv_b_r, dxbc_act, cdim, tm)
    g_conv_w = jnp.concatenate([gcw0, gcw1, gcw2, gcw3], axis=0)
    dxbc = _conv_bwd_dx(dpre, conv_full, tm)

    def dproj_fn(a, b, c, e, kr_blk, dt_blk):
        parts = [a, b, c, e, (kr_blk + dt_blk).astype(bf16)]
        if d_in_p > off_tail + tail:
            parts.append(jnp.zeros((a.shape[0], d_in_p - off_tail - tail), bf16))
        return jnp.concatenate(parts, axis=1)

    dproj = _rowwise(dproj_fn, t, tm, [_rows(dxbc, tm), _rows(dcq, tm), _rows(dckv, tm), _rows(dz, tm),
                                       _rows(dkr, tm), _rows(ddt_raw, tm)],
                     [_rout(t, d_in_p, bf16, tm)], [], "in_proj_grad_pack")[0]
    split_cols = lambda gf: jnp.stack(jnp.split(gf, nchip, axis=1))
    split_rows = lambda gf: gf.reshape(nchip, gf.shape[0] // nchip, gf.shape[1])
    gw_uq = gw_uq_p.reshape(Q_RANK, nh, QK_PAD)[:, :, :qk_head].reshape(Q_RANK, nh * qk_head)
    rest_g = [split_cols(gw_uq), gw_ukv, split_rows(gw_out)]
    du_in, rest_from_sib = _matmul([(dproj, w_in_p)], "nt", f32, "in_proj_bwd",
                                   comm=_sibling_send_halves(rest_g), at=(0.0,))
    rest_pairs = pair_sums(MIXER[1:], rest_g, rest_from_sib)
    gw_in_p, rest_from_chips = _matmul([(u, dproj)], "tn", f32, "grad_w_in",
                                       comm=_chips_exchange(rest_pairs), at=(0.0,))
    rest_halves = chip_sums(MIXER[1:], rest_g, rest_from_sib, rest_from_chips)

    def first_bwd_fn(xv, duv, dh1v, w):
        dxn, dw = _rms_bwd(xv, w, duv)
        return dh1v + dxn, dw

    grad_x, g_pre_mix = _rowwise(first_bwd_fn, t, tmw, [_rows(xs, tmw), _rows(du_in, tmw), _rows(dh1, tmw),
                                                         _par(w_pre_mix)],
                                 [_rout(t, d, f32, tmw)], [(1, d)], "pre_mix_norm_bwd")

    gseg = lambda a, b: gw_in_p[:, a:b]
    gw_in = jnp.concatenate([gseg(off_cq, off_ckv), gseg(off_ckv, off_z), gseg(off_tail, off_tail + ROPE),
                             gseg(off_z, off_tail), gseg(off_xbc, off_cq),
                             gseg(off_tail + ROPE, off_tail + ROPE + hs)], axis=1)
    in_g = [split_cols(gw_in)]
    in_from_sib = _comm_call(_sibling_send_halves(in_g), "sibling_send_halves")
    in_pairs = pair_sums(MIXER[:1], in_g, in_from_sib)
    in_from_chips = _comm_call(_chips_exchange(in_pairs), "chips_exchange")
    mix_halves = chip_sums(MIXER[:1], in_g, in_from_sib, in_from_chips) + rest_halves
    mix_sib_halves = _comm_call(_sibling_swap(mix_halves), "sibling_swap")
    halves = mix_halves + ffn_halves
    sib_halves = mix_sib_halves + ffn_sib_halves
    gshard = {}

    gsmall = {"q_norm_w": g_qn, "kv_norm_w": g_kvn, "conv_w": g_conv_w, "conv_b": g_conv_b, "dt_bias": g_dtb,
              "a_log": g_alog, "d_skip": g_dskip, "ssd_norm_w": g_ssd_n, "attn_out_norm_w": g_attn_n,
              "pre_mix_norm_w": g_pre_mix, "post_mix_norm_w": g_post_mix, "pre_ffn_norm_w": g_pre_ffn,
              "post_ffn_norm_w": g_post_ffn}
    small_sizes = [int(np.prod(gsmall[n].shape)) for n in SMALL]
    srows = _round_up(-(-sum(small_sizes) // LANES), 8)
    spart = _flat_pad([gsmall[n] for n in SMALL], srows * LANES).reshape(srows, LANES)
    sall = _gather_all(spart)

    def sum8_fn(a):
        tot = a[0]
        for k in range(1, 8):
            tot = tot + a[k]
        return tot

    ssum = _blocked(sum8_fn, (1,), [(sall, sall.shape, lambda i: (0, 0, 0))],
                    [((srows, LANES), f32, (srows, LANES), lambda i: (0, 0))], [], "small_grad_sum")[0].reshape(-1)
    gred = {}
    off = 0
    for n, sz in zip(SMALL, small_sizes):
        gred[n] = ssum[off:off + sz].reshape(gsmall[n].shape)
        off += sz
    gshard["conv_w"] = lax.dynamic_slice_in_dim(gred["conv_w"], my_chip * ccs, ccs, axis=1)
    for n in SMALL:
        if n != "conv_w":
            gshard[n] = gred[n].reshape(wts[n].shape)

    delta, new_m, new_v = {}, {}, {}
    big_out = {}
    for n, mine_h, sib_h in zip(BIG, halves, sib_halves):
        big_out[n] = _adamw_halves(local[n], local["m_" + n], local["v_" + n], mine_h, sib_h, sp, "adamw_" + n)
    pack = lambda src: _flat_pad([src[n] for n in SMALL], srows * LANES).reshape(srows, LANES)
    sd, sm, sv = _adamw_call(pack(wts), pack(gshard), pack(mom_m), pack(mom_v), "adamw_small")
    off = 0
    for n in SMALL:
        sz = int(np.prod(wts[n].shape))
        for dst, src in ((delta, sd), (new_m, sm), (new_v, sv)):
            dst[n] = src.reshape(-1)[off:off + sz].reshape(wts[n].shape)
        off += sz

    small_out = (gshard, delta, new_m, new_v)
    pick = lambda k: [big_out[n][k] if n in big_out else small_out[k][n][None] for n in ORDER]
    return (loss, grad_x[None], *pick(0), *pick(1), *pick(2), *pick(3))
```

```python
import functools

import numpy as np
import jax
import jax.numpy as jnp
from jax import lax
from jax.experimental import pallas as pl
from jax.experimental.pallas import tpu as pltpu

f32, bf16 = jnp.float32, jnp.bfloat16

EPS = 1e-6
V_HEAD = 128
NOPE = 128
ROPE = 64
QK_PAD = 256
Q_RANK = 512
KV_RANK = 512
ROPE_THETA = 10000.0
SSD_P = 64
SSD_G = 2
SSD_N = 128
SSD_K = 4
CHUNK = 128
ADAM_LR, ADAM_B1, ADAM_B2, ADAM_EPS, ADAM_WD, ADAM_STEP = 0.001, 0.9, 0.999, 1e-08, 0.01, 10

VMEM_LIMIT_BYTES = 48 * 1024 * 1024
LANES = 128
ATT_TILE = 1024
ATT_TILE_FWD = 1024
MM_TM, MM_TN, MM_TK = 1408, 1024, 1408
CHUNK_WHOLE_MAX = 1536
ROW_TILE = 256
ROW_BLOCK_BYTES = 2 * 1024 * 1024

NN = (((1,), (0,)), ((), ()))
NT = (((1,), (1,)), ((), ()))
TN = (((0,), (0,)), ((), ()))
MESH = pl.DeviceIdType.MESH
ANY = pl.BlockSpec(memory_space=pl.ANY)


def _tile(dim, cap, align=LANES):
    if dim <= cap:
        return dim
    t = (cap // align) * align
    while t >= align:
        if dim % t == 0:
            return t
        t -= align
    raise ValueError(f"no tile for {dim} under {cap}")


def _row_tile(rows, cols, align):
    best = None
    for tr in range(align, rows + 1, align):
        if rows % tr == 0 and tr * cols * 4 <= ROW_BLOCK_BYTES:
            best = tr
    return best or rows


def _round_up(n, m):
    return -(-n // m) * m


def _params(sem):
    return pltpu.CompilerParams(dimension_semantics=sem, vmem_limit_bytes=VMEM_LIMIT_BYTES)


def _dot(a, b, dims):
    return lax.dot_general(a.astype(bf16), b.astype(bf16), dims, preferred_element_type=f32)


def _call(body, name, out_shape, grid, in_specs, out_specs, scratch_shapes, sem, args, comm=None, at=None,
          prefetch=()):
    npf = len(prefetch)
    if comm is None:
        gs = pltpu.PrefetchScalarGridSpec(num_scalar_prefetch=npf, grid=grid, in_specs=list(in_specs),
                                          out_specs=list(out_specs), scratch_shapes=list(scratch_shapes))
        res = pl.pallas_call(body, name=name, out_shape=list(out_shape), grid_spec=gs,
                             compiler_params=_params(sem))(*prefetch, *args)
        return list(res), []
    n_in, n_out, n_sc = len(args), len(out_shape), len(scratch_shapes)
    na, no = len(comm.args), len(comm.outs)
    steps = int(np.prod(grid))

    def full(*allrefs):
        pf, refs = allrefs[:npf], allrefs[npf:]
        cin = refs[n_in:n_in + na]
        o0 = n_in + na
        cout = refs[o0 + n_out:o0 + n_out + no]
        s0 = o0 + n_out + no
        send_sems, recv_sems = refs[s0 + n_sc], refs[s0 + n_sc + 1]
        lin = pl.program_id(0)
        for dim in range(1, len(grid)):
            lin = lin * grid[dim] + pl.program_id(dim)
        for p in range(comm.nphase - 1):
            @pl.when(lin == int(round(at[p] * (steps - 1))))
            def _(p=p):
                comm.run(p, cin, cout, send_sems, recv_sems)
        body(*pf, *refs[:n_in], *refs[o0:o0 + n_out], *refs[s0:s0 + n_sc])

        @pl.when(lin == steps - 1)
        def _():
            comm.run(comm.nphase - 1, cin, cout, send_sems, recv_sems)

    gs = pltpu.PrefetchScalarGridSpec(
        num_scalar_prefetch=npf, grid=grid, in_specs=list(in_specs) + [ANY] * na,
        out_specs=list(out_specs) + [ANY] * no, scratch_shapes=list(scratch_shapes) + comm.sems())
    res = pl.pallas_call(full, name=name, out_shape=list(out_shape) + comm.outs, grid_spec=gs,
                         compiler_params=_params(("arbitrary",) * len(grid)))(*prefetch, *args, *comm.args)
    return list(res[:n_out]), list(res[n_out:])


def _chunk_tile(cs, cap):
    return cs if cs <= CHUNK_WHOLE_MAX else _tile(cs, cap)


def _matmul(pairs, mode, out_dtype, name, out_chunks=None, comm=None, at=None):
    a0, b0 = pairs[0]
    chunked = b0.ndim == 3
    cs = b0.shape[2] if chunked else None
    bcols = b0.shape[0] * b0.shape[2] if chunked else b0.shape[1]
    brows = b0.shape[1] if chunked else b0.shape[0]
    if mode == "nn":
        (m, k), n = a0.shape, bcols
    elif mode == "nt":
        (m, k), n = a0.shape, brows
    else:
        (k, m), n = a0.shape, bcols
    tm = _tile(m, MM_TM)
    if mode == "nt":
        tn = _tile(n, MM_TN)
        tk = _chunk_tile(cs, MM_TK) if chunked else _tile(k, MM_TK)
    else:
        tk = _tile(k, MM_TK)
        if chunked:
            tn = _chunk_tile(cs, MM_TN)
        elif out_chunks:
            tn = _chunk_tile(n // out_chunks, MM_TN)
        else:
            tn = _tile(n, MM_TN)
    nk = k // tk
    if mode == "nn":
        a_spec = pl.BlockSpec((tm, tk), lambda i, j, kk: (i, kk))
        if chunked:
            q = cs // tn
            b_spec = pl.BlockSpec((None, tk, tn), lambda i, j, kk: (j // q, kk, j % q))
        else:
            b_spec = pl.BlockSpec((tk, tn), lambda i, j, kk: (kk, j))
        dims = NN
    elif mode == "nt":
        a_spec = pl.BlockSpec((tm, tk), lambda i, j, kk: (i, kk))
        if chunked:
            q = cs // tk
            b_spec = pl.BlockSpec((None, tn, tk), lambda i, j, kk: (kk // q, j, kk % q))
        else:
            b_spec = pl.BlockSpec((tn, tk), lambda i, j, kk: (j, kk))
        dims = NT
    else:
        a_spec = pl.BlockSpec((tk, tm), lambda i, j, kk: (kk, i))
        b_spec = pl.BlockSpec((tk, tn), lambda i, j, kk: (kk, j))
        dims = TN
    if out_chunks:
        qo = (n // out_chunks) // tn
        out_shape = jax.ShapeDtypeStruct((out_chunks, m, n // out_chunks), out_dtype)
        o_spec = pl.BlockSpec((None, tm, tn), lambda i, j, kk: (j // qo, i, j % qo))
    else:
        out_shape = jax.ShapeDtypeStruct((m, n), out_dtype)
        o_spec = pl.BlockSpec((tm, tn), lambda i, j, kk: (i, j))
    npair = len(pairs)

    def body(*refs):
        o_ref, acc = refs[2 * npair], refs[2 * npair + 1]
        kk = pl.program_id(2)

        @pl.when(kk == 0)
        def _():
            acc[...] = jnp.zeros_like(acc)

        part = _dot(refs[0][...], refs[1][...], dims)
        for p in range(1, npair):
            part = part + _dot(refs[2 * p][...], refs[2 * p + 1][...], dims)
        acc[...] += part

        @pl.when(kk == nk - 1)
        def _():
            o_ref[...] = acc[...].astype(out_dtype)

    args = [t for pr in pairs for t in pr]
    res, cres = _call(body, name, [out_shape], (m // tm, n // tn, nk), [a_spec, b_spec] * npair, [o_spec],
                      [pltpu.VMEM((tm, tn), f32)], ("parallel", "parallel", "arbitrary"), args, comm, at)
    return res[0] if comm is None else (res[0], cres)


def _sigmoid(x):
    return 1.0 / (1.0 + jnp.exp(-x))


def _ffn_up(v, wg, wu):
    m, k = v.shape
    nchunk, _, cs = wg.shape
    n = nchunk * cs
    tm, tn = _tile(m, 512), _chunk_tile(cs, 512)
    q = cs // tn
    w_spec = pl.BlockSpec((None, k, tn), lambda j, i: (j // q, 0, j % q))

    def body(v_ref, wg_ref, wu_ref, g_ref, u_ref, act_ref):
        vb = v_ref[...]
        g = _dot(vb, wg_ref[...], NN)
        u = _dot(vb, wu_ref[...], NN)
        sg = _sigmoid(g)
        silu = g * sg
        g_ref[...] = silu.astype(bf16)
        u_ref[...] = (u * (sg * (1.0 + g * (1.0 - sg)))).astype(bf16)
        act_ref[...] = (silu * u).astype(bf16)

    out = jax.ShapeDtypeStruct((m, n), bf16)
    o_spec = pl.BlockSpec((tm, tn), lambda j, i: (i, j))
    return pl.pallas_call(
        body, name="ffn_up", out_shape=(out, out, out), grid=(n // tn, m // tm),
        in_specs=[pl.BlockSpec((tm, k), lambda j, i: (i, 0)), w_spec, w_spec],
        out_specs=(o_spec, o_spec, o_spec),
        compiler_params=_params(("parallel", "parallel")),
    )(v, wg, wu)


def _ffn_down_bwd(dffn, wd, dact_du, dact_dg):
    m, k = dffn.shape
    n = wd.shape[0]
    tm, tn = _tile(m, 1024), _tile(n, 512)

    def body(d_ref, w_ref, fu_ref, fg_ref, dg_ref, du_ref):
        dact = _dot(d_ref[...], w_ref[...], NT)
        du_ref[...] = (dact * fu_ref[...].astype(f32)).astype(bf16)
        dg_ref[...] = (dact * fg_ref[...].astype(f32)).astype(bf16)

    out = jax.ShapeDtypeStruct((m, n), bf16)
    o_spec = pl.BlockSpec((tm, tn), lambda i, j: (i, j))
    return pl.pallas_call(
        body, name="ffn_down_bwd", out_shape=(out, out), grid=(m // tm, n // tn),
        in_specs=[pl.BlockSpec((tm, k), lambda i, j: (i, 0)), pl.BlockSpec((tn, k), lambda i, j: (j, 0)),
                  o_spec, o_spec],
        out_specs=(o_spec, o_spec),
        compiler_params=_params(("parallel", "parallel")),
    )(dffn, wd, dact_du, dact_dg)


def _blocked(fn, grid, ins, outs, accs, name, sp=None):
    n_in, n_out, n_acc = len(ins), len(outs), len(accs)
    nsp = 0 if sp is None else 1

    def body(*refs):
        refs = refs[nsp:]
        tiles = [r[...] for r in refs[:n_in]]
        res = fn(*tiles)
        if not isinstance(res, (tuple, list)):
            res = (res,)
        for r, val in zip(refs[n_in:n_in + n_out], res[:n_out]):
            r[...] = val.astype(r.dtype)
        if n_acc:
            first = pl.program_id(0) == 0
            for d in range(1, len(grid)):
                first = jnp.logical_and(first, pl.program_id(d) == 0)

            @pl.when(first)
            def _():
                for r in refs[n_in + n_out:]:
                    r[...] = jnp.zeros_like(r)

            for r, val in zip(refs[n_in + n_out:], res[n_out:]):
                r[...] += val

    def acc_map(shape):
        zeros = (0,) * len(shape)
        return lambda *idx: zeros

    in_specs = [pl.BlockSpec(bs, im) for _, bs, im in ins]
    out_specs = [pl.BlockSpec(bs, im) for _, _, bs, im in outs] + [pl.BlockSpec(s, acc_map(s)) for s in accs]
    out_shape = [jax.ShapeDtypeStruct(s, d) for s, d, _, _ in outs] + [jax.ShapeDtypeStruct(s, f32) for s in accs]
    sem = ("arbitrary",) * len(grid) if n_acc else ("parallel",) * len(grid)
    args = [a for a, _, _ in ins]
    if sp is None:
        res = pl.pallas_call(body, name=name, out_shape=out_shape, grid=grid, in_specs=in_specs,
                             out_specs=out_specs, compiler_params=_params(sem))(*args)
    else:
        gs = pltpu.PrefetchScalarGridSpec(num_scalar_prefetch=1, grid=grid, in_specs=in_specs, out_specs=out_specs)
        res = pl.pallas_call(body, name=name, out_shape=out_shape, grid_spec=gs,
                             compiler_params=_params(sem))(sp, *args)
    return res


def _rows(a, tm, cols=None, cb=0):
    w = a.shape[1] if cols is None else cols
    return (a, (tm, w), lambda i: (i, cb))


def _par(a):
    zeros = (0,) * a.ndim
    return (a, a.shape, lambda i: zeros)


def _rout(t, w, dtype, tm):
    return ((t, w), dtype, (tm, w), lambda i: (i, 0))


def _rowwise(fn, t, tm, ins, outs, accs, name):
    return _blocked(fn, (t // tm,), ins, outs, accs, name)


def _rms(x, w):
    r = lax.rsqrt(jnp.mean(x * x, axis=-1, keepdims=True) + EPS)
    return x * r * w


def _rms_bwd(x, w, dy):
    r = lax.rsqrt(jnp.mean(x * x, axis=-1, keepdims=True) + EPS)
    xh = x * r
    dyw = dy * w
    dx = r * (dyw - xh * jnp.mean(dyw * xh, axis=-1, keepdims=True))
    return dx, jnp.sum(dy * xh, axis=0, keepdims=True)


def _silu_grad(x):
    s = _sigmoid(x)
    return s * (1.0 + x * (1.0 - s))


def _rope(blk, cosp, sina, sinb):
    return blk * cosp + pltpu.roll(blk, 96, 1) * sina + pltpu.roll(blk, 32, 1) * sinb


def _rope_bwd(dy, cosp, sina, sinb):
    return dy * cosp + pltpu.roll(dy * sina, 32, 1) + pltpu.roll(dy * sinb, 96, 1)


HALO = 8


def _conv_taps(buf, w, tm, base):
    acc = buf[base:base + tm, :] * w[0:1]
    for k in range(1, SSD_K):
        acc = acc + buf[base + k:base + k + tm, :] * w[k:k + 1]
    return acc


def _conv_specs(t, tm, cdim):
    cur = pl.BlockSpec((tm, cdim), lambda i: (i, 0))
    prev = pl.BlockSpec((HALO, cdim), lambda i: (jnp.maximum(i * (tm // HALO) - 1, 0), 0))
    nxt = pl.BlockSpec((HALO, cdim), lambda i: (jnp.minimum((i + 1) * (tm // HALO), t // HALO - 1), 0))
    return cur, prev, nxt


def _conv_fwd(src, w, b, cdim, tm):
    t = src.shape[0]
    cur, prev, _ = _conv_specs(t, tm, cdim)

    def body(x_ref, p_ref, w_ref, b_ref, o_ref, buf):
        buf[0:HALO, :] = jnp.where(pl.program_id(0) > 0, p_ref[...], 0.0)
        buf[HALO:HALO + tm, :] = x_ref[...]
        pre = _conv_taps(buf, w_ref[...], tm, HALO - (SSD_K - 1)) + b_ref[...]
        o_ref[...] = pre * _sigmoid(pre)

    par = lambda a: pl.BlockSpec(a.shape, lambda i: (0, 0))
    return pl.pallas_call(
        body, name="conv_silu", out_shape=jax.ShapeDtypeStruct((t, cdim), f32), grid=(t // tm,),
        in_specs=[cur, prev, par(w), par(b)], out_specs=cur,
        scratch_shapes=[pltpu.VMEM((tm + HALO, cdim), f32)], compiler_params=_params(("parallel",)),
    )(src, src, w, b)


def _conv_bwd_pre(src, w, b, dact, cdim, tm):
    t = src.shape[0]
    cur, prev, _ = _conv_specs(t, tm, cdim)

    def body(x_ref, p_ref, w_ref, b_ref, d_ref, dpre_ref, dw0, dw1, dw2, dw3, db, buf):
        @pl.when(pl.program_id(0) == 0)
        def _():
            for r in (dw0, dw1, dw2, dw3, db):
                r[...] = jnp.zeros_like(r)

        buf[0:HALO, :] = jnp.where(pl.program_id(0) > 0, p_ref[...], 0.0)
        buf[HALO:HALO + tm, :] = x_ref[...]
        base = HALO - (SSD_K - 1)
        pre = _conv_taps(buf, w_ref[...], tm, base) + b_ref[...]
        dpre = d_ref[...] * _silu_grad(pre)
        dpre_ref[...] = dpre
        for k, r in enumerate((dw0, dw1, dw2, dw3)):
            r[...] += jnp.sum(dpre * buf[base + k:base + k + tm, :], axis=0, keepdims=True)
        db[...] += jnp.sum(dpre, axis=0, keepdims=True)

    par = lambda a: pl.BlockSpec(a.shape, lambda i: (0, 0))
    acc = pl.BlockSpec((1, cdim), lambda i: (0, 0))
    acc_shape = jax.ShapeDtypeStruct((1, cdim), f32)
    return pl.pallas_call(
        body, name="conv_silu_bwd", out_shape=[jax.ShapeDtypeStruct((t, cdim), f32)] + [acc_shape] * 5,
        grid=(t // tm,), in_specs=[cur, prev, par(w), par(b), cur], out_specs=[cur] + [acc] * 5,
        scratch_shapes=[pltpu.VMEM((tm + HALO, cdim), f32)], compiler_params=_params(("arbitrary",)),
    )(src, src, w, b, dact)


def _conv_bwd_dx(dpre, w, tm):
    t, cdim = dpre.shape
    cur, _, nxt = _conv_specs(t, tm, cdim)
    last = t // tm - 1

    def body(d_ref, n_ref, w_ref, o_ref, buf):
        buf[0:tm, :] = d_ref[...]
        buf[tm:tm + HALO, :] = jnp.where(pl.program_id(0) < last, n_ref[...], 0.0)
        wv = w_ref[...]
        acc = buf[0:tm, :] * wv[SSD_K - 1:SSD_K]
        for k in range(SSD_K - 1):
            s = SSD_K - 1 - k
            acc = acc + buf[s:s + tm, :] * wv[k:k + 1]
        o_ref[...] = acc.astype(bf16)

    return pl.pallas_call(
        body, name="conv_bwd_dx", out_shape=jax.ShapeDtypeStruct((t, cdim), bf16), grid=(t // tm,),
        in_specs=[cur, nxt, pl.BlockSpec(w.shape, lambda i: (0, 0))], out_specs=cur,
        scratch_shapes=[pltpu.VMEM((tm + HALO, cdim), f32)], compiler_params=_params(("parallel",)),
    )(dpre, dpre, w)


def _causal_mask(s):
    row = lax.broadcasted_iota(jnp.int32, s.shape, 0)
    col = lax.broadcasted_iota(jnp.int32, s.shape, 1)
    return jnp.where(row >= col, s, -jnp.inf)


def _causal_pairs(nq, q_major):
    pairs = [(qi, ki) for qi in range(nq) for ki in range(qi + 1)]
    if not q_major:
        pairs.sort(key=lambda p: (p[1], p[0]))
    return (jnp.asarray([p[0] for p in pairs], jnp.int32), jnp.asarray([p[1] for p in pairs], jnp.int32))


def _flash_fwd(q, kv, kr, nheads, scale, comm=None, at=None):
    t = q.shape[0]
    tq = _tile(t, ATT_TILE_FWD)
    nq = t // tq

    qtab, ktab = _causal_pairs(nq, q_major=True)
    hp = 2 if nheads % 2 == 0 else 1

    def body(qt, kt, q_ref, kv_ref, kr_ref, o_ref, lse_ref, m_sc, l_sc, acc_sc):
        qi, ki = qt[pl.program_id(1)], kt[pl.program_id(1)]

        @pl.when(ki == 0)
        def _():
            m_sc[...] = jnp.full_like(m_sc, -jnp.inf)
            l_sc[...] = jnp.zeros_like(l_sc)
            acc_sc[...] = jnp.zeros_like(acc_sc)

        def step(masked, last):
            krb = kr_ref[...]
            for j in range(hp):
                kn = kv_ref[:, j * QK_PAD: j * QK_PAD + NOPE]
                v = kv_ref[:, j * QK_PAD + NOPE:(j + 1) * QK_PAD]
                k = jnp.concatenate([kn, krb], axis=1)
                s = lax.dot_general(q_ref[:, j * QK_PAD:(j + 1) * QK_PAD], k, NT, preferred_element_type=f32) * scale
                if masked:
                    s = _causal_mask(s)
                m_old = m_sc[j]
                m_new = jnp.maximum(m_old, jnp.max(s, axis=1, keepdims=True))
                alpha = jnp.exp(m_old - m_new)
                p = jnp.exp(s - m_new)
                l = alpha * l_sc[j] + jnp.sum(p, axis=1, keepdims=True)
                acc = alpha * acc_sc[j] + lax.dot_general(p.astype(bf16), v, NN, preferred_element_type=f32)
                if last:
                    o_ref[:, j * V_HEAD:(j + 1) * V_HEAD] = acc / l
                    lse_ref[:, j * V_HEAD:(j + 1) * V_HEAD] = jnp.broadcast_to(m_new + jnp.log(l), (tq, V_HEAD))
                else:
                    l_sc[j] = l
                    acc_sc[j] = acc
                    m_sc[j] = m_new

        @pl.when(ki < qi)
        def _():
            step(False, False)

        @pl.when(ki == qi)
        def _():
            step(True, True)

    o_spec = pl.BlockSpec((tq, hp * V_HEAD), lambda h, s, qt, kt: (qt[s], h))
    out = jax.ShapeDtypeStruct((t, nheads * V_HEAD), f32)
    (o, lse), cres = _call(
        body, "flash_fwd", [out, out], (nheads // hp, len(qtab)),
        [pl.BlockSpec((tq, hp * QK_PAD), lambda h, s, qt, kt: (qt[s], h)),
         pl.BlockSpec((tq, hp * QK_PAD), lambda h, s, qt, kt: (kt[s], h)),
         pl.BlockSpec((tq, LANES), lambda h, s, qt, kt: (kt[s], 0))],
        [o_spec, o_spec],
        [pltpu.VMEM((hp, tq, 1), f32), pltpu.VMEM((hp, tq, 1), f32), pltpu.VMEM((hp, tq, V_HEAD), f32)],
        ("parallel", "arbitrary"), [q, kv, kr], comm, at, prefetch=(qtab, ktab))
    return o, lse, cres


def _flash_bwd(q, kv, kr, do, lse, delta, nheads, scale, comm=None, at=None):
    t = q.shape[0]
    tq = _tile(t, ATT_TILE)
    nq = t // tq
    qtab, ktab = _causal_pairs(nq, q_major=False)

    def body(qt, kt, q_ref, kn_ref, kr_ref, v_ref, do_ref, lse_ref, dl_ref, dkv_ref, dkr_ref, dq_ref, dk_sc, dv_sc):
        qi, ki = qt[pl.program_id(1)], kt[pl.program_id(1)]

        @pl.when(pl.program_id(1) == 0)
        def _():
            dq_ref[...] = jnp.zeros_like(dq_ref)

        @pl.when(qi == ki)
        def _():
            dk_sc[...] = jnp.zeros_like(dk_sc)
            dv_sc[...] = jnp.zeros_like(dv_sc)

        def step(masked):
            qb = q_ref[...]
            dob = do_ref[...]
            k = jnp.concatenate([kn_ref[...], kr_ref[...]], axis=1)
            s = lax.dot_general(qb, k, NT, preferred_element_type=f32) * scale
            if masked:
                s = _causal_mask(s)
            p = jnp.exp(s - lse_ref[:, 0:1])
            dv_sc[...] += lax.dot_general(p.astype(bf16), dob, TN, preferred_element_type=f32)
            dp = lax.dot_general(dob, v_ref[...], NT, preferred_element_type=f32)
            ds = (p * (dp - dl_ref[:, 0:1]) * scale).astype(bf16)
            dk_sc[...] += lax.dot_general(ds, qb, TN, preferred_element_type=f32)
            rows = pl.ds(pl.multiple_of(qi * tq, tq), tq)
            dq_ref[rows, :] += lax.dot_general(ds, k, NN, preferred_element_type=f32)

        @pl.when(qi > ki)
        def _():
            step(False)

        @pl.when(qi == ki)
        def _():
            step(True)

        @pl.when(qi == nq - 1)
        def _():
            dk = dk_sc[...]
            dkv_ref[...] = jnp.concatenate([dk[:, :NOPE], dv_sc[...]], axis=1).astype(bf16)
            dkr_ref[...] = dk[:, NOPE:]

    hspec = pl.BlockSpec((tq, V_HEAD), lambda h, s, qt, kt: (qt[s], h))
    (dkv, dkr, dq), cres = _call(
        body, "flash_bwd",
        [jax.ShapeDtypeStruct((t, nheads * QK_PAD), bf16), jax.ShapeDtypeStruct((t, nheads * LANES), f32),
         jax.ShapeDtypeStruct((t, nheads * QK_PAD), f32)],
        (nheads, len(qtab)),
        [pl.BlockSpec((tq, QK_PAD), lambda h, s, qt, kt: (qt[s], h)),
         pl.BlockSpec((tq, NOPE), lambda h, s, qt, kt: (kt[s], 2 * h)),
         pl.BlockSpec((tq, LANES), lambda h, s, qt, kt: (kt[s], 0)),
         pl.BlockSpec((tq, V_HEAD), lambda h, s, qt, kt: (kt[s], 2 * h + 1)),
         hspec, hspec, hspec],
        [pl.BlockSpec((tq, QK_PAD), lambda h, s, qt, kt: (kt[s], h)),
         pl.BlockSpec((tq, LANES), lambda h, s, qt, kt: (kt[s], h)),
         pl.BlockSpec((t, QK_PAD), lambda h, s, qt, kt: (0, h))],
        [pltpu.VMEM((tq, QK_PAD), f32), pltpu.VMEM((tq, V_HEAD), f32)],
        ("parallel", "arbitrary"), [q, kv, kr, kv, do, lse, delta], comm, at, prefetch=(qtab, ktab))
    return dkv, dkr, dq, cres


def _split3(a):
    hi = a.astype(bf16)
    r1 = a - hi.astype(f32)
    mid = r1.astype(bf16)
    lo = (r1 - mid.astype(f32)).astype(bf16)
    return hi, mid, lo


def _split2(a):
    hi = a.astype(bf16)
    return hi, (a - hi.astype(f32)).astype(bf16)


def _ones_dot_left(tri, a):
    hi, mid, lo = _split3(a)
    d = lambda v: lax.dot_general(tri, v, NN, preferred_element_type=f32)
    return d(hi) + d(mid) + d(lo)


def _ones_dot_right(a, tri):
    hi, mid, lo = _split3(a)
    d = lambda v: lax.dot_general(v, tri, NN, preferred_element_type=f32)
    return d(hi) + d(mid) + d(lo)


def _softplus(x):
    return jnp.maximum(x, 0.0) + jnp.log(1.0 + jnp.exp(-jnp.abs(x)))


def _head_spread(hs, width):
    shift = SSD_P.bit_length() - 1
    return (lax.broadcasted_iota(jnp.int32, (hs, width), 0)
            == lax.shift_right_logical(lax.broadcasted_iota(jnp.int32, (hs, width), 1), shift)).astype(bf16)


def _ssd_common(dt_ref, dtT_ref, dtb_ref, dtbT_ref, alog_ref, alogT_ref):
    ii = lax.broadcasted_iota(jnp.int32, (CHUNK, CHUNK), 0)
    jj = lax.broadcasted_iota(jnp.int32, (CHUNK, CHUNK), 1)
    tri = ii >= jj
    raw = dt_ref[...] + dtb_ref[...]
    dt = _softplus(raw)
    a_neg = -jnp.exp(alog_ref[...])
    cum = _ones_dot_left(tri.astype(bf16), dt * a_neg)
    dt_t = _softplus(dtT_ref[...] + dtbT_ref[...])
    cum_t = _ones_dot_right(dt_t * (-jnp.exp(alogT_ref[...])), (ii <= jj).astype(bf16))
    return tri, raw, dt, a_neg, cum, cum_t


def _ssd_fwd(xbc, dt, dt_t, dtb, dtb_t, alog, alog_t, dskip, width):
    t, cdim = xbc.shape
    hs = dt.shape[1]
    nc = t // CHUNK
    epg = hs // SSD_G
    gn = SSD_G * SSD_N

    def body(x_ref, dt_ref, dtT_ref, dtb_ref, dtbT_ref, alog_ref, alogT_ref, d_ref, y_ref, hp_ref, h_sc, yd_sc):
        @pl.when(pl.program_id(0) == 0)
        def _():
            h_sc[...] = jnp.zeros_like(h_sc)

        tri, _, dtv, _, cum, cum_t = _ssd_common(dt_ref, dtT_ref, dtb_ref, dtbT_ref, alog_ref, alogT_ref)
        spread = _head_spread(hs, width)
        clast = cum[CHUNK - 1:CHUNK, :]
        dec = jnp.exp(clast)
        wide = _ones_dot_right(jnp.concatenate([dtv, jnp.exp(cum), jnp.exp(clast - cum),
                                                jnp.broadcast_to(d_ref[...], (CHUNK, hs))], axis=0), spread)
        dt_x, ee_x, ff_x, dsk_x = (wide[i * CHUNK:(i + 1) * CHUNK] for i in range(4))
        xs = x_ref[:, :width]
        xdt = xs * dt_x
        xf = xdt * ff_x
        h_all = h_sc[...]
        hp_ref[0] = h_all
        h_all = h_all.reshape(hs * SSD_P, SSD_N)
        ch_parts = []
        for g in range(SSD_G):
            gsl = slice(g * epg * SSD_P, (g + 1) * epg * SSD_P)
            bb = x_ref[:, width + g * SSD_N: width + (g + 1) * SSD_N].astype(bf16)
            cb_ = x_ref[:, width + gn + g * SSD_N: width + gn + (g + 1) * SSD_N].astype(bf16)
            cbm = lax.dot_general(cb_, bb, NT, preferred_element_type=f32)
            ch_parts.append(_dot(cb_, h_all[gsl], NT))
            st = _dot(xf[:, gsl], bb, TN)
            for e in range(g * epg, (g + 1) * epg):
                esl = slice(e * SSD_P, (e + 1) * SSD_P)
                lmat = jnp.exp(jnp.where(tri, cum[:, e:e + 1] - cum_t[e:e + 1, :], -jnp.inf))
                yd_sc[:, esl] = _dot(cbm * lmat, xdt[:, esl], NN)
                j = e - g * epg
                h_sc[e] = h_sc[e] * dec[:, e:e + 1] + st[j * SSD_P:(j + 1) * SSD_P, :]
        y_ref[...] = yd_sc[...] + jnp.concatenate(ch_parts, axis=1) * ee_x + xs * dsk_x

    par = lambda a: pl.BlockSpec(a.shape, lambda i: (0,) * a.ndim)
    return pl.pallas_call(
        body, name="ssd_fwd",
        out_shape=(jax.ShapeDtypeStruct((t, width), f32), jax.ShapeDtypeStruct((nc, hs, SSD_P, SSD_N), f32)),
        grid=(nc,),
        in_specs=[pl.BlockSpec((CHUNK, cdim), lambda i: (i, 0)), pl.BlockSpec((CHUNK, hs), lambda i: (i, 0)),
                  pl.BlockSpec((hs, CHUNK), lambda i: (0, i)), par(dtb), par(dtb_t), par(alog), par(alog_t), par(dskip)],
        out_specs=(pl.BlockSpec((CHUNK, width), lambda i: (i, 0)),
                   pl.BlockSpec((1, hs, SSD_P, SSD_N), lambda i: (i, 0, 0, 0))),
        scratch_shapes=[pltpu.VMEM((hs, SSD_P, SSD_N), f32), pltpu.VMEM((CHUNK, width), f32)],
        compiler_params=_params(("arbitrary",)),
    )(xbc, dt, dt_t, dtb, dtb_t, alog, alog_t, dskip)


def _ssd_bwd(xbc, dt, dt_t, dtb, dtb_t, alog, alog_t, dskip, hprev, dy, width, comm=None, at=None):
    t, cdim = xbc.shape
    hs = dt.shape[1]
    nc = t // CHUNK
    epg = hs // SSD_G
    gn = SSD_G * SSD_N

    def body(x_ref, dt_ref, dtT_ref, dtb_ref, dtbT_ref, alog_ref, alogT_ref, d_ref, hp_ref, dy_ref,
             dx_ref, ddt_ref, dalog_ref, ddsk_ref, ddtb_ref, dh_sc, dxd_sc):
        @pl.when(pl.program_id(0) == 0)
        def _():
            dh_sc[...] = jnp.zeros_like(dh_sc)
            dalog_ref[...] = jnp.zeros_like(dalog_ref)
            ddsk_ref[...] = jnp.zeros_like(ddsk_ref)
            ddtb_ref[...] = jnp.zeros_like(ddtb_ref)

        tri, raw, dtv, a_neg, cum, cum_t = _ssd_common(dt_ref, dtT_ref, dtb_ref, dtbT_ref, alog_ref, alogT_ref)
        dsk = d_ref[...]
        head_row = lax.broadcasted_iota(jnp.int32, (1, hs), 1)
        head_col = lax.broadcasted_iota(jnp.int32, (hs, 1), 0)
        last_row = (lax.broadcasted_iota(jnp.int32, (CHUNK, 1), 0) == CHUNK - 1).astype(f32)
        shift = SSD_P.bit_length() - 1
        spread = _head_spread(hs, width)
        gather = (lax.shift_right_logical(lax.broadcasted_iota(jnp.int32, (width, hs), 0), shift)
                  == lax.broadcasted_iota(jnp.int32, (width, hs), 1)).astype(bf16)
        clast = cum[CHUNK - 1:CHUNK, :]
        ee = jnp.exp(cum)
        ff = jnp.exp(clast - cum)
        dec = jnp.exp(clast)
        wide = _ones_dot_right(jnp.concatenate([dtv, ee, ff, jnp.broadcast_to(dsk, (CHUNK, hs))], axis=0), spread)
        dt_x, ee_x, ff_x, dsk_x = (wide[i * CHUNK:(i + 1) * CHUNK] for i in range(4))
        xs = x_ref[:, :width]
        dyv = dy_ref[...]
        xdt = xs * dt_x
        dye = dyv * ee_x
        xf = xdt * ff_x
        h_all = hp_ref[0].reshape(hs * SSD_P, SSD_N)
        dh_all = dh_sc[...].reshape(hs * SSD_P, SSD_N)
        hi, lo = _split2(_ones_dot_left(spread, dh_all * h_all))
        ones8 = jnp.ones((8, SSD_N), bf16)
        hh = (lax.dot_general(ones8, hi, NT, preferred_element_type=f32)
              + lax.dot_general(ones8, lo, NT, preferred_element_type=f32))[0:1]
        rowsum_m = jnp.zeros((CHUNK, hs), f32)
        colsum_m = jnp.zeros((hs, CHUNK), f32)
        ch_parts, bds_parts = [], []
        for g in range(SSD_G):
            bsl = slice(width + g * SSD_N, width + (g + 1) * SSD_N)
            csl = slice(width + gn + g * SSD_N, width + gn + (g + 1) * SSD_N)
            gsl = slice(g * epg * SSD_P, (g + 1) * epg * SSD_P)
            bb = x_ref[:, bsl].astype(bf16)
            cb_ = x_ref[:, csl].astype(bf16)
            cbm = lax.dot_general(cb_, bb, NT, preferred_element_type=f32)
            hg = h_all[gsl].astype(bf16)
            dhg = dh_all[gsl].astype(bf16)
            ch_parts.append(_dot(cb_, hg, NT))
            bds_parts.append(_dot(bb, dhg, NT))
            dcg = _dot(dye[:, gsl], hg, NN)
            dbg = _dot(xf[:, gsl], dhg, NN)
            dh_new = _dot(dye[:, gsl], cb_, TN)
            dcb = jnp.zeros((CHUNK, CHUNK), f32)
            for e in range(g * epg, (g + 1) * epg):
                esl = slice(e * SSD_P, (e + 1) * SSD_P)
                lmat = jnp.exp(jnp.where(tri, cum[:, e:e + 1] - cum_t[e:e + 1, :], -jnp.inf))
                gmat = cbm * lmat
                dy_e = dyv[:, esl].astype(bf16)
                dgm = _dot(dy_e, xdt[:, esl], NT)
                dxd_sc[:, esl] = _dot(gmat, dy_e, TN)
                dcb = dcb + dgm * lmat
                mm = dgm * gmat
                rowsum_m = rowsum_m + jnp.sum(mm, axis=1, keepdims=True) * (head_row == e).astype(f32)
                colsum_m = colsum_m + (head_col == e).astype(f32) * jnp.sum(mm, axis=0, keepdims=True)
                j = e - g * epg
                dh_sc[e] = dh_new[j * SSD_P:(j + 1) * SSD_P, :] + dec[:, e:e + 1] * dh_sc[e]
            dx_ref[:, bsl] = dbg + _dot(dcb, cb_, TN)
            dx_ref[:, csl] = dcg + _dot(dcb, bb, NN)
        ch_all = jnp.concatenate(ch_parts, axis=1)
        bds_all = jnp.concatenate(bds_parts, axis=1)
        dxdt = bds_all * ff_x + dxd_sc[...]
        dx_ref[:, :width] = dxdt * dt_x + dyv * dsk_x
        sums = _ones_dot_right(jnp.concatenate([dxdt * xs, dyv * ch_all, bds_all * xdt, dyv * xs], axis=0), gather)
        ddtx_all = sums[0:CHUNK]
        dff = sums[2 * CHUNK:3 * CHUNK] * ff
        ddsk = jnp.sum(sums[3 * CHUNK:4 * CHUNK], axis=0, keepdims=True)
        dclast = jnp.sum(dff, axis=0, keepdims=True) + dec * hh
        eye = (lax.broadcasted_iota(jnp.int32, (hs, hs), 0) == lax.broadcasted_iota(jnp.int32, (hs, hs), 1)).astype(bf16)
        colsum_t = sum(lax.dot_general(v, eye, TN, preferred_element_type=f32) for v in _split3(colsum_m))
        dcum_all = sums[CHUNK:2 * CHUNK] * ee - dff + rowsum_m - colsum_t + dclast * last_row
        ii = lax.broadcasted_iota(jnp.int32, (CHUNK, CHUNK), 0)
        jj = lax.broadcasted_iota(jnp.int32, (CHUNK, CHUNK), 1)
        da = _ones_dot_left((jj >= ii).astype(bf16), dcum_all)
        ddt = da * a_neg + ddtx_all
        dalog_ref[...] += jnp.sum(da * dtv, axis=0, keepdims=True) * a_neg
        draw = ddt * _sigmoid(raw)
        place = (lax.broadcasted_iota(jnp.int32, (hs, LANES), 0) + ROPE
                 == lax.broadcasted_iota(jnp.int32, (hs, LANES), 1)).astype(bf16)
        ddt_ref[...] = _ones_dot_right(draw, place)
        ddtb_ref[...] += jnp.sum(draw, axis=0, keepdims=True)
        ddsk_ref[...] += ddsk

    rev = lambda i: nc - 1 - i
    par = lambda a: pl.BlockSpec(a.shape, lambda i: (0,) * a.ndim)
    acc = pl.BlockSpec((1, hs), lambda i: (0, 0))
    acc_shape = jax.ShapeDtypeStruct((1, hs), f32)
    res, cres = _call(
        body, "ssd_bwd",
        [jax.ShapeDtypeStruct((t, cdim), f32), jax.ShapeDtypeStruct((t, LANES), f32), acc_shape, acc_shape, acc_shape],
        (nc,),
        [pl.BlockSpec((CHUNK, cdim), lambda i: (rev(i), 0)), pl.BlockSpec((CHUNK, hs), lambda i: (rev(i), 0)),
         pl.BlockSpec((hs, CHUNK), lambda i: (0, rev(i))), par(dtb), par(dtb_t), par(alog), par(alog_t),
         par(dskip), pl.BlockSpec((1, hs, SSD_P, SSD_N), lambda i: (rev(i), 0, 0, 0)),
         pl.BlockSpec((CHUNK, width), lambda i: (rev(i), 0))],
        [pl.BlockSpec((CHUNK, cdim), lambda i: (rev(i), 0)), pl.BlockSpec((CHUNK, LANES), lambda i: (rev(i), 0)),
         acc, acc, acc],
        [pltpu.VMEM((hs, SSD_P, SSD_N), f32), pltpu.VMEM((CHUNK, width), f32)], ("arbitrary",),
        [xbc, dt, dt_t, dtb, dtb_t, alog, alog_t, dskip, hprev, dy], comm, at)
    return (*res, cres)


def _where_am_i():
    x, y, c = lax.axis_index("x"), lax.axis_index("y"), lax.axis_index("c")
    chips = [(1 - x, y), (x, 1 - y), (1 - x, 1 - y)]
    return x, y, c, chips


def _remote(src, dst, send_sems, recv_sems, k, to):
    return pltpu.make_async_remote_copy(src_ref=src, dst_ref=dst, send_sem=send_sems.at[k], recv_sem=recv_sems.at[k],
                                        device_id=to, device_id_type=MESH)


class _Comm:
    def __init__(self, args, outs, nsem, nphase, run):
        self.args, self.outs, self.nsem, self.nphase, self.run = list(args), list(outs), nsem, nphase, run

    def sems(self):
        return [pltpu.SemaphoreType.DMA((self.nsem,)), pltpu.SemaphoreType.DMA((self.nsem,))]


def _comm_call(comm, name):
    na, no = len(comm.args), len(comm.outs)

    def body(*refs):
        for phase in range(comm.nphase):
            comm.run(phase, refs[:na], refs[na:na + no], refs[na + no], refs[na + no + 1])

    return list(pl.pallas_call(body, name=name, out_shape=comm.outs, in_specs=[ANY] * na, out_specs=[ANY] * no,
                               scratch_shapes=comm.sems())(*comm.args))


def _half(ref, c, r2):
    return ref.at[pl.ds(c * r2, r2)]


def _gather_weights(shards, wholes):
    ns, nw = len(shards), len(wholes)
    per = 7

    def run(phase, srcs, outs, send_sems, recv_sems):
        x, y, c, chips = _where_am_i()
        me, sib = 2 * x + y, (x, y, 1 - c)
        def first():
            cps = []
            for w in range(ns + nw):
                src, out, base = srcs[w], outs[w], per * w
                halved = w < ns
                r2 = src.shape[0] // 2
                piece = _half(src, c, r2) if halved else src
                for j, (cx, cy) in enumerate(chips):
                    dst = _half(out.at[me], c, r2) if halved else out.at[me]
                    cps.append(_remote(piece, dst, send_sems, recv_sems, base + j, (cx, cy, c)))
                cps.append(_remote(src, out.at[me], send_sems, recv_sems, base + 6, sib))
            return cps

        def passed():
            cps = []
            for w in range(ns):
                r2 = srcs[w].shape[0] // 2
                for j, (cx, cy) in enumerate(chips):
                    got = _half(outs[w].at[2 * cx + cy], c, r2)
                    cps.append(_remote(got, got, send_sems, recv_sems, per * w + 3 + j, sib))
            return cps

        if phase == 0:
            for cp in first():
                cp.start()
        elif phase == 1:
            it = iter(passed())
            for w in range(ns + nw):
                src, out, base = srcs[w], outs[w], per * w
                r2 = src.shape[0] // 2
                for j, (cx, cy) in enumerate(chips):
                    got = _half(out.at[2 * cx + cy], c, r2) if w < ns else out.at[2 * cx + cy]
                    _remote(got, got, send_sems, recv_sems, base + j, sib).wait_recv()
                    if w < ns:
                        next(it).start()
        else:
            for w in range(ns + nw):
                src, out, base = srcs[w], outs[w], per * w
                r2 = src.shape[0] // 2
                if w < ns:
                    for j, (cx, cy) in enumerate(chips):
                        got = _half(out.at[2 * cx + cy], 1 - c, r2)
                        _remote(got, got, send_sems, recv_sems, base + 3 + j, sib).wait_recv()
                _remote(src, out.at[me], send_sems, recv_sems, base + 6, sib).wait_recv()
            for cp in first() + passed():
                cp.wait_send()

    args = list(shards) + list(wholes)
    outs = [jax.ShapeDtypeStruct((4,) + a.shape, a.dtype) for a in args]
    return _Comm(args, outs, per * len(args), 3, run)


def _sibling_send_halves(gs):
    n = len(gs)

    def run(phase, srcs, outs, send_sems, recv_sems):
        x, y, c, _ = _where_am_i()
        sib = (x, y, 1 - c)
        cps = []
        for w in range(n):
            r2 = srcs[w].shape[1] // 2
            for k in range(4):
                cps.append(_remote(_half(srcs[w].at[k], 1 - c, r2), outs[w].at[k], send_sems, recv_sems, 4 * w + k, sib))
        for cp in cps:
            cp.start() if phase == 0 else cp.wait()

    outs = [jax.ShapeDtypeStruct((4, g.shape[1] // 2, g.shape[2]), g.dtype) for g in gs]
    return _Comm(gs, outs, 4 * n, 2, run)


def _chips_exchange(ps):
    n = len(ps)

    def run(phase, srcs, outs, send_sems, recv_sems):
        x, y, c, chips = _where_am_i()
        me = 2 * x + y
        cps = []
        for w in range(n):
            for j, (cx, cy) in enumerate(chips):
                cps.append(_remote(srcs[w].at[2 * cx + cy], outs[w].at[me], send_sems, recv_sems, 3 * w + j, (cx, cy, c)))
        if phase == 0:
            for cp in cps:
                cp.start()
        else:
            for w in range(n):
                for j, (cx, cy) in enumerate(chips):
                    got = outs[w].at[2 * cx + cy]
                    _remote(got, got, send_sems, recv_sems, 3 * w + j, (cx, cy, c)).wait_recv()
            for cp in cps:
                cp.wait_send()

    outs = [jax.ShapeDtypeStruct(p.shape, p.dtype) for p in ps]
    return _Comm(ps, outs, 3 * n, 2, run)


def _sibling_swap(rs):
    n = len(rs)

    def run(phase, srcs, outs, send_sems, recv_sems):
        x, y, c, _ = _where_am_i()
        for w in range(n):
            cp = _remote(srcs[w], outs[w], send_sems, recv_sems, w, (x, y, 1 - c))
            cp.start() if phase == 0 else cp.wait()

    outs = [jax.ShapeDtypeStruct(r.shape, r.dtype) for r in rs]
    return _Comm(rs, outs, n, 2, run)


def _gather_all(v):
    rows = v.shape[0]

    def body(x_ref, out_ref, send_sems, recv_sems, local_sem):
        x, y, c, chips = _where_am_i()
        me, sib = (x, y, c), (x, y, 1 - c)
        blk = lambda px, py, pc: out_ref.at[4 * px + 2 * py + pc]
        mine = pltpu.make_async_copy(x_ref, blk(*me), local_sem)
        mine.start()
        first = [_remote(x_ref, blk(*me), send_sems, recv_sems, 0, sib)]
        first += [_remote(x_ref, blk(*me), send_sems, recv_sems, 1 + j, (*chip, c)) for j, chip in enumerate(chips)]
        for cp in first:
            cp.start()
        passed = [_remote(blk(*chip, c), blk(*chip, c), send_sems, recv_sems, 4 + j, sib) for j, chip in enumerate(chips)]
        for j, chip in enumerate(chips):
            _remote(blk(*chip, c), blk(*chip, c), send_sems, recv_sems, 1 + j, me).wait_recv()
            passed[j].start()
        _remote(blk(*sib), blk(*sib), send_sems, recv_sems, 0, me).wait_recv()
        for j, chip in enumerate(chips):
            _remote(blk(*chip, 1 - c), blk(*chip, 1 - c), send_sems, recv_sems, 4 + j, me).wait_recv()
        for cp in first + passed:
            cp.wait_send()
        mine.wait()

    vm = pl.BlockSpec(memory_space=pltpu.VMEM)
    return pl.pallas_call(
        body, name="gather_all", out_shape=jax.ShapeDtypeStruct((8, rows, LANES), v.dtype),
        in_specs=[vm], out_specs=vm,
        scratch_shapes=[pltpu.SemaphoreType.DMA((7,)), pltpu.SemaphoreType.DMA((7,)), pltpu.SemaphoreType.DMA],
    )(v)


def _flat_pad(parts, total):
    v = jnp.concatenate([p.reshape(-1) for p in parts])
    return jnp.pad(v, (0, total - v.shape[0]))


def _adamw(w, g, m, v):
    m = ADAM_B1 * m + (1.0 - ADAM_B1) * g
    v = ADAM_B2 * v + (1.0 - ADAM_B2) * jnp.square(g)
    m_hat = m / (1.0 - ADAM_B1 ** ADAM_STEP)
    v_hat = v / (1.0 - ADAM_B2 ** ADAM_STEP)
    delta = -ADAM_LR * (m_hat / (jnp.sqrt(v_hat) + ADAM_EPS) + ADAM_WD * w)
    return delta, m, v


def _adamw_call(w, g, m, v, name):
    r, cdim = w.shape
    tm = _tile(r, ROW_TILE, 8)
    o = _rout(r, cdim, f32, tm)
    return _rowwise(_adamw, r, tm, [_rows(w, tm), _rows(g, tm), _rows(m, tm), _rows(v, tm)], [o, o, o], [], name)


def _adamw_halves(w, m, v, g_mine, g_sib, sp, name):
    _, r, cdim = w.shape
    r2 = r // 2
    tr = _row_tile(r2, cdim, 8)
    nb = r2 // tr

    def body(sp_ref, w_ref, m_ref, v_ref, ga_ref, gb_ref, g_out, d_out, m_out, v_out):
        g = jnp.where(pl.program_id(0) == sp_ref[1], ga_ref[...], gb_ref[...])
        delta, mn, vn = _adamw(w_ref[...], g, m_ref[...], v_ref[...])
        g_out[...] = g
        d_out[...] = delta
        m_out[...] = mn
        v_out[...] = vn

    full = pl.BlockSpec((None, tr, cdim), lambda h, i, s: (0, h * nb + i, 0))
    mine = pl.BlockSpec((tr, cdim), lambda h, i, s: (jnp.where(h == s[1], i, 0), 0))
    sib = pl.BlockSpec((tr, cdim), lambda h, i, s: (jnp.where(h == s[1], 0, i), 0))
    out = jax.ShapeDtypeStruct((1, r, cdim), f32)
    gs = pltpu.PrefetchScalarGridSpec(num_scalar_prefetch=1, grid=(2, nb), in_specs=[full, full, full, mine, sib],
                                      out_specs=[full, full, full, full])
    return pl.pallas_call(body, name=name, out_shape=[out, out, out, out], grid_spec=gs,
                          compiler_params=_params(("parallel", "parallel")))(sp, w, m, v, g_mine, g_sib)


MIXER = ("w_in", "w_uq", "w_ukv", "w_out")
FFN = ("w_gate", "w_up", "w_down")
BIG = MIXER + FFN
SMALL = ("q_norm_w", "kv_norm_w", "conv_w", "conv_b", "dt_bias", "a_log", "d_skip", "ssd_norm_w", "attn_out_norm_w",
         "pre_mix_norm_w", "post_mix_norm_w", "pre_ffn_norm_w", "post_ffn_norm_w")
ORDER = ("w_in", "q_norm_w", "w_uq", "kv_norm_w", "w_ukv", "conv_w", "conv_b", "dt_bias", "a_log", "d_skip",
         "ssd_norm_w", "attn_out_norm_w", "w_out", "pre_mix_norm_w", "post_mix_norm_w", "pre_ffn_norm_w",
         "post_ffn_norm_w", "w_gate", "w_up", "w_down")


def kernel(x, positions, w_in, q_norm_w, w_uq, kv_norm_w, w_ukv, conv_w, conv_b, dt_bias, a_log, d_skip, ssd_norm_w, attn_out_norm_w, w_out, pre_mix_norm_w, post_mix_norm_w, pre_ffn_norm_w, post_ffn_norm_w, w_gate, w_up, w_down, loss_target, m_w_in, m_q_norm_w, m_w_uq, m_kv_norm_w, m_w_ukv, m_conv_w, m_conv_b, m_dt_bias, m_a_log, m_d_skip, m_ssd_norm_w, m_attn_out_norm_w, m_w_out, m_pre_mix_norm_w, m_post_mix_norm_w, m_pre_ffn_norm_w, m_post_ffn_norm_w, m_w_gate, m_w_up, m_w_down, v_w_in, v_q_norm_w, v_w_uq, v_kv_norm_w, v_w_ukv, v_conv_w, v_conv_b, v_dt_bias, v_a_log, v_d_skip, v_ssd_norm_w, v_attn_out_norm_w, v_w_out, v_pre_mix_norm_w, v_post_mix_norm_w, v_pre_ffn_norm_w, v_post_ffn_norm_w, v_w_gate, v_w_up, v_w_down):
    local = dict(locals())
    wts = {n: local[n][0] for n in ORDER}
    mom_m = {n: local["m_" + n][0] for n in ORDER}
    mom_v = {n: local["v_" + n][0] for n in ORDER}
    xs = x[0]
    tgt = loss_target[0]
    t, d = xs.shape
    nchip = 4
    my_x, my_y, my_c = lax.axis_index("x"), lax.axis_index("y"), lax.axis_index("c")
    my_chip = 2 * my_x + my_y

    mla_w = d // 2
    nh = mla_w // V_HEAD
    width = d - mla_w
    hs = width // SSD_P
    gn = SSD_G * SSD_N
    cdim = width + 2 * gn
    in_sizes = (Q_RANK, KV_RANK, ROPE, width, cdim, hs)
    d_in = sum(in_sizes)
    tail = LANES
    off_xbc = 0
    off_cq = cdim
    off_ckv = off_cq + Q_RANK
    off_z = off_ckv + KV_RANK
    off_tail = off_z + width
    d_in_p = _round_up(off_tail + tail, 256)
    gw = width // SSD_G
    assert off_cq % Q_RANK == 0 and off_ckv % KV_RANK == 0 and off_z % gw == 0 and off_tail % LANES == 0
    qk_head = NOPE + ROPE
    scale = qk_head ** -0.5
    tm = _tile(t, ROW_TILE, 8)
    tmw = _tile(t, ROW_TILE // 2, 8)

    sp = jnp.stack([my_chip, my_c]).astype(jnp.int32)

    def pair_sums(names, gl, from_sib):
        res = []
        for n, g, fs in zip(names, gl, from_sib):
            _, r2, cs = fs.shape
            tr = _row_tile(r2, cs, 16)
            nb = r2 // tr
            res.append(_blocked(
                lambda a, b: a + b, (nchip, nb),
                [(g, (None, tr, cs), lambda k, i, s, nb=nb: (k, s[1] * nb + i, 0)),
                 (fs, (None, tr, cs), lambda k, i, s: (k, i, 0))],
                [((nchip, r2, cs), bf16, (None, tr, cs), lambda k, i, s: (k, i, 0))], [], "pair_sum_" + n, sp=sp)[0])
        return res

    def chip_sums(names, gl, from_sib, from_chips):
        res = []
        for n, g, fs, fc in zip(names, gl, from_sib, from_chips):
            _, r2, cs = fs.shape
            tr = _row_tile(r2, cs, 16)
            nb = r2 // tr
            res.append(_blocked(
                lambda a, b, r1, r2_, r3: ((a + b) + r1.astype(f32)) + r2_.astype(f32) + r3.astype(f32), (nb,),
                [(g, (None, tr, cs), lambda i, s, nb=nb: (s[0], s[1] * nb + i, 0)),
                 (fs, (None, tr, cs), lambda i, s: (s[0], i, 0)),
                 (fc, (None, tr, cs), lambda i, s: (s[0] ^ 1, i, 0)),
                 (fc, (None, tr, cs), lambda i, s: (s[0] ^ 2, i, 0)),
                 (fc, (None, tr, cs), lambda i, s: (s[0] ^ 3, i, 0))],
                [((r2, cs), f32, (tr, cs), lambda i, s: (i, 0))], [], "chip_sum_" + n, sp=sp)[0])
        return res

    ck, ccs = wts["conv_w"].shape
    cs_in = wts["w_in"].shape[1]
    flat_t = lambda a3: jnp.swapaxes(a3, 1, 2).reshape(1, cs_in * d // LANES, LANES)
    w_in_g, conv_g = _comm_call(_gather_weights([flat_t(w_in)[0].astype(bf16)], [wts["conv_w"]]), "gather_w_in")
    cat_cols = lambda g: jnp.concatenate([g[k] for k in range(nchip)], axis=1)
    conv_full = cat_cols(conv_g)
    mixer_gather = _gather_weights([wts[n].astype(bf16) for n in MIXER[1:]], [])
    ffn_gather = _gather_weights([wts[n].astype(bf16) for n in FFN], [])

    wi = w_in_g.reshape(nchip * cs_in, d)
    o = np.cumsum((0,) + in_sizes)
    seg = lambda i: wi[o[i]:o[i + 1]]
    w_in_pt = jnp.concatenate([seg(4), seg(0), seg(1), seg(3), seg(2), seg(5),
                               jnp.zeros((d_in_p - off_tail - ROPE - hs, d), bf16)], axis=0)

    inv_freq = ROPE_THETA ** (-jnp.arange(0, ROPE, 2, dtype=f32) / ROPE)
    ang = positions[0].astype(f32)[:, None] * inv_freq
    cos, sin = jnp.cos(ang), jnp.sin(ang)
    z32, z64, z96 = jnp.zeros((t, 32), f32), jnp.zeros((t, 64), f32), jnp.zeros((t, 96), f32)
    cosp = jnp.concatenate([cos, cos, z64], axis=1)
    sina = jnp.concatenate([-sin, z96], axis=1)
    sinb = jnp.concatenate([z32, sin, z64], axis=1)

    row = lambda a: a.reshape(1, -1)
    w_pre_mix, w_post_mix = row(wts["pre_mix_norm_w"]), row(wts["post_mix_norm_w"])
    w_pre_ffn, w_post_ffn = row(wts["pre_ffn_norm_w"]), row(wts["post_ffn_norm_w"])
    w_qn, w_kvn = row(wts["q_norm_w"]), row(wts["kv_norm_w"])
    w_attn_n, w_ssd_n = row(wts["attn_out_norm_w"]), row(wts["ssd_norm_w"])
    conv_b_r = row(wts["conv_b"])
    dtb, alog, dskip = row(wts["dt_bias"]), row(wts["a_log"]), row(wts["d_skip"])
    dtb_t, alog_t = dtb.reshape(hs, 1), alog.reshape(hs, 1)

    u = _rowwise(lambda a, w: _rms(a, w), t, tm, [_rows(xs, tm), _par(w_pre_mix)], [_rout(t, d, bf16, tm)], [],
                 "pre_mix_norm")[0]
    proj, (w_uq_g, w_ukv_f, w_out_g) = _matmul([(u, w_in_pt)], "nt", f32, "in_proj", comm=mixer_gather,
                                               at=(0.0, 0.6))
    w_uq_p = jnp.pad(cat_cols(w_uq_g).reshape(Q_RANK, nh, qk_head), ((0, 0), (0, 0), (0, QK_PAD - qk_head))
                     ).reshape(Q_RANK, nh * QK_PAD)
    w_out_f = w_out_g.reshape(-1, w_out_g.shape[2])
    cq_in = _rows(proj, tm, Q_RANK, off_cq // Q_RANK)
    ckv_in = _rows(proj, tm, KV_RANK, off_ckv // KV_RANK)
    cqn = _rowwise(lambda a, w: _rms(a, w), t, tm, [cq_in, _par(w_qn)], [_rout(t, Q_RANK, bf16, tm)], [], "q_norm")[0]
    ckvn = _rowwise(lambda a, w: _rms(a, w), t, tm, [ckv_in, _par(w_kvn)], [_rout(t, KV_RANK, bf16, tm)], [],
                    "kv_norm")[0]
    q_raw = _matmul([(cqn, w_uq_p)], "nn", f32, "q_up")
    kv = _matmul([(ckvn, w_ukv_f)], "nn", bf16, "kv_up")

    def q_rope_fn(qt, cp, sa, sb):
        parts = []
        for h in range(nh):
            parts.append(qt[:, h * QK_PAD: h * QK_PAD + NOPE])
            parts.append(_rope(qt[:, h * QK_PAD + NOPE:(h + 1) * QK_PAD], cp, sa, sb))
        return jnp.concatenate(parts, axis=1)

    tail_cb = off_tail // LANES
    q2 = _rowwise(q_rope_fn, t, tm, [_rows(q_raw, tm), _rows(cosp, tm), _rows(sina, tm), _rows(sinb, tm)],
                  [_rout(t, nh * QK_PAD, bf16, tm)], [], "q_rope")[0]
    kr2 = _rowwise(_rope, t, tm, [_rows(proj, tm, LANES, tail_cb), _rows(cosp, tm), _rows(sina, tm), _rows(sinb, tm)],
                   [_rout(t, LANES, bf16, tm)], [], "k_rope")[0]
    o_att, lse, ffn_w = _flash_fwd(q2, kv, kr2, nh, scale, ffn_gather, (0.0, 0.75))
    w_gate_f, w_up_f = ffn_w[0], ffn_w[1]
    w_down_f = ffn_w[2].reshape(-1, ffn_w[2].shape[2])

    xbc_act = _conv_fwd(proj, conv_full, conv_b_r, cdim, tm)
    dt_raw, dt_raw_t = _blocked(
        lambda blk: (blk[:, ROPE:ROPE + hs], blk.T[ROPE:ROPE + hs, :]), (t // tm,),
        [_rows(proj, tm, LANES, tail_cb)],
        [((t, hs), f32, (tm, hs), lambda i: (i, 0)), ((hs, t), f32, (hs, tm), lambda i: (0, i))], [], "dt_split")
    y_ssd, hprev = _ssd_fwd(xbc_act, dt_raw, dt_raw_t, dtb, dtb_t, alog, alog_t, dskip, width)
    z_ins = [_rows(proj, tm, gw, off_z // gw + i) for i in range(SSD_G)]

    def mix_norms_fn(ov, yv, *rest):
        zs, wa, ws = rest[:SSD_G], rest[SSD_G], rest[SSD_G + 1]
        outs = [_rms(ov, wa)]
        for i in range(SSD_G):
            sl = slice(i * gw, (i + 1) * gw)
            outs.append(_rms(yv[:, sl] * (zs[i] * _sigmoid(zs[i])), ws[:, sl]))
        return jnp.concatenate(outs, axis=1)

    cat = _rowwise(mix_norms_fn, t, tm, [_rows(o_att, tm), _rows(y_ssd, tm)] + z_ins + [_par(w_attn_n), _par(w_ssd_n)],
                   [_rout(t, d, bf16, tm)], [], "attn_ssd_out_norms")[0]
    mix = _matmul([(cat, w_out_f)], "nn", f32, "out_proj")

    def post_mix_fn(mx, xv, w1, w2):
        h1v = xv + _rms(mx, w1)
        return h1v, _rms(h1v, w2)

    h1, v_in = _rowwise(post_mix_fn, t, tmw, [_rows(mix, tmw), _rows(xs, tmw), _par(w_post_mix), _par(w_pre_ffn)],
                        [_rout(t, d, f32, tmw), _rout(t, d, bf16, tmw)], [], "post_mix_pre_ffn_norm")
    g_ff, u_ff, act = _ffn_up(v_in, w_gate_f, w_up_f)
    ffn = _matmul([(act, w_down_f)], "nn", f32, "ffn_down")

    def final_fn(fv, h1v, tg, w):
        h2 = h1v + _rms(fv, w)
        err = h2 - tg
        lpart = 0.5 * jnp.sum(jnp.sum(err * err, axis=1, keepdims=True), axis=0, keepdims=True) / d
        dh2 = err / d
        dff, dw = _rms_bwd(fv, w, dh2)
        return dff, dh2, jnp.broadcast_to(lpart, (1, LANES)), dw

    dffn, dh2, loss_acc, g_post_ffn = _rowwise(
        final_fn, t, tmw, [_rows(ffn, tmw), _rows(h1, tmw), _rows(tgt, tmw), _par(w_post_ffn)],
        [_rout(t, d, bf16, tmw), _rout(t, d, f32, tmw)], [(1, LANES), (1, d)], "loss_post_ffn_norm_bwd")
    loss = lax.psum(loss_acc[0, 0], ("x", "y", "c"))

    dg_ff, du_ff = _ffn_down_bwd(dffn, w_down_f, g_ff, u_ff)
    gw_down = _matmul([(act, dffn)], "tn", f32, "grad_w_down")
    gw_gate = _matmul([(v_in, dg_ff)], "tn", f32, "grad_w_gate", out_chunks=nchip)
    gw_up = _matmul([(v_in, du_ff)], "tn", f32, "grad_w_up", out_chunks=nchip)
    ffn_g = [gw_gate, gw_up, gw_down.reshape(nchip, -1, d)]
    dv_in, ffn_from_sib = _matmul([(dg_ff, w_gate_f), (du_ff, w_up_f)], "nt", f32, "ffn_up_bwd",
                                  comm=_sibling_send_halves(ffn_g), at=(0.0,))
    ffn_pairs = pair_sums(FFN, ffn_g, ffn_from_sib)

    def mid_bwd_fn(h1v, dvv, dh2v, mx, w_pf, w_pm):
        dxn, dw_pf = _rms_bwd(h1v, w_pf, dvv)
        dh1v = dh2v + dxn
        dmx, dw_pm = _rms_bwd(mx, w_pm, dh1v)
        return dh1v, dmx, dw_pf, dw_pm

    dh1, dmix, g_pre_ffn, g_post_mix = _rowwise(
        mid_bwd_fn, t, tmw, [_rows(h1, tmw), _rows(dv_in, tmw), _rows(dh2, tmw), _rows(mix, tmw),
                             _par(w_pre_ffn), _par(w_post_mix)],
        [_rout(t, d, f32, tmw), _rout(t, d, bf16, tmw)], [(1, d), (1, d)], "pre_ffn_post_mix_norm_bwd")
    dcat = _matmul([(dmix, w_out_f)], "nt", f32, "out_proj_bwd")
    gw_out = _matmul([(cat, dmix)], "tn", f32, "grad_w_out")

    def mix_norms_bwd_fn(ov, yv, *rest):
        zs, dcv, wa, ws = rest[:SSD_G], rest[SSD_G], rest[SSD_G + 1], rest[SSD_G + 2]
        dov, dwa = _rms_bwd(ov, wa, dcv[:, :mla_w])
        dl = [jnp.broadcast_to(jnp.sum(dov[:, h * V_HEAD:(h + 1) * V_HEAD] * ov[:, h * V_HEAD:(h + 1) * V_HEAD],
                                       axis=1, keepdims=True), (ov.shape[0], V_HEAD)) for h in range(nh)]
        dys, dzs, dws = [], [], []
        for i in range(SSD_G):
            sl = slice(i * gw, (i + 1) * gw)
            zv, yi = zs[i], yv[:, sl]
            sg = _sigmoid(zv)
            sz = zv * sg
            dgi, dwi = _rms_bwd(yi * sz, ws[:, sl], dcv[:, mla_w + i * gw: mla_w + (i + 1) * gw])
            dys.append(dgi * sz)
            dzs.append(dgi * yi * (sg * (1.0 + zv * (1.0 - sg))))
            dws.append(dwi)
        cc = lambda a: jnp.concatenate(a, axis=1)
        return dov, cc(dl), cc(dys), cc(dzs), dwa, cc(dws)

    do_att, delta, dy_ssd, dz, g_attn_n, g_ssd_n = _rowwise(
        mix_norms_bwd_fn, t, tm,
        [_rows(o_att, tm), _rows(y_ssd, tm)] + z_ins + [_rows(dcat, tm), _par(w_attn_n), _par(w_ssd_n)],
        [_rout(t, mla_w, bf16, tm), _rout(t, mla_w, f32, tm), _rout(t, width, f32, tm), _rout(t, width, bf16, tm)],
        [(1, mla_w), (1, width)], "attn_ssd_out_norms_bwd")
    dkv, dkr_h, dq2, ffn_from_chips = _flash_bwd(q2, kv, kr2, do_att, lse, delta, nh, scale,
                                                 _chips_exchange(ffn_pairs), (0.0,))
    ffn_halves = chip_sums(FFN, ffn_g, ffn_from_sib, ffn_from_chips)

    def q_rope_bwd_fn(dqt, cp, sa, sb):
        parts = []
        for h in range(nh):
            parts.append(dqt[:, h * QK_PAD: h * QK_PAD + NOPE])
            parts.append(_rope_bwd(dqt[:, h * QK_PAD + NOPE:(h + 1) * QK_PAD], cp, sa, sb))
        return jnp.concatenate(parts, axis=1)

    dq_raw = _rowwise(q_rope_bwd_fn, t, tm, [_rows(dq2, tm), _rows(cosp, tm), _rows(sina, tm), _rows(sinb, tm)],
                      [_rout(t, nh * QK_PAD, bf16, tm)], [], "q_rope_bwd")[0]

    def k_rope_bwd_fn(dk, cp, sa, sb):
        tot = dk[:, 0:LANES]
        for h in range(1, nh):
            tot = tot + dk[:, h * LANES:(h + 1) * LANES]
        return _rope_bwd(tot, cp, sa, sb)

    dkr = _rowwise(k_rope_bwd_fn, t, tm, [_rows(dkr_h, tm), _rows(cosp, tm), _rows(sina, tm), _rows(sinb, tm)],
                   [_rout(t, LANES, f32, tm)], [], "k_rope_bwd")[0]
    gw_uq_p = _matmul([(cqn, dq_raw)], "tn", f32, "grad_w_uq")
    gw_ukv = _matmul([(ckvn, dkv)], "tn", f32, "grad_w_ukv", out_chunks=nchip)
    dcqn = _matmul([(dq_raw, w_uq_p)], "nt", f32, "q_up_bwd")
    dckvn = _matmul([(dkv, w_ukv_f)], "nt", f32, "kv_up_bwd")

    def lat_norm_bwd_fn(a, w, dyv):
        return _rms_bwd(a, w, dyv)

    dcq, g_qn = _rowwise(lat_norm_bwd_fn, t, tm, [cq_in, _par(w_qn), _rows(dcqn, tm)],
                         [_rout(t, Q_RANK, bf16, tm)], [(1, Q_RANK)], "q_norm_bwd")
    dckv, g_kvn = _rowwise(lat_norm_bwd_fn, t, tm, [ckv_in, _par(w_kvn), _rows(dckvn, tm)],
                           [_rout(t, KV_RANK, bf16, tm)], [(1, KV_RANK)], "kv_norm_bwd")

    dxbc_act, ddt_raw, g_alog, g_dskip, g_dtb, ffn_sib_halves = _ssd_bwd(
        xbc_act, dt_raw, dt_raw_t, dtb, dtb_t, alog, alog_t, dskip, hprev, dy_ssd, width,
        _sibling_swap(ffn_halves), (0.0,))
    dpre, gcw0, gcw1, gcw2, gcw3, g_conv_b = _conv_bwd_pre(proj, conv_full, conv_b_r, dxbc_act, cdim, tm)
    g_conv_w = jnp.concatenate([gcw0, gcw1, gcw2, gcw3], axis=0)
    dxbc = _conv_bwd_dx(dpre, conv_full, tm)

    def dproj_fn(a, b, c, e, kr_blk, dt_blk):
        parts = [a, b, c, e, (kr_blk + dt_blk).astype(bf16)]
        if d_in_p > off_tail + tail:
            parts.append(jnp.zeros((a.shape[0], d_in_p - off_tail - tail), bf16))
        return jnp.concatenate(parts, axis=1)

    dproj = _rowwise(dproj_fn, t, tm, [_rows(dxbc, tm), _rows(dcq, tm), _rows(dckv, tm), _rows(dz, tm),
                                       _rows(dkr, tm), _rows(ddt_raw, tm)],
                     [_rout(t, d_in_p, bf16, tm)], [], "in_proj_grad_pack")[0]
    split_cols = lambda gf: jnp.stack(jnp.split(gf, nchip, axis=1))
    split_rows = lambda gf: gf.reshape(nchip, gf.shape[0] // nchip, gf.shape[1])
    gw_uq = gw_uq_p.reshape(Q_RANK, nh, QK_PAD)[:, :, :qk_head].reshape(Q_RANK, nh * qk_head)
    rest_g = [split_cols(gw_uq), gw_ukv, split_rows(gw_out)]
    du_in, rest_from_sib = _matmul([(dproj, w_in_pt)], "nn", f32, "in_proj_bwd",
                                   comm=_sibling_send_halves(rest_g), at=(0.0,))
    rest_pairs = pair_sums(MIXER[1:], rest_g, rest_from_sib)
    gw_in_pt, rest_from_chips = _matmul([(dproj, u)], "tn", f32, "grad_w_in",
                                        comm=_chips_exchange(rest_pairs), at=(0.0,))
    rest_halves = chip_sums(MIXER[1:], rest_g, rest_from_sib, rest_from_chips)

    def first_bwd_fn(xv, duv, dh1v, w):
        dxn, dw = _rms_bwd(xv, w, duv)
        return dh1v + dxn, dw

    grad_x, g_pre_mix = _rowwise(first_bwd_fn, t, tmw, [_rows(xs, tmw), _rows(du_in, tmw), _rows(dh1, tmw),
                                                         _par(w_pre_mix)],
                                 [_rout(t, d, f32, tmw)], [(1, d)], "pre_mix_norm_bwd")

    gseg = lambda a, b: gw_in_pt[a:b]
    gw_in_t = jnp.concatenate([gseg(off_cq, off_ckv), gseg(off_ckv, off_z), gseg(off_tail, off_tail + ROPE),
                               gseg(off_z, off_tail), gseg(off_xbc, off_cq),
                               gseg(off_tail + ROPE, off_tail + ROPE + hs)], axis=0)
    in_g = [gw_in_t.reshape(nchip, cs_in * d // LANES, LANES)]
    in_from_sib = _comm_call(_sibling_send_halves(in_g), "sibling_send_halves")
    in_pairs = pair_sums(MIXER[:1], in_g, in_from_sib)
    in_from_chips = _comm_call(_chips_exchange(in_pairs), "chips_exchange")
    mix_halves = chip_sums(MIXER[:1], in_g, in_from_sib, in_from_chips) + rest_halves
    mix_sib_halves = _comm_call(_sibling_swap(mix_halves), "sibling_swap")
    halves = mix_halves + ffn_halves
    sib_halves = mix_sib_halves + ffn_sib_halves
    gshard = {}

    gsmall = {"q_norm_w": g_qn, "kv_norm_w": g_kvn, "conv_w": g_conv_w, "conv_b": g_conv_b, "dt_bias": g_dtb,
              "a_log": g_alog, "d_skip": g_dskip, "ssd_norm_w": g_ssd_n, "attn_out_norm_w": g_attn_n,
              "pre_mix_norm_w": g_pre_mix, "post_mix_norm_w": g_post_mix, "pre_ffn_norm_w": g_pre_ffn,
              "post_ffn_norm_w": g_post_ffn}
    small_sizes = [int(np.prod(gsmall[n].shape)) for n in SMALL]
    srows = _round_up(-(-sum(small_sizes) // LANES), 8)
    spart = _flat_pad([gsmall[n] for n in SMALL], srows * LANES).reshape(srows, LANES)
    sall = _gather_all(spart)

    def sum8_fn(a):
        tot = a[0]
        for k in range(1, 8):
            tot = tot + a[k]
        return tot

    ssum = _blocked(sum8_fn, (1,), [(sall, sall.shape, lambda i: (0, 0, 0))],
                    [((srows, LANES), f32, (srows, LANES), lambda i: (0, 0))], [], "small_grad_sum")[0].reshape(-1)
    gred = {}
    off = 0
    for n, sz in zip(SMALL, small_sizes):
        gred[n] = ssum[off:off + sz].reshape(gsmall[n].shape)
        off += sz
    gshard["conv_w"] = lax.dynamic_slice_in_dim(gred["conv_w"], my_chip * ccs, ccs, axis=1)
    for n in SMALL:
        if n != "conv_w":
            gshard[n] = gred[n].reshape(wts[n].shape)

    delta, new_m, new_v = {}, {}, {}
    big_out = {}
    for n, mine_h, sib_h in zip(BIG, halves, sib_halves):
        view = flat_t if n == "w_in" else (lambda a3: a3)
        res = _adamw_halves(view(local[n]), view(local["m_" + n]), view(local["v_" + n]), mine_h, sib_h, sp,
                            "adamw_" + n)
        big_out[n] = [jnp.swapaxes(a.reshape(1, cs_in, d), 1, 2) for a in res] if n == "w_in" else res
    pack = lambda src: _flat_pad([src[n] for n in SMALL], srows * LANES).reshape(srows, LANES)
    sd, sm, sv = _adamw_call(pack(wts), pack(gshard), pack(mom_m), pack(mom_v), "adamw_small")
    off = 0
    for n in SMALL:
        sz = int(np.prod(wts[n].shape))
        for dst, src in ((delta, sd), (new_m, sm), (new_v, sv)):
            dst[n] = src.reshape(-1)[off:off + sz].reshape(wts[n].shape)
        off += sz

    small_out = (gshard, delta, new_m, new_v)
    pick = lambda k: [big_out[n][k] if n in big_out else small_out[k][n][None] for n in ORDER]
    return (loss, grad_x[None], *pick(0), *pick(1), *pick(2), *pick(3))
```

```python
import functools

import numpy as np
import jax
import jax.numpy as jnp
from jax import lax
from jax.experimental import pallas as pl
from jax.experimental.pallas import tpu as pltpu

f32, bf16 = jnp.float32, jnp.bfloat16

EPS = 1e-6
V_HEAD = 128
NOPE = 128
ROPE = 64
QK_PAD = 256
Q_RANK = 512
KV_RANK = 512
ROPE_THETA = 10000.0
SSD_P = 64
SSD_G = 2
SSD_N = 128
SSD_K = 4
CHUNK = 128
ADAM_LR, ADAM_B1, ADAM_B2, ADAM_EPS, ADAM_WD, ADAM_STEP = 0.001, 0.9, 0.999, 1e-08, 0.01, 10

VMEM_LIMIT_BYTES = 48 * 1024 * 1024
LANES = 128
ATT_TILE = 1024
ATT_TILE_FWD = 1024
MM_TM, MM_TN, MM_TK = 1408, 1024, 1408
CHUNK_WHOLE_MAX = 1536
ROW_TILE = 256
ROW_BLOCK_BYTES = 2 * 1024 * 1024

NN = (((1,), (0,)), ((), ()))
NT = (((1,), (1,)), ((), ()))
TN = (((0,), (0,)), ((), ()))
MESH = pl.DeviceIdType.MESH
ANY = pl.BlockSpec(memory_space=pl.ANY)


def _tile(dim, cap, align=LANES):
    if dim <= cap:
        return dim
    t = (cap // align) * align
    while t >= align:
        if dim % t == 0:
            return t
        t -= align
    raise ValueError(f"no tile for {dim} under {cap}")


def _row_tile(rows, cols, align):
    best = None
    for tr in range(align, rows + 1, align):
        if rows % tr == 0 and tr * cols * 4 <= ROW_BLOCK_BYTES:
            best = tr
    return best or rows


def _round_up(n, m):
    return -(-n // m) * m


def _params(sem):
    return pltpu.CompilerParams(dimension_semantics=sem, vmem_limit_bytes=VMEM_LIMIT_BYTES)


def _dot(a, b, dims):
    return lax.dot_general(a.astype(bf16), b.astype(bf16), dims, preferred_element_type=f32)


def _call(body, name, out_shape, grid, in_specs, out_specs, scratch_shapes, sem, args, comm=None, at=None,
          prefetch=()):
    npf = len(prefetch)
    if comm is None:
        gs = pltpu.PrefetchScalarGridSpec(num_scalar_prefetch=npf, grid=grid, in_specs=list(in_specs),
                                          out_specs=list(out_specs), scratch_shapes=list(scratch_shapes))
        res = pl.pallas_call(body, name=name, out_shape=list(out_shape), grid_spec=gs,
                             compiler_params=_params(sem))(*prefetch, *args)
        return list(res), []
    n_in, n_out, n_sc = len(args), len(out_shape), len(scratch_shapes)
    na, no = len(comm.args), len(comm.outs)
    steps = int(np.prod(grid))

    def full(*allrefs):
        pf, refs = allrefs[:npf], allrefs[npf:]
        cin = refs[n_in:n_in + na]
        o0 = n_in + na
        cout = refs[o0 + n_out:o0 + n_out + no]
        s0 = o0 + n_out + no
        send_sems, recv_sems = refs[s0 + n_sc], refs[s0 + n_sc + 1]
        lin = pl.program_id(0)
        for dim in range(1, len(grid)):
            lin = lin * grid[dim] + pl.program_id(dim)
        for p in range(comm.nphase - 1):
            @pl.when(lin == int(round(at[p] * (steps - 1))))
            def _(p=p):
                comm.run(p, cin, cout, send_sems, recv_sems)
        body(*pf, *refs[:n_in], *refs[o0:o0 + n_out], *refs[s0:s0 + n_sc])

        @pl.when(lin == steps - 1)
        def _():
            comm.run(comm.nphase - 1, cin, cout, send_sems, recv_sems)

    gs = pltpu.PrefetchScalarGridSpec(
        num_scalar_prefetch=npf, grid=grid, in_specs=list(in_specs) + [ANY] * na,
        out_specs=list(out_specs) + [ANY] * no, scratch_shapes=list(scratch_shapes) + comm.sems())
    res = pl.pallas_call(full, name=name, out_shape=list(out_shape) + comm.outs, grid_spec=gs,
                         compiler_params=_params(("arbitrary",) * len(grid)))(*prefetch, *args, *comm.args)
    return list(res[:n_out]), list(res[n_out:])


def _chunk_tile(cs, cap):
    return cs if cs <= CHUNK_WHOLE_MAX else _tile(cs, cap)


def _matmul(pairs, mode, out_dtype, name, out_chunks=None, comm=None, at=None):
    a0, b0 = pairs[0]
    chunked = b0.ndim == 3
    cs = b0.shape[2] if chunked else None
    bcols = b0.shape[0] * b0.shape[2] if chunked else b0.shape[1]
    brows = b0.shape[1] if chunked else b0.shape[0]
    if mode == "nn":
        (m, k), n = a0.shape, bcols
    elif mode == "nt":
        (m, k), n = a0.shape, brows
    else:
        (k, m), n = a0.shape, bcols
    tm = _tile(m, MM_TM)
    if mode == "nt":
        tn = _tile(n, MM_TN)
        tk = _chunk_tile(cs, MM_TK) if chunked else _tile(k, MM_TK)
    else:
        tk = _tile(k, MM_TK)
        if chunked:
            tn = _chunk_tile(cs, MM_TN)
        elif out_chunks:
            tn = _chunk_tile(n // out_chunks, MM_TN)
        else:
            tn = _tile(n, MM_TN)
    nk = k // tk
    if mode == "nn":
        a_spec = pl.BlockSpec((tm, tk), lambda i, j, kk: (i, kk))
        if chunked:
            q = cs // tn
            b_spec = pl.BlockSpec((None, tk, tn), lambda i, j, kk: (j // q, kk, j % q))
        else:
            b_spec = pl.BlockSpec((tk, tn), lambda i, j, kk: (kk, j))
        dims = NN
    elif mode == "nt":
        a_spec = pl.BlockSpec((tm, tk), lambda i, j, kk: (i, kk))
        if chunked:
            q = cs // tk
            b_spec = pl.BlockSpec((None, tn, tk), lambda i, j, kk: (kk // q, j, kk % q))
        else:
            b_spec = pl.BlockSpec((tn, tk), lambda i, j, kk: (j, kk))
        dims = NT
    else:
        a_spec = pl.BlockSpec((tk, tm), lambda i, j, kk: (kk, i))
        b_spec = pl.BlockSpec((tk, tn), lambda i, j, kk: (kk, j))
        dims = TN
    if out_chunks:
        qo = (n // out_chunks) // tn
        out_shape = jax.ShapeDtypeStruct((out_chunks, m, n // out_chunks), out_dtype)
        o_spec = pl.BlockSpec((None, tm, tn), lambda i, j, kk: (j // qo, i, j % qo))
    else:
        out_shape = jax.ShapeDtypeStruct((m, n), out_dtype)
        o_spec = pl.BlockSpec((tm, tn), lambda i, j, kk: (i, j))
    npair = len(pairs)

    def body(*refs):
        o_ref, acc = refs[2 * npair], refs[2 * npair + 1]
        kk = pl.program_id(2)

        @pl.when(kk == 0)
        def _():
            acc[...] = jnp.zeros_like(acc)

        part = _dot(refs[0][...], refs[1][...], dims)
        for p in range(1, npair):
            part = part + _dot(refs[2 * p][...], refs[2 * p + 1][...], dims)
        acc[...] += part

        @pl.when(kk == nk - 1)
        def _():
            o_ref[...] = acc[...].astype(out_dtype)

    args = [t for pr in pairs for t in pr]
    res, cres = _call(body, name, [out_shape], (m // tm, n // tn, nk), [a_spec, b_spec] * npair, [o_spec],
                      [pltpu.VMEM((tm, tn), f32)], ("parallel", "parallel", "arbitrary"), args, comm, at)
    return res[0] if comm is None else (res[0], cres)


def _sigmoid(x):
    return 1.0 / (1.0 + jnp.exp(-x))


def _ffn_up(v, wg, wu):
    m, k = v.shape
    nchunk, _, cs = wg.shape
    n = nchunk * cs
    tm, tn = _tile(m, 512), _chunk_tile(cs, 512)
    q = cs // tn
    w_spec = pl.BlockSpec((None, k, tn), lambda j, i: (j // q, 0, j % q))

    def body(v_ref, wg_ref, wu_ref, g_ref, u_ref, act_ref):
        vb = v_ref[...]
        g = _dot(vb, wg_ref[...], NN)
        u = _dot(vb, wu_ref[...], NN)
        sg = _sigmoid(g)
        silu = g * sg
        g_ref[...] = silu.astype(bf16)
        u_ref[...] = (u * (sg * (1.0 + g * (1.0 - sg)))).astype(bf16)
        act_ref[...] = (silu * u).astype(bf16)

    out = jax.ShapeDtypeStruct((m, n), bf16)
    o_spec = pl.BlockSpec((tm, tn), lambda j, i: (i, j))
    return pl.pallas_call(
        body, name="ffn_up", out_shape=(out, out, out), grid=(n // tn, m // tm),
        in_specs=[pl.BlockSpec((tm, k), lambda j, i: (i, 0)), w_spec, w_spec],
        out_specs=(o_spec, o_spec, o_spec),
        compiler_params=_params(("parallel", "parallel")),
    )(v, wg, wu)


def _ffn_down_bwd(dffn, wd, dact_du, dact_dg):
    m, k = dffn.shape
    n = wd.shape[0]
    tm, tn = _tile(m, 1024), _tile(n, 512)

    def body(d_ref, w_ref, fu_ref, fg_ref, dg_ref, du_ref):
        dact = _dot(d_ref[...], w_ref[...], NT)
        du_ref[...] = (dact * fu_ref[...].astype(f32)).astype(bf16)
        dg_ref[...] = (dact * fg_ref[...].astype(f32)).astype(bf16)

    out = jax.ShapeDtypeStruct((m, n), bf16)
    o_spec = pl.BlockSpec((tm, tn), lambda i, j: (i, j))
    return pl.pallas_call(
        body, name="ffn_down_bwd", out_shape=(out, out), grid=(m // tm, n // tn),
        in_specs=[pl.BlockSpec((tm, k), lambda i, j: (i, 0)), pl.BlockSpec((tn, k), lambda i, j: (j, 0)),
                  o_spec, o_spec],
        out_specs=(o_spec, o_spec),
        compiler_params=_params(("parallel", "parallel")),
    )(dffn, wd, dact_du, dact_dg)


def _blocked(fn, grid, ins, outs, accs, name, sp=None):
    n_in, n_out, n_acc = len(ins), len(outs), len(accs)
    nsp = 0 if sp is None else 1

    def body(*refs):
        refs = refs[nsp:]
        tiles = [r[...] for r in refs[:n_in]]
        res = fn(*tiles)
        if not isinstance(res, (tuple, list)):
            res = (res,)
        for r, val in zip(refs[n_in:n_in + n_out], res[:n_out]):
            r[...] = val.astype(r.dtype)
        if n_acc:
            first = pl.program_id(0) == 0
            for d in range(1, len(grid)):
                first = jnp.logical_and(first, pl.program_id(d) == 0)

            @pl.when(first)
            def _():
                for r in refs[n_in + n_out:]:
                    r[...] = jnp.zeros_like(r)

            for r, val in zip(refs[n_in + n_out:], res[n_out:]):
                r[...] += val

    def acc_map(shape):
        zeros = (0,) * len(shape)
        return lambda *idx: zeros

    in_specs = [pl.BlockSpec(bs, im) for _, bs, im in ins]
    out_specs = [pl.BlockSpec(bs, im) for _, _, bs, im in outs] + [pl.BlockSpec(s, acc_map(s)) for s in accs]
    out_shape = [jax.ShapeDtypeStruct(s, d) for s, d, _, _ in outs] + [jax.ShapeDtypeStruct(s, f32) for s in accs]
    sem = ("arbitrary",) * len(grid) if n_acc else ("parallel",) * len(grid)
    args = [a for a, _, _ in ins]
    if sp is None:
        res = pl.pallas_call(body, name=name, out_shape=out_shape, grid=grid, in_specs=in_specs,
                             out_specs=out_specs, compiler_params=_params(sem))(*args)
    else:
        gs = pltpu.PrefetchScalarGridSpec(num_scalar_prefetch=1, grid=grid, in_specs=in_specs, out_specs=out_specs)
        res = pl.pallas_call(body, name=name, out_shape=out_shape, grid_spec=gs,
                             compiler_params=_params(sem))(sp, *args)
    return res


def _rows(a, tm, cols=None, cb=0):
    w = a.shape[1] if cols is None else cols
    return (a, (tm, w), lambda i: (i, cb))


def _par(a):
    zeros = (0,) * a.ndim
    return (a, a.shape, lambda i: zeros)


def _rout(t, w, dtype, tm):
    return ((t, w), dtype, (tm, w), lambda i: (i, 0))


def _rowwise(fn, t, tm, ins, outs, accs, name):
    return _blocked(fn, (t // tm,), ins, outs, accs, name)


def _rms(x, w):
    r = lax.rsqrt(jnp.mean(x * x, axis=-1, keepdims=True) + EPS)
    return x * r * w


def _rms_bwd(x, w, dy):
    r = lax.rsqrt(jnp.mean(x * x, axis=-1, keepdims=True) + EPS)
    xh = x * r
    dyw = dy * w
    dx = r * (dyw - xh * jnp.mean(dyw * xh, axis=-1, keepdims=True))
    return dx, jnp.sum(dy * xh, axis=0, keepdims=True)


def _silu_grad(x):
    s = _sigmoid(x)
    return s * (1.0 + x * (1.0 - s))


def _rope(blk, cosp, sina, sinb):
    return blk * cosp + pltpu.roll(blk, 96, 1) * sina + pltpu.roll(blk, 32, 1) * sinb


def _rope_bwd(dy, cosp, sina, sinb):
    return dy * cosp + pltpu.roll(dy * sina, 32, 1) + pltpu.roll(dy * sinb, 96, 1)


HALO = 8


def _conv_taps(buf, w, tm, base):
    acc = buf[base:base + tm, :] * w[0:1]
    for k in range(1, SSD_K):
        acc = acc + buf[base + k:base + k + tm, :] * w[k:k + 1]
    return acc


def _conv_specs(t, tm, cdim):
    cur = pl.BlockSpec((tm, cdim), lambda i: (i, 0))
    prev = pl.BlockSpec((HALO, cdim), lambda i: (jnp.maximum(i * (tm // HALO) - 1, 0), 0))
    nxt = pl.BlockSpec((HALO, cdim), lambda i: (jnp.minimum((i + 1) * (tm // HALO), t // HALO - 1), 0))
    return cur, prev, nxt


def _conv_fwd(src, w, b, cdim, tm):
    t = src.shape[0]
    cur, prev, _ = _conv_specs(t, tm, cdim)

    def body(x_ref, p_ref, w_ref, b_ref, o_ref, buf):
        buf[0:HALO, :] = jnp.where(pl.program_id(0) > 0, p_ref[...], 0.0)
        buf[HALO:HALO + tm, :] = x_ref[...]
        pre = _conv_taps(buf, w_ref[...], tm, HALO - (SSD_K - 1)) + b_ref[...]
        o_ref[...] = pre * _sigmoid(pre)

    par = lambda a: pl.BlockSpec(a.shape, lambda i: (0, 0))
    return pl.pallas_call(
        body, name="conv_silu", out_shape=jax.ShapeDtypeStruct((t, cdim), f32), grid=(t // tm,),
        in_specs=[cur, prev, par(w), par(b)], out_specs=cur,
        scratch_shapes=[pltpu.VMEM((tm + HALO, cdim), f32)], compiler_params=_params(("parallel",)),
    )(src, src, w, b)


def _conv_bwd_pre(src, w, b, dact, cdim, tm):
    t = src.shape[0]
    cur, prev, _ = _conv_specs(t, tm, cdim)

    def body(x_ref, p_ref, w_ref, b_ref, d_ref, dpre_ref, dw0, dw1, dw2, dw3, db, buf):
        @pl.when(pl.program_id(0) == 0)
        def _():
            for r in (dw0, dw1, dw2, dw3, db):
                r[...] = jnp.zeros_like(r)

        buf[0:HALO, :] = jnp.where(pl.program_id(0) > 0, p_ref[...], 0.0)
        buf[HALO:HALO + tm, :] = x_ref[...]
        base = HALO - (SSD_K - 1)
        pre = _conv_taps(buf, w_ref[...], tm, base) + b_ref[...]
        dpre = d_ref[...] * _silu_grad(pre)
        dpre_ref[...] = dpre
        for k, r in enumerate((dw0, dw1, dw2, dw3)):
            r[...] += jnp.sum(dpre * buf[base + k:base + k + tm, :], axis=0, keepdims=True)
        db[...] += jnp.sum(dpre, axis=0, keepdims=True)

    par = lambda a: pl.BlockSpec(a.shape, lambda i: (0, 0))
    acc = pl.BlockSpec((1, cdim), lambda i: (0, 0))
    acc_shape = jax.ShapeDtypeStruct((1, cdim), f32)
    return pl.pallas_call(
        body, name="conv_silu_bwd", out_shape=[jax.ShapeDtypeStruct((t, cdim), f32)] + [acc_shape] * 5,
        grid=(t // tm,), in_specs=[cur, prev, par(w), par(b), cur], out_specs=[cur] + [acc] * 5,
        scratch_shapes=[pltpu.VMEM((tm + HALO, cdim), f32)], compiler_params=_params(("arbitrary",)),
    )(src, src, w, b, dact)


def _conv_bwd_dx(dpre, w, tm):
    t, cdim = dpre.shape
    cur, _, nxt = _conv_specs(t, tm, cdim)
    last = t // tm - 1

    def body(d_ref, n_ref, w_ref, o_ref, buf):
        buf[0:tm, :] = d_ref[...]
        buf[tm:tm + HALO, :] = jnp.where(pl.program_id(0) < last, n_ref[...], 0.0)
        wv = w_ref[...]
        acc = buf[0:tm, :] * wv[SSD_K - 1:SSD_K]
        for k in range(SSD_K - 1):
            s = SSD_K - 1 - k
            acc = acc + buf[s:s + tm, :] * wv[k:k + 1]
        o_ref[...] = acc.astype(bf16)

    return pl.pallas_call(
        body, name="conv_bwd_dx", out_shape=jax.ShapeDtypeStruct((t, cdim), bf16), grid=(t // tm,),
        in_specs=[cur, nxt, pl.BlockSpec(w.shape, lambda i: (0, 0))], out_specs=cur,
        scratch_shapes=[pltpu.VMEM((tm + HALO, cdim), f32)], compiler_params=_params(("parallel",)),
    )(dpre, dpre, w)


def _causal_mask(s, row0=0):
    row = lax.broadcasted_iota(jnp.int32, s.shape, 0) + row0
    col = lax.broadcasted_iota(jnp.int32, s.shape, 1)
    return jnp.where(row >= col, s, -jnp.inf)


def _causal_pairs(nq, q_major):
    pairs = [(qi, ki) for qi in range(nq) for ki in range(qi + 1)]
    if not q_major:
        pairs.sort(key=lambda p: (p[1], p[0]))
    return (jnp.asarray([p[0] for p in pairs], jnp.int32), jnp.asarray([p[1] for p in pairs], jnp.int32))


def _flash_fwd(q, kv, kr, nheads, scale, comm=None, at=None):
    t = q.shape[0]
    tq = _tile(t, ATT_TILE_FWD)
    nq = t // tq

    qtab, ktab = _causal_pairs(nq, q_major=True)
    hp = 2 if nheads % 2 == 0 else 1

    def body(qt, kt, q_ref, kv_ref, kr_ref, o_ref, lse_ref, m_sc, l_sc, acc_sc):
        qi, ki = qt[pl.program_id(1)], kt[pl.program_id(1)]

        @pl.when(ki == 0)
        def _():
            m_sc[...] = jnp.full_like(m_sc, -jnp.inf)
            l_sc[...] = jnp.zeros_like(l_sc)
            acc_sc[...] = jnp.zeros_like(acc_sc)

        def step(masked, last):
            krb = kr_ref[...]
            ss = []
            for j in range(hp):
                k = jnp.concatenate([kv_ref[:, j * QK_PAD: j * QK_PAD + NOPE], krb], axis=1)
                ss.append(lax.dot_general(q_ref[:, j * QK_PAD:(j + 1) * QK_PAD], k, NT, preferred_element_type=f32))
            soft = []
            for j in range(hp):
                s = ss[j] * scale
                if masked:
                    s = _causal_mask(s)
                m_old = m_sc[j]
                m_new = jnp.maximum(m_old, jnp.max(s, axis=1, keepdims=True))
                alpha = jnp.exp(m_old - m_new)
                p = jnp.exp(s - m_new)
                soft.append((m_new, alpha, alpha * l_sc[j] + jnp.sum(p, axis=1, keepdims=True), p.astype(bf16)))
            for j in range(hp):
                m_new, alpha, l, p = soft[j]
                v = kv_ref[:, j * QK_PAD + NOPE:(j + 1) * QK_PAD]
                acc = alpha * acc_sc[j] + lax.dot_general(p, v, NN, preferred_element_type=f32)
                if last:
                    o_ref[:, j * V_HEAD:(j + 1) * V_HEAD] = acc / l
                    lse_ref[:, j * V_HEAD:(j + 1) * V_HEAD] = jnp.broadcast_to(m_new + jnp.log(l), (tq, V_HEAD))
                else:
                    l_sc[j] = l
                    acc_sc[j] = acc
                    m_sc[j] = m_new

        @pl.when(ki < qi)
        def _():
            step(False, False)

        @pl.when(ki == qi)
        def _():
            step(True, True)

    o_spec = pl.BlockSpec((tq, hp * V_HEAD), lambda h, s, qt, kt: (qt[s], h))
    out = jax.ShapeDtypeStruct((t, nheads * V_HEAD), f32)
    (o, lse), cres = _call(
        body, "flash_fwd", [out, out], (nheads // hp, len(qtab)),
        [pl.BlockSpec((tq, hp * QK_PAD), lambda h, s, qt, kt: (qt[s], h)),
         pl.BlockSpec((tq, hp * QK_PAD), lambda h, s, qt, kt: (kt[s], h)),
         pl.BlockSpec((tq, LANES), lambda h, s, qt, kt: (kt[s], 0))],
        [o_spec, o_spec],
        [pltpu.VMEM((hp, tq, 1), f32), pltpu.VMEM((hp, tq, 1), f32), pltpu.VMEM((hp, tq, V_HEAD), f32)],
        ("parallel", "arbitrary"), [q, kv, kr], comm, at, prefetch=(qtab, ktab))
    return o, lse, cres


def _flash_bwd(q, kv, kr, do, lse, delta, nheads, scale, comm=None, at=None):
    t = q.shape[0]
    tq = _tile(t, ATT_TILE)
    nq = t // tq
    qtab, ktab = _causal_pairs(nq, q_major=False)
    nsub = 2 if tq % 32 == 0 else 1

    def body(qt, kt, q_ref, kn_ref, kr_ref, v_ref, do_ref, lse_ref, dl_ref, dkv_ref, dkr_ref, dq_ref, dk_sc, dv_sc):
        qi, ki = qt[pl.program_id(1)], kt[pl.program_id(1)]

        @pl.when(pl.program_id(1) == 0)
        def _():
            dq_ref[...] = jnp.zeros_like(dq_ref)

        @pl.when(qi == ki)
        def _():
            dk_sc[...] = jnp.zeros_like(dk_sc)
            dv_sc[...] = jnp.zeros_like(dv_sc)

        def step(masked):
            k = jnp.concatenate([kn_ref[...], kr_ref[...]], axis=1)
            vb = v_ref[...]
            parts = [slice(i * (tq // nsub), (i + 1) * (tq // nsub)) for i in range(nsub)]
            qs = [q_ref[r, :] for r in parts]
            dos = [do_ref[r, :] for r in parts]
            ss = [lax.dot_general(qb, k, NT, preferred_element_type=f32) for qb in qs]
            dps = [lax.dot_general(dob, vb, NT, preferred_element_type=f32) for dob in dos]
            ps, dss = [], []
            for r, s, dp in zip(parts, ss, dps):
                s = s * scale
                if masked:
                    s = _causal_mask(s, r.start)
                p = jnp.exp(s - lse_ref[r, 0:1])
                ps.append(p.astype(bf16))
                dss.append((p * (dp - dl_ref[r, 0:1]) * scale).astype(bf16))
            for r, qb, dob, p, ds in zip(parts, qs, dos, ps, dss):
                dv_sc[...] += lax.dot_general(p, dob, TN, preferred_element_type=f32)
                dk_sc[...] += lax.dot_general(ds, qb, TN, preferred_element_type=f32)
                rows = pl.ds(pl.multiple_of(qi * tq + r.start, tq // nsub), tq // nsub)
                dq_ref[rows, :] += lax.dot_general(ds, k, NN, preferred_element_type=f32)

        @pl.when(qi > ki)
        def _():
            step(False)

        @pl.when(qi == ki)
        def _():
            step(True)

        @pl.when(qi == nq - 1)
        def _():
            dk = dk_sc[...]
            dkv_ref[...] = jnp.concatenate([dk[:, :NOPE], dv_sc[...]], axis=1).astype(bf16)
            dkr_ref[...] = dk[:, NOPE:]

    hspec = pl.BlockSpec((tq, V_HEAD), lambda h, s, qt, kt: (qt[s], h))
    (dkv, dkr, dq), cres = _call(
        body, "flash_bwd",
        [jax.ShapeDtypeStruct((t, nheads * QK_PAD), bf16), jax.ShapeDtypeStruct((t, nheads * LANES), f32),
         jax.ShapeDtypeStruct((t, nheads * QK_PAD), f32)],
        (nheads, len(qtab)),
        [pl.BlockSpec((tq, QK_PAD), lambda h, s, qt, kt: (qt[s], h)),
         pl.BlockSpec((tq, NOPE), lambda h, s, qt, kt: (kt[s], 2 * h)),
         pl.BlockSpec((tq, LANES), lambda h, s, qt, kt: (kt[s], 0)),
         pl.BlockSpec((tq, V_HEAD), lambda h, s, qt, kt: (kt[s], 2 * h + 1)),
         hspec, hspec, hspec],
        [pl.BlockSpec((tq, QK_PAD), lambda h, s, qt, kt: (kt[s], h)),
         pl.BlockSpec((tq, LANES), lambda h, s, qt, kt: (kt[s], h)),
         pl.BlockSpec((t, QK_PAD), lambda h, s, qt, kt: (0, h))],
        [pltpu.VMEM((tq, QK_PAD), f32), pltpu.VMEM((tq, V_HEAD), f32)],
        ("parallel", "arbitrary"), [q, kv, kr, kv, do, lse, delta], comm, at, prefetch=(qtab, ktab))
    return dkv, dkr, dq, cres


def _split3(a):
    hi = a.astype(bf16)
    r1 = a - hi.astype(f32)
    mid = r1.astype(bf16)
    lo = (r1 - mid.astype(f32)).astype(bf16)
    return hi, mid, lo


def _split2(a):
    hi = a.astype(bf16)
    return hi, (a - hi.astype(f32)).astype(bf16)


def _ones_dot_left(tri, a):
    hi, mid, lo = _split3(a)
    d = lambda v: lax.dot_general(tri, v, NN, preferred_element_type=f32)
    return d(hi) + d(mid) + d(lo)


def _ones_dot_right(a, tri):
    hi, mid, lo = _split3(a)
    d = lambda v: lax.dot_general(v, tri, NN, preferred_element_type=f32)
    return d(hi) + d(mid) + d(lo)


def _softplus(x):
    return jnp.maximum(x, 0.0) + jnp.log(1.0 + jnp.exp(-jnp.abs(x)))


def _head_spread(hs, width):
    shift = SSD_P.bit_length() - 1
    return (lax.broadcasted_iota(jnp.int32, (hs, width), 0)
            == lax.shift_right_logical(lax.broadcasted_iota(jnp.int32, (hs, width), 1), shift)).astype(bf16)


def _ssd_common(dt_ref, dtT_ref, dtb_ref, dtbT_ref, alog_ref, alogT_ref):
    ii = lax.broadcasted_iota(jnp.int32, (CHUNK, CHUNK), 0)
    jj = lax.broadcasted_iota(jnp.int32, (CHUNK, CHUNK), 1)
    tri = ii >= jj
    raw = dt_ref[...] + dtb_ref[...]
    dt = _softplus(raw)
    a_neg = -jnp.exp(alog_ref[...])
    cum = _ones_dot_left(tri.astype(bf16), dt * a_neg)
    dt_t = _softplus(dtT_ref[...] + dtbT_ref[...])
    cum_t = _ones_dot_right(dt_t * (-jnp.exp(alogT_ref[...])), (ii <= jj).astype(bf16))
    return tri, raw, dt, a_neg, cum, cum_t


def _ssd_fwd(xbc, dt, dt_t, dtb, dtb_t, alog, alog_t, dskip, width):
    t, cdim = xbc.shape
    hs = dt.shape[1]
    nc = t // CHUNK
    epg = hs // SSD_G
    gn = SSD_G * SSD_N

    def body(x_ref, dt_ref, dtT_ref, dtb_ref, dtbT_ref, alog_ref, alogT_ref, d_ref, y_ref, hp_ref, h_sc, yd_sc):
        @pl.when(pl.program_id(0) == 0)
        def _():
            h_sc[...] = jnp.zeros_like(h_sc)

        tri, _, dtv, _, cum, cum_t = _ssd_common(dt_ref, dtT_ref, dtb_ref, dtbT_ref, alog_ref, alogT_ref)
        spread = _head_spread(hs, width)
        clast = cum[CHUNK - 1:CHUNK, :]
        dec = jnp.exp(clast)
        wide = _ones_dot_right(jnp.concatenate([dtv, jnp.exp(cum), jnp.exp(clast - cum),
                                                jnp.broadcast_to(d_ref[...], (CHUNK, hs))], axis=0), spread)
        dt_x, ee_x, ff_x, dsk_x = (wide[i * CHUNK:(i + 1) * CHUNK] for i in range(4))
        xs = x_ref[:, :width]
        xdt = xs * dt_x
        xf = xdt * ff_x
        h_all = h_sc[...]
        hp_ref[0] = h_all
        h_all = h_all.reshape(hs * SSD_P, SSD_N)
        ch_parts = []
        for g in range(SSD_G):
            gsl = slice(g * epg * SSD_P, (g + 1) * epg * SSD_P)
            bb = x_ref[:, width + g * SSD_N: width + (g + 1) * SSD_N].astype(bf16)
            cb_ = x_ref[:, width + gn + g * SSD_N: width + gn + (g + 1) * SSD_N].astype(bf16)
            cbm = lax.dot_general(cb_, bb, NT, preferred_element_type=f32)
            ch_parts.append(_dot(cb_, h_all[gsl], NT))
            st = _dot(xf[:, gsl], bb, TN)
            for e in range(g * epg, (g + 1) * epg):
                esl = slice(e * SSD_P, (e + 1) * SSD_P)
                lmat = jnp.exp(jnp.where(tri, cum[:, e:e + 1] - cum_t[e:e + 1, :], -jnp.inf))
                yd_sc[:, esl] = _dot(cbm * lmat, xdt[:, esl], NN)
                j = e - g * epg
                h_sc[e] = h_sc[e] * dec[:, e:e + 1] + st[j * SSD_P:(j + 1) * SSD_P, :]
        y_ref[...] = yd_sc[...] + jnp.concatenate(ch_parts, axis=1) * ee_x + xs * dsk_x

    par = lambda a: pl.BlockSpec(a.shape, lambda i: (0,) * a.ndim)
    return pl.pallas_call(
        body, name="ssd_fwd",
        out_shape=(jax.ShapeDtypeStruct((t, width), f32), jax.ShapeDtypeStruct((nc, hs, SSD_P, SSD_N), f32)),
        grid=(nc,),
        in_specs=[pl.BlockSpec((CHUNK, cdim), lambda i: (i, 0)), pl.BlockSpec((CHUNK, hs), lambda i: (i, 0)),
                  pl.BlockSpec((hs, CHUNK), lambda i: (0, i)), par(dtb), par(dtb_t), par(alog), par(alog_t), par(dskip)],
        out_specs=(pl.BlockSpec((CHUNK, width), lambda i: (i, 0)),
                   pl.BlockSpec((1, hs, SSD_P, SSD_N), lambda i: (i, 0, 0, 0))),
        scratch_shapes=[pltpu.VMEM((hs, SSD_P, SSD_N), f32), pltpu.VMEM((CHUNK, width), f32)],
        compiler_params=_params(("arbitrary",)),
    )(xbc, dt, dt_t, dtb, dtb_t, alog, alog_t, dskip)


def _ssd_bwd(xbc, dt, dt_t, dtb, dtb_t, alog, alog_t, dskip, hprev, dy, width, comm=None, at=None):
    t, cdim = xbc.shape
    hs = dt.shape[1]
    nc = t // CHUNK
    epg = hs // SSD_G
    gn = SSD_G * SSD_N

    def body(x_ref, dt_ref, dtT_ref, dtb_ref, dtbT_ref, alog_ref, alogT_ref, d_ref, hp_ref, dy_ref,
             dx_ref, ddt_ref, dalog_ref, ddsk_ref, ddtb_ref, dh_sc, dxd_sc):
        @pl.when(pl.program_id(0) == 0)
        def _():
            dh_sc[...] = jnp.zeros_like(dh_sc)
            dalog_ref[...] = jnp.zeros_like(dalog_ref)
            ddsk_ref[...] = jnp.zeros_like(ddsk_ref)
            ddtb_ref[...] = jnp.zeros_like(ddtb_ref)

        tri, raw, dtv, a_neg, cum, cum_t = _ssd_common(dt_ref, dtT_ref, dtb_ref, dtbT_ref, alog_ref, alogT_ref)
        dsk = d_ref[...]
        head_row = lax.broadcasted_iota(jnp.int32, (1, hs), 1)
        head_col = lax.broadcasted_iota(jnp.int32, (hs, 1), 0)
        last_row = (lax.broadcasted_iota(jnp.int32, (CHUNK, 1), 0) == CHUNK - 1).astype(f32)
        shift = SSD_P.bit_length() - 1
        spread = _head_spread(hs, width)
        gather = (lax.shift_right_logical(lax.broadcasted_iota(jnp.int32, (width, hs), 0), shift)
                  == lax.broadcasted_iota(jnp.int32, (width, hs), 1)).astype(bf16)
        clast = cum[CHUNK - 1:CHUNK, :]
        ee = jnp.exp(cum)
        ff = jnp.exp(clast - cum)
        dec = jnp.exp(clast)
        wide = _ones_dot_right(jnp.concatenate([dtv, ee, ff, jnp.broadcast_to(dsk, (CHUNK, hs))], axis=0), spread)
        dt_x, ee_x, ff_x, dsk_x = (wide[i * CHUNK:(i + 1) * CHUNK] for i in range(4))
        xs = x_ref[:, :width]
        dyv = dy_ref[...]
        xdt = xs * dt_x
        dye = dyv * ee_x
        xf = xdt * ff_x
        h_all = hp_ref[0].reshape(hs * SSD_P, SSD_N)
        dh_all = dh_sc[...].reshape(hs * SSD_P, SSD_N)
        hi, lo = _split2(_ones_dot_left(spread, dh_all * h_all))
        ones8 = jnp.ones((8, SSD_N), bf16)
        hh = (lax.dot_general(ones8, hi, NT, preferred_element_type=f32)
              + lax.dot_general(ones8, lo, NT, preferred_element_type=f32))[0:1]
        rowsum_m = jnp.zeros((CHUNK, hs), f32)
        colsum_m = jnp.zeros((hs, CHUNK), f32)
        ch_parts, bds_parts = [], []
        for g in range(SSD_G):
            bsl = slice(width + g * SSD_N, width + (g + 1) * SSD_N)
            csl = slice(width + gn + g * SSD_N, width + gn + (g + 1) * SSD_N)
            gsl = slice(g * epg * SSD_P, (g + 1) * epg * SSD_P)
            bb = x_ref[:, bsl].astype(bf16)
            cb_ = x_ref[:, csl].astype(bf16)
            cbm = lax.dot_general(cb_, bb, NT, preferred_element_type=f32)
            hg = h_all[gsl].astype(bf16)
            dhg = dh_all[gsl].astype(bf16)
            ch_parts.append(_dot(cb_, hg, NT))
            bds_parts.append(_dot(bb, dhg, NT))
            dcg = _dot(dye[:, gsl], hg, NN)
            dbg = _dot(xf[:, gsl], dhg, NN)
            dh_new = _dot(dye[:, gsl], cb_, TN)
            dcb = jnp.zeros((CHUNK, CHUNK), f32)
            for e in range(g * epg, (g + 1) * epg):
                esl = slice(e * SSD_P, (e + 1) * SSD_P)
                lmat = jnp.exp(jnp.where(tri, cum[:, e:e + 1] - cum_t[e:e + 1, :], -jnp.inf))
                gmat = cbm * lmat
                dy_e = dyv[:, esl].astype(bf16)
                dgm = _dot(dy_e, xdt[:, esl], NT)
                dxd_sc[:, esl] = _dot(gmat, dy_e, TN)
                dcb = dcb + dgm * lmat
                mm = dgm * gmat
                rowsum_m = rowsum_m + jnp.sum(mm, axis=1, keepdims=True) * (head_row == e).astype(f32)
                colsum_m = colsum_m + (head_col == e).astype(f32) * jnp.sum(mm, axis=0, keepdims=True)
                j = e - g * epg
                dh_sc[e] = dh_new[j * SSD_P:(j + 1) * SSD_P, :] + dec[:, e:e + 1] * dh_sc[e]
            dx_ref[:, bsl] = dbg + _dot(dcb, cb_, TN)
            dx_ref[:, csl] = dcg + _dot(dcb, bb, NN)
        ch_all = jnp.concatenate(ch_parts, axis=1)
        bds_all = jnp.concatenate(bds_parts, axis=1)
        dxdt = bds_all * ff_x + dxd_sc[...]
        dx_ref[:, :width] = dxdt * dt_x + dyv * dsk_x
        sums = _ones_dot_right(jnp.concatenate([dxdt * xs, dyv * ch_all, bds_all * xdt, dyv * xs], axis=0), gather)
        ddtx_all = sums[0:CHUNK]
        dff = sums[2 * CHUNK:3 * CHUNK] * ff
        ddsk = jnp.sum(sums[3 * CHUNK:4 * CHUNK], axis=0, keepdims=True)
        dclast = jnp.sum(dff, axis=0, keepdims=True) + dec * hh
        eye = (lax.broadcasted_iota(jnp.int32, (hs, hs), 0) == lax.broadcasted_iota(jnp.int32, (hs, hs), 1)).astype(bf16)
        colsum_t = sum(lax.dot_general(v, eye, TN, preferred_element_type=f32) for v in _split3(colsum_m))
        dcum_all = sums[CHUNK:2 * CHUNK] * ee - dff + rowsum_m - colsum_t + dclast * last_row
        ii = lax.broadcasted_iota(jnp.int32, (CHUNK, CHUNK), 0)
        jj = lax.broadcasted_iota(jnp.int32, (CHUNK, CHUNK), 1)
        da = _ones_dot_left((jj >= ii).astype(bf16), dcum_all)
        ddt = da * a_neg + ddtx_all
        dalog_ref[...] += jnp.sum(da * dtv, axis=0, keepdims=True) * a_neg
        draw = ddt * _sigmoid(raw)
        place = (lax.broadcasted_iota(jnp.int32, (hs, LANES), 0) + ROPE
                 == lax.broadcasted_iota(jnp.int32, (hs, LANES), 1)).astype(bf16)
        ddt_ref[...] = _ones_dot_right(draw, place)
        ddtb_ref[...] += jnp.sum(draw, axis=0, keepdims=True)
        ddsk_ref[...] += ddsk

    rev = lambda i: nc - 1 - i
    par = lambda a: pl.BlockSpec(a.shape, lambda i: (0,) * a.ndim)
    acc = pl.BlockSpec((1, hs), lambda i: (0, 0))
    acc_shape = jax.ShapeDtypeStruct((1, hs), f32)
    res, cres = _call(
        body, "ssd_bwd",
        [jax.ShapeDtypeStruct((t, cdim), f32), jax.ShapeDtypeStruct((t, LANES), f32), acc_shape, acc_shape, acc_shape],
        (nc,),
        [pl.BlockSpec((CHUNK, cdim), lambda i: (rev(i), 0)), pl.BlockSpec((CHUNK, hs), lambda i: (rev(i), 0)),
         pl.BlockSpec((hs, CHUNK), lambda i: (0, rev(i))), par(dtb), par(dtb_t), par(alog), par(alog_t),
         par(dskip), pl.BlockSpec((1, hs, SSD_P, SSD_N), lambda i: (rev(i), 0, 0, 0)),
         pl.BlockSpec((CHUNK, width), lambda i: (rev(i), 0))],
        [pl.BlockSpec((CHUNK, cdim), lambda i: (rev(i), 0)), pl.BlockSpec((CHUNK, LANES), lambda i: (rev(i), 0)),
         acc, acc, acc],
        [pltpu.VMEM((hs, SSD_P, SSD_N), f32), pltpu.VMEM((CHUNK, width), f32)], ("arbitrary",),
        [xbc, dt, dt_t, dtb, dtb_t, alog, alog_t, dskip, hprev, dy], comm, at)
    return (*res, cres)


def _where_am_i():
    x, y, c = lax.axis_index("x"), lax.axis_index("y"), lax.axis_index("c")
    chips = [(1 - x, y), (x, 1 - y), (1 - x, 1 - y)]
    return x, y, c, chips


def _remote(src, dst, send_sems, recv_sems, k, to):
    return pltpu.make_async_remote_copy(src_ref=src, dst_ref=dst, send_sem=send_sems.at[k], recv_sem=recv_sems.at[k],
                                        device_id=to, device_id_type=MESH)


class _Comm:
    def __init__(self, args, outs, nsem, nphase, run):
        self.args, self.outs, self.nsem, self.nphase, self.run = list(args), list(outs), nsem, nphase, run

    def sems(self):
        return [pltpu.SemaphoreType.DMA((self.nsem,)), pltpu.SemaphoreType.DMA((self.nsem,))]


def _comm_call(comm, name):
    na, no = len(comm.args), len(comm.outs)

    def body(*refs):
        for phase in range(comm.nphase):
            comm.run(phase, refs[:na], refs[na:na + no], refs[na + no], refs[na + no + 1])

    return list(pl.pallas_call(body, name=name, out_shape=comm.outs, in_specs=[ANY] * na, out_specs=[ANY] * no,
                               scratch_shapes=comm.sems())(*comm.args))


def _half(ref, c, r2):
    return ref.at[pl.ds(c * r2, r2)]


def _gather_weights(shards, wholes):
    ns, nw = len(shards), len(wholes)
    per = 7

    def run(phase, srcs, outs, send_sems, recv_sems):
        x, y, c, chips = _where_am_i()
        me, sib = 2 * x + y, (x, y, 1 - c)
        def first():
            cps = []
            for w in range(ns + nw):
                src, out, base = srcs[w], outs[w], per * w
                halved = w < ns
                r2 = src.shape[0] // 2
                piece = _half(src, c, r2) if halved else src
                for j, (cx, cy) in enumerate(chips):
                    dst = _half(out.at[me], c, r2) if halved else out.at[me]
                    cps.append(_remote(piece, dst, send_sems, recv_sems, base + j, (cx, cy, c)))
                cps.append(_remote(src, out.at[me], send_sems, recv_sems, base + 6, sib))
            return cps

        def passed():
            cps = []
            for w in range(ns):
                r2 = srcs[w].shape[0] // 2
                for j, (cx, cy) in enumerate(chips):
                    got = _half(outs[w].at[2 * cx + cy], c, r2)
                    cps.append(_remote(got, got, send_sems, recv_sems, per * w + 3 + j, sib))
            return cps

        if phase == 0:
            for cp in first():
                cp.start()
        elif phase == 1:
            it = iter(passed())
            for w in range(ns + nw):
                src, out, base = srcs[w], outs[w], per * w
                r2 = src.shape[0] // 2
                for j, (cx, cy) in enumerate(chips):
                    got = _half(out.at[2 * cx + cy], c, r2) if w < ns else out.at[2 * cx + cy]
                    _remote(got, got, send_sems, recv_sems, base + j, sib).wait_recv()
                    if w < ns:
                        next(it).start()
        else:
            for w in range(ns + nw):
                src, out, base = srcs[w], outs[w], per * w
                r2 = src.shape[0] // 2
                if w < ns:
                    for j, (cx, cy) in enumerate(chips):
                        got = _half(out.at[2 * cx + cy], 1 - c, r2)
                        _remote(got, got, send_sems, recv_sems, base + 3 + j, sib).wait_recv()
                _remote(src, out.at[me], send_sems, recv_sems, base + 6, sib).wait_recv()
            for cp in first() + passed():
                cp.wait_send()

    args = list(shards) + list(wholes)
    outs = [jax.ShapeDtypeStruct((4,) + a.shape, a.dtype) for a in args]
    return _Comm(args, outs, per * len(args), 3, run)


def _sibling_send_halves(gs):
    n = len(gs)

    def run(phase, srcs, outs, send_sems, recv_sems):
        x, y, c, _ = _where_am_i()
        sib = (x, y, 1 - c)
        cps = []
        for w in range(n):
            r2 = srcs[w].shape[1] // 2
            for k in range(4):
                cps.append(_remote(_half(srcs[w].at[k], 1 - c, r2), outs[w].at[k], send_sems, recv_sems, 4 * w + k, sib))
        for cp in cps:
            cp.start() if phase == 0 else cp.wait()

    outs = [jax.ShapeDtypeStruct((4, g.shape[1] // 2, g.shape[2]), g.dtype) for g in gs]
    return _Comm(gs, outs, 4 * n, 2, run)


def _chips_exchange(ps):
    n = len(ps)

    def run(phase, srcs, outs, send_sems, recv_sems):
        x, y, c, chips = _where_am_i()
        me = 2 * x + y
        cps = []
        for w in range(n):
            for j, (cx, cy) in enumerate(chips):
                cps.append(_remote(srcs[w].at[2 * cx + cy], outs[w].at[me], send_sems, recv_sems, 3 * w + j, (cx, cy, c)))
        if phase == 0:
            for cp in cps:
                cp.start()
        else:
            for w in range(n):
                for j, (cx, cy) in enumerate(chips):
                    got = outs[w].at[2 * cx + cy]
                    _remote(got, got, send_sems, recv_sems, 3 * w + j, (cx, cy, c)).wait_recv()
            for cp in cps:
                cp.wait_send()

    outs = [jax.ShapeDtypeStruct(p.shape, p.dtype) for p in ps]
    return _Comm(ps, outs, 3 * n, 2, run)


def _sibling_swap(rs):
    n = len(rs)

    def run(phase, srcs, outs, send_sems, recv_sems):
        x, y, c, _ = _where_am_i()
        for w in range(n):
            cp = _remote(srcs[w], outs[w], send_sems, recv_sems, w, (x, y, 1 - c))
            cp.start() if phase == 0 else cp.wait()

    outs = [jax.ShapeDtypeStruct(r.shape, r.dtype) for r in rs]
    return _Comm(rs, outs, n, 2, run)


def _gather_all(v):
    rows = v.shape[0]

    def body(x_ref, out_ref, send_sems, recv_sems, local_sem):
        x, y, c, chips = _where_am_i()
        me, sib = (x, y, c), (x, y, 1 - c)
        blk = lambda px, py, pc: out_ref.at[4 * px + 2 * py + pc]
        mine = pltpu.make_async_copy(x_ref, blk(*me), local_sem)
        mine.start()
        first = [_remote(x_ref, blk(*me), send_sems, recv_sems, 0, sib)]
        first += [_remote(x_ref, blk(*me), send_sems, recv_sems, 1 + j, (*chip, c)) for j, chip in enumerate(chips)]
        for cp in first:
            cp.start()
        passed = [_remote(blk(*chip, c), blk(*chip, c), send_sems, recv_sems, 4 + j, sib) for j, chip in enumerate(chips)]
        for j, chip in enumerate(chips):
            _remote(blk(*chip, c), blk(*chip, c), send_sems, recv_sems, 1 + j, me).wait_recv()
            passed[j].start()
        _remote(blk(*sib), blk(*sib), send_sems, recv_sems, 0, me).wait_recv()
        for j, chip in enumerate(chips):
            _remote(blk(*chip, 1 - c), blk(*chip, 1 - c), send_sems, recv_sems, 4 + j, me).wait_recv()
        for cp in first + passed:
            cp.wait_send()
        mine.wait()

    vm = pl.BlockSpec(memory_space=pltpu.VMEM)
    return pl.pallas_call(
        body, name="gather_all", out_shape=jax.ShapeDtypeStruct((8, rows, LANES), v.dtype),
        in_specs=[vm], out_specs=vm,
        scratch_shapes=[pltpu.SemaphoreType.DMA((7,)), pltpu.SemaphoreType.DMA((7,)), pltpu.SemaphoreType.DMA],
    )(v)


def _flat_pad(parts, total):
    v = jnp.concatenate([p.reshape(-1) for p in parts])
    return jnp.pad(v, (0, total - v.shape[0]))


def _adamw(w, g, m, v):
    m = ADAM_B1 * m + (1.0 - ADAM_B1) * g
    v = ADAM_B2 * v + (1.0 - ADAM_B2) * jnp.square(g)
    m_hat = m / (1.0 - ADAM_B1 ** ADAM_STEP)
    v_hat = v / (1.0 - ADAM_B2 ** ADAM_STEP)
    delta = -ADAM_LR * (m_hat / (jnp.sqrt(v_hat) + ADAM_EPS) + ADAM_WD * w)
    return delta, m, v


def _adamw_call(w, g, m, v, name):
    r, cdim = w.shape
    tm = _tile(r, ROW_TILE, 8)
    o = _rout(r, cdim, f32, tm)
    return _rowwise(_adamw, r, tm, [_rows(w, tm), _rows(g, tm), _rows(m, tm), _rows(v, tm)], [o, o, o], [], name)


def _adamw_halves(w, m, v, g_mine, g_sib, sp, name):
    _, r, cdim = w.shape
    r2 = r // 2
    tr = _row_tile(r2, cdim, 8)
    nb = r2 // tr

    def body(sp_ref, w_ref, m_ref, v_ref, ga_ref, gb_ref, g_out, d_out, m_out, v_out):
        g = jnp.where(pl.program_id(0) == sp_ref[1], ga_ref[...], gb_ref[...])
        delta, mn, vn = _adamw(w_ref[...], g, m_ref[...], v_ref[...])
        g_out[...] = g
        d_out[...] = delta
        m_out[...] = mn
        v_out[...] = vn

    full = pl.BlockSpec((None, tr, cdim), lambda h, i, s: (0, h * nb + i, 0))
    mine = pl.BlockSpec((tr, cdim), lambda h, i, s: (jnp.where(h == s[1], i, 0), 0))
    sib = pl.BlockSpec((tr, cdim), lambda h, i, s: (jnp.where(h == s[1], 0, i), 0))
    out = jax.ShapeDtypeStruct((1, r, cdim), f32)
    gs = pltpu.PrefetchScalarGridSpec(num_scalar_prefetch=1, grid=(2, nb), in_specs=[full, full, full, mine, sib],
                                      out_specs=[full, full, full, full])
    return pl.pallas_call(body, name=name, out_shape=[out, out, out, out], grid_spec=gs,
                          compiler_params=_params(("parallel", "parallel")))(sp, w, m, v, g_mine, g_sib)


MIXER = ("w_in", "w_uq", "w_ukv", "w_out")
FFN = ("w_gate", "w_up", "w_down")
BIG = MIXER + FFN
SMALL = ("q_norm_w", "kv_norm_w", "conv_w", "conv_b", "dt_bias", "a_log", "d_skip", "ssd_norm_w", "attn_out_norm_w",
         "pre_mix_norm_w", "post_mix_norm_w", "pre_ffn_norm_w", "post_ffn_norm_w")
ORDER = ("w_in", "q_norm_w", "w_uq", "kv_norm_w", "w_ukv", "conv_w", "conv_b", "dt_bias", "a_log", "d_skip",
         "ssd_norm_w", "attn_out_norm_w", "w_out", "pre_mix_norm_w", "post_mix_norm_w", "pre_ffn_norm_w",
         "post_ffn_norm_w", "w_gate", "w_up", "w_down")


def kernel(x, positions, w_in, q_norm_w, w_uq, kv_norm_w, w_ukv, conv_w, conv_b, dt_bias, a_log, d_skip, ssd_norm_w, attn_out_norm_w, w_out, pre_mix_norm_w, post_mix_norm_w, pre_ffn_norm_w, post_ffn_norm_w, w_gate, w_up, w_down, loss_target, m_w_in, m_q_norm_w, m_w_uq, m_kv_norm_w, m_w_ukv, m_conv_w, m_conv_b, m_dt_bias, m_a_log, m_d_skip, m_ssd_norm_w, m_attn_out_norm_w, m_w_out, m_pre_mix_norm_w, m_post_mix_norm_w, m_pre_ffn_norm_w, m_post_ffn_norm_w, m_w_gate, m_w_up, m_w_down, v_w_in, v_q_norm_w, v_w_uq, v_kv_norm_w, v_w_ukv, v_conv_w, v_conv_b, v_dt_bias, v_a_log, v_d_skip, v_ssd_norm_w, v_attn_out_norm_w, v_w_out, v_pre_mix_norm_w, v_post_mix_norm_w, v_pre_ffn_norm_w, v_post_ffn_norm_w, v_w_gate, v_w_up, v_w_down):
    local = dict(locals())
    wts = {n: local[n][0] for n in ORDER}
    mom_m = {n: local["m_" + n][0] for n in ORDER}
    mom_v = {n: local["v_" + n][0] for n in ORDER}
    xs = x[0]
    tgt = loss_target[0]
    t, d = xs.shape
    nchip = 4
    my_x, my_y, my_c = lax.axis_index("x"), lax.axis_index("y"), lax.axis_index("c")
    my_chip = 2 * my_x + my_y

    mla_w = d // 2
    nh = mla_w // V_HEAD
    width = d - mla_w
    hs = width // SSD_P
    gn = SSD_G * SSD_N
    cdim = width + 2 * gn
    in_sizes = (Q_RANK, KV_RANK, ROPE, width, cdim, hs)
    d_in = sum(in_sizes)
    tail = LANES
    off_xbc = 0
    off_cq = cdim
    off_ckv = off_cq + Q_RANK
    off_z = off_ckv + KV_RANK
    off_tail = off_z + width
    d_in_p = _round_up(off_tail + tail, 256)
    gw = width // SSD_G
    assert off_cq % Q_RANK == 0 and off_ckv % KV_RANK == 0 and off_z % gw == 0 and off_tail % LANES == 0
    qk_head = NOPE + ROPE
    scale = qk_head ** -0.5
    tm = _tile(t, ROW_TILE, 8)
    tmw = _tile(t, ROW_TILE // 2, 8)

    sp = jnp.stack([my_chip, my_c]).astype(jnp.int32)

    def pair_sums(names, gl, from_sib):
        res = []
        for n, g, fs in zip(names, gl, from_sib):
            _, r2, cs = fs.shape
            tr = _row_tile(r2, cs, 16)
            nb = r2 // tr
            res.append(_blocked(
                lambda a, b: a + b, (nchip, nb),
                [(g, (None, tr, cs), lambda k, i, s, nb=nb: (k, s[1] * nb + i, 0)),
                 (fs, (None, tr, cs), lambda k, i, s: (k, i, 0))],
                [((nchip, r2, cs), bf16, (None, tr, cs), lambda k, i, s: (k, i, 0))], [], "pair_sum_" + n, sp=sp)[0])
        return res

    def chip_sums(names, gl, from_sib, from_chips):
        res = []
        for n, g, fs, fc in zip(names, gl, from_sib, from_chips):
            _, r2, cs = fs.shape
            tr = _row_tile(r2, cs, 16)
            nb = r2 // tr
            res.append(_blocked(
                lambda a, b, r1, r2_, r3: ((a + b) + r1.astype(f32)) + r2_.astype(f32) + r3.astype(f32), (nb,),
                [(g, (None, tr, cs), lambda i, s, nb=nb: (s[0], s[1] * nb + i, 0)),
                 (fs, (None, tr, cs), lambda i, s: (s[0], i, 0)),
                 (fc, (None, tr, cs), lambda i, s: (s[0] ^ 1, i, 0)),
                 (fc, (None, tr, cs), lambda i, s: (s[0] ^ 2, i, 0)),
                 (fc, (None, tr, cs), lambda i, s: (s[0] ^ 3, i, 0))],
                [((r2, cs), f32, (tr, cs), lambda i, s: (i, 0))], [], "chip_sum_" + n, sp=sp)[0])
        return res

    ck, ccs = wts["conv_w"].shape
    cs_in = wts["w_in"].shape[1]
    flat_t = lambda a3: jnp.swapaxes(a3, 1, 2).reshape(1, cs_in * d // LANES, LANES)
    w_in_g, conv_g = _comm_call(_gather_weights([flat_t(w_in)[0].astype(bf16)], [wts["conv_w"]]), "gather_w_in")
    cat_cols = lambda g: jnp.concatenate([g[k] for k in range(nchip)], axis=1)
    conv_full = cat_cols(conv_g)
    mixer_gather = _gather_weights([wts[n].astype(bf16) for n in MIXER[1:]], [])
    ffn_gather = _gather_weights([wts[n].astype(bf16) for n in FFN], [])

    wi = w_in_g.reshape(nchip * cs_in, d)
    o = np.cumsum((0,) + in_sizes)
    seg = lambda i: wi[o[i]:o[i + 1]]
    w_in_pt = jnp.concatenate([seg(4), seg(0), seg(1), seg(3), seg(2), seg(5),
                               jnp.zeros((d_in_p - off_tail - ROPE - hs, d), bf16)], axis=0)

    inv_freq = ROPE_THETA ** (-jnp.arange(0, ROPE, 2, dtype=f32) / ROPE)
    ang = positions[0].astype(f32)[:, None] * inv_freq
    cos, sin = jnp.cos(ang), jnp.sin(ang)
    z32, z64, z96 = jnp.zeros((t, 32), f32), jnp.zeros((t, 64), f32), jnp.zeros((t, 96), f32)
    cosp = jnp.concatenate([cos, cos, z64], axis=1)
    sina = jnp.concatenate([-sin, z96], axis=1)
    sinb = jnp.concatenate([z32, sin, z64], axis=1)

    row = lambda a: a.reshape(1, -1)
    w_pre_mix, w_post_mix = row(wts["pre_mix_norm_w"]), row(wts["post_mix_norm_w"])
    w_pre_ffn, w_post_ffn = row(wts["pre_ffn_norm_w"]), row(wts["post_ffn_norm_w"])
    w_qn, w_kvn = row(wts["q_norm_w"]), row(wts["kv_norm_w"])
    w_attn_n, w_ssd_n = row(wts["attn_out_norm_w"]), row(wts["ssd_norm_w"])
    conv_b_r = row(wts["conv_b"])
    dtb, alog, dskip = row(wts["dt_bias"]), row(wts["a_log"]), row(wts["d_skip"])
    dtb_t, alog_t = dtb.reshape(hs, 1), alog.reshape(hs, 1)

    u = _rowwise(lambda a, w: _rms(a, w), t, tm, [_rows(xs, tm), _par(w_pre_mix)], [_rout(t, d, bf16, tm)], [],
                 "pre_mix_norm")[0]
    proj, (w_uq_g, w_ukv_f, w_out_g) = _matmul([(u, w_in_pt)], "nt", f32, "in_proj", comm=mixer_gather,
                                               at=(0.0, 0.6))
    w_uq_p = jnp.pad(cat_cols(w_uq_g).reshape(Q_RANK, nh, qk_head), ((0, 0), (0, 0), (0, QK_PAD - qk_head))
                     ).reshape(Q_RANK, nh * QK_PAD)
    w_out_f = w_out_g.reshape(-1, w_out_g.shape[2])
    cq_in = _rows(proj, tm, Q_RANK, off_cq // Q_RANK)
    ckv_in = _rows(proj, tm, KV_RANK, off_ckv // KV_RANK)
    cqn = _rowwise(lambda a, w: _rms(a, w), t, tm, [cq_in, _par(w_qn)], [_rout(t, Q_RANK, bf16, tm)], [], "q_norm")[0]
    ckvn = _rowwise(lambda a, w: _rms(a, w), t, tm, [ckv_in, _par(w_kvn)], [_rout(t, KV_RANK, bf16, tm)], [],
                    "kv_norm")[0]
    q_raw = _matmul([(cqn, w_uq_p)], "nn", f32, "q_up")
    kv = _matmul([(ckvn, w_ukv_f)], "nn", bf16, "kv_up")

    def q_rope_fn(qt, cp, sa, sb):
        parts = []
        for h in range(nh):
            parts.append(qt[:, h * QK_PAD: h * QK_PAD + NOPE])
            parts.append(_rope(qt[:, h * QK_PAD + NOPE:(h + 1) * QK_PAD], cp, sa, sb))
        return jnp.concatenate(parts, axis=1)

    tail_cb = off_tail // LANES
    q2 = _rowwise(q_rope_fn, t, tm, [_rows(q_raw, tm), _rows(cosp, tm), _rows(sina, tm), _rows(sinb, tm)],
                  [_rout(t, nh * QK_PAD, bf16, tm)], [], "q_rope")[0]
    kr2 = _rowwise(_rope, t, tm, [_rows(proj, tm, LANES, tail_cb), _rows(cosp, tm), _rows(sina, tm), _rows(sinb, tm)],
                   [_rout(t, LANES, bf16, tm)], [], "k_rope")[0]
    o_att, lse, ffn_w = _flash_fwd(q2, kv, kr2, nh, scale, ffn_gather, (0.0, 0.75))
    w_gate_f, w_up_f = ffn_w[0], ffn_w[1]
    w_down_f = ffn_w[2].reshape(-1, ffn_w[2].shape[2])

    xbc_act = _conv_fwd(proj, conv_full, conv_b_r, cdim, tm)
    dt_raw, dt_raw_t = _blocked(
        lambda blk: (blk[:, ROPE:ROPE + hs], blk.T[ROPE:ROPE + hs, :]), (t // tm,),
        [_rows(proj, tm, LANES, tail_cb)],
        [((t, hs), f32, (tm, hs), lambda i: (i, 0)), ((hs, t), f32, (hs, tm), lambda i: (0, i))], [], "dt_split")
    y_ssd, hprev = _ssd_fwd(xbc_act, dt_raw, dt_raw_t, dtb, dtb_t, alog, alog_t, dskip, width)
    z_ins = [_rows(proj, tm, gw, off_z // gw + i) for i in range(SSD_G)]

    def mix_norms_fn(ov, yv, *rest):
        zs, wa, ws = rest[:SSD_G], rest[SSD_G], rest[SSD_G + 1]
        outs = [_rms(ov, wa)]
        for i in range(SSD_G):
            sl = slice(i * gw, (i + 1) * gw)
            outs.append(_rms(yv[:, sl] * (zs[i] * _sigmoid(zs[i])), ws[:, sl]))
        return jnp.concatenate(outs, axis=1)

    cat = _rowwise(mix_norms_fn, t, tm, [_rows(o_att, tm), _rows(y_ssd, tm)] + z_ins + [_par(w_attn_n), _par(w_ssd_n)],
                   [_rout(t, d, bf16, tm)], [], "attn_ssd_out_norms")[0]
    mix = _matmul([(cat, w_out_f)], "nn", f32, "out_proj")

    def post_mix_fn(mx, xv, w1, w2):
        h1v = xv + _rms(mx, w1)
        return h1v, _rms(h1v, w2)

    h1, v_in = _rowwise(post_mix_fn, t, tmw, [_rows(mix, tmw), _rows(xs, tmw), _par(w_post_mix), _par(w_pre_ffn)],
                        [_rout(t, d, f32, tmw), _rout(t, d, bf16, tmw)], [], "post_mix_pre_ffn_norm")
    g_ff, u_ff, act = _ffn_up(v_in, w_gate_f, w_up_f)
    ffn = _matmul([(act, w_down_f)], "nn", f32, "ffn_down")

    def final_fn(fv, h1v, tg, w):
        h2 = h1v + _rms(fv, w)
        err = h2 - tg
        lpart = 0.5 * jnp.sum(jnp.sum(err * err, axis=1, keepdims=True), axis=0, keepdims=True) / d
        dh2 = err / d
        dff, dw = _rms_bwd(fv, w, dh2)
        return dff, dh2, jnp.broadcast_to(lpart, (1, LANES)), dw

    dffn, dh2, loss_acc, g_post_ffn = _rowwise(
        final_fn, t, tmw, [_rows(ffn, tmw), _rows(h1, tmw), _rows(tgt, tmw), _par(w_post_ffn)],
        [_rout(t, d, bf16, tmw), _rout(t, d, f32, tmw)], [(1, LANES), (1, d)], "loss_post_ffn_norm_bwd")
    loss = lax.psum(loss_acc[0, 0], ("x", "y", "c"))

    dg_ff, du_ff = _ffn_down_bwd(dffn, w_down_f, g_ff, u_ff)
    gw_down = _matmul([(act, dffn)], "tn", f32, "grad_w_down")
    gw_gate = _matmul([(v_in, dg_ff)], "tn", f32, "grad_w_gate", out_chunks=nchip)
    gw_up = _matmul([(v_in, du_ff)], "tn", f32, "grad_w_up", out_chunks=nchip)
    ffn_g = [gw_gate, gw_up, gw_down.reshape(nchip, -1, d)]
    dv_in, ffn_from_sib = _matmul([(dg_ff, w_gate_f), (du_ff, w_up_f)], "nt", f32, "ffn_up_bwd",
                                  comm=_sibling_send_halves(ffn_g), at=(0.0,))
    ffn_pairs = pair_sums(FFN, ffn_g, ffn_from_sib)

    def mid_bwd_fn(h1v, dvv, dh2v, mx, w_pf, w_pm):
        dxn, dw_pf = _rms_bwd(h1v, w_pf, dvv)
        dh1v = dh2v + dxn
        dmx, dw_pm = _rms_bwd(mx, w_pm, dh1v)
        return dh1v, dmx, dw_pf, dw_pm

    dh1, dmix, g_pre_ffn, g_post_mix = _rowwise(
        mid_bwd_fn, t, tmw, [_rows(h1, tmw), _rows(dv_in, tmw), _rows(dh2, tmw), _rows(mix, tmw),
                             _par(w_pre_ffn), _par(w_post_mix)],
        [_rout(t, d, f32, tmw), _rout(t, d, bf16, tmw)], [(1, d), (1, d)], "pre_ffn_post_mix_norm_bwd")
    dcat = _matmul([(dmix, w_out_f)], "nt", f32, "out_proj_bwd")
    gw_out = _matmul([(cat, dmix)], "tn", f32, "grad_w_out")

    def mix_norms_bwd_fn(ov, yv, *rest):
        zs, dcv, wa, ws = rest[:SSD_G], rest[SSD_G], rest[SSD_G + 1], rest[SSD_G + 2]
        dov, dwa = _rms_bwd(ov, wa, dcv[:, :mla_w])
        dl = [jnp.broadcast_to(jnp.sum(dov[:, h * V_HEAD:(h + 1) * V_HEAD] * ov[:, h * V_HEAD:(h + 1) * V_HEAD],
                                       axis=1, keepdims=True), (ov.shape[0], V_HEAD)) for h in range(nh)]
        dys, dzs, dws = [], [], []
        for i in range(SSD_G):
            sl = slice(i * gw, (i + 1) * gw)
            zv, yi = zs[i], yv[:, sl]
            sg = _sigmoid(zv)
            sz = zv * sg
            dgi, dwi = _rms_bwd(yi * sz, ws[:, sl], dcv[:, mla_w + i * gw: mla_w + (i + 1) * gw])
            dys.append(dgi * sz)
            dzs.append(dgi * yi * (sg * (1.0 + zv * (1.0 - sg))))
            dws.append(dwi)
        cc = lambda a: jnp.concatenate(a, axis=1)
        return dov, cc(dl), cc(dys), cc(dzs), dwa, cc(dws)

    do_att, delta, dy_ssd, dz, g_attn_n, g_ssd_n = _rowwise(
        mix_norms_bwd_fn, t, tm,
        [_rows(o_att, tm), _rows(y_ssd, tm)] + z_ins + [_rows(dcat, tm), _par(w_attn_n), _par(w_ssd_n)],
        [_rout(t, mla_w, bf16, tm), _rout(t, mla_w, f32, tm), _rout(t, width, f32, tm), _rout(t, width, bf16, tm)],
        [(1, mla_w), (1, width)], "attn_ssd_out_norms_bwd")
    dkv, dkr_h, dq2, ffn_from_chips = _flash_bwd(q2, kv, kr2, do_att, lse, delta, nh, scale,
                                                 _chips_exchange(ffn_pairs), (0.0,))
    ffn_halves = chip_sums(FFN, ffn_g, ffn_from_sib, ffn_from_chips)

    def q_rope_bwd_fn(dqt, cp, sa, sb):
        parts = []
        for h in range(nh):
            parts.append(dqt[:, h * QK_PAD: h * QK_PAD + NOPE])
            parts.append(_rope_bwd(dqt[:, h * QK_PAD + NOPE:(h + 1) * QK_PAD], cp, sa, sb))
        return jnp.concatenate(parts, axis=1)

    dq_raw = _rowwise(q_rope_bwd_fn, t, tm, [_rows(dq2, tm), _rows(cosp, tm), _rows(sina, tm), _rows(sinb, tm)],
                      [_rout(t, nh * QK_PAD, bf16, tm)], [], "q_rope_bwd")[0]

    def k_rope_bwd_fn(dk, cp, sa, sb):
        tot = dk[:, 0:LANES]
        for h in range(1, nh):
            tot = tot + dk[:, h * LANES:(h + 1) * LANES]
        return _rope_bwd(tot, cp, sa, sb)

    dkr = _rowwise(k_rope_bwd_fn, t, tm, [_rows(dkr_h, tm), _rows(cosp, tm), _rows(sina, tm), _rows(sinb, tm)],
                   [_rout(t, LANES, f32, tm)], [], "k_rope_bwd")[0]
    gw_uq_p = _matmul([(cqn, dq_raw)], "tn", f32, "grad_w_uq")
    gw_ukv = _matmul([(ckvn, dkv)], "tn", f32, "grad_w_ukv", out_chunks=nchip)
    dcqn = _matmul([(dq_raw, w_uq_p)], "nt", f32, "q_up_bwd")
    dckvn = _matmul([(dkv, w_ukv_f)], "nt", f32, "kv_up_bwd")

    def lat_norm_bwd_fn(a, w, dyv):
        return _rms_bwd(a, w, dyv)

    dcq, g_qn = _rowwise(lat_norm_bwd_fn, t, tm, [cq_in, _par(w_qn), _rows(dcqn, tm)],
                         [_rout(t, Q_RANK, bf16, tm)], [(1, Q_RANK)], "q_norm_bwd")
    dckv, g_kvn = _rowwise(lat_norm_bwd_fn, t, tm, [ckv_in, _par(w_kvn), _rows(dckvn, tm)],
                           [_rout(t, KV_RANK, bf16, tm)], [(1, KV_RANK)], "kv_norm_bwd")

    dxbc_act, ddt_raw, g_alog, g_dskip, g_dtb, ffn_sib_halves = _ssd_bwd(
        xbc_act, dt_raw, dt_raw_t, dtb, dtb_t, alog, alog_t, dskip, hprev, dy_ssd, width,
        _sibling_swap(ffn_halves), (0.0,))
    dpre, gcw0, gcw1, gcw2, gcw3, g_conv_b = _conv_bwd_pre(proj, conv_full, conv_b_r, dxbc_act, cdim, tm)
    g_conv_w = jnp.concatenate([gcw0, gcw1, gcw2, gcw3], axis=0)
    dxbc = _conv_bwd_dx(dpre, conv_full, tm)

    def dproj_fn(a, b, c, e, kr_blk, dt_blk):
        parts = [a, b, c, e, (kr_blk + dt_blk).astype(bf16)]
        if d_in_p > off_tail + tail:
            parts.append(jnp.zeros((a.shape[0], d_in_p - off_tail - tail), bf16))
        return jnp.concatenate(parts, axis=1)

    dproj = _rowwise(dproj_fn, t, tm, [_rows(dxbc, tm), _rows(dcq, tm), _rows(dckv, tm), _rows(dz, tm),
                                       _rows(dkr, tm), _rows(ddt_raw, tm)],
                     [_rout(t, d_in_p, bf16, tm)], [], "in_proj_grad_pack")[0]
    split_cols = lambda gf: jnp.stack(jnp.split(gf, nchip, axis=1))
    split_rows = lambda gf: gf.reshape(nchip, gf.shape[0] // nchip, gf.shape[1])
    gw_uq = gw_uq_p.reshape(Q_RANK, nh, QK_PAD)[:, :, :qk_head].reshape(Q_RANK, nh * qk_head)
    rest_g = [split_cols(gw_uq), gw_ukv, split_rows(gw_out)]
    du_in, rest_from_sib = _matmul([(dproj, w_in_pt)], "nn", f32, "in_proj_bwd",
                                   comm=_sibling_send_halves(rest_g), at=(0.0,))
    rest_pairs = pair_sums(MIXER[1:], rest_g, rest_from_sib)
    gw_in_pt, rest_from_chips = _matmul([(dproj, u)], "tn", f32, "grad_w_in",
                                        comm=_chips_exchange(rest_pairs), at=(0.0,))
    rest_halves = chip_sums(MIXER[1:], rest_g, rest_from_sib, rest_from_chips)

    def first_bwd_fn(xv, duv, dh1v, w):
        dxn, dw = _rms_bwd(xv, w, duv)
        return dh1v + dxn, dw

    grad_x, g_pre_mix = _rowwise(first_bwd_fn, t, tmw, [_rows(xs, tmw), _rows(du_in, tmw), _rows(dh1, tmw),
                                                         _par(w_pre_mix)],
                                 [_rout(t, d, f32, tmw)], [(1, d)], "pre_mix_norm_bwd")

    gseg = lambda a, b: gw_in_pt[a:b]
    gw_in_t = jnp.concatenate([gseg(off_cq, off_ckv), gseg(off_ckv, off_z), gseg(off_tail, off_tail + ROPE),
                               gseg(off_z, off_tail), gseg(off_xbc, off_cq),
                               gseg(off_tail + ROPE, off_tail + ROPE + hs)], axis=0)
    in_g = [gw_in_t.reshape(nchip, cs_in * d // LANES, LANES)]
    in_from_sib = _comm_call(_sibling_send_halves(in_g), "sibling_send_halves")
    in_pairs = pair_sums(MIXER[:1], in_g, in_from_sib)
    in_from_chips = _comm_call(_chips_exchange(in_pairs), "chips_exchange")
    mix_halves = chip_sums(MIXER[:1], in_g, in_from_sib, in_from_chips) + rest_halves
    mix_sib_halves = _comm_call(_sibling_swap(mix_halves), "sibling_swap")
    halves = mix_halves + ffn_halves
    sib_halves = mix_sib_halves + ffn_sib_halves
    gshard = {}

    gsmall = {"q_norm_w": g_qn, "kv_norm_w": g_kvn, "conv_w": g_conv_w, "conv_b": g_conv_b, "dt_bias": g_dtb,
              "a_log": g_alog, "d_skip": g_dskip, "ssd_norm_w": g_ssd_n, "attn_out_norm_w": g_attn_n,
              "pre_mix_norm_w": g_pre_mix, "post_mix_norm_w": g_post_mix, "pre_ffn_norm_w": g_pre_ffn,
              "post_ffn_norm_w": g_post_ffn}
    small_sizes = [int(np.prod(gsmall[n].shape)) for n in SMALL]
    srows = _round_up(-(-sum(small_sizes) // LANES), 8)
    spart = _flat_pad([gsmall[n] for n in SMALL], srows * LANES).reshape(srows, LANES)
    sall = _gather_all(spart)

    def sum8_fn(a):
        tot = a[0]
        for k in range(1, 8):
            tot = tot + a[k]
        return tot

    ssum = _blocked(sum8_fn, (1,), [(sall, sall.shape, lambda i: (0, 0, 0))],
                    [((srows, LANES), f32, (srows, LANES), lambda i: (0, 0))], [], "small_grad_sum")[0].reshape(-1)
    gred = {}
    off = 0
    for n, sz in zip(SMALL, small_sizes):
        gred[n] = ssum[off:off + sz].reshape(gsmall[n].shape)
        off += sz
    gshard["conv_w"] = lax.dynamic_slice_in_dim(gred["conv_w"], my_chip * ccs, ccs, axis=1)
    for n in SMALL:
        if n != "conv_w":
            gshard[n] = gred[n].reshape(wts[n].shape)

    delta, new_m, new_v = {}, {}, {}
    big_out = {}
    for n, mine_h, sib_h in zip(BIG, halves, sib_halves):
        view = flat_t if n == "w_in" else (lambda a3: a3)
        res = _adamw_halves(view(local[n]), view(local["m_" + n]), view(local["v_" + n]), mine_h, sib_h, sp,
                            "adamw_" + n)
        big_out[n] = [jnp.swapaxes(a.reshape(1, cs_in, d), 1, 2) for a in res] if n == "w_in" else res
    pack = lambda src: _flat_pad([src[n] for n in SMALL], srows * LANES).reshape(srows, LANES)
    sd, sm, sv = _adamw_call(pack(wts), pack(gshard), pack(mom_m), pack(mom_v), "adamw_small")
    off = 0
    for n in SMALL:
        sz = int(np.prod(wts[n].shape))
        for dst, src in ((delta, sd), (new_m, sm), (new_v, sv)):
            dst[n] = src.reshape(-1)[off:off + sz].reshape(wts[n].shape)
        off += sz

    small_out = (gshard, delta, new_m, new_v)
    pick = lambda k: [big_out[n][k] if n in big_out else small_out[k][n][None] for n in ORDER]
    return (loss, grad_x[None], *pick(0), *pick(1), *pick(2), *pick(3))
```

```python
import functools

import numpy as np
import jax
import jax.numpy as jnp
from jax import lax
from jax.experimental import pallas as pl
from jax.experimental.pallas import tpu as pltpu

f32, bf16 = jnp.float32, jnp.bfloat16

EPS = 1e-6
V_HEAD = 128
NOPE = 128
ROPE = 64
QK_PAD = 256
Q_RANK = 512
KV_RANK = 512
ROPE_THETA = 10000.0
SSD_P = 64
SSD_G = 2
SSD_N = 128
SSD_K = 4
CHUNK = 128
ADAM_LR, ADAM_B1, ADAM_B2, ADAM_EPS, ADAM_WD, ADAM_STEP = 0.001, 0.9, 0.999, 1e-08, 0.01, 10

VMEM_LIMIT_BYTES = 48 * 1024 * 1024
LANES = 128
ATT_TILE = 1024
ATT_TILE_FWD = 1024
MM_TM, MM_TN, MM_TK = 1408, 1024, 1408
CHUNK_WHOLE_MAX = 1536
ROW_TILE = 256
ROW_BLOCK_BYTES = 2 * 1024 * 1024

NN = (((1,), (0,)), ((), ()))
NT = (((1,), (1,)), ((), ()))
TN = (((0,), (0,)), ((), ()))
MESH = pl.DeviceIdType.MESH
ANY = pl.BlockSpec(memory_space=pl.ANY)


def _tile(dim, cap, align=LANES):
    if dim <= cap:
        return dim
    t = (cap // align) * align
    while t >= align:
        if dim % t == 0:
            return t
        t -= align
    raise ValueError(f"no tile for {dim} under {cap}")


def _row_tile(rows, cols, align):
    best = None
    for tr in range(align, rows + 1, align):
        if rows % tr == 0 and tr * cols * 4 <= ROW_BLOCK_BYTES:
            best = tr
    return best or rows


def _round_up(n, m):
    return -(-n // m) * m


def _params(sem):
    return pltpu.CompilerParams(dimension_semantics=sem, vmem_limit_bytes=VMEM_LIMIT_BYTES)


def _dot(a, b, dims):
    return lax.dot_general(a.astype(bf16), b.astype(bf16), dims, preferred_element_type=f32)


def _call(body, name, out_shape, grid, in_specs, out_specs, scratch_shapes, sem, args, comm=None, at=None,
          prefetch=()):
    npf = len(prefetch)
    if comm is None:
        gs = pltpu.PrefetchScalarGridSpec(num_scalar_prefetch=npf, grid=grid, in_specs=list(in_specs),
                                          out_specs=list(out_specs), scratch_shapes=list(scratch_shapes))
        res = pl.pallas_call(body, name=name, out_shape=list(out_shape), grid_spec=gs,
                             compiler_params=_params(sem))(*prefetch, *args)
        return list(res), []
    n_in, n_out, n_sc = len(args), len(out_shape), len(scratch_shapes)
    na, no = len(comm.args), len(comm.outs)
    steps = int(np.prod(grid))

    def full(*allrefs):
        pf, refs = allrefs[:npf], allrefs[npf:]
        cin = refs[n_in:n_in + na]
        o0 = n_in + na
        cout = refs[o0 + n_out:o0 + n_out + no]
        s0 = o0 + n_out + no
        send_sems, recv_sems = refs[s0 + n_sc], refs[s0 + n_sc + 1]
        lin = pl.program_id(0)
        for dim in range(1, len(grid)):
            lin = lin * grid[dim] + pl.program_id(dim)
        for p in range(comm.nphase - 1):
            @pl.when(lin == int(round(at[p] * (steps - 1))))
            def _(p=p):
                comm.run(p, cin, cout, send_sems, recv_sems)
        body(*pf, *refs[:n_in], *refs[o0:o0 + n_out], *refs[s0:s0 + n_sc])

        @pl.when(lin == steps - 1)
        def _():
            comm.run(comm.nphase - 1, cin, cout, send_sems, recv_sems)

    gs = pltpu.PrefetchScalarGridSpec(
        num_scalar_prefetch=npf, grid=grid, in_specs=list(in_specs) + [ANY] * na,
        out_specs=list(out_specs) + [ANY] * no, scratch_shapes=list(scratch_shapes) + comm.sems())
    res = pl.pallas_call(full, name=name, out_shape=list(out_shape) + comm.outs, grid_spec=gs,
                         compiler_params=_params(("arbitrary",) * len(grid)))(*prefetch, *args, *comm.args)
    return list(res[:n_out]), list(res[n_out:])


def _chunk_tile(cs, cap):
    return cs if cs <= CHUNK_WHOLE_MAX else _tile(cs, cap)


def _matmul(pairs, mode, out_dtype, name, out_chunks=None, comm=None, at=None):
    a0, b0 = pairs[0]
    chunked = b0.ndim == 3
    cs = b0.shape[2] if chunked else None
    bcols = b0.shape[0] * b0.shape[2] if chunked else b0.shape[1]
    brows = b0.shape[1] if chunked else b0.shape[0]
    if mode == "nn":
        (m, k), n = a0.shape, bcols
    elif mode == "nt":
        (m, k), n = a0.shape, brows
    else:
        (k, m), n = a0.shape, bcols
    tm = _tile(m, MM_TM)
    if mode == "nt":
        tn = _tile(n, MM_TN)
        tk = _chunk_tile(cs, MM_TK) if chunked else _tile(k, MM_TK)
    else:
        tk = _tile(k, MM_TK)
        if chunked:
            tn = _chunk_tile(cs, MM_TN)
        elif out_chunks:
            tn = _chunk_tile(n // out_chunks, MM_TN)
        else:
            tn = _tile(n, MM_TN)
    nk = k // tk
    if mode == "nn":
        a_spec = pl.BlockSpec((tm, tk), lambda i, j, kk: (i, kk))
        if chunked:
            q = cs // tn
            b_spec = pl.BlockSpec((None, tk, tn), lambda i, j, kk: (j // q, kk, j % q))
        else:
            b_spec = pl.BlockSpec((tk, tn), lambda i, j, kk: (kk, j))
        dims = NN
    elif mode == "nt":
        a_spec = pl.BlockSpec((tm, tk), lambda i, j, kk: (i, kk))
        if chunked:
            q = cs // tk
            b_spec = pl.BlockSpec((None, tn, tk), lambda i, j, kk: (kk // q, j, kk % q))
        else:
            b_spec = pl.BlockSpec((tn, tk), lambda i, j, kk: (j, kk))
        dims = NT
    else:
        a_spec = pl.BlockSpec((tk, tm), lambda i, j, kk: (kk, i))
        b_spec = pl.BlockSpec((tk, tn), lambda i, j, kk: (kk, j))
        dims = TN
    if out_chunks:
        qo = (n // out_chunks) // tn
        out_shape = jax.ShapeDtypeStruct((out_chunks, m, n // out_chunks), out_dtype)
        o_spec = pl.BlockSpec((None, tm, tn), lambda i, j, kk: (j // qo, i, j % qo))
    else:
        out_shape = jax.ShapeDtypeStruct((m, n), out_dtype)
        o_spec = pl.BlockSpec((tm, tn), lambda i, j, kk: (i, j))
    npair = len(pairs)

    def body(*refs):
        o_ref, acc = refs[2 * npair], refs[2 * npair + 1]
        kk = pl.program_id(2)

        @pl.when(kk == 0)
        def _():
            acc[...] = jnp.zeros_like(acc)

        part = _dot(refs[0][...], refs[1][...], dims)
        for p in range(1, npair):
            part = part + _dot(refs[2 * p][...], refs[2 * p + 1][...], dims)
        acc[...] += part

        @pl.when(kk == nk - 1)
        def _():
            o_ref[...] = acc[...].astype(out_dtype)

    args = [t for pr in pairs for t in pr]
    res, cres = _call(body, name, [out_shape], (m // tm, n // tn, nk), [a_spec, b_spec] * npair, [o_spec],
                      [pltpu.VMEM((tm, tn), f32)], ("parallel", "parallel", "arbitrary"), args, comm, at)
    return res[0] if comm is None else (res[0], cres)


def _sigmoid(x):
    return 1.0 / (1.0 + jnp.exp(-x))


def _ffn_up(v, wg, wu):
    m, k = v.shape
    nchunk, _, cs = wg.shape
    n = nchunk * cs
    tm, tn = _tile(m, 512), _chunk_tile(cs, 512)
    q = cs // tn
    w_spec = pl.BlockSpec((None, k, tn), lambda j, i: (j // q, 0, j % q))

    def body(v_ref, wg_ref, wu_ref, g_ref, u_ref, act_ref):
        vb = v_ref[...]
        g = _dot(vb, wg_ref[...], NN)
        u = _dot(vb, wu_ref[...], NN)
        sg = _sigmoid(g)
        silu = g * sg
        g_ref[...] = silu.astype(bf16)
        u_ref[...] = (u * (sg * (1.0 + g * (1.0 - sg)))).astype(bf16)
        act_ref[...] = (silu * u).astype(bf16)

    out = jax.ShapeDtypeStruct((m, n), bf16)
    o_spec = pl.BlockSpec((tm, tn), lambda j, i: (i, j))
    return pl.pallas_call(
        body, name="ffn_up", out_shape=(out, out, out), grid=(n // tn, m // tm),
        in_specs=[pl.BlockSpec((tm, k), lambda j, i: (i, 0)), w_spec, w_spec],
        out_specs=(o_spec, o_spec, o_spec),
        compiler_params=_params(("parallel", "parallel")),
    )(v, wg, wu)


def _ffn_down_bwd(dffn, wd, dact_du, dact_dg):
    m, k = dffn.shape
    n = wd.shape[0]
    tm, tn = _tile(m, 1024), _tile(n, 512)

    def body(d_ref, w_ref, fu_ref, fg_ref, dg_ref, du_ref):
        dact = _dot(d_ref[...], w_ref[...], NT)
        du_ref[...] = (dact * fu_ref[...].astype(f32)).astype(bf16)
        dg_ref[...] = (dact * fg_ref[...].astype(f32)).astype(bf16)

    out = jax.ShapeDtypeStruct((m, n), bf16)
    o_spec = pl.BlockSpec((tm, tn), lambda i, j: (i, j))
    return pl.pallas_call(
        body, name="ffn_down_bwd", out_shape=(out, out), grid=(m // tm, n // tn),
        in_specs=[pl.BlockSpec((tm, k), lambda i, j: (i, 0)), pl.BlockSpec((tn, k), lambda i, j: (j, 0)),
                  o_spec, o_spec],
        out_specs=(o_spec, o_spec),
        compiler_params=_params(("parallel", "parallel")),
    )(dffn, wd, dact_du, dact_dg)


def _blocked(fn, grid, ins, outs, accs, name, sp=None):
    n_in, n_out, n_acc = len(ins), len(outs), len(accs)
    nsp = 0 if sp is None else 1

    def body(*refs):
        refs = refs[nsp:]
        tiles = [r[...] for r in refs[:n_in]]
        res = fn(*tiles)
        if not isinstance(res, (tuple, list)):
            res = (res,)
        for r, val in zip(refs[n_in:n_in + n_out], res[:n_out]):
            r[...] = val.astype(r.dtype)
        if n_acc:
            first = pl.program_id(0) == 0
            for d in range(1, len(grid)):
                first = jnp.logical_and(first, pl.program_id(d) == 0)

            @pl.when(first)
            def _():
                for r in refs[n_in + n_out:]:
                    r[...] = jnp.zeros_like(r)

            for r, val in zip(refs[n_in + n_out:], res[n_out:]):
                r[...] += val

    def acc_map(shape):
        zeros = (0,) * len(shape)
        return lambda *idx: zeros

    in_specs = [pl.BlockSpec(bs, im) for _, bs, im in ins]
    out_specs = [pl.BlockSpec(bs, im) for _, _, bs, im in outs] + [pl.BlockSpec(s, acc_map(s)) for s in accs]
    out_shape = [jax.ShapeDtypeStruct(s, d) for s, d, _, _ in outs] + [jax.ShapeDtypeStruct(s, f32) for s in accs]
    sem = ("arbitrary",) * len(grid) if n_acc else ("parallel",) * len(grid)
    args = [a for a, _, _ in ins]
    if sp is None:
        res = pl.pallas_call(body, name=name, out_shape=out_shape, grid=grid, in_specs=in_specs,
                             out_specs=out_specs, compiler_params=_params(sem))(*args)
    else:
        gs = pltpu.PrefetchScalarGridSpec(num_scalar_prefetch=1, grid=grid, in_specs=in_specs, out_specs=out_specs)
        res = pl.pallas_call(body, name=name, out_shape=out_shape, grid_spec=gs,
                             compiler_params=_params(sem))(sp, *args)
    return res


def _rows(a, tm, cols=None, cb=0):
    w = a.shape[1] if cols is None else cols
    return (a, (tm, w), lambda i: (i, cb))


def _par(a):
    zeros = (0,) * a.ndim
    return (a, a.shape, lambda i: zeros)


def _rout(t, w, dtype, tm):
    return ((t, w), dtype, (tm, w), lambda i: (i, 0))


def _rowwise(fn, t, tm, ins, outs, accs, name):
    return _blocked(fn, (t // tm,), ins, outs, accs, name)


def _rms(x, w):
    r = lax.rsqrt(jnp.mean(x * x, axis=-1, keepdims=True) + EPS)
    return x * r * w


def _rms_bwd(x, w, dy):
    r = lax.rsqrt(jnp.mean(x * x, axis=-1, keepdims=True) + EPS)
    xh = x * r
    dyw = dy * w
    dx = r * (dyw - xh * jnp.mean(dyw * xh, axis=-1, keepdims=True))
    return dx, jnp.sum(dy * xh, axis=0, keepdims=True)


def _silu_grad(x):
    s = _sigmoid(x)
    return s * (1.0 + x * (1.0 - s))


def _rope(blk, cosp, sina, sinb):
    return blk * cosp + pltpu.roll(blk, 96, 1) * sina + pltpu.roll(blk, 32, 1) * sinb


def _rope_bwd(dy, cosp, sina, sinb):
    return dy * cosp + pltpu.roll(dy * sina, 32, 1) + pltpu.roll(dy * sinb, 96, 1)


HALO = 8


def _conv_taps(buf, w, tm, base):
    acc = buf[base:base + tm, :] * w[0:1]
    for k in range(1, SSD_K):
        acc = acc + buf[base + k:base + k + tm, :] * w[k:k + 1]
    return acc


def _conv_specs(t, tm, cdim):
    cur = pl.BlockSpec((tm, cdim), lambda i: (i, 0))
    prev = pl.BlockSpec((HALO, cdim), lambda i: (jnp.maximum(i * (tm // HALO) - 1, 0), 0))
    nxt = pl.BlockSpec((HALO, cdim), lambda i: (jnp.minimum((i + 1) * (tm // HALO), t // HALO - 1), 0))
    return cur, prev, nxt


def _conv_fwd(src, w, b, cdim, tm):
    t = src.shape[0]
    cur, prev, _ = _conv_specs(t, tm, cdim)

    def body(x_ref, p_ref, w_ref, b_ref, o_ref, buf):
        buf[0:HALO, :] = jnp.where(pl.program_id(0) > 0, p_ref[...], 0.0)
        buf[HALO:HALO + tm, :] = x_ref[...]
        pre = _conv_taps(buf, w_ref[...], tm, HALO - (SSD_K - 1)) + b_ref[...]
        o_ref[...] = pre * _sigmoid(pre)

    par = lambda a: pl.BlockSpec(a.shape, lambda i: (0, 0))
    return pl.pallas_call(
        body, name="conv_silu", out_shape=jax.ShapeDtypeStruct((t, cdim), f32), grid=(t // tm,),
        in_specs=[cur, prev, par(w), par(b)], out_specs=cur,
        scratch_shapes=[pltpu.VMEM((tm + HALO, cdim), f32)], compiler_params=_params(("parallel",)),
    )(src, src, w, b)


def _conv_bwd_pre(src, w, b, dact, cdim, tm):
    t = src.shape[0]
    cur, prev, _ = _conv_specs(t, tm, cdim)

    def body(x_ref, p_ref, w_ref, b_ref, d_ref, dpre_ref, dw0, dw1, dw2, dw3, db, buf):
        @pl.when(pl.program_id(0) == 0)
        def _():
            for r in (dw0, dw1, dw2, dw3, db):
                r[...] = jnp.zeros_like(r)

        buf[0:HALO, :] = jnp.where(pl.program_id(0) > 0, p_ref[...], 0.0)
        buf[HALO:HALO + tm, :] = x_ref[...]
        base = HALO - (SSD_K - 1)
        pre = _conv_taps(buf, w_ref[...], tm, base) + b_ref[...]
        dpre = d_ref[...] * _silu_grad(pre)
        dpre_ref[...] = dpre
        for k, r in enumerate((dw0, dw1, dw2, dw3)):
            r[...] += jnp.sum(dpre * buf[base + k:base + k + tm, :], axis=0, keepdims=True)
        db[...] += jnp.sum(dpre, axis=0, keepdims=True)

    par = lambda a: pl.BlockSpec(a.shape, lambda i: (0, 0))
    acc = pl.BlockSpec((1, cdim), lambda i: (0, 0))
    acc_shape = jax.ShapeDtypeStruct((1, cdim), f32)
    return pl.pallas_call(
        body, name="conv_silu_bwd", out_shape=[jax.ShapeDtypeStruct((t, cdim), f32)] + [acc_shape] * 5,
        grid=(t // tm,), in_specs=[cur, prev, par(w), par(b), cur], out_specs=[cur] + [acc] * 5,
        scratch_shapes=[pltpu.VMEM((tm + HALO, cdim), f32)], compiler_params=_params(("arbitrary",)),
    )(src, src, w, b, dact)


def _conv_bwd_dx(dpre, w, tm):
    t, cdim = dpre.shape
    cur, _, nxt = _conv_specs(t, tm, cdim)
    last = t // tm - 1

    def body(d_ref, n_ref, w_ref, o_ref, buf):
        buf[0:tm, :] = d_ref[...]
        buf[tm:tm + HALO, :] = jnp.where(pl.program_id(0) < last, n_ref[...], 0.0)
        wv = w_ref[...]
        acc = buf[0:tm, :] * wv[SSD_K - 1:SSD_K]
        for k in range(SSD_K - 1):
            s = SSD_K - 1 - k
            acc = acc + buf[s:s + tm, :] * wv[k:k + 1]
        o_ref[...] = acc.astype(bf16)

    return pl.pallas_call(
        body, name="conv_bwd_dx", out_shape=jax.ShapeDtypeStruct((t, cdim), bf16), grid=(t // tm,),
        in_specs=[cur, nxt, pl.BlockSpec(w.shape, lambda i: (0, 0))], out_specs=cur,
        scratch_shapes=[pltpu.VMEM((tm + HALO, cdim), f32)], compiler_params=_params(("parallel",)),
    )(dpre, dpre, w)


def _causal_mask(s, row0=0):
    row = lax.broadcasted_iota(jnp.int32, s.shape, 0) + row0
    col = lax.broadcasted_iota(jnp.int32, s.shape, 1)
    return jnp.where(row >= col, s, -jnp.inf)


def _causal_pairs(nq, q_major):
    pairs = [(qi, ki) for qi in range(nq) for ki in range(qi + 1)]
    if not q_major:
        pairs.sort(key=lambda p: (p[1], p[0]))
    return (jnp.asarray([p[0] for p in pairs], jnp.int32), jnp.asarray([p[1] for p in pairs], jnp.int32))


def _flash_fwd(q, kv, kr, nheads, scale, comm=None, at=None):
    t = q.shape[0]
    tq = _tile(t, ATT_TILE_FWD)
    nq = t // tq

    qtab, ktab = _causal_pairs(nq, q_major=True)
    hp = 2 if nheads % 2 == 0 else 1

    def body(qt, kt, q_ref, kv_ref, kr_ref, o_ref, lse_ref, m_sc, l_sc, acc_sc):
        qi, ki = qt[pl.program_id(1)], kt[pl.program_id(1)]

        @pl.when(ki == 0)
        def _():
            m_sc[...] = jnp.full_like(m_sc, -jnp.inf)
            l_sc[...] = jnp.zeros_like(l_sc)
            acc_sc[...] = jnp.zeros_like(acc_sc)

        def step(masked, last):
            krb = kr_ref[...]
            ss = []
            for j in range(hp):
                k = jnp.concatenate([kv_ref[:, j * QK_PAD: j * QK_PAD + NOPE], krb], axis=1)
                ss.append(lax.dot_general(q_ref[:, j * QK_PAD:(j + 1) * QK_PAD], k, NT, preferred_element_type=f32))
            soft = []
            for j in range(hp):
                s = ss[j] * scale
                if masked:
                    s = _causal_mask(s)
                m_old = m_sc[j]
                m_new = jnp.maximum(m_old, jnp.max(s, axis=1, keepdims=True))
                alpha = jnp.exp(m_old - m_new)
                p = jnp.exp(s - m_new)
                soft.append((m_new, alpha, alpha * l_sc[j] + jnp.sum(p, axis=1, keepdims=True), p.astype(bf16)))
            for j in range(hp):
                m_new, alpha, l, p = soft[j]
                v = kv_ref[:, j * QK_PAD + NOPE:(j + 1) * QK_PAD]
                acc = alpha * acc_sc[j] + lax.dot_general(p, v, NN, preferred_element_type=f32)
                if last:
                    o_ref[:, j * V_HEAD:(j + 1) * V_HEAD] = acc / l
                    lse_ref[:, j * V_HEAD:(j + 1) * V_HEAD] = jnp.broadcast_to(m_new + jnp.log(l), (tq, V_HEAD))
                else:
                    l_sc[j] = l
                    acc_sc[j] = acc
                    m_sc[j] = m_new

        @pl.when(ki < qi)
        def _():
            step(False, False)

        @pl.when(ki == qi)
        def _():
            step(True, True)

    o_spec = pl.BlockSpec((tq, hp * V_HEAD), lambda h, s, qt, kt: (qt[s], h))
    out = jax.ShapeDtypeStruct((t, nheads * V_HEAD), f32)
    (o, lse), cres = _call(
        body, "flash_fwd", [out, out], (nheads // hp, len(qtab)),
        [pl.BlockSpec((tq, hp * QK_PAD), lambda h, s, qt, kt: (qt[s], h)),
         pl.BlockSpec((tq, hp * QK_PAD), lambda h, s, qt, kt: (kt[s], h)),
         pl.BlockSpec((tq, LANES), lambda h, s, qt, kt: (kt[s], 0))],
        [o_spec, o_spec],
        [pltpu.VMEM((hp, tq, 1), f32), pltpu.VMEM((hp, tq, 1), f32), pltpu.VMEM((hp, tq, V_HEAD), f32)],
        ("parallel", "arbitrary"), [q, kv, kr], comm, at, prefetch=(qtab, ktab))
    return o, lse, cres


def _flash_bwd(q, kv, kr, do, lse, delta, nheads, scale, comm=None, at=None):
    t = q.shape[0]
    tq = _tile(t, ATT_TILE)
    nq = t // tq
    qtab, ktab = _causal_pairs(nq, q_major=False)
    nsub = 2 if tq % 32 == 0 else 1

    def body(qt, kt, q_ref, kn_ref, kr_ref, v_ref, do_ref, lse_ref, dl_ref, dkv_ref, dkr_ref, dq_ref, dk_sc, dv_sc):
        qi, ki = qt[pl.program_id(1)], kt[pl.program_id(1)]

        @pl.when(pl.program_id(1) == 0)
        def _():
            dq_ref[...] = jnp.zeros_like(dq_ref)

        @pl.when(qi == ki)
        def _():
            dk_sc[...] = jnp.zeros_like(dk_sc)
            dv_sc[...] = jnp.zeros_like(dv_sc)

        def step(masked):
            k = jnp.concatenate([kn_ref[...], kr_ref[...]], axis=1)
            vb = v_ref[...]
            parts = [slice(i * (tq // nsub), (i + 1) * (tq // nsub)) for i in range(nsub)]
            qs = [q_ref[r, :] for r in parts]
            dos = [do_ref[r, :] for r in parts]
            ss = [lax.dot_general(qb, k, NT, preferred_element_type=f32) for qb in qs]
            dps = [lax.dot_general(dob, vb, NT, preferred_element_type=f32) for dob in dos]
            ps, dss = [], []
            for r, s, dp in zip(parts, ss, dps):
                s = s * scale
                if masked:
                    s = _causal_mask(s, r.start)
                p = jnp.exp(s - lse_ref[r, 0:1])
                ps.append(p.astype(bf16))
                dss.append((p * (dp - dl_ref[r, 0:1]) * scale).astype(bf16))
            for r, qb, dob, p, ds in zip(parts, qs, dos, ps, dss):
                dv_sc[...] += lax.dot_general(p, dob, TN, preferred_element_type=f32)
                dk_sc[...] += lax.dot_general(ds, qb, TN, preferred_element_type=f32)
                rows = pl.ds(pl.multiple_of(qi * tq + r.start, tq // nsub), tq // nsub)
                dq_ref[rows, :] += lax.dot_general(ds, k, NN, preferred_element_type=f32)

        @pl.when(qi > ki)
        def _():
            step(False)

        @pl.when(qi == ki)
        def _():
            step(True)

        @pl.when(qi == nq - 1)
        def _():
            dk = dk_sc[...]
            dkv_ref[...] = jnp.concatenate([dk[:, :NOPE], dv_sc[...]], axis=1).astype(bf16)
            dkr_ref[...] = dk[:, NOPE:]

    hspec = pl.BlockSpec((tq, V_HEAD), lambda h, s, qt, kt: (qt[s], h))
    (dkv, dkr, dq), cres = _call(
        body, "flash_bwd",
        [jax.ShapeDtypeStruct((t, nheads * QK_PAD), bf16), jax.ShapeDtypeStruct((t, nheads * LANES), f32),
         jax.ShapeDtypeStruct((t, nheads * QK_PAD), f32)],
        (nheads, len(qtab)),
        [pl.BlockSpec((tq, QK_PAD), lambda h, s, qt, kt: (qt[s], h)),
         pl.BlockSpec((tq, NOPE), lambda h, s, qt, kt: (kt[s], 2 * h)),
         pl.BlockSpec((tq, LANES), lambda h, s, qt, kt: (kt[s], 0)),
         pl.BlockSpec((tq, V_HEAD), lambda h, s, qt, kt: (kt[s], 2 * h + 1)),
         hspec, hspec, hspec],
        [pl.BlockSpec((tq, QK_PAD), lambda h, s, qt, kt: (kt[s], h)),
         pl.BlockSpec((tq, LANES), lambda h, s, qt, kt: (kt[s], h)),
         pl.BlockSpec((t, QK_PAD), lambda h, s, qt, kt: (0, h))],
        [pltpu.VMEM((tq, QK_PAD), f32), pltpu.VMEM((tq, V_HEAD), f32)],
        ("parallel", "arbitrary"), [q, kv, kr, kv, do, lse, delta], comm, at, prefetch=(qtab, ktab))
    return dkv, dkr, dq, cres


def _split3(a):
    hi = a.astype(bf16)
    r1 = a - hi.astype(f32)
    mid = r1.astype(bf16)
    lo = (r1 - mid.astype(f32)).astype(bf16)
    return hi, mid, lo


def _split2(a):
    hi = a.astype(bf16)
    return hi, (a - hi.astype(f32)).astype(bf16)


def _ones_dot_left(tri, a):
    hi, mid, lo = _split3(a)
    d = lambda v: lax.dot_general(tri, v, NN, preferred_element_type=f32)
    return d(hi) + d(mid) + d(lo)


def _ones_dot_right(a, tri):
    hi, mid, lo = _split3(a)
    d = lambda v: lax.dot_general(v, tri, NN, preferred_element_type=f32)
    return d(hi) + d(mid) + d(lo)


def _softplus(x):
    return jnp.maximum(x, 0.0) + jnp.log(1.0 + jnp.exp(-jnp.abs(x)))


def _head_spread(hs, width):
    shift = SSD_P.bit_length() - 1
    return (lax.broadcasted_iota(jnp.int32, (hs, width), 0)
            == lax.shift_right_logical(lax.broadcasted_iota(jnp.int32, (hs, width), 1), shift)).astype(bf16)


def _ssd_common(dt_ref, dtT_ref, dtb_ref, dtbT_ref, alog_ref, alogT_ref):
    ii = lax.broadcasted_iota(jnp.int32, (CHUNK, CHUNK), 0)
    jj = lax.broadcasted_iota(jnp.int32, (CHUNK, CHUNK), 1)
    tri = ii >= jj
    raw = dt_ref[...] + dtb_ref[...]
    dt = _softplus(raw)
    a_neg = -jnp.exp(alog_ref[...])
    cum = _ones_dot_left(tri.astype(bf16), dt * a_neg)
    dt_t = _softplus(dtT_ref[...] + dtbT_ref[...])
    cum_t = _ones_dot_right(dt_t * (-jnp.exp(alogT_ref[...])), (ii <= jj).astype(bf16))
    return tri, raw, dt, a_neg, cum, cum_t


def _ssd_fwd(xbc, dt, dt_t, dtb, dtb_t, alog, alog_t, dskip, width):
    t, cdim = xbc.shape
    hs = dt.shape[1]
    nc = t // CHUNK
    epg = hs // SSD_G
    gn = SSD_G * SSD_N

    def body(x_ref, dt_ref, dtT_ref, dtb_ref, dtbT_ref, alog_ref, alogT_ref, d_ref, y_ref, hp_ref, h_sc, yd_sc):
        @pl.when(pl.program_id(0) == 0)
        def _():
            h_sc[...] = jnp.zeros_like(h_sc)

        tri, _, dtv, _, cum, cum_t = _ssd_common(dt_ref, dtT_ref, dtb_ref, dtbT_ref, alog_ref, alogT_ref)
        spread = _head_spread(hs, width)
        clast = cum[CHUNK - 1:CHUNK, :]
        dec = jnp.exp(clast)
        wide = _ones_dot_right(jnp.concatenate([dtv, jnp.exp(cum), jnp.exp(clast - cum),
                                                jnp.broadcast_to(d_ref[...], (CHUNK, hs))], axis=0), spread)
        dt_x, ee_x, ff_x, dsk_x = (wide[i * CHUNK:(i + 1) * CHUNK] for i in range(4))
        xs = x_ref[:, :width]
        xdt = xs * dt_x
        xf = xdt * ff_x
        h_all = h_sc[...]
        hp_ref[0] = h_all
        h_all = h_all.reshape(hs * SSD_P, SSD_N)
        ch_parts = []
        for g in range(SSD_G):
            gsl = slice(g * epg * SSD_P, (g + 1) * epg * SSD_P)
            bb = x_ref[:, width + g * SSD_N: width + (g + 1) * SSD_N].astype(bf16)
            cb_ = x_ref[:, width + gn + g * SSD_N: width + gn + (g + 1) * SSD_N].astype(bf16)
            cbm = lax.dot_general(cb_, bb, NT, preferred_element_type=f32)
            ch_parts.append(_dot(cb_, h_all[gsl], NT))
            st = _dot(xf[:, gsl], bb, TN)
            for e in range(g * epg, (g + 1) * epg):
                esl = slice(e * SSD_P, (e + 1) * SSD_P)
                lmat = jnp.exp(jnp.where(tri, cum[:, e:e + 1] - cum_t[e:e + 1, :], -jnp.inf))
                yd_sc[:, esl] = _dot(cbm * lmat, xdt[:, esl], NN)
                j = e - g * epg
                h_sc[e] = h_sc[e] * dec[:, e:e + 1] + st[j * SSD_P:(j + 1) * SSD_P, :]
        y_ref[...] = yd_sc[...] + jnp.concatenate(ch_parts, axis=1) * ee_x + xs * dsk_x

    par = lambda a: pl.BlockSpec(a.shape, lambda i: (0,) * a.ndim)
    return pl.pallas_call(
        body, name="ssd_fwd",
        out_shape=(jax.ShapeDtypeStruct((t, width), f32), jax.ShapeDtypeStruct((nc, hs, SSD_P, SSD_N), f32)),
        grid=(nc,),
        in_specs=[pl.BlockSpec((CHUNK, cdim), lambda i: (i, 0)), pl.BlockSpec((CHUNK, hs), lambda i: (i, 0)),
                  pl.BlockSpec((hs, CHUNK), lambda i: (0, i)), par(dtb), par(dtb_t), par(alog), par(alog_t), par(dskip)],
        out_specs=(pl.BlockSpec((CHUNK, width), lambda i: (i, 0)),
                   pl.BlockSpec((1, hs, SSD_P, SSD_N), lambda i: (i, 0, 0, 0))),
        scratch_shapes=[pltpu.VMEM((hs, SSD_P, SSD_N), f32), pltpu.VMEM((CHUNK, width), f32)],
        compiler_params=_params(("arbitrary",)),
    )(xbc, dt, dt_t, dtb, dtb_t, alog, alog_t, dskip)


def _ssd_bwd(xbc, dt, dt_t, dtb, dtb_t, alog, alog_t, dskip, hprev, dy, width, comm=None, at=None):
    t, cdim = xbc.shape
    hs = dt.shape[1]
    nc = t // CHUNK
    epg = hs // SSD_G
    gn = SSD_G * SSD_N

    def body(x_ref, dt_ref, dtT_ref, dtb_ref, dtbT_ref, alog_ref, alogT_ref, d_ref, hp_ref, dy_ref,
             dx_ref, ddt_ref, dalog_ref, ddsk_ref, ddtb_ref, dh_sc, dxd_sc):
        @pl.when(pl.program_id(0) == 0)
        def _():
            dh_sc[...] = jnp.zeros_like(dh_sc)
            dalog_ref[...] = jnp.zeros_like(dalog_ref)
            ddsk_ref[...] = jnp.zeros_like(ddsk_ref)
            ddtb_ref[...] = jnp.zeros_like(ddtb_ref)

        tri, raw, dtv, a_neg, cum, cum_t = _ssd_common(dt_ref, dtT_ref, dtb_ref, dtbT_ref, alog_ref, alogT_ref)
        dsk = d_ref[...]
        head_row = lax.broadcasted_iota(jnp.int32, (1, hs), 1)
        head_col = lax.broadcasted_iota(jnp.int32, (hs, 1), 0)
        last_row = (lax.broadcasted_iota(jnp.int32, (CHUNK, 1), 0) == CHUNK - 1).astype(f32)
        shift = SSD_P.bit_length() - 1
        spread = _head_spread(hs, width)
        gather = (lax.shift_right_logical(lax.broadcasted_iota(jnp.int32, (width, hs), 0), shift)
                  == lax.broadcasted_iota(jnp.int32, (width, hs), 1)).astype(bf16)
        clast = cum[CHUNK - 1:CHUNK, :]
        ee = jnp.exp(cum)
        ff = jnp.exp(clast - cum)
        dec = jnp.exp(clast)
        wide = _ones_dot_right(jnp.concatenate([dtv, ee, ff, jnp.broadcast_to(dsk, (CHUNK, hs))], axis=0), spread)
        dt_x, ee_x, ff_x, dsk_x = (wide[i * CHUNK:(i + 1) * CHUNK] for i in range(4))
        xs = x_ref[:, :width]
        dyv = dy_ref[...]
        xdt = xs * dt_x
        dye = dyv * ee_x
        xf = xdt * ff_x
        h_all = hp_ref[0].reshape(hs * SSD_P, SSD_N)
        dh_all = dh_sc[...].reshape(hs * SSD_P, SSD_N)
        hi, lo = _split2(_ones_dot_left(spread, dh_all * h_all))
        ones8 = jnp.ones((8, SSD_N), bf16)
        hh = (lax.dot_general(ones8, hi, NT, preferred_element_type=f32)
              + lax.dot_general(ones8, lo, NT, preferred_element_type=f32))[0:1]
        rowsum_m = jnp.zeros((CHUNK, hs), f32)
        colsum_m = jnp.zeros((hs, CHUNK), f32)
        ch_parts, bds_parts = [], []
        for g in range(SSD_G):
            bsl = slice(width + g * SSD_N, width + (g + 1) * SSD_N)
            csl = slice(width + gn + g * SSD_N, width + gn + (g + 1) * SSD_N)
            gsl = slice(g * epg * SSD_P, (g + 1) * epg * SSD_P)
            bb = x_ref[:, bsl].astype(bf16)
            cb_ = x_ref[:, csl].astype(bf16)
            cbm = lax.dot_general(cb_, bb, NT, preferred_element_type=f32)
            hg = h_all[gsl].astype(bf16)
            dhg = dh_all[gsl].astype(bf16)
            ch_parts.append(_dot(cb_, hg, NT))
            bds_parts.append(_dot(bb, dhg, NT))
            dcg = _dot(dye[:, gsl], hg, NN)
            dbg = _dot(xf[:, gsl], dhg, NN)
            dh_new = _dot(dye[:, gsl], cb_, TN)
            dcb = jnp.zeros((CHUNK, CHUNK), f32)
            for e in range(g * epg, (g + 1) * epg):
                esl = slice(e * SSD_P, (e + 1) * SSD_P)
                lmat = jnp.exp(jnp.where(tri, cum[:, e:e + 1] - cum_t[e:e + 1, :], -jnp.inf))
                gmat = cbm * lmat
                dy_e = dyv[:, esl].astype(bf16)
                dgm = _dot(dy_e, xdt[:, esl], NT)
                dxd_sc[:, esl] = _dot(gmat, dy_e, TN)
                dcb = dcb + dgm * lmat
                mm = dgm * gmat
                rowsum_m = rowsum_m + jnp.sum(mm, axis=1, keepdims=True) * (head_row == e).astype(f32)
                colsum_m = colsum_m + (head_col == e).astype(f32) * jnp.sum(mm, axis=0, keepdims=True)
                j = e - g * epg
                dh_sc[e] = dh_new[j * SSD_P:(j + 1) * SSD_P, :] + dec[:, e:e + 1] * dh_sc[e]
            dx_ref[:, bsl] = dbg + _dot(dcb, cb_, TN)
            dx_ref[:, csl] = dcg + _dot(dcb, bb, NN)
        ch_all = jnp.concatenate(ch_parts, axis=1)
        bds_all = jnp.concatenate(bds_parts, axis=1)
        dxdt = bds_all * ff_x + dxd_sc[...]
        dx_ref[:, :width] = dxdt * dt_x + dyv * dsk_x
        sums = _ones_dot_right(jnp.concatenate([dxdt * xs, dyv * ch_all, bds_all * xdt, dyv * xs], axis=0), gather)
        ddtx_all = sums[0:CHUNK]
        dff = sums[2 * CHUNK:3 * CHUNK] * ff
        ddsk = jnp.sum(sums[3 * CHUNK:4 * CHUNK], axis=0, keepdims=True)
        dclast = jnp.sum(dff, axis=0, keepdims=True) + dec * hh
        eye = (lax.broadcasted_iota(jnp.int32, (hs, hs), 0) == lax.broadcasted_iota(jnp.int32, (hs, hs), 1)).astype(bf16)
        colsum_t = sum(lax.dot_general(v, eye, TN, preferred_element_type=f32) for v in _split3(colsum_m))
        dcum_all = sums[CHUNK:2 * CHUNK] * ee - dff + rowsum_m - colsum_t + dclast * last_row
        ii = lax.broadcasted_iota(jnp.int32, (CHUNK, CHUNK), 0)
        jj = lax.broadcasted_iota(jnp.int32, (CHUNK, CHUNK), 1)
        da = _ones_dot_left((jj >= ii).astype(bf16), dcum_all)
        ddt = da * a_neg + ddtx_all
        dalog_ref[...] += jnp.sum(da * dtv, axis=0, keepdims=True) * a_neg
        draw = ddt * _sigmoid(raw)
        place = (lax.broadcasted_iota(jnp.int32, (hs, LANES), 0) + ROPE
                 == lax.broadcasted_iota(jnp.int32, (hs, LANES), 1)).astype(bf16)
        ddt_ref[...] = _ones_dot_right(draw, place)
        ddtb_ref[...] += jnp.sum(draw, axis=0, keepdims=True)
        ddsk_ref[...] += ddsk

    rev = lambda i: nc - 1 - i
    par = lambda a: pl.BlockSpec(a.shape, lambda i: (0,) * a.ndim)
    acc = pl.BlockSpec((1, hs), lambda i: (0, 0))
    acc_shape = jax.ShapeDtypeStruct((1, hs), f32)
    res, cres = _call(
        body, "ssd_bwd",
        [jax.ShapeDtypeStruct((t, cdim), f32), jax.ShapeDtypeStruct((t, LANES), f32), acc_shape, acc_shape, acc_shape],
        (nc,),
        [pl.BlockSpec((CHUNK, cdim), lambda i: (rev(i), 0)), pl.BlockSpec((CHUNK, hs), lambda i: (rev(i), 0)),
         pl.BlockSpec((hs, CHUNK), lambda i: (0, rev(i))), par(dtb), par(dtb_t), par(alog), par(alog_t),
         par(dskip), pl.BlockSpec((1, hs, SSD_P, SSD_N), lambda i: (rev(i), 0, 0, 0)),
         pl.BlockSpec((CHUNK, width), lambda i: (rev(i), 0))],
        [pl.BlockSpec((CHUNK, cdim), lambda i: (rev(i), 0)), pl.BlockSpec((CHUNK, LANES), lambda i: (rev(i), 0)),
         acc, acc, acc],
        [pltpu.VMEM((hs, SSD_P, SSD_N), f32), pltpu.VMEM((CHUNK, width), f32)], ("arbitrary",),
        [xbc, dt, dt_t, dtb, dtb_t, alog, alog_t, dskip, hprev, dy], comm, at)
    return (*res, cres)


def _where_am_i():
    x, y, c = lax.axis_index("x"), lax.axis_index("y"), lax.axis_index("c")
    chips = [(1 - x, y), (x, 1 - y), (1 - x, 1 - y)]
    return x, y, c, chips


def _remote(src, dst, send_sems, recv_sems, k, to):
    return pltpu.make_async_remote_copy(src_ref=src, dst_ref=dst, send_sem=send_sems.at[k], recv_sem=recv_sems.at[k],
                                        device_id=to, device_id_type=MESH)


class _Comm:
    def __init__(self, args, outs, nsem, nphase, run):
        self.args, self.outs, self.nsem, self.nphase, self.run = list(args), list(outs), nsem, nphase, run

    def sems(self):
        return [pltpu.SemaphoreType.DMA((self.nsem,)), pltpu.SemaphoreType.DMA((self.nsem,))]


def _comm_call(comm, name):
    na, no = len(comm.args), len(comm.outs)

    def body(*refs):
        for phase in range(comm.nphase):
            comm.run(phase, refs[:na], refs[na:na + no], refs[na + no], refs[na + no + 1])

    return list(pl.pallas_call(body, name=name, out_shape=comm.outs, in_specs=[ANY] * na, out_specs=[ANY] * no,
                               scratch_shapes=comm.sems())(*comm.args))


def _half(ref, c, r2):
    return ref.at[pl.ds(c * r2, r2)]


def _gather_weights(shards, wholes):
    ns, nw = len(shards), len(wholes)
    per = 7

    def run(phase, srcs, outs, send_sems, recv_sems):
        x, y, c, chips = _where_am_i()
        me, sib = 2 * x + y, (x, y, 1 - c)
        def first():
            cps = []
            for w in range(ns + nw):
                src, out, base = srcs[w], outs[w], per * w
                halved = w < ns
                r2 = src.shape[0] // 2
                piece = _half(src, c, r2) if halved else src
                for j, (cx, cy) in enumerate(chips):
                    dst = _half(out.at[me], c, r2) if halved else out.at[me]
                    cps.append(_remote(piece, dst, send_sems, recv_sems, base + j, (cx, cy, c)))
                cps.append(_remote(src, out.at[me], send_sems, recv_sems, base + 6, sib))
            return cps

        def passed():
            cps = []
            for w in range(ns):
                r2 = srcs[w].shape[0] // 2
                for j, (cx, cy) in enumerate(chips):
                    got = _half(outs[w].at[2 * cx + cy], c, r2)
                    cps.append(_remote(got, got, send_sems, recv_sems, per * w + 3 + j, sib))
            return cps

        if phase == 0:
            for cp in first():
                cp.start()
        elif phase == 1:
            it = iter(passed())
            for w in range(ns + nw):
                src, out, base = srcs[w], outs[w], per * w
                r2 = src.shape[0] // 2
                for j, (cx, cy) in enumerate(chips):
                    got = _half(out.at[2 * cx + cy], c, r2) if w < ns else out.at[2 * cx + cy]
                    _remote(got, got, send_sems, recv_sems, base + j, sib).wait_recv()
                    if w < ns:
                        next(it).start()
        else:
            for w in range(ns + nw):
                src, out, base = srcs[w], outs[w], per * w
                r2 = src.shape[0] // 2
                if w < ns:
                    for j, (cx, cy) in enumerate(chips):
                        got = _half(out.at[2 * cx + cy], 1 - c, r2)
                        _remote(got, got, send_sems, recv_sems, base + 3 + j, sib).wait_recv()
                _remote(src, out.at[me], send_sems, recv_sems, base + 6, sib).wait_recv()
            for cp in first() + passed():
                cp.wait_send()

    args = list(shards) + list(wholes)
    outs = [jax.ShapeDtypeStruct((4,) + a.shape, a.dtype) for a in args]
    return _Comm(args, outs, per * len(args), 3, run)


def _sibling_send_halves(gs):
    n = len(gs)

    def run(phase, srcs, outs, send_sems, recv_sems):
        x, y, c, _ = _where_am_i()
        sib = (x, y, 1 - c)
        cps = []
        for w in range(n):
            r2 = srcs[w].shape[1] // 2
            for k in range(4):
                cps.append(_remote(_half(srcs[w].at[k], 1 - c, r2), outs[w].at[k], send_sems, recv_sems, 4 * w + k, sib))
        for cp in cps:
            cp.start() if phase == 0 else cp.wait()

    outs = [jax.ShapeDtypeStruct((4, g.shape[1] // 2, g.shape[2]), g.dtype) for g in gs]
    return _Comm(gs, outs, 4 * n, 2, run)


def _chips_exchange(ps):
    n = len(ps)

    def run(phase, srcs, outs, send_sems, recv_sems):
        x, y, c, chips = _where_am_i()
        me = 2 * x + y
        cps = []
        for w in range(n):
            for j, (cx, cy) in enumerate(chips):
                cps.append(_remote(srcs[w].at[2 * cx + cy], outs[w].at[me], send_sems, recv_sems, 3 * w + j, (cx, cy, c)))
        if phase == 0:
            for cp in cps:
                cp.start()
        else:
            for w in range(n):
                for j, (cx, cy) in enumerate(chips):
                    got = outs[w].at[2 * cx + cy]
                    _remote(got, got, send_sems, recv_sems, 3 * w + j, (cx, cy, c)).wait_recv()
            for cp in cps:
                cp.wait_send()

    outs = [jax.ShapeDtypeStruct(p.shape, p.dtype) for p in ps]
    return _Comm(ps, outs, 3 * n, 2, run)


def _sibling_swap(rs):
    n = len(rs)

    def run(phase, srcs, outs, send_sems, recv_sems):
        x, y, c, _ = _where_am_i()
        for w in range(n):
            cp = _remote(srcs[w], outs[w], send_sems, recv_sems, w, (x, y, 1 - c))
            cp.start() if phase == 0 else cp.wait()

    outs = [jax.ShapeDtypeStruct(r.shape, r.dtype) for r in rs]
    return _Comm(rs, outs, n, 2, run)


def _gather_all(v):
    rows = v.shape[0]

    def body(x_ref, out_ref, send_sems, recv_sems, local_sem):
        x, y, c, chips = _where_am_i()
        me, sib = (x, y, c), (x, y, 1 - c)
        blk = lambda px, py, pc: out_ref.at[4 * px + 2 * py + pc]
        mine = pltpu.make_async_copy(x_ref, blk(*me), local_sem)
        mine.start()
        first = [_remote(x_ref, blk(*me), send_sems, recv_sems, 0, sib)]
        first += [_remote(x_ref, blk(*me), send_sems, recv_sems, 1 + j, (*chip, c)) for j, chip in enumerate(chips)]
        for cp in first:
            cp.start()
        passed = [_remote(blk(*chip, c), blk(*chip, c), send_sems, recv_sems, 4 + j, sib) for j, chip in enumerate(chips)]
        for j, chip in enumerate(chips):
            _remote(blk(*chip, c), blk(*chip, c), send_sems, recv_sems, 1 + j, me).wait_recv()
            passed[j].start()
        _remote(blk(*sib), blk(*sib), send_sems, recv_sems, 0, me).wait_recv()
        for j, chip in enumerate(chips):
            _remote(blk(*chip, 1 - c), blk(*chip, 1 - c), send_sems, recv_sems, 4 + j, me).wait_recv()
        for cp in first + passed:
            cp.wait_send()
        mine.wait()

    vm = pl.BlockSpec(memory_space=pltpu.VMEM)
    return pl.pallas_call(
        body, name="gather_all", out_shape=jax.ShapeDtypeStruct((8, rows, LANES), v.dtype),
        in_specs=[vm], out_specs=vm,
        scratch_shapes=[pltpu.SemaphoreType.DMA((7,)), pltpu.SemaphoreType.DMA((7,)), pltpu.SemaphoreType.DMA],
    )(v)


def _flat_pad(parts, total):
    v = jnp.concatenate([p.reshape(-1) for p in parts])
    return jnp.pad(v, (0, total - v.shape[0]))


def _adamw(w, g, m, v):
    m = ADAM_B1 * m + (1.0 - ADAM_B1) * g
    v = ADAM_B2 * v + (1.0 - ADAM_B2) * jnp.square(g)
    m_hat = m / (1.0 - ADAM_B1 ** ADAM_STEP)
    v_hat = v / (1.0 - ADAM_B2 ** ADAM_STEP)
    delta = -ADAM_LR * (m_hat / (jnp.sqrt(v_hat) + ADAM_EPS) + ADAM_WD * w)
    return delta, m, v


def _adamw_call(w, g, m, v, name):
    r, cdim = w.shape
    tm = _tile(r, ROW_TILE, 8)
    o = _rout(r, cdim, f32, tm)
    return _rowwise(_adamw, r, tm, [_rows(w, tm), _rows(g, tm), _rows(m, tm), _rows(v, tm)], [o, o, o], [], name)


def _adamw_halves(w, m, v, g_mine, g_sib, sp, name):
    _, r, cdim = w.shape
    r2 = r // 2
    tr = _row_tile(r2, cdim, 8)
    nb = r2 // tr

    def body(sp_ref, w_ref, m_ref, v_ref, ga_ref, gb_ref, g_out, d_out, m_out, v_out):
        g = jnp.where(pl.program_id(0) == sp_ref[1], ga_ref[...], gb_ref[...])
        delta, mn, vn = _adamw(w_ref[...], g, m_ref[...], v_ref[...])
        g_out[...] = g
        d_out[...] = delta
        m_out[...] = mn
        v_out[...] = vn

    full = pl.BlockSpec((None, tr, cdim), lambda h, i, s: (0, h * nb + i, 0))
    mine = pl.BlockSpec((tr, cdim), lambda h, i, s: (jnp.where(h == s[1], i, 0), 0))
    sib = pl.BlockSpec((tr, cdim), lambda h, i, s: (jnp.where(h == s[1], 0, i), 0))
    out = jax.ShapeDtypeStruct((1, r, cdim), f32)
    gs = pltpu.PrefetchScalarGridSpec(num_scalar_prefetch=1, grid=(2, nb), in_specs=[full, full, full, mine, sib],
                                      out_specs=[full, full, full, full])
    return pl.pallas_call(body, name=name, out_shape=[out, out, out, out], grid_spec=gs,
                          compiler_params=_params(("parallel", "parallel")))(sp, w, m, v, g_mine, g_sib)


MIXER = ("w_in", "w_uq", "w_ukv", "w_out")
FFN = ("w_gate", "w_up", "w_down")
BIG = MIXER + FFN
SMALL = ("q_norm_w", "kv_norm_w", "conv_w", "conv_b", "dt_bias", "a_log", "d_skip", "ssd_norm_w", "attn_out_norm_w",
         "pre_mix_norm_w", "post_mix_norm_w", "pre_ffn_norm_w", "post_ffn_norm_w")
ORDER = ("w_in", "q_norm_w", "w_uq", "kv_norm_w", "w_ukv", "conv_w", "conv_b", "dt_bias", "a_log", "d_skip",
         "ssd_norm_w", "attn_out_norm_w", "w_out", "pre_mix_norm_w", "post_mix_norm_w", "pre_ffn_norm_w",
         "post_ffn_norm_w", "w_gate", "w_up", "w_down")


def kernel(x, positions, w_in, q_norm_w, w_uq, kv_norm_w, w_ukv, conv_w, conv_b, dt_bias, a_log, d_skip, ssd_norm_w, attn_out_norm_w, w_out, pre_mix_norm_w, post_mix_norm_w, pre_ffn_norm_w, post_ffn_norm_w, w_gate, w_up, w_down, loss_target, m_w_in, m_q_norm_w, m_w_uq, m_kv_norm_w, m_w_ukv, m_conv_w, m_conv_b, m_dt_bias, m_a_log, m_d_skip, m_ssd_norm_w, m_attn_out_norm_w, m_w_out, m_pre_mix_norm_w, m_post_mix_norm_w, m_pre_ffn_norm_w, m_post_ffn_norm_w, m_w_gate, m_w_up, m_w_down, v_w_in, v_q_norm_w, v_w_uq, v_kv_norm_w, v_w_ukv, v_conv_w, v_conv_b, v_dt_bias, v_a_log, v_d_skip, v_ssd_norm_w, v_attn_out_norm_w, v_w_out, v_pre_mix_norm_w, v_post_mix_norm_w, v_pre_ffn_norm_w, v_post_ffn_norm_w, v_w_gate, v_w_up, v_w_down):
    local = dict(locals())
    wts = {n: local[n][0] for n in ORDER}
    mom_m = {n: local["m_" + n][0] for n in ORDER}
    mom_v = {n: local["v_" + n][0] for n in ORDER}
    xs = x[0]
    tgt = loss_target[0]
    t, d = xs.shape
    nchip = 4
    my_x, my_y, my_c = lax.axis_index("x"), lax.axis_index("y"), lax.axis_index("c")
    my_chip = 2 * my_x + my_y

    mla_w = d // 2
    nh = mla_w // V_HEAD
    width = d - mla_w
    hs = width // SSD_P
    gn = SSD_G * SSD_N
    cdim = width + 2 * gn
    in_sizes = (Q_RANK, KV_RANK, ROPE, width, cdim, hs)
    d_in = sum(in_sizes)
    tail = LANES
    off_xbc = 0
    off_cq = cdim
    off_ckv = off_cq + Q_RANK
    off_z = off_ckv + KV_RANK
    off_tail = off_z + width
    d_in_p = _round_up(off_tail + tail, 256)
    gw = width // SSD_G
    assert off_cq % Q_RANK == 0 and off_ckv % KV_RANK == 0 and off_z % gw == 0 and off_tail % LANES == 0
    qk_head = NOPE + ROPE
    scale = qk_head ** -0.5
    tm = _tile(t, ROW_TILE, 8)
    tmw = _tile(t, ROW_TILE // 2, 8)

    sp = jnp.stack([my_chip, my_c]).astype(jnp.int32)

    def pair_sums(names, gl, from_sib):
        res = []
        for n, g, fs in zip(names, gl, from_sib):
            _, r2, cs = fs.shape
            tr = _row_tile(r2, cs, 16)
            nb = r2 // tr
            res.append(_blocked(
                lambda a, b: a + b, (nchip, nb),
                [(g, (None, tr, cs), lambda k, i, s, nb=nb: (k, s[1] * nb + i, 0)),
                 (fs, (None, tr, cs), lambda k, i, s: (k, i, 0))],
                [((nchip, r2, cs), bf16, (None, tr, cs), lambda k, i, s: (k, i, 0))], [], "pair_sum_" + n, sp=sp)[0])
        return res

    def chip_sums(names, gl, from_sib, from_chips):
        res = []
        for n, g, fs, fc in zip(names, gl, from_sib, from_chips):
            _, r2, cs = fs.shape
            tr = _row_tile(r2, cs, 16)
            nb = r2 // tr
            res.append(_blocked(
                lambda a, b, r1, r2_, r3: ((a + b) + r1.astype(f32)) + r2_.astype(f32) + r3.astype(f32), (nb,),
                [(g, (None, tr, cs), lambda i, s, nb=nb: (s[0], s[1] * nb + i, 0)),
                 (fs, (None, tr, cs), lambda i, s: (s[0], i, 0)),
                 (fc, (None, tr, cs), lambda i, s: (s[0] ^ 1, i, 0)),
                 (fc, (None, tr, cs), lambda i, s: (s[0] ^ 2, i, 0)),
                 (fc, (None, tr, cs), lambda i, s: (s[0] ^ 3, i, 0))],
                [((r2, cs), f32, (tr, cs), lambda i, s: (i, 0))], [], "chip_sum_" + n, sp=sp)[0])
        return res

    ck, ccs = wts["conv_w"].shape
    cs_in = wts["w_in"].shape[1]
    flat_t = lambda a3: jnp.swapaxes(a3, 1, 2).reshape(1, cs_in * d // LANES, LANES)
    w_in_g, conv_g = _comm_call(_gather_weights([flat_t(w_in)[0].astype(bf16)], [wts["conv_w"]]), "gather_w_in")
    cat_cols = lambda g: jnp.concatenate([g[k] for k in range(nchip)], axis=1)
    conv_full = cat_cols(conv_g)
    mixer_gather = _gather_weights([wts[n].astype(bf16) for n in MIXER[1:] + FFN[2:]], [])
    ffn_gather = _gather_weights([wts[n].astype(bf16) for n in FFN[:2]], [])

    wi = w_in_g.reshape(nchip * cs_in, d)
    o = np.cumsum((0,) + in_sizes)
    seg = lambda i: wi[o[i]:o[i + 1]]
    w_in_pt = jnp.concatenate([seg(4), seg(0), seg(1), seg(3), seg(2), seg(5),
                               jnp.zeros((d_in_p - off_tail - ROPE - hs, d), bf16)], axis=0)

    inv_freq = ROPE_THETA ** (-jnp.arange(0, ROPE, 2, dtype=f32) / ROPE)
    ang = positions[0].astype(f32)[:, None] * inv_freq
    cos, sin = jnp.cos(ang), jnp.sin(ang)
    z32, z64, z96 = jnp.zeros((t, 32), f32), jnp.zeros((t, 64), f32), jnp.zeros((t, 96), f32)
    cosp = jnp.concatenate([cos, cos, z64], axis=1)
    sina = jnp.concatenate([-sin, z96], axis=1)
    sinb = jnp.concatenate([z32, sin, z64], axis=1)

    row = lambda a: a.reshape(1, -1)
    w_pre_mix, w_post_mix = row(wts["pre_mix_norm_w"]), row(wts["post_mix_norm_w"])
    w_pre_ffn, w_post_ffn = row(wts["pre_ffn_norm_w"]), row(wts["post_ffn_norm_w"])
    w_qn, w_kvn = row(wts["q_norm_w"]), row(wts["kv_norm_w"])
    w_attn_n, w_ssd_n = row(wts["attn_out_norm_w"]), row(wts["ssd_norm_w"])
    conv_b_r = row(wts["conv_b"])
    dtb, alog, dskip = row(wts["dt_bias"]), row(wts["a_log"]), row(wts["d_skip"])
    dtb_t, alog_t = dtb.reshape(hs, 1), alog.reshape(hs, 1)

    u = _rowwise(lambda a, w: _rms(a, w), t, tm, [_rows(xs, tm), _par(w_pre_mix)], [_rout(t, d, bf16, tm)], [],
                 "pre_mix_norm")[0]
    proj, (w_uq_g, w_ukv_f, w_out_g, w_down_g) = _matmul([(u, w_in_pt)], "nt", f32, "in_proj", comm=mixer_gather,
                                                         at=(0.0, 0.6))
    w_down_f = w_down_g.reshape(-1, w_down_g.shape[2])
    w_uq_p = jnp.pad(cat_cols(w_uq_g).reshape(Q_RANK, nh, qk_head), ((0, 0), (0, 0), (0, QK_PAD - qk_head))
                     ).reshape(Q_RANK, nh * QK_PAD)
    w_out_f = w_out_g.reshape(-1, w_out_g.shape[2])
    cq_in = _rows(proj, tm, Q_RANK, off_cq // Q_RANK)
    ckv_in = _rows(proj, tm, KV_RANK, off_ckv // KV_RANK)
    cqn = _rowwise(lambda a, w: _rms(a, w), t, tm, [cq_in, _par(w_qn)], [_rout(t, Q_RANK, bf16, tm)], [], "q_norm")[0]
    ckvn = _rowwise(lambda a, w: _rms(a, w), t, tm, [ckv_in, _par(w_kvn)], [_rout(t, KV_RANK, bf16, tm)], [],
                    "kv_norm")[0]
    q_raw = _matmul([(cqn, w_uq_p)], "nn", f32, "q_up")
    kv = _matmul([(ckvn, w_ukv_f)], "nn", bf16, "kv_up")

    def q_rope_fn(qt, cp, sa, sb):
        parts = []
        for h in range(nh):
            parts.append(qt[:, h * QK_PAD: h * QK_PAD + NOPE])
            parts.append(_rope(qt[:, h * QK_PAD + NOPE:(h + 1) * QK_PAD], cp, sa, sb))
        return jnp.concatenate(parts, axis=1)

    tail_cb = off_tail // LANES
    q2 = _rowwise(q_rope_fn, t, tm, [_rows(q_raw, tm), _rows(cosp, tm), _rows(sina, tm), _rows(sinb, tm)],
                  [_rout(t, nh * QK_PAD, bf16, tm)], [], "q_rope")[0]
    kr2 = _rowwise(_rope, t, tm, [_rows(proj, tm, LANES, tail_cb), _rows(cosp, tm), _rows(sina, tm), _rows(sinb, tm)],
                   [_rout(t, LANES, bf16, tm)], [], "k_rope")[0]
    o_att, lse, ffn_w = _flash_fwd(q2, kv, kr2, nh, scale, ffn_gather, (0.0, 0.75))
    w_gate_f, w_up_f = ffn_w

    xbc_act = _conv_fwd(proj, conv_full, conv_b_r, cdim, tm)
    dt_raw, dt_raw_t = _blocked(
        lambda blk: (blk[:, ROPE:ROPE + hs], blk.T[ROPE:ROPE + hs, :]), (t // tm,),
        [_rows(proj, tm, LANES, tail_cb)],
        [((t, hs), f32, (tm, hs), lambda i: (i, 0)), ((hs, t), f32, (hs, tm), lambda i: (0, i))], [], "dt_split")
    y_ssd, hprev = _ssd_fwd(xbc_act, dt_raw, dt_raw_t, dtb, dtb_t, alog, alog_t, dskip, width)
    z_ins = [_rows(proj, tm, gw, off_z // gw + i) for i in range(SSD_G)]

    def mix_norms_fn(ov, yv, *rest):
        zs, wa, ws = rest[:SSD_G], rest[SSD_G], rest[SSD_G + 1]
        outs = [_rms(ov, wa)]
        for i in range(SSD_G):
            sl = slice(i * gw, (i + 1) * gw)
            outs.append(_rms(yv[:, sl] * (zs[i] * _sigmoid(zs[i])), ws[:, sl]))
        return jnp.concatenate(outs, axis=1)

    cat = _rowwise(mix_norms_fn, t, tm, [_rows(o_att, tm), _rows(y_ssd, tm)] + z_ins + [_par(w_attn_n), _par(w_ssd_n)],
                   [_rout(t, d, bf16, tm)], [], "attn_ssd_out_norms")[0]
    mix = _matmul([(cat, w_out_f)], "nn", f32, "out_proj")

    def post_mix_fn(mx, xv, w1, w2):
        h1v = xv + _rms(mx, w1)
        return h1v, _rms(h1v, w2)

    h1, v_in = _rowwise(post_mix_fn, t, tmw, [_rows(mix, tmw), _rows(xs, tmw), _par(w_post_mix), _par(w_pre_ffn)],
                        [_rout(t, d, f32, tmw), _rout(t, d, bf16, tmw)], [], "post_mix_pre_ffn_norm")
    g_ff, u_ff, act = _ffn_up(v_in, w_gate_f, w_up_f)
    ffn = _matmul([(act, w_down_f)], "nn", f32, "ffn_down")

    def final_fn(fv, h1v, tg, w):
        h2 = h1v + _rms(fv, w)
        err = h2 - tg
        lpart = 0.5 * jnp.sum(jnp.sum(err * err, axis=1, keepdims=True), axis=0, keepdims=True) / d
        dh2 = err / d
        dff, dw = _rms_bwd(fv, w, dh2)
        return dff, dh2, jnp.broadcast_to(lpart, (1, LANES)), dw

    dffn, dh2, loss_acc, g_post_ffn = _rowwise(
        final_fn, t, tmw, [_rows(ffn, tmw), _rows(h1, tmw), _rows(tgt, tmw), _par(w_post_ffn)],
        [_rout(t, d, bf16, tmw), _rout(t, d, f32, tmw)], [(1, LANES), (1, d)], "loss_post_ffn_norm_bwd")
    loss = lax.psum(loss_acc[0, 0], ("x", "y", "c"))

    dg_ff, du_ff = _ffn_down_bwd(dffn, w_down_f, g_ff, u_ff)
    gw_down = _matmul([(act, dffn)], "tn", f32, "grad_w_down")
    gw_gate = _matmul([(v_in, dg_ff)], "tn", f32, "grad_w_gate", out_chunks=nchip)
    gw_up = _matmul([(v_in, du_ff)], "tn", f32, "grad_w_up", out_chunks=nchip)
    ffn_g = [gw_gate, gw_up, gw_down.reshape(nchip, -1, d)]
    dv_in, ffn_from_sib = _matmul([(dg_ff, w_gate_f), (du_ff, w_up_f)], "nt", f32, "ffn_up_bwd",
                                  comm=_sibling_send_halves(ffn_g), at=(0.0,))
    ffn_pairs = pair_sums(FFN, ffn_g, ffn_from_sib)

    def mid_bwd_fn(h1v, dvv, dh2v, mx, w_pf, w_pm):
        dxn, dw_pf = _rms_bwd(h1v, w_pf, dvv)
        dh1v = dh2v + dxn
        dmx, dw_pm = _rms_bwd(mx, w_pm, dh1v)
        return dh1v, dmx, dw_pf, dw_pm

    dh1, dmix, g_pre_ffn, g_post_mix = _rowwise(
        mid_bwd_fn, t, tmw, [_rows(h1, tmw), _rows(dv_in, tmw), _rows(dh2, tmw), _rows(mix, tmw),
                             _par(w_pre_ffn), _par(w_post_mix)],
        [_rout(t, d, f32, tmw), _rout(t, d, bf16, tmw)], [(1, d), (1, d)], "pre_ffn_post_mix_norm_bwd")
    dcat = _matmul([(dmix, w_out_f)], "nt", f32, "out_proj_bwd")
    gw_out = _matmul([(cat, dmix)], "tn", f32, "grad_w_out")

    def mix_norms_bwd_fn(ov, yv, *rest):
        zs, dcv, wa, ws = rest[:SSD_G], rest[SSD_G], rest[SSD_G + 1], rest[SSD_G + 2]
        dov, dwa = _rms_bwd(ov, wa, dcv[:, :mla_w])
        dl = [jnp.broadcast_to(jnp.sum(dov[:, h * V_HEAD:(h + 1) * V_HEAD] * ov[:, h * V_HEAD:(h + 1) * V_HEAD],
                                       axis=1, keepdims=True), (ov.shape[0], V_HEAD)) for h in range(nh)]
        dys, dzs, dws = [], [], []
        for i in range(SSD_G):
            sl = slice(i * gw, (i + 1) * gw)
            zv, yi = zs[i], yv[:, sl]
            sg = _sigmoid(zv)
            sz = zv * sg
            dgi, dwi = _rms_bwd(yi * sz, ws[:, sl], dcv[:, mla_w + i * gw: mla_w + (i + 1) * gw])
            dys.append(dgi * sz)
            dzs.append(dgi * yi * (sg * (1.0 + zv * (1.0 - sg))))
            dws.append(dwi)
        cc = lambda a: jnp.concatenate(a, axis=1)
        return dov, cc(dl), cc(dys), cc(dzs), dwa, cc(dws)

    do_att, delta, dy_ssd, dz, g_attn_n, g_ssd_n = _rowwise(
        mix_norms_bwd_fn, t, tm,
        [_rows(o_att, tm), _rows(y_ssd, tm)] + z_ins + [_rows(dcat, tm), _par(w_attn_n), _par(w_ssd_n)],
        [_rout(t, mla_w, bf16, tm), _rout(t, mla_w, f32, tm), _rout(t, width, f32, tm), _rout(t, width, bf16, tm)],
        [(1, mla_w), (1, width)], "attn_ssd_out_norms_bwd")
    dkv, dkr_h, dq2, gu_from_chips = _flash_bwd(q2, kv, kr2, do_att, lse, delta, nh, scale,
                                                _chips_exchange(ffn_pairs[:2]), (0.0,))

    def q_rope_bwd_fn(dqt, cp, sa, sb):
        parts = []
        for h in range(nh):
            parts.append(dqt[:, h * QK_PAD: h * QK_PAD + NOPE])
            parts.append(_rope_bwd(dqt[:, h * QK_PAD + NOPE:(h + 1) * QK_PAD], cp, sa, sb))
        return jnp.concatenate(parts, axis=1)

    dq_raw = _rowwise(q_rope_bwd_fn, t, tm, [_rows(dq2, tm), _rows(cosp, tm), _rows(sina, tm), _rows(sinb, tm)],
                      [_rout(t, nh * QK_PAD, bf16, tm)], [], "q_rope_bwd")[0]

    def k_rope_bwd_fn(dk, cp, sa, sb):
        tot = dk[:, 0:LANES]
        for h in range(1, nh):
            tot = tot + dk[:, h * LANES:(h + 1) * LANES]
        return _rope_bwd(tot, cp, sa, sb)

    dkr = _rowwise(k_rope_bwd_fn, t, tm, [_rows(dkr_h, tm), _rows(cosp, tm), _rows(sina, tm), _rows(sinb, tm)],
                   [_rout(t, LANES, f32, tm)], [], "k_rope_bwd")[0]
    gw_uq_p = _matmul([(cqn, dq_raw)], "tn", f32, "grad_w_uq")
    gw_ukv = _matmul([(ckvn, dkv)], "tn", f32, "grad_w_ukv", out_chunks=nchip)
    dcqn = _matmul([(dq_raw, w_uq_p)], "nt", f32, "q_up_bwd")
    dckvn = _matmul([(dkv, w_ukv_f)], "nt", f32, "kv_up_bwd")

    def lat_norm_bwd_fn(a, w, dyv):
        return _rms_bwd(a, w, dyv)

    dcq, g_qn = _rowwise(lat_norm_bwd_fn, t, tm, [cq_in, _par(w_qn), _rows(dcqn, tm)],
                         [_rout(t, Q_RANK, bf16, tm)], [(1, Q_RANK)], "q_norm_bwd")
    dckv, g_kvn = _rowwise(lat_norm_bwd_fn, t, tm, [ckv_in, _par(w_kvn), _rows(dckvn, tm)],
                           [_rout(t, KV_RANK, bf16, tm)], [(1, KV_RANK)], "kv_norm_bwd")

    dxbc_act, ddt_raw, g_alog, g_dskip, g_dtb, down_from_chips = _ssd_bwd(
        xbc_act, dt_raw, dt_raw_t, dtb, dtb_t, alog, alog_t, dskip, hprev, dy_ssd, width,
        _chips_exchange(ffn_pairs[2:]), (0.0,))
    ffn_halves = chip_sums(FFN, ffn_g, ffn_from_sib, gu_from_chips + down_from_chips)
    dpre, gcw0, gcw1, gcw2, gcw3, g_conv_b = _conv_bwd_pre(proj, conv_full, conv_b_r, dxbc_act, cdim, tm)
    g_conv_w = jnp.concatenate([gcw0, gcw1, gcw2, gcw3], axis=0)
    dxbc = _conv_bwd_dx(dpre, conv_full, tm)

    def dproj_fn(a, b, c, e, kr_blk, dt_blk):
        parts = [a, b, c, e, (kr_blk + dt_blk).astype(bf16)]
        if d_in_p > off_tail + tail:
            parts.append(jnp.zeros((a.shape[0], d_in_p - off_tail - tail), bf16))
        return jnp.concatenate(parts, axis=1)

    dproj = _rowwise(dproj_fn, t, tm, [_rows(dxbc, tm), _rows(dcq, tm), _rows(dckv, tm), _rows(dz, tm),
                                       _rows(dkr, tm), _rows(ddt_raw, tm)],
                     [_rout(t, d_in_p, bf16, tm)], [], "in_proj_grad_pack")[0]
    split_cols = lambda gf: jnp.stack(jnp.split(gf, nchip, axis=1))
    split_rows = lambda gf: gf.reshape(nchip, gf.shape[0] // nchip, gf.shape[1])
    gw_uq = gw_uq_p.reshape(Q_RANK, nh, QK_PAD)[:, :, :qk_head].reshape(Q_RANK, nh * qk_head)
    rest_g = [split_cols(gw_uq), gw_ukv, split_rows(gw_out)]
    du_in, rest_from_sib = _matmul([(dproj, w_in_pt)], "nn", f32, "in_proj_bwd",
                                   comm=_sibling_send_halves(rest_g), at=(0.0,))
    rest_pairs = pair_sums(MIXER[1:], rest_g, rest_from_sib)
    gw_in_pt, rest_from_chips = _matmul([(dproj, u)], "tn", f32, "grad_w_in",
                                        comm=_chips_exchange(rest_pairs), at=(0.0,))
    rest_halves = chip_sums(MIXER[1:], rest_g, rest_from_sib, rest_from_chips)

    def first_bwd_fn(xv, duv, dh1v, w):
        dxn, dw = _rms_bwd(xv, w, duv)
        return dh1v + dxn, dw

    grad_x, g_pre_mix = _rowwise(first_bwd_fn, t, tmw, [_rows(xs, tmw), _rows(du_in, tmw), _rows(dh1, tmw),
                                                         _par(w_pre_mix)],
                                 [_rout(t, d, f32, tmw)], [(1, d)], "pre_mix_norm_bwd")

    gseg = lambda a, b: gw_in_pt[a:b]
    gw_in_t = jnp.concatenate([gseg(off_cq, off_ckv), gseg(off_ckv, off_z), gseg(off_tail, off_tail + ROPE),
                               gseg(off_z, off_tail), gseg(off_xbc, off_cq),
                               gseg(off_tail + ROPE, off_tail + ROPE + hs)], axis=0)
    in_g = [gw_in_t.reshape(nchip, cs_in * d // LANES, LANES)]
    in_from_sib = _comm_call(_sibling_send_halves(in_g), "sibling_send_halves")
    in_pairs = pair_sums(MIXER[:1], in_g, in_from_sib)
    in_from_chips = _comm_call(_chips_exchange(in_pairs), "chips_exchange")
    mix_halves = chip_sums(MIXER[:1], in_g, in_from_sib, in_from_chips) + rest_halves
    halves = mix_halves + ffn_halves
    sib_halves = _comm_call(_sibling_swap(halves), "sibling_swap")
    gshard = {}

    gsmall = {"q_norm_w": g_qn, "kv_norm_w": g_kvn, "conv_w": g_conv_w, "conv_b": g_conv_b, "dt_bias": g_dtb,
              "a_log": g_alog, "d_skip": g_dskip, "ssd_norm_w": g_ssd_n, "attn_out_norm_w": g_attn_n,
              "pre_mix_norm_w": g_pre_mix, "post_mix_norm_w": g_post_mix, "pre_ffn_norm_w": g_pre_ffn,
              "post_ffn_norm_w": g_post_ffn}
    small_sizes = [int(np.prod(gsmall[n].shape)) for n in SMALL]
    srows = _round_up(-(-sum(small_sizes) // LANES), 8)
    spart = _flat_pad([gsmall[n] for n in SMALL], srows * LANES).reshape(srows, LANES)
    sall = _gather_all(spart)

    def sum8_fn(a):
        tot = a[0]
        for k in range(1, 8):
            tot = tot + a[k]
        return tot

    ssum = _blocked(sum8_fn, (1,), [(sall, sall.shape, lambda i: (0, 0, 0))],
                    [((srows, LANES), f32, (srows, LANES), lambda i: (0, 0))], [], "small_grad_sum")[0].reshape(-1)
    gred = {}
    off = 0
    for n, sz in zip(SMALL, small_sizes):
        gred[n] = ssum[off:off + sz].reshape(gsmall[n].shape)
        off += sz
    gshard["conv_w"] = lax.dynamic_slice_in_dim(gred["conv_w"], my_chip * ccs, ccs, axis=1)
    for n in SMALL:
        if n != "conv_w":
            gshard[n] = gred[n].reshape(wts[n].shape)

    delta, new_m, new_v = {}, {}, {}
    big_out = {}
    for n, mine_h, sib_h in zip(BIG, halves, sib_halves):
        view = flat_t if n == "w_in" else (lambda a3: a3)
        res = _adamw_halves(view(local[n]), view(local["m_" + n]), view(local["v_" + n]), mine_h, sib_h, sp,
                            "adamw_" + n)
        big_out[n] = [jnp.swapaxes(a.reshape(1, cs_in, d), 1, 2) for a in res] if n == "w_in" else res
    pack = lambda src: _flat_pad([src[n] for n in SMALL], srows * LANES).reshape(srows, LANES)
    sd, sm, sv = _adamw_call(pack(wts), pack(gshard), pack(mom_m), pack(mom_v), "adamw_small")
    off = 0
    for n in SMALL:
        sz = int(np.prod(wts[n].shape))
        for dst, src in ((delta, sd), (new_m, sm), (new_v, sv)):
            dst[n] = src.reshape(-1)[off:off + sz].reshape(wts[n].shape)
        off += sz

    small_out = (gshard, delta, new_m, new_v)
    pick = lambda k: [big_out[n][k] if n in big_out else small_out[k][n][None] for n in ORDER]
    return (loss, grad_x[None], *pick(0), *pick(1), *pick(2), *pick(3))
```

```python
import functools

import numpy as np
import jax
import jax.numpy as jnp
from jax import lax
from jax.experimental import pallas as pl
from jax.experimental.pallas import tpu as pltpu

f32, bf16 = jnp.float32, jnp.bfloat16

EPS = 1e-6
V_HEAD = 128
NOPE = 128
ROPE = 64
QK_PAD = 256
Q_RANK = 512
KV_RANK = 512
ROPE_THETA = 10000.0
SSD_P = 64
SSD_G = 2
SSD_N = 128
SSD_K = 4
CHUNK = 128
ADAM_LR, ADAM_B1, ADAM_B2, ADAM_EPS, ADAM_WD, ADAM_STEP = 0.001, 0.9, 0.999, 1e-08, 0.01, 10

VMEM_LIMIT_BYTES = 48 * 1024 * 1024
LANES = 128
ATT_TILE = 1024
ATT_TILE_FWD = 1024
MM_TM, MM_TN, MM_TK = 1408, 1024, 1408
CHUNK_WHOLE_MAX = 1536
ROW_TILE = 256
ROW_BLOCK_BYTES = 2 * 1024 * 1024

NN = (((1,), (0,)), ((), ()))
NT = (((1,), (1,)), ((), ()))
TN = (((0,), (0,)), ((), ()))
MESH = pl.DeviceIdType.MESH
ANY = pl.BlockSpec(memory_space=pl.ANY)


def _tile(dim, cap, align=LANES):
    if dim <= cap:
        return dim
    t = (cap // align) * align
    while t >= align:
        if dim % t == 0:
            return t
        t -= align
    raise ValueError(f"no tile for {dim} under {cap}")


def _row_tile(rows, cols, align):
    best = None
    for tr in range(align, rows + 1, align):
        if rows % tr == 0 and tr * cols * 4 <= ROW_BLOCK_BYTES:
            best = tr
    return best or rows


def _round_up(n, m):
    return -(-n // m) * m


def _params(sem):
    return pltpu.CompilerParams(dimension_semantics=sem, vmem_limit_bytes=VMEM_LIMIT_BYTES)


def _dot(a, b, dims):
    return lax.dot_general(a.astype(bf16), b.astype(bf16), dims, preferred_element_type=f32)


def _call(body, name, out_shape, grid, in_specs, out_specs, scratch_shapes, sem, args, comm=None, at=None,
          prefetch=()):
    npf = len(prefetch)
    if comm is None:
        gs = pltpu.PrefetchScalarGridSpec(num_scalar_prefetch=npf, grid=grid, in_specs=list(in_specs),
                                          out_specs=list(out_specs), scratch_shapes=list(scratch_shapes))
        res = pl.pallas_call(body, name=name, out_shape=list(out_shape), grid_spec=gs,
                             compiler_params=_params(sem))(*prefetch, *args)
        return list(res), []
    n_in, n_out, n_sc = len(args), len(out_shape), len(scratch_shapes)
    na, no = len(comm.args), len(comm.outs)
    steps = int(np.prod(grid))

    def full(*allrefs):
        pf, refs = allrefs[:npf], allrefs[npf:]
        cin = refs[n_in:n_in + na]
        o0 = n_in + na
        cout = refs[o0 + n_out:o0 + n_out + no]
        s0 = o0 + n_out + no
        send_sems, recv_sems = refs[s0 + n_sc], refs[s0 + n_sc + 1]
        lin = pl.program_id(0)
        for dim in range(1, len(grid)):
            lin = lin * grid[dim] + pl.program_id(dim)
        for p in range(comm.nphase - 1):
            @pl.when(lin == int(round(at[p] * (steps - 1))))
            def _(p=p):
                comm.run(p, cin, cout, send_sems, recv_sems)
        body(*pf, *refs[:n_in], *refs[o0:o0 + n_out], *refs[s0:s0 + n_sc])

        @pl.when(lin == steps - 1)
        def _():
            comm.run(comm.nphase - 1, cin, cout, send_sems, recv_sems)

    gs = pltpu.PrefetchScalarGridSpec(
        num_scalar_prefetch=npf, grid=grid, in_specs=list(in_specs) + [ANY] * na,
        out_specs=list(out_specs) + [ANY] * no, scratch_shapes=list(scratch_shapes) + comm.sems())
    res = pl.pallas_call(full, name=name, out_shape=list(out_shape) + comm.outs, grid_spec=gs,
                         compiler_params=_params(("arbitrary",) * len(grid)))(*prefetch, *args, *comm.args)
    return list(res[:n_out]), list(res[n_out:])


def _chunk_tile(cs, cap):
    return cs if cs <= CHUNK_WHOLE_MAX else _tile(cs, cap)


def _matmul(pairs, mode, out_dtype, name, out_chunks=None, comm=None, at=None):
    a0, b0 = pairs[0]
    chunked = b0.ndim == 3
    cs = b0.shape[2] if chunked else None
    bcols = b0.shape[0] * b0.shape[2] if chunked else b0.shape[1]
    brows = b0.shape[1] if chunked else b0.shape[0]
    if mode == "nn":
        (m, k), n = a0.shape, bcols
    elif mode == "nt":
        (m, k), n = a0.shape, brows
    else:
        (k, m), n = a0.shape, bcols
    tm = _tile(m, MM_TM)
    if mode == "nt":
        tn = _tile(n, MM_TN)
        tk = _chunk_tile(cs, MM_TK) if chunked else _tile(k, MM_TK)
    else:
        tk = _tile(k, MM_TK)
        if chunked:
            tn = _chunk_tile(cs, MM_TN)
        elif out_chunks:
            tn = _chunk_tile(n // out_chunks, MM_TN)
        else:
            tn = _tile(n, MM_TN)
    nk = k // tk
    if mode == "nn":
        a_spec = pl.BlockSpec((tm, tk), lambda i, j, kk: (i, kk))
        if chunked:
            q = cs // tn
            b_spec = pl.BlockSpec((None, tk, tn), lambda i, j, kk: (j // q, kk, j % q))
        else:
            b_spec = pl.BlockSpec((tk, tn), lambda i, j, kk: (kk, j))
        dims = NN
    elif mode == "nt":
        a_spec = pl.BlockSpec((tm, tk), lambda i, j, kk: (i, kk))
        if chunked:
            q = cs // tk
            b_spec = pl.BlockSpec((None, tn, tk), lambda i, j, kk: (kk // q, j, kk % q))
        else:
            b_spec = pl.BlockSpec((tn, tk), lambda i, j, kk: (j, kk))
        dims = NT
    else:
        a_spec = pl.BlockSpec((tk, tm), lambda i, j, kk: (kk, i))
        b_spec = pl.BlockSpec((tk, tn), lambda i, j, kk: (kk, j))
        dims = TN
    if out_chunks:
        qo = (n // out_chunks) // tn
        out_shape = jax.ShapeDtypeStruct((out_chunks, m, n // out_chunks), out_dtype)
        o_spec = pl.BlockSpec((None, tm, tn), lambda i, j, kk: (j // qo, i, j % qo))
    else:
        out_shape = jax.ShapeDtypeStruct((m, n), out_dtype)
        o_spec = pl.BlockSpec((tm, tn), lambda i, j, kk: (i, j))
    npair = len(pairs)

    def body(*refs):
        o_ref, acc = refs[2 * npair], refs[2 * npair + 1]
        kk = pl.program_id(2)

        @pl.when(kk == 0)
        def _():
            acc[...] = jnp.zeros_like(acc)

        part = _dot(refs[0][...], refs[1][...], dims)
        for p in range(1, npair):
            part = part + _dot(refs[2 * p][...], refs[2 * p + 1][...], dims)
        acc[...] += part

        @pl.when(kk == nk - 1)
        def _():
            o_ref[...] = acc[...].astype(out_dtype)

    args = [t for pr in pairs for t in pr]
    res, cres = _call(body, name, [out_shape], (m // tm, n // tn, nk), [a_spec, b_spec] * npair, [o_spec],
                      [pltpu.VMEM((tm, tn), f32)], ("parallel", "parallel", "arbitrary"), args, comm, at)
    return res[0] if comm is None else (res[0], cres)


def _sigmoid(x):
    return 1.0 / (1.0 + jnp.exp(-x))


def _ffn_up(v, wg, wu):
    m, k = v.shape
    nchunk, _, cs = wg.shape
    n = nchunk * cs
    tm, tn = _tile(m, 512), _chunk_tile(cs, 512)
    q = cs // tn
    w_spec = pl.BlockSpec((None, k, tn), lambda j, i: (j // q, 0, j % q))

    def body(v_ref, wg_ref, wu_ref, g_ref, u_ref, act_ref):
        vb = v_ref[...]
        g = _dot(vb, wg_ref[...], NN)
        u = _dot(vb, wu_ref[...], NN)
        sg = _sigmoid(g)
        silu = g * sg
        g_ref[...] = silu.astype(bf16)
        u_ref[...] = (u * (sg * (1.0 + g * (1.0 - sg)))).astype(bf16)
        act_ref[...] = (silu * u).astype(bf16)

    out = jax.ShapeDtypeStruct((m, n), bf16)
    o_spec = pl.BlockSpec((tm, tn), lambda j, i: (i, j))
    return pl.pallas_call(
        body, name="ffn_up", out_shape=(out, out, out), grid=(n // tn, m // tm),
        in_specs=[pl.BlockSpec((tm, k), lambda j, i: (i, 0)), w_spec, w_spec],
        out_specs=(o_spec, o_spec, o_spec),
        compiler_params=_params(("parallel", "parallel")),
    )(v, wg, wu)


def _ffn_down_bwd(dffn, wd, dact_du, dact_dg):
    m, k = dffn.shape
    n = wd.shape[0]
    tm, tn = _tile(m, 1024), _tile(n, 512)

    def body(d_ref, w_ref, fu_ref, fg_ref, dg_ref, du_ref):
        dact = _dot(d_ref[...], w_ref[...], NT)
        du_ref[...] = (dact * fu_ref[...].astype(f32)).astype(bf16)
        dg_ref[...] = (dact * fg_ref[...].astype(f32)).astype(bf16)

    out = jax.ShapeDtypeStruct((m, n), bf16)
    o_spec = pl.BlockSpec((tm, tn), lambda i, j: (i, j))
    return pl.pallas_call(
        body, name="ffn_down_bwd", out_shape=(out, out), grid=(m // tm, n // tn),
        in_specs=[pl.BlockSpec((tm, k), lambda i, j: (i, 0)), pl.BlockSpec((tn, k), lambda i, j: (j, 0)),
                  o_spec, o_spec],
        out_specs=(o_spec, o_spec),
        compiler_params=_params(("parallel", "parallel")),
    )(dffn, wd, dact_du, dact_dg)


def _blocked(fn, grid, ins, outs, accs, name, sp=None):
    n_in, n_out, n_acc = len(ins), len(outs), len(accs)
    nsp = 0 if sp is None else 1

    def body(*refs):
        refs = refs[nsp:]
        tiles = [r[...] for r in refs[:n_in]]
        res = fn(*tiles)
        if not isinstance(res, (tuple, list)):
            res = (res,)
        for r, val in zip(refs[n_in:n_in + n_out], res[:n_out]):
            r[...] = val.astype(r.dtype)
        if n_acc:
            first = pl.program_id(0) == 0
            for d in range(1, len(grid)):
                first = jnp.logical_and(first, pl.program_id(d) == 0)

            @pl.when(first)
            def _():
                for r in refs[n_in + n_out:]:
                    r[...] = jnp.zeros_like(r)

            for r, val in zip(refs[n_in + n_out:], res[n_out:]):
                r[...] += val

    def acc_map(shape):
        zeros = (0,) * len(shape)
        return lambda *idx: zeros

    in_specs = [pl.BlockSpec(bs, im) for _, bs, im in ins]
    out_specs = [pl.BlockSpec(bs, im) for _, _, bs, im in outs] + [pl.BlockSpec(s, acc_map(s)) for s in accs]
    out_shape = [jax.ShapeDtypeStruct(s, d) for s, d, _, _ in outs] + [jax.ShapeDtypeStruct(s, f32) for s in accs]
    sem = ("arbitrary",) * len(grid) if n_acc else ("parallel",) * len(grid)
    args = [a for a, _, _ in ins]
    if sp is None:
        res = pl.pallas_call(body, name=name, out_shape=out_shape, grid=grid, in_specs=in_specs,
                             out_specs=out_specs, compiler_params=_params(sem))(*args)
    else:
        gs = pltpu.PrefetchScalarGridSpec(num_scalar_prefetch=1, grid=grid, in_specs=in_specs, out_specs=out_specs)
        res = pl.pallas_call(body, name=name, out_shape=out_shape, grid_spec=gs,
                             compiler_params=_params(sem))(sp, *args)
    return res


def _rows(a, tm, cols=None, cb=0):
    w = a.shape[1] if cols is None else cols
    return (a, (tm, w), lambda i: (i, cb))


def _par(a):
    zeros = (0,) * a.ndim
    return (a, a.shape, lambda i: zeros)


def _rout(t, w, dtype, tm):
    return ((t, w), dtype, (tm, w), lambda i: (i, 0))


def _rowwise(fn, t, tm, ins, outs, accs, name):
    return _blocked(fn, (t // tm,), ins, outs, accs, name)


def _rms(x, w):
    r = lax.rsqrt(jnp.mean(x * x, axis=-1, keepdims=True) + EPS)
    return x * r * w


def _rms_bwd(x, w, dy):
    r = lax.rsqrt(jnp.mean(x * x, axis=-1, keepdims=True) + EPS)
    xh = x * r
    dyw = dy * w
    dx = r * (dyw - xh * jnp.mean(dyw * xh, axis=-1, keepdims=True))
    return dx, jnp.sum(dy * xh, axis=0, keepdims=True)


def _silu_grad(x):
    s = _sigmoid(x)
    return s * (1.0 + x * (1.0 - s))


def _rope(blk, cosp, sina, sinb):
    return blk * cosp + pltpu.roll(blk, 96, 1) * sina + pltpu.roll(blk, 32, 1) * sinb


def _rope_bwd(dy, cosp, sina, sinb):
    return dy * cosp + pltpu.roll(dy * sina, 32, 1) + pltpu.roll(dy * sinb, 96, 1)


HALO = 8


def _conv_taps(buf, w, tm, base):
    acc = buf[base:base + tm, :] * w[0:1]
    for k in range(1, SSD_K):
        acc = acc + buf[base + k:base + k + tm, :] * w[k:k + 1]
    return acc


def _conv_specs(t, tm, cdim):
    cur = pl.BlockSpec((tm, cdim), lambda i: (i, 0))
    prev = pl.BlockSpec((HALO, cdim), lambda i: (jnp.maximum(i * (tm // HALO) - 1, 0), 0))
    nxt = pl.BlockSpec((HALO, cdim), lambda i: (jnp.minimum((i + 1) * (tm // HALO), t // HALO - 1), 0))
    return cur, prev, nxt


def _conv_fwd(src, w, b, cdim, tm):
    t = src.shape[0]
    cur, prev, _ = _conv_specs(t, tm, cdim)

    def body(x_ref, p_ref, w_ref, b_ref, o_ref, buf):
        buf[0:HALO, :] = jnp.where(pl.program_id(0) > 0, p_ref[...], 0.0)
        buf[HALO:HALO + tm, :] = x_ref[...]
        pre = _conv_taps(buf, w_ref[...], tm, HALO - (SSD_K - 1)) + b_ref[...]
        o_ref[...] = pre * _sigmoid(pre)

    par = lambda a: pl.BlockSpec(a.shape, lambda i: (0, 0))
    return pl.pallas_call(
        body, name="conv_silu", out_shape=jax.ShapeDtypeStruct((t, cdim), f32), grid=(t // tm,),
        in_specs=[cur, prev, par(w), par(b)], out_specs=cur,
        scratch_shapes=[pltpu.VMEM((tm + HALO, cdim), f32)], compiler_params=_params(("parallel",)),
    )(src, src, w, b)


def _conv_bwd_pre(src, w, b, dact, cdim, tm):
    t = src.shape[0]
    cur, prev, _ = _conv_specs(t, tm, cdim)

    def body(x_ref, p_ref, w_ref, b_ref, d_ref, dpre_ref, dw0, dw1, dw2, dw3, db, buf):
        @pl.when(pl.program_id(0) == 0)
        def _():
            for r in (dw0, dw1, dw2, dw3, db):
                r[...] = jnp.zeros_like(r)

        buf[0:HALO, :] = jnp.where(pl.program_id(0) > 0, p_ref[...], 0.0)
        buf[HALO:HALO + tm, :] = x_ref[...]
        base = HALO - (SSD_K - 1)
        pre = _conv_taps(buf, w_ref[...], tm, base) + b_ref[...]
        dpre = d_ref[...] * _silu_grad(pre)
        dpre_ref[...] = dpre
        for k, r in enumerate((dw0, dw1, dw2, dw3)):
            r[...] += jnp.sum(dpre * buf[base + k:base + k + tm, :], axis=0, keepdims=True)
        db[...] += jnp.sum(dpre, axis=0, keepdims=True)

    par = lambda a: pl.BlockSpec(a.shape, lambda i: (0, 0))
    acc = pl.BlockSpec((1, cdim), lambda i: (0, 0))
    acc_shape = jax.ShapeDtypeStruct((1, cdim), f32)
    return pl.pallas_call(
        body, name="conv_silu_bwd", out_shape=[jax.ShapeDtypeStruct((t, cdim), f32)] + [acc_shape] * 5,
        grid=(t // tm,), in_specs=[cur, prev, par(w), par(b), cur], out_specs=[cur] + [acc] * 5,
        scratch_shapes=[pltpu.VMEM((tm + HALO, cdim), f32)], compiler_params=_params(("arbitrary",)),
    )(src, src, w, b, dact)


def _conv_bwd_dx(dpre, w, tm):
    t, cdim = dpre.shape
    cur, _, nxt = _conv_specs(t, tm, cdim)
    last = t // tm - 1

    def body(d_ref, n_ref, w_ref, o_ref, buf):
        buf[0:tm, :] = d_ref[...]
        buf[tm:tm + HALO, :] = jnp.where(pl.program_id(0) < last, n_ref[...], 0.0)
        wv = w_ref[...]
        acc = buf[0:tm, :] * wv[SSD_K - 1:SSD_K]
        for k in range(SSD_K - 1):
            s = SSD_K - 1 - k
            acc = acc + buf[s:s + tm, :] * wv[k:k + 1]
        o_ref[...] = acc.astype(bf16)

    return pl.pallas_call(
        body, name="conv_bwd_dx", out_shape=jax.ShapeDtypeStruct((t, cdim), bf16), grid=(t // tm,),
        in_specs=[cur, nxt, pl.BlockSpec(w.shape, lambda i: (0, 0))], out_specs=cur,
        scratch_shapes=[pltpu.VMEM((tm + HALO, cdim), f32)], compiler_params=_params(("parallel",)),
    )(dpre, dpre, w)


def _causal_mask(s, row0=0):
    row = lax.broadcasted_iota(jnp.int32, s.shape, 0) + row0
    col = lax.broadcasted_iota(jnp.int32, s.shape, 1)
    return jnp.where(row >= col, s, -jnp.inf)


def _causal_pairs(nq, q_major):
    pairs = [(qi, ki) for qi in range(nq) for ki in range(qi + 1)]
    if not q_major:
        pairs.sort(key=lambda p: (p[1], p[0]))
    return (jnp.asarray([p[0] for p in pairs], jnp.int32), jnp.asarray([p[1] for p in pairs], jnp.int32))


def _flash_fwd(q, kv, kr, nheads, scale, comm=None, at=None):
    t = q.shape[0]
    tq = _tile(t, ATT_TILE_FWD)
    nq = t // tq

    qtab, ktab = _causal_pairs(nq, q_major=True)
    hp = 2 if nheads % 2 == 0 else 1

    def body(qt, kt, q_ref, kv_ref, kr_ref, o_ref, lse_ref, m_sc, l_sc, acc_sc):
        qi, ki = qt[pl.program_id(1)], kt[pl.program_id(1)]

        @pl.when(ki == 0)
        def _():
            m_sc[...] = jnp.full_like(m_sc, -jnp.inf)
            l_sc[...] = jnp.zeros_like(l_sc)
            acc_sc[...] = jnp.zeros_like(acc_sc)

        def step(masked, last):
            krb = kr_ref[...]
            ss = []
            for j in range(hp):
                k = jnp.concatenate([kv_ref[:, j * QK_PAD: j * QK_PAD + NOPE], krb], axis=1)
                ss.append(lax.dot_general(q_ref[:, j * QK_PAD:(j + 1) * QK_PAD], k, NT, preferred_element_type=f32))
            soft = []
            for j in range(hp):
                s = ss[j] * scale
                if masked:
                    s = _causal_mask(s)
                m_old = m_sc[j]
                m_new = jnp.maximum(m_old, jnp.max(s, axis=1, keepdims=True))
                alpha = jnp.exp(m_old - m_new)
                p = jnp.exp(s - m_new)
                soft.append((m_new, alpha, alpha * l_sc[j] + jnp.sum(p, axis=1, keepdims=True), p.astype(bf16)))
            for j in range(hp):
                m_new, alpha, l, p = soft[j]
                v = kv_ref[:, j * QK_PAD + NOPE:(j + 1) * QK_PAD]
                acc = alpha * acc_sc[j] + lax.dot_general(p, v, NN, preferred_element_type=f32)
                if last:
                    o_ref[:, j * V_HEAD:(j + 1) * V_HEAD] = acc / l
                    lse_ref[:, j * V_HEAD:(j + 1) * V_HEAD] = jnp.broadcast_to(m_new + jnp.log(l), (tq, V_HEAD))
                else:
                    l_sc[j] = l
                    acc_sc[j] = acc
                    m_sc[j] = m_new

        @pl.when(ki < qi)
        def _():
            step(False, False)

        @pl.when(ki == qi)
        def _():
            step(True, True)

    o_spec = pl.BlockSpec((tq, hp * V_HEAD), lambda h, s, qt, kt: (qt[s], h))
    out = jax.ShapeDtypeStruct((t, nheads * V_HEAD), f32)
    (o, lse), cres = _call(
        body, "flash_fwd", [out, out], (nheads // hp, len(qtab)),
        [pl.BlockSpec((tq, hp * QK_PAD), lambda h, s, qt, kt: (qt[s], h)),
         pl.BlockSpec((tq, hp * QK_PAD), lambda h, s, qt, kt: (kt[s], h)),
         pl.BlockSpec((tq, LANES), lambda h, s, qt, kt: (kt[s], 0))],
        [o_spec, o_spec],
        [pltpu.VMEM((hp, tq, 1), f32), pltpu.VMEM((hp, tq, 1), f32), pltpu.VMEM((hp, tq, V_HEAD), f32)],
        ("parallel", "arbitrary"), [q, kv, kr], comm, at, prefetch=(qtab, ktab))
    return o, lse, cres


def _flash_bwd(q, kv, kr, do, lse, delta, nheads, scale, comm=None, at=None):
    t = q.shape[0]
    tq = _tile(t, ATT_TILE)
    nq = t // tq
    qtab, ktab = _causal_pairs(nq, q_major=False)
    nsub = 2 if tq % 32 == 0 else 1

    def body(qt, kt, q_ref, kn_ref, kr_ref, v_ref, do_ref, lse_ref, dl_ref, dkv_ref, dkr_ref, dq_ref, dk_sc, dv_sc):
        qi, ki = qt[pl.program_id(1)], kt[pl.program_id(1)]

        @pl.when(pl.program_id(1) == 0)
        def _():
            dq_ref[...] = jnp.zeros_like(dq_ref)

        @pl.when(qi == ki)
        def _():
            dk_sc[...] = jnp.zeros_like(dk_sc)
            dv_sc[...] = jnp.zeros_like(dv_sc)

        def step(masked):
            k = jnp.concatenate([kn_ref[...], kr_ref[...]], axis=1)
            vb = v_ref[...]
            parts = [slice(i * (tq // nsub), (i + 1) * (tq // nsub)) for i in range(nsub)]
            qs = [q_ref[r, :] for r in parts]
            dos = [do_ref[r, :] for r in parts]
            ss = [lax.dot_general(qb, k, NT, preferred_element_type=f32) for qb in qs]
            dps = [lax.dot_general(dob, vb, NT, preferred_element_type=f32) for dob in dos]
            ps, dss = [], []
            for r, s, dp in zip(parts, ss, dps):
                s = s * scale
                if masked:
                    s = _causal_mask(s, r.start)
                p = jnp.exp(s - lse_ref[r, 0:1])
                ps.append(p.astype(bf16))
                dss.append((p * (dp - dl_ref[r, 0:1]) * scale).astype(bf16))
            for r, qb, dob, p, ds in zip(parts, qs, dos, ps, dss):
                dv_sc[...] += lax.dot_general(p, dob, TN, preferred_element_type=f32)
                dk_sc[...] += lax.dot_general(ds, qb, TN, preferred_element_type=f32)
                rows = pl.ds(pl.multiple_of(qi * tq + r.start, tq // nsub), tq // nsub)
                dq_ref[rows, :] += lax.dot_general(ds, k, NN, preferred_element_type=f32)

        @pl.when(qi > ki)
        def _():
            step(False)

        @pl.when(qi == ki)
        def _():
            step(True)

        @pl.when(qi == nq - 1)
        def _():
            dk = dk_sc[...]
            dkv_ref[...] = jnp.concatenate([dk[:, :NOPE], dv_sc[...]], axis=1).astype(bf16)
            dkr_ref[...] = dk[:, NOPE:]

    hspec = pl.BlockSpec((tq, V_HEAD), lambda h, s, qt, kt: (qt[s], h))
    (dkv, dkr, dq), cres = _call(
        body, "flash_bwd",
        [jax.ShapeDtypeStruct((t, nheads * QK_PAD), bf16), jax.ShapeDtypeStruct((t, nheads * LANES), f32),
         jax.ShapeDtypeStruct((t, nheads * QK_PAD), f32)],
        (nheads, len(qtab)),
        [pl.BlockSpec((tq, QK_PAD), lambda h, s, qt, kt: (qt[s], h)),
         pl.BlockSpec((tq, NOPE), lambda h, s, qt, kt: (kt[s], 2 * h)),
         pl.BlockSpec((tq, LANES), lambda h, s, qt, kt: (kt[s], 0)),
         pl.BlockSpec((tq, V_HEAD), lambda h, s, qt, kt: (kt[s], 2 * h + 1)),
         hspec, hspec, hspec],
        [pl.BlockSpec((tq, QK_PAD), lambda h, s, qt, kt: (kt[s], h)),
         pl.BlockSpec((tq, LANES), lambda h, s, qt, kt: (kt[s], h)),
         pl.BlockSpec((t, QK_PAD), lambda h, s, qt, kt: (0, h))],
        [pltpu.VMEM((tq, QK_PAD), f32), pltpu.VMEM((tq, V_HEAD), f32)],
        ("parallel", "arbitrary"), [q, kv, kr, kv, do, lse, delta], comm, at, prefetch=(qtab, ktab))
    return dkv, dkr, dq, cres


def _split3(a):
    hi = a.astype(bf16)
    r1 = a - hi.astype(f32)
    mid = r1.astype(bf16)
    lo = (r1 - mid.astype(f32)).astype(bf16)
    return hi, mid, lo


def _split2(a):
    hi = a.astype(bf16)
    return hi, (a - hi.astype(f32)).astype(bf16)


def _ones_dot_left(tri, a):
    hi, mid, lo = _split3(a)
    d = lambda v: lax.dot_general(tri, v, NN, preferred_element_type=f32)
    return d(hi) + d(mid) + d(lo)


def _ones_dot_right(a, tri):
    hi, mid, lo = _split3(a)
    d = lambda v: lax.dot_general(v, tri, NN, preferred_element_type=f32)
    return d(hi) + d(mid) + d(lo)


def _softplus(x):
    return jnp.maximum(x, 0.0) + jnp.log(1.0 + jnp.exp(-jnp.abs(x)))


def _head_spread(hs, width):
    shift = SSD_P.bit_length() - 1
    return (lax.broadcasted_iota(jnp.int32, (hs, width), 0)
            == lax.shift_right_logical(lax.broadcasted_iota(jnp.int32, (hs, width), 1), shift)).astype(bf16)


def _ssd_common(dt_ref, dtT_ref, dtb_ref, dtbT_ref, alog_ref, alogT_ref):
    ii = lax.broadcasted_iota(jnp.int32, (CHUNK, CHUNK), 0)
    jj = lax.broadcasted_iota(jnp.int32, (CHUNK, CHUNK), 1)
    tri = ii >= jj
    raw = dt_ref[...] + dtb_ref[...]
    dt = _softplus(raw)
    a_neg = -jnp.exp(alog_ref[...])
    cum = _ones_dot_left(tri.astype(bf16), dt * a_neg)
    dt_t = _softplus(dtT_ref[...] + dtbT_ref[...])
    cum_t = _ones_dot_right(dt_t * (-jnp.exp(alogT_ref[...])), (ii <= jj).astype(bf16))
    return tri, raw, dt, a_neg, cum, cum_t


def _ssd_fwd(xbc, dt, dt_t, dtb, dtb_t, alog, alog_t, dskip, width, comm=None, at=None):
    t, cdim = xbc.shape
    hs = dt.shape[1]
    nc = t // CHUNK
    epg = hs // SSD_G
    gn = SSD_G * SSD_N

    def body(x_ref, dt_ref, dtT_ref, dtb_ref, dtbT_ref, alog_ref, alogT_ref, d_ref, y_ref, hp_ref, h_sc, yd_sc):
        @pl.when(pl.program_id(0) == 0)
        def _():
            h_sc[...] = jnp.zeros_like(h_sc)

        tri, _, dtv, _, cum, cum_t = _ssd_common(dt_ref, dtT_ref, dtb_ref, dtbT_ref, alog_ref, alogT_ref)
        spread = _head_spread(hs, width)
        clast = cum[CHUNK - 1:CHUNK, :]
        dec = jnp.exp(clast)
        wide = _ones_dot_right(jnp.concatenate([dtv, jnp.exp(cum), jnp.exp(clast - cum),
                                                jnp.broadcast_to(d_ref[...], (CHUNK, hs))], axis=0), spread)
        dt_x, ee_x, ff_x, dsk_x = (wide[i * CHUNK:(i + 1) * CHUNK] for i in range(4))
        xs = x_ref[:, :width]
        xdt = xs * dt_x
        xf = xdt * ff_x
        h_all = h_sc[...]
        hp_ref[0] = h_all
        h_all = h_all.reshape(hs * SSD_P, SSD_N)
        ch_parts = []
        for g in range(SSD_G):
            gsl = slice(g * epg * SSD_P, (g + 1) * epg * SSD_P)
            bb = x_ref[:, width + g * SSD_N: width + (g + 1) * SSD_N].astype(bf16)
            cb_ = x_ref[:, width + gn + g * SSD_N: width + gn + (g + 1) * SSD_N].astype(bf16)
            cbm = lax.dot_general(cb_, bb, NT, preferred_element_type=f32)
            ch_parts.append(_dot(cb_, h_all[gsl], NT))
            st = _dot(xf[:, gsl], bb, TN)
            for e in range(g * epg, (g + 1) * epg):
                esl = slice(e * SSD_P, (e + 1) * SSD_P)
                lmat = jnp.exp(jnp.where(tri, cum[:, e:e + 1] - cum_t[e:e + 1, :], -jnp.inf))
                yd_sc[:, esl] = _dot(cbm * lmat, xdt[:, esl], NN)
                j = e - g * epg
                h_sc[e] = h_sc[e] * dec[:, e:e + 1] + st[j * SSD_P:(j + 1) * SSD_P, :]
        y_ref[...] = yd_sc[...] + jnp.concatenate(ch_parts, axis=1) * ee_x + xs * dsk_x

    par = lambda a: pl.BlockSpec(a.shape, lambda i: (0,) * a.ndim)
    (y, hp), cres = _call(
        body, "ssd_fwd",
        [jax.ShapeDtypeStruct((t, width), f32), jax.ShapeDtypeStruct((nc, hs, SSD_P, SSD_N), f32)], (nc,),
        [pl.BlockSpec((CHUNK, cdim), lambda i: (i, 0)), pl.BlockSpec((CHUNK, hs), lambda i: (i, 0)),
         pl.BlockSpec((hs, CHUNK), lambda i: (0, i)), par(dtb), par(dtb_t), par(alog), par(alog_t), par(dskip)],
        [pl.BlockSpec((CHUNK, width), lambda i: (i, 0)), pl.BlockSpec((1, hs, SSD_P, SSD_N), lambda i: (i, 0, 0, 0))],
        [pltpu.VMEM((hs, SSD_P, SSD_N), f32), pltpu.VMEM((CHUNK, width), f32)], ("arbitrary",),
        [xbc, dt, dt_t, dtb, dtb_t, alog, alog_t, dskip], comm, at)
    return y, hp, cres


def _ssd_bwd(xbc, dt, dt_t, dtb, dtb_t, alog, alog_t, dskip, hprev, dy, width, comm=None, at=None):
    t, cdim = xbc.shape
    hs = dt.shape[1]
    nc = t // CHUNK
    epg = hs // SSD_G
    gn = SSD_G * SSD_N

    def body(x_ref, dt_ref, dtT_ref, dtb_ref, dtbT_ref, alog_ref, alogT_ref, d_ref, hp_ref, dy_ref,
             dx_ref, ddt_ref, dalog_ref, ddsk_ref, ddtb_ref, dh_sc, dxd_sc):
        @pl.when(pl.program_id(0) == 0)
        def _():
            dh_sc[...] = jnp.zeros_like(dh_sc)
            dalog_ref[...] = jnp.zeros_like(dalog_ref)
            ddsk_ref[...] = jnp.zeros_like(ddsk_ref)
            ddtb_ref[...] = jnp.zeros_like(ddtb_ref)

        tri, raw, dtv, a_neg, cum, cum_t = _ssd_common(dt_ref, dtT_ref, dtb_ref, dtbT_ref, alog_ref, alogT_ref)
        dsk = d_ref[...]
        head_row = lax.broadcasted_iota(jnp.int32, (1, hs), 1)
        head_col = lax.broadcasted_iota(jnp.int32, (hs, 1), 0)
        last_row = (lax.broadcasted_iota(jnp.int32, (CHUNK, 1), 0) == CHUNK - 1).astype(f32)
        shift = SSD_P.bit_length() - 1
        spread = _head_spread(hs, width)
        gather = (lax.shift_right_logical(lax.broadcasted_iota(jnp.int32, (width, hs), 0), shift)
                  == lax.broadcasted_iota(jnp.int32, (width, hs), 1)).astype(bf16)
        clast = cum[CHUNK - 1:CHUNK, :]
        ee = jnp.exp(cum)
        ff = jnp.exp(clast - cum)
        dec = jnp.exp(clast)
        wide = _ones_dot_right(jnp.concatenate([dtv, ee, ff, jnp.broadcast_to(dsk, (CHUNK, hs))], axis=0), spread)
        dt_x, ee_x, ff_x, dsk_x = (wide[i * CHUNK:(i + 1) * CHUNK] for i in range(4))
        xs = x_ref[:, :width]
        dyv = dy_ref[...]
        xdt = xs * dt_x
        dye = dyv * ee_x
        xf = xdt * ff_x
        h_all = hp_ref[0].reshape(hs * SSD_P, SSD_N)
        dh_all = dh_sc[...].reshape(hs * SSD_P, SSD_N)
        hi, lo = _split2(_ones_dot_left(spread, dh_all * h_all))
        ones8 = jnp.ones((8, SSD_N), bf16)
        hh = (lax.dot_general(ones8, hi, NT, preferred_element_type=f32)
              + lax.dot_general(ones8, lo, NT, preferred_element_type=f32))[0:1]
        rowsum_m = jnp.zeros((CHUNK, hs), f32)
        colsum_m = jnp.zeros((hs, CHUNK), f32)
        ch_parts, bds_parts = [], []
        for g in range(SSD_G):
            bsl = slice(width + g * SSD_N, width + (g + 1) * SSD_N)
            csl = slice(width + gn + g * SSD_N, width + gn + (g + 1) * SSD_N)
            gsl = slice(g * epg * SSD_P, (g + 1) * epg * SSD_P)
            bb = x_ref[:, bsl].astype(bf16)
            cb_ = x_ref[:, csl].astype(bf16)
            cbm = lax.dot_general(cb_, bb, NT, preferred_element_type=f32)
            hg = h_all[gsl].astype(bf16)
            dhg = dh_all[gsl].astype(bf16)
            ch_parts.append(_dot(cb_, hg, NT))
            bds_parts.append(_dot(bb, dhg, NT))
            dcg = _dot(dye[:, gsl], hg, NN)
            dbg = _dot(xf[:, gsl], dhg, NN)
            dh_new = _dot(dye[:, gsl], cb_, TN)
            dcb = jnp.zeros((CHUNK, CHUNK), f32)
            for e in range(g * epg, (g + 1) * epg):
                esl = slice(e * SSD_P, (e + 1) * SSD_P)
                lmat = jnp.exp(jnp.where(tri, cum[:, e:e + 1] - cum_t[e:e + 1, :], -jnp.inf))
                gmat = cbm * lmat
                dy_e = dyv[:, esl].astype(bf16)
                dgm = _dot(dy_e, xdt[:, esl], NT)
                dxd_sc[:, esl] = _dot(gmat, dy_e, TN)
                dcb = dcb + dgm * lmat
                mm = dgm * gmat
                rowsum_m = rowsum_m + jnp.sum(mm, axis=1, keepdims=True) * (head_row == e).astype(f32)
                colsum_m = colsum_m + (head_col == e).astype(f32) * jnp.sum(mm, axis=0, keepdims=True)
                j = e - g * epg
                dh_sc[e] = dh_new[j * SSD_P:(j + 1) * SSD_P, :] + dec[:, e:e + 1] * dh_sc[e]
            dx_ref[:, bsl] = dbg + _dot(dcb, cb_, TN)
            dx_ref[:, csl] = dcg + _dot(dcb, bb, NN)
        ch_all = jnp.concatenate(ch_parts, axis=1)
        bds_all = jnp.concatenate(bds_parts, axis=1)
        dxdt = bds_all * ff_x + dxd_sc[...]
        dx_ref[:, :width] = dxdt * dt_x + dyv * dsk_x
        sums = _ones_dot_right(jnp.concatenate([dxdt * xs, dyv * ch_all, bds_all * xdt, dyv * xs], axis=0), gather)
        ddtx_all = sums[0:CHUNK]
        dff = sums[2 * CHUNK:3 * CHUNK] * ff
        ddsk = jnp.sum(sums[3 * CHUNK:4 * CHUNK], axis=0, keepdims=True)
        dclast = jnp.sum(dff, axis=0, keepdims=True) + dec * hh
        eye = (lax.broadcasted_iota(jnp.int32, (hs, hs), 0) == lax.broadcasted_iota(jnp.int32, (hs, hs), 1)).astype(bf16)
        colsum_t = sum(lax.dot_general(v, eye, TN, preferred_element_type=f32) for v in _split3(colsum_m))
        dcum_all = sums[CHUNK:2 * CHUNK] * ee - dff + rowsum_m - colsum_t + dclast * last_row
        ii = lax.broadcasted_iota(jnp.int32, (CHUNK, CHUNK), 0)
        jj = lax.broadcasted_iota(jnp.int32, (CHUNK, CHUNK), 1)
        da = _ones_dot_left((jj >= ii).astype(bf16), dcum_all)
        ddt = da * a_neg + ddtx_all
        dalog_ref[...] += jnp.sum(da * dtv, axis=0, keepdims=True) * a_neg
        draw = ddt * _sigmoid(raw)
        place = (lax.broadcasted_iota(jnp.int32, (hs, LANES), 0) + ROPE
                 == lax.broadcasted_iota(jnp.int32, (hs, LANES), 1)).astype(bf16)
        ddt_ref[...] = _ones_dot_right(draw, place)
        ddtb_ref[...] += jnp.sum(draw, axis=0, keepdims=True)
        ddsk_ref[...] += ddsk

    rev = lambda i: nc - 1 - i
    par = lambda a: pl.BlockSpec(a.shape, lambda i: (0,) * a.ndim)
    acc = pl.BlockSpec((1, hs), lambda i: (0, 0))
    acc_shape = jax.ShapeDtypeStruct((1, hs), f32)
    res, cres = _call(
        body, "ssd_bwd",
        [jax.ShapeDtypeStruct((t, cdim), f32), jax.ShapeDtypeStruct((t, LANES), f32), acc_shape, acc_shape, acc_shape],
        (nc,),
        [pl.BlockSpec((CHUNK, cdim), lambda i: (rev(i), 0)), pl.BlockSpec((CHUNK, hs), lambda i: (rev(i), 0)),
         pl.BlockSpec((hs, CHUNK), lambda i: (0, rev(i))), par(dtb), par(dtb_t), par(alog), par(alog_t),
         par(dskip), pl.BlockSpec((1, hs, SSD_P, SSD_N), lambda i: (rev(i), 0, 0, 0)),
         pl.BlockSpec((CHUNK, width), lambda i: (rev(i), 0))],
        [pl.BlockSpec((CHUNK, cdim), lambda i: (rev(i), 0)), pl.BlockSpec((CHUNK, LANES), lambda i: (rev(i), 0)),
         acc, acc, acc],
        [pltpu.VMEM((hs, SSD_P, SSD_N), f32), pltpu.VMEM((CHUNK, width), f32)], ("arbitrary",),
        [xbc, dt, dt_t, dtb, dtb_t, alog, alog_t, dskip, hprev, dy], comm, at)
    return (*res, cres)


def _where_am_i():
    x, y, c = lax.axis_index("x"), lax.axis_index("y"), lax.axis_index("c")
    chips = [(1 - x, y), (x, 1 - y), (1 - x, 1 - y)]
    return x, y, c, chips


def _remote(src, dst, send_sems, recv_sems, k, to):
    return pltpu.make_async_remote_copy(src_ref=src, dst_ref=dst, send_sem=send_sems.at[k], recv_sem=recv_sems.at[k],
                                        device_id=to, device_id_type=MESH)


class _Comm:
    def __init__(self, args, outs, nsem, nphase, run):
        self.args, self.outs, self.nsem, self.nphase, self.run = list(args), list(outs), nsem, nphase, run

    def sems(self):
        return [pltpu.SemaphoreType.DMA((self.nsem,)), pltpu.SemaphoreType.DMA((self.nsem,))]


def _comm_call(comm, name):
    na, no = len(comm.args), len(comm.outs)

    def body(*refs):
        for phase in range(comm.nphase):
            comm.run(phase, refs[:na], refs[na:na + no], refs[na + no], refs[na + no + 1])

    return list(pl.pallas_call(body, name=name, out_shape=comm.outs, in_specs=[ANY] * na, out_specs=[ANY] * no,
                               scratch_shapes=comm.sems())(*comm.args))


def _half(ref, c, r2):
    return ref.at[pl.ds(c * r2, r2)]


def _gather_weights(shards, wholes):
    ns, nw = len(shards), len(wholes)
    per = 7

    def run(phase, srcs, outs, send_sems, recv_sems):
        x, y, c, chips = _where_am_i()
        me, sib = 2 * x + y, (x, y, 1 - c)
        def first():
            cps = []
            for w in range(ns + nw):
                src, out, base = srcs[w], outs[w], per * w
                halved = w < ns
                r2 = src.shape[0] // 2
                piece = _half(src, c, r2) if halved else src
                for j, (cx, cy) in enumerate(chips):
                    dst = _half(out.at[me], c, r2) if halved else out.at[me]
                    cps.append(_remote(piece, dst, send_sems, recv_sems, base + j, (cx, cy, c)))
                cps.append(_remote(src, out.at[me], send_sems, recv_sems, base + 6, sib))
            return cps

        def passed():
            cps = []
            for w in range(ns):
                r2 = srcs[w].shape[0] // 2
                for j, (cx, cy) in enumerate(chips):
                    got = _half(outs[w].at[2 * cx + cy], c, r2)
                    cps.append(_remote(got, got, send_sems, recv_sems, per * w + 3 + j, sib))
            return cps

        if phase == 0:
            for cp in first():
                cp.start()
        elif phase == 1:
            it = iter(passed())
            for w in range(ns + nw):
                src, out, base = srcs[w], outs[w], per * w
                r2 = src.shape[0] // 2
                for j, (cx, cy) in enumerate(chips):
                    got = _half(out.at[2 * cx + cy], c, r2) if w < ns else out.at[2 * cx + cy]
                    _remote(got, got, send_sems, recv_sems, base + j, sib).wait_recv()
                    if w < ns:
                        next(it).start()
        else:
            for w in range(ns + nw):
                src, out, base = srcs[w], outs[w], per * w
                r2 = src.shape[0] // 2
                if w < ns:
                    for j, (cx, cy) in enumerate(chips):
                        got = _half(out.at[2 * cx + cy], 1 - c, r2)
                        _remote(got, got, send_sems, recv_sems, base + 3 + j, sib).wait_recv()
                _remote(src, out.at[me], send_sems, recv_sems, base + 6, sib).wait_recv()
            for cp in first() + passed():
                cp.wait_send()

    args = list(shards) + list(wholes)
    outs = [jax.ShapeDtypeStruct((4,) + a.shape, a.dtype) for a in args]
    return _Comm(args, outs, per * len(args), 3, run)


def _sibling_send_halves(gs):
    n = len(gs)

    def run(phase, srcs, outs, send_sems, recv_sems):
        x, y, c, _ = _where_am_i()
        sib = (x, y, 1 - c)
        cps = []
        for w in range(n):
            r2 = srcs[w].shape[1] // 2
            for k in range(4):
                cps.append(_remote(_half(srcs[w].at[k], 1 - c, r2), outs[w].at[k], send_sems, recv_sems, 4 * w + k, sib))
        for cp in cps:
            cp.start() if phase == 0 else cp.wait()

    outs = [jax.ShapeDtypeStruct((4, g.shape[1] // 2, g.shape[2]), g.dtype) for g in gs]
    return _Comm(gs, outs, 4 * n, 2, run)


def _chips_exchange(ps):
    n = len(ps)

    def run(phase, srcs, outs, send_sems, recv_sems):
        x, y, c, chips = _where_am_i()
        me = 2 * x + y
        cps = []
        for w in range(n):
            for j, (cx, cy) in enumerate(chips):
                cps.append(_remote(srcs[w].at[2 * cx + cy], outs[w].at[me], send_sems, recv_sems, 3 * w + j, (cx, cy, c)))
        if phase == 0:
            for cp in cps:
                cp.start()
        else:
            for w in range(n):
                for j, (cx, cy) in enumerate(chips):
                    got = outs[w].at[2 * cx + cy]
                    _remote(got, got, send_sems, recv_sems, 3 * w + j, (cx, cy, c)).wait_recv()
            for cp in cps:
                cp.wait_send()

    outs = [jax.ShapeDtypeStruct(p.shape, p.dtype) for p in ps]
    return _Comm(ps, outs, 3 * n, 2, run)


def _sibling_swap(rs):
    n = len(rs)

    def run(phase, srcs, outs, send_sems, recv_sems):
        x, y, c, _ = _where_am_i()
        for w in range(n):
            cp = _remote(srcs[w], outs[w], send_sems, recv_sems, w, (x, y, 1 - c))
            cp.start() if phase == 0 else cp.wait()

    outs = [jax.ShapeDtypeStruct(r.shape, r.dtype) for r in rs]
    return _Comm(rs, outs, n, 2, run)


def _gather_all(v):
    rows = v.shape[0]

    def body(x_ref, out_ref, send_sems, recv_sems, local_sem):
        x, y, c, chips = _where_am_i()
        me, sib = (x, y, c), (x, y, 1 - c)
        blk = lambda px, py, pc: out_ref.at[4 * px + 2 * py + pc]
        mine = pltpu.make_async_copy(x_ref, blk(*me), local_sem)
        mine.start()
        first = [_remote(x_ref, blk(*me), send_sems, recv_sems, 0, sib)]
        first += [_remote(x_ref, blk(*me), send_sems, recv_sems, 1 + j, (*chip, c)) for j, chip in enumerate(chips)]
        for cp in first:
            cp.start()
        passed = [_remote(blk(*chip, c), blk(*chip, c), send_sems, recv_sems, 4 + j, sib) for j, chip in enumerate(chips)]
        for j, chip in enumerate(chips):
            _remote(blk(*chip, c), blk(*chip, c), send_sems, recv_sems, 1 + j, me).wait_recv()
            passed[j].start()
        _remote(blk(*sib), blk(*sib), send_sems, recv_sems, 0, me).wait_recv()
        for j, chip in enumerate(chips):
            _remote(blk(*chip, 1 - c), blk(*chip, 1 - c), send_sems, recv_sems, 4 + j, me).wait_recv()
        for cp in first + passed:
            cp.wait_send()
        mine.wait()

    vm = pl.BlockSpec(memory_space=pltpu.VMEM)
    return pl.pallas_call(
        body, name="gather_all", out_shape=jax.ShapeDtypeStruct((8, rows, LANES), v.dtype),
        in_specs=[vm], out_specs=vm,
        scratch_shapes=[pltpu.SemaphoreType.DMA((7,)), pltpu.SemaphoreType.DMA((7,)), pltpu.SemaphoreType.DMA],
    )(v)


def _flat_pad(parts, total):
    v = jnp.concatenate([p.reshape(-1) for p in parts])
    return jnp.pad(v, (0, total - v.shape[0]))


def _adamw(w, g, m, v):
    m = ADAM_B1 * m + (1.0 - ADAM_B1) * g
    v = ADAM_B2 * v + (1.0 - ADAM_B2) * jnp.square(g)
    m_hat = m / (1.0 - ADAM_B1 ** ADAM_STEP)
    v_hat = v / (1.0 - ADAM_B2 ** ADAM_STEP)
    delta = -ADAM_LR * (m_hat / (jnp.sqrt(v_hat) + ADAM_EPS) + ADAM_WD * w)
    return delta, m, v


def _adamw_call(w, g, m, v, name):
    r, cdim = w.shape
    tm = _tile(r, ROW_TILE, 8)
    o = _rout(r, cdim, f32, tm)
    return _rowwise(_adamw, r, tm, [_rows(w, tm), _rows(g, tm), _rows(m, tm), _rows(v, tm)], [o, o, o], [], name)


def _adamw_halves(w, m, v, g_mine, g_sib, sp, name):
    _, r, cdim = w.shape
    r2 = r // 2
    tr = _row_tile(r2, cdim, 8)
    nb = r2 // tr

    def body(sp_ref, w_ref, m_ref, v_ref, ga_ref, gb_ref, g_out, d_out, m_out, v_out):
        g = jnp.where(pl.program_id(0) == sp_ref[1], ga_ref[...], gb_ref[...])
        delta, mn, vn = _adamw(w_ref[...], g, m_ref[...], v_ref[...])
        g_out[...] = g
        d_out[...] = delta
        m_out[...] = mn
        v_out[...] = vn

    full = pl.BlockSpec((None, tr, cdim), lambda h, i, s: (0, h * nb + i, 0))
    mine = pl.BlockSpec((tr, cdim), lambda h, i, s: (jnp.where(h == s[1], i, 0), 0))
    sib = pl.BlockSpec((tr, cdim), lambda h, i, s: (jnp.where(h == s[1], 0, i), 0))
    out = jax.ShapeDtypeStruct((1, r, cdim), f32)
    gs = pltpu.PrefetchScalarGridSpec(num_scalar_prefetch=1, grid=(2, nb), in_specs=[full, full, full, mine, sib],
                                      out_specs=[full, full, full, full])
    return pl.pallas_call(body, name=name, out_shape=[out, out, out, out], grid_spec=gs,
                          compiler_params=_params(("parallel", "parallel")))(sp, w, m, v, g_mine, g_sib)


MIXER = ("w_in", "w_uq", "w_ukv", "w_out")
FFN = ("w_gate", "w_up", "w_down")
BIG = MIXER + FFN
SMALL = ("q_norm_w", "kv_norm_w", "conv_w", "conv_b", "dt_bias", "a_log", "d_skip", "ssd_norm_w", "attn_out_norm_w",
         "pre_mix_norm_w", "post_mix_norm_w", "pre_ffn_norm_w", "post_ffn_norm_w")
ORDER = ("w_in", "q_norm_w", "w_uq", "kv_norm_w", "w_ukv", "conv_w", "conv_b", "dt_bias", "a_log", "d_skip",
         "ssd_norm_w", "attn_out_norm_w", "w_out", "pre_mix_norm_w", "post_mix_norm_w", "pre_ffn_norm_w",
         "post_ffn_norm_w", "w_gate", "w_up", "w_down")


def kernel(x, positions, w_in, q_norm_w, w_uq, kv_norm_w, w_ukv, conv_w, conv_b, dt_bias, a_log, d_skip, ssd_norm_w, attn_out_norm_w, w_out, pre_mix_norm_w, post_mix_norm_w, pre_ffn_norm_w, post_ffn_norm_w, w_gate, w_up, w_down, loss_target, m_w_in, m_q_norm_w, m_w_uq, m_kv_norm_w, m_w_ukv, m_conv_w, m_conv_b, m_dt_bias, m_a_log, m_d_skip, m_ssd_norm_w, m_attn_out_norm_w, m_w_out, m_pre_mix_norm_w, m_post_mix_norm_w, m_pre_ffn_norm_w, m_post_ffn_norm_w, m_w_gate, m_w_up, m_w_down, v_w_in, v_q_norm_w, v_w_uq, v_kv_norm_w, v_w_ukv, v_conv_w, v_conv_b, v_dt_bias, v_a_log, v_d_skip, v_ssd_norm_w, v_attn_out_norm_w, v_w_out, v_pre_mix_norm_w, v_post_mix_norm_w, v_pre_ffn_norm_w, v_post_ffn_norm_w, v_w_gate, v_w_up, v_w_down):
    local = dict(locals())
    wts = {n: local[n][0] for n in ORDER}
    mom_m = {n: local["m_" + n][0] for n in ORDER}
    mom_v = {n: local["v_" + n][0] for n in ORDER}
    xs = x[0]
    tgt = loss_target[0]
    t, d = xs.shape
    nchip = 4
    my_x, my_y, my_c = lax.axis_index("x"), lax.axis_index("y"), lax.axis_index("c")
    my_chip = 2 * my_x + my_y

    mla_w = d // 2
    nh = mla_w // V_HEAD
    width = d - mla_w
    hs = width // SSD_P
    gn = SSD_G * SSD_N
    cdim = width + 2 * gn
    in_sizes = (Q_RANK, KV_RANK, ROPE, width, cdim, hs)
    d_in = sum(in_sizes)
    tail = LANES
    off_xbc = 0
    off_cq = cdim
    off_ckv = off_cq + Q_RANK
    off_z = off_ckv + KV_RANK
    off_tail = off_z + width
    d_in_p = _round_up(off_tail + tail, 256)
    gw = width // SSD_G
    assert off_cq % Q_RANK == 0 and off_ckv % KV_RANK == 0 and off_z % gw == 0 and off_tail % LANES == 0
    qk_head = NOPE + ROPE
    scale = qk_head ** -0.5
    tm = _tile(t, ROW_TILE, 8)
    tmw = _tile(t, ROW_TILE // 2, 8)

    sp = jnp.stack([my_chip, my_c]).astype(jnp.int32)

    def pair_sums(names, gl, from_sib):
        res = []
        for n, g, fs in zip(names, gl, from_sib):
            _, r2, cs = fs.shape
            tr = _row_tile(r2, cs, 16)
            nb = r2 // tr
            res.append(_blocked(
                lambda a, b: a + b, (nchip, nb),
                [(g, (None, tr, cs), lambda k, i, s, nb=nb: (k, s[1] * nb + i, 0)),
                 (fs, (None, tr, cs), lambda k, i, s: (k, i, 0))],
                [((nchip, r2, cs), bf16, (None, tr, cs), lambda k, i, s: (k, i, 0))], [], "pair_sum_" + n, sp=sp)[0])
        return res

    def chip_sums(names, gl, from_sib, from_chips):
        res = []
        for n, g, fs, fc in zip(names, gl, from_sib, from_chips):
            _, r2, cs = fs.shape
            tr = _row_tile(r2, cs, 16)
            nb = r2 // tr
            res.append(_blocked(
                lambda a, b, r1, r2_, r3: ((a + b) + r1.astype(f32)) + r2_.astype(f32) + r3.astype(f32), (nb,),
                [(g, (None, tr, cs), lambda i, s, nb=nb: (s[0], s[1] * nb + i, 0)),
                 (fs, (None, tr, cs), lambda i, s: (s[0], i, 0)),
                 (fc, (None, tr, cs), lambda i, s: (s[0] ^ 1, i, 0)),
                 (fc, (None, tr, cs), lambda i, s: (s[0] ^ 2, i, 0)),
                 (fc, (None, tr, cs), lambda i, s: (s[0] ^ 3, i, 0))],
                [((r2, cs), f32, (tr, cs), lambda i, s: (i, 0))], [], "chip_sum_" + n, sp=sp)[0])
        return res

    ck, ccs = wts["conv_w"].shape
    cs_in = wts["w_in"].shape[1]
    flat_t = lambda a3: jnp.swapaxes(a3, 1, 2).reshape(1, cs_in * d // LANES, LANES)
    w_in_g, conv_g = _comm_call(_gather_weights([flat_t(w_in)[0].astype(bf16)], [wts["conv_w"]]), "gather_w_in")
    cat_cols = lambda g: jnp.concatenate([g[k] for k in range(nchip)], axis=1)
    conv_full = cat_cols(conv_g)
    mixer_gather = _gather_weights([wts[n].astype(bf16) for n in MIXER[1:]], [])
    ffn_gather = _gather_weights([wts[n].astype(bf16) for n in FFN[:2]], [])
    w_down_b = wts["w_down"].astype(bf16)
    dn2 = w_down_b.shape[0] // 2
    down_gathers = [_gather_weights([w_down_b[:dn2]], []), _gather_weights([w_down_b[dn2:]], [])]

    wi = w_in_g.reshape(nchip * cs_in, d)
    o = np.cumsum((0,) + in_sizes)
    seg = lambda i: wi[o[i]:o[i + 1]]
    w_in_pt = jnp.concatenate([seg(4), seg(0), seg(1), seg(3), seg(2), seg(5),
                               jnp.zeros((d_in_p - off_tail - ROPE - hs, d), bf16)], axis=0)

    inv_freq = ROPE_THETA ** (-jnp.arange(0, ROPE, 2, dtype=f32) / ROPE)
    ang = positions[0].astype(f32)[:, None] * inv_freq
    cos, sin = jnp.cos(ang), jnp.sin(ang)
    z32, z64, z96 = jnp.zeros((t, 32), f32), jnp.zeros((t, 64), f32), jnp.zeros((t, 96), f32)
    cosp = jnp.concatenate([cos, cos, z64], axis=1)
    sina = jnp.concatenate([-sin, z96], axis=1)
    sinb = jnp.concatenate([z32, sin, z64], axis=1)

    row = lambda a: a.reshape(1, -1)
    w_pre_mix, w_post_mix = row(wts["pre_mix_norm_w"]), row(wts["post_mix_norm_w"])
    w_pre_ffn, w_post_ffn = row(wts["pre_ffn_norm_w"]), row(wts["post_ffn_norm_w"])
    w_qn, w_kvn = row(wts["q_norm_w"]), row(wts["kv_norm_w"])
    w_attn_n, w_ssd_n = row(wts["attn_out_norm_w"]), row(wts["ssd_norm_w"])
    conv_b_r = row(wts["conv_b"])
    dtb, alog, dskip = row(wts["dt_bias"]), row(wts["a_log"]), row(wts["d_skip"])
    dtb_t, alog_t = dtb.reshape(hs, 1), alog.reshape(hs, 1)

    u = _rowwise(lambda a, w: _rms(a, w), t, tm, [_rows(xs, tm), _par(w_pre_mix)], [_rout(t, d, bf16, tm)], [],
                 "pre_mix_norm")[0]
    proj, (w_uq_g, w_ukv_f, w_out_g) = _matmul([(u, w_in_pt)], "nt", f32, "in_proj", comm=mixer_gather,
                                               at=(0.0, 0.6))
    w_uq_p = jnp.pad(cat_cols(w_uq_g).reshape(Q_RANK, nh, qk_head), ((0, 0), (0, 0), (0, QK_PAD - qk_head))
                     ).reshape(Q_RANK, nh * QK_PAD)
    w_out_f = w_out_g.reshape(-1, w_out_g.shape[2])
    cq_in = _rows(proj, tm, Q_RANK, off_cq // Q_RANK)
    ckv_in = _rows(proj, tm, KV_RANK, off_ckv // KV_RANK)
    cqn = _rowwise(lambda a, w: _rms(a, w), t, tm, [cq_in, _par(w_qn)], [_rout(t, Q_RANK, bf16, tm)], [], "q_norm")[0]
    ckvn = _rowwise(lambda a, w: _rms(a, w), t, tm, [ckv_in, _par(w_kvn)], [_rout(t, KV_RANK, bf16, tm)], [],
                    "kv_norm")[0]
    q_raw = _matmul([(cqn, w_uq_p)], "nn", f32, "q_up")
    kv = _matmul([(ckvn, w_ukv_f)], "nn", bf16, "kv_up")

    def q_rope_fn(qt, cp, sa, sb):
        parts = []
        for h in range(nh):
            parts.append(qt[:, h * QK_PAD: h * QK_PAD + NOPE])
            parts.append(_rope(qt[:, h * QK_PAD + NOPE:(h + 1) * QK_PAD], cp, sa, sb))
        return jnp.concatenate(parts, axis=1)

    tail_cb = off_tail // LANES
    q2 = _rowwise(q_rope_fn, t, tm, [_rows(q_raw, tm), _rows(cosp, tm), _rows(sina, tm), _rows(sinb, tm)],
                  [_rout(t, nh * QK_PAD, bf16, tm)], [], "q_rope")[0]
    kr2 = _rowwise(_rope, t, tm, [_rows(proj, tm, LANES, tail_cb), _rows(cosp, tm), _rows(sina, tm), _rows(sinb, tm)],
                   [_rout(t, LANES, bf16, tm)], [], "k_rope")[0]
    o_att, lse, ffn_w = _flash_fwd(q2, kv, kr2, nh, scale, ffn_gather, (0.0, 0.75))
    w_gate_f, w_up_f = ffn_w

    xbc_act = _conv_fwd(proj, conv_full, conv_b_r, cdim, tm)
    dt_raw, dt_raw_t = _blocked(
        lambda blk: (blk[:, ROPE:ROPE + hs], blk.T[ROPE:ROPE + hs, :]), (t // tm,),
        [_rows(proj, tm, LANES, tail_cb)],
        [((t, hs), f32, (tm, hs), lambda i: (i, 0)), ((hs, t), f32, (hs, tm), lambda i: (0, i))], [], "dt_split")
    y_ssd, hprev, (down_a,) = _ssd_fwd(xbc_act, dt_raw, dt_raw_t, dtb, dtb_t, alog, alog_t, dskip, width,
                                       down_gathers[0], (0.0, 0.6))
    z_ins = [_rows(proj, tm, gw, off_z // gw + i) for i in range(SSD_G)]

    def mix_norms_fn(ov, yv, *rest):
        zs, wa, ws = rest[:SSD_G], rest[SSD_G], rest[SSD_G + 1]
        outs = [_rms(ov, wa)]
        for i in range(SSD_G):
            sl = slice(i * gw, (i + 1) * gw)
            outs.append(_rms(yv[:, sl] * (zs[i] * _sigmoid(zs[i])), ws[:, sl]))
        return jnp.concatenate(outs, axis=1)

    cat = _rowwise(mix_norms_fn, t, tm, [_rows(o_att, tm), _rows(y_ssd, tm)] + z_ins + [_par(w_attn_n), _par(w_ssd_n)],
                   [_rout(t, d, bf16, tm)], [], "attn_ssd_out_norms")[0]
    mix, (down_b,) = _matmul([(cat, w_out_f)], "nn", f32, "out_proj", comm=down_gathers[1], at=(0.0, 0.6))
    w_down_f = jnp.concatenate([down_a, down_b], axis=1).reshape(-1, down_a.shape[2])

    def post_mix_fn(mx, xv, w1, w2):
        h1v = xv + _rms(mx, w1)
        return h1v, _rms(h1v, w2)

    h1, v_in = _rowwise(post_mix_fn, t, tmw, [_rows(mix, tmw), _rows(xs, tmw), _par(w_post_mix), _par(w_pre_ffn)],
                        [_rout(t, d, f32, tmw), _rout(t, d, bf16, tmw)], [], "post_mix_pre_ffn_norm")
    g_ff, u_ff, act = _ffn_up(v_in, w_gate_f, w_up_f)
    ffn = _matmul([(act, w_down_f)], "nn", f32, "ffn_down")

    def final_fn(fv, h1v, tg, w):
        h2 = h1v + _rms(fv, w)
        err = h2 - tg
        lpart = 0.5 * jnp.sum(jnp.sum(err * err, axis=1, keepdims=True), axis=0, keepdims=True) / d
        dh2 = err / d
        dff, dw = _rms_bwd(fv, w, dh2)
        return dff, dh2, jnp.broadcast_to(lpart, (1, LANES)), dw

    dffn, dh2, loss_acc, g_post_ffn = _rowwise(
        final_fn, t, tmw, [_rows(ffn, tmw), _rows(h1, tmw), _rows(tgt, tmw), _par(w_post_ffn)],
        [_rout(t, d, bf16, tmw), _rout(t, d, f32, tmw)], [(1, LANES), (1, d)], "loss_post_ffn_norm_bwd")
    loss = lax.psum(loss_acc[0, 0], ("x", "y", "c"))

    dg_ff, du_ff = _ffn_down_bwd(dffn, w_down_f, g_ff, u_ff)
    gw_down = _matmul([(act, dffn)], "tn", f32, "grad_w_down")
    gw_gate = _matmul([(v_in, dg_ff)], "tn", f32, "grad_w_gate", out_chunks=nchip)
    gw_up = _matmul([(v_in, du_ff)], "tn", f32, "grad_w_up", out_chunks=nchip)
    ffn_g = [gw_gate, gw_up, gw_down.reshape(nchip, -1, d)]
    dv_in, ffn_from_sib = _matmul([(dg_ff, w_gate_f), (du_ff, w_up_f)], "nt", f32, "ffn_up_bwd",
                                  comm=_sibling_send_halves(ffn_g), at=(0.0,))
    ffn_pairs = pair_sums(FFN, ffn_g, ffn_from_sib)

    def mid_bwd_fn(h1v, dvv, dh2v, mx, w_pf, w_pm):
        dxn, dw_pf = _rms_bwd(h1v, w_pf, dvv)
        dh1v = dh2v + dxn
        dmx, dw_pm = _rms_bwd(mx, w_pm, dh1v)
        return dh1v, dmx, dw_pf, dw_pm

    dh1, dmix, g_pre_ffn, g_post_mix = _rowwise(
        mid_bwd_fn, t, tmw, [_rows(h1, tmw), _rows(dv_in, tmw), _rows(dh2, tmw), _rows(mix, tmw),
                             _par(w_pre_ffn), _par(w_post_mix)],
        [_rout(t, d, f32, tmw), _rout(t, d, bf16, tmw)], [(1, d), (1, d)], "pre_ffn_post_mix_norm_bwd")
    dcat = _matmul([(dmix, w_out_f)], "nt", f32, "out_proj_bwd")
    gw_out = _matmul([(cat, dmix)], "tn", f32, "grad_w_out")

    def mix_norms_bwd_fn(ov, yv, *rest):
        zs, dcv, wa, ws = rest[:SSD_G], rest[SSD_G], rest[SSD_G + 1], rest[SSD_G + 2]
        dov, dwa = _rms_bwd(ov, wa, dcv[:, :mla_w])
        dl = [jnp.broadcast_to(jnp.sum(dov[:, h * V_HEAD:(h + 1) * V_HEAD] * ov[:, h * V_HEAD:(h + 1) * V_HEAD],
                                       axis=1, keepdims=True), (ov.shape[0], V_HEAD)) for h in range(nh)]
        dys, dzs, dws = [], [], []
        for i in range(SSD_G):
            sl = slice(i * gw, (i + 1) * gw)
            zv, yi = zs[i], yv[:, sl]
            sg = _sigmoid(zv)
            sz = zv * sg
            dgi, dwi = _rms_bwd(yi * sz, ws[:, sl], dcv[:, mla_w + i * gw: mla_w + (i + 1) * gw])
            dys.append(dgi * sz)
            dzs.append(dgi * yi * (sg * (1.0 + zv * (1.0 - sg))))
            dws.append(dwi)
        cc = lambda a: jnp.concatenate(a, axis=1)
        return dov, cc(dl), cc(dys), cc(dzs), dwa, cc(dws)

    do_att, delta, dy_ssd, dz, g_attn_n, g_ssd_n = _rowwise(
        mix_norms_bwd_fn, t, tm,
        [_rows(o_att, tm), _rows(y_ssd, tm)] + z_ins + [_rows(dcat, tm), _par(w_attn_n), _par(w_ssd_n)],
        [_rout(t, mla_w, bf16, tm), _rout(t, mla_w, f32, tm), _rout(t, width, f32, tm), _rout(t, width, bf16, tm)],
        [(1, mla_w), (1, width)], "attn_ssd_out_norms_bwd")
    dkv, dkr_h, dq2, gu_from_chips = _flash_bwd(q2, kv, kr2, do_att, lse, delta, nh, scale,
                                                _chips_exchange(ffn_pairs[:2]), (0.0,))

    def q_rope_bwd_fn(dqt, cp, sa, sb):
        parts = []
        for h in range(nh):
            parts.append(dqt[:, h * QK_PAD: h * QK_PAD + NOPE])
            parts.append(_rope_bwd(dqt[:, h * QK_PAD + NOPE:(h + 1) * QK_PAD], cp, sa, sb))
        return jnp.concatenate(parts, axis=1)

    dq_raw = _rowwise(q_rope_bwd_fn, t, tm, [_rows(dq2, tm), _rows(cosp, tm), _rows(sina, tm), _rows(sinb, tm)],
                      [_rout(t, nh * QK_PAD, bf16, tm)], [], "q_rope_bwd")[0]

    def k_rope_bwd_fn(dk, cp, sa, sb):
        tot = dk[:, 0:LANES]
        for h in range(1, nh):
            tot = tot + dk[:, h * LANES:(h + 1) * LANES]
        return _rope_bwd(tot, cp, sa, sb)

    dkr = _rowwise(k_rope_bwd_fn, t, tm, [_rows(dkr_h, tm), _rows(cosp, tm), _rows(sina, tm), _rows(sinb, tm)],
                   [_rout(t, LANES, f32, tm)], [], "k_rope_bwd")[0]
    gw_uq_p = _matmul([(cqn, dq_raw)], "tn", f32, "grad_w_uq")
    gw_ukv = _matmul([(ckvn, dkv)], "tn", f32, "grad_w_ukv", out_chunks=nchip)
    dcqn = _matmul([(dq_raw, w_uq_p)], "nt", f32, "q_up_bwd")
    dckvn = _matmul([(dkv, w_ukv_f)], "nt", f32, "kv_up_bwd")

    def lat_norm_bwd_fn(a, w, dyv):
        return _rms_bwd(a, w, dyv)

    dcq, g_qn = _rowwise(lat_norm_bwd_fn, t, tm, [cq_in, _par(w_qn), _rows(dcqn, tm)],
                         [_rout(t, Q_RANK, bf16, tm)], [(1, Q_RANK)], "q_norm_bwd")
    dckv, g_kvn = _rowwise(lat_norm_bwd_fn, t, tm, [ckv_in, _par(w_kvn), _rows(dckvn, tm)],
                           [_rout(t, KV_RANK, bf16, tm)], [(1, KV_RANK)], "kv_norm_bwd")

    dxbc_act, ddt_raw, g_alog, g_dskip, g_dtb, down_from_chips = _ssd_bwd(
        xbc_act, dt_raw, dt_raw_t, dtb, dtb_t, alog, alog_t, dskip, hprev, dy_ssd, width,
        _chips_exchange(ffn_pairs[2:]), (0.0,))
    ffn_halves = chip_sums(FFN, ffn_g, ffn_from_sib, gu_from_chips + down_from_chips)
    dpre, gcw0, gcw1, gcw2, gcw3, g_conv_b = _conv_bwd_pre(proj, conv_full, conv_b_r, dxbc_act, cdim, tm)
    g_conv_w = jnp.concatenate([gcw0, gcw1, gcw2, gcw3], axis=0)
    dxbc = _conv_bwd_dx(dpre, conv_full, tm)

    def dproj_fn(a, b, c, e, kr_blk, dt_blk):
        parts = [a, b, c, e, (kr_blk + dt_blk).astype(bf16)]
        if d_in_p > off_tail + tail:
            parts.append(jnp.zeros((a.shape[0], d_in_p - off_tail - tail), bf16))
        return jnp.concatenate(parts, axis=1)

    dproj = _rowwise(dproj_fn, t, tm, [_rows(dxbc, tm), _rows(dcq, tm), _rows(dckv, tm), _rows(dz, tm),
                                       _rows(dkr, tm), _rows(ddt_raw, tm)],
                     [_rout(t, d_in_p, bf16, tm)], [], "in_proj_grad_pack")[0]
    split_cols = lambda gf: jnp.stack(jnp.split(gf, nchip, axis=1))
    split_rows = lambda gf: gf.reshape(nchip, gf.shape[0] // nchip, gf.shape[1])
    gw_uq = gw_uq_p.reshape(Q_RANK, nh, QK_PAD)[:, :, :qk_head].reshape(Q_RANK, nh * qk_head)
    rest_g = [split_cols(gw_uq), gw_ukv, split_rows(gw_out)]
    du_in, rest_from_sib = _matmul([(dproj, w_in_pt)], "nn", f32, "in_proj_bwd",
                                   comm=_sibling_send_halves(rest_g), at=(0.0,))
    rest_pairs = pair_sums(MIXER[1:], rest_g, rest_from_sib)
    gw_in_pt, rest_from_chips = _matmul([(dproj, u)], "tn", f32, "grad_w_in",
                                        comm=_chips_exchange(rest_pairs), at=(0.0,))
    rest_halves = chip_sums(MIXER[1:], rest_g, rest_from_sib, rest_from_chips)

    def first_bwd_fn(xv, duv, dh1v, w):
        dxn, dw = _rms_bwd(xv, w, duv)
        return dh1v + dxn, dw

    grad_x, g_pre_mix = _rowwise(first_bwd_fn, t, tmw, [_rows(xs, tmw), _rows(du_in, tmw), _rows(dh1, tmw),
                                                         _par(w_pre_mix)],
                                 [_rout(t, d, f32, tmw)], [(1, d)], "pre_mix_norm_bwd")

    gseg = lambda a, b: gw_in_pt[a:b]
    gw_in_t = jnp.concatenate([gseg(off_cq, off_ckv), gseg(off_ckv, off_z), gseg(off_tail, off_tail + ROPE),
                               gseg(off_z, off_tail), gseg(off_xbc, off_cq),
                               gseg(off_tail + ROPE, off_tail + ROPE + hs)], axis=0)
    in_g = [gw_in_t.reshape(nchip, cs_in * d // LANES, LANES)]
    in_from_sib = _comm_call(_sibling_send_halves(in_g), "sibling_send_halves")
    in_pairs = pair_sums(MIXER[:1], in_g, in_from_sib)
    in_from_chips = _comm_call(_chips_exchange(in_pairs), "chips_exchange")
    mix_halves = chip_sums(MIXER[:1], in_g, in_from_sib, in_from_chips) + rest_halves
    halves = mix_halves + ffn_halves
    sib_halves = _comm_call(_sibling_swap(halves), "sibling_swap")
    gshard = {}

    gsmall = {"q_norm_w": g_qn, "kv_norm_w": g_kvn, "conv_w": g_conv_w, "conv_b": g_conv_b, "dt_bias": g_dtb,
              "a_log": g_alog, "d_skip": g_dskip, "ssd_norm_w": g_ssd_n, "attn_out_norm_w": g_attn_n,
              "pre_mix_norm_w": g_pre_mix, "post_mix_norm_w": g_post_mix, "pre_ffn_norm_w": g_pre_ffn,
              "post_ffn_norm_w": g_post_ffn}
    small_sizes = [int(np.prod(gsmall[n].shape)) for n in SMALL]
    srows = _round_up(-(-sum(small_sizes) // LANES), 8)
    spart = _flat_pad([gsmall[n] for n in SMALL], srows * LANES).reshape(srows, LANES)
    sall = _gather_all(spart)

    def sum8_fn(a):
        tot = a[0]
        for k in range(1, 8):
            tot = tot + a[k]
        return tot

    ssum = _blocked(sum8_fn, (1,), [(sall, sall.shape, lambda i: (0, 0, 0))],
                    [((srows, LANES), f32, (srows, LANES), lambda i: (0, 0))], [], "small_grad_sum")[0].reshape(-1)
    gred = {}
    off = 0
    for n, sz in zip(SMALL, small_sizes):
        gred[n] = ssum[off:off + sz].reshape(gsmall[n].shape)
        off += sz
    gshard["conv_w"] = lax.dynamic_slice_in_dim(gred["conv_w"], my_chip * ccs, ccs, axis=1)
    for n in SMALL:
        if n != "conv_w":
            gshard[n] = gred[n].reshape(wts[n].shape)

    delta, new_m, new_v = {}, {}, {}
    big_out = {}
    for n, mine_h, sib_h in zip(BIG, halves, sib_halves):
        view = flat_t if n == "w_in" else (lambda a3: a3)
        res = _adamw_halves(view(local[n]), view(local["m_" + n]), view(local["v_" + n]), mine_h, sib_h, sp,
                            "adamw_" + n)
        big_out[n] = [jnp.swapaxes(a.reshape(1, cs_in, d), 1, 2) for a in res] if n == "w_in" else res
    pack = lambda src: _flat_pad([src[n] for n in SMALL], srows * LANES).reshape(srows, LANES)
    sd, sm, sv = _adamw_call(pack(wts), pack(gshard), pack(mom_m), pack(mom_v), "adamw_small")
    off = 0
    for n in SMALL:
        sz = int(np.prod(wts[n].shape))
        for dst, src in ((delta, sd), (new_m, sm), (new_v, sv)):
            dst[n] = src.reshape(-1)[off:off + sz].reshape(wts[n].shape)
        off += sz

    small_out = (gshard, delta, new_m, new_v)
    pick = lambda k: [big_out[n][k] if n in big_out else small_out[k][n][None] for n in ORDER]
    return (loss, grad_x[None], *pick(0), *pick(1), *pick(2), *pick(3))
```

```python
import functools

import numpy as np
import jax
import jax.numpy as jnp
from jax import lax
from jax.experimental import pallas as pl
from jax.experimental.pallas import tpu as pltpu

f32, bf16 = jnp.float32, jnp.bfloat16

EPS = 1e-6
V_HEAD = 128
NOPE = 128
ROPE = 64
QK_PAD = 256
Q_RANK = 512
KV_RANK = 512
ROPE_THETA = 10000.0
SSD_P = 64
SSD_G = 2
SSD_N = 128
SSD_K = 4
CHUNK = 128
ADAM_LR, ADAM_B1, ADAM_B2, ADAM_EPS, ADAM_WD, ADAM_STEP = 0.001, 0.9, 0.999, 1e-08, 0.01, 10

VMEM_LIMIT_BYTES = 48 * 1024 * 1024
LANES = 128
ATT_TILE = 1024
ATT_TILE_FWD = 1024
MM_TM, MM_TN, MM_TK = 1408, 1024, 1408
CHUNK_WHOLE_MAX = 1536
ROW_TILE = 256
ROW_BLOCK_BYTES = 2 * 1024 * 1024

NN = (((1,), (0,)), ((), ()))
NT = (((1,), (1,)), ((), ()))
TN = (((0,), (0,)), ((), ()))
MESH = pl.DeviceIdType.MESH
ANY = pl.BlockSpec(memory_space=pl.ANY)


def _tile(dim, cap, align=LANES):
    if dim <= cap:
        return dim
    t = (cap // align) * align
    while t >= align:
        if dim % t == 0:
            return t
        t -= align
    raise ValueError(f"no tile for {dim} under {cap}")


def _row_tile(rows, cols, align):
    best = None
    for tr in range(align, rows + 1, align):
        if rows % tr == 0 and tr * cols * 4 <= ROW_BLOCK_BYTES:
            best = tr
    return best or rows


def _round_up(n, m):
    return -(-n // m) * m


def _params(sem):
    return pltpu.CompilerParams(dimension_semantics=sem, vmem_limit_bytes=VMEM_LIMIT_BYTES)


def _dot(a, b, dims):
    return lax.dot_general(a.astype(bf16), b.astype(bf16), dims, preferred_element_type=f32)


def _call(body, name, out_shape, grid, in_specs, out_specs, scratch_shapes, sem, args, comm=None, at=None,
          prefetch=()):
    npf = len(prefetch)
    if comm is None:
        gs = pltpu.PrefetchScalarGridSpec(num_scalar_prefetch=npf, grid=grid, in_specs=list(in_specs),
                                          out_specs=list(out_specs), scratch_shapes=list(scratch_shapes))
        res = pl.pallas_call(body, name=name, out_shape=list(out_shape), grid_spec=gs,
                             compiler_params=_params(sem))(*prefetch, *args)
        return list(res), []
    n_in, n_out, n_sc = len(args), len(out_shape), len(scratch_shapes)
    na, no = len(comm.args), len(comm.outs)
    steps = int(np.prod(grid))

    def full(*allrefs):
        pf, refs = allrefs[:npf], allrefs[npf:]
        cin = refs[n_in:n_in + na]
        o0 = n_in + na
        cout = refs[o0 + n_out:o0 + n_out + no]
        s0 = o0 + n_out + no
        send_sems, recv_sems = refs[s0 + n_sc], refs[s0 + n_sc + 1]
        lin = pl.program_id(0)
        for dim in range(1, len(grid)):
            lin = lin * grid[dim] + pl.program_id(dim)
        for p in range(comm.nphase - 1):
            @pl.when(lin == int(round(at[p] * (steps - 1))))
            def _(p=p):
                comm.run(p, cin, cout, send_sems, recv_sems)
        body(*pf, *refs[:n_in], *refs[o0:o0 + n_out], *refs[s0:s0 + n_sc])

        @pl.when(lin == steps - 1)
        def _():
            comm.run(comm.nphase - 1, cin, cout, send_sems, recv_sems)

    gs = pltpu.PrefetchScalarGridSpec(
        num_scalar_prefetch=npf, grid=grid, in_specs=list(in_specs) + [ANY] * na,
        out_specs=list(out_specs) + [ANY] * no, scratch_shapes=list(scratch_shapes) + comm.sems())
    res = pl.pallas_call(full, name=name, out_shape=list(out_shape) + comm.outs, grid_spec=gs,
                         compiler_params=_params(("arbitrary",) * len(grid)))(*prefetch, *args, *comm.args)
    return list(res[:n_out]), list(res[n_out:])


def _chunk_tile(cs, cap):
    return cs if cs <= CHUNK_WHOLE_MAX else _tile(cs, cap)


def _matmul(pairs, mode, out_dtype, name, out_chunks=None, comm=None, at=None):
    a0, b0 = pairs[0]
    chunked = b0.ndim == 3
    cs = b0.shape[2] if chunked else None
    bcols = b0.shape[0] * b0.shape[2] if chunked else b0.shape[1]
    brows = b0.shape[1] if chunked else b0.shape[0]
    if mode == "nn":
        (m, k), n = a0.shape, bcols
    elif mode == "nt":
        (m, k), n = a0.shape, brows
    else:
        (k, m), n = a0.shape, bcols
    tm = _tile(m, MM_TM)
    if mode == "nt":
        tn = _tile(n, MM_TN)
        tk = _chunk_tile(cs, MM_TK) if chunked else _tile(k, MM_TK)
    else:
        tk = _tile(k, MM_TK)
        if chunked:
            tn = _chunk_tile(cs, MM_TN)
        elif out_chunks:
            tn = _chunk_tile(n // out_chunks, MM_TN)
        else:
            tn = _tile(n, MM_TN)
    nk = k // tk
    if mode == "nn":
        a_spec = pl.BlockSpec((tm, tk), lambda i, j, kk: (i, kk))
        if chunked:
            q = cs // tn
            b_spec = pl.BlockSpec((None, tk, tn), lambda i, j, kk: (j // q, kk, j % q))
        else:
            b_spec = pl.BlockSpec((tk, tn), lambda i, j, kk: (kk, j))
        dims = NN
    elif mode == "nt":
        a_spec = pl.BlockSpec((tm, tk), lambda i, j, kk: (i, kk))
        if chunked:
            q = cs // tk
            b_spec = pl.BlockSpec((None, tn, tk), lambda i, j, kk: (kk // q, j, kk % q))
        else:
            b_spec = pl.BlockSpec((tn, tk), lambda i, j, kk: (j, kk))
        dims = NT
    else:
        a_spec = pl.BlockSpec((tk, tm), lambda i, j, kk: (kk, i))
        b_spec = pl.BlockSpec((tk, tn), lambda i, j, kk: (kk, j))
        dims = TN
    if out_chunks:
        qo = (n // out_chunks) // tn
        out_shape = jax.ShapeDtypeStruct((out_chunks, m, n // out_chunks), out_dtype)
        o_spec = pl.BlockSpec((None, tm, tn), lambda i, j, kk: (j // qo, i, j % qo))
    else:
        out_shape = jax.ShapeDtypeStruct((m, n), out_dtype)
        o_spec = pl.BlockSpec((tm, tn), lambda i, j, kk: (i, j))
    npair = len(pairs)

    def body(*refs):
        o_ref, acc = refs[2 * npair], refs[2 * npair + 1]
        kk = pl.program_id(2)

        @pl.when(kk == 0)
        def _():
            acc[...] = jnp.zeros_like(acc)

        part = _dot(refs[0][...], refs[1][...], dims)
        for p in range(1, npair):
            part = part + _dot(refs[2 * p][...], refs[2 * p + 1][...], dims)
        acc[...] += part

        @pl.when(kk == nk - 1)
        def _():
            o_ref[...] = acc[...].astype(out_dtype)

    args = [t for pr in pairs for t in pr]
    res, cres = _call(body, name, [out_shape], (m // tm, n // tn, nk), [a_spec, b_spec] * npair, [o_spec],
                      [pltpu.VMEM((tm, tn), f32)], ("parallel", "parallel", "arbitrary"), args, comm, at)
    return res[0] if comm is None else (res[0], cres)


def _sigmoid(x):
    return 1.0 / (1.0 + jnp.exp(-x))


def _ffn_up(v, wg, wu):
    m, k = v.shape
    nchunk, _, cs = wg.shape
    n = nchunk * cs
    tm, tn = _tile(m, 512), _chunk_tile(cs, 512)
    q = cs // tn
    w_spec = pl.BlockSpec((None, k, tn), lambda j, i: (j // q, 0, j % q))

    def body(v_ref, wg_ref, wu_ref, g_ref, u_ref, act_ref):
        vb = v_ref[...]
        g = _dot(vb, wg_ref[...], NN)
        u = _dot(vb, wu_ref[...], NN)
        sg = _sigmoid(g)
        silu = g * sg
        g_ref[...] = silu.astype(bf16)
        u_ref[...] = (u * (sg * (1.0 + g * (1.0 - sg)))).astype(bf16)
        act_ref[...] = (silu * u).astype(bf16)

    out = jax.ShapeDtypeStruct((m, n), bf16)
    o_spec = pl.BlockSpec((tm, tn), lambda j, i: (i, j))
    return pl.pallas_call(
        body, name="ffn_up", out_shape=(out, out, out), grid=(n // tn, m // tm),
        in_specs=[pl.BlockSpec((tm, k), lambda j, i: (i, 0)), w_spec, w_spec],
        out_specs=(o_spec, o_spec, o_spec),
        compiler_params=_params(("parallel", "parallel")),
    )(v, wg, wu)


def _ffn_down_bwd(dffn, wd, dact_du, dact_dg):
    m, k = dffn.shape
    n = wd.shape[0]
    tm, tn = _tile(m, 1024), _tile(n, 512)

    def body(d_ref, w_ref, fu_ref, fg_ref, dg_ref, du_ref):
        dact = _dot(d_ref[...], w_ref[...], NT)
        du_ref[...] = (dact * fu_ref[...].astype(f32)).astype(bf16)
        dg_ref[...] = (dact * fg_ref[...].astype(f32)).astype(bf16)

    out = jax.ShapeDtypeStruct((m, n), bf16)
    o_spec = pl.BlockSpec((tm, tn), lambda i, j: (i, j))
    return pl.pallas_call(
        body, name="ffn_down_bwd", out_shape=(out, out), grid=(m // tm, n // tn),
        in_specs=[pl.BlockSpec((tm, k), lambda i, j: (i, 0)), pl.BlockSpec((tn, k), lambda i, j: (j, 0)),
                  o_spec, o_spec],
        out_specs=(o_spec, o_spec),
        compiler_params=_params(("parallel", "parallel")),
    )(dffn, wd, dact_du, dact_dg)


def _blocked(fn, grid, ins, outs, accs, name, sp=None, comm=None, at=None):
    n_in, n_out, n_acc = len(ins), len(outs), len(accs)
    nsp = 0 if sp is None else 1

    def body(*refs):
        refs = refs[nsp:]
        tiles = [r[...] for r in refs[:n_in]]
        res = fn(*tiles)
        if not isinstance(res, (tuple, list)):
            res = (res,)
        for r, val in zip(refs[n_in:n_in + n_out], res[:n_out]):
            r[...] = val.astype(r.dtype)
        if n_acc:
            first = pl.program_id(0) == 0
            for d in range(1, len(grid)):
                first = jnp.logical_and(first, pl.program_id(d) == 0)

            @pl.when(first)
            def _():
                for r in refs[n_in + n_out:]:
                    r[...] = jnp.zeros_like(r)

            for r, val in zip(refs[n_in + n_out:], res[n_out:]):
                r[...] += val

    def acc_map(shape):
        zeros = (0,) * len(shape)
        return lambda *idx: zeros

    in_specs = [pl.BlockSpec(bs, im) for _, bs, im in ins]
    out_specs = [pl.BlockSpec(bs, im) for _, _, bs, im in outs] + [pl.BlockSpec(s, acc_map(s)) for s in accs]
    out_shape = [jax.ShapeDtypeStruct(s, d) for s, d, _, _ in outs] + [jax.ShapeDtypeStruct(s, f32) for s in accs]
    sem = ("arbitrary",) * len(grid) if n_acc else ("parallel",) * len(grid)
    args = [a for a, _, _ in ins]
    res, cres = _call(body, name, out_shape, grid, in_specs, out_specs, [], sem, args, comm, at,
                      prefetch=() if sp is None else (sp,))
    return res if comm is None else (res, cres)


def _rows(a, tm, cols=None, cb=0):
    w = a.shape[1] if cols is None else cols
    return (a, (tm, w), lambda i: (i, cb))


def _par(a):
    zeros = (0,) * a.ndim
    return (a, a.shape, lambda i: zeros)


def _rout(t, w, dtype, tm):
    return ((t, w), dtype, (tm, w), lambda i: (i, 0))


def _rowwise(fn, t, tm, ins, outs, accs, name, comm=None, at=None):
    return _blocked(fn, (t // tm,), ins, outs, accs, name, comm=comm, at=at)


def _rms(x, w):
    r = lax.rsqrt(jnp.mean(x * x, axis=-1, keepdims=True) + EPS)
    return x * r * w


def _rms_bwd(x, w, dy):
    r = lax.rsqrt(jnp.mean(x * x, axis=-1, keepdims=True) + EPS)
    xh = x * r
    dyw = dy * w
    dx = r * (dyw - xh * jnp.mean(dyw * xh, axis=-1, keepdims=True))
    return dx, jnp.sum(dy * xh, axis=0, keepdims=True)


def _silu_grad(x):
    s = _sigmoid(x)
    return s * (1.0 + x * (1.0 - s))


def _rope(blk, cosp, sina, sinb):
    return blk * cosp + pltpu.roll(blk, 96, 1) * sina + pltpu.roll(blk, 32, 1) * sinb


def _rope_bwd(dy, cosp, sina, sinb):
    return dy * cosp + pltpu.roll(dy * sina, 32, 1) + pltpu.roll(dy * sinb, 96, 1)


HALO = 8


def _conv_taps(buf, w, tm, base):
    acc = buf[base:base + tm, :] * w[0:1]
    for k in range(1, SSD_K):
        acc = acc + buf[base + k:base + k + tm, :] * w[k:k + 1]
    return acc


def _conv_specs(t, tm, cdim):
    cur = pl.BlockSpec((tm, cdim), lambda i: (i, 0))
    prev = pl.BlockSpec((HALO, cdim), lambda i: (jnp.maximum(i * (tm // HALO) - 1, 0), 0))
    nxt = pl.BlockSpec((HALO, cdim), lambda i: (jnp.minimum((i + 1) * (tm // HALO), t // HALO - 1), 0))
    return cur, prev, nxt


def _conv_fwd(src, w, b, cdim, tm):
    t = src.shape[0]
    cur, prev, _ = _conv_specs(t, tm, cdim)

    def body(x_ref, p_ref, w_ref, b_ref, o_ref, buf):
        buf[0:HALO, :] = jnp.where(pl.program_id(0) > 0, p_ref[...], 0.0)
        buf[HALO:HALO + tm, :] = x_ref[...]
        pre = _conv_taps(buf, w_ref[...], tm, HALO - (SSD_K - 1)) + b_ref[...]
        o_ref[...] = pre * _sigmoid(pre)

    par = lambda a: pl.BlockSpec(a.shape, lambda i: (0, 0))
    return pl.pallas_call(
        body, name="conv_silu", out_shape=jax.ShapeDtypeStruct((t, cdim), f32), grid=(t // tm,),
        in_specs=[cur, prev, par(w), par(b)], out_specs=cur,
        scratch_shapes=[pltpu.VMEM((tm + HALO, cdim), f32)], compiler_params=_params(("parallel",)),
    )(src, src, w, b)


def _conv_bwd_pre(src, w, b, dact, cdim, tm):
    t = src.shape[0]
    cur, prev, _ = _conv_specs(t, tm, cdim)

    def body(x_ref, p_ref, w_ref, b_ref, d_ref, dpre_ref, dw0, dw1, dw2, dw3, db, buf):
        @pl.when(pl.program_id(0) == 0)
        def _():
            for r in (dw0, dw1, dw2, dw3, db):
                r[...] = jnp.zeros_like(r)

        buf[0:HALO, :] = jnp.where(pl.program_id(0) > 0, p_ref[...], 0.0)
        buf[HALO:HALO + tm, :] = x_ref[...]
        base = HALO - (SSD_K - 1)
        pre = _conv_taps(buf, w_ref[...], tm, base) + b_ref[...]
        dpre = d_ref[...] * _silu_grad(pre)
        dpre_ref[...] = dpre
        for k, r in enumerate((dw0, dw1, dw2, dw3)):
            r[...] += jnp.sum(dpre * buf[base + k:base + k + tm, :], axis=0, keepdims=True)
        db[...] += jnp.sum(dpre, axis=0, keepdims=True)

    par = lambda a: pl.BlockSpec(a.shape, lambda i: (0, 0))
    acc = pl.BlockSpec((1, cdim), lambda i: (0, 0))
    acc_shape = jax.ShapeDtypeStruct((1, cdim), f32)
    return pl.pallas_call(
        body, name="conv_silu_bwd", out_shape=[jax.ShapeDtypeStruct((t, cdim), f32)] + [acc_shape] * 5,
        grid=(t // tm,), in_specs=[cur, prev, par(w), par(b), cur], out_specs=[cur] + [acc] * 5,
        scratch_shapes=[pltpu.VMEM((tm + HALO, cdim), f32)], compiler_params=_params(("arbitrary",)),
    )(src, src, w, b, dact)


def _conv_bwd_dx(dpre, w, tm):
    t, cdim = dpre.shape
    cur, _, nxt = _conv_specs(t, tm, cdim)
    last = t // tm - 1

    def body(d_ref, n_ref, w_ref, o_ref, buf):
        buf[0:tm, :] = d_ref[...]
        buf[tm:tm + HALO, :] = jnp.where(pl.program_id(0) < last, n_ref[...], 0.0)
        wv = w_ref[...]
        acc = buf[0:tm, :] * wv[SSD_K - 1:SSD_K]
        for k in range(SSD_K - 1):
            s = SSD_K - 1 - k
            acc = acc + buf[s:s + tm, :] * wv[k:k + 1]
        o_ref[...] = acc.astype(bf16)

    return pl.pallas_call(
        body, name="conv_bwd_dx", out_shape=jax.ShapeDtypeStruct((t, cdim), bf16), grid=(t // tm,),
        in_specs=[cur, nxt, pl.BlockSpec(w.shape, lambda i: (0, 0))], out_specs=cur,
        scratch_shapes=[pltpu.VMEM((tm + HALO, cdim), f32)], compiler_params=_params(("parallel",)),
    )(dpre, dpre, w)


def _causal_mask(s, row0=0):
    row = lax.broadcasted_iota(jnp.int32, s.shape, 0) + row0
    col = lax.broadcasted_iota(jnp.int32, s.shape, 1)
    return jnp.where(row >= col, s, -jnp.inf)


def _causal_pairs(nq, q_major):
    pairs = [(qi, ki) for qi in range(nq) for ki in range(qi + 1)]
    if not q_major:
        pairs.sort(key=lambda p: (p[1], p[0]))
    return (jnp.asarray([p[0] for p in pairs], jnp.int32), jnp.asarray([p[1] for p in pairs], jnp.int32))


def _flash_fwd(q, kv, kr, nheads, scale, comm=None, at=None):
    t = q.shape[0]
    tq = _tile(t, ATT_TILE_FWD)
    nq = t // tq

    qtab, ktab = _causal_pairs(nq, q_major=True)
    hp = 2 if nheads % 2 == 0 else 1

    def body(qt, kt, q_ref, kv_ref, kr_ref, o_ref, lse_ref, m_sc, l_sc, acc_sc):
        qi, ki = qt[pl.program_id(1)], kt[pl.program_id(1)]

        @pl.when(ki == 0)
        def _():
            m_sc[...] = jnp.full_like(m_sc, -jnp.inf)
            l_sc[...] = jnp.zeros_like(l_sc)
            acc_sc[...] = jnp.zeros_like(acc_sc)

        def step(masked, last):
            krb = kr_ref[...]
            ss = []
            for j in range(hp):
                k = jnp.concatenate([kv_ref[:, j * QK_PAD: j * QK_PAD + NOPE], krb], axis=1)
                ss.append(lax.dot_general(q_ref[:, j * QK_PAD:(j + 1) * QK_PAD], k, NT, preferred_element_type=f32))
            soft = []
            for j in range(hp):
                s = ss[j] * scale
                if masked:
                    s = _causal_mask(s)
                m_old = m_sc[j]
                m_new = jnp.maximum(m_old, jnp.max(s, axis=1, keepdims=True))
                alpha = jnp.exp(m_old - m_new)
                p = jnp.exp(s - m_new)
                soft.append((m_new, alpha, alpha * l_sc[j] + jnp.sum(p, axis=1, keepdims=True), p.astype(bf16)))
            for j in range(hp):
                m_new, alpha, l, p = soft[j]
                v = kv_ref[:, j * QK_PAD + NOPE:(j + 1) * QK_PAD]
                acc = alpha * acc_sc[j] + lax.dot_general(p, v, NN, preferred_element_type=f32)
                if last:
                    o_ref[:, j * V_HEAD:(j + 1) * V_HEAD] = acc / l
                    lse_ref[:, j * V_HEAD:(j + 1) * V_HEAD] = jnp.broadcast_to(m_new + jnp.log(l), (tq, V_HEAD))
                else:
                    l_sc[j] = l
                    acc_sc[j] = acc
                    m_sc[j] = m_new

        @pl.when(ki < qi)
        def _():
            step(False, False)

        @pl.when(ki == qi)
        def _():
            step(True, True)

    o_spec = pl.BlockSpec((tq, hp * V_HEAD), lambda h, s, qt, kt: (qt[s], h))
    out = jax.ShapeDtypeStruct((t, nheads * V_HEAD), f32)
    (o, lse), cres = _call(
        body, "flash_fwd", [out, out], (nheads // hp, len(qtab)),
        [pl.BlockSpec((tq, hp * QK_PAD), lambda h, s, qt, kt: (qt[s], h)),
         pl.BlockSpec((tq, hp * QK_PAD), lambda h, s, qt, kt: (kt[s], h)),
         pl.BlockSpec((tq, LANES), lambda h, s, qt, kt: (kt[s], 0))],
        [o_spec, o_spec],
        [pltpu.VMEM((hp, tq, 1), f32), pltpu.VMEM((hp, tq, 1), f32), pltpu.VMEM((hp, tq, V_HEAD), f32)],
        ("parallel", "arbitrary"), [q, kv, kr], comm, at, prefetch=(qtab, ktab))
    return o, lse, cres


def _flash_bwd(q, kv, kr, do, lse, delta, nheads, scale, comm=None, at=None):
    t = q.shape[0]
    tq = _tile(t, ATT_TILE)
    nq = t // tq
    qtab, ktab = _causal_pairs(nq, q_major=False)
    nsub = 2 if tq % 32 == 0 else 1

    def body(qt, kt, q_ref, kn_ref, kr_ref, v_ref, do_ref, lse_ref, dl_ref, dkv_ref, dkr_ref, dq_ref, dk_sc, dv_sc):
        qi, ki = qt[pl.program_id(1)], kt[pl.program_id(1)]

        @pl.when(pl.program_id(1) == 0)
        def _():
            dq_ref[...] = jnp.zeros_like(dq_ref)

        @pl.when(qi == ki)
        def _():
            dk_sc[...] = jnp.zeros_like(dk_sc)
            dv_sc[...] = jnp.zeros_like(dv_sc)

        def step(masked):
            k = jnp.concatenate([kn_ref[...], kr_ref[...]], axis=1)
            vb = v_ref[...]
            parts = [slice(i * (tq // nsub), (i + 1) * (tq // nsub)) for i in range(nsub)]
            qs = [q_ref[r, :] for r in parts]
            dos = [do_ref[r, :] for r in parts]
            ss = [lax.dot_general(qb, k, NT, preferred_element_type=f32) for qb in qs]
            dps = [lax.dot_general(dob, vb, NT, preferred_element_type=f32) for dob in dos]
            ps, dss = [], []
            for r, s, dp in zip(parts, ss, dps):
                s = s * scale
                if masked:
                    s = _causal_mask(s, r.start)
                p = jnp.exp(s - lse_ref[r, 0:1])
                ps.append(p.astype(bf16))
                dss.append((p * (dp - dl_ref[r, 0:1]) * scale).astype(bf16))
            for r, qb, dob, p, ds in zip(parts, qs, dos, ps, dss):
                dv_sc[...] += lax.dot_general(p, dob, TN, preferred_element_type=f32)
                dk_sc[...] += lax.dot_general(ds, qb, TN, preferred_element_type=f32)
                rows = pl.ds(pl.multiple_of(qi * tq + r.start, tq // nsub), tq // nsub)
                dq_ref[rows, :] += lax.dot_general(ds, k, NN, preferred_element_type=f32)

        @pl.when(qi > ki)
        def _():
            step(False)

        @pl.when(qi == ki)
        def _():
            step(True)

        @pl.when(qi == nq - 1)
        def _():
            dk = dk_sc[...]
            dkv_ref[...] = jnp.concatenate([dk[:, :NOPE], dv_sc[...]], axis=1).astype(bf16)
            dkr_ref[...] = dk[:, NOPE:]

    hspec = pl.BlockSpec((tq, V_HEAD), lambda h, s, qt, kt: (qt[s], h))
    (dkv, dkr, dq), cres = _call(
        body, "flash_bwd",
        [jax.ShapeDtypeStruct((t, nheads * QK_PAD), bf16), jax.ShapeDtypeStruct((t, nheads * LANES), f32),
         jax.ShapeDtypeStruct((t, nheads * QK_PAD), f32)],
        (nheads, len(qtab)),
        [pl.BlockSpec((tq, QK_PAD), lambda h, s, qt, kt: (qt[s], h)),
         pl.BlockSpec((tq, NOPE), lambda h, s, qt, kt: (kt[s], 2 * h)),
         pl.BlockSpec((tq, LANES), lambda h, s, qt, kt: (kt[s], 0)),
         pl.BlockSpec((tq, V_HEAD), lambda h, s, qt, kt: (kt[s], 2 * h + 1)),
         hspec, hspec, hspec],
        [pl.BlockSpec((tq, QK_PAD), lambda h, s, qt, kt: (kt[s], h)),
         pl.BlockSpec((tq, LANES), lambda h, s, qt, kt: (kt[s], h)),
         pl.BlockSpec((t, QK_PAD), lambda h, s, qt, kt: (0, h))],
        [pltpu.VMEM((tq, QK_PAD), f32), pltpu.VMEM((tq, V_HEAD), f32)],
        ("parallel", "arbitrary"), [q, kv, kr, kv, do, lse, delta], comm, at, prefetch=(qtab, ktab))
    return dkv, dkr, dq, cres


def _split3(a):
    hi = a.astype(bf16)
    r1 = a - hi.astype(f32)
    mid = r1.astype(bf16)
    lo = (r1 - mid.astype(f32)).astype(bf16)
    return hi, mid, lo


def _split2(a):
    hi = a.astype(bf16)
    return hi, (a - hi.astype(f32)).astype(bf16)


def _ones_dot_left(tri, a):
    hi, mid, lo = _split3(a)
    d = lambda v: lax.dot_general(tri, v, NN, preferred_element_type=f32)
    return d(hi) + d(mid) + d(lo)


def _ones_dot_right(a, tri):
    hi, mid, lo = _split3(a)
    d = lambda v: lax.dot_general(v, tri, NN, preferred_element_type=f32)
    return d(hi) + d(mid) + d(lo)


def _softplus(x):
    return jnp.maximum(x, 0.0) + jnp.log(1.0 + jnp.exp(-jnp.abs(x)))


def _head_spread(hs, width):
    shift = SSD_P.bit_length() - 1
    return (lax.broadcasted_iota(jnp.int32, (hs, width), 0)
            == lax.shift_right_logical(lax.broadcasted_iota(jnp.int32, (hs, width), 1), shift)).astype(bf16)


def _ssd_common(dt_ref, dtT_ref, dtb_ref, dtbT_ref, alog_ref, alogT_ref):
    ii = lax.broadcasted_iota(jnp.int32, (CHUNK, CHUNK), 0)
    jj = lax.broadcasted_iota(jnp.int32, (CHUNK, CHUNK), 1)
    tri = ii >= jj
    raw = dt_ref[...] + dtb_ref[...]
    dt = _softplus(raw)
    a_neg = -jnp.exp(alog_ref[...])
    cum = _ones_dot_left(tri.astype(bf16), dt * a_neg)
    dt_t = _softplus(dtT_ref[...] + dtbT_ref[...])
    cum_t = _ones_dot_right(dt_t * (-jnp.exp(alogT_ref[...])), (ii <= jj).astype(bf16))
    return tri, raw, dt, a_neg, cum, cum_t


def _ssd_fwd(xbc, dt, dt_t, dtb, dtb_t, alog, alog_t, dskip, width, comm=None, at=None):
    t, cdim = xbc.shape
    hs = dt.shape[1]
    nc = t // CHUNK
    epg = hs // SSD_G
    gn = SSD_G * SSD_N

    def body(x_ref, dt_ref, dtT_ref, dtb_ref, dtbT_ref, alog_ref, alogT_ref, d_ref, y_ref, hp_ref, h_sc, yd_sc):
        @pl.when(pl.program_id(0) == 0)
        def _():
            h_sc[...] = jnp.zeros_like(h_sc)

        tri, _, dtv, _, cum, cum_t = _ssd_common(dt_ref, dtT_ref, dtb_ref, dtbT_ref, alog_ref, alogT_ref)
        spread = _head_spread(hs, width)
        clast = cum[CHUNK - 1:CHUNK, :]
        dec = jnp.exp(clast)
        wide = _ones_dot_right(jnp.concatenate([dtv, jnp.exp(cum), jnp.exp(clast - cum),
                                                jnp.broadcast_to(d_ref[...], (CHUNK, hs))], axis=0), spread)
        dt_x, ee_x, ff_x, dsk_x = (wide[i * CHUNK:(i + 1) * CHUNK] for i in range(4))
        xs = x_ref[:, :width]
        xdt = xs * dt_x
        xf = xdt * ff_x
        h_all = h_sc[...]
        hp_ref[0] = h_all
        h_all = h_all.reshape(hs * SSD_P, SSD_N)
        ch_parts = []
        for g in range(SSD_G):
            gsl = slice(g * epg * SSD_P, (g + 1) * epg * SSD_P)
            bb = x_ref[:, width + g * SSD_N: width + (g + 1) * SSD_N].astype(bf16)
            cb_ = x_ref[:, width + gn + g * SSD_N: width + gn + (g + 1) * SSD_N].astype(bf16)
            cbm = lax.dot_general(cb_, bb, NT, preferred_element_type=f32)
            ch_parts.append(_dot(cb_, h_all[gsl], NT))
            st = _dot(xf[:, gsl], bb, TN)
            for e in range(g * epg, (g + 1) * epg):
                esl = slice(e * SSD_P, (e + 1) * SSD_P)
                lmat = jnp.exp(jnp.where(tri, cum[:, e:e + 1] - cum_t[e:e + 1, :], -jnp.inf))
                yd_sc[:, esl] = _dot(cbm * lmat, xdt[:, esl], NN)
                j = e - g * epg
                h_sc[e] = h_sc[e] * dec[:, e:e + 1] + st[j * SSD_P:(j + 1) * SSD_P, :]
        y_ref[...] = yd_sc[...] + jnp.concatenate(ch_parts, axis=1) * ee_x + xs * dsk_x

    par = lambda a: pl.BlockSpec(a.shape, lambda i: (0,) * a.ndim)
    (y, hp), cres = _call(
        body, "ssd_fwd",
        [jax.ShapeDtypeStruct((t, width), f32), jax.ShapeDtypeStruct((nc, hs, SSD_P, SSD_N), f32)], (nc,),
        [pl.BlockSpec((CHUNK, cdim), lambda i: (i, 0)), pl.BlockSpec((CHUNK, hs), lambda i: (i, 0)),
         pl.BlockSpec((hs, CHUNK), lambda i: (0, i)), par(dtb), par(dtb_t), par(alog), par(alog_t), par(dskip)],
        [pl.BlockSpec((CHUNK, width), lambda i: (i, 0)), pl.BlockSpec((1, hs, SSD_P, SSD_N), lambda i: (i, 0, 0, 0))],
        [pltpu.VMEM((hs, SSD_P, SSD_N), f32), pltpu.VMEM((CHUNK, width), f32)], ("arbitrary",),
        [xbc, dt, dt_t, dtb, dtb_t, alog, alog_t, dskip], comm, at)
    return y, hp, cres


def _ssd_bwd(xbc, dt, dt_t, dtb, dtb_t, alog, alog_t, dskip, hprev, dy, width, comm=None, at=None):
    t, cdim = xbc.shape
    hs = dt.shape[1]
    nc = t // CHUNK
    epg = hs // SSD_G
    gn = SSD_G * SSD_N

    def body(x_ref, dt_ref, dtT_ref, dtb_ref, dtbT_ref, alog_ref, alogT_ref, d_ref, hp_ref, dy_ref,
             dx_ref, ddt_ref, dalog_ref, ddsk_ref, ddtb_ref, dh_sc, dxd_sc):
        @pl.when(pl.program_id(0) == 0)
        def _():
            dh_sc[...] = jnp.zeros_like(dh_sc)
            dalog_ref[...] = jnp.zeros_like(dalog_ref)
            ddsk_ref[...] = jnp.zeros_like(ddsk_ref)
            ddtb_ref[...] = jnp.zeros_like(ddtb_ref)

        tri, raw, dtv, a_neg, cum, cum_t = _ssd_common(dt_ref, dtT_ref, dtb_ref, dtbT_ref, alog_ref, alogT_ref)
        dsk = d_ref[...]
        head_row = lax.broadcasted_iota(jnp.int32, (1, hs), 1)
        head_col = lax.broadcasted_iota(jnp.int32, (hs, 1), 0)
        last_row = (lax.broadcasted_iota(jnp.int32, (CHUNK, 1), 0) == CHUNK - 1).astype(f32)
        shift = SSD_P.bit_length() - 1
        spread = _head_spread(hs, width)
        gather = (lax.shift_right_logical(lax.broadcasted_iota(jnp.int32, (width, hs), 0), shift)
                  == lax.broadcasted_iota(jnp.int32, (width, hs), 1)).astype(bf16)
        clast = cum[CHUNK - 1:CHUNK, :]
        ee = jnp.exp(cum)
        ff = jnp.exp(clast - cum)
        dec = jnp.exp(clast)
        wide = _ones_dot_right(jnp.concatenate([dtv, ee, ff, jnp.broadcast_to(dsk, (CHUNK, hs))], axis=0), spread)
        dt_x, ee_x, ff_x, dsk_x = (wide[i * CHUNK:(i + 1) * CHUNK] for i in range(4))
        xs = x_ref[:, :width]
        dyv = dy_ref[...]
        xdt = xs * dt_x
        dye = dyv * ee_x
        xf = xdt * ff_x
        h_all = hp_ref[0].reshape(hs * SSD_P, SSD_N)
        dh_all = dh_sc[...].reshape(hs * SSD_P, SSD_N)
        hi, lo = _split2(_ones_dot_left(spread, dh_all * h_all))
        ones8 = jnp.ones((8, SSD_N), bf16)
        hh = (lax.dot_general(ones8, hi, NT, preferred_element_type=f32)
              + lax.dot_general(ones8, lo, NT, preferred_element_type=f32))[0:1]
        rowsum_m = jnp.zeros((CHUNK, hs), f32)
        colsum_m = jnp.zeros((hs, CHUNK), f32)
        ch_parts, bds_parts = [], []
        for g in range(SSD_G):
            bsl = slice(width + g * SSD_N, width + (g + 1) * SSD_N)
            csl = slice(width + gn + g * SSD_N, width + gn + (g + 1) * SSD_N)
            gsl = slice(g * epg * SSD_P, (g + 1) * epg * SSD_P)
            bb = x_ref[:, bsl].astype(bf16)
            cb_ = x_ref[:, csl].astype(bf16)
            cbm = lax.dot_general(cb_, bb, NT, preferred_element_type=f32)
            hg = h_all[gsl].astype(bf16)
            dhg = dh_all[gsl].astype(bf16)
            ch_parts.append(_dot(cb_, hg, NT))
            bds_parts.append(_dot(bb, dhg, NT))
            dcg = _dot(dye[:, gsl], hg, NN)
            dbg = _dot(xf[:, gsl], dhg, NN)
            dh_new = _dot(dye[:, gsl], cb_, TN)
            dcb = jnp.zeros((CHUNK, CHUNK), f32)
            for e in range(g * epg, (g + 1) * epg):
                esl = slice(e * SSD_P, (e + 1) * SSD_P)
                lmat = jnp.exp(jnp.where(tri, cum[:, e:e + 1] - cum_t[e:e + 1, :], -jnp.inf))
                gmat = cbm * lmat
                dy_e = dyv[:, esl].astype(bf16)
                dgm = _dot(dy_e, xdt[:, esl], NT)
                dxd_sc[:, esl] = _dot(gmat, dy_e, TN)
                dcb = dcb + dgm * lmat
                mm = dgm * gmat
                rowsum_m = rowsum_m + jnp.sum(mm, axis=1, keepdims=True) * (head_row == e).astype(f32)
                colsum_m = colsum_m + (head_col == e).astype(f32) * jnp.sum(mm, axis=0, keepdims=True)
                j = e - g * epg
                dh_sc[e] = dh_new[j * SSD_P:(j + 1) * SSD_P, :] + dec[:, e:e + 1] * dh_sc[e]
            dx_ref[:, bsl] = dbg + _dot(dcb, cb_, TN)
            dx_ref[:, csl] = dcg + _dot(dcb, bb, NN)
        ch_all = jnp.concatenate(ch_parts, axis=1)
        bds_all = jnp.concatenate(bds_parts, axis=1)
        dxdt = bds_all * ff_x + dxd_sc[...]
        dx_ref[:, :width] = dxdt * dt_x + dyv * dsk_x
        sums = _ones_dot_right(jnp.concatenate([dxdt * xs, dyv * ch_all, bds_all * xdt, dyv * xs], axis=0), gather)
        ddtx_all = sums[0:CHUNK]
        dff = sums[2 * CHUNK:3 * CHUNK] * ff
        ddsk = jnp.sum(sums[3 * CHUNK:4 * CHUNK], axis=0, keepdims=True)
        dclast = jnp.sum(dff, axis=0, keepdims=True) + dec * hh
        eye = (lax.broadcasted_iota(jnp.int32, (hs, hs), 0) == lax.broadcasted_iota(jnp.int32, (hs, hs), 1)).astype(bf16)
        colsum_t = sum(lax.dot_general(v, eye, TN, preferred_element_type=f32) for v in _split3(colsum_m))
        dcum_all = sums[CHUNK:2 * CHUNK] * ee - dff + rowsum_m - colsum_t + dclast * last_row
        ii = lax.broadcasted_iota(jnp.int32, (CHUNK, CHUNK), 0)
        jj = lax.broadcasted_iota(jnp.int32, (CHUNK, CHUNK), 1)
        da = _ones_dot_left((jj >= ii).astype(bf16), dcum_all)
        ddt = da * a_neg + ddtx_all
        dalog_ref[...] += jnp.sum(da * dtv, axis=0, keepdims=True) * a_neg
        draw = ddt * _sigmoid(raw)
        place = (lax.broadcasted_iota(jnp.int32, (hs, LANES), 0) + ROPE
                 == lax.broadcasted_iota(jnp.int32, (hs, LANES), 1)).astype(bf16)
        ddt_ref[...] = _ones_dot_right(draw, place)
        ddtb_ref[...] += jnp.sum(draw, axis=0, keepdims=True)
        ddsk_ref[...] += ddsk

    rev = lambda i: nc - 1 - i
    par = lambda a: pl.BlockSpec(a.shape, lambda i: (0,) * a.ndim)
    acc = pl.BlockSpec((1, hs), lambda i: (0, 0))
    acc_shape = jax.ShapeDtypeStruct((1, hs), f32)
    res, cres = _call(
        body, "ssd_bwd",
        [jax.ShapeDtypeStruct((t, cdim), f32), jax.ShapeDtypeStruct((t, LANES), f32), acc_shape, acc_shape, acc_shape],
        (nc,),
        [pl.BlockSpec((CHUNK, cdim), lambda i: (rev(i), 0)), pl.BlockSpec((CHUNK, hs), lambda i: (rev(i), 0)),
         pl.BlockSpec((hs, CHUNK), lambda i: (0, rev(i))), par(dtb), par(dtb_t), par(alog), par(alog_t),
         par(dskip), pl.BlockSpec((1, hs, SSD_P, SSD_N), lambda i: (rev(i), 0, 0, 0)),
         pl.BlockSpec((CHUNK, width), lambda i: (rev(i), 0))],
        [pl.BlockSpec((CHUNK, cdim), lambda i: (rev(i), 0)), pl.BlockSpec((CHUNK, LANES), lambda i: (rev(i), 0)),
         acc, acc, acc],
        [pltpu.VMEM((hs, SSD_P, SSD_N), f32), pltpu.VMEM((CHUNK, width), f32)], ("arbitrary",),
        [xbc, dt, dt_t, dtb, dtb_t, alog, alog_t, dskip, hprev, dy], comm, at)
    return (*res, cres)


def _where_am_i():
    x, y, c = lax.axis_index("x"), lax.axis_index("y"), lax.axis_index("c")
    chips = [(1 - x, y), (x, 1 - y), (1 - x, 1 - y)]
    return x, y, c, chips


def _remote(src, dst, send_sems, recv_sems, k, to):
    return pltpu.make_async_remote_copy(src_ref=src, dst_ref=dst, send_sem=send_sems.at[k], recv_sem=recv_sems.at[k],
                                        device_id=to, device_id_type=MESH)


class _Comm:
    def __init__(self, args, outs, nsem, nphase, run):
        self.args, self.outs, self.nsem, self.nphase, self.run = list(args), list(outs), nsem, nphase, run

    def sems(self):
        return [pltpu.SemaphoreType.DMA((self.nsem,)), pltpu.SemaphoreType.DMA((self.nsem,))]


def _comm_call(comm, name):
    na, no = len(comm.args), len(comm.outs)

    def body(*refs):
        for phase in range(comm.nphase):
            comm.run(phase, refs[:na], refs[na:na + no], refs[na + no], refs[na + no + 1])

    return list(pl.pallas_call(body, name=name, out_shape=comm.outs, in_specs=[ANY] * na, out_specs=[ANY] * no,
                               scratch_shapes=comm.sems())(*comm.args))


def _half(ref, c, r2):
    return ref.at[pl.ds(c * r2, r2)]


def _gather_weights(shards, wholes):
    ns, nw = len(shards), len(wholes)
    per = 7

    def run(phase, srcs, outs, send_sems, recv_sems):
        x, y, c, chips = _where_am_i()
        me, sib = 2 * x + y, (x, y, 1 - c)
        def first():
            cps = []
            for w in range(ns + nw):
                src, out, base = srcs[w], outs[w], per * w
                halved = w < ns
                r2 = src.shape[0] // 2
                piece = _half(src, c, r2) if halved else src
                for j, (cx, cy) in enumerate(chips):
                    dst = _half(out.at[me], c, r2) if halved else out.at[me]
                    cps.append(_remote(piece, dst, send_sems, recv_sems, base + j, (cx, cy, c)))
                cps.append(_remote(src, out.at[me], send_sems, recv_sems, base + 6, sib))
            return cps

        def passed():
            cps = []
            for w in range(ns):
                r2 = srcs[w].shape[0] // 2
                for j, (cx, cy) in enumerate(chips):
                    got = _half(outs[w].at[2 * cx + cy], c, r2)
                    cps.append(_remote(got, got, send_sems, recv_sems, per * w + 3 + j, sib))
            return cps

        if phase == 0:
            for cp in first():
                cp.start()
        elif phase == 1:
            it = iter(passed())
            for w in range(ns + nw):
                src, out, base = srcs[w], outs[w], per * w
                r2 = src.shape[0] // 2
                for j, (cx, cy) in enumerate(chips):
                    got = _half(out.at[2 * cx + cy], c, r2) if w < ns else out.at[2 * cx + cy]
                    _remote(got, got, send_sems, recv_sems, base + j, sib).wait_recv()
                    if w < ns:
                        next(it).start()
        else:
            for w in range(ns + nw):
                src, out, base = srcs[w], outs[w], per * w
                r2 = src.shape[0] // 2
                if w < ns:
                    for j, (cx, cy) in enumerate(chips):
                        got = _half(out.at[2 * cx + cy], 1 - c, r2)
                        _remote(got, got, send_sems, recv_sems, base + 3 + j, sib).wait_recv()
                _remote(src, out.at[me], send_sems, recv_sems, base + 6, sib).wait_recv()
            for cp in first() + passed():
                cp.wait_send()

    args = list(shards) + list(wholes)
    outs = [jax.ShapeDtypeStruct((4,) + a.shape, a.dtype) for a in args]
    return _Comm(args, outs, per * len(args), 3, run)


def _sibling_send_halves(gs):
    n = len(gs)

    def run(phase, srcs, outs, send_sems, recv_sems):
        x, y, c, _ = _where_am_i()
        sib = (x, y, 1 - c)
        cps = []
        for w in range(n):
            r2 = srcs[w].shape[1] // 2
            for k in range(4):
                cps.append(_remote(_half(srcs[w].at[k], 1 - c, r2), outs[w].at[k], send_sems, recv_sems, 4 * w + k, sib))
        for cp in cps:
            cp.start() if phase == 0 else cp.wait()

    outs = [jax.ShapeDtypeStruct((4, g.shape[1] // 2, g.shape[2]), g.dtype) for g in gs]
    return _Comm(gs, outs, 4 * n, 2, run)


def _chips_exchange(ps):
    n = len(ps)

    def run(phase, srcs, outs, send_sems, recv_sems):
        x, y, c, chips = _where_am_i()
        me = 2 * x + y
        cps = []
        for w in range(n):
            for j, (cx, cy) in enumerate(chips):
                cps.append(_remote(srcs[w].at[2 * cx + cy], outs[w].at[me], send_sems, recv_sems, 3 * w + j, (cx, cy, c)))
        if phase == 0:
            for cp in cps:
                cp.start()
        else:
            for w in range(n):
                for j, (cx, cy) in enumerate(chips):
                    got = outs[w].at[2 * cx + cy]
                    _remote(got, got, send_sems, recv_sems, 3 * w + j, (cx, cy, c)).wait_recv()
            for cp in cps:
                cp.wait_send()

    outs = [jax.ShapeDtypeStruct(p.shape, p.dtype) for p in ps]
    return _Comm(ps, outs, 3 * n, 2, run)


def _sibling_swap(rs):
    n = len(rs)

    def run(phase, srcs, outs, send_sems, recv_sems):
        x, y, c, _ = _where_am_i()
        for w in range(n):
            cp = _remote(srcs[w], outs[w], send_sems, recv_sems, w, (x, y, 1 - c))
            cp.start() if phase == 0 else cp.wait()

    outs = [jax.ShapeDtypeStruct(r.shape, r.dtype) for r in rs]
    return _Comm(rs, outs, n, 2, run)


def _gather_all(v):
    rows = v.shape[0]

    def body(x_ref, out_ref, send_sems, recv_sems, local_sem):
        x, y, c, chips = _where_am_i()
        me, sib = (x, y, c), (x, y, 1 - c)
        blk = lambda px, py, pc: out_ref.at[4 * px + 2 * py + pc]
        mine = pltpu.make_async_copy(x_ref, blk(*me), local_sem)
        mine.start()
        first = [_remote(x_ref, blk(*me), send_sems, recv_sems, 0, sib)]
        first += [_remote(x_ref, blk(*me), send_sems, recv_sems, 1 + j, (*chip, c)) for j, chip in enumerate(chips)]
        for cp in first:
            cp.start()
        passed = [_remote(blk(*chip, c), blk(*chip, c), send_sems, recv_sems, 4 + j, sib) for j, chip in enumerate(chips)]
        for j, chip in enumerate(chips):
            _remote(blk(*chip, c), blk(*chip, c), send_sems, recv_sems, 1 + j, me).wait_recv()
            passed[j].start()
        _remote(blk(*sib), blk(*sib), send_sems, recv_sems, 0, me).wait_recv()
        for j, chip in enumerate(chips):
            _remote(blk(*chip, 1 - c), blk(*chip, 1 - c), send_sems, recv_sems, 4 + j, me).wait_recv()
        for cp in first + passed:
            cp.wait_send()
        mine.wait()

    vm = pl.BlockSpec(memory_space=pltpu.VMEM)
    return pl.pallas_call(
        body, name="gather_all", out_shape=jax.ShapeDtypeStruct((8, rows, LANES), v.dtype),
        in_specs=[vm], out_specs=vm,
        scratch_shapes=[pltpu.SemaphoreType.DMA((7,)), pltpu.SemaphoreType.DMA((7,)), pltpu.SemaphoreType.DMA],
    )(v)


def _flat_pad(parts, total):
    v = jnp.concatenate([p.reshape(-1) for p in parts])
    return jnp.pad(v, (0, total - v.shape[0]))


def _adamw(w, g, m, v):
    m = ADAM_B1 * m + (1.0 - ADAM_B1) * g
    v = ADAM_B2 * v + (1.0 - ADAM_B2) * jnp.square(g)
    m_hat = m / (1.0 - ADAM_B1 ** ADAM_STEP)
    v_hat = v / (1.0 - ADAM_B2 ** ADAM_STEP)
    delta = -ADAM_LR * (m_hat / (jnp.sqrt(v_hat) + ADAM_EPS) + ADAM_WD * w)
    return delta, m, v


def _adamw_call(w, g, m, v, name):
    r, cdim = w.shape
    tm = _tile(r, ROW_TILE, 8)
    o = _rout(r, cdim, f32, tm)
    return _rowwise(_adamw, r, tm, [_rows(w, tm), _rows(g, tm), _rows(m, tm), _rows(v, tm)], [o, o, o], [], name)


def _adamw_halves(w, m, v, g_mine, g_sib, sp, name):
    _, r, cdim = w.shape
    r2 = r // 2
    tr = _row_tile(r2, cdim, 8)
    nb = r2 // tr

    def body(sp_ref, w_ref, m_ref, v_ref, ga_ref, gb_ref, g_out, d_out, m_out, v_out):
        g = jnp.where(pl.program_id(0) == sp_ref[1], ga_ref[...], gb_ref[...])
        delta, mn, vn = _adamw(w_ref[...], g, m_ref[...], v_ref[...])
        g_out[...] = g
        d_out[...] = delta
        m_out[...] = mn
        v_out[...] = vn

    full = pl.BlockSpec((None, tr, cdim), lambda h, i, s: (0, h * nb + i, 0))
    mine = pl.BlockSpec((tr, cdim), lambda h, i, s: (jnp.where(h == s[1], i, 0), 0))
    sib = pl.BlockSpec((tr, cdim), lambda h, i, s: (jnp.where(h == s[1], 0, i), 0))
    out = jax.ShapeDtypeStruct((1, r, cdim), f32)
    gs = pltpu.PrefetchScalarGridSpec(num_scalar_prefetch=1, grid=(2, nb), in_specs=[full, full, full, mine, sib],
                                      out_specs=[full, full, full, full])
    return pl.pallas_call(body, name=name, out_shape=[out, out, out, out], grid_spec=gs,
                          compiler_params=_params(("parallel", "parallel")))(sp, w, m, v, g_mine, g_sib)


MIXER = ("w_in", "w_uq", "w_ukv", "w_out")
FFN = ("w_gate", "w_up", "w_down")
BIG = MIXER + FFN
SMALL = ("q_norm_w", "kv_norm_w", "conv_w", "conv_b", "dt_bias", "a_log", "d_skip", "ssd_norm_w", "attn_out_norm_w",
         "pre_mix_norm_w", "post_mix_norm_w", "pre_ffn_norm_w", "post_ffn_norm_w")
ORDER = ("w_in", "q_norm_w", "w_uq", "kv_norm_w", "w_ukv", "conv_w", "conv_b", "dt_bias", "a_log", "d_skip",
         "ssd_norm_w", "attn_out_norm_w", "w_out", "pre_mix_norm_w", "post_mix_norm_w", "pre_ffn_norm_w",
         "post_ffn_norm_w", "w_gate", "w_up", "w_down")


def kernel(x, positions, w_in, q_norm_w, w_uq, kv_norm_w, w_ukv, conv_w, conv_b, dt_bias, a_log, d_skip, ssd_norm_w, attn_out_norm_w, w_out, pre_mix_norm_w, post_mix_norm_w, pre_ffn_norm_w, post_ffn_norm_w, w_gate, w_up, w_down, loss_target, m_w_in, m_q_norm_w, m_w_uq, m_kv_norm_w, m_w_ukv, m_conv_w, m_conv_b, m_dt_bias, m_a_log, m_d_skip, m_ssd_norm_w, m_attn_out_norm_w, m_w_out, m_pre_mix_norm_w, m_post_mix_norm_w, m_pre_ffn_norm_w, m_post_ffn_norm_w, m_w_gate, m_w_up, m_w_down, v_w_in, v_q_norm_w, v_w_uq, v_kv_norm_w, v_w_ukv, v_conv_w, v_conv_b, v_dt_bias, v_a_log, v_d_skip, v_ssd_norm_w, v_attn_out_norm_w, v_w_out, v_pre_mix_norm_w, v_post_mix_norm_w, v_pre_ffn_norm_w, v_post_ffn_norm_w, v_w_gate, v_w_up, v_w_down):
    local = dict(locals())
    wts = {n: local[n][0] for n in ORDER}
    mom_m = {n: local["m_" + n][0] for n in ORDER}
    mom_v = {n: local["v_" + n][0] for n in ORDER}
    xs = x[0]
    tgt = loss_target[0]
    t, d = xs.shape
    nchip = 4
    my_x, my_y, my_c = lax.axis_index("x"), lax.axis_index("y"), lax.axis_index("c")
    my_chip = 2 * my_x + my_y

    mla_w = d // 2
    nh = mla_w // V_HEAD
    width = d - mla_w
    hs = width // SSD_P
    gn = SSD_G * SSD_N
    cdim = width + 2 * gn
    in_sizes = (Q_RANK, KV_RANK, ROPE, width, cdim, hs)
    d_in = sum(in_sizes)
    tail = LANES
    off_xbc = 0
    off_cq = cdim
    off_ckv = off_cq + Q_RANK
    off_z = off_ckv + KV_RANK
    off_tail = off_z + width
    d_in_p = _round_up(off_tail + tail, 256)
    gw = width // SSD_G
    assert off_cq % Q_RANK == 0 and off_ckv % KV_RANK == 0 and off_z % gw == 0 and off_tail % LANES == 0
    qk_head = NOPE + ROPE
    scale = qk_head ** -0.5
    tm = _tile(t, ROW_TILE, 8)
    tmw = _tile(t, ROW_TILE // 2, 8)

    sp = jnp.stack([my_chip, my_c]).astype(jnp.int32)

    def pair_sums(names, gl, from_sib):
        res = []
        for n, g, fs in zip(names, gl, from_sib):
            _, r2, cs = fs.shape
            tr = _row_tile(r2, cs, 16)
            nb = r2 // tr
            res.append(_blocked(
                lambda a, b: a + b, (nchip, nb),
                [(g, (None, tr, cs), lambda k, i, s, nb=nb: (k, s[1] * nb + i, 0)),
                 (fs, (None, tr, cs), lambda k, i, s: (k, i, 0))],
                [((nchip, r2, cs), bf16, (None, tr, cs), lambda k, i, s: (k, i, 0))], [], "pair_sum_" + n, sp=sp)[0])
        return res

    def chip_sums(names, gl, from_sib, from_chips):
        res = []
        for n, g, fs, fc in zip(names, gl, from_sib, from_chips):
            _, r2, cs = fs.shape
            tr = _row_tile(r2, cs, 16)
            nb = r2 // tr
            res.append(_blocked(
                lambda a, b, r1, r2_, r3: ((a + b) + r1.astype(f32)) + r2_.astype(f32) + r3.astype(f32), (nb,),
                [(g, (None, tr, cs), lambda i, s, nb=nb: (s[0], s[1] * nb + i, 0)),
                 (fs, (None, tr, cs), lambda i, s: (s[0], i, 0)),
                 (fc, (None, tr, cs), lambda i, s: (s[0] ^ 1, i, 0)),
                 (fc, (None, tr, cs), lambda i, s: (s[0] ^ 2, i, 0)),
                 (fc, (None, tr, cs), lambda i, s: (s[0] ^ 3, i, 0))],
                [((r2, cs), f32, (tr, cs), lambda i, s: (i, 0))], [], "chip_sum_" + n, sp=sp)[0])
        return res

    ck, ccs = wts["conv_w"].shape
    cs_in = wts["w_in"].shape[1]
    flat_t = lambda a3: jnp.swapaxes(a3, 1, 2).reshape(1, cs_in * d // LANES, LANES)
    (u,), (w_in_g, conv_g) = _rowwise(
        lambda a, w: _rms(a, w), t, tm, [_rows(xs, tm), _par(wts["pre_mix_norm_w"].reshape(1, -1))],
        [_rout(t, d, bf16, tm)], [], "pre_mix_norm",
        comm=_gather_weights([flat_t(w_in)[0].astype(bf16)], [wts["conv_w"]]), at=(0.0, 0.5))
    cat_cols = lambda g: jnp.concatenate([g[k] for k in range(nchip)], axis=1)
    conv_full = cat_cols(conv_g)
    mixer_gather = _gather_weights([wts[n].astype(bf16) for n in MIXER[1:]], [])
    ffn_gather = _gather_weights([wts[n].astype(bf16) for n in FFN[:2]], [])
    w_down_b = wts["w_down"].astype(bf16)
    dn2 = w_down_b.shape[0] // 2
    down_gathers = [_gather_weights([w_down_b[:dn2]], []), _gather_weights([w_down_b[dn2:]], [])]

    wi = w_in_g.reshape(nchip * cs_in, d)
    o = np.cumsum((0,) + in_sizes)
    seg = lambda i: wi[o[i]:o[i + 1]]
    w_in_pt = jnp.concatenate([seg(4), seg(0), seg(1), seg(3), seg(2), seg(5),
                               jnp.zeros((d_in_p - off_tail - ROPE - hs, d), bf16)], axis=0)

    inv_freq = ROPE_THETA ** (-jnp.arange(0, ROPE, 2, dtype=f32) / ROPE)
    ang = positions[0].astype(f32)[:, None] * inv_freq
    cos, sin = jnp.cos(ang), jnp.sin(ang)
    z32, z64, z96 = jnp.zeros((t, 32), f32), jnp.zeros((t, 64), f32), jnp.zeros((t, 96), f32)
    cosp = jnp.concatenate([cos, cos, z64], axis=1)
    sina = jnp.concatenate([-sin, z96], axis=1)
    sinb = jnp.concatenate([z32, sin, z64], axis=1)

    row = lambda a: a.reshape(1, -1)
    w_pre_mix, w_post_mix = row(wts["pre_mix_norm_w"]), row(wts["post_mix_norm_w"])
    w_pre_ffn, w_post_ffn = row(wts["pre_ffn_norm_w"]), row(wts["post_ffn_norm_w"])
    w_qn, w_kvn = row(wts["q_norm_w"]), row(wts["kv_norm_w"])
    w_attn_n, w_ssd_n = row(wts["attn_out_norm_w"]), row(wts["ssd_norm_w"])
    conv_b_r = row(wts["conv_b"])
    dtb, alog, dskip = row(wts["dt_bias"]), row(wts["a_log"]), row(wts["d_skip"])
    dtb_t, alog_t = dtb.reshape(hs, 1), alog.reshape(hs, 1)

    proj, (w_uq_g, w_ukv_f, w_out_g) = _matmul([(u, w_in_pt)], "nt", f32, "in_proj", comm=mixer_gather,
                                               at=(0.0, 0.6))
    w_uq_p = jnp.pad(cat_cols(w_uq_g).reshape(Q_RANK, nh, qk_head), ((0, 0), (0, 0), (0, QK_PAD - qk_head))
                     ).reshape(Q_RANK, nh * QK_PAD)
    w_out_f = w_out_g.reshape(-1, w_out_g.shape[2])
    cq_in = _rows(proj, tm, Q_RANK, off_cq // Q_RANK)
    ckv_in = _rows(proj, tm, KV_RANK, off_ckv // KV_RANK)
    cqn = _rowwise(lambda a, w: _rms(a, w), t, tm, [cq_in, _par(w_qn)], [_rout(t, Q_RANK, bf16, tm)], [], "q_norm")[0]
    ckvn = _rowwise(lambda a, w: _rms(a, w), t, tm, [ckv_in, _par(w_kvn)], [_rout(t, KV_RANK, bf16, tm)], [],
                    "kv_norm")[0]
    q_raw = _matmul([(cqn, w_uq_p)], "nn", f32, "q_up")
    kv = _matmul([(ckvn, w_ukv_f)], "nn", bf16, "kv_up")

    def q_rope_fn(qt, cp, sa, sb):
        parts = []
        for h in range(nh):
            parts.append(qt[:, h * QK_PAD: h * QK_PAD + NOPE])
            parts.append(_rope(qt[:, h * QK_PAD + NOPE:(h + 1) * QK_PAD], cp, sa, sb))
        return jnp.concatenate(parts, axis=1)

    tail_cb = off_tail // LANES
    q2 = _rowwise(q_rope_fn, t, tm, [_rows(q_raw, tm), _rows(cosp, tm), _rows(sina, tm), _rows(sinb, tm)],
                  [_rout(t, nh * QK_PAD, bf16, tm)], [], "q_rope")[0]
    kr2 = _rowwise(_rope, t, tm, [_rows(proj, tm, LANES, tail_cb), _rows(cosp, tm), _rows(sina, tm), _rows(sinb, tm)],
                   [_rout(t, LANES, bf16, tm)], [], "k_rope")[0]
    o_att, lse, ffn_w = _flash_fwd(q2, kv, kr2, nh, scale, ffn_gather, (0.0, 0.75))
    w_gate_f, w_up_f = ffn_w

    xbc_act = _conv_fwd(proj, conv_full, conv_b_r, cdim, tm)
    dt_raw, dt_raw_t = _blocked(
        lambda blk: (blk[:, ROPE:ROPE + hs], blk.T[ROPE:ROPE + hs, :]), (t // tm,),
        [_rows(proj, tm, LANES, tail_cb)],
        [((t, hs), f32, (tm, hs), lambda i: (i, 0)), ((hs, t), f32, (hs, tm), lambda i: (0, i))], [], "dt_split")
    y_ssd, hprev, (down_a,) = _ssd_fwd(xbc_act, dt_raw, dt_raw_t, dtb, dtb_t, alog, alog_t, dskip, width,
                                       down_gathers[0], (0.0, 0.6))
    z_ins = [_rows(proj, tm, gw, off_z // gw + i) for i in range(SSD_G)]

    def mix_norms_fn(ov, yv, *rest):
        zs, wa, ws = rest[:SSD_G], rest[SSD_G], rest[SSD_G + 1]
        outs = [_rms(ov, wa)]
        for i in range(SSD_G):
            sl = slice(i * gw, (i + 1) * gw)
            outs.append(_rms(yv[:, sl] * (zs[i] * _sigmoid(zs[i])), ws[:, sl]))
        return jnp.concatenate(outs, axis=1)

    cat = _rowwise(mix_norms_fn, t, tm, [_rows(o_att, tm), _rows(y_ssd, tm)] + z_ins + [_par(w_attn_n), _par(w_ssd_n)],
                   [_rout(t, d, bf16, tm)], [], "attn_ssd_out_norms")[0]
    mix, (down_b,) = _matmul([(cat, w_out_f)], "nn", f32, "out_proj", comm=down_gathers[1], at=(0.0, 0.6))
    w_down_f = jnp.concatenate([down_a, down_b], axis=1).reshape(-1, down_a.shape[2])

    def post_mix_fn(mx, xv, w1, w2):
        h1v = xv + _rms(mx, w1)
        return h1v, _rms(h1v, w2)

    h1, v_in = _rowwise(post_mix_fn, t, tmw, [_rows(mix, tmw), _rows(xs, tmw), _par(w_post_mix), _par(w_pre_ffn)],
                        [_rout(t, d, f32, tmw), _rout(t, d, bf16, tmw)], [], "post_mix_pre_ffn_norm")
    g_ff, u_ff, act = _ffn_up(v_in, w_gate_f, w_up_f)
    ffn = _matmul([(act, w_down_f)], "nn", f32, "ffn_down")

    def final_fn(fv, h1v, tg, w):
        h2 = h1v + _rms(fv, w)
        err = h2 - tg
        lpart = 0.5 * jnp.sum(jnp.sum(err * err, axis=1, keepdims=True), axis=0, keepdims=True) / d
        dh2 = err / d
        dff, dw = _rms_bwd(fv, w, dh2)
        return dff, dh2, jnp.broadcast_to(lpart, (1, LANES)), dw

    dffn, dh2, loss_acc, g_post_ffn = _rowwise(
        final_fn, t, tmw, [_rows(ffn, tmw), _rows(h1, tmw), _rows(tgt, tmw), _par(w_post_ffn)],
        [_rout(t, d, bf16, tmw), _rout(t, d, f32, tmw)], [(1, LANES), (1, d)], "loss_post_ffn_norm_bwd")
    loss = lax.psum(loss_acc[0, 0], ("x", "y", "c"))

    dg_ff, du_ff = _ffn_down_bwd(dffn, w_down_f, g_ff, u_ff)
    gw_down = _matmul([(act, dffn)], "tn", f32, "grad_w_down")
    gw_gate = _matmul([(v_in, dg_ff)], "tn", f32, "grad_w_gate", out_chunks=nchip)
    gw_up = _matmul([(v_in, du_ff)], "tn", f32, "grad_w_up", out_chunks=nchip)
    ffn_g = [gw_gate, gw_up, gw_down.reshape(nchip, -1, d)]
    dv_in, ffn_from_sib = _matmul([(dg_ff, w_gate_f), (du_ff, w_up_f)], "nt", f32, "ffn_up_bwd",
                                  comm=_sibling_send_halves(ffn_g), at=(0.0,))
    ffn_pairs = pair_sums(FFN, ffn_g, ffn_from_sib)

    def mid_bwd_fn(h1v, dvv, dh2v, mx, w_pf, w_pm):
        dxn, dw_pf = _rms_bwd(h1v, w_pf, dvv)
        dh1v = dh2v + dxn
        dmx, dw_pm = _rms_bwd(mx, w_pm, dh1v)
        return dh1v, dmx, dw_pf, dw_pm

    dh1, dmix, g_pre_ffn, g_post_mix = _rowwise(
        mid_bwd_fn, t, tmw, [_rows(h1, tmw), _rows(dv_in, tmw), _rows(dh2, tmw), _rows(mix, tmw),
                             _par(w_pre_ffn), _par(w_post_mix)],
        [_rout(t, d, f32, tmw), _rout(t, d, bf16, tmw)], [(1, d), (1, d)], "pre_ffn_post_mix_norm_bwd")
    dcat = _matmul([(dmix, w_out_f)], "nt", f32, "out_proj_bwd")
    gw_out = _matmul([(cat, dmix)], "tn", f32, "grad_w_out")

    def mix_norms_bwd_fn(ov, yv, *rest):
        zs, dcv, wa, ws = rest[:SSD_G], rest[SSD_G], rest[SSD_G + 1], rest[SSD_G + 2]
        dov, dwa = _rms_bwd(ov, wa, dcv[:, :mla_w])
        dl = [jnp.broadcast_to(jnp.sum(dov[:, h * V_HEAD:(h + 1) * V_HEAD] * ov[:, h * V_HEAD:(h + 1) * V_HEAD],
                                       axis=1, keepdims=True), (ov.shape[0], V_HEAD)) for h in range(nh)]
        dys, dzs, dws = [], [], []
        for i in range(SSD_G):
            sl = slice(i * gw, (i + 1) * gw)
            zv, yi = zs[i], yv[:, sl]
            sg = _sigmoid(zv)
            sz = zv * sg
            dgi, dwi = _rms_bwd(yi * sz, ws[:, sl], dcv[:, mla_w + i * gw: mla_w + (i + 1) * gw])
            dys.append(dgi * sz)
            dzs.append(dgi * yi * (sg * (1.0 + zv * (1.0 - sg))))
            dws.append(dwi)
        cc = lambda a: jnp.concatenate(a, axis=1)
        return dov, cc(dl), cc(dys), cc(dzs), dwa, cc(dws)

    do_att, delta, dy_ssd, dz, g_attn_n, g_ssd_n = _rowwise(
        mix_norms_bwd_fn, t, tm,
        [_rows(o_att, tm), _rows(y_ssd, tm)] + z_ins + [_rows(dcat, tm), _par(w_attn_n), _par(w_ssd_n)],
        [_rout(t, mla_w, bf16, tm), _rout(t, mla_w, f32, tm), _rout(t, width, f32, tm), _rout(t, width, bf16, tm)],
        [(1, mla_w), (1, width)], "attn_ssd_out_norms_bwd")
    dkv, dkr_h, dq2, gu_from_chips = _flash_bwd(q2, kv, kr2, do_att, lse, delta, nh, scale,
                                                _chips_exchange(ffn_pairs[:2]), (0.0,))

    def q_rope_bwd_fn(dqt, cp, sa, sb):
        parts = []
        for h in range(nh):
            parts.append(dqt[:, h * QK_PAD: h * QK_PAD + NOPE])
            parts.append(_rope_bwd(dqt[:, h * QK_PAD + NOPE:(h + 1) * QK_PAD], cp, sa, sb))
        return jnp.concatenate(parts, axis=1)

    dq_raw = _rowwise(q_rope_bwd_fn, t, tm, [_rows(dq2, tm), _rows(cosp, tm), _rows(sina, tm), _rows(sinb, tm)],
                      [_rout(t, nh * QK_PAD, bf16, tm)], [], "q_rope_bwd")[0]

    def k_rope_bwd_fn(dk, cp, sa, sb):
        tot = dk[:, 0:LANES]
        for h in range(1, nh):
            tot = tot + dk[:, h * LANES:(h + 1) * LANES]
        return _rope_bwd(tot, cp, sa, sb)

    dkr = _rowwise(k_rope_bwd_fn, t, tm, [_rows(dkr_h, tm), _rows(cosp, tm), _rows(sina, tm), _rows(sinb, tm)],
                   [_rout(t, LANES, f32, tm)], [], "k_rope_bwd")[0]
    gw_uq_p = _matmul([(cqn, dq_raw)], "tn", f32, "grad_w_uq")
    gw_ukv = _matmul([(ckvn, dkv)], "tn", f32, "grad_w_ukv", out_chunks=nchip)
    dcqn = _matmul([(dq_raw, w_uq_p)], "nt", f32, "q_up_bwd")
    dckvn = _matmul([(dkv, w_ukv_f)], "nt", f32, "kv_up_bwd")

    def lat_norm_bwd_fn(a, w, dyv):
        return _rms_bwd(a, w, dyv)

    dcq, g_qn = _rowwise(lat_norm_bwd_fn, t, tm, [cq_in, _par(w_qn), _rows(dcqn, tm)],
                         [_rout(t, Q_RANK, bf16, tm)], [(1, Q_RANK)], "q_norm_bwd")
    dckv, g_kvn = _rowwise(lat_norm_bwd_fn, t, tm, [ckv_in, _par(w_kvn), _rows(dckvn, tm)],
                           [_rout(t, KV_RANK, bf16, tm)], [(1, KV_RANK)], "kv_norm_bwd")

    dxbc_act, ddt_raw, g_alog, g_dskip, g_dtb, down_from_chips = _ssd_bwd(
        xbc_act, dt_raw, dt_raw_t, dtb, dtb_t, alog, alog_t, dskip, hprev, dy_ssd, width,
        _chips_exchange(ffn_pairs[2:]), (0.0,))
    ffn_halves = chip_sums(FFN, ffn_g, ffn_from_sib, gu_from_chips + down_from_chips)
    dpre, gcw0, gcw1, gcw2, gcw3, g_conv_b = _conv_bwd_pre(proj, conv_full, conv_b_r, dxbc_act, cdim, tm)
    g_conv_w = jnp.concatenate([gcw0, gcw1, gcw2, gcw3], axis=0)
    dxbc = _conv_bwd_dx(dpre, conv_full, tm)

    def dproj_fn(a, b, c, e, kr_blk, dt_blk):
        parts = [a, b, c, e, (kr_blk + dt_blk).astype(bf16)]
        if d_in_p > off_tail + tail:
            parts.append(jnp.zeros((a.shape[0], d_in_p - off_tail - tail), bf16))
        return jnp.concatenate(parts, axis=1)

    dproj = _rowwise(dproj_fn, t, tm, [_rows(dxbc, tm), _rows(dcq, tm), _rows(dckv, tm), _rows(dz, tm),
                                       _rows(dkr, tm), _rows(ddt_raw, tm)],
                     [_rout(t, d_in_p, bf16, tm)], [], "in_proj_grad_pack")[0]
    split_cols = lambda gf: jnp.stack(jnp.split(gf, nchip, axis=1))
    split_rows = lambda gf: gf.reshape(nchip, gf.shape[0] // nchip, gf.shape[1])
    gw_uq = gw_uq_p.reshape(Q_RANK, nh, QK_PAD)[:, :, :qk_head].reshape(Q_RANK, nh * qk_head)
    rest_g = [split_cols(gw_uq), gw_ukv, split_rows(gw_out)]
    gw_in_pt, rest_from_sib = _matmul([(dproj, u)], "tn", f32, "grad_w_in",
                                      comm=_sibling_send_halves(rest_g), at=(0.0,))
    gseg = lambda a, b: gw_in_pt[a:b]
    gw_in_t = jnp.concatenate([gseg(off_cq, off_ckv), gseg(off_ckv, off_z), gseg(off_tail, off_tail + ROPE),
                               gseg(off_z, off_tail), gseg(off_xbc, off_cq),
                               gseg(off_tail + ROPE, off_tail + ROPE + hs)], axis=0)
    in_g = [gw_in_t.reshape(nchip, cs_in * d // LANES, LANES)]
    du_in, in_from_sib = _matmul([(dproj, w_in_pt)], "nn", f32, "in_proj_bwd",
                                 comm=_sibling_send_halves(in_g), at=(0.0,))
    mix_g, mix_from_sib = in_g + rest_g, in_from_sib + rest_from_sib
    mix_pairs = pair_sums(MIXER, mix_g, mix_from_sib)

    def first_bwd_fn(xv, duv, dh1v, w):
        dxn, dw = _rms_bwd(xv, w, duv)
        return dh1v + dxn, dw

    (grad_x, g_pre_mix), mix_from_chips = _rowwise(
        first_bwd_fn, t, tmw, [_rows(xs, tmw), _rows(du_in, tmw), _rows(dh1, tmw), _par(w_pre_mix)],
        [_rout(t, d, f32, tmw)], [(1, d)], "pre_mix_norm_bwd", comm=_chips_exchange(mix_pairs), at=(0.0,))

    mix_halves = chip_sums(MIXER, mix_g, mix_from_sib, mix_from_chips)
    halves = mix_halves + ffn_halves
    sib_halves = _comm_call(_sibling_swap(halves), "sibling_swap")
    gshard = {}

    gsmall = {"q_norm_w": g_qn, "kv_norm_w": g_kvn, "conv_w": g_conv_w, "conv_b": g_conv_b, "dt_bias": g_dtb,
              "a_log": g_alog, "d_skip": g_dskip, "ssd_norm_w": g_ssd_n, "attn_out_norm_w": g_attn_n,
              "pre_mix_norm_w": g_pre_mix, "post_mix_norm_w": g_post_mix, "pre_ffn_norm_w": g_pre_ffn,
              "post_ffn_norm_w": g_post_ffn}
    small_sizes = [int(np.prod(gsmall[n].shape)) for n in SMALL]
    srows = _round_up(-(-sum(small_sizes) // LANES), 8)
    spart = _flat_pad([gsmall[n] for n in SMALL], srows * LANES).reshape(srows, LANES)
    sall = _gather_all(spart)

    def sum8_fn(a):
        tot = a[0]
        for k in range(1, 8):
            tot = tot + a[k]
        return tot

    ssum = _blocked(sum8_fn, (1,), [(sall, sall.shape, lambda i: (0, 0, 0))],
                    [((srows, LANES), f32, (srows, LANES), lambda i: (0, 0))], [], "small_grad_sum")[0].reshape(-1)
    gred = {}
    off = 0
    for n, sz in zip(SMALL, small_sizes):
        gred[n] = ssum[off:off + sz].reshape(gsmall[n].shape)
        off += sz
    gshard["conv_w"] = lax.dynamic_slice_in_dim(gred["conv_w"], my_chip * ccs, ccs, axis=1)
    for n in SMALL:
        if n != "conv_w":
            gshard[n] = gred[n].reshape(wts[n].shape)

    delta, new_m, new_v = {}, {}, {}
    big_out = {}
    for n, mine_h, sib_h in zip(BIG, halves, sib_halves):
        view = flat_t if n == "w_in" else (lambda a3: a3)
        res = _adamw_halves(view(local[n]), view(local["m_" + n]), view(local["v_" + n]), mine_h, sib_h, sp,
                            "adamw_" + n)
        big_out[n] = [jnp.swapaxes(a.reshape(1, cs_in, d), 1, 2) for a in res] if n == "w_in" else res
    pack = lambda src: _flat_pad([src[n] for n in SMALL], srows * LANES).reshape(srows, LANES)
    sd, sm, sv = _adamw_call(pack(wts), pack(gshard), pack(mom_m), pack(mom_v), "adamw_small")
    off = 0
    for n in SMALL:
        sz = int(np.prod(wts[n].shape))
        for dst, src in ((delta, sd), (new_m, sm), (new_v, sv)):
            dst[n] = src.reshape(-1)[off:off + sz].reshape(wts[n].shape)
        off += sz

    small_out = (gshard, delta, new_m, new_v)
    pick = lambda k: [big_out[n][k] if n in big_out else small_out[k][n][None] for n in ORDER]
    return (loss, grad_x[None], *pick(0), *pick(1), *pick(2), *pick(3))
```

```python
import functools

import numpy as np
import jax
import jax.numpy as jnp
from jax import lax
from jax.experimental import pallas as pl
from jax.experimental.pallas import tpu as pltpu

f32, bf16 = jnp.float32, jnp.bfloat16

EPS = 1e-6
V_HEAD = 128
NOPE = 128
ROPE = 64
QK_PAD = 256
Q_RANK = 512
KV_RANK = 512
ROPE_THETA = 10000.0
SSD_P = 64
SSD_G = 2
SSD_N = 128
SSD_K = 4
CHUNK = 128
ADAM_LR, ADAM_B1, ADAM_B2, ADAM_EPS, ADAM_WD, ADAM_STEP = 0.001, 0.9, 0.999, 1e-08, 0.01, 10

VMEM_LIMIT_BYTES = 48 * 1024 * 1024
LANES = 128
ATT_TILE = 1024
ATT_TILE_FWD = 1024
MM_TM, MM_TN, MM_TK = 1408, 1024, 1408
CHUNK_WHOLE_MAX = 1536
ROW_TILE = 256
ROW_BLOCK_BYTES = 2 * 1024 * 1024

NN = (((1,), (0,)), ((), ()))
NT = (((1,), (1,)), ((), ()))
TN = (((0,), (0,)), ((), ()))
MESH = pl.DeviceIdType.MESH
ANY = pl.BlockSpec(memory_space=pl.ANY)


def _tile(dim, cap, align=LANES):
    if dim <= cap:
        return dim
    t = (cap // align) * align
    while t >= align:
        if dim % t == 0:
            return t
        t -= align
    raise ValueError(f"no tile for {dim} under {cap}")


def _row_tile(rows, cols, align):
    best = None
    for tr in range(align, rows + 1, align):
        if rows % tr == 0 and tr * cols * 4 <= ROW_BLOCK_BYTES:
            best = tr
    return best or rows


def _round_up(n, m):
    return -(-n // m) * m


def _params(sem):
    return pltpu.CompilerParams(dimension_semantics=sem, vmem_limit_bytes=VMEM_LIMIT_BYTES)


def _dot(a, b, dims):
    return lax.dot_general(a.astype(bf16), b.astype(bf16), dims, preferred_element_type=f32)


def _call(body, name, out_shape, grid, in_specs, out_specs, scratch_shapes, sem, args, comm=None, at=None,
          prefetch=()):
    npf = len(prefetch)
    if comm is None:
        gs = pltpu.PrefetchScalarGridSpec(num_scalar_prefetch=npf, grid=grid, in_specs=list(in_specs),
                                          out_specs=list(out_specs), scratch_shapes=list(scratch_shapes))
        res = pl.pallas_call(body, name=name, out_shape=list(out_shape), grid_spec=gs,
                             compiler_params=_params(sem))(*prefetch, *args)
        return list(res), []
    n_in, n_out, n_sc = len(args), len(out_shape), len(scratch_shapes)
    na, no = len(comm.args), len(comm.outs)
    steps = int(np.prod(grid))

    def full(*allrefs):
        pf, refs = allrefs[:npf], allrefs[npf:]
        cin = refs[n_in:n_in + na]
        o0 = n_in + na
        cout = refs[o0 + n_out:o0 + n_out + no]
        s0 = o0 + n_out + no
        send_sems, recv_sems = refs[s0 + n_sc], refs[s0 + n_sc + 1]
        lin = pl.program_id(0)
        for dim in range(1, len(grid)):
            lin = lin * grid[dim] + pl.program_id(dim)
        for p in range(comm.nphase - 1):
            @pl.when(lin == int(round(at[p] * (steps - 1))))
            def _(p=p):
                comm.run(p, cin, cout, send_sems, recv_sems)
        body(*pf, *refs[:n_in], *refs[o0:o0 + n_out], *refs[s0:s0 + n_sc])

        @pl.when(lin == steps - 1)
        def _():
            comm.run(comm.nphase - 1, cin, cout, send_sems, recv_sems)

    gs = pltpu.PrefetchScalarGridSpec(
        num_scalar_prefetch=npf, grid=grid, in_specs=list(in_specs) + [ANY] * na,
        out_specs=list(out_specs) + [ANY] * no, scratch_shapes=list(scratch_shapes) + comm.sems())
    res = pl.pallas_call(full, name=name, out_shape=list(out_shape) + comm.outs, grid_spec=gs,
                         compiler_params=_params(("arbitrary",) * len(grid)))(*prefetch, *args, *comm.args)
    return list(res[:n_out]), list(res[n_out:])


def _chunk_tile(cs, cap):
    return cs if cs <= CHUNK_WHOLE_MAX else _tile(cs, cap)


def _matmul(pairs, mode, out_dtype, name, out_chunks=None, comm=None, at=None):
    a0, b0 = pairs[0]
    chunked = b0.ndim == 3
    cs = b0.shape[2] if chunked else None
    bcols = b0.shape[0] * b0.shape[2] if chunked else b0.shape[1]
    brows = b0.shape[1] if chunked else b0.shape[0]
    if mode == "nn":
        (m, k), n = a0.shape, bcols
    elif mode == "nt":
        (m, k), n = a0.shape, brows
    else:
        (k, m), n = a0.shape, bcols
    tm = _tile(m, MM_TM)
    if mode == "nt":
        tn = _tile(n, MM_TN)
        tk = _chunk_tile(cs, MM_TK) if chunked else _tile(k, MM_TK)
    else:
        tk = _tile(k, MM_TK)
        if chunked:
            tn = _chunk_tile(cs, MM_TN)
        elif out_chunks:
            tn = _chunk_tile(n // out_chunks, MM_TN)
        else:
            tn = _tile(n, MM_TN)
    nk = k // tk
    if mode == "nn":
        a_spec = pl.BlockSpec((tm, tk), lambda i, j, kk: (i, kk))
        if chunked:
            q = cs // tn
            b_spec = pl.BlockSpec((None, tk, tn), lambda i, j, kk: (j // q, kk, j % q))
        else:
            b_spec = pl.BlockSpec((tk, tn), lambda i, j, kk: (kk, j))
        dims = NN
    elif mode == "nt":
        a_spec = pl.BlockSpec((tm, tk), lambda i, j, kk: (i, kk))
        if chunked:
            q = cs // tk
            b_spec = pl.BlockSpec((None, tn, tk), lambda i, j, kk: (kk // q, j, kk % q))
        else:
            b_spec = pl.BlockSpec((tn, tk), lambda i, j, kk: (j, kk))
        dims = NT
    else:
        a_spec = pl.BlockSpec((tk, tm), lambda i, j, kk: (kk, i))
        b_spec = pl.BlockSpec((tk, tn), lambda i, j, kk: (kk, j))
        dims = TN
    if out_chunks:
        qo = (n // out_chunks) // tn
        out_shape = jax.ShapeDtypeStruct((out_chunks, m, n // out_chunks), out_dtype)
        o_spec = pl.BlockSpec((None, tm, tn), lambda i, j, kk: (j // qo, i, j % qo))
    else:
        out_shape = jax.ShapeDtypeStruct((m, n), out_dtype)
        o_spec = pl.BlockSpec((tm, tn), lambda i, j, kk: (i, j))
    npair = len(pairs)

    def body(*refs):
        o_ref, acc = refs[2 * npair], refs[2 * npair + 1]
        kk = pl.program_id(2)

        @pl.when(kk == 0)
        def _():
            acc[...] = jnp.zeros_like(acc)

        part = _dot(refs[0][...], refs[1][...], dims)
        for p in range(1, npair):
            part = part + _dot(refs[2 * p][...], refs[2 * p + 1][...], dims)
        acc[...] += part

        @pl.when(kk == nk - 1)
        def _():
            o_ref[...] = acc[...].astype(out_dtype)

    args = [t for pr in pairs for t in pr]
    res, cres = _call(body, name, [out_shape], (m // tm, n // tn, nk), [a_spec, b_spec] * npair, [o_spec],
                      [pltpu.VMEM((tm, tn), f32)], ("parallel", "parallel", "arbitrary"), args, comm, at)
    return res[0] if comm is None else (res[0], cres)


def _sigmoid(x):
    return 1.0 / (1.0 + jnp.exp(-x))


def _ffn_up(v, wg, wu):
    m, k = v.shape
    nchunk, _, cs = wg.shape
    n = nchunk * cs
    tm, tn = _tile(m, 512), _chunk_tile(cs, 512)
    q = cs // tn
    w_spec = pl.BlockSpec((None, k, tn), lambda j, i: (j // q, 0, j % q))

    def body(v_ref, wg_ref, wu_ref, g_ref, u_ref, act_ref):
        vb = v_ref[...]
        g = _dot(vb, wg_ref[...], NN)
        u = _dot(vb, wu_ref[...], NN)
        sg = _sigmoid(g)
        silu = g * sg
        g_ref[...] = silu.astype(bf16)
        u_ref[...] = (u * (sg * (1.0 + g * (1.0 - sg)))).astype(bf16)
        act_ref[...] = (silu * u).astype(bf16)

    out = jax.ShapeDtypeStruct((m, n), bf16)
    o_spec = pl.BlockSpec((tm, tn), lambda j, i: (i, j))
    return pl.pallas_call(
        body, name="ffn_up", out_shape=(out, out, out), grid=(n // tn, m // tm),
        in_specs=[pl.BlockSpec((tm, k), lambda j, i: (i, 0)), w_spec, w_spec],
        out_specs=(o_spec, o_spec, o_spec),
        compiler_params=_params(("parallel", "parallel")),
    )(v, wg, wu)


def _ffn_down_bwd(dffn, wd, dact_du, dact_dg):
    m, k = dffn.shape
    n = wd.shape[0]
    tm, tn = _tile(m, 1024), _tile(n, 512)

    def body(d_ref, w_ref, fu_ref, fg_ref, dg_ref, du_ref):
        dact = _dot(d_ref[...], w_ref[...], NT)
        du_ref[...] = (dact * fu_ref[...].astype(f32)).astype(bf16)
        dg_ref[...] = (dact * fg_ref[...].astype(f32)).astype(bf16)

    out = jax.ShapeDtypeStruct((m, n), bf16)
    o_spec = pl.BlockSpec((tm, tn), lambda i, j: (i, j))
    return pl.pallas_call(
        body, name="ffn_down_bwd", out_shape=(out, out), grid=(m // tm, n // tn),
        in_specs=[pl.BlockSpec((tm, k), lambda i, j: (i, 0)), pl.BlockSpec((tn, k), lambda i, j: (j, 0)),
                  o_spec, o_spec],
        out_specs=(o_spec, o_spec),
        compiler_params=_params(("parallel", "parallel")),
    )(dffn, wd, dact_du, dact_dg)


def _blocked(fn, grid, ins, outs, accs, name, sp=None, comm=None, at=None):
    n_in, n_out, n_acc = len(ins), len(outs), len(accs)
    nsp = 0 if sp is None else 1

    def body(*refs):
        refs = refs[nsp:]
        tiles = [r[...] for r in refs[:n_in]]
        res = fn(*tiles)
        if not isinstance(res, (tuple, list)):
            res = (res,)
        for r, val in zip(refs[n_in:n_in + n_out], res[:n_out]):
            r[...] = val.astype(r.dtype)
        if n_acc:
            first = pl.program_id(0) == 0
            for d in range(1, len(grid)):
                first = jnp.logical_and(first, pl.program_id(d) == 0)

            @pl.when(first)
            def _():
                for r in refs[n_in + n_out:]:
                    r[...] = jnp.zeros_like(r)

            for r, val in zip(refs[n_in + n_out:], res[n_out:]):
                r[...] += val

    def acc_map(shape):
        zeros = (0,) * len(shape)
        return lambda *idx: zeros

    in_specs = [pl.BlockSpec(bs, im) for _, bs, im in ins]
    out_specs = [pl.BlockSpec(bs, im) for _, _, bs, im in outs] + [pl.BlockSpec(s, acc_map(s)) for s in accs]
    out_shape = [jax.ShapeDtypeStruct(s, d) for s, d, _, _ in outs] + [jax.ShapeDtypeStruct(s, f32) for s in accs]
    sem = ("arbitrary",) * len(grid) if n_acc else ("parallel",) * len(grid)
    args = [a for a, _, _ in ins]
    res, cres = _call(body, name, out_shape, grid, in_specs, out_specs, [], sem, args, comm, at,
                      prefetch=() if sp is None else (sp,))
    return res if comm is None else (res, cres)


def _rows(a, tm, cols=None, cb=0):
    w = a.shape[1] if cols is None else cols
    return (a, (tm, w), lambda i: (i, cb))


def _par(a):
    zeros = (0,) * a.ndim
    return (a, a.shape, lambda i: zeros)


def _rout(t, w, dtype, tm):
    return ((t, w), dtype, (tm, w), lambda i: (i, 0))


def _rowwise(fn, t, tm, ins, outs, accs, name, comm=None, at=None):
    return _blocked(fn, (t // tm,), ins, outs, accs, name, comm=comm, at=at)


def _rms(x, w):
    r = lax.rsqrt(jnp.mean(x * x, axis=-1, keepdims=True) + EPS)
    return x * r * w


def _rms_bwd(x, w, dy):
    r = lax.rsqrt(jnp.mean(x * x, axis=-1, keepdims=True) + EPS)
    xh = x * r
    dyw = dy * w
    dx = r * (dyw - xh * jnp.mean(dyw * xh, axis=-1, keepdims=True))
    return dx, jnp.sum(dy * xh, axis=0, keepdims=True)


def _silu_grad(x):
    s = _sigmoid(x)
    return s * (1.0 + x * (1.0 - s))


def _rope(blk, cosp, sina, sinb):
    return blk * cosp + pltpu.roll(blk, 96, 1) * sina + pltpu.roll(blk, 32, 1) * sinb


def _rope_bwd(dy, cosp, sina, sinb):
    return dy * cosp + pltpu.roll(dy * sina, 32, 1) + pltpu.roll(dy * sinb, 96, 1)


HALO = 8


def _conv_taps(buf, w, tm, base):
    acc = buf[base:base + tm, :] * w[0:1]
    for k in range(1, SSD_K):
        acc = acc + buf[base + k:base + k + tm, :] * w[k:k + 1]
    return acc


def _conv_specs(t, tm, cdim):
    cur = pl.BlockSpec((tm, cdim), lambda i: (i, 0))
    prev = pl.BlockSpec((HALO, cdim), lambda i: (jnp.maximum(i * (tm // HALO) - 1, 0), 0))
    nxt = pl.BlockSpec((HALO, cdim), lambda i: (jnp.minimum((i + 1) * (tm // HALO), t // HALO - 1), 0))
    return cur, prev, nxt


def _conv_fwd(src, w, b, cdim, tm):
    t = src.shape[0]
    cur, prev, _ = _conv_specs(t, tm, cdim)

    def body(x_ref, p_ref, w_ref, b_ref, o_ref, buf):
        buf[0:HALO, :] = jnp.where(pl.program_id(0) > 0, p_ref[...], 0.0)
        buf[HALO:HALO + tm, :] = x_ref[...]
        pre = _conv_taps(buf, w_ref[...], tm, HALO - (SSD_K - 1)) + b_ref[...]
        o_ref[...] = pre * _sigmoid(pre)

    par = lambda a: pl.BlockSpec(a.shape, lambda i: (0, 0))
    return pl.pallas_call(
        body, name="conv_silu", out_shape=jax.ShapeDtypeStruct((t, cdim), f32), grid=(t // tm,),
        in_specs=[cur, prev, par(w), par(b)], out_specs=cur,
        scratch_shapes=[pltpu.VMEM((tm + HALO, cdim), f32)], compiler_params=_params(("parallel",)),
    )(src, src, w, b)


def _conv_bwd_pre(src, w, b, dact, cdim, tm):
    t = src.shape[0]
    cur, prev, _ = _conv_specs(t, tm, cdim)

    def body(x_ref, p_ref, w_ref, b_ref, d_ref, dpre_ref, dw0, dw1, dw2, dw3, db, buf):
        @pl.when(pl.program_id(0) == 0)
        def _():
            for r in (dw0, dw1, dw2, dw3, db):
                r[...] = jnp.zeros_like(r)

        buf[0:HALO, :] = jnp.where(pl.program_id(0) > 0, p_ref[...], 0.0)
        buf[HALO:HALO + tm, :] = x_ref[...]
        base = HALO - (SSD_K - 1)
        pre = _conv_taps(buf, w_ref[...], tm, base) + b_ref[...]
        dpre = d_ref[...] * _silu_grad(pre)
        dpre_ref[...] = dpre
        for k, r in enumerate((dw0, dw1, dw2, dw3)):
            r[...] += jnp.sum(dpre * buf[base + k:base + k + tm, :], axis=0, keepdims=True)
        db[...] += jnp.sum(dpre, axis=0, keepdims=True)

    par = lambda a: pl.BlockSpec(a.shape, lambda i: (0, 0))
    acc = pl.BlockSpec((1, cdim), lambda i: (0, 0))
    acc_shape = jax.ShapeDtypeStruct((1, cdim), f32)
    return pl.pallas_call(
        body, name="conv_silu_bwd", out_shape=[jax.ShapeDtypeStruct((t, cdim), f32)] + [acc_shape] * 5,
        grid=(t // tm,), in_specs=[cur, prev, par(w), par(b), cur], out_specs=[cur] + [acc] * 5,
        scratch_shapes=[pltpu.VMEM((tm + HALO, cdim), f32)], compiler_params=_params(("arbitrary",)),
    )(src, src, w, b, dact)


def _conv_bwd_dx(dpre, w, tm):
    t, cdim = dpre.shape
    cur, _, nxt = _conv_specs(t, tm, cdim)
    last = t // tm - 1

    def body(d_ref, n_ref, w_ref, o_ref, buf):
        buf[0:tm, :] = d_ref[...]
        buf[tm:tm + HALO, :] = jnp.where(pl.program_id(0) < last, n_ref[...], 0.0)
        wv = w_ref[...]
        acc = buf[0:tm, :] * wv[SSD_K - 1:SSD_K]
        for k in range(SSD_K - 1):
            s = SSD_K - 1 - k
            acc = acc + buf[s:s + tm, :] * wv[k:k + 1]
        o_ref[...] = acc.astype(bf16)

    return pl.pallas_call(
        body, name="conv_bwd_dx", out_shape=jax.ShapeDtypeStruct((t, cdim), bf16), grid=(t // tm,),
        in_specs=[cur, nxt, pl.BlockSpec(w.shape, lambda i: (0, 0))], out_specs=cur,
        scratch_shapes=[pltpu.VMEM((tm + HALO, cdim), f32)], compiler_params=_params(("parallel",)),
    )(dpre, dpre, w)


def _causal_mask(s, row0=0):
    row = lax.broadcasted_iota(jnp.int32, s.shape, 0) + row0
    col = lax.broadcasted_iota(jnp.int32, s.shape, 1)
    return jnp.where(row >= col, s, -jnp.inf)


def _causal_pairs(nq, q_major):
    pairs = [(qi, ki) for qi in range(nq) for ki in range(qi + 1)]
    if not q_major:
        pairs.sort(key=lambda p: (p[1], p[0]))
    return (jnp.asarray([p[0] for p in pairs], jnp.int32), jnp.asarray([p[1] for p in pairs], jnp.int32))


def _flash_fwd(q, kv, kr, nheads, scale, comm=None, at=None):
    t = q.shape[0]
    tq = _tile(t, ATT_TILE_FWD)
    nq = t // tq

    qtab, ktab = _causal_pairs(nq, q_major=True)
    hp = 2 if nheads % 2 == 0 else 1

    def body(qt, kt, q_ref, kv_ref, kr_ref, o_ref, lse_ref, m_sc, l_sc, acc_sc):
        qi, ki = qt[pl.program_id(1)], kt[pl.program_id(1)]

        @pl.when(ki == 0)
        def _():
            m_sc[...] = jnp.full_like(m_sc, -jnp.inf)
            l_sc[...] = jnp.zeros_like(l_sc)
            acc_sc[...] = jnp.zeros_like(acc_sc)

        def step(masked, last):
            krb = kr_ref[...]
            ss = []
            for j in range(hp):
                k = jnp.concatenate([kv_ref[:, j * QK_PAD: j * QK_PAD + NOPE], krb], axis=1)
                ss.append(lax.dot_general(q_ref[:, j * QK_PAD:(j + 1) * QK_PAD], k, NT, preferred_element_type=f32))
            soft = []
            for j in range(hp):
                s = ss[j] * scale
                if masked:
                    s = _causal_mask(s)
                m_old = m_sc[j]
                m_new = jnp.maximum(m_old, jnp.max(s, axis=1, keepdims=True))
                alpha = jnp.exp(m_old - m_new)
                p = jnp.exp(s - m_new)
                soft.append((m_new, alpha, alpha * l_sc[j] + jnp.sum(p, axis=1, keepdims=True), p.astype(bf16)))
            for j in range(hp):
                m_new, alpha, l, p = soft[j]
                v = kv_ref[:, j * QK_PAD + NOPE:(j + 1) * QK_PAD]
                acc = alpha * acc_sc[j] + lax.dot_general(p, v, NN, preferred_element_type=f32)
                if last:
                    o_ref[:, j * V_HEAD:(j + 1) * V_HEAD] = acc / l
                    lse_ref[:, j * V_HEAD:(j + 1) * V_HEAD] = jnp.broadcast_to(m_new + jnp.log(l), (tq, V_HEAD))
                else:
                    l_sc[j] = l
                    acc_sc[j] = acc
                    m_sc[j] = m_new

        @pl.when(ki < qi)
        def _():
            step(False, False)

        @pl.when(ki == qi)
        def _():
            step(True, True)

    o_spec = pl.BlockSpec((tq, hp * V_HEAD), lambda h, s, qt, kt: (qt[s], h))
    out = jax.ShapeDtypeStruct((t, nheads * V_HEAD), f32)
    (o, lse), cres = _call(
        body, "flash_fwd", [out, out], (nheads // hp, len(qtab)),
        [pl.BlockSpec((tq, hp * QK_PAD), lambda h, s, qt, kt: (qt[s], h)),
         pl.BlockSpec((tq, hp * QK_PAD), lambda h, s, qt, kt: (kt[s], h)),
         pl.BlockSpec((tq, LANES), lambda h, s, qt, kt: (kt[s], 0))],
        [o_spec, o_spec],
        [pltpu.VMEM((hp, tq, 1), f32), pltpu.VMEM((hp, tq, 1), f32), pltpu.VMEM((hp, tq, V_HEAD), f32)],
        ("parallel", "arbitrary"), [q, kv, kr], comm, at, prefetch=(qtab, ktab))
    return o, lse, cres


def _flash_bwd(q, kv, kr, do, lse, delta, nheads, scale, comm=None, at=None):
    t = q.shape[0]
    tq = _tile(t, ATT_TILE)
    nq = t // tq
    qtab, ktab = _causal_pairs(nq, q_major=False)
    nsub = 2 if tq % 32 == 0 else 1

    def body(qt, kt, q_ref, kn_ref, kr_ref, v_ref, do_ref, lse_ref, dl_ref, dkv_ref, dkr_ref, dq_ref, dk_sc, dv_sc):
        qi, ki = qt[pl.program_id(1)], kt[pl.program_id(1)]

        @pl.when(pl.program_id(1) == 0)
        def _():
            dq_ref[...] = jnp.zeros_like(dq_ref)

        @pl.when(qi == ki)
        def _():
            dk_sc[...] = jnp.zeros_like(dk_sc)
            dv_sc[...] = jnp.zeros_like(dv_sc)

        def step(masked):
            k = jnp.concatenate([kn_ref[...], kr_ref[...]], axis=1)
            vb = v_ref[...]
            parts = [slice(i * (tq // nsub), (i + 1) * (tq // nsub)) for i in range(nsub)]
            qs = [q_ref[r, :] for r in parts]
            dos = [do_ref[r, :] for r in parts]
            ss = [lax.dot_general(qb, k, NT, preferred_element_type=f32) for qb in qs]
            dps = [lax.dot_general(dob, vb, NT, preferred_element_type=f32) for dob in dos]
            ps, dss = [], []
            for r, s, dp in zip(parts, ss, dps):
                s = s * scale
                if masked:
                    s = _causal_mask(s, r.start)
                p = jnp.exp(s - lse_ref[r, 0:1])
                ps.append(p.astype(bf16))
                dss.append((p * (dp - dl_ref[r, 0:1]) * scale).astype(bf16))
            for r, qb, dob, p, ds in zip(parts, qs, dos, ps, dss):
                dv_sc[...] += lax.dot_general(p, dob, TN, preferred_element_type=f32)
                dk_sc[...] += lax.dot_general(ds, qb, TN, preferred_element_type=f32)
                rows = pl.ds(pl.multiple_of(qi * tq + r.start, tq // nsub), tq // nsub)
                dq_ref[rows, :] += lax.dot_general(ds, k, NN, preferred_element_type=f32)

        @pl.when(qi > ki)
        def _():
            step(False)

        @pl.when(qi == ki)
        def _():
            step(True)

        @pl.when(qi == nq - 1)
        def _():
            dk = dk_sc[...]
            dkv_ref[...] = jnp.concatenate([dk[:, :NOPE], dv_sc[...]], axis=1).astype(bf16)
            dkr_ref[...] = dk[:, NOPE:]

    hspec = pl.BlockSpec((tq, V_HEAD), lambda h, s, qt, kt: (qt[s], h))
    (dkv, dkr, dq), cres = _call(
        body, "flash_bwd",
        [jax.ShapeDtypeStruct((t, nheads * QK_PAD), bf16), jax.ShapeDtypeStruct((t, nheads * LANES), f32),
         jax.ShapeDtypeStruct((t, nheads * QK_PAD), f32)],
        (nheads, len(qtab)),
        [pl.BlockSpec((tq, QK_PAD), lambda h, s, qt, kt: (qt[s], h)),
         pl.BlockSpec((tq, NOPE), lambda h, s, qt, kt: (kt[s], 2 * h)),
         pl.BlockSpec((tq, LANES), lambda h, s, qt, kt: (kt[s], 0)),
         pl.BlockSpec((tq, V_HEAD), lambda h, s, qt, kt: (kt[s], 2 * h + 1)),
         hspec, hspec, hspec],
        [pl.BlockSpec((tq, QK_PAD), lambda h, s, qt, kt: (kt[s], h)),
         pl.BlockSpec((tq, LANES), lambda h, s, qt, kt: (kt[s], h)),
         pl.BlockSpec((t, QK_PAD), lambda h, s, qt, kt: (0, h))],
        [pltpu.VMEM((tq, QK_PAD), f32), pltpu.VMEM((tq, V_HEAD), f32)],
        ("parallel", "arbitrary"), [q, kv, kr, kv, do, lse, delta], comm, at, prefetch=(qtab, ktab))
    return dkv, dkr, dq, cres


def _split3(a):
    hi = a.astype(bf16)
    r1 = a - hi.astype(f32)
    mid = r1.astype(bf16)
    lo = (r1 - mid.astype(f32)).astype(bf16)
    return hi, mid, lo


def _split2(a):
    hi = a.astype(bf16)
    return hi, (a - hi.astype(f32)).astype(bf16)


def _ones_dot_left(tri, a):
    hi, mid, lo = _split3(a)
    d = lambda v: lax.dot_general(tri, v, NN, preferred_element_type=f32)
    return d(hi) + d(mid) + d(lo)


def _ones_dot_right(a, tri):
    hi, mid, lo = _split3(a)
    d = lambda v: lax.dot_general(v, tri, NN, preferred_element_type=f32)
    return d(hi) + d(mid) + d(lo)


def _softplus(x):
    return jnp.maximum(x, 0.0) + jnp.log(1.0 + jnp.exp(-jnp.abs(x)))


def _head_spread(hs, width):
    shift = SSD_P.bit_length() - 1
    return (lax.broadcasted_iota(jnp.int32, (hs, width), 0)
            == lax.shift_right_logical(lax.broadcasted_iota(jnp.int32, (hs, width), 1), shift)).astype(bf16)


def _ssd_common(dt_ref, dtT_ref, dtb_ref, dtbT_ref, alog_ref, alogT_ref):
    ii = lax.broadcasted_iota(jnp.int32, (CHUNK, CHUNK), 0)
    jj = lax.broadcasted_iota(jnp.int32, (CHUNK, CHUNK), 1)
    tri = ii >= jj
    raw = dt_ref[...] + dtb_ref[...]
    dt = _softplus(raw)
    a_neg = -jnp.exp(alog_ref[...])
    cum = _ones_dot_left(tri.astype(bf16), dt * a_neg)
    dt_t = _softplus(dtT_ref[...] + dtbT_ref[...])
    cum_t = _ones_dot_right(dt_t * (-jnp.exp(alogT_ref[...])), (ii <= jj).astype(bf16))
    return tri, raw, dt, a_neg, cum, cum_t


def _ssd_fwd(xbc, dt, dt_t, dtb, dtb_t, alog, alog_t, dskip, width, comm=None, at=None):
    t, cdim = xbc.shape
    hs = dt.shape[1]
    nc = t // CHUNK
    epg = hs // SSD_G
    gn = SSD_G * SSD_N

    def body(x_ref, dt_ref, dtT_ref, dtb_ref, dtbT_ref, alog_ref, alogT_ref, d_ref, y_ref, hp_ref, h_sc, yd_sc):
        @pl.when(pl.program_id(0) == 0)
        def _():
            h_sc[...] = jnp.zeros_like(h_sc)

        tri, _, dtv, _, cum, cum_t = _ssd_common(dt_ref, dtT_ref, dtb_ref, dtbT_ref, alog_ref, alogT_ref)
        spread = _head_spread(hs, width)
        clast = cum[CHUNK - 1:CHUNK, :]
        dec = jnp.exp(clast)
        wide = _ones_dot_right(jnp.concatenate([dtv, jnp.exp(cum), jnp.exp(clast - cum),
                                                jnp.broadcast_to(d_ref[...], (CHUNK, hs))], axis=0), spread)
        dt_x, ee_x, ff_x, dsk_x = (wide[i * CHUNK:(i + 1) * CHUNK] for i in range(4))
        xs = x_ref[:, :width]
        xdt = xs * dt_x
        xf = xdt * ff_x
        h_all = h_sc[...]
        hp_ref[0] = h_all
        h_all = h_all.reshape(hs * SSD_P, SSD_N)
        ch_parts = []
        for g in range(SSD_G):
            gsl = slice(g * epg * SSD_P, (g + 1) * epg * SSD_P)
            bb = x_ref[:, width + g * SSD_N: width + (g + 1) * SSD_N].astype(bf16)
            cb_ = x_ref[:, width + gn + g * SSD_N: width + gn + (g + 1) * SSD_N].astype(bf16)
            cbm = lax.dot_general(cb_, bb, NT, preferred_element_type=f32)
            ch_parts.append(_dot(cb_, h_all[gsl], NT))
            st = _dot(xf[:, gsl], bb, TN)
            for e in range(g * epg, (g + 1) * epg):
                esl = slice(e * SSD_P, (e + 1) * SSD_P)
                lmat = jnp.exp(jnp.where(tri, cum[:, e:e + 1] - cum_t[e:e + 1, :], -jnp.inf))
                yd_sc[:, esl] = _dot(cbm * lmat, xdt[:, esl], NN)
                j = e - g * epg
                h_sc[e] = h_sc[e] * dec[:, e:e + 1] + st[j * SSD_P:(j + 1) * SSD_P, :]
        y_ref[...] = yd_sc[...] + jnp.concatenate(ch_parts, axis=1) * ee_x + xs * dsk_x

    par = lambda a: pl.BlockSpec(a.shape, lambda i: (0,) * a.ndim)
    (y, hp), cres = _call(
        body, "ssd_fwd",
        [jax.ShapeDtypeStruct((t, width), f32), jax.ShapeDtypeStruct((nc, hs, SSD_P, SSD_N), f32)], (nc,),
        [pl.BlockSpec((CHUNK, cdim), lambda i: (i, 0)), pl.BlockSpec((CHUNK, hs), lambda i: (i, 0)),
         pl.BlockSpec((hs, CHUNK), lambda i: (0, i)), par(dtb), par(dtb_t), par(alog), par(alog_t), par(dskip)],
        [pl.BlockSpec((CHUNK, width), lambda i: (i, 0)), pl.BlockSpec((1, hs, SSD_P, SSD_N), lambda i: (i, 0, 0, 0))],
        [pltpu.VMEM((hs, SSD_P, SSD_N), f32), pltpu.VMEM((CHUNK, width), f32)], ("arbitrary",),
        [xbc, dt, dt_t, dtb, dtb_t, alog, alog_t, dskip], comm, at)
    return y, hp, cres


def _ssd_bwd(xbc, dt, dt_t, dtb, dtb_t, alog, alog_t, dskip, hprev, dy, width, comm=None, at=None):
    t, cdim = xbc.shape
    hs = dt.shape[1]
    nc = t // CHUNK
    epg = hs // SSD_G
    gn = SSD_G * SSD_N

    def body(x_ref, dt_ref, dtT_ref, dtb_ref, dtbT_ref, alog_ref, alogT_ref, d_ref, hp_ref, dy_ref,
             dx_ref, ddt_ref, dalog_ref, ddsk_ref, ddtb_ref, dh_sc, dxd_sc):
        @pl.when(pl.program_id(0) == 0)
        def _():
            dh_sc[...] = jnp.zeros_like(dh_sc)
            dalog_ref[...] = jnp.zeros_like(dalog_ref)
            ddsk_ref[...] = jnp.zeros_like(ddsk_ref)
            ddtb_ref[...] = jnp.zeros_like(ddtb_ref)

        tri, raw, dtv, a_neg, cum, cum_t = _ssd_common(dt_ref, dtT_ref, dtb_ref, dtbT_ref, alog_ref, alogT_ref)
        dsk = d_ref[...]
        head_row = lax.broadcasted_iota(jnp.int32, (1, hs), 1)
        head_col = lax.broadcasted_iota(jnp.int32, (hs, 1), 0)
        last_row = (lax.broadcasted_iota(jnp.int32, (CHUNK, 1), 0) == CHUNK - 1).astype(f32)
        shift = SSD_P.bit_length() - 1
        spread = _head_spread(hs, width)
        gather = (lax.shift_right_logical(lax.broadcasted_iota(jnp.int32, (width, hs), 0), shift)
                  == lax.broadcasted_iota(jnp.int32, (width, hs), 1)).astype(bf16)
        clast = cum[CHUNK - 1:CHUNK, :]
        ee = jnp.exp(cum)
        ff = jnp.exp(clast - cum)
        dec = jnp.exp(clast)
        wide = _ones_dot_right(jnp.concatenate([dtv, ee, ff, jnp.broadcast_to(dsk, (CHUNK, hs))], axis=0), spread)
        dt_x, ee_x, ff_x, dsk_x = (wide[i * CHUNK:(i + 1) * CHUNK] for i in range(4))
        xs = x_ref[:, :width]
        dyv = dy_ref[...]
        xdt = xs * dt_x
        dye = dyv * ee_x
        xf = xdt * ff_x
        h_all = hp_ref[0].reshape(hs * SSD_P, SSD_N)
        dh_all = dh_sc[...].reshape(hs * SSD_P, SSD_N)
        hi, lo = _split2(_ones_dot_left(spread, dh_all * h_all))
        ones8 = jnp.ones((8, SSD_N), bf16)
        hh = (lax.dot_general(ones8, hi, NT, preferred_element_type=f32)
              + lax.dot_general(ones8, lo, NT, preferred_element_type=f32))[0:1]
        rowsum_m = jnp.zeros((CHUNK, hs), f32)
        colsum_m = jnp.zeros((hs, CHUNK), f32)
        ch_parts, bds_parts = [], []
        for g in range(SSD_G):
            bsl = slice(width + g * SSD_N, width + (g + 1) * SSD_N)
            csl = slice(width + gn + g * SSD_N, width + gn + (g + 1) * SSD_N)
            gsl = slice(g * epg * SSD_P, (g + 1) * epg * SSD_P)
            bb = x_ref[:, bsl].astype(bf16)
            cb_ = x_ref[:, csl].astype(bf16)
            cbm = lax.dot_general(cb_, bb, NT, preferred_element_type=f32)
            hg = h_all[gsl].astype(bf16)
            dhg = dh_all[gsl].astype(bf16)
            ch_parts.append(_dot(cb_, hg, NT))
            bds_parts.append(_dot(bb, dhg, NT))
            dcg = _dot(dye[:, gsl], hg, NN)
            dbg = _dot(xf[:, gsl], dhg, NN)
            dh_new = _dot(dye[:, gsl], cb_, TN)
            dcb = jnp.zeros((CHUNK, CHUNK), f32)
            for e in range(g * epg, (g + 1) * epg):
                esl = slice(e * SSD_P, (e + 1) * SSD_P)
                lmat = jnp.exp(jnp.where(tri, cum[:, e:e + 1] - cum_t[e:e + 1, :], -jnp.inf))
                gmat = cbm * lmat
                dy_e = dyv[:, esl].astype(bf16)
                dgm = _dot(dy_e, xdt[:, esl], NT)
                dxd_sc[:, esl] = _dot(gmat, dy_e, TN)
                dcb = dcb + dgm * lmat
                mm = dgm * gmat
                rowsum_m = rowsum_m + jnp.sum(mm, axis=1, keepdims=True) * (head_row == e).astype(f32)
                colsum_m = colsum_m + (head_col == e).astype(f32) * jnp.sum(mm, axis=0, keepdims=True)
                j = e - g * epg
                dh_sc[e] = dh_new[j * SSD_P:(j + 1) * SSD_P, :] + dec[:, e:e + 1] * dh_sc[e]
            dx_ref[:, bsl] = dbg + _dot(dcb, cb_, TN)
            dx_ref[:, csl] = dcg + _dot(dcb, bb, NN)
        ch_all = jnp.concatenate(ch_parts, axis=1)
        bds_all = jnp.concatenate(bds_parts, axis=1)
        dxdt = bds_all * ff_x + dxd_sc[...]
        dx_ref[:, :width] = dxdt * dt_x + dyv * dsk_x
        sums = _ones_dot_right(jnp.concatenate([dxdt * xs, dyv * ch_all, bds_all * xdt, dyv * xs], axis=0), gather)
        ddtx_all = sums[0:CHUNK]
        dff = sums[2 * CHUNK:3 * CHUNK] * ff
        ddsk = jnp.sum(sums[3 * CHUNK:4 * CHUNK], axis=0, keepdims=True)
        dclast = jnp.sum(dff, axis=0, keepdims=True) + dec * hh
        eye = (lax.broadcasted_iota(jnp.int32, (hs, hs), 0) == lax.broadcasted_iota(jnp.int32, (hs, hs), 1)).astype(bf16)
        colsum_t = sum(lax.dot_general(v, eye, TN, preferred_element_type=f32) for v in _split3(colsum_m))
        dcum_all = sums[CHUNK:2 * CHUNK] * ee - dff + rowsum_m - colsum_t + dclast * last_row
        ii = lax.broadcasted_iota(jnp.int32, (CHUNK, CHUNK), 0)
        jj = lax.broadcasted_iota(jnp.int32, (CHUNK, CHUNK), 1)
        da = _ones_dot_left((jj >= ii).astype(bf16), dcum_all)
        ddt = da * a_neg + ddtx_all
        dalog_ref[...] += jnp.sum(da * dtv, axis=0, keepdims=True) * a_neg
        draw = ddt * _sigmoid(raw)
        place = (lax.broadcasted_iota(jnp.int32, (hs, LANES), 0) + ROPE
                 == lax.broadcasted_iota(jnp.int32, (hs, LANES), 1)).astype(bf16)
        ddt_ref[...] = _ones_dot_right(draw, place)
        ddtb_ref[...] += jnp.sum(draw, axis=0, keepdims=True)
        ddsk_ref[...] += ddsk

    rev = lambda i: nc - 1 - i
    par = lambda a: pl.BlockSpec(a.shape, lambda i: (0,) * a.ndim)
    acc = pl.BlockSpec((1, hs), lambda i: (0, 0))
    acc_shape = jax.ShapeDtypeStruct((1, hs), f32)
    res, cres = _call(
        body, "ssd_bwd",
        [jax.ShapeDtypeStruct((t, cdim), f32), jax.ShapeDtypeStruct((t, LANES), f32), acc_shape, acc_shape, acc_shape],
        (nc,),
        [pl.BlockSpec((CHUNK, cdim), lambda i: (rev(i), 0)), pl.BlockSpec((CHUNK, hs), lambda i: (rev(i), 0)),
         pl.BlockSpec((hs, CHUNK), lambda i: (0, rev(i))), par(dtb), par(dtb_t), par(alog), par(alog_t),
         par(dskip), pl.BlockSpec((1, hs, SSD_P, SSD_N), lambda i: (rev(i), 0, 0, 0)),
         pl.BlockSpec((CHUNK, width), lambda i: (rev(i), 0))],
        [pl.BlockSpec((CHUNK, cdim), lambda i: (rev(i), 0)), pl.BlockSpec((CHUNK, LANES), lambda i: (rev(i), 0)),
         acc, acc, acc],
        [pltpu.VMEM((hs, SSD_P, SSD_N), f32), pltpu.VMEM((CHUNK, width), f32)], ("arbitrary",),
        [xbc, dt, dt_t, dtb, dtb_t, alog, alog_t, dskip, hprev, dy], comm, at)
    return (*res, cres)


def _where_am_i():
    x, y, c = lax.axis_index("x"), lax.axis_index("y"), lax.axis_index("c")
    chips = [(1 - x, y), (x, 1 - y), (1 - x, 1 - y)]
    return x, y, c, chips


def _remote(src, dst, send_sems, recv_sems, k, to):
    return pltpu.make_async_remote_copy(src_ref=src, dst_ref=dst, send_sem=send_sems.at[k], recv_sem=recv_sems.at[k],
                                        device_id=to, device_id_type=MESH)


class _Comm:
    def __init__(self, args, outs, nsem, nphase, run):
        self.args, self.outs, self.nsem, self.nphase, self.run = list(args), list(outs), nsem, nphase, run

    def sems(self):
        return [pltpu.SemaphoreType.DMA((self.nsem,)), pltpu.SemaphoreType.DMA((self.nsem,))]


class _SemWindow:
    def __init__(self, ref, off):
        self.ref, self.off, self.at = ref, off, self

    def __getitem__(self, k):
        return self.ref.at[k + self.off]


def _both(a, b):
    assert a.nphase == b.nphase
    na, no = len(a.args), len(a.outs)

    def run(phase, srcs, outs, send_sems, recv_sems):
        a.run(phase, srcs[:na], outs[:no], send_sems, recv_sems)
        b.run(phase, srcs[na:], outs[no:], _SemWindow(send_sems, a.nsem), _SemWindow(recv_sems, a.nsem))

    return _Comm(a.args + b.args, a.outs + b.outs, a.nsem + b.nsem, a.nphase, run)


def _comm_call(comm, name):
    na, no = len(comm.args), len(comm.outs)

    def body(*refs):
        for phase in range(comm.nphase):
            comm.run(phase, refs[:na], refs[na:na + no], refs[na + no], refs[na + no + 1])

    return list(pl.pallas_call(body, name=name, out_shape=comm.outs, in_specs=[ANY] * na, out_specs=[ANY] * no,
                               scratch_shapes=comm.sems())(*comm.args))


def _half(ref, c, r2):
    return ref.at[pl.ds(c * r2, r2)]


def _gather_weights(shards, wholes):
    ns, nw = len(shards), len(wholes)
    per = 7

    def run(phase, srcs, outs, send_sems, recv_sems):
        x, y, c, chips = _where_am_i()
        me, sib = 2 * x + y, (x, y, 1 - c)
        def first():
            cps = []
            for w in range(ns + nw):
                src, out, base = srcs[w], outs[w], per * w
                halved = w < ns
                r2 = src.shape[0] // 2
                piece = _half(src, c, r2) if halved else src
                for j, (cx, cy) in enumerate(chips):
                    dst = _half(out.at[me], c, r2) if halved else out.at[me]
                    cps.append(_remote(piece, dst, send_sems, recv_sems, base + j, (cx, cy, c)))
                cps.append(_remote(src, out.at[me], send_sems, recv_sems, base + 6, sib))
            return cps

        def passed():
            cps = []
            for w in range(ns):
                r2 = srcs[w].shape[0] // 2
                for j, (cx, cy) in enumerate(chips):
                    got = _half(outs[w].at[2 * cx + cy], c, r2)
                    cps.append(_remote(got, got, send_sems, recv_sems, per * w + 3 + j, sib))
            return cps

        if phase == 0:
            for cp in first():
                cp.start()
        elif phase == 1:
            it = iter(passed())
            for w in range(ns + nw):
                src, out, base = srcs[w], outs[w], per * w
                r2 = src.shape[0] // 2
                for j, (cx, cy) in enumerate(chips):
                    got = _half(out.at[2 * cx + cy], c, r2) if w < ns else out.at[2 * cx + cy]
                    _remote(got, got, send_sems, recv_sems, base + j, sib).wait_recv()
                    if w < ns:
                        next(it).start()
        else:
            for w in range(ns + nw):
                src, out, base = srcs[w], outs[w], per * w
                r2 = src.shape[0] // 2
                if w < ns:
                    for j, (cx, cy) in enumerate(chips):
                        got = _half(out.at[2 * cx + cy], 1 - c, r2)
                        _remote(got, got, send_sems, recv_sems, base + 3 + j, sib).wait_recv()
                _remote(src, out.at[me], send_sems, recv_sems, base + 6, sib).wait_recv()
            for cp in first() + passed():
                cp.wait_send()

    args = list(shards) + list(wholes)
    outs = [jax.ShapeDtypeStruct((4,) + a.shape, a.dtype) for a in args]
    return _Comm(args, outs, per * len(args), 3, run)


def _sibling_send_halves(gs):
    n = len(gs)

    def run(phase, srcs, outs, send_sems, recv_sems):
        x, y, c, _ = _where_am_i()
        sib = (x, y, 1 - c)
        cps = []
        for w in range(n):
            r2 = srcs[w].shape[1] // 2
            for k in range(4):
                cps.append(_remote(_half(srcs[w].at[k], 1 - c, r2), outs[w].at[k], send_sems, recv_sems, 4 * w + k, sib))
        for cp in cps:
            cp.start() if phase == 0 else cp.wait()

    outs = [jax.ShapeDtypeStruct((4, g.shape[1] // 2, g.shape[2]), g.dtype) for g in gs]
    return _Comm(gs, outs, 4 * n, 2, run)


def _chips_exchange(ps):
    n = len(ps)

    def run(phase, srcs, outs, send_sems, recv_sems):
        x, y, c, chips = _where_am_i()
        me = 2 * x + y
        cps = []
        for w in range(n):
            for j, (cx, cy) in enumerate(chips):
                cps.append(_remote(srcs[w].at[2 * cx + cy], outs[w].at[me], send_sems, recv_sems, 3 * w + j, (cx, cy, c)))
        if phase == 0:
            for cp in cps:
                cp.start()
        else:
            for w in range(n):
                for j, (cx, cy) in enumerate(chips):
                    got = outs[w].at[2 * cx + cy]
                    _remote(got, got, send_sems, recv_sems, 3 * w + j, (cx, cy, c)).wait_recv()
            for cp in cps:
                cp.wait_send()

    outs = [jax.ShapeDtypeStruct(p.shape, p.dtype) for p in ps]
    return _Comm(ps, outs, 3 * n, 2, run)


def _sibling_swap(rs):
    n = len(rs)

    def run(phase, srcs, outs, send_sems, recv_sems):
        x, y, c, _ = _where_am_i()
        for w in range(n):
            cp = _remote(srcs[w], outs[w], send_sems, recv_sems, w, (x, y, 1 - c))
            cp.start() if phase == 0 else cp.wait()

    outs = [jax.ShapeDtypeStruct(r.shape, r.dtype) for r in rs]
    return _Comm(rs, outs, n, 2, run)


def _gather_all(v):
    rows = v.shape[0]

    def body(x_ref, out_ref, send_sems, recv_sems, local_sem):
        x, y, c, chips = _where_am_i()
        me, sib = (x, y, c), (x, y, 1 - c)
        blk = lambda px, py, pc: out_ref.at[4 * px + 2 * py + pc]
        mine = pltpu.make_async_copy(x_ref, blk(*me), local_sem)
        mine.start()
        first = [_remote(x_ref, blk(*me), send_sems, recv_sems, 0, sib)]
        first += [_remote(x_ref, blk(*me), send_sems, recv_sems, 1 + j, (*chip, c)) for j, chip in enumerate(chips)]
        for cp in first:
            cp.start()
        passed = [_remote(blk(*chip, c), blk(*chip, c), send_sems, recv_sems, 4 + j, sib) for j, chip in enumerate(chips)]
        for j, chip in enumerate(chips):
            _remote(blk(*chip, c), blk(*chip, c), send_sems, recv_sems, 1 + j, me).wait_recv()
            passed[j].start()
        _remote(blk(*sib), blk(*sib), send_sems, recv_sems, 0, me).wait_recv()
        for j, chip in enumerate(chips):
            _remote(blk(*chip, 1 - c), blk(*chip, 1 - c), send_sems, recv_sems, 4 + j, me).wait_recv()
        for cp in first + passed:
            cp.wait_send()
        mine.wait()

    vm = pl.BlockSpec(memory_space=pltpu.VMEM)
    return pl.pallas_call(
        body, name="gather_all", out_shape=jax.ShapeDtypeStruct((8, rows, LANES), v.dtype),
        in_specs=[vm], out_specs=vm,
        scratch_shapes=[pltpu.SemaphoreType.DMA((7,)), pltpu.SemaphoreType.DMA((7,)), pltpu.SemaphoreType.DMA],
    )(v)


def _flat_pad(parts, total):
    v = jnp.concatenate([p.reshape(-1) for p in parts])
    return jnp.pad(v, (0, total - v.shape[0]))


def _adamw(w, g, m, v):
    m = ADAM_B1 * m + (1.0 - ADAM_B1) * g
    v = ADAM_B2 * v + (1.0 - ADAM_B2) * jnp.square(g)
    m_hat = m / (1.0 - ADAM_B1 ** ADAM_STEP)
    v_hat = v / (1.0 - ADAM_B2 ** ADAM_STEP)
    delta = -ADAM_LR * (m_hat / (jnp.sqrt(v_hat) + ADAM_EPS) + ADAM_WD * w)
    return delta, m, v


def _adamw_call(w, g, m, v, name):
    r, cdim = w.shape
    tm = _tile(r, ROW_TILE, 8)
    o = _rout(r, cdim, f32, tm)
    return _rowwise(_adamw, r, tm, [_rows(w, tm), _rows(g, tm), _rows(m, tm), _rows(v, tm)], [o, o, o], [], name)


def _adamw_halves(w, m, v, g_mine, g_sib, sp, name):
    _, r, cdim = w.shape
    r2 = r // 2
    tr = _row_tile(r2, cdim, 8)
    nb = r2 // tr

    def body(sp_ref, w_ref, m_ref, v_ref, ga_ref, gb_ref, g_out, d_out, m_out, v_out):
        g = jnp.where(pl.program_id(0) == sp_ref[1], ga_ref[...], gb_ref[...])
        delta, mn, vn = _adamw(w_ref[...], g, m_ref[...], v_ref[...])
        g_out[...] = g
        d_out[...] = delta
        m_out[...] = mn
        v_out[...] = vn

    full = pl.BlockSpec((None, tr, cdim), lambda h, i, s: (0, h * nb + i, 0))
    mine = pl.BlockSpec((tr, cdim), lambda h, i, s: (jnp.where(h == s[1], i, 0), 0))
    sib = pl.BlockSpec((tr, cdim), lambda h, i, s: (jnp.where(h == s[1], 0, i), 0))
    out = jax.ShapeDtypeStruct((1, r, cdim), f32)
    gs = pltpu.PrefetchScalarGridSpec(num_scalar_prefetch=1, grid=(2, nb), in_specs=[full, full, full, mine, sib],
                                      out_specs=[full, full, full, full])
    return pl.pallas_call(body, name=name, out_shape=[out, out, out, out], grid_spec=gs,
                          compiler_params=_params(("parallel", "parallel")))(sp, w, m, v, g_mine, g_sib)


MIXER = ("w_in", "w_uq", "w_ukv", "w_out")
FFN = ("w_gate", "w_up", "w_down")
BIG = MIXER + FFN
SMALL = ("q_norm_w", "kv_norm_w", "conv_w", "conv_b", "dt_bias", "a_log", "d_skip", "ssd_norm_w", "attn_out_norm_w",
         "pre_mix_norm_w", "post_mix_norm_w", "pre_ffn_norm_w", "post_ffn_norm_w")
ORDER = ("w_in", "q_norm_w", "w_uq", "kv_norm_w", "w_ukv", "conv_w", "conv_b", "dt_bias", "a_log", "d_skip",
         "ssd_norm_w", "attn_out_norm_w", "w_out", "pre_mix_norm_w", "post_mix_norm_w", "pre_ffn_norm_w",
         "post_ffn_norm_w", "w_gate", "w_up", "w_down")


def kernel(x, positions, w_in, q_norm_w, w_uq, kv_norm_w, w_ukv, conv_w, conv_b, dt_bias, a_log, d_skip, ssd_norm_w, attn_out_norm_w, w_out, pre_mix_norm_w, post_mix_norm_w, pre_ffn_norm_w, post_ffn_norm_w, w_gate, w_up, w_down, loss_target, m_w_in, m_q_norm_w, m_w_uq, m_kv_norm_w, m_w_ukv, m_conv_w, m_conv_b, m_dt_bias, m_a_log, m_d_skip, m_ssd_norm_w, m_attn_out_norm_w, m_w_out, m_pre_mix_norm_w, m_post_mix_norm_w, m_pre_ffn_norm_w, m_post_ffn_norm_w, m_w_gate, m_w_up, m_w_down, v_w_in, v_q_norm_w, v_w_uq, v_kv_norm_w, v_w_ukv, v_conv_w, v_conv_b, v_dt_bias, v_a_log, v_d_skip, v_ssd_norm_w, v_attn_out_norm_w, v_w_out, v_pre_mix_norm_w, v_post_mix_norm_w, v_pre_ffn_norm_w, v_post_ffn_norm_w, v_w_gate, v_w_up, v_w_down):
    local = dict(locals())
    wts = {n: local[n][0] for n in ORDER}
    mom_m = {n: local["m_" + n][0] for n in ORDER}
    mom_v = {n: local["v_" + n][0] for n in ORDER}
    xs = x[0]
    tgt = loss_target[0]
    t, d = xs.shape
    nchip = 4
    my_x, my_y, my_c = lax.axis_index("x"), lax.axis_index("y"), lax.axis_index("c")
    my_chip = 2 * my_x + my_y

    mla_w = d // 2
    nh = mla_w // V_HEAD
    width = d - mla_w
    hs = width // SSD_P
    gn = SSD_G * SSD_N
    cdim = width + 2 * gn
    in_sizes = (Q_RANK, KV_RANK, ROPE, width, cdim, hs)
    d_in = sum(in_sizes)
    tail = LANES
    off_xbc = 0
    off_cq = cdim
    off_ckv = off_cq + Q_RANK
    off_z = off_ckv + KV_RANK
    off_tail = off_z + width
    d_in_p = _round_up(off_tail + tail, 256)
    gw = width // SSD_G
    assert off_cq % Q_RANK == 0 and off_ckv % KV_RANK == 0 and off_z % gw == 0 and off_tail % LANES == 0
    qk_head = NOPE + ROPE
    scale = qk_head ** -0.5
    tm = _tile(t, ROW_TILE, 8)
    tmw = _tile(t, ROW_TILE // 2, 8)

    sp = jnp.stack([my_chip, my_c]).astype(jnp.int32)

    def pair_sums(names, gl, from_sib):
        res = []
        for n, g, fs in zip(names, gl, from_sib):
            _, r2, cs = fs.shape
            tr = _row_tile(r2, cs, 16)
            nb = r2 // tr
            res.append(_blocked(
                lambda a, b: a + b, (nchip, nb),
                [(g, (None, tr, cs), lambda k, i, s, nb=nb: (k, s[1] * nb + i, 0)),
                 (fs, (None, tr, cs), lambda k, i, s: (k, i, 0))],
                [((nchip, r2, cs), bf16, (None, tr, cs), lambda k, i, s: (k, i, 0))], [], "pair_sum_" + n, sp=sp)[0])
        return res

    def chip_sums(names, gl, from_sib, from_chips):
        res = []
        for n, g, fs, fc in zip(names, gl, from_sib, from_chips):
            _, r2, cs = fs.shape
            tr = _row_tile(r2, cs, 16)
            nb = r2 // tr
            res.append(_blocked(
                lambda a, b, r1, r2_, r3: ((a + b) + r1.astype(f32)) + r2_.astype(f32) + r3.astype(f32), (nb,),
                [(g, (None, tr, cs), lambda i, s, nb=nb: (s[0], s[1] * nb + i, 0)),
                 (fs, (None, tr, cs), lambda i, s: (s[0], i, 0)),
                 (fc, (None, tr, cs), lambda i, s: (s[0] ^ 1, i, 0)),
                 (fc, (None, tr, cs), lambda i, s: (s[0] ^ 2, i, 0)),
                 (fc, (None, tr, cs), lambda i, s: (s[0] ^ 3, i, 0))],
                [((r2, cs), f32, (tr, cs), lambda i, s: (i, 0))], [], "chip_sum_" + n, sp=sp)[0])
        return res

    ck, ccs = wts["conv_w"].shape
    cs_in = wts["w_in"].shape[1]
    flat_t = lambda a3: jnp.swapaxes(a3, 1, 2).reshape(1, cs_in * d // LANES, LANES)
    (u,), (w_in_g, conv_g) = _rowwise(
        lambda a, w: _rms(a, w), t, tm, [_rows(xs, tm), _par(wts["pre_mix_norm_w"].reshape(1, -1))],
        [_rout(t, d, bf16, tm)], [], "pre_mix_norm",
        comm=_gather_weights([flat_t(w_in)[0].astype(bf16)], [wts["conv_w"]]), at=(0.0, 0.5))
    cat_cols = lambda g: jnp.concatenate([g[k] for k in range(nchip)], axis=1)
    conv_full = cat_cols(conv_g)
    mixer_gather = _gather_weights([wts[n].astype(bf16) for n in MIXER[1:]], [])
    ffn_gather = _gather_weights([wts[n].astype(bf16) for n in FFN[:2]], [])
    w_down_b = wts["w_down"].astype(bf16)
    dn2 = w_down_b.shape[0] // 2
    down_gathers = [_gather_weights([w_down_b[:dn2]], []), _gather_weights([w_down_b[dn2:]], [])]

    wi = w_in_g.reshape(nchip * cs_in, d)
    o = np.cumsum((0,) + in_sizes)
    seg = lambda i: wi[o[i]:o[i + 1]]
    w_in_pt = jnp.concatenate([seg(4), seg(0), seg(1), seg(3), seg(2), seg(5),
                               jnp.zeros((d_in_p - off_tail - ROPE - hs, d), bf16)], axis=0)

    inv_freq = ROPE_THETA ** (-jnp.arange(0, ROPE, 2, dtype=f32) / ROPE)
    ang = positions[0].astype(f32)[:, None] * inv_freq
    cos, sin = jnp.cos(ang), jnp.sin(ang)
    z32, z64, z96 = jnp.zeros((t, 32), f32), jnp.zeros((t, 64), f32), jnp.zeros((t, 96), f32)
    cosp = jnp.concatenate([cos, cos, z64], axis=1)
    sina = jnp.concatenate([-sin, z96], axis=1)
    sinb = jnp.concatenate([z32, sin, z64], axis=1)

    row = lambda a: a.reshape(1, -1)
    w_pre_mix, w_post_mix = row(wts["pre_mix_norm_w"]), row(wts["post_mix_norm_w"])
    w_pre_ffn, w_post_ffn = row(wts["pre_ffn_norm_w"]), row(wts["post_ffn_norm_w"])
    w_qn, w_kvn = row(wts["q_norm_w"]), row(wts["kv_norm_w"])
    w_attn_n, w_ssd_n = row(wts["attn_out_norm_w"]), row(wts["ssd_norm_w"])
    conv_b_r = row(wts["conv_b"])
    dtb, alog, dskip = row(wts["dt_bias"]), row(wts["a_log"]), row(wts["d_skip"])
    dtb_t, alog_t = dtb.reshape(hs, 1), alog.reshape(hs, 1)

    proj, (w_uq_g, w_ukv_f, w_out_g) = _matmul([(u, w_in_pt)], "nt", f32, "in_proj", comm=mixer_gather,
                                               at=(0.0, 0.6))
    w_uq_p = jnp.pad(cat_cols(w_uq_g).reshape(Q_RANK, nh, qk_head), ((0, 0), (0, 0), (0, QK_PAD - qk_head))
                     ).reshape(Q_RANK, nh * QK_PAD)
    w_out_f = w_out_g.reshape(-1, w_out_g.shape[2])
    cq_in = _rows(proj, tm, Q_RANK, off_cq // Q_RANK)
    ckv_in = _rows(proj, tm, KV_RANK, off_ckv // KV_RANK)
    cqn = _rowwise(lambda a, w: _rms(a, w), t, tm, [cq_in, _par(w_qn)], [_rout(t, Q_RANK, bf16, tm)], [], "q_norm")[0]
    ckvn = _rowwise(lambda a, w: _rms(a, w), t, tm, [ckv_in, _par(w_kvn)], [_rout(t, KV_RANK, bf16, tm)], [],
                    "kv_norm")[0]
    q_raw = _matmul([(cqn, w_uq_p)], "nn", f32, "q_up")
    kv = _matmul([(ckvn, w_ukv_f)], "nn", bf16, "kv_up")

    def q_rope_fn(qt, cp, sa, sb):
        parts = []
        for h in range(nh):
            parts.append(qt[:, h * QK_PAD: h * QK_PAD + NOPE])
            parts.append(_rope(qt[:, h * QK_PAD + NOPE:(h + 1) * QK_PAD], cp, sa, sb))
        return jnp.concatenate(parts, axis=1)

    tail_cb = off_tail // LANES
    q2 = _rowwise(q_rope_fn, t, tm, [_rows(q_raw, tm), _rows(cosp, tm), _rows(sina, tm), _rows(sinb, tm)],
                  [_rout(t, nh * QK_PAD, bf16, tm)], [], "q_rope")[0]
    kr2 = _rowwise(_rope, t, tm, [_rows(proj, tm, LANES, tail_cb), _rows(cosp, tm), _rows(sina, tm), _rows(sinb, tm)],
                   [_rout(t, LANES, bf16, tm)], [], "k_rope")[0]
    o_att, lse, ffn_w = _flash_fwd(q2, kv, kr2, nh, scale, ffn_gather, (0.0, 0.75))
    w_gate_f, w_up_f = ffn_w

    xbc_act = _conv_fwd(proj, conv_full, conv_b_r, cdim, tm)
    dt_raw, dt_raw_t = _blocked(
        lambda blk: (blk[:, ROPE:ROPE + hs], blk.T[ROPE:ROPE + hs, :]), (t // tm,),
        [_rows(proj, tm, LANES, tail_cb)],
        [((t, hs), f32, (tm, hs), lambda i: (i, 0)), ((hs, t), f32, (hs, tm), lambda i: (0, i))], [], "dt_split")
    y_ssd, hprev, (down_a,) = _ssd_fwd(xbc_act, dt_raw, dt_raw_t, dtb, dtb_t, alog, alog_t, dskip, width,
                                       down_gathers[0], (0.0, 0.6))
    z_ins = [_rows(proj, tm, gw, off_z // gw + i) for i in range(SSD_G)]

    def mix_norms_fn(ov, yv, *rest):
        zs, wa, ws = rest[:SSD_G], rest[SSD_G], rest[SSD_G + 1]
        outs = [_rms(ov, wa)]
        for i in range(SSD_G):
            sl = slice(i * gw, (i + 1) * gw)
            outs.append(_rms(yv[:, sl] * (zs[i] * _sigmoid(zs[i])), ws[:, sl]))
        return jnp.concatenate(outs, axis=1)

    cat = _rowwise(mix_norms_fn, t, tm, [_rows(o_att, tm), _rows(y_ssd, tm)] + z_ins + [_par(w_attn_n), _par(w_ssd_n)],
                   [_rout(t, d, bf16, tm)], [], "attn_ssd_out_norms")[0]
    mix, (down_b,) = _matmul([(cat, w_out_f)], "nn", f32, "out_proj", comm=down_gathers[1], at=(0.0, 0.6))
    w_down_f = jnp.concatenate([down_a, down_b], axis=1).reshape(-1, down_a.shape[2])

    def post_mix_fn(mx, xv, w1, w2):
        h1v = xv + _rms(mx, w1)
        return h1v, _rms(h1v, w2)

    h1, v_in = _rowwise(post_mix_fn, t, tmw, [_rows(mix, tmw), _rows(xs, tmw), _par(w_post_mix), _par(w_pre_ffn)],
                        [_rout(t, d, f32, tmw), _rout(t, d, bf16, tmw)], [], "post_mix_pre_ffn_norm")
    g_ff, u_ff, act = _ffn_up(v_in, w_gate_f, w_up_f)
    ffn = _matmul([(act, w_down_f)], "nn", f32, "ffn_down")

    def final_fn(fv, h1v, tg, w):
        h2 = h1v + _rms(fv, w)
        err = h2 - tg
        lpart = 0.5 * jnp.sum(jnp.sum(err * err, axis=1, keepdims=True), axis=0, keepdims=True) / d
        dh2 = err / d
        dff, dw = _rms_bwd(fv, w, dh2)
        return dff, dh2, jnp.broadcast_to(lpart, (1, LANES)), dw

    dffn, dh2, loss_acc, g_post_ffn = _rowwise(
        final_fn, t, tmw, [_rows(ffn, tmw), _rows(h1, tmw), _rows(tgt, tmw), _par(w_post_ffn)],
        [_rout(t, d, bf16, tmw), _rout(t, d, f32, tmw)], [(1, LANES), (1, d)], "loss_post_ffn_norm_bwd")
    loss = lax.psum(loss_acc[0, 0], ("x", "y", "c"))

    dg_ff, du_ff = _ffn_down_bwd(dffn, w_down_f, g_ff, u_ff)
    gw_down = _matmul([(act, dffn)], "tn", f32, "grad_w_down")
    gw_gate = _matmul([(v_in, dg_ff)], "tn", f32, "grad_w_gate", out_chunks=nchip)
    gw_up = _matmul([(v_in, du_ff)], "tn", f32, "grad_w_up", out_chunks=nchip)
    ffn_g = [gw_gate, gw_up, gw_down.reshape(nchip, -1, d)]
    dv_in, ffn_from_sib = _matmul([(dg_ff, w_gate_f), (du_ff, w_up_f)], "nt", f32, "ffn_up_bwd",
                                  comm=_sibling_send_halves(ffn_g), at=(0.0,))
    ffn_pairs = pair_sums(FFN, ffn_g, ffn_from_sib)

    def mid_bwd_fn(h1v, dvv, dh2v, mx, w_pf, w_pm):
        dxn, dw_pf = _rms_bwd(h1v, w_pf, dvv)
        dh1v = dh2v + dxn
        dmx, dw_pm = _rms_bwd(mx, w_pm, dh1v)
        return dh1v, dmx, dw_pf, dw_pm

    dh1, dmix, g_pre_ffn, g_post_mix = _rowwise(
        mid_bwd_fn, t, tmw, [_rows(h1, tmw), _rows(dv_in, tmw), _rows(dh2, tmw), _rows(mix, tmw),
                             _par(w_pre_ffn), _par(w_post_mix)],
        [_rout(t, d, f32, tmw), _rout(t, d, bf16, tmw)], [(1, d), (1, d)], "pre_ffn_post_mix_norm_bwd")
    dcat = _matmul([(dmix, w_out_f)], "nt", f32, "out_proj_bwd")
    gw_out = _matmul([(cat, dmix)], "tn", f32, "grad_w_out")

    def mix_norms_bwd_fn(ov, yv, *rest):
        zs, dcv, wa, ws = rest[:SSD_G], rest[SSD_G], rest[SSD_G + 1], rest[SSD_G + 2]
        dov, dwa = _rms_bwd(ov, wa, dcv[:, :mla_w])
        dl = [jnp.broadcast_to(jnp.sum(dov[:, h * V_HEAD:(h + 1) * V_HEAD] * ov[:, h * V_HEAD:(h + 1) * V_HEAD],
                                       axis=1, keepdims=True), (ov.shape[0], V_HEAD)) for h in range(nh)]
        dys, dzs, dws = [], [], []
        for i in range(SSD_G):
            sl = slice(i * gw, (i + 1) * gw)
            zv, yi = zs[i], yv[:, sl]
            sg = _sigmoid(zv)
            sz = zv * sg
            dgi, dwi = _rms_bwd(yi * sz, ws[:, sl], dcv[:, mla_w + i * gw: mla_w + (i + 1) * gw])
            dys.append(dgi * sz)
            dzs.append(dgi * yi * (sg * (1.0 + zv * (1.0 - sg))))
            dws.append(dwi)
        cc = lambda a: jnp.concatenate(a, axis=1)
        return dov, cc(dl), cc(dys), cc(dzs), dwa, cc(dws)

    do_att, delta, dy_ssd, dz, g_attn_n, g_ssd_n = _rowwise(
        mix_norms_bwd_fn, t, tm,
        [_rows(o_att, tm), _rows(y_ssd, tm)] + z_ins + [_rows(dcat, tm), _par(w_attn_n), _par(w_ssd_n)],
        [_rout(t, mla_w, bf16, tm), _rout(t, mla_w, f32, tm), _rout(t, width, f32, tm), _rout(t, width, bf16, tm)],
        [(1, mla_w), (1, width)], "attn_ssd_out_norms_bwd")
    dkv, dkr_h, dq2, gu_from_chips = _flash_bwd(q2, kv, kr2, do_att, lse, delta, nh, scale,
                                                _chips_exchange(ffn_pairs[:2]), (0.0,))

    def q_rope_bwd_fn(dqt, cp, sa, sb):
        parts = []
        for h in range(nh):
            parts.append(dqt[:, h * QK_PAD: h * QK_PAD + NOPE])
            parts.append(_rope_bwd(dqt[:, h * QK_PAD + NOPE:(h + 1) * QK_PAD], cp, sa, sb))
        return jnp.concatenate(parts, axis=1)

    dq_raw = _rowwise(q_rope_bwd_fn, t, tm, [_rows(dq2, tm), _rows(cosp, tm), _rows(sina, tm), _rows(sinb, tm)],
                      [_rout(t, nh * QK_PAD, bf16, tm)], [], "q_rope_bwd")[0]

    def k_rope_bwd_fn(dk, cp, sa, sb):
        tot = dk[:, 0:LANES]
        for h in range(1, nh):
            tot = tot + dk[:, h * LANES:(h + 1) * LANES]
        return _rope_bwd(tot, cp, sa, sb)

    dkr = _rowwise(k_rope_bwd_fn, t, tm, [_rows(dkr_h, tm), _rows(cosp, tm), _rows(sina, tm), _rows(sinb, tm)],
                   [_rout(t, LANES, f32, tm)], [], "k_rope_bwd")[0]
    gw_uq_p = _matmul([(cqn, dq_raw)], "tn", f32, "grad_w_uq")
    gw_ukv = _matmul([(ckvn, dkv)], "tn", f32, "grad_w_ukv", out_chunks=nchip)
    dcqn = _matmul([(dq_raw, w_uq_p)], "nt", f32, "q_up_bwd")
    dckvn = _matmul([(dkv, w_ukv_f)], "nt", f32, "kv_up_bwd")

    def lat_norm_bwd_fn(a, w, dyv):
        return _rms_bwd(a, w, dyv)

    dcq, g_qn = _rowwise(lat_norm_bwd_fn, t, tm, [cq_in, _par(w_qn), _rows(dcqn, tm)],
                         [_rout(t, Q_RANK, bf16, tm)], [(1, Q_RANK)], "q_norm_bwd")
    dckv, g_kvn = _rowwise(lat_norm_bwd_fn, t, tm, [ckv_in, _par(w_kvn), _rows(dckvn, tm)],
                           [_rout(t, KV_RANK, bf16, tm)], [(1, KV_RANK)], "kv_norm_bwd")

    dxbc_act, ddt_raw, g_alog, g_dskip, g_dtb, down_from_chips = _ssd_bwd(
        xbc_act, dt_raw, dt_raw_t, dtb, dtb_t, alog, alog_t, dskip, hprev, dy_ssd, width,
        _chips_exchange(ffn_pairs[2:]), (0.0,))
    ffn_halves = chip_sums(FFN, ffn_g, ffn_from_sib, gu_from_chips + down_from_chips)
    dpre, gcw0, gcw1, gcw2, gcw3, g_conv_b = _conv_bwd_pre(proj, conv_full, conv_b_r, dxbc_act, cdim, tm)
    g_conv_w = jnp.concatenate([gcw0, gcw1, gcw2, gcw3], axis=0)
    dxbc = _conv_bwd_dx(dpre, conv_full, tm)

    def dproj_fn(a, b, c, e, kr_blk, dt_blk):
        parts = [a, b, c, e, (kr_blk + dt_blk).astype(bf16)]
        if d_in_p > off_tail + tail:
            parts.append(jnp.zeros((a.shape[0], d_in_p - off_tail - tail), bf16))
        return jnp.concatenate(parts, axis=1)

    dproj = _rowwise(dproj_fn, t, tm, [_rows(dxbc, tm), _rows(dcq, tm), _rows(dckv, tm), _rows(dz, tm),
                                       _rows(dkr, tm), _rows(ddt_raw, tm)],
                     [_rout(t, d_in_p, bf16, tm)], [], "in_proj_grad_pack")[0]
    split_cols = lambda gf: jnp.stack(jnp.split(gf, nchip, axis=1))
    split_rows = lambda gf: gf.reshape(nchip, gf.shape[0] // nchip, gf.shape[1])
    gw_uq = gw_uq_p.reshape(Q_RANK, nh, QK_PAD)[:, :, :qk_head].reshape(Q_RANK, nh * qk_head)
    rest_g = [split_cols(gw_uq), gw_ukv, split_rows(gw_out)]
    gw_in_pt, rest_from_sib = _matmul([(dproj, u)], "tn", f32, "grad_w_in",
                                      comm=_sibling_send_halves(rest_g), at=(0.0,))
    gseg = lambda a, b: gw_in_pt[a:b]
    gw_in_t = jnp.concatenate([gseg(off_cq, off_ckv), gseg(off_ckv, off_z), gseg(off_tail, off_tail + ROPE),
                               gseg(off_z, off_tail), gseg(off_xbc, off_cq),
                               gseg(off_tail + ROPE, off_tail + ROPE + hs)], axis=0)
    in_g = [gw_in_t.reshape(nchip, cs_in * d // LANES, LANES)]
    rest_pairs = pair_sums(MIXER[1:], rest_g, rest_from_sib)
    du_in, both = _matmul([(dproj, w_in_pt)], "nn", f32, "in_proj_bwd",
                          comm=_both(_sibling_send_halves(in_g), _chips_exchange(rest_pairs)), at=(0.0,))
    in_from_sib, rest_from_chips = both[:1], both[1:]
    in_pairs = pair_sums(MIXER[:1], in_g, in_from_sib)

    def first_bwd_fn(xv, duv, dh1v, w):
        dxn, dw = _rms_bwd(xv, w, duv)
        return dh1v + dxn, dw

    (grad_x, g_pre_mix), in_from_chips = _rowwise(
        first_bwd_fn, t, tmw, [_rows(xs, tmw), _rows(du_in, tmw), _rows(dh1, tmw), _par(w_pre_mix)],
        [_rout(t, d, f32, tmw)], [(1, d)], "pre_mix_norm_bwd", comm=_chips_exchange(in_pairs), at=(0.0,))

    mix_halves = (chip_sums(MIXER[:1], in_g, in_from_sib, in_from_chips)
                  + chip_sums(MIXER[1:], rest_g, rest_from_sib, rest_from_chips))
    halves = mix_halves + ffn_halves
    sib_halves = _comm_call(_sibling_swap(halves), "sibling_swap")
    gshard = {}

    gsmall = {"q_norm_w": g_qn, "kv_norm_w": g_kvn, "conv_w": g_conv_w, "conv_b": g_conv_b, "dt_bias": g_dtb,
              "a_log": g_alog, "d_skip": g_dskip, "ssd_norm_w": g_ssd_n, "attn_out_norm_w": g_attn_n,
              "pre_mix_norm_w": g_pre_mix, "post_mix_norm_w": g_post_mix, "pre_ffn_norm_w": g_pre_ffn,
              "post_ffn_norm_w": g_post_ffn}
    small_sizes = [int(np.prod(gsmall[n].shape)) for n in SMALL]
    srows = _round_up(-(-sum(small_sizes) // LANES), 8)
    spart = _flat_pad([gsmall[n] for n in SMALL], srows * LANES).reshape(srows, LANES)
    sall = _gather_all(spart)

    def sum8_fn(a):
        tot = a[0]
        for k in range(1, 8):
            tot = tot + a[k]
        return tot

    ssum = _blocked(sum8_fn, (1,), [(sall, sall.shape, lambda i: (0, 0, 0))],
                    [((srows, LANES), f32, (srows, LANES), lambda i: (0, 0))], [], "small_grad_sum")[0].reshape(-1)
    gred = {}
    off = 0
    for n, sz in zip(SMALL, small_sizes):
        gred[n] = ssum[off:off + sz].reshape(gsmall[n].shape)
        off += sz
    gshard["conv_w"] = lax.dynamic_slice_in_dim(gred["conv_w"], my_chip * ccs, ccs, axis=1)
    for n in SMALL:
        if n != "conv_w":
            gshard[n] = gred[n].reshape(wts[n].shape)

    delta, new_m, new_v = {}, {}, {}
    big_out = {}
    for n, mine_h, sib_h in zip(BIG, halves, sib_halves):
        view = flat_t if n == "w_in" else (lambda a3: a3)
        res = _adamw_halves(view(local[n]), view(local["m_" + n]), view(local["v_" + n]), mine_h, sib_h, sp,
                            "adamw_" + n)
        big_out[n] = [jnp.swapaxes(a.reshape(1, cs_in, d), 1, 2) for a in res] if n == "w_in" else res
    pack = lambda src: _flat_pad([src[n] for n in SMALL], srows * LANES).reshape(srows, LANES)
    sd, sm, sv = _adamw_call(pack(wts), pack(gshard), pack(mom_m), pack(mom_v), "adamw_small")
    off = 0
    for n in SMALL:
        sz = int(np.prod(wts[n].shape))
        for dst, src in ((delta, sd), (new_m, sm), (new_v, sv)):
            dst[n] = src.reshape(-1)[off:off + sz].reshape(wts[n].shape)
        off += sz

    small_out = (gshard, delta, new_m, new_v)
    pick = lambda k: [big_out[n][k] if n in big_out else small_out[k][n][None] for n in ORDER]
    return (loss, grad_x[None], *pick(0), *pick(1), *pick(2), *pick(3))
```

```python
import functools

import numpy as np
import jax
import jax.numpy as jnp
from jax import lax
from jax.experimental import pallas as pl
from jax.experimental.pallas import tpu as pltpu

f32, bf16 = jnp.float32, jnp.bfloat16

EPS = 1e-6
V_HEAD = 128
NOPE = 128
ROPE = 64
QK_PAD = 256
Q_RANK = 512
KV_RANK = 512
ROPE_THETA = 10000.0
SSD_P = 64
SSD_G = 2
SSD_N = 128
SSD_K = 4
CHUNK = 128
ADAM_LR, ADAM_B1, ADAM_B2, ADAM_EPS, ADAM_WD, ADAM_STEP = 0.001, 0.9, 0.999, 1e-08, 0.01, 10

VMEM_LIMIT_BYTES = 48 * 1024 * 1024
LANES = 128
ATT_TILE = 1024
ATT_TILE_FWD = 1024
MM_TM, MM_TN, MM_TK = 1408, 1024, 1408
CHUNK_WHOLE_MAX = 1536
ROW_TILE = 256
ROW_BLOCK_BYTES = 2 * 1024 * 1024

NN = (((1,), (0,)), ((), ()))
NT = (((1,), (1,)), ((), ()))
TN = (((0,), (0,)), ((), ()))
MESH = pl.DeviceIdType.MESH
ANY = pl.BlockSpec(memory_space=pl.ANY)


def _tile(dim, cap, align=LANES):
    if dim <= cap:
        return dim
    t = (cap // align) * align
    while t >= align:
        if dim % t == 0:
            return t
        t -= align
    raise ValueError(f"no tile for {dim} under {cap}")


def _row_tile(rows, cols, align):
    best = None
    for tr in range(align, rows + 1, align):
        if rows % tr == 0 and tr * cols * 4 <= ROW_BLOCK_BYTES:
            best = tr
    return best or rows


def _round_up(n, m):
    return -(-n // m) * m


def _params(sem):
    return pltpu.CompilerParams(dimension_semantics=sem, vmem_limit_bytes=VMEM_LIMIT_BYTES)


def _dot(a, b, dims):
    return lax.dot_general(a.astype(bf16), b.astype(bf16), dims, preferred_element_type=f32)


def _call(body, name, out_shape, grid, in_specs, out_specs, scratch_shapes, sem, args, comm=None, at=None,
          prefetch=()):
    npf = len(prefetch)
    if comm is None:
        gs = pltpu.PrefetchScalarGridSpec(num_scalar_prefetch=npf, grid=grid, in_specs=list(in_specs),
                                          out_specs=list(out_specs), scratch_shapes=list(scratch_shapes))
        res = pl.pallas_call(body, name=name, out_shape=list(out_shape), grid_spec=gs,
                             compiler_params=_params(sem))(*prefetch, *args)
        return list(res), []
    n_in, n_out, n_sc = len(args), len(out_shape), len(scratch_shapes)
    na, no = len(comm.args), len(comm.outs)
    steps = int(np.prod(grid))

    def full(*allrefs):
        pf, refs = allrefs[:npf], allrefs[npf:]
        cin = refs[n_in:n_in + na]
        o0 = n_in + na
        cout = refs[o0 + n_out:o0 + n_out + no]
        s0 = o0 + n_out + no
        send_sems, recv_sems = refs[s0 + n_sc], refs[s0 + n_sc + 1]
        lin = pl.program_id(0)
        for dim in range(1, len(grid)):
            lin = lin * grid[dim] + pl.program_id(dim)
        for p in range(comm.nphase - 1):
            @pl.when(lin == int(round(at[p] * (steps - 1))))
            def _(p=p):
                comm.run(p, cin, cout, send_sems, recv_sems)
        body(*pf, *refs[:n_in], *refs[o0:o0 + n_out], *refs[s0:s0 + n_sc])

        @pl.when(lin == steps - 1)
        def _():
            comm.run(comm.nphase - 1, cin, cout, send_sems, recv_sems)

    gs = pltpu.PrefetchScalarGridSpec(
        num_scalar_prefetch=npf, grid=grid, in_specs=list(in_specs) + [ANY] * na,
        out_specs=list(out_specs) + [ANY] * no, scratch_shapes=list(scratch_shapes) + comm.sems())
    res = pl.pallas_call(full, name=name, out_shape=list(out_shape) + comm.outs, grid_spec=gs,
                         compiler_params=_params(("arbitrary",) * len(grid)))(*prefetch, *args, *comm.args)
    return list(res[:n_out]), list(res[n_out:])


def _chunk_tile(cs, cap):
    return cs if cs <= CHUNK_WHOLE_MAX else _tile(cs, cap)


def _matmul(pairs, mode, out_dtype, name, out_chunks=None, comm=None, at=None):
    a0, b0 = pairs[0]
    chunked = b0.ndim == 3
    cs = b0.shape[2] if chunked else None
    bcols = b0.shape[0] * b0.shape[2] if chunked else b0.shape[1]
    brows = b0.shape[1] if chunked else b0.shape[0]
    if mode == "nn":
        (m, k), n = a0.shape, bcols
    elif mode == "nt":
        (m, k), n = a0.shape, brows
    else:
        (k, m), n = a0.shape, bcols
    tm = _tile(m, MM_TM)
    if mode == "nt":
        tn = _tile(n, MM_TN)
        tk = _chunk_tile(cs, MM_TK) if chunked else _tile(k, MM_TK)
    else:
        tk = _tile(k, MM_TK)
        if chunked:
            tn = _chunk_tile(cs, MM_TN)
        elif out_chunks:
            tn = _chunk_tile(n // out_chunks, MM_TN)
        else:
            tn = _tile(n, MM_TN)
    nk = k // tk
    if mode == "nn":
        a_spec = pl.BlockSpec((tm, tk), lambda i, j, kk: (i, kk))
        if chunked:
            q = cs // tn
            b_spec = pl.BlockSpec((None, tk, tn), lambda i, j, kk: (j // q, kk, j % q))
        else:
            b_spec = pl.BlockSpec((tk, tn), lambda i, j, kk: (kk, j))
        dims = NN
    elif mode == "nt":
        a_spec = pl.BlockSpec((tm, tk), lambda i, j, kk: (i, kk))
        if chunked:
            q = cs // tk
            b_spec = pl.BlockSpec((None, tn, tk), lambda i, j, kk: (kk // q, j, kk % q))
        else:
            b_spec = pl.BlockSpec((tn, tk), lambda i, j, kk: (j, kk))
        dims = NT
    else:
        a_spec = pl.BlockSpec((tk, tm), lambda i, j, kk: (kk, i))
        b_spec = pl.BlockSpec((tk, tn), lambda i, j, kk: (kk, j))
        dims = TN
    if out_chunks:
        qo = (n // out_chunks) // tn
        out_shape = jax.ShapeDtypeStruct((out_chunks, m, n // out_chunks), out_dtype)
        o_spec = pl.BlockSpec((None, tm, tn), lambda i, j, kk: (j // qo, i, j % qo))
    else:
        out_shape = jax.ShapeDtypeStruct((m, n), out_dtype)
        o_spec = pl.BlockSpec((tm, tn), lambda i, j, kk: (i, j))
    npair = len(pairs)

    def body(*refs):
        o_ref, acc = refs[2 * npair], refs[2 * npair + 1]
        kk = pl.program_id(2)

        @pl.when(kk == 0)
        def _():
            acc[...] = jnp.zeros_like(acc)

        part = _dot(refs[0][...], refs[1][...], dims)
        for p in range(1, npair):
            part = part + _dot(refs[2 * p][...], refs[2 * p + 1][...], dims)
        acc[...] += part

        @pl.when(kk == nk - 1)
        def _():
            o_ref[...] = acc[...].astype(out_dtype)

    args = [t for pr in pairs for t in pr]
    res, cres = _call(body, name, [out_shape], (m // tm, n // tn, nk), [a_spec, b_spec] * npair, [o_spec],
                      [pltpu.VMEM((tm, tn), f32)], ("parallel", "parallel", "arbitrary"), args, comm, at)
    return res[0] if comm is None else (res[0], cres)


def _sigmoid(x):
    return 1.0 / (1.0 + jnp.exp(-x))


def _ffn_up(v, wg, wu):
    m, k = v.shape
    nchunk, _, cs = wg.shape
    n = nchunk * cs
    tm, tn = _tile(m, 512), _chunk_tile(cs, 512)
    q = cs // tn
    w_spec = pl.BlockSpec((None, k, tn), lambda j, i: (j // q, 0, j % q))

    def body(v_ref, wg_ref, wu_ref, g_ref, u_ref, act_ref):
        vb = v_ref[...]
        g = _dot(vb, wg_ref[...], NN)
        u = _dot(vb, wu_ref[...], NN)
        sg = _sigmoid(g)
        silu = g * sg
        g_ref[...] = silu.astype(bf16)
        u_ref[...] = (u * (sg * (1.0 + g * (1.0 - sg)))).astype(bf16)
        act_ref[...] = (silu * u).astype(bf16)

    out = jax.ShapeDtypeStruct((m, n), bf16)
    o_spec = pl.BlockSpec((tm, tn), lambda j, i: (i, j))
    return pl.pallas_call(
        body, name="ffn_up", out_shape=(out, out, out), grid=(n // tn, m // tm),
        in_specs=[pl.BlockSpec((tm, k), lambda j, i: (i, 0)), w_spec, w_spec],
        out_specs=(o_spec, o_spec, o_spec),
        compiler_params=_params(("parallel", "parallel")),
    )(v, wg, wu)


def _ffn_down_bwd(dffn, wd, dact_du, dact_dg):
    m, k = dffn.shape
    n = wd.shape[0]
    tm, tn = _tile(m, 1024), _tile(n, 512)

    def body(d_ref, w_ref, fu_ref, fg_ref, dg_ref, du_ref):
        dact = _dot(d_ref[...], w_ref[...], NT)
        du_ref[...] = (dact * fu_ref[...].astype(f32)).astype(bf16)
        dg_ref[...] = (dact * fg_ref[...].astype(f32)).astype(bf16)

    out = jax.ShapeDtypeStruct((m, n), bf16)
    o_spec = pl.BlockSpec((tm, tn), lambda i, j: (i, j))
    return pl.pallas_call(
        body, name="ffn_down_bwd", out_shape=(out, out), grid=(m // tm, n // tn),
        in_specs=[pl.BlockSpec((tm, k), lambda i, j: (i, 0)), pl.BlockSpec((tn, k), lambda i, j: (j, 0)),
                  o_spec, o_spec],
        out_specs=(o_spec, o_spec),
        compiler_params=_params(("parallel", "parallel")),
    )(dffn, wd, dact_du, dact_dg)


def _blocked(fn, grid, ins, outs, accs, name, sp=None, comm=None, at=None):
    n_in, n_out, n_acc = len(ins), len(outs), len(accs)
    nsp = 0 if sp is None else 1

    def body(*refs):
        refs = refs[nsp:]
        tiles = [r[...] for r in refs[:n_in]]
        res = fn(*tiles)
        if not isinstance(res, (tuple, list)):
            res = (res,)
        for r, val in zip(refs[n_in:n_in + n_out], res[:n_out]):
            r[...] = val.astype(r.dtype)
        if n_acc:
            first = pl.program_id(0) == 0
            for d in range(1, len(grid)):
                first = jnp.logical_and(first, pl.program_id(d) == 0)

            @pl.when(first)
            def _():
                for r in refs[n_in + n_out:]:
                    r[...] = jnp.zeros_like(r)

            for r, val in zip(refs[n_in + n_out:], res[n_out:]):
                r[...] += val

    def acc_map(shape):
        zeros = (0,) * len(shape)
        return lambda *idx: zeros

    in_specs = [pl.BlockSpec(bs, im) for _, bs, im in ins]
    out_specs = [pl.BlockSpec(bs, im) for _, _, bs, im in outs] + [pl.BlockSpec(s, acc_map(s)) for s in accs]
    out_shape = [jax.ShapeDtypeStruct(s, d) for s, d, _, _ in outs] + [jax.ShapeDtypeStruct(s, f32) for s in accs]
    sem = ("arbitrary",) * len(grid) if n_acc else ("parallel",) * len(grid)
    args = [a for a, _, _ in ins]
    res, cres = _call(body, name, out_shape, grid, in_specs, out_specs, [], sem, args, comm, at,
                      prefetch=() if sp is None else (sp,))
    return res if comm is None else (res, cres)


def _rows(a, tm, cols=None, cb=0):
    w = a.shape[1] if cols is None else cols
    return (a, (tm, w), lambda i: (i, cb))


def _par(a):
    zeros = (0,) * a.ndim
    return (a, a.shape, lambda i: zeros)


def _rout(t, w, dtype, tm):
    return ((t, w), dtype, (tm, w), lambda i: (i, 0))


def _rowwise(fn, t, tm, ins, outs, accs, name, comm=None, at=None):
    return _blocked(fn, (t // tm,), ins, outs, accs, name, comm=comm, at=at)


def _rms(x, w):
    r = lax.rsqrt(jnp.mean(x * x, axis=-1, keepdims=True) + EPS)
    return x * r * w


def _rms_bwd(x, w, dy):
    r = lax.rsqrt(jnp.mean(x * x, axis=-1, keepdims=True) + EPS)
    xh = x * r
    dyw = dy * w
    dx = r * (dyw - xh * jnp.mean(dyw * xh, axis=-1, keepdims=True))
    return dx, jnp.sum(dy * xh, axis=0, keepdims=True)


def _silu_grad(x):
    s = _sigmoid(x)
    return s * (1.0 + x * (1.0 - s))


def _rope(blk, cosp, sina, sinb):
    return blk * cosp + pltpu.roll(blk, 96, 1) * sina + pltpu.roll(blk, 32, 1) * sinb


def _rope_bwd(dy, cosp, sina, sinb):
    return dy * cosp + pltpu.roll(dy * sina, 32, 1) + pltpu.roll(dy * sinb, 96, 1)


HALO = 8


def _conv_taps(buf, w, tm, base):
    acc = buf[base:base + tm, :] * w[0:1]
    for k in range(1, SSD_K):
        acc = acc + buf[base + k:base + k + tm, :] * w[k:k + 1]
    return acc


def _conv_specs(t, tm, cdim):
    cur = pl.BlockSpec((tm, cdim), lambda i: (i, 0))
    prev = pl.BlockSpec((HALO, cdim), lambda i: (jnp.maximum(i * (tm // HALO) - 1, 0), 0))
    nxt = pl.BlockSpec((HALO, cdim), lambda i: (jnp.minimum((i + 1) * (tm // HALO), t // HALO - 1), 0))
    return cur, prev, nxt


def _conv_fwd(src, w, b, cdim, tm):
    t = src.shape[0]
    cur, prev, _ = _conv_specs(t, tm, cdim)

    def body(x_ref, p_ref, w_ref, b_ref, o_ref, buf):
        buf[0:HALO, :] = jnp.where(pl.program_id(0) > 0, p_ref[...], 0.0)
        buf[HALO:HALO + tm, :] = x_ref[...]
        pre = _conv_taps(buf, w_ref[...], tm, HALO - (SSD_K - 1)) + b_ref[...]
        o_ref[...] = pre * _sigmoid(pre)

    par = lambda a: pl.BlockSpec(a.shape, lambda i: (0, 0))
    return pl.pallas_call(
        body, name="conv_silu", out_shape=jax.ShapeDtypeStruct((t, cdim), f32), grid=(t // tm,),
        in_specs=[cur, prev, par(w), par(b)], out_specs=cur,
        scratch_shapes=[pltpu.VMEM((tm + HALO, cdim), f32)], compiler_params=_params(("parallel",)),
    )(src, src, w, b)


def _conv_bwd_pre(src, w, b, dact, cdim, tm):
    t = src.shape[0]
    cur, prev, _ = _conv_specs(t, tm, cdim)

    def body(x_ref, p_ref, w_ref, b_ref, d_ref, dpre_ref, dw0, dw1, dw2, dw3, db, buf):
        @pl.when(pl.program_id(0) == 0)
        def _():
            for r in (dw0, dw1, dw2, dw3, db):
                r[...] = jnp.zeros_like(r)

        buf[0:HALO, :] = jnp.where(pl.program_id(0) > 0, p_ref[...], 0.0)
        buf[HALO:HALO + tm, :] = x_ref[...]
        base = HALO - (SSD_K - 1)
        pre = _conv_taps(buf, w_ref[...], tm, base) + b_ref[...]
        dpre = d_ref[...] * _silu_grad(pre)
        dpre_ref[...] = dpre
        for k, r in enumerate((dw0, dw1, dw2, dw3)):
            r[...] += jnp.sum(dpre * buf[base + k:base + k + tm, :], axis=0, keepdims=True)
        db[...] += jnp.sum(dpre, axis=0, keepdims=True)

    par = lambda a: pl.BlockSpec(a.shape, lambda i: (0, 0))
    acc = pl.BlockSpec((1, cdim), lambda i: (0, 0))
    acc_shape = jax.ShapeDtypeStruct((1, cdim), f32)
    return pl.pallas_call(
        body, name="conv_silu_bwd", out_shape=[jax.ShapeDtypeStruct((t, cdim), f32)] + [acc_shape] * 5,
        grid=(t // tm,), in_specs=[cur, prev, par(w), par(b), cur], out_specs=[cur] + [acc] * 5,
        scratch_shapes=[pltpu.VMEM((tm + HALO, cdim), f32)], compiler_params=_params(("arbitrary",)),
    )(src, src, w, b, dact)


def _conv_bwd_dx(dpre, w, tm):
    t, cdim = dpre.shape
    cur, _, nxt = _conv_specs(t, tm, cdim)
    last = t // tm - 1

    def body(d_ref, n_ref, w_ref, o_ref, buf):
        buf[0:tm, :] = d_ref[...]
        buf[tm:tm + HALO, :] = jnp.where(pl.program_id(0) < last, n_ref[...], 0.0)
        wv = w_ref[...]
        acc = buf[0:tm, :] * wv[SSD_K - 1:SSD_K]
        for k in range(SSD_K - 1):
            s = SSD_K - 1 - k
            acc = acc + buf[s:s + tm, :] * wv[k:k + 1]
        o_ref[...] = acc.astype(bf16)

    return pl.pallas_call(
        body, name="conv_bwd_dx", out_shape=jax.ShapeDtypeStruct((t, cdim), bf16), grid=(t // tm,),
        in_specs=[cur, nxt, pl.BlockSpec(w.shape, lambda i: (0, 0))], out_specs=cur,
        scratch_shapes=[pltpu.VMEM((tm + HALO, cdim), f32)], compiler_params=_params(("parallel",)),
    )(dpre, dpre, w)


def _causal_mask(s, row0=0):
    row = lax.broadcasted_iota(jnp.int32, s.shape, 0) + row0
    col = lax.broadcasted_iota(jnp.int32, s.shape, 1)
    return jnp.where(row >= col, s, -jnp.inf)


def _causal_pairs(nq, q_major):
    pairs = [(qi, ki) for qi in range(nq) for ki in range(qi + 1)]
    if not q_major:
        pairs.sort(key=lambda p: (p[1], p[0]))
    return (jnp.asarray([p[0] for p in pairs], jnp.int32), jnp.asarray([p[1] for p in pairs], jnp.int32))


def _flash_fwd(q, kv, kr, nheads, scale, comm=None, at=None):
    t = q.shape[0]
    tq = _tile(t, ATT_TILE_FWD)
    nq = t // tq

    qtab, ktab = _causal_pairs(nq, q_major=True)
    hp = 2 if nheads % 2 == 0 else 1

    def body(qt, kt, q_ref, kv_ref, kr_ref, o_ref, lse_ref, m_sc, l_sc, acc_sc):
        qi, ki = qt[pl.program_id(1)], kt[pl.program_id(1)]

        @pl.when(ki == 0)
        def _():
            m_sc[...] = jnp.full_like(m_sc, -jnp.inf)
            l_sc[...] = jnp.zeros_like(l_sc)
            acc_sc[...] = jnp.zeros_like(acc_sc)

        def step(masked, last):
            krb = kr_ref[...]
            ss = []
            for j in range(hp):
                k = jnp.concatenate([kv_ref[:, j * QK_PAD: j * QK_PAD + NOPE], krb], axis=1)
                ss.append(lax.dot_general(q_ref[:, j * QK_PAD:(j + 1) * QK_PAD], k, NT, preferred_element_type=f32))
            soft = []
            for j in range(hp):
                s = ss[j] * scale
                if masked:
                    s = _causal_mask(s)
                m_old = m_sc[j]
                m_new = jnp.maximum(m_old, jnp.max(s, axis=1, keepdims=True))
                alpha = jnp.exp(m_old - m_new)
                p = jnp.exp(s - m_new)
                soft.append((m_new, alpha, alpha * l_sc[j] + jnp.sum(p, axis=1, keepdims=True), p.astype(bf16)))
            for j in range(hp):
                m_new, alpha, l, p = soft[j]
                v = kv_ref[:, j * QK_PAD + NOPE:(j + 1) * QK_PAD]
                acc = alpha * acc_sc[j] + lax.dot_general(p, v, NN, preferred_element_type=f32)
                if last:
                    o_ref[:, j * V_HEAD:(j + 1) * V_HEAD] = acc / l
                    lse_ref[:, j * V_HEAD:(j + 1) * V_HEAD] = jnp.broadcast_to(m_new + jnp.log(l), (tq, V_HEAD))
                else:
                    l_sc[j] = l
                    acc_sc[j] = acc
                    m_sc[j] = m_new

        @pl.when(ki < qi)
        def _():
            step(False, False)

        @pl.when(ki == qi)
        def _():
            step(True, True)

    o_spec = pl.BlockSpec((tq, hp * V_HEAD), lambda h, s, qt, kt: (qt[s], h))
    out = jax.ShapeDtypeStruct((t, nheads * V_HEAD), f32)
    (o, lse), cres = _call(
        body, "flash_fwd", [out, out], (nheads // hp, len(qtab)),
        [pl.BlockSpec((tq, hp * QK_PAD), lambda h, s, qt, kt: (qt[s], h)),
         pl.BlockSpec((tq, hp * QK_PAD), lambda h, s, qt, kt: (kt[s], h)),
         pl.BlockSpec((tq, LANES), lambda h, s, qt, kt: (kt[s], 0))],
        [o_spec, o_spec],
        [pltpu.VMEM((hp, tq, 1), f32), pltpu.VMEM((hp, tq, 1), f32), pltpu.VMEM((hp, tq, V_HEAD), f32)],
        ("parallel", "arbitrary"), [q, kv, kr], comm, at, prefetch=(qtab, ktab))
    return o, lse, cres


def _flash_bwd(q, kv, kr, do, lse, delta, nheads, scale, comm=None, at=None):
    t = q.shape[0]
    tq = _tile(t, ATT_TILE)
    nq = t // tq
    qtab, ktab = _causal_pairs(nq, q_major=False)
    nsub = 2 if tq % 32 == 0 else 1

    def body(qt, kt, q_ref, kn_ref, kr_ref, v_ref, do_ref, lse_ref, dl_ref, dkv_ref, dkr_ref, dq_ref, dk_sc, dv_sc):
        qi, ki = qt[pl.program_id(1)], kt[pl.program_id(1)]

        @pl.when(pl.program_id(1) == 0)
        def _():
            dq_ref[...] = jnp.zeros_like(dq_ref)

        @pl.when(qi == ki)
        def _():
            dk_sc[...] = jnp.zeros_like(dk_sc)
            dv_sc[...] = jnp.zeros_like(dv_sc)

        def step(masked):
            k = jnp.concatenate([kn_ref[...], kr_ref[...]], axis=1)
            vb = v_ref[...]
            parts = [slice(i * (tq // nsub), (i + 1) * (tq // nsub)) for i in range(nsub)]
            qs = [q_ref[r, :] for r in parts]
            dos = [do_ref[r, :] for r in parts]
            ss = [lax.dot_general(qb, k, NT, preferred_element_type=f32) for qb in qs]
            dps = [lax.dot_general(dob, vb, NT, preferred_element_type=f32) for dob in dos]
            ps, dss = [], []
            for r, s, dp in zip(parts, ss, dps):
                s = s * scale
                if masked:
                    s = _causal_mask(s, r.start)
                p = jnp.exp(s - lse_ref[r, 0:1])
                ps.append(p.astype(bf16))
                dss.append((p * (dp - dl_ref[r, 0:1]) * scale).astype(bf16))
            for r, qb, dob, p, ds in zip(parts, qs, dos, ps, dss):
                dv_sc[...] += lax.dot_general(p, dob, TN, preferred_element_type=f32)
                dk_sc[...] += lax.dot_general(ds, qb, TN, preferred_element_type=f32)
                rows = pl.ds(pl.multiple_of(qi * tq + r.start, tq // nsub), tq // nsub)
                dq_ref[rows, :] += lax.dot_general(ds, k, NN, preferred_element_type=f32)

        @pl.when(qi > ki)
        def _():
            step(False)

        @pl.when(qi == ki)
        def _():
            step(True)

        @pl.when(qi == nq - 1)
        def _():
            dk = dk_sc[...]
            dkv_ref[...] = jnp.concatenate([dk[:, :NOPE], dv_sc[...]], axis=1).astype(bf16)
            dkr_ref[...] = dk[:, NOPE:]

    hspec = pl.BlockSpec((tq, V_HEAD), lambda h, s, qt, kt: (qt[s], h))
    (dkv, dkr, dq), cres = _call(
        body, "flash_bwd",
        [jax.ShapeDtypeStruct((t, nheads * QK_PAD), bf16), jax.ShapeDtypeStruct((t, nheads * LANES), f32),
         jax.ShapeDtypeStruct((t, nheads * QK_PAD), f32)],
        (nheads, len(qtab)),
        [pl.BlockSpec((tq, QK_PAD), lambda h, s, qt, kt: (qt[s], h)),
         pl.BlockSpec((tq, NOPE), lambda h, s, qt, kt: (kt[s], 2 * h)),
         pl.BlockSpec((tq, LANES), lambda h, s, qt, kt: (kt[s], 0)),
         pl.BlockSpec((tq, V_HEAD), lambda h, s, qt, kt: (kt[s], 2 * h + 1)),
         hspec, hspec, hspec],
        [pl.BlockSpec((tq, QK_PAD), lambda h, s, qt, kt: (kt[s], h)),
         pl.BlockSpec((tq, LANES), lambda h, s, qt, kt: (kt[s], h)),
         pl.BlockSpec((t, QK_PAD), lambda h, s, qt, kt: (0, h))],
        [pltpu.VMEM((tq, QK_PAD), f32), pltpu.VMEM((tq, V_HEAD), f32)],
        ("parallel", "arbitrary"), [q, kv, kr, kv, do, lse, delta], comm, at, prefetch=(qtab, ktab))
    return dkv, dkr, dq, cres


def _split3(a):
    hi = a.astype(bf16)
    r1 = a - hi.astype(f32)
    mid = r1.astype(bf16)
    lo = (r1 - mid.astype(f32)).astype(bf16)
    return hi, mid, lo


def _split2(a):
    hi = a.astype(bf16)
    return hi, (a - hi.astype(f32)).astype(bf16)


def _ones_dot_left(tri, a):
    hi, mid, lo = _split3(a)
    d = lambda v: lax.dot_general(tri, v, NN, preferred_element_type=f32)
    return d(hi) + d(mid) + d(lo)


def _ones_dot_right(a, tri):
    hi, mid, lo = _split3(a)
    d = lambda v: lax.dot_general(v, tri, NN, preferred_element_type=f32)
    return d(hi) + d(mid) + d(lo)


def _softplus(x):
    return jnp.maximum(x, 0.0) + jnp.log(1.0 + jnp.exp(-jnp.abs(x)))


def _head_spread(hs, width):
    shift = SSD_P.bit_length() - 1
    return (lax.broadcasted_iota(jnp.int32, (hs, width), 0)
            == lax.shift_right_logical(lax.broadcasted_iota(jnp.int32, (hs, width), 1), shift)).astype(bf16)


def _ssd_common(dt_ref, dtT_ref, dtb_ref, dtbT_ref, alog_ref, alogT_ref):
    ii = lax.broadcasted_iota(jnp.int32, (CHUNK, CHUNK), 0)
    jj = lax.broadcasted_iota(jnp.int32, (CHUNK, CHUNK), 1)
    tri = ii >= jj
    raw = dt_ref[...] + dtb_ref[...]
    dt = _softplus(raw)
    a_neg = -jnp.exp(alog_ref[...])
    cum = _ones_dot_left(tri.astype(bf16), dt * a_neg)
    dt_t = _softplus(dtT_ref[...] + dtbT_ref[...])
    cum_t = _ones_dot_right(dt_t * (-jnp.exp(alogT_ref[...])), (ii <= jj).astype(bf16))
    return tri, raw, dt, a_neg, cum, cum_t


def _ssd_fwd(xbc, dt, dt_t, dtb, dtb_t, alog, alog_t, dskip, width, comm=None, at=None):
    t, cdim = xbc.shape
    hs = dt.shape[1]
    nc = t // CHUNK
    epg = hs // SSD_G
    gn = SSD_G * SSD_N

    def body(x_ref, dt_ref, dtT_ref, dtb_ref, dtbT_ref, alog_ref, alogT_ref, d_ref, y_ref, hp_ref, h_sc, yd_sc):
        @pl.when(pl.program_id(0) == 0)
        def _():
            h_sc[...] = jnp.zeros_like(h_sc)

        tri, _, dtv, _, cum, cum_t = _ssd_common(dt_ref, dtT_ref, dtb_ref, dtbT_ref, alog_ref, alogT_ref)
        spread = _head_spread(hs, width)
        clast = cum[CHUNK - 1:CHUNK, :]
        dec = jnp.exp(clast)
        wide = _ones_dot_right(jnp.concatenate([dtv, jnp.exp(cum), jnp.exp(clast - cum),
                                                jnp.broadcast_to(d_ref[...], (CHUNK, hs))], axis=0), spread)
        dt_x, ee_x, ff_x, dsk_x = (wide[i * CHUNK:(i + 1) * CHUNK] for i in range(4))
        xs = x_ref[:, :width]
        xdt = xs * dt_x
        xf = xdt * ff_x
        h_all = h_sc[...]
        hp_ref[0] = h_all
        h_all = h_all.reshape(hs * SSD_P, SSD_N)
        ch_parts = []
        for g in range(SSD_G):
            gsl = slice(g * epg * SSD_P, (g + 1) * epg * SSD_P)
            bb = x_ref[:, width + g * SSD_N: width + (g + 1) * SSD_N].astype(bf16)
            cb_ = x_ref[:, width + gn + g * SSD_N: width + gn + (g + 1) * SSD_N].astype(bf16)
            cbm = lax.dot_general(cb_, bb, NT, preferred_element_type=f32)
            ch_parts.append(_dot(cb_, h_all[gsl], NT))
            st = _dot(xf[:, gsl], bb, TN)
            for e in range(g * epg, (g + 1) * epg):
                esl = slice(e * SSD_P, (e + 1) * SSD_P)
                lmat = jnp.exp(jnp.where(tri, cum[:, e:e + 1] - cum_t[e:e + 1, :], -jnp.inf))
                yd_sc[:, esl] = _dot(cbm * lmat, xdt[:, esl], NN)
                j = e - g * epg
                h_sc[e] = h_sc[e] * dec[:, e:e + 1] + st[j * SSD_P:(j + 1) * SSD_P, :]
        y_ref[...] = yd_sc[...] + jnp.concatenate(ch_parts, axis=1) * ee_x + xs * dsk_x

    par = lambda a: pl.BlockSpec(a.shape, lambda i: (0,) * a.ndim)
    (y, hp), cres = _call(
        body, "ssd_fwd",
        [jax.ShapeDtypeStruct((t, width), f32), jax.ShapeDtypeStruct((nc, hs, SSD_P, SSD_N), f32)], (nc,),
        [pl.BlockSpec((CHUNK, cdim), lambda i: (i, 0)), pl.BlockSpec((CHUNK, hs), lambda i: (i, 0)),
         pl.BlockSpec((hs, CHUNK), lambda i: (0, i)), par(dtb), par(dtb_t), par(alog), par(alog_t), par(dskip)],
        [pl.BlockSpec((CHUNK, width), lambda i: (i, 0)), pl.BlockSpec((1, hs, SSD_P, SSD_N), lambda i: (i, 0, 0, 0))],
        [pltpu.VMEM((hs, SSD_P, SSD_N), f32), pltpu.VMEM((CHUNK, width), f32)], ("arbitrary",),
        [xbc, dt, dt_t, dtb, dtb_t, alog, alog_t, dskip], comm, at)
    return y, hp, cres


def _ssd_bwd(xbc, dt, dt_t, dtb, dtb_t, alog, alog_t, dskip, hprev, dy, width, comm=None, at=None):
    t, cdim = xbc.shape
    hs = dt.shape[1]
    nc = t // CHUNK
    epg = hs // SSD_G
    gn = SSD_G * SSD_N

    def body(x_ref, dt_ref, dtT_ref, dtb_ref, dtbT_ref, alog_ref, alogT_ref, d_ref, hp_ref, dy_ref,
             dx_ref, ddt_ref, dalog_ref, ddsk_ref, ddtb_ref, dh_sc, dxd_sc):
        @pl.when(pl.program_id(0) == 0)
        def _():
            dh_sc[...] = jnp.zeros_like(dh_sc)
            dalog_ref[...] = jnp.zeros_like(dalog_ref)
            ddsk_ref[...] = jnp.zeros_like(ddsk_ref)
            ddtb_ref[...] = jnp.zeros_like(ddtb_ref)

        tri, raw, dtv, a_neg, cum, cum_t = _ssd_common(dt_ref, dtT_ref, dtb_ref, dtbT_ref, alog_ref, alogT_ref)
        dsk = d_ref[...]
        head_row = lax.broadcasted_iota(jnp.int32, (1, hs), 1)
        head_col = lax.broadcasted_iota(jnp.int32, (hs, 1), 0)
        last_row = (lax.broadcasted_iota(jnp.int32, (CHUNK, 1), 0) == CHUNK - 1).astype(f32)
        shift = SSD_P.bit_length() - 1
        spread = _head_spread(hs, width)
        gather = (lax.shift_right_logical(lax.broadcasted_iota(jnp.int32, (width, hs), 0), shift)
                  == lax.broadcasted_iota(jnp.int32, (width, hs), 1)).astype(bf16)
        clast = cum[CHUNK - 1:CHUNK, :]
        ee = jnp.exp(cum)
        ff = jnp.exp(clast - cum)
        dec = jnp.exp(clast)
        wide = _ones_dot_right(jnp.concatenate([dtv, ee, ff, jnp.broadcast_to(dsk, (CHUNK, hs))], axis=0), spread)
        dt_x, ee_x, ff_x, dsk_x = (wide[i * CHUNK:(i + 1) * CHUNK] for i in range(4))
        xs = x_ref[:, :width]
        dyv = dy_ref[...]
        xdt = xs * dt_x
        dye = dyv * ee_x
        xf = xdt * ff_x
        h_all = hp_ref[0].reshape(hs * SSD_P, SSD_N)
        dh_all = dh_sc[...].reshape(hs * SSD_P, SSD_N)
        hi, lo = _split2(_ones_dot_left(spread, dh_all * h_all))
        ones8 = jnp.ones((8, SSD_N), bf16)
        hh = (lax.dot_general(ones8, hi, NT, preferred_element_type=f32)
              + lax.dot_general(ones8, lo, NT, preferred_element_type=f32))[0:1]
        rowsum_m = jnp.zeros((CHUNK, hs), f32)
        colsum_m = jnp.zeros((hs, CHUNK), f32)
        ch_parts, bds_parts = [], []
        for g in range(SSD_G):
            bsl = slice(width + g * SSD_N, width + (g + 1) * SSD_N)
            csl = slice(width + gn + g * SSD_N, width + gn + (g + 1) * SSD_N)
            gsl = slice(g * epg * SSD_P, (g + 1) * epg * SSD_P)
            bb = x_ref[:, bsl].astype(bf16)
            cb_ = x_ref[:, csl].astype(bf16)
            cbm = lax.dot_general(cb_, bb, NT, preferred_element_type=f32)
            hg = h_all[gsl].astype(bf16)
            dhg = dh_all[gsl].astype(bf16)
            ch_parts.append(_dot(cb_, hg, NT))
            bds_parts.append(_dot(bb, dhg, NT))
            dcg = _dot(dye[:, gsl], hg, NN)
            dbg = _dot(xf[:, gsl], dhg, NN)
            dh_new = _dot(dye[:, gsl], cb_, TN)
            dcb = jnp.zeros((CHUNK, CHUNK), f32)
            for e in range(g * epg, (g + 1) * epg):
                esl = slice(e * SSD_P, (e + 1) * SSD_P)
                lmat = jnp.exp(jnp.where(tri, cum[:, e:e + 1] - cum_t[e:e + 1, :], -jnp.inf))
                gmat = cbm * lmat
                dy_e = dyv[:, esl].astype(bf16)
                dgm = _dot(dy_e, xdt[:, esl], NT)
                dxd_sc[:, esl] = _dot(gmat, dy_e, TN)
                dcb = dcb + dgm * lmat
                mm = dgm * gmat
                rowsum_m = rowsum_m + jnp.sum(mm, axis=1, keepdims=True) * (head_row == e).astype(f32)
                colsum_m = colsum_m + (head_col == e).astype(f32) * jnp.sum(mm, axis=0, keepdims=True)
                j = e - g * epg
                dh_sc[e] = dh_new[j * SSD_P:(j + 1) * SSD_P, :] + dec[:, e:e + 1] * dh_sc[e]
            dx_ref[:, bsl] = dbg + _dot(dcb, cb_, TN)
            dx_ref[:, csl] = dcg + _dot(dcb, bb, NN)
        ch_all = jnp.concatenate(ch_parts, axis=1)
        bds_all = jnp.concatenate(bds_parts, axis=1)
        dxdt = bds_all * ff_x + dxd_sc[...]
        dx_ref[:, :width] = dxdt * dt_x + dyv * dsk_x
        sums = _ones_dot_right(jnp.concatenate([dxdt * xs, dyv * ch_all, bds_all * xdt, dyv * xs], axis=0), gather)
        ddtx_all = sums[0:CHUNK]
        dff = sums[2 * CHUNK:3 * CHUNK] * ff
        ddsk = jnp.sum(sums[3 * CHUNK:4 * CHUNK], axis=0, keepdims=True)
        dclast = jnp.sum(dff, axis=0, keepdims=True) + dec * hh
        eye = (lax.broadcasted_iota(jnp.int32, (hs, hs), 0) == lax.broadcasted_iota(jnp.int32, (hs, hs), 1)).astype(bf16)
        colsum_t = sum(lax.dot_general(v, eye, TN, preferred_element_type=f32) for v in _split3(colsum_m))
        dcum_all = sums[CHUNK:2 * CHUNK] * ee - dff + rowsum_m - colsum_t + dclast * last_row
        ii = lax.broadcasted_iota(jnp.int32, (CHUNK, CHUNK), 0)
        jj = lax.broadcasted_iota(jnp.int32, (CHUNK, CHUNK), 1)
        da = _ones_dot_left((jj >= ii).astype(bf16), dcum_all)
        ddt = da * a_neg + ddtx_all
        dalog_ref[...] += jnp.sum(da * dtv, axis=0, keepdims=True) * a_neg
        draw = ddt * _sigmoid(raw)
        place = (lax.broadcasted_iota(jnp.int32, (hs, LANES), 0) + ROPE
                 == lax.broadcasted_iota(jnp.int32, (hs, LANES), 1)).astype(bf16)
        ddt_ref[...] = _ones_dot_right(draw, place)
        ddtb_ref[...] += jnp.sum(draw, axis=0, keepdims=True)
        ddsk_ref[...] += ddsk

    rev = lambda i: nc - 1 - i
    par = lambda a: pl.BlockSpec(a.shape, lambda i: (0,) * a.ndim)
    acc = pl.BlockSpec((1, hs), lambda i: (0, 0))
    acc_shape = jax.ShapeDtypeStruct((1, hs), f32)
    res, cres = _call(
        body, "ssd_bwd",
        [jax.ShapeDtypeStruct((t, cdim), f32), jax.ShapeDtypeStruct((t, LANES), f32), acc_shape, acc_shape, acc_shape],
        (nc,),
        [pl.BlockSpec((CHUNK, cdim), lambda i: (rev(i), 0)), pl.BlockSpec((CHUNK, hs), lambda i: (rev(i), 0)),
         pl.BlockSpec((hs, CHUNK), lambda i: (0, rev(i))), par(dtb), par(dtb_t), par(alog), par(alog_t),
         par(dskip), pl.BlockSpec((1, hs, SSD_P, SSD_N), lambda i: (rev(i), 0, 0, 0)),
         pl.BlockSpec((CHUNK, width), lambda i: (rev(i), 0))],
        [pl.BlockSpec((CHUNK, cdim), lambda i: (rev(i), 0)), pl.BlockSpec((CHUNK, LANES), lambda i: (rev(i), 0)),
         acc, acc, acc],
        [pltpu.VMEM((hs, SSD_P, SSD_N), f32), pltpu.VMEM((CHUNK, width), f32)], ("arbitrary",),
        [xbc, dt, dt_t, dtb, dtb_t, alog, alog_t, dskip, hprev, dy], comm, at)
    return (*res, cres)


def _where_am_i():
    x, y, c = lax.axis_index("x"), lax.axis_index("y"), lax.axis_index("c")
    chips = [(1 - x, y), (x, 1 - y), (1 - x, 1 - y)]
    return x, y, c, chips


def _remote(src, dst, send_sems, recv_sems, k, to):
    return pltpu.make_async_remote_copy(src_ref=src, dst_ref=dst, send_sem=send_sems.at[k], recv_sem=recv_sems.at[k],
                                        device_id=to, device_id_type=MESH)


class _Comm:
    def __init__(self, args, outs, nsem, nphase, run):
        self.args, self.outs, self.nsem, self.nphase, self.run = list(args), list(outs), nsem, nphase, run

    def sems(self):
        return [pltpu.SemaphoreType.DMA((self.nsem,)), pltpu.SemaphoreType.DMA((self.nsem,))]


class _SemWindow:
    def __init__(self, ref, off):
        self.ref, self.off, self.at = ref, off, self

    def __getitem__(self, k):
        return self.ref.at[k + self.off]


def _both(a, b):
    assert a.nphase == b.nphase
    na, no = len(a.args), len(a.outs)

    def run(phase, srcs, outs, send_sems, recv_sems):
        a.run(phase, srcs[:na], outs[:no], send_sems, recv_sems)
        b.run(phase, srcs[na:], outs[no:], _SemWindow(send_sems, a.nsem), _SemWindow(recv_sems, a.nsem))

    return _Comm(a.args + b.args, a.outs + b.outs, a.nsem + b.nsem, a.nphase, run)


def _comm_call(comm, name):
    na, no = len(comm.args), len(comm.outs)

    def body(*refs):
        for phase in range(comm.nphase):
            comm.run(phase, refs[:na], refs[na:na + no], refs[na + no], refs[na + no + 1])

    return list(pl.pallas_call(body, name=name, out_shape=comm.outs, in_specs=[ANY] * na, out_specs=[ANY] * no,
                               scratch_shapes=comm.sems())(*comm.args))


def _half(ref, c, r2):
    return ref.at[pl.ds(c * r2, r2)]


def _gather_weights(shards, wholes):
    ns, nw = len(shards), len(wholes)
    per = 7

    def run(phase, srcs, outs, send_sems, recv_sems):
        x, y, c, chips = _where_am_i()
        me, sib = 2 * x + y, (x, y, 1 - c)
        def first():
            cps = []
            for w in range(ns + nw):
                src, out, base = srcs[w], outs[w], per * w
                halved = w < ns
                r2 = src.shape[0] // 2
                piece = _half(src, c, r2) if halved else src
                for j, (cx, cy) in enumerate(chips):
                    dst = _half(out.at[me], c, r2) if halved else out.at[me]
                    cps.append(_remote(piece, dst, send_sems, recv_sems, base + j, (cx, cy, c)))
                cps.append(_remote(src, out.at[me], send_sems, recv_sems, base + 6, sib))
            return cps

        def passed():
            cps = []
            for w in range(ns):
                r2 = srcs[w].shape[0] // 2
                for j, (cx, cy) in enumerate(chips):
                    got = _half(outs[w].at[2 * cx + cy], c, r2)
                    cps.append(_remote(got, got, send_sems, recv_sems, per * w + 3 + j, sib))
            return cps

        if phase == 0:
            for cp in first():
                cp.start()
        elif phase == 1:
            it = iter(passed())
            for w in range(ns + nw):
                src, out, base = srcs[w], outs[w], per * w
                r2 = src.shape[0] // 2
                for j, (cx, cy) in enumerate(chips):
                    got = _half(out.at[2 * cx + cy], c, r2) if w < ns else out.at[2 * cx + cy]
                    _remote(got, got, send_sems, recv_sems, base + j, sib).wait_recv()
                    if w < ns:
                        next(it).start()
        else:
            for w in range(ns + nw):
                src, out, base = srcs[w], outs[w], per * w
                r2 = src.shape[0] // 2
                if w < ns:
                    for j, (cx, cy) in enumerate(chips):
                        got = _half(out.at[2 * cx + cy], 1 - c, r2)
                        _remote(got, got, send_sems, recv_sems, base + 3 + j, sib).wait_recv()
                _remote(src, out.at[me], send_sems, recv_sems, base + 6, sib).wait_recv()
            for cp in first() + passed():
                cp.wait_send()

    args = list(shards) + list(wholes)
    outs = [jax.ShapeDtypeStruct((4,) + a.shape, a.dtype) for a in args]
    return _Comm(args, outs, per * len(args), 3, run)


def _sibling_send_halves(gs):
    n = len(gs)

    def run(phase, srcs, outs, send_sems, recv_sems):
        x, y, c, _ = _where_am_i()
        sib = (x, y, 1 - c)
        cps = []
        for w in range(n):
            r2 = srcs[w].shape[1] // 2
            for k in range(4):
                cps.append(_remote(_half(srcs[w].at[k], 1 - c, r2), outs[w].at[k], send_sems, recv_sems, 4 * w + k, sib))
        for cp in cps:
            cp.start() if phase == 0 else cp.wait()

    outs = [jax.ShapeDtypeStruct((4, g.shape[1] // 2, g.shape[2]), g.dtype) for g in gs]
    return _Comm(gs, outs, 4 * n, 2, run)


def _chips_exchange(ps):
    n = len(ps)

    def run(phase, srcs, outs, send_sems, recv_sems):
        x, y, c, chips = _where_am_i()
        me = 2 * x + y
        cps = []
        for w in range(n):
            for j, (cx, cy) in enumerate(chips):
                cps.append(_remote(srcs[w].at[2 * cx + cy], outs[w].at[me], send_sems, recv_sems, 3 * w + j, (cx, cy, c)))
        if phase == 0:
            for cp in cps:
                cp.start()
        else:
            for w in range(n):
                for j, (cx, cy) in enumerate(chips):
                    got = outs[w].at[2 * cx + cy]
                    _remote(got, got, send_sems, recv_sems, 3 * w + j, (cx, cy, c)).wait_recv()
            for cp in cps:
                cp.wait_send()

    outs = [jax.ShapeDtypeStruct(p.shape, p.dtype) for p in ps]
    return _Comm(ps, outs, 3 * n, 2, run)


def _sibling_swap(rs):
    n = len(rs)

    def run(phase, srcs, outs, send_sems, recv_sems):
        x, y, c, _ = _where_am_i()
        for w in range(n):
            cp = _remote(srcs[w], outs[w], send_sems, recv_sems, w, (x, y, 1 - c))
            cp.start() if phase == 0 else cp.wait()

    outs = [jax.ShapeDtypeStruct(r.shape, r.dtype) for r in rs]
    return _Comm(rs, outs, n, 2, run)


def _gather_all(v):
    rows = v.shape[0]

    def body(x_ref, out_ref, send_sems, recv_sems, local_sem):
        x, y, c, chips = _where_am_i()
        me, sib = (x, y, c), (x, y, 1 - c)
        blk = lambda px, py, pc: out_ref.at[4 * px + 2 * py + pc]
        mine = pltpu.make_async_copy(x_ref, blk(*me), local_sem)
        mine.start()
        first = [_remote(x_ref, blk(*me), send_sems, recv_sems, 0, sib)]
        first += [_remote(x_ref, blk(*me), send_sems, recv_sems, 1 + j, (*chip, c)) for j, chip in enumerate(chips)]
        for cp in first:
            cp.start()
        passed = [_remote(blk(*chip, c), blk(*chip, c), send_sems, recv_sems, 4 + j, sib) for j, chip in enumerate(chips)]
        for j, chip in enumerate(chips):
            _remote(blk(*chip, c), blk(*chip, c), send_sems, recv_sems, 1 + j, me).wait_recv()
            passed[j].start()
        _remote(blk(*sib), blk(*sib), send_sems, recv_sems, 0, me).wait_recv()
        for j, chip in enumerate(chips):
            _remote(blk(*chip, 1 - c), blk(*chip, 1 - c), send_sems, recv_sems, 4 + j, me).wait_recv()
        for cp in first + passed:
            cp.wait_send()
        mine.wait()

    vm = pl.BlockSpec(memory_space=pltpu.VMEM)
    return pl.pallas_call(
        body, name="gather_all", out_shape=jax.ShapeDtypeStruct((8, rows, LANES), v.dtype),
        in_specs=[vm], out_specs=vm,
        scratch_shapes=[pltpu.SemaphoreType.DMA((7,)), pltpu.SemaphoreType.DMA((7,)), pltpu.SemaphoreType.DMA],
    )(v)


def _flat_pad(parts, total):
    v = jnp.concatenate([p.reshape(-1) for p in parts])
    return jnp.pad(v, (0, total - v.shape[0]))


def _adamw(w, g, m, v):
    m = ADAM_B1 * m + (1.0 - ADAM_B1) * g
    v = ADAM_B2 * v + (1.0 - ADAM_B2) * jnp.square(g)
    m_hat = m / (1.0 - ADAM_B1 ** ADAM_STEP)
    v_hat = v / (1.0 - ADAM_B2 ** ADAM_STEP)
    delta = -ADAM_LR * (m_hat / (jnp.sqrt(v_hat) + ADAM_EPS) + ADAM_WD * w)
    return delta, m, v


def _adamw_call(w, g, m, v, name):
    r, cdim = w.shape
    tm = _tile(r, ROW_TILE, 8)
    o = _rout(r, cdim, f32, tm)
    return _rowwise(_adamw, r, tm, [_rows(w, tm), _rows(g, tm), _rows(m, tm), _rows(v, tm)], [o, o, o], [], name)


def _adamw_halves(w, m, v, g_mine, g_sib, sp, name):
    _, r, cdim = w.shape
    r2 = r // 2
    tr = _row_tile(r2, cdim, 8)
    nb = r2 // tr

    def body(sp_ref, w_ref, m_ref, v_ref, ga_ref, gb_ref, g_out, d_out, m_out, v_out):
        g = jnp.where(pl.program_id(0) == sp_ref[1], ga_ref[...], gb_ref[...])
        delta, mn, vn = _adamw(w_ref[...], g, m_ref[...], v_ref[...])
        g_out[...] = g
        d_out[...] = delta
        m_out[...] = mn
        v_out[...] = vn

    full = pl.BlockSpec((None, tr, cdim), lambda h, i, s: (0, h * nb + i, 0))
    mine = pl.BlockSpec((tr, cdim), lambda h, i, s: (jnp.where(h == s[1], i, 0), 0))
    sib = pl.BlockSpec((tr, cdim), lambda h, i, s: (jnp.where(h == s[1], 0, i), 0))
    out = jax.ShapeDtypeStruct((1, r, cdim), f32)
    gs = pltpu.PrefetchScalarGridSpec(num_scalar_prefetch=1, grid=(2, nb), in_specs=[full, full, full, mine, sib],
                                      out_specs=[full, full, full, full])
    return pl.pallas_call(body, name=name, out_shape=[out, out, out, out], grid_spec=gs,
                          compiler_params=_params(("parallel", "parallel")))(sp, w, m, v, g_mine, g_sib)


MIXER = ("w_in", "w_uq", "w_ukv", "w_out")
FFN = ("w_gate", "w_up", "w_down")
BIG = MIXER + FFN
SMALL = ("q_norm_w", "kv_norm_w", "conv_w", "conv_b", "dt_bias", "a_log", "d_skip", "ssd_norm_w", "attn_out_norm_w",
         "pre_mix_norm_w", "post_mix_norm_w", "pre_ffn_norm_w", "post_ffn_norm_w")
ORDER = ("w_in", "q_norm_w", "w_uq", "kv_norm_w", "w_ukv", "conv_w", "conv_b", "dt_bias", "a_log", "d_skip",
         "ssd_norm_w", "attn_out_norm_w", "w_out", "pre_mix_norm_w", "post_mix_norm_w", "pre_ffn_norm_w",
         "post_ffn_norm_w", "w_gate", "w_up", "w_down")


def kernel(x, positions, w_in, q_norm_w, w_uq, kv_norm_w, w_ukv, conv_w, conv_b, dt_bias, a_log, d_skip, ssd_norm_w, attn_out_norm_w, w_out, pre_mix_norm_w, post_mix_norm_w, pre_ffn_norm_w, post_ffn_norm_w, w_gate, w_up, w_down, loss_target, m_w_in, m_q_norm_w, m_w_uq, m_kv_norm_w, m_w_ukv, m_conv_w, m_conv_b, m_dt_bias, m_a_log, m_d_skip, m_ssd_norm_w, m_attn_out_norm_w, m_w_out, m_pre_mix_norm_w, m_post_mix_norm_w, m_pre_ffn_norm_w, m_post_ffn_norm_w, m_w_gate, m_w_up, m_w_down, v_w_in, v_q_norm_w, v_w_uq, v_kv_norm_w, v_w_ukv, v_conv_w, v_conv_b, v_dt_bias, v_a_log, v_d_skip, v_ssd_norm_w, v_attn_out_norm_w, v_w_out, v_pre_mix_norm_w, v_post_mix_norm_w, v_pre_ffn_norm_w, v_post_ffn_norm_w, v_w_gate, v_w_up, v_w_down):
    local = dict(locals())
    wts = {n: local[n][0] for n in ORDER}
    mom_m = {n: local["m_" + n][0] for n in ORDER}
    mom_v = {n: local["v_" + n][0] for n in ORDER}
    xs = x[0]
    tgt = loss_target[0]
    t, d = xs.shape
    nchip = 4
    my_x, my_y, my_c = lax.axis_index("x"), lax.axis_index("y"), lax.axis_index("c")
    my_chip = 2 * my_x + my_y

    mla_w = d // 2
    nh = mla_w // V_HEAD
    width = d - mla_w
    hs = width // SSD_P
    gn = SSD_G * SSD_N
    cdim = width + 2 * gn
    in_sizes = (Q_RANK, KV_RANK, ROPE, width, cdim, hs)
    d_in = sum(in_sizes)
    tail = LANES
    off_xbc = 0
    off_cq = cdim
    off_ckv = off_cq + Q_RANK
    off_z = off_ckv + KV_RANK
    off_tail = off_z + width
    d_in_p = _round_up(off_tail + tail, 256)
    gw = width // SSD_G
    assert off_cq % Q_RANK == 0 and off_ckv % KV_RANK == 0 and off_z % gw == 0 and off_tail % LANES == 0
    qk_head = NOPE + ROPE
    scale = qk_head ** -0.5
    tm = _tile(t, ROW_TILE, 8)
    tmw = _tile(t, ROW_TILE // 2, 8)

    sp = jnp.stack([my_chip, my_c]).astype(jnp.int32)

    def pair_sums(names, gl, from_sib):
        res = []
        for n, g, fs in zip(names, gl, from_sib):
            _, r2, cs = fs.shape
            tr = _row_tile(r2, cs, 16)
            nb = r2 // tr
            res.append(_blocked(
                lambda a, b: a + b, (nchip, nb),
                [(g, (None, tr, cs), lambda k, i, s, nb=nb: (k, s[1] * nb + i, 0)),
                 (fs, (None, tr, cs), lambda k, i, s: (k, i, 0))],
                [((nchip, r2, cs), bf16, (None, tr, cs), lambda k, i, s: (k, i, 0))], [], "pair_sum_" + n, sp=sp)[0])
        return res

    def chip_sums(names, gl, from_sib, from_chips):
        res = []
        for n, g, fs, fc in zip(names, gl, from_sib, from_chips):
            _, r2, cs = fs.shape
            tr = _row_tile(r2, cs, 16)
            nb = r2 // tr
            res.append(_blocked(
                lambda a, b, r1, r2_, r3: ((a + b) + r1.astype(f32)) + r2_.astype(f32) + r3.astype(f32), (nb,),
                [(g, (None, tr, cs), lambda i, s, nb=nb: (s[0], s[1] * nb + i, 0)),
                 (fs, (None, tr, cs), lambda i, s: (s[0], i, 0)),
                 (fc, (None, tr, cs), lambda i, s: (s[0] ^ 1, i, 0)),
                 (fc, (None, tr, cs), lambda i, s: (s[0] ^ 2, i, 0)),
                 (fc, (None, tr, cs), lambda i, s: (s[0] ^ 3, i, 0))],
                [((r2, cs), f32, (tr, cs), lambda i, s: (i, 0))], [], "chip_sum_" + n, sp=sp)[0])
        return res

    ck, ccs = wts["conv_w"].shape
    cs_in = wts["w_in"].shape[1]
    flat_t = lambda a3: jnp.swapaxes(a3, 1, 2).reshape(1, cs_in * d // LANES, LANES)
    (u,), (w_in_g, conv_g) = _rowwise(
        lambda a, w: _rms(a, w), t, tm, [_rows(xs, tm), _par(wts["pre_mix_norm_w"].reshape(1, -1))],
        [_rout(t, d, bf16, tm)], [], "pre_mix_norm",
        comm=_gather_weights([flat_t(w_in)[0].astype(bf16)], [wts["conv_w"]]), at=(0.0, 0.5))
    cat_cols = lambda g: jnp.concatenate([g[k] for k in range(nchip)], axis=1)
    conv_full = cat_cols(conv_g)
    mixer_gather = _gather_weights([wts[n].astype(bf16) for n in MIXER[1:]], [])
    ffn_gather = _gather_weights([wts[n].astype(bf16) for n in FFN[:2]], [])
    w_down_b = wts["w_down"].astype(bf16)
    dn2 = w_down_b.shape[0] // 2
    down_gathers = [_gather_weights([w_down_b[:dn2]], []), _gather_weights([w_down_b[dn2:]], [])]

    wi = w_in_g.reshape(nchip * cs_in, d)
    o = np.cumsum((0,) + in_sizes)
    seg = lambda i: wi[o[i]:o[i + 1]]
    w_in_pt = jnp.concatenate([seg(4), seg(0), seg(1), seg(3), seg(2), seg(5),
                               jnp.zeros((d_in_p - off_tail - ROPE - hs, d), bf16)], axis=0)

    inv_freq = ROPE_THETA ** (-jnp.arange(0, ROPE, 2, dtype=f32) / ROPE)
    ang = positions[0].astype(f32)[:, None] * inv_freq
    cos, sin = jnp.cos(ang), jnp.sin(ang)
    z32, z64, z96 = jnp.zeros((t, 32), f32), jnp.zeros((t, 64), f32), jnp.zeros((t, 96), f32)
    cosp = jnp.concatenate([cos, cos, z64], axis=1)
    sina = jnp.concatenate([-sin, z96], axis=1)
    sinb = jnp.concatenate([z32, sin, z64], axis=1)

    row = lambda a: a.reshape(1, -1)
    w_pre_mix, w_post_mix = row(wts["pre_mix_norm_w"]), row(wts["post_mix_norm_w"])
    w_pre_ffn, w_post_ffn = row(wts["pre_ffn_norm_w"]), row(wts["post_ffn_norm_w"])
    w_qn, w_kvn = row(wts["q_norm_w"]), row(wts["kv_norm_w"])
    w_attn_n, w_ssd_n = row(wts["attn_out_norm_w"]), row(wts["ssd_norm_w"])
    conv_b_r = row(wts["conv_b"])
    dtb, alog, dskip = row(wts["dt_bias"]), row(wts["a_log"]), row(wts["d_skip"])
    dtb_t, alog_t = dtb.reshape(hs, 1), alog.reshape(hs, 1)

    proj, (w_uq_g, w_ukv_f, w_out_g) = _matmul([(u, w_in_pt)], "nt", f32, "in_proj", comm=mixer_gather,
                                               at=(0.0, 0.6))
    w_uq_p = jnp.pad(cat_cols(w_uq_g).reshape(Q_RANK, nh, qk_head), ((0, 0), (0, 0), (0, QK_PAD - qk_head))
                     ).reshape(Q_RANK, nh * QK_PAD)
    w_out_f = w_out_g.reshape(-1, w_out_g.shape[2])
    cq_in = _rows(proj, tm, Q_RANK, off_cq // Q_RANK)
    ckv_in = _rows(proj, tm, KV_RANK, off_ckv // KV_RANK)
    cqn = _rowwise(lambda a, w: _rms(a, w), t, tm, [cq_in, _par(w_qn)], [_rout(t, Q_RANK, bf16, tm)], [], "q_norm")[0]
    ckvn = _rowwise(lambda a, w: _rms(a, w), t, tm, [ckv_in, _par(w_kvn)], [_rout(t, KV_RANK, bf16, tm)], [],
                    "kv_norm")[0]
    q_raw = _matmul([(cqn, w_uq_p)], "nn", f32, "q_up")
    kv = _matmul([(ckvn, w_ukv_f)], "nn", bf16, "kv_up")

    def q_rope_fn(qt, cp, sa, sb):
        parts = []
        for h in range(nh):
            parts.append(qt[:, h * QK_PAD: h * QK_PAD + NOPE])
            parts.append(_rope(qt[:, h * QK_PAD + NOPE:(h + 1) * QK_PAD], cp, sa, sb))
        return jnp.concatenate(parts, axis=1)

    tail_cb = off_tail // LANES
    q2 = _rowwise(q_rope_fn, t, tm, [_rows(q_raw, tm), _rows(cosp, tm), _rows(sina, tm), _rows(sinb, tm)],
                  [_rout(t, nh * QK_PAD, bf16, tm)], [], "q_rope")[0]
    kr2 = _rowwise(_rope, t, tm, [_rows(proj, tm, LANES, tail_cb), _rows(cosp, tm), _rows(sina, tm), _rows(sinb, tm)],
                   [_rout(t, LANES, bf16, tm)], [], "k_rope")[0]
    o_att, lse, ffn_w = _flash_fwd(q2, kv, kr2, nh, scale, ffn_gather, (0.0, 0.75))
    w_gate_f, w_up_f = ffn_w

    xbc_act = _conv_fwd(proj, conv_full, conv_b_r, cdim, tm)
    dt_raw, dt_raw_t = _blocked(
        lambda blk: (blk[:, ROPE:ROPE + hs], blk.T[ROPE:ROPE + hs, :]), (t // tm,),
        [_rows(proj, tm, LANES, tail_cb)],
        [((t, hs), f32, (tm, hs), lambda i: (i, 0)), ((hs, t), f32, (hs, tm), lambda i: (0, i))], [], "dt_split")
    y_ssd, hprev, (down_a,) = _ssd_fwd(xbc_act, dt_raw, dt_raw_t, dtb, dtb_t, alog, alog_t, dskip, width,
                                       down_gathers[0], (0.0, 0.6))
    z_ins = [_rows(proj, tm, gw, off_z // gw + i) for i in range(SSD_G)]

    def mix_norms_fn(ov, yv, *rest):
        zs, wa, ws = rest[:SSD_G], rest[SSD_G], rest[SSD_G + 1]
        outs = [_rms(ov, wa)]
        for i in range(SSD_G):
            sl = slice(i * gw, (i + 1) * gw)
            outs.append(_rms(yv[:, sl] * (zs[i] * _sigmoid(zs[i])), ws[:, sl]))
        return jnp.concatenate(outs, axis=1)

    cat = _rowwise(mix_norms_fn, t, tm, [_rows(o_att, tm), _rows(y_ssd, tm)] + z_ins + [_par(w_attn_n), _par(w_ssd_n)],
                   [_rout(t, d, bf16, tm)], [], "attn_ssd_out_norms")[0]
    mix, (down_b,) = _matmul([(cat, w_out_f)], "nn", f32, "out_proj", comm=down_gathers[1], at=(0.0, 0.6))
    w_down_f = jnp.concatenate([down_a, down_b], axis=1).reshape(-1, down_a.shape[2])

    def post_mix_fn(mx, xv, w1, w2):
        h1v = xv + _rms(mx, w1)
        return h1v, _rms(h1v, w2)

    h1, v_in = _rowwise(post_mix_fn, t, tmw, [_rows(mix, tmw), _rows(xs, tmw), _par(w_post_mix), _par(w_pre_ffn)],
                        [_rout(t, d, f32, tmw), _rout(t, d, bf16, tmw)], [], "post_mix_pre_ffn_norm")
    g_ff, u_ff, act = _ffn_up(v_in, w_gate_f, w_up_f)
    ffn = _matmul([(act, w_down_f)], "nn", f32, "ffn_down")

    def final_fn(fv, h1v, tg, w):
        h2 = h1v + _rms(fv, w)
        err = h2 - tg
        lpart = 0.5 * jnp.sum(jnp.sum(err * err, axis=1, keepdims=True), axis=0, keepdims=True) / d
        dh2 = err / d
        dff, dw = _rms_bwd(fv, w, dh2)
        return dff, dh2, jnp.broadcast_to(lpart, (1, LANES)), dw

    dffn, dh2, loss_acc, g_post_ffn = _rowwise(
        final_fn, t, tmw, [_rows(ffn, tmw), _rows(h1, tmw), _rows(tgt, tmw), _par(w_post_ffn)],
        [_rout(t, d, bf16, tmw), _rout(t, d, f32, tmw)], [(1, LANES), (1, d)], "loss_post_ffn_norm_bwd")
    loss = lax.psum(loss_acc[0, 0], ("x", "y", "c"))

    dg_ff, du_ff = _ffn_down_bwd(dffn, w_down_f, g_ff, u_ff)
    gw_down = _matmul([(act, dffn)], "tn", f32, "grad_w_down")
    gw_gate = _matmul([(v_in, dg_ff)], "tn", f32, "grad_w_gate", out_chunks=nchip)
    gw_up = _matmul([(v_in, du_ff)], "tn", f32, "grad_w_up", out_chunks=nchip)
    ffn_g = [gw_gate, gw_up, gw_down.reshape(nchip, -1, d)]
    dv_in, ffn_from_sib = _matmul([(dg_ff, w_gate_f), (du_ff, w_up_f)], "nt", f32, "ffn_up_bwd",
                                  comm=_sibling_send_halves(ffn_g), at=(0.0,))
    ffn_pairs = pair_sums(FFN, ffn_g, ffn_from_sib)

    def mid_bwd_fn(h1v, dvv, dh2v, mx, w_pf, w_pm):
        dxn, dw_pf = _rms_bwd(h1v, w_pf, dvv)
        dh1v = dh2v + dxn
        dmx, dw_pm = _rms_bwd(mx, w_pm, dh1v)
        return dh1v, dmx, dw_pf, dw_pm

    dh1, dmix, g_pre_ffn, g_post_mix = _rowwise(
        mid_bwd_fn, t, tmw, [_rows(h1, tmw), _rows(dv_in, tmw), _rows(dh2, tmw), _rows(mix, tmw),
                             _par(w_pre_ffn), _par(w_post_mix)],
        [_rout(t, d, f32, tmw), _rout(t, d, bf16, tmw)], [(1, d), (1, d)], "pre_ffn_post_mix_norm_bwd")
    dcat = _matmul([(dmix, w_out_f)], "nt", f32, "out_proj_bwd")
    gw_out = _matmul([(cat, dmix)], "tn", f32, "grad_w_out")

    def mix_norms_bwd_fn(ov, yv, *rest):
        zs, dcv, wa, ws = rest[:SSD_G], rest[SSD_G], rest[SSD_G + 1], rest[SSD_G + 2]
        dov, dwa = _rms_bwd(ov, wa, dcv[:, :mla_w])
        dl = [jnp.broadcast_to(jnp.sum(dov[:, h * V_HEAD:(h + 1) * V_HEAD] * ov[:, h * V_HEAD:(h + 1) * V_HEAD],
                                       axis=1, keepdims=True), (ov.shape[0], V_HEAD)) for h in range(nh)]
        dys, dzs, dws = [], [], []
        for i in range(SSD_G):
            sl = slice(i * gw, (i + 1) * gw)
            zv, yi = zs[i], yv[:, sl]
            sg = _sigmoid(zv)
            sz = zv * sg
            dgi, dwi = _rms_bwd(yi * sz, ws[:, sl], dcv[:, mla_w + i * gw: mla_w + (i + 1) * gw])
            dys.append(dgi * sz)
            dzs.append(dgi * yi * (sg * (1.0 + zv * (1.0 - sg))))
            dws.append(dwi)
        cc = lambda a: jnp.concatenate(a, axis=1)
        return dov, cc(dl), cc(dys), cc(dzs), dwa, cc(dws)

    do_att, delta, dy_ssd, dz, g_attn_n, g_ssd_n = _rowwise(
        mix_norms_bwd_fn, t, tm,
        [_rows(o_att, tm), _rows(y_ssd, tm)] + z_ins + [_rows(dcat, tm), _par(w_attn_n), _par(w_ssd_n)],
        [_rout(t, mla_w, bf16, tm), _rout(t, mla_w, f32, tm), _rout(t, width, f32, tm), _rout(t, width, bf16, tm)],
        [(1, mla_w), (1, width)], "attn_ssd_out_norms_bwd")
    dkv, dkr_h, dq2, gu_from_chips = _flash_bwd(q2, kv, kr2, do_att, lse, delta, nh, scale,
                                                _chips_exchange(ffn_pairs[:2]), (0.0,))

    def q_rope_bwd_fn(dqt, cp, sa, sb):
        parts = []
        for h in range(nh):
            parts.append(dqt[:, h * QK_PAD: h * QK_PAD + NOPE])
            parts.append(_rope_bwd(dqt[:, h * QK_PAD + NOPE:(h + 1) * QK_PAD], cp, sa, sb))
        return jnp.concatenate(parts, axis=1)

    dq_raw = _rowwise(q_rope_bwd_fn, t, tm, [_rows(dq2, tm), _rows(cosp, tm), _rows(sina, tm), _rows(sinb, tm)],
                      [_rout(t, nh * QK_PAD, bf16, tm)], [], "q_rope_bwd")[0]

    def k_rope_bwd_fn(dk, cp, sa, sb):
        tot = dk[:, 0:LANES]
        for h in range(1, nh):
            tot = tot + dk[:, h * LANES:(h + 1) * LANES]
        return _rope_bwd(tot, cp, sa, sb)

    dkr = _rowwise(k_rope_bwd_fn, t, tm, [_rows(dkr_h, tm), _rows(cosp, tm), _rows(sina, tm), _rows(sinb, tm)],
                   [_rout(t, LANES, f32, tm)], [], "k_rope_bwd")[0]
    gw_uq_p = _matmul([(cqn, dq_raw)], "tn", f32, "grad_w_uq")
    gw_ukv = _matmul([(ckvn, dkv)], "tn", f32, "grad_w_ukv", out_chunks=nchip)
    dcqn = _matmul([(dq_raw, w_uq_p)], "nt", f32, "q_up_bwd")
    dckvn = _matmul([(dkv, w_ukv_f)], "nt", f32, "kv_up_bwd")

    def lat_norm_bwd_fn(a, w, dyv):
        return _rms_bwd(a, w, dyv)

    dcq, g_qn = _rowwise(lat_norm_bwd_fn, t, tm, [cq_in, _par(w_qn), _rows(dcqn, tm)],
                         [_rout(t, Q_RANK, bf16, tm)], [(1, Q_RANK)], "q_norm_bwd")
    dckv, g_kvn = _rowwise(lat_norm_bwd_fn, t, tm, [ckv_in, _par(w_kvn), _rows(dckvn, tm)],
                           [_rout(t, KV_RANK, bf16, tm)], [(1, KV_RANK)], "kv_norm_bwd")

    dxbc_act, ddt_raw, g_alog, g_dskip, g_dtb, down_from_chips = _ssd_bwd(
        xbc_act, dt_raw, dt_raw_t, dtb, dtb_t, alog, alog_t, dskip, hprev, dy_ssd, width,
        _chips_exchange(ffn_pairs[2:]), (0.0,))
    ffn_halves = chip_sums(FFN, ffn_g, ffn_from_sib, gu_from_chips + down_from_chips)
    dpre, gcw0, gcw1, gcw2, gcw3, g_conv_b = _conv_bwd_pre(proj, conv_full, conv_b_r, dxbc_act, cdim, tm)
    g_conv_w = jnp.concatenate([gcw0, gcw1, gcw2, gcw3], axis=0)
    dxbc = _conv_bwd_dx(dpre, conv_full, tm)

    def dproj_fn(a, b, c, e, kr_blk, dt_blk):
        parts = [a, b, c, e, (kr_blk + dt_blk).astype(bf16)]
        if d_in_p > off_tail + tail:
            parts.append(jnp.zeros((a.shape[0], d_in_p - off_tail - tail), bf16))
        return jnp.concatenate(parts, axis=1)

    dproj = _rowwise(dproj_fn, t, tm, [_rows(dxbc, tm), _rows(dcq, tm), _rows(dckv, tm), _rows(dz, tm),
                                       _rows(dkr, tm), _rows(ddt_raw, tm)],
                     [_rout(t, d_in_p, bf16, tm)], [], "in_proj_grad_pack")[0]
    split_cols = lambda gf: jnp.stack(jnp.split(gf, nchip, axis=1))
    split_rows = lambda gf: gf.reshape(nchip, gf.shape[0] // nchip, gf.shape[1])
    gw_uq = gw_uq_p.reshape(Q_RANK, nh, QK_PAD)[:, :, :qk_head].reshape(Q_RANK, nh * qk_head)
    rest_g = [split_cols(gw_uq), gw_ukv, split_rows(gw_out)]
    gw_in_pt, both = _matmul([(dproj, u)], "tn", f32, "grad_w_in",
                             comm=_both(_sibling_send_halves(rest_g), _sibling_swap(ffn_halves)),
                             at=(0.0,))
    rest_from_sib, ffn_sib_halves = both[:len(rest_g)], both[len(rest_g):]
    gseg = lambda a, b: gw_in_pt[a:b]
    gw_in_t = jnp.concatenate([gseg(off_cq, off_ckv), gseg(off_ckv, off_z), gseg(off_tail, off_tail + ROPE),
                               gseg(off_z, off_tail), gseg(off_xbc, off_cq),
                               gseg(off_tail + ROPE, off_tail + ROPE + hs)], axis=0)
    in_g = [gw_in_t.reshape(nchip, cs_in * d // LANES, LANES)]
    rest_pairs = pair_sums(MIXER[1:], rest_g, rest_from_sib)
    du_in, both = _matmul([(dproj, w_in_pt)], "nn", f32, "in_proj_bwd",
                          comm=_both(_sibling_send_halves(in_g), _chips_exchange(rest_pairs)), at=(0.0,))
    in_from_sib, rest_from_chips = both[:1], both[1:]
    in_pairs = pair_sums(MIXER[:1], in_g, in_from_sib)

    def first_bwd_fn(xv, duv, dh1v, w):
        dxn, dw = _rms_bwd(xv, w, duv)
        return dh1v + dxn, dw

    (grad_x, g_pre_mix), in_from_chips = _rowwise(
        first_bwd_fn, t, tmw, [_rows(xs, tmw), _rows(du_in, tmw), _rows(dh1, tmw), _par(w_pre_mix)],
        [_rout(t, d, f32, tmw)], [(1, d)], "pre_mix_norm_bwd", comm=_chips_exchange(in_pairs), at=(0.0,))

    mix_halves = (chip_sums(MIXER[:1], in_g, in_from_sib, in_from_chips)
                  + chip_sums(MIXER[1:], rest_g, rest_from_sib, rest_from_chips))
    halves = mix_halves + ffn_halves
    sib_halves = _comm_call(_sibling_swap(mix_halves), "sibling_swap") + ffn_sib_halves
    gshard = {}

    gsmall = {"q_norm_w": g_qn, "kv_norm_w": g_kvn, "conv_w": g_conv_w, "conv_b": g_conv_b, "dt_bias": g_dtb,
              "a_log": g_alog, "d_skip": g_dskip, "ssd_norm_w": g_ssd_n, "attn_out_norm_w": g_attn_n,
              "pre_mix_norm_w": g_pre_mix, "post_mix_norm_w": g_post_mix, "pre_ffn_norm_w": g_pre_ffn,
              "post_ffn_norm_w": g_post_ffn}
    small_sizes = [int(np.prod(gsmall[n].shape)) for n in SMALL]
    srows = _round_up(-(-sum(small_sizes) // LANES), 8)
    spart = _flat_pad([gsmall[n] for n in SMALL], srows * LANES).reshape(srows, LANES)
    sall = _gather_all(spart)

    def sum8_fn(a):
        tot = a[0]
        for k in range(1, 8):
            tot = tot + a[k]
        return tot

    ssum = _blocked(sum8_fn, (1,), [(sall, sall.shape, lambda i: (0, 0, 0))],
                    [((srows, LANES), f32, (srows, LANES), lambda i: (0, 0))], [], "small_grad_sum")[0].reshape(-1)
    gred = {}
    off = 0
    for n, sz in zip(SMALL, small_sizes):
        gred[n] = ssum[off:off + sz].reshape(gsmall[n].shape)
        off += sz
    gshard["conv_w"] = lax.dynamic_slice_in_dim(gred["conv_w"], my_chip * ccs, ccs, axis=1)
    for n in SMALL:
        if n != "conv_w":
            gshard[n] = gred[n].reshape(wts[n].shape)

    delta, new_m, new_v = {}, {}, {}
    big_out = {}
    for n, mine_h, sib_h in zip(BIG, halves, sib_halves):
        view = flat_t if n == "w_in" else (lambda a3: a3)
        res = _adamw_halves(view(local[n]), view(local["m_" + n]), view(local["v_" + n]), mine_h, sib_h, sp,
                            "adamw_" + n)
        big_out[n] = [jnp.swapaxes(a.reshape(1, cs_in, d), 1, 2) for a in res] if n == "w_in" else res
    pack = lambda src: _flat_pad([src[n] for n in SMALL], srows * LANES).reshape(srows, LANES)
    sd, sm, sv = _adamw_call(pack(wts), pack(gshard), pack(mom_m), pack(mom_v), "adamw_small")
    off = 0
    for n in SMALL:
        sz = int(np.prod(wts[n].shape))
        for dst, src in ((delta, sd), (new_m, sm), (new_v, sv)):
            dst[n] = src.reshape(-1)[off:off + sz].reshape(wts[n].shape)
        off += sz

    small_out = (gshard, delta, new_m, new_v)
    pick = lambda k: [big_out[n][k] if n in big_out else small_out[k][n][None] for n in ORDER]
    return (loss, grad_x[None], *pick(0), *pick(1), *pick(2), *pick(3))
```

```python
import functools

import numpy as np
import jax
import jax.numpy as jnp
from jax import lax
from jax.experimental import pallas as pl
from jax.experimental.pallas import tpu as pltpu

f32, bf16 = jnp.float32, jnp.bfloat16

EPS = 1e-6
V_HEAD = 128
NOPE = 128
ROPE = 64
QK_PAD = 256
Q_RANK = 512
KV_RANK = 512
ROPE_THETA = 10000.0
SSD_P = 64
SSD_G = 2
SSD_N = 128
SSD_K = 4
CHUNK = 128
ADAM_LR, ADAM_B1, ADAM_B2, ADAM_EPS, ADAM_WD, ADAM_STEP = 0.001, 0.9, 0.999, 1e-08, 0.01, 10

VMEM_LIMIT_BYTES = 48 * 1024 * 1024
LANES = 128
ATT_TILE = 1024
ATT_TILE_FWD = 1024
MM_TM, MM_TN, MM_TK = 1408, 1024, 2048
CHUNK_WHOLE_MAX = 1536
ROW_TILE = 256
ROW_BLOCK_BYTES = 2 * 1024 * 1024

NN = (((1,), (0,)), ((), ()))
NT = (((1,), (1,)), ((), ()))
TN = (((0,), (0,)), ((), ()))
MESH = pl.DeviceIdType.MESH
ANY = pl.BlockSpec(memory_space=pl.ANY)


def _tile(dim, cap, align=LANES):
    if dim <= cap:
        return dim
    t = (cap // align) * align
    while t >= align:
        if dim % t == 0:
            return t
        t -= align
    raise ValueError(f"no tile for {dim} under {cap}")


def _row_tile(rows, cols, align):
    best = None
    for tr in range(align, rows + 1, align):
        if rows % tr == 0 and tr * cols * 4 <= ROW_BLOCK_BYTES:
            best = tr
    return best or rows


def _round_up(n, m):
    return -(-n // m) * m


def _params(sem):
    return pltpu.CompilerParams(dimension_semantics=sem, vmem_limit_bytes=VMEM_LIMIT_BYTES)


def _dot(a, b, dims):
    return lax.dot_general(a.astype(bf16), b.astype(bf16), dims, preferred_element_type=f32)


def _call(body, name, out_shape, grid, in_specs, out_specs, scratch_shapes, sem, args, comm=None, at=None,
          prefetch=()):
    npf = len(prefetch)
    if comm is None:
        gs = pltpu.PrefetchScalarGridSpec(num_scalar_prefetch=npf, grid=grid, in_specs=list(in_specs),
                                          out_specs=list(out_specs), scratch_shapes=list(scratch_shapes))
        res = pl.pallas_call(body, name=name, out_shape=list(out_shape), grid_spec=gs,
                             compiler_params=_params(sem))(*prefetch, *args)
        return list(res), []
    n_in, n_out, n_sc = len(args), len(out_shape), len(scratch_shapes)
    na, no = len(comm.args), len(comm.outs)
    steps = int(np.prod(grid))

    def full(*allrefs):
        pf, refs = allrefs[:npf], allrefs[npf:]
        cin = refs[n_in:n_in + na]
        o0 = n_in + na
        cout = refs[o0 + n_out:o0 + n_out + no]
        s0 = o0 + n_out + no
        send_sems, recv_sems = refs[s0 + n_sc], refs[s0 + n_sc + 1]
        lin = pl.program_id(0)
        for dim in range(1, len(grid)):
            lin = lin * grid[dim] + pl.program_id(dim)
        for p in range(comm.nphase - 1):
            @pl.when(lin == int(round(at[p] * (steps - 1))))
            def _(p=p):
                comm.run(p, cin, cout, send_sems, recv_sems)
        body(*pf, *refs[:n_in], *refs[o0:o0 + n_out], *refs[s0:s0 + n_sc])

        @pl.when(lin == steps - 1)
        def _():
            comm.run(comm.nphase - 1, cin, cout, send_sems, recv_sems)

    gs = pltpu.PrefetchScalarGridSpec(
        num_scalar_prefetch=npf, grid=grid, in_specs=list(in_specs) + [ANY] * na,
        out_specs=list(out_specs) + [ANY] * no, scratch_shapes=list(scratch_shapes) + comm.sems())
    res = pl.pallas_call(full, name=name, out_shape=list(out_shape) + comm.outs, grid_spec=gs,
                         compiler_params=_params(("arbitrary",) * len(grid)))(*prefetch, *args, *comm.args)
    return list(res[:n_out]), list(res[n_out:])


def _chunk_tile(cs, cap):
    return cs if cs <= CHUNK_WHOLE_MAX else _tile(cs, cap)


def _matmul(pairs, mode, out_dtype, name, out_chunks=None, comm=None, at=None):
    a0, b0 = pairs[0]
    chunked = b0.ndim == 3
    cs = b0.shape[2] if chunked else None
    bcols = b0.shape[0] * b0.shape[2] if chunked else b0.shape[1]
    brows = b0.shape[1] if chunked else b0.shape[0]
    if mode == "nn":
        (m, k), n = a0.shape, bcols
    elif mode == "nt":
        (m, k), n = a0.shape, brows
    else:
        (k, m), n = a0.shape, bcols
    tm = _tile(m, MM_TM)
    if mode == "nt":
        tn = _tile(n, MM_TN)
        tk = _chunk_tile(cs, MM_TK) if chunked else _tile(k, MM_TK)
    else:
        tk = _tile(k, MM_TK)
        if chunked:
            tn = _chunk_tile(cs, MM_TN)
        elif out_chunks:
            tn = _chunk_tile(n // out_chunks, MM_TN)
        else:
            tn = _tile(n, MM_TN)
    nk = k // tk
    if mode == "nn":
        a_spec = pl.BlockSpec((tm, tk), lambda i, j, kk: (i, kk))
        if chunked:
            q = cs // tn
            b_spec = pl.BlockSpec((None, tk, tn), lambda i, j, kk: (j // q, kk, j % q))
        else:
            b_spec = pl.BlockSpec((tk, tn), lambda i, j, kk: (kk, j))
        dims = NN
    elif mode == "nt":
        a_spec = pl.BlockSpec((tm, tk), lambda i, j, kk: (i, kk))
        if chunked:
            q = cs // tk
            b_spec = pl.BlockSpec((None, tn, tk), lambda i, j, kk: (kk // q, j, kk % q))
        else:
            b_spec = pl.BlockSpec((tn, tk), lambda i, j, kk: (j, kk))
        dims = NT
    else:
        a_spec = pl.BlockSpec((tk, tm), lambda i, j, kk: (kk, i))
        b_spec = pl.BlockSpec((tk, tn), lambda i, j, kk: (kk, j))
        dims = TN
    if out_chunks:
        qo = (n // out_chunks) // tn
        out_shape = jax.ShapeDtypeStruct((out_chunks, m, n // out_chunks), out_dtype)
        o_spec = pl.BlockSpec((None, tm, tn), lambda i, j, kk: (j // qo, i, j % qo))
    else:
        out_shape = jax.ShapeDtypeStruct((m, n), out_dtype)
        o_spec = pl.BlockSpec((tm, tn), lambda i, j, kk: (i, j))
    npair = len(pairs)

    def body(*refs):
        o_ref, acc = refs[2 * npair], refs[2 * npair + 1]
        kk = pl.program_id(2)

        @pl.when(kk == 0)
        def _():
            acc[...] = jnp.zeros_like(acc)

        part = _dot(refs[0][...], refs[1][...], dims)
        for p in range(1, npair):
            part = part + _dot(refs[2 * p][...], refs[2 * p + 1][...], dims)
        acc[...] += part

        @pl.when(kk == nk - 1)
        def _():
            o_ref[...] = acc[...].astype(out_dtype)

    args = [t for pr in pairs for t in pr]
    res, cres = _call(body, name, [out_shape], (m // tm, n // tn, nk), [a_spec, b_spec] * npair, [o_spec],
                      [pltpu.VMEM((tm, tn), f32)], ("parallel", "parallel", "arbitrary"), args, comm, at)
    return res[0] if comm is None else (res[0], cres)


def _sigmoid(x):
    return 1.0 / (1.0 + jnp.exp(-x))


def _ffn_up(v, wg, wu):
    m, k = v.shape
    nchunk, _, cs = wg.shape
    n = nchunk * cs
    tm, tn = _tile(m, 512), _chunk_tile(cs, 512)
    q = cs // tn
    w_spec = pl.BlockSpec((None, k, tn), lambda j, i: (j // q, 0, j % q))

    def body(v_ref, wg_ref, wu_ref, g_ref, u_ref, act_ref):
        vb = v_ref[...]
        g = _dot(vb, wg_ref[...], NN)
        u = _dot(vb, wu_ref[...], NN)
        sg = _sigmoid(g)
        silu = g * sg
        g_ref[...] = silu.astype(bf16)
        u_ref[...] = (u * (sg * (1.0 + g * (1.0 - sg)))).astype(bf16)
        act_ref[...] = (silu * u).astype(bf16)

    out = jax.ShapeDtypeStruct((m, n), bf16)
    o_spec = pl.BlockSpec((tm, tn), lambda j, i: (i, j))
    return pl.pallas_call(
        body, name="ffn_up", out_shape=(out, out, out), grid=(n // tn, m // tm),
        in_specs=[pl.BlockSpec((tm, k), lambda j, i: (i, 0)), w_spec, w_spec],
        out_specs=(o_spec, o_spec, o_spec),
        compiler_params=_params(("parallel", "parallel")),
    )(v, wg, wu)


def _ffn_down_bwd(dffn, wd, dact_du, dact_dg):
    m, k = dffn.shape
    n = wd.shape[0]
    tm, tn = _tile(m, 1024), _tile(n, 512)

    def body(d_ref, w_ref, fu_ref, fg_ref, dg_ref, du_ref):
        dact = _dot(d_ref[...], w_ref[...], NT)
        du_ref[...] = (dact * fu_ref[...].astype(f32)).astype(bf16)
        dg_ref[...] = (dact * fg_ref[...].astype(f32)).astype(bf16)

    out = jax.ShapeDtypeStruct((m, n), bf16)
    o_spec = pl.BlockSpec((tm, tn), lambda i, j: (i, j))
    return pl.pallas_call(
        body, name="ffn_down_bwd", out_shape=(out, out), grid=(m // tm, n // tn),
        in_specs=[pl.BlockSpec((tm, k), lambda i, j: (i, 0)), pl.BlockSpec((tn, k), lambda i, j: (j, 0)),
                  o_spec, o_spec],
        out_specs=(o_spec, o_spec),
        compiler_params=_params(("parallel", "parallel")),
    )(dffn, wd, dact_du, dact_dg)


def _blocked(fn, grid, ins, outs, accs, name, sp=None, comm=None, at=None):
    n_in, n_out, n_acc = len(ins), len(outs), len(accs)
    nsp = 0 if sp is None else 1

    def body(*refs):
        refs = refs[nsp:]
        tiles = [r[...] for r in refs[:n_in]]
        res = fn(*tiles)
        if not isinstance(res, (tuple, list)):
            res = (res,)
        for r, val in zip(refs[n_in:n_in + n_out], res[:n_out]):
            r[...] = val.astype(r.dtype)
        if n_acc:
            first = pl.program_id(0) == 0
            for d in range(1, len(grid)):
                first = jnp.logical_and(first, pl.program_id(d) == 0)

            @pl.when(first)
            def _():
                for r in refs[n_in + n_out:]:
                    r[...] = jnp.zeros_like(r)

            for r, val in zip(refs[n_in + n_out:], res[n_out:]):
                r[...] += val

    def acc_map(shape):
        zeros = (0,) * len(shape)
        return lambda *idx: zeros

    in_specs = [pl.BlockSpec(bs, im) for _, bs, im in ins]
    out_specs = [pl.BlockSpec(bs, im) for _, _, bs, im in outs] + [pl.BlockSpec(s, acc_map(s)) for s in accs]
    out_shape = [jax.ShapeDtypeStruct(s, d) for s, d, _, _ in outs] + [jax.ShapeDtypeStruct(s, f32) for s in accs]
    sem = ("arbitrary",) * len(grid) if n_acc else ("parallel",) * len(grid)
    args = [a for a, _, _ in ins]
    res, cres = _call(body, name, out_shape, grid, in_specs, out_specs, [], sem, args, comm, at,
                      prefetch=() if sp is None else (sp,))
    return res if comm is None else (res, cres)


def _rows(a, tm, cols=None, cb=0):
    w = a.shape[1] if cols is None else cols
    return (a, (tm, w), lambda i: (i, cb))


def _par(a):
    zeros = (0,) * a.ndim
    return (a, a.shape, lambda i: zeros)


def _rout(t, w, dtype, tm):
    return ((t, w), dtype, (tm, w), lambda i: (i, 0))


def _rowwise(fn, t, tm, ins, outs, accs, name, comm=None, at=None):
    return _blocked(fn, (t // tm,), ins, outs, accs, name, comm=comm, at=at)


def _rms(x, w):
    r = lax.rsqrt(jnp.mean(x * x, axis=-1, keepdims=True) + EPS)
    return x * r * w


def _rms_bwd(x, w, dy):
    r = lax.rsqrt(jnp.mean(x * x, axis=-1, keepdims=True) + EPS)
    xh = x * r
    dyw = dy * w
    dx = r * (dyw - xh * jnp.mean(dyw * xh, axis=-1, keepdims=True))
    return dx, jnp.sum(dy * xh, axis=0, keepdims=True)


def _silu_grad(x):
    s = _sigmoid(x)
    return s * (1.0 + x * (1.0 - s))


def _rope(blk, cosp, sina, sinb):
    return blk * cosp + pltpu.roll(blk, 96, 1) * sina + pltpu.roll(blk, 32, 1) * sinb


def _rope_bwd(dy, cosp, sina, sinb):
    return dy * cosp + pltpu.roll(dy * sina, 32, 1) + pltpu.roll(dy * sinb, 96, 1)


HALO = 8


def _conv_taps(buf, w, tm, base):
    acc = buf[base:base + tm, :] * w[0:1]
    for k in range(1, SSD_K):
        acc = acc + buf[base + k:base + k + tm, :] * w[k:k + 1]
    return acc


def _conv_specs(t, tm, cdim):
    cur = pl.BlockSpec((tm, cdim), lambda i: (i, 0))
    prev = pl.BlockSpec((HALO, cdim), lambda i: (jnp.maximum(i * (tm // HALO) - 1, 0), 0))
    nxt = pl.BlockSpec((HALO, cdim), lambda i: (jnp.minimum((i + 1) * (tm // HALO), t // HALO - 1), 0))
    return cur, prev, nxt


def _conv_fwd(src, w, b, cdim, tm):
    t = src.shape[0]
    cur, prev, _ = _conv_specs(t, tm, cdim)

    def body(x_ref, p_ref, w_ref, b_ref, o_ref, buf):
        buf[0:HALO, :] = jnp.where(pl.program_id(0) > 0, p_ref[...], 0.0)
        buf[HALO:HALO + tm, :] = x_ref[...]
        pre = _conv_taps(buf, w_ref[...], tm, HALO - (SSD_K - 1)) + b_ref[...]
        o_ref[...] = pre * _sigmoid(pre)

    par = lambda a: pl.BlockSpec(a.shape, lambda i: (0, 0))
    return pl.pallas_call(
        body, name="conv_silu", out_shape=jax.ShapeDtypeStruct((t, cdim), f32), grid=(t // tm,),
        in_specs=[cur, prev, par(w), par(b)], out_specs=cur,
        scratch_shapes=[pltpu.VMEM((tm + HALO, cdim), f32)], compiler_params=_params(("parallel",)),
    )(src, src, w, b)


def _conv_bwd_pre(src, w, b, dact, cdim, tm):
    t = src.shape[0]
    cur, prev, _ = _conv_specs(t, tm, cdim)

    def body(x_ref, p_ref, w_ref, b_ref, d_ref, dpre_ref, dw0, dw1, dw2, dw3, db, buf):
        @pl.when(pl.program_id(0) == 0)
        def _():
            for r in (dw0, dw1, dw2, dw3, db):
                r[...] = jnp.zeros_like(r)

        buf[0:HALO, :] = jnp.where(pl.program_id(0) > 0, p_ref[...], 0.0)
        buf[HALO:HALO + tm, :] = x_ref[...]
        base = HALO - (SSD_K - 1)
        pre = _conv_taps(buf, w_ref[...], tm, base) + b_ref[...]
        dpre = d_ref[...] * _silu_grad(pre)
        dpre_ref[...] = dpre
        for k, r in enumerate((dw0, dw1, dw2, dw3)):
            r[...] += jnp.sum(dpre * buf[base + k:base + k + tm, :], axis=0, keepdims=True)
        db[...] += jnp.sum(dpre, axis=0, keepdims=True)

    par = lambda a: pl.BlockSpec(a.shape, lambda i: (0, 0))
    acc = pl.BlockSpec((1, cdim), lambda i: (0, 0))
    acc_shape = jax.ShapeDtypeStruct((1, cdim), f32)
    return pl.pallas_call(
        body, name="conv_silu_bwd", out_shape=[jax.ShapeDtypeStruct((t, cdim), f32)] + [acc_shape] * 5,
        grid=(t // tm,), in_specs=[cur, prev, par(w), par(b), cur], out_specs=[cur] + [acc] * 5,
        scratch_shapes=[pltpu.VMEM((tm + HALO, cdim), f32)], compiler_params=_params(("arbitrary",)),
    )(src, src, w, b, dact)


def _conv_bwd_dx(dpre, w, tm):
    t, cdim = dpre.shape
    cur, _, nxt = _conv_specs(t, tm, cdim)
    last = t // tm - 1

    def body(d_ref, n_ref, w_ref, o_ref, buf):
        buf[0:tm, :] = d_ref[...]
        buf[tm:tm + HALO, :] = jnp.where(pl.program_id(0) < last, n_ref[...], 0.0)
        wv = w_ref[...]
        acc = buf[0:tm, :] * wv[SSD_K - 1:SSD_K]
        for k in range(SSD_K - 1):
            s = SSD_K - 1 - k
            acc = acc + buf[s:s + tm, :] * wv[k:k + 1]
        o_ref[...] = acc.astype(bf16)

    return pl.pallas_call(
        body, name="conv_bwd_dx", out_shape=jax.ShapeDtypeStruct((t, cdim), bf16), grid=(t // tm,),
        in_specs=[cur, nxt, pl.BlockSpec(w.shape, lambda i: (0, 0))], out_specs=cur,
        scratch_shapes=[pltpu.VMEM((tm + HALO, cdim), f32)], compiler_params=_params(("parallel",)),
    )(dpre, dpre, w)


def _causal_mask(s, row0=0):
    row = lax.broadcasted_iota(jnp.int32, s.shape, 0) + row0
    col = lax.broadcasted_iota(jnp.int32, s.shape, 1)
    return jnp.where(row >= col, s, -jnp.inf)


def _causal_pairs(nq, q_major):
    pairs = [(qi, ki) for qi in range(nq) for ki in range(qi + 1)]
    if not q_major:
        pairs.sort(key=lambda p: (p[1], p[0]))
    return (jnp.asarray([p[0] for p in pairs], jnp.int32), jnp.asarray([p[1] for p in pairs], jnp.int32))


def _flash_fwd(q, kv, kr, nheads, scale, comm=None, at=None):
    t = q.shape[0]
    tq = _tile(t, ATT_TILE_FWD)
    nq = t // tq

    qtab, ktab = _causal_pairs(nq, q_major=True)
    hp = 2 if nheads % 2 == 0 else 1

    def body(qt, kt, q_ref, kv_ref, kr_ref, o_ref, lse_ref, m_sc, l_sc, acc_sc):
        qi, ki = qt[pl.program_id(1)], kt[pl.program_id(1)]

        @pl.when(ki == 0)
        def _():
            m_sc[...] = jnp.full_like(m_sc, -jnp.inf)
            l_sc[...] = jnp.zeros_like(l_sc)
            acc_sc[...] = jnp.zeros_like(acc_sc)

        def step(masked, last):
            krb = kr_ref[...]
            ss = []
            for j in range(hp):
                k = jnp.concatenate([kv_ref[:, j * QK_PAD: j * QK_PAD + NOPE], krb], axis=1)
                ss.append(lax.dot_general(q_ref[:, j * QK_PAD:(j + 1) * QK_PAD], k, NT, preferred_element_type=f32))
            soft = []
            for j in range(hp):
                s = ss[j] * scale
                if masked:
                    s = _causal_mask(s)
                m_old = m_sc[j]
                m_new = jnp.maximum(m_old, jnp.max(s, axis=1, keepdims=True))
                alpha = jnp.exp(m_old - m_new)
                p = jnp.exp(s - m_new)
                soft.append((m_new, alpha, alpha * l_sc[j] + jnp.sum(p, axis=1, keepdims=True), p.astype(bf16)))
            for j in range(hp):
                m_new, alpha, l, p = soft[j]
                v = kv_ref[:, j * QK_PAD + NOPE:(j + 1) * QK_PAD]
                acc = alpha * acc_sc[j] + lax.dot_general(p, v, NN, preferred_element_type=f32)
                if last:
                    o_ref[:, j * V_HEAD:(j + 1) * V_HEAD] = acc / l
                    lse_ref[:, j * V_HEAD:(j + 1) * V_HEAD] = jnp.broadcast_to(m_new + jnp.log(l), (tq, V_HEAD))
                else:
                    l_sc[j] = l
                    acc_sc[j] = acc
                    m_sc[j] = m_new

        @pl.when(ki < qi)
        def _():
            step(False, False)

        @pl.when(ki == qi)
        def _():
            step(True, True)

    o_spec = pl.BlockSpec((tq, hp * V_HEAD), lambda h, s, qt, kt: (qt[s], h))
    out = jax.ShapeDtypeStruct((t, nheads * V_HEAD), f32)
    (o, lse), cres = _call(
        body, "flash_fwd", [out, out], (nheads // hp, len(qtab)),
        [pl.BlockSpec((tq, hp * QK_PAD), lambda h, s, qt, kt: (qt[s], h)),
         pl.BlockSpec((tq, hp * QK_PAD), lambda h, s, qt, kt: (kt[s], h)),
         pl.BlockSpec((tq, LANES), lambda h, s, qt, kt: (kt[s], 0))],
        [o_spec, o_spec],
        [pltpu.VMEM((hp, tq, 1), f32), pltpu.VMEM((hp, tq, 1), f32), pltpu.VMEM((hp, tq, V_HEAD), f32)],
        ("parallel", "arbitrary"), [q, kv, kr], comm, at, prefetch=(qtab, ktab))
    return o, lse, cres


def _flash_bwd(q, kv, kr, do, lse, delta, nheads, scale, comm=None, at=None):
    t = q.shape[0]
    tq = _tile(t, ATT_TILE)
    nq = t // tq
    qtab, ktab = _causal_pairs(nq, q_major=False)
    nsub = 2 if tq % 32 == 0 else 1

    def body(qt, kt, q_ref, kn_ref, kr_ref, v_ref, do_ref, lse_ref, dl_ref, dkv_ref, dkr_ref, dq_ref, dk_sc, dv_sc):
        qi, ki = qt[pl.program_id(1)], kt[pl.program_id(1)]

        @pl.when(pl.program_id(1) == 0)
        def _():
            dq_ref[...] = jnp.zeros_like(dq_ref)

        @pl.when(qi == ki)
        def _():
            dk_sc[...] = jnp.zeros_like(dk_sc)
            dv_sc[...] = jnp.zeros_like(dv_sc)

        def step(masked):
            k = jnp.concatenate([kn_ref[...], kr_ref[...]], axis=1)
            vb = v_ref[...]
            parts = [slice(i * (tq // nsub), (i + 1) * (tq // nsub)) for i in range(nsub)]
            qs = [q_ref[r, :] for r in parts]
            dos = [do_ref[r, :] for r in parts]
            ss = [lax.dot_general(qb, k, NT, preferred_element_type=f32) for qb in qs]
            dps = [lax.dot_general(dob, vb, NT, preferred_element_type=f32) for dob in dos]
            ps, dss = [], []
            for r, s, dp in zip(parts, ss, dps):
                s = s * scale
                if masked:
                    s = _causal_mask(s, r.start)
                p = jnp.exp(s - lse_ref[r, 0:1])
                ps.append(p.astype(bf16))
                dss.append((p * (dp - dl_ref[r, 0:1]) * scale).astype(bf16))
            for r, qb, dob, p, ds in zip(parts, qs, dos, ps, dss):
                dv_sc[...] += lax.dot_general(p, dob, TN, preferred_element_type=f32)
                dk_sc[...] += lax.dot_general(ds, qb, TN, preferred_element_type=f32)
                rows = pl.ds(pl.multiple_of(qi * tq + r.start, tq // nsub), tq // nsub)
                dq_ref[rows, :] += lax.dot_general(ds, k, NN, preferred_element_type=f32)

        @pl.when(qi > ki)
        def _():
            step(False)

        @pl.when(qi == ki)
        def _():
            step(True)

        @pl.when(qi == nq - 1)
        def _():
            dk = dk_sc[...]
            dkv_ref[...] = jnp.concatenate([dk[:, :NOPE], dv_sc[...]], axis=1).astype(bf16)
            dkr_ref[...] = dk[:, NOPE:]

    hspec = pl.BlockSpec((tq, V_HEAD), lambda h, s, qt, kt: (qt[s], h))
    (dkv, dkr, dq), cres = _call(
        body, "flash_bwd",
        [jax.ShapeDtypeStruct((t, nheads * QK_PAD), bf16), jax.ShapeDtypeStruct((t, nheads * LANES), f32),
         jax.ShapeDtypeStruct((t, nheads * QK_PAD), f32)],
        (nheads, len(qtab)),
        [pl.BlockSpec((tq, QK_PAD), lambda h, s, qt, kt: (qt[s], h)),
         pl.BlockSpec((tq, NOPE), lambda h, s, qt, kt: (kt[s], 2 * h)),
         pl.BlockSpec((tq, LANES), lambda h, s, qt, kt: (kt[s], 0)),
         pl.BlockSpec((tq, V_HEAD), lambda h, s, qt, kt: (kt[s], 2 * h + 1)),
         hspec, hspec, hspec],
        [pl.BlockSpec((tq, QK_PAD), lambda h, s, qt, kt: (kt[s], h)),
         pl.BlockSpec((tq, LANES), lambda h, s, qt, kt: (kt[s], h)),
         pl.BlockSpec((t, QK_PAD), lambda h, s, qt, kt: (0, h))],
        [pltpu.VMEM((tq, QK_PAD), f32), pltpu.VMEM((tq, V_HEAD), f32)],
        ("parallel", "arbitrary"), [q, kv, kr, kv, do, lse, delta], comm, at, prefetch=(qtab, ktab))
    return dkv, dkr, dq, cres


def _split3(a):
    hi = a.astype(bf16)
    r1 = a - hi.astype(f32)
    mid = r1.astype(bf16)
    lo = (r1 - mid.astype(f32)).astype(bf16)
    return hi, mid, lo


def _split2(a):
    hi = a.astype(bf16)
    return hi, (a - hi.astype(f32)).astype(bf16)


def _ones_dot_left(tri, a):
    hi, mid, lo = _split3(a)
    d = lambda v: lax.dot_general(tri, v, NN, preferred_element_type=f32)
    return d(hi) + d(mid) + d(lo)


def _ones_dot_right(a, tri):
    hi, mid, lo = _split3(a)
    d = lambda v: lax.dot_general(v, tri, NN, preferred_element_type=f32)
    return d(hi) + d(mid) + d(lo)


def _softplus(x):
    return jnp.maximum(x, 0.0) + jnp.log(1.0 + jnp.exp(-jnp.abs(x)))


def _head_spread(hs, width):
    shift = SSD_P.bit_length() - 1
    return (lax.broadcasted_iota(jnp.int32, (hs, width), 0)
            == lax.shift_right_logical(lax.broadcasted_iota(jnp.int32, (hs, width), 1), shift)).astype(bf16)


def _ssd_common(dt_ref, dtT_ref, dtb_ref, dtbT_ref, alog_ref, alogT_ref):
    ii = lax.broadcasted_iota(jnp.int32, (CHUNK, CHUNK), 0)
    jj = lax.broadcasted_iota(jnp.int32, (CHUNK, CHUNK), 1)
    tri = ii >= jj
    raw = dt_ref[...] + dtb_ref[...]
    dt = _softplus(raw)
    a_neg = -jnp.exp(alog_ref[...])
    cum = _ones_dot_left(tri.astype(bf16), dt * a_neg)
    dt_t = _softplus(dtT_ref[...] + dtbT_ref[...])
    cum_t = _ones_dot_right(dt_t * (-jnp.exp(alogT_ref[...])), (ii <= jj).astype(bf16))
    return tri, raw, dt, a_neg, cum, cum_t


def _ssd_fwd(xbc, dt, dt_t, dtb, dtb_t, alog, alog_t, dskip, width, comm=None, at=None):
    t, cdim = xbc.shape
    hs = dt.shape[1]
    nc = t // CHUNK
    epg = hs // SSD_G
    gn = SSD_G * SSD_N

    def body(x_ref, dt_ref, dtT_ref, dtb_ref, dtbT_ref, alog_ref, alogT_ref, d_ref, y_ref, hp_ref, h_sc, yd_sc):
        @pl.when(pl.program_id(0) == 0)
        def _():
            h_sc[...] = jnp.zeros_like(h_sc)

        tri, _, dtv, _, cum, cum_t = _ssd_common(dt_ref, dtT_ref, dtb_ref, dtbT_ref, alog_ref, alogT_ref)
        spread = _head_spread(hs, width)
        clast = cum[CHUNK - 1:CHUNK, :]
        dec = jnp.exp(clast)
        wide = _ones_dot_right(jnp.concatenate([dtv, jnp.exp(cum), jnp.exp(clast - cum),
                                                jnp.broadcast_to(d_ref[...], (CHUNK, hs))], axis=0), spread)
        dt_x, ee_x, ff_x, dsk_x = (wide[i * CHUNK:(i + 1) * CHUNK] for i in range(4))
        xs = x_ref[:, :width]
        xdt = xs * dt_x
        xf = xdt * ff_x
        h_all = h_sc[...]
        hp_ref[0] = h_all
        h_all = h_all.reshape(hs * SSD_P, SSD_N)
        ch_parts = []
        for g in range(SSD_G):
            gsl = slice(g * epg * SSD_P, (g + 1) * epg * SSD_P)
            bb = x_ref[:, width + g * SSD_N: width + (g + 1) * SSD_N].astype(bf16)
            cb_ = x_ref[:, width + gn + g * SSD_N: width + gn + (g + 1) * SSD_N].astype(bf16)
            cbm = lax.dot_general(cb_, bb, NT, preferred_element_type=f32)
            ch_parts.append(_dot(cb_, h_all[gsl], NT))
            st = _dot(xf[:, gsl], bb, TN)
            for e in range(g * epg, (g + 1) * epg):
                esl = slice(e * SSD_P, (e + 1) * SSD_P)
                lmat = jnp.exp(jnp.where(tri, cum[:, e:e + 1] - cum_t[e:e + 1, :], -jnp.inf))
                yd_sc[:, esl] = _dot(cbm * lmat, xdt[:, esl], NN)
                j = e - g * epg
                h_sc[e] = h_sc[e] * dec[:, e:e + 1] + st[j * SSD_P:(j + 1) * SSD_P, :]
        y_ref[...] = yd_sc[...] + jnp.concatenate(ch_parts, axis=1) * ee_x + xs * dsk_x

    par = lambda a: pl.BlockSpec(a.shape, lambda i: (0,) * a.ndim)
    (y, hp), cres = _call(
        body, "ssd_fwd",
        [jax.ShapeDtypeStruct((t, width), f32), jax.ShapeDtypeStruct((nc, hs, SSD_P, SSD_N), f32)], (nc,),
        [pl.BlockSpec((CHUNK, cdim), lambda i: (i, 0)), pl.BlockSpec((CHUNK, hs), lambda i: (i, 0)),
         pl.BlockSpec((hs, CHUNK), lambda i: (0, i)), par(dtb), par(dtb_t), par(alog), par(alog_t), par(dskip)],
        [pl.BlockSpec((CHUNK, width), lambda i: (i, 0)), pl.BlockSpec((1, hs, SSD_P, SSD_N), lambda i: (i, 0, 0, 0))],
        [pltpu.VMEM((hs, SSD_P, SSD_N), f32), pltpu.VMEM((CHUNK, width), f32)], ("arbitrary",),
        [xbc, dt, dt_t, dtb, dtb_t, alog, alog_t, dskip], comm, at)
    return y, hp, cres


def _ssd_bwd(xbc, dt, dt_t, dtb, dtb_t, alog, alog_t, dskip, hprev, dy, width, comm=None, at=None):
    t, cdim = xbc.shape
    hs = dt.shape[1]
    nc = t // CHUNK
    epg = hs // SSD_G
    gn = SSD_G * SSD_N

    def body(x_ref, dt_ref, dtT_ref, dtb_ref, dtbT_ref, alog_ref, alogT_ref, d_ref, hp_ref, dy_ref,
             dx_ref, ddt_ref, dalog_ref, ddsk_ref, ddtb_ref, dh_sc, dxd_sc):
        @pl.when(pl.program_id(0) == 0)
        def _():
            dh_sc[...] = jnp.zeros_like(dh_sc)
            dalog_ref[...] = jnp.zeros_like(dalog_ref)
            ddsk_ref[...] = jnp.zeros_like(ddsk_ref)
            ddtb_ref[...] = jnp.zeros_like(ddtb_ref)

        tri, raw, dtv, a_neg, cum, cum_t = _ssd_common(dt_ref, dtT_ref, dtb_ref, dtbT_ref, alog_ref, alogT_ref)
        dsk = d_ref[...]
        head_row = lax.broadcasted_iota(jnp.int32, (1, hs), 1)
        head_col = lax.broadcasted_iota(jnp.int32, (hs, 1), 0)
        last_row = (lax.broadcasted_iota(jnp.int32, (CHUNK, 1), 0) == CHUNK - 1).astype(f32)
        shift = SSD_P.bit_length() - 1
        spread = _head_spread(hs, width)
        gather = (lax.shift_right_logical(lax.broadcasted_iota(jnp.int32, (width, hs), 0), shift)
                  == lax.broadcasted_iota(jnp.int32, (width, hs), 1)).astype(bf16)
        clast = cum[CHUNK - 1:CHUNK, :]
        ee = jnp.exp(cum)
        ff = jnp.exp(clast - cum)
        dec = jnp.exp(clast)
        wide = _ones_dot_right(jnp.concatenate([dtv, ee, ff, jnp.broadcast_to(dsk, (CHUNK, hs))], axis=0), spread)
        dt_x, ee_x, ff_x, dsk_x = (wide[i * CHUNK:(i + 1) * CHUNK] for i in range(4))
        xs = x_ref[:, :width]
        dyv = dy_ref[...]
        xdt = xs * dt_x
        dye = dyv * ee_x
        xf = xdt * ff_x
        h_all = hp_ref[0].reshape(hs * SSD_P, SSD_N)
        dh_all = dh_sc[...].reshape(hs * SSD_P, SSD_N)
        hi, lo = _split2(_ones_dot_left(spread, dh_all * h_all))
        ones8 = jnp.ones((8, SSD_N), bf16)
        hh = (lax.dot_general(ones8, hi, NT, preferred_element_type=f32)
              + lax.dot_general(ones8, lo, NT, preferred_element_type=f32))[0:1]
        rowsum_m = jnp.zeros((CHUNK, hs), f32)
        colsum_m = jnp.zeros((hs, CHUNK), f32)
        ch_parts, bds_parts = [], []
        for g in range(SSD_G):
            bsl = slice(width + g * SSD_N, width + (g + 1) * SSD_N)
            csl = slice(width + gn + g * SSD_N, width + gn + (g + 1) * SSD_N)
            gsl = slice(g * epg * SSD_P, (g + 1) * epg * SSD_P)
            bb = x_ref[:, bsl].astype(bf16)
            cb_ = x_ref[:, csl].astype(bf16)
            cbm = lax.dot_general(cb_, bb, NT, preferred_element_type=f32)
            hg = h_all[gsl].astype(bf16)
            dhg = dh_all[gsl].astype(bf16)
            ch_parts.append(_dot(cb_, hg, NT))
            bds_parts.append(_dot(bb, dhg, NT))
            dcg = _dot(dye[:, gsl], hg, NN)
            dbg = _dot(xf[:, gsl], dhg, NN)
            dh_new = _dot(dye[:, gsl], cb_, TN)
            dcb = jnp.zeros((CHUNK, CHUNK), f32)
            for e in range(g * epg, (g + 1) * epg):
                esl = slice(e * SSD_P, (e + 1) * SSD_P)
                lmat = jnp.exp(jnp.where(tri, cum[:, e:e + 1] - cum_t[e:e + 1, :], -jnp.inf))
                gmat = cbm * lmat
                dy_e = dyv[:, esl].astype(bf16)
                dgm = _dot(dy_e, xdt[:, esl], NT)
                dxd_sc[:, esl] = _dot(gmat, dy_e, TN)
                dcb = dcb + dgm * lmat
                mm = dgm * gmat
                rowsum_m = rowsum_m + jnp.sum(mm, axis=1, keepdims=True) * (head_row == e).astype(f32)
                colsum_m = colsum_m + (head_col == e).astype(f32) * jnp.sum(mm, axis=0, keepdims=True)
                j = e - g * epg
                dh_sc[e] = dh_new[j * SSD_P:(j + 1) * SSD_P, :] + dec[:, e:e + 1] * dh_sc[e]
            dx_ref[:, bsl] = dbg + _dot(dcb, cb_, TN)
            dx_ref[:, csl] = dcg + _dot(dcb, bb, NN)
        ch_all = jnp.concatenate(ch_parts, axis=1)
        bds_all = jnp.concatenate(bds_parts, axis=1)
        dxdt = bds_all * ff_x + dxd_sc[...]
        dx_ref[:, :width] = dxdt * dt_x + dyv * dsk_x
        sums = _ones_dot_right(jnp.concatenate([dxdt * xs, dyv * ch_all, bds_all * xdt, dyv * xs], axis=0), gather)
        ddtx_all = sums[0:CHUNK]
        dff = sums[2 * CHUNK:3 * CHUNK] * ff
        ddsk = jnp.sum(sums[3 * CHUNK:4 * CHUNK], axis=0, keepdims=True)
        dclast = jnp.sum(dff, axis=0, keepdims=True) + dec * hh
        eye = (lax.broadcasted_iota(jnp.int32, (hs, hs), 0) == lax.broadcasted_iota(jnp.int32, (hs, hs), 1)).astype(bf16)
        colsum_t = sum(lax.dot_general(v, eye, TN, preferred_element_type=f32) for v in _split3(colsum_m))
        dcum_all = sums[CHUNK:2 * CHUNK] * ee - dff + rowsum_m - colsum_t + dclast * last_row
        ii = lax.broadcasted_iota(jnp.int32, (CHUNK, CHUNK), 0)
        jj = lax.broadcasted_iota(jnp.int32, (CHUNK, CHUNK), 1)
        da = _ones_dot_left((jj >= ii).astype(bf16), dcum_all)
        ddt = da * a_neg + ddtx_all
        dalog_ref[...] += jnp.sum(da * dtv, axis=0, keepdims=True) * a_neg
        draw = ddt * _sigmoid(raw)
        place = (lax.broadcasted_iota(jnp.int32, (hs, LANES), 0) + ROPE
                 == lax.broadcasted_iota(jnp.int32, (hs, LANES), 1)).astype(bf16)
        ddt_ref[...] = _ones_dot_right(draw, place)
        ddtb_ref[...] += jnp.sum(draw, axis=0, keepdims=True)
        ddsk_ref[...] += ddsk

    rev = lambda i: nc - 1 - i
    par = lambda a: pl.BlockSpec(a.shape, lambda i: (0,) * a.ndim)
    acc = pl.BlockSpec((1, hs), lambda i: (0, 0))
    acc_shape = jax.ShapeDtypeStruct((1, hs), f32)
    res, cres = _call(
        body, "ssd_bwd",
        [jax.ShapeDtypeStruct((t, cdim), f32), jax.ShapeDtypeStruct((t, LANES), f32), acc_shape, acc_shape, acc_shape],
        (nc,),
        [pl.BlockSpec((CHUNK, cdim), lambda i: (rev(i), 0)), pl.BlockSpec((CHUNK, hs), lambda i: (rev(i), 0)),
         pl.BlockSpec((hs, CHUNK), lambda i: (0, rev(i))), par(dtb), par(dtb_t), par(alog), par(alog_t),
         par(dskip), pl.BlockSpec((1, hs, SSD_P, SSD_N), lambda i: (rev(i), 0, 0, 0)),
         pl.BlockSpec((CHUNK, width), lambda i: (rev(i), 0))],
        [pl.BlockSpec((CHUNK, cdim), lambda i: (rev(i), 0)), pl.BlockSpec((CHUNK, LANES), lambda i: (rev(i), 0)),
         acc, acc, acc],
        [pltpu.VMEM((hs, SSD_P, SSD_N), f32), pltpu.VMEM((CHUNK, width), f32)], ("arbitrary",),
        [xbc, dt, dt_t, dtb, dtb_t, alog, alog_t, dskip, hprev, dy], comm, at)
    return (*res, cres)


def _where_am_i():
    x, y, c = lax.axis_index("x"), lax.axis_index("y"), lax.axis_index("c")
    chips = [(1 - x, y), (x, 1 - y), (1 - x, 1 - y)]
    return x, y, c, chips


def _remote(src, dst, send_sems, recv_sems, k, to):
    return pltpu.make_async_remote_copy(src_ref=src, dst_ref=dst, send_sem=send_sems.at[k], recv_sem=recv_sems.at[k],
                                        device_id=to, device_id_type=MESH)


class _Comm:
    def __init__(self, args, outs, nsem, nphase, run):
        self.args, self.outs, self.nsem, self.nphase, self.run = list(args), list(outs), nsem, nphase, run

    def sems(self):
        return [pltpu.SemaphoreType.DMA((self.nsem,)), pltpu.SemaphoreType.DMA((self.nsem,))]


class _SemWindow:
    def __init__(self, ref, off):
        self.ref, self.off, self.at = ref, off, self

    def __getitem__(self, k):
        return self.ref.at[k + self.off]


def _both(a, b):
    assert a.nphase == b.nphase
    na, no = len(a.args), len(a.outs)

    def run(phase, srcs, outs, send_sems, recv_sems):
        a.run(phase, srcs[:na], outs[:no], send_sems, recv_sems)
        b.run(phase, srcs[na:], outs[no:], _SemWindow(send_sems, a.nsem), _SemWindow(recv_sems, a.nsem))

    return _Comm(a.args + b.args, a.outs + b.outs, a.nsem + b.nsem, a.nphase, run)


def _comm_call(comm, name):
    na, no = len(comm.args), len(comm.outs)

    def body(*refs):
        for phase in range(comm.nphase):
            comm.run(phase, refs[:na], refs[na:na + no], refs[na + no], refs[na + no + 1])

    return list(pl.pallas_call(body, name=name, out_shape=comm.outs, in_specs=[ANY] * na, out_specs=[ANY] * no,
                               scratch_shapes=comm.sems())(*comm.args))


def _half(ref, c, r2):
    return ref.at[pl.ds(c * r2, r2)]


def _gather_weights(shards, wholes):
    ns, nw = len(shards), len(wholes)
    per = 7

    def run(phase, srcs, outs, send_sems, recv_sems):
        x, y, c, chips = _where_am_i()
        me, sib = 2 * x + y, (x, y, 1 - c)
        def first():
            cps = []
            for w in range(ns + nw):
                src, out, base = srcs[w], outs[w], per * w
                halved = w < ns
                r2 = src.shape[0] // 2
                piece = _half(src, c, r2) if halved else src
                for j, (cx, cy) in enumerate(chips):
                    dst = _half(out.at[me], c, r2) if halved else out.at[me]
                    cps.append(_remote(piece, dst, send_sems, recv_sems, base + j, (cx, cy, c)))
                cps.append(_remote(src, out.at[me], send_sems, recv_sems, base + 6, sib))
            return cps

        def passed():
            cps = []
            for w in range(ns):
                r2 = srcs[w].shape[0] // 2
                for j, (cx, cy) in enumerate(chips):
                    got = _half(outs[w].at[2 * cx + cy], c, r2)
                    cps.append(_remote(got, got, send_sems, recv_sems, per * w + 3 + j, sib))
            return cps

        if phase == 0:
            for cp in first():
                cp.start()
        elif phase == 1:
            it = iter(passed())
            for w in range(ns + nw):
                src, out, base = srcs[w], outs[w], per * w
                r2 = src.shape[0] // 2
                for j, (cx, cy) in enumerate(chips):
                    got = _half(out.at[2 * cx + cy], c, r2) if w < ns else out.at[2 * cx + cy]
                    _remote(got, got, send_sems, recv_sems, base + j, sib).wait_recv()
                    if w < ns:
                        next(it).start()
        else:
            for w in range(ns + nw):
                src, out, base = srcs[w], outs[w], per * w
                r2 = src.shape[0] // 2
                if w < ns:
                    for j, (cx, cy) in enumerate(chips):
                        got = _half(out.at[2 * cx + cy], 1 - c, r2)
                        _remote(got, got, send_sems, recv_sems, base + 3 + j, sib).wait_recv()
                _remote(src, out.at[me], send_sems, recv_sems, base + 6, sib).wait_recv()
            for cp in first() + passed():
                cp.wait_send()

    args = list(shards) + list(wholes)
    outs = [jax.ShapeDtypeStruct((4,) + a.shape, a.dtype) for a in args]
    return _Comm(args, outs, per * len(args), 3, run)


def _sibling_send_halves(gs):
    n = len(gs)

    def run(phase, srcs, outs, send_sems, recv_sems):
        x, y, c, _ = _where_am_i()
        sib = (x, y, 1 - c)
        cps = []
        for w in range(n):
            r2 = srcs[w].shape[1] // 2
            for k in range(4):
                cps.append(_remote(_half(srcs[w].at[k], 1 - c, r2), outs[w].at[k], send_sems, recv_sems, 4 * w + k, sib))
        for cp in cps:
            cp.start() if phase == 0 else cp.wait()

    outs = [jax.ShapeDtypeStruct((4, g.shape[1] // 2, g.shape[2]), g.dtype) for g in gs]
    return _Comm(gs, outs, 4 * n, 2, run)


def _chips_exchange(ps):
    n = len(ps)

    def run(phase, srcs, outs, send_sems, recv_sems):
        x, y, c, chips = _where_am_i()
        me = 2 * x + y
        cps = []
        for w in range(n):
            for j, (cx, cy) in enumerate(chips):
                cps.append(_remote(srcs[w].at[2 * cx + cy], outs[w].at[me], send_sems, recv_sems, 3 * w + j, (cx, cy, c)))
        if phase == 0:
            for cp in cps:
                cp.start()
        else:
            for w in range(n):
                for j, (cx, cy) in enumerate(chips):
                    got = outs[w].at[2 * cx + cy]
                    _remote(got, got, send_sems, recv_sems, 3 * w + j, (cx, cy, c)).wait_recv()
            for cp in cps:
                cp.wait_send()

    outs = [jax.ShapeDtypeStruct(p.shape, p.dtype) for p in ps]
    return _Comm(ps, outs, 3 * n, 2, run)


def _sibling_swap(rs):
    n = len(rs)

    def run(phase, srcs, outs, send_sems, recv_sems):
        x, y, c, _ = _where_am_i()
        for w in range(n):
            cp = _remote(srcs[w], outs[w], send_sems, recv_sems, w, (x, y, 1 - c))
            cp.start() if phase == 0 else cp.wait()

    outs = [jax.ShapeDtypeStruct(r.shape, r.dtype) for r in rs]
    return _Comm(rs, outs, n, 2, run)


def _gather_all(v):
    rows = v.shape[0]

    def body(x_ref, out_ref, send_sems, recv_sems, local_sem):
        x, y, c, chips = _where_am_i()
        me, sib = (x, y, c), (x, y, 1 - c)
        blk = lambda px, py, pc: out_ref.at[4 * px + 2 * py + pc]
        mine = pltpu.make_async_copy(x_ref, blk(*me), local_sem)
        mine.start()
        first = [_remote(x_ref, blk(*me), send_sems, recv_sems, 0, sib)]
        first += [_remote(x_ref, blk(*me), send_sems, recv_sems, 1 + j, (*chip, c)) for j, chip in enumerate(chips)]
        for cp in first:
            cp.start()
        passed = [_remote(blk(*chip, c), blk(*chip, c), send_sems, recv_sems, 4 + j, sib) for j, chip in enumerate(chips)]
        for j, chip in enumerate(chips):
            _remote(blk(*chip, c), blk(*chip, c), send_sems, recv_sems, 1 + j, me).wait_recv()
            passed[j].start()
        _remote(blk(*sib), blk(*sib), send_sems, recv_sems, 0, me).wait_recv()
        for j, chip in enumerate(chips):
            _remote(blk(*chip, 1 - c), blk(*chip, 1 - c), send_sems, recv_sems, 4 + j, me).wait_recv()
        for cp in first + passed:
            cp.wait_send()
        mine.wait()

    vm = pl.BlockSpec(memory_space=pltpu.VMEM)
    return pl.pallas_call(
        body, name="gather_all", out_shape=jax.ShapeDtypeStruct((8, rows, LANES), v.dtype),
        in_specs=[vm], out_specs=vm,
        scratch_shapes=[pltpu.SemaphoreType.DMA((7,)), pltpu.SemaphoreType.DMA((7,)), pltpu.SemaphoreType.DMA],
    )(v)


def _flat_pad(parts, total):
    v = jnp.concatenate([p.reshape(-1) for p in parts])
    return jnp.pad(v, (0, total - v.shape[0]))


def _adamw(w, g, m, v):
    m = ADAM_B1 * m + (1.0 - ADAM_B1) * g
    v = ADAM_B2 * v + (1.0 - ADAM_B2) * jnp.square(g)
    m_hat = m / (1.0 - ADAM_B1 ** ADAM_STEP)
    v_hat = v / (1.0 - ADAM_B2 ** ADAM_STEP)
    delta = -ADAM_LR * (m_hat / (jnp.sqrt(v_hat) + ADAM_EPS) + ADAM_WD * w)
    return delta, m, v


def _adamw_call(w, g, m, v, name):
    r, cdim = w.shape
    tm = _tile(r, ROW_TILE, 8)
    o = _rout(r, cdim, f32, tm)
    return _rowwise(_adamw, r, tm, [_rows(w, tm), _rows(g, tm), _rows(m, tm), _rows(v, tm)], [o, o, o], [], name)


def _adamw_halves(w, m, v, g_mine, g_sib, sp, name):
    _, r, cdim = w.shape
    r2 = r // 2
    tr = _row_tile(r2, cdim, 8)
    nb = r2 // tr

    def body(sp_ref, w_ref, m_ref, v_ref, ga_ref, gb_ref, g_out, d_out, m_out, v_out):
        g = jnp.where(pl.program_id(0) == sp_ref[1], ga_ref[...], gb_ref[...])
        delta, mn, vn = _adamw(w_ref[...], g, m_ref[...], v_ref[...])
        g_out[...] = g
        d_out[...] = delta
        m_out[...] = mn
        v_out[...] = vn

    full = pl.BlockSpec((None, tr, cdim), lambda h, i, s: (0, h * nb + i, 0))
    mine = pl.BlockSpec((tr, cdim), lambda h, i, s: (jnp.where(h == s[1], i, 0), 0))
    sib = pl.BlockSpec((tr, cdim), lambda h, i, s: (jnp.where(h == s[1], 0, i), 0))
    out = jax.ShapeDtypeStruct((1, r, cdim), f32)
    gs = pltpu.PrefetchScalarGridSpec(num_scalar_prefetch=1, grid=(2, nb), in_specs=[full, full, full, mine, sib],
                                      out_specs=[full, full, full, full])
    return pl.pallas_call(body, name=name, out_shape=[out, out, out, out], grid_spec=gs,
                          compiler_params=_params(("parallel", "parallel")))(sp, w, m, v, g_mine, g_sib)


MIXER = ("w_in", "w_uq", "w_ukv", "w_out")
FFN = ("w_gate", "w_up", "w_down")
BIG = MIXER + FFN
SMALL = ("q_norm_w", "kv_norm_w", "conv_w", "conv_b", "dt_bias", "a_log", "d_skip", "ssd_norm_w", "attn_out_norm_w",
         "pre_mix_norm_w", "post_mix_norm_w", "pre_ffn_norm_w", "post_ffn_norm_w")
ORDER = ("w_in", "q_norm_w", "w_uq", "kv_norm_w", "w_ukv", "conv_w", "conv_b", "dt_bias", "a_log", "d_skip",
         "ssd_norm_w", "attn_out_norm_w", "w_out", "pre_mix_norm_w", "post_mix_norm_w", "pre_ffn_norm_w",
         "post_ffn_norm_w", "w_gate", "w_up", "w_down")


def kernel(x, positions, w_in, q_norm_w, w_uq, kv_norm_w, w_ukv, conv_w, conv_b, dt_bias, a_log, d_skip, ssd_norm_w, attn_out_norm_w, w_out, pre_mix_norm_w, post_mix_norm_w, pre_ffn_norm_w, post_ffn_norm_w, w_gate, w_up, w_down, loss_target, m_w_in, m_q_norm_w, m_w_uq, m_kv_norm_w, m_w_ukv, m_conv_w, m_conv_b, m_dt_bias, m_a_log, m_d_skip, m_ssd_norm_w, m_attn_out_norm_w, m_w_out, m_pre_mix_norm_w, m_post_mix_norm_w, m_pre_ffn_norm_w, m_post_ffn_norm_w, m_w_gate, m_w_up, m_w_down, v_w_in, v_q_norm_w, v_w_uq, v_kv_norm_w, v_w_ukv, v_conv_w, v_conv_b, v_dt_bias, v_a_log, v_d_skip, v_ssd_norm_w, v_attn_out_norm_w, v_w_out, v_pre_mix_norm_w, v_post_mix_norm_w, v_pre_ffn_norm_w, v_post_ffn_norm_w, v_w_gate, v_w_up, v_w_down):
    local = dict(locals())
    wts = {n: local[n][0] for n in ORDER}
    mom_m = {n: local["m_" + n][0] for n in ORDER}
    mom_v = {n: local["v_" + n][0] for n in ORDER}
    xs = x[0]
    tgt = loss_target[0]
    t, d = xs.shape
    nchip = 4
    my_x, my_y, my_c = lax.axis_index("x"), lax.axis_index("y"), lax.axis_index("c")
    my_chip = 2 * my_x + my_y

    mla_w = d // 2
    nh = mla_w // V_HEAD
    width = d - mla_w
    hs = width // SSD_P
    gn = SSD_G * SSD_N
    cdim = width + 2 * gn
    in_sizes = (Q_RANK, KV_RANK, ROPE, width, cdim, hs)
    d_in = sum(in_sizes)
    tail = LANES
    off_xbc = 0
    off_cq = cdim
    off_ckv = off_cq + Q_RANK
    off_z = off_ckv + KV_RANK
    off_tail = off_z + width
    d_in_p = _round_up(off_tail + tail, 256)
    gw = width // SSD_G
    assert off_cq % Q_RANK == 0 and off_ckv % KV_RANK == 0 and off_z % gw == 0 and off_tail % LANES == 0
    qk_head = NOPE + ROPE
    scale = qk_head ** -0.5
    tm = _tile(t, ROW_TILE, 8)
    tmw = _tile(t, ROW_TILE // 2, 8)

    sp = jnp.stack([my_chip, my_c]).astype(jnp.int32)

    def pair_sums(names, gl, from_sib):
        res = []
        for n, g, fs in zip(names, gl, from_sib):
            _, r2, cs = fs.shape
            tr = _row_tile(r2, cs, 16)
            nb = r2 // tr
            res.append(_blocked(
                lambda a, b: a + b, (nchip, nb),
                [(g, (None, tr, cs), lambda k, i, s, nb=nb: (k, s[1] * nb + i, 0)),
                 (fs, (None, tr, cs), lambda k, i, s: (k, i, 0))],
                [((nchip, r2, cs), bf16, (None, tr, cs), lambda k, i, s: (k, i, 0))], [], "pair_sum_" + n, sp=sp)[0])
        return res

    def chip_sums(names, gl, from_sib, from_chips):
        res = []
        for n, g, fs, fc in zip(names, gl, from_sib, from_chips):
            _, r2, cs = fs.shape
            tr = _row_tile(r2, cs, 16)
            nb = r2 // tr
            res.append(_blocked(
                lambda a, b, r1, r2_, r3: ((a + b) + r1.astype(f32)) + r2_.astype(f32) + r3.astype(f32), (nb,),
                [(g, (None, tr, cs), lambda i, s, nb=nb: (s[0], s[1] * nb + i, 0)),
                 (fs, (None, tr, cs), lambda i, s: (s[0], i, 0)),
                 (fc, (None, tr, cs), lambda i, s: (s[0] ^ 1, i, 0)),
                 (fc, (None, tr, cs), lambda i, s: (s[0] ^ 2, i, 0)),
                 (fc, (None, tr, cs), lambda i, s: (s[0] ^ 3, i, 0))],
                [((r2, cs), f32, (tr, cs), lambda i, s: (i, 0))], [], "chip_sum_" + n, sp=sp)[0])
        return res

    ck, ccs = wts["conv_w"].shape
    cs_in = wts["w_in"].shape[1]
    flat_t = lambda a3: jnp.swapaxes(a3, 1, 2).reshape(1, cs_in * d // LANES, LANES)
    (u,), (w_in_g, conv_g) = _rowwise(
        lambda a, w: _rms(a, w), t, tm, [_rows(xs, tm), _par(wts["pre_mix_norm_w"].reshape(1, -1))],
        [_rout(t, d, bf16, tm)], [], "pre_mix_norm",
        comm=_gather_weights([flat_t(w_in)[0].astype(bf16)], [wts["conv_w"]]), at=(0.0, 0.5))
    cat_cols = lambda g: jnp.concatenate([g[k] for k in range(nchip)], axis=1)
    conv_full = cat_cols(conv_g)
    mixer_gather = _gather_weights([wts[n].astype(bf16) for n in MIXER[1:]], [])
    ffn_gather = _gather_weights([wts[n].astype(bf16) for n in FFN[:2]], [])
    w_down_b = wts["w_down"].astype(bf16)
    dn2 = w_down_b.shape[0] // 2
    down_gathers = [_gather_weights([w_down_b[:dn2]], []), _gather_weights([w_down_b[dn2:]], [])]

    wi = w_in_g.reshape(nchip * cs_in, d)
    o = np.cumsum((0,) + in_sizes)
    seg = lambda i: wi[o[i]:o[i + 1]]
    w_in_pt = jnp.concatenate([seg(4), seg(0), seg(1), seg(3), seg(2), seg(5),
                               jnp.zeros((d_in_p - off_tail - ROPE - hs, d), bf16)], axis=0)

    inv_freq = ROPE_THETA ** (-jnp.arange(0, ROPE, 2, dtype=f32) / ROPE)
    ang = positions[0].astype(f32)[:, None] * inv_freq
    cos, sin = jnp.cos(ang), jnp.sin(ang)
    z32, z64, z96 = jnp.zeros((t, 32), f32), jnp.zeros((t, 64), f32), jnp.zeros((t, 96), f32)
    cosp = jnp.concatenate([cos, cos, z64], axis=1)
    sina = jnp.concatenate([-sin, z96], axis=1)
    sinb = jnp.concatenate([z32, sin, z64], axis=1)

    row = lambda a: a.reshape(1, -1)
    w_pre_mix, w_post_mix = row(wts["pre_mix_norm_w"]), row(wts["post_mix_norm_w"])
    w_pre_ffn, w_post_ffn = row(wts["pre_ffn_norm_w"]), row(wts["post_ffn_norm_w"])
    w_qn, w_kvn = row(wts["q_norm_w"]), row(wts["kv_norm_w"])
    w_attn_n, w_ssd_n = row(wts["attn_out_norm_w"]), row(wts["ssd_norm_w"])
    conv_b_r = row(wts["conv_b"])
    dtb, alog, dskip = row(wts["dt_bias"]), row(wts["a_log"]), row(wts["d_skip"])
    dtb_t, alog_t = dtb.reshape(hs, 1), alog.reshape(hs, 1)

    proj, (w_uq_g, w_ukv_f, w_out_g) = _matmul([(u, w_in_pt)], "nt", f32, "in_proj", comm=mixer_gather,
                                               at=(0.0, 0.6))
    w_uq_p = jnp.pad(cat_cols(w_uq_g).reshape(Q_RANK, nh, qk_head), ((0, 0), (0, 0), (0, QK_PAD - qk_head))
                     ).reshape(Q_RANK, nh * QK_PAD)
    w_out_f = w_out_g.reshape(-1, w_out_g.shape[2])
    cq_in = _rows(proj, tm, Q_RANK, off_cq // Q_RANK)
    ckv_in = _rows(proj, tm, KV_RANK, off_ckv // KV_RANK)
    cqn = _rowwise(lambda a, w: _rms(a, w), t, tm, [cq_in, _par(w_qn)], [_rout(t, Q_RANK, bf16, tm)], [], "q_norm")[0]
    ckvn = _rowwise(lambda a, w: _rms(a, w), t, tm, [ckv_in, _par(w_kvn)], [_rout(t, KV_RANK, bf16, tm)], [],
                    "kv_norm")[0]
    q_raw = _matmul([(cqn, w_uq_p)], "nn", f32, "q_up")
    kv = _matmul([(ckvn, w_ukv_f)], "nn", bf16, "kv_up")

    def q_rope_fn(qt, cp, sa, sb):
        parts = []
        for h in range(nh):
            parts.append(qt[:, h * QK_PAD: h * QK_PAD + NOPE])
            parts.append(_rope(qt[:, h * QK_PAD + NOPE:(h + 1) * QK_PAD], cp, sa, sb))
        return jnp.concatenate(parts, axis=1)

    tail_cb = off_tail // LANES
    q2 = _rowwise(q_rope_fn, t, tm, [_rows(q_raw, tm), _rows(cosp, tm), _rows(sina, tm), _rows(sinb, tm)],
                  [_rout(t, nh * QK_PAD, bf16, tm)], [], "q_rope")[0]
    kr2 = _rowwise(_rope, t, tm, [_rows(proj, tm, LANES, tail_cb), _rows(cosp, tm), _rows(sina, tm), _rows(sinb, tm)],
                   [_rout(t, LANES, bf16, tm)], [], "k_rope")[0]
    o_att, lse, ffn_w = _flash_fwd(q2, kv, kr2, nh, scale, ffn_gather, (0.0, 0.75))
    w_gate_f, w_up_f = ffn_w

    xbc_act = _conv_fwd(proj, conv_full, conv_b_r, cdim, tm)
    dt_raw, dt_raw_t = _blocked(
        lambda blk: (blk[:, ROPE:ROPE + hs], blk.T[ROPE:ROPE + hs, :]), (t // tm,),
        [_rows(proj, tm, LANES, tail_cb)],
        [((t, hs), f32, (tm, hs), lambda i: (i, 0)), ((hs, t), f32, (hs, tm), lambda i: (0, i))], [], "dt_split")
    y_ssd, hprev, (down_a,) = _ssd_fwd(xbc_act, dt_raw, dt_raw_t, dtb, dtb_t, alog, alog_t, dskip, width,
                                       down_gathers[0], (0.0, 0.6))
    z_ins = [_rows(proj, tm, gw, off_z // gw + i) for i in range(SSD_G)]

    def mix_norms_fn(ov, yv, *rest):
        zs, wa, ws = rest[:SSD_G], rest[SSD_G], rest[SSD_G + 1]
        outs = [_rms(ov, wa)]
        for i in range(SSD_G):
            sl = slice(i * gw, (i + 1) * gw)
            outs.append(_rms(yv[:, sl] * (zs[i] * _sigmoid(zs[i])), ws[:, sl]))
        return jnp.concatenate(outs, axis=1)

    cat = _rowwise(mix_norms_fn, t, tm, [_rows(o_att, tm), _rows(y_ssd, tm)] + z_ins + [_par(w_attn_n), _par(w_ssd_n)],
                   [_rout(t, d, bf16, tm)], [], "attn_ssd_out_norms")[0]
    mix, (down_b,) = _matmul([(cat, w_out_f)], "nn", f32, "out_proj", comm=down_gathers[1], at=(0.0, 0.6))
    w_down_f = jnp.concatenate([down_a, down_b], axis=1).reshape(-1, down_a.shape[2])

    def post_mix_fn(mx, xv, w1, w2):
        h1v = xv + _rms(mx, w1)
        return h1v, _rms(h1v, w2)

    h1, v_in = _rowwise(post_mix_fn, t, tmw, [_rows(mix, tmw), _rows(xs, tmw), _par(w_post_mix), _par(w_pre_ffn)],
                        [_rout(t, d, f32, tmw), _rout(t, d, bf16, tmw)], [], "post_mix_pre_ffn_norm")
    g_ff, u_ff, act = _ffn_up(v_in, w_gate_f, w_up_f)
    ffn = _matmul([(act, w_down_f)], "nn", f32, "ffn_down")

    def final_fn(fv, h1v, tg, w):
        h2 = h1v + _rms(fv, w)
        err = h2 - tg
        lpart = 0.5 * jnp.sum(jnp.sum(err * err, axis=1, keepdims=True), axis=0, keepdims=True) / d
        dh2 = err / d
        dff, dw = _rms_bwd(fv, w, dh2)
        return dff, dh2, jnp.broadcast_to(lpart, (1, LANES)), dw

    dffn, dh2, loss_acc, g_post_ffn = _rowwise(
        final_fn, t, tmw, [_rows(ffn, tmw), _rows(h1, tmw), _rows(tgt, tmw), _par(w_post_ffn)],
        [_rout(t, d, bf16, tmw), _rout(t, d, f32, tmw)], [(1, LANES), (1, d)], "loss_post_ffn_norm_bwd")
    loss = lax.psum(loss_acc[0, 0], ("x", "y", "c"))

    dg_ff, du_ff = _ffn_down_bwd(dffn, w_down_f, g_ff, u_ff)
    gw_down = _matmul([(act, dffn)], "tn", f32, "grad_w_down")
    gw_gate = _matmul([(v_in, dg_ff)], "tn", f32, "grad_w_gate", out_chunks=nchip)
    gw_up = _matmul([(v_in, du_ff)], "tn", f32, "grad_w_up", out_chunks=nchip)
    ffn_g = [gw_gate, gw_up, gw_down.reshape(nchip, -1, d)]
    dv_in, ffn_from_sib = _matmul([(dg_ff, w_gate_f), (du_ff, w_up_f)], "nt", f32, "ffn_up_bwd",
                                  comm=_sibling_send_halves(ffn_g), at=(0.0,))
    ffn_pairs = pair_sums(FFN, ffn_g, ffn_from_sib)

    def mid_bwd_fn(h1v, dvv, dh2v, mx, w_pf, w_pm):
        dxn, dw_pf = _rms_bwd(h1v, w_pf, dvv)
        dh1v = dh2v + dxn
        dmx, dw_pm = _rms_bwd(mx, w_pm, dh1v)
        return dh1v, dmx, dw_pf, dw_pm

    dh1, dmix, g_pre_ffn, g_post_mix = _rowwise(
        mid_bwd_fn, t, tmw, [_rows(h1, tmw), _rows(dv_in, tmw), _rows(dh2, tmw), _rows(mix, tmw),
                             _par(w_pre_ffn), _par(w_post_mix)],
        [_rout(t, d, f32, tmw), _rout(t, d, bf16, tmw)], [(1, d), (1, d)], "pre_ffn_post_mix_norm_bwd")
    dcat = _matmul([(dmix, w_out_f)], "nt", f32, "out_proj_bwd")
    gw_out = _matmul([(cat, dmix)], "tn", f32, "grad_w_out")

    def mix_norms_bwd_fn(ov, yv, *rest):
        zs, dcv, wa, ws = rest[:SSD_G], rest[SSD_G], rest[SSD_G + 1], rest[SSD_G + 2]
        dov, dwa = _rms_bwd(ov, wa, dcv[:, :mla_w])
        dl = [jnp.broadcast_to(jnp.sum(dov[:, h * V_HEAD:(h + 1) * V_HEAD] * ov[:, h * V_HEAD:(h + 1) * V_HEAD],
                                       axis=1, keepdims=True), (ov.shape[0], V_HEAD)) for h in range(nh)]
        dys, dzs, dws = [], [], []
        for i in range(SSD_G):
            sl = slice(i * gw, (i + 1) * gw)
            zv, yi = zs[i], yv[:, sl]
            sg = _sigmoid(zv)
            sz = zv * sg
            dgi, dwi = _rms_bwd(yi * sz, ws[:, sl], dcv[:, mla_w + i * gw: mla_w + (i + 1) * gw])
            dys.append(dgi * sz)
            dzs.append(dgi * yi * (sg * (1.0 + zv * (1.0 - sg))))
            dws.append(dwi)
        cc = lambda a: jnp.concatenate(a, axis=1)
        return dov, cc(dl), cc(dys), cc(dzs), dwa, cc(dws)

    do_att, delta, dy_ssd, dz, g_attn_n, g_ssd_n = _rowwise(
        mix_norms_bwd_fn, t, tm,
        [_rows(o_att, tm), _rows(y_ssd, tm)] + z_ins + [_rows(dcat, tm), _par(w_attn_n), _par(w_ssd_n)],
        [_rout(t, mla_w, bf16, tm), _rout(t, mla_w, f32, tm), _rout(t, width, f32, tm), _rout(t, width, bf16, tm)],
        [(1, mla_w), (1, width)], "attn_ssd_out_norms_bwd")
    dkv, dkr_h, dq2, gu_from_chips = _flash_bwd(q2, kv, kr2, do_att, lse, delta, nh, scale,
                                                _chips_exchange(ffn_pairs[:2]), (0.0,))

    def q_rope_bwd_fn(dqt, cp, sa, sb):
        parts = []
        for h in range(nh):
            parts.append(dqt[:, h * QK_PAD: h * QK_PAD + NOPE])
            parts.append(_rope_bwd(dqt[:, h * QK_PAD + NOPE:(h + 1) * QK_PAD], cp, sa, sb))
        return jnp.concatenate(parts, axis=1)

    dq_raw = _rowwise(q_rope_bwd_fn, t, tm, [_rows(dq2, tm), _rows(cosp, tm), _rows(sina, tm), _rows(sinb, tm)],
                      [_rout(t, nh * QK_PAD, bf16, tm)], [], "q_rope_bwd")[0]

    def k_rope_bwd_fn(dk, cp, sa, sb):
        tot = dk[:, 0:LANES]
        for h in range(1, nh):
            tot = tot + dk[:, h * LANES:(h + 1) * LANES]
        return _rope_bwd(tot, cp, sa, sb)

    dkr = _rowwise(k_rope_bwd_fn, t, tm, [_rows(dkr_h, tm), _rows(cosp, tm), _rows(sina, tm), _rows(sinb, tm)],
                   [_rout(t, LANES, f32, tm)], [], "k_rope_bwd")[0]
    gw_uq_p = _matmul([(cqn, dq_raw)], "tn", f32, "grad_w_uq")
    gw_ukv = _matmul([(ckvn, dkv)], "tn", f32, "grad_w_ukv", out_chunks=nchip)
    dcqn = _matmul([(dq_raw, w_uq_p)], "nt", f32, "q_up_bwd")
    dckvn = _matmul([(dkv, w_ukv_f)], "nt", f32, "kv_up_bwd")

    def lat_norm_bwd_fn(a, w, dyv):
        return _rms_bwd(a, w, dyv)

    dcq, g_qn = _rowwise(lat_norm_bwd_fn, t, tm, [cq_in, _par(w_qn), _rows(dcqn, tm)],
                         [_rout(t, Q_RANK, bf16, tm)], [(1, Q_RANK)], "q_norm_bwd")
    dckv, g_kvn = _rowwise(lat_norm_bwd_fn, t, tm, [ckv_in, _par(w_kvn), _rows(dckvn, tm)],
                           [_rout(t, KV_RANK, bf16, tm)], [(1, KV_RANK)], "kv_norm_bwd")

    dxbc_act, ddt_raw, g_alog, g_dskip, g_dtb, down_from_chips = _ssd_bwd(
        xbc_act, dt_raw, dt_raw_t, dtb, dtb_t, alog, alog_t, dskip, hprev, dy_ssd, width,
        _chips_exchange(ffn_pairs[2:]), (0.0,))
    ffn_halves = chip_sums(FFN, ffn_g, ffn_from_sib, gu_from_chips + down_from_chips)
    dpre, gcw0, gcw1, gcw2, gcw3, g_conv_b = _conv_bwd_pre(proj, conv_full, conv_b_r, dxbc_act, cdim, tm)
    g_conv_w = jnp.concatenate([gcw0, gcw1, gcw2, gcw3], axis=0)
    dxbc = _conv_bwd_dx(dpre, conv_full, tm)

    def dproj_fn(a, b, c, e, kr_blk, dt_blk):
        parts = [a, b, c, e, (kr_blk + dt_blk).astype(bf16)]
        if d_in_p > off_tail + tail:
            parts.append(jnp.zeros((a.shape[0], d_in_p - off_tail - tail), bf16))
        return jnp.concatenate(parts, axis=1)

    dproj = _rowwise(dproj_fn, t, tm, [_rows(dxbc, tm), _rows(dcq, tm), _rows(dckv, tm), _rows(dz, tm),
                                       _rows(dkr, tm), _rows(ddt_raw, tm)],
                     [_rout(t, d_in_p, bf16, tm)], [], "in_proj_grad_pack")[0]
    split_cols = lambda gf: jnp.stack(jnp.split(gf, nchip, axis=1))
    split_rows = lambda gf: gf.reshape(nchip, gf.shape[0] // nchip, gf.shape[1])
    gw_uq = gw_uq_p.reshape(Q_RANK, nh, QK_PAD)[:, :, :qk_head].reshape(Q_RANK, nh * qk_head)
    rest_g = [split_cols(gw_uq), gw_ukv, split_rows(gw_out)]
    gw_in_pt, both = _matmul([(dproj, u)], "tn", f32, "grad_w_in",
                             comm=_both(_sibling_send_halves(rest_g), _sibling_swap(ffn_halves)),
                             at=(0.0,))
    rest_from_sib, ffn_sib_halves = both[:len(rest_g)], both[len(rest_g):]
    gseg = lambda a, b: gw_in_pt[a:b]
    gw_in_t = jnp.concatenate([gseg(off_cq, off_ckv), gseg(off_ckv, off_z), gseg(off_tail, off_tail + ROPE),
                               gseg(off_z, off_tail), gseg(off_xbc, off_cq),
                               gseg(off_tail + ROPE, off_tail + ROPE + hs)], axis=0)
    in_g = [gw_in_t.reshape(nchip, cs_in * d // LANES, LANES)]
    rest_pairs = pair_sums(MIXER[1:], rest_g, rest_from_sib)
    du_in, both = _matmul([(dproj, w_in_pt)], "nn", f32, "in_proj_bwd",
                          comm=_both(_sibling_send_halves(in_g), _chips_exchange(rest_pairs)), at=(0.0,))
    in_from_sib, rest_from_chips = both[:1], both[1:]
    in_pairs = pair_sums(MIXER[:1], in_g, in_from_sib)

    def first_bwd_fn(xv, duv, dh1v, w):
        dxn, dw = _rms_bwd(xv, w, duv)
        return dh1v + dxn, dw

    (grad_x, g_pre_mix), in_from_chips = _rowwise(
        first_bwd_fn, t, tmw, [_rows(xs, tmw), _rows(du_in, tmw), _rows(dh1, tmw), _par(w_pre_mix)],
        [_rout(t, d, f32, tmw)], [(1, d)], "pre_mix_norm_bwd", comm=_chips_exchange(in_pairs), at=(0.0,))

    mix_halves = (chip_sums(MIXER[:1], in_g, in_from_sib, in_from_chips)
                  + chip_sums(MIXER[1:], rest_g, rest_from_sib, rest_from_chips))
    halves = mix_halves + ffn_halves
    sib_halves = _comm_call(_sibling_swap(mix_halves), "sibling_swap") + ffn_sib_halves
    gshard = {}

    gsmall = {"q_norm_w": g_qn, "kv_norm_w": g_kvn, "conv_w": g_conv_w, "conv_b": g_conv_b, "dt_bias": g_dtb,
              "a_log": g_alog, "d_skip": g_dskip, "ssd_norm_w": g_ssd_n, "attn_out_norm_w": g_attn_n,
              "pre_mix_norm_w": g_pre_mix, "post_mix_norm_w": g_post_mix, "pre_ffn_norm_w": g_pre_ffn,
              "post_ffn_norm_w": g_post_ffn}
    small_sizes = [int(np.prod(gsmall[n].shape)) for n in SMALL]
    srows = _round_up(-(-sum(small_sizes) // LANES), 8)
    spart = _flat_pad([gsmall[n] for n in SMALL], srows * LANES).reshape(srows, LANES)
    sall = _gather_all(spart)

    def sum8_fn(a):
        tot = a[0]
        for k in range(1, 8):
            tot = tot + a[k]
        return tot

    ssum = _blocked(sum8_fn, (1,), [(sall, sall.shape, lambda i: (0, 0, 0))],
                    [((srows, LANES), f32, (srows, LANES), lambda i: (0, 0))], [], "small_grad_sum")[0].reshape(-1)
    gred = {}
    off = 0
    for n, sz in zip(SMALL, small_sizes):
        gred[n] = ssum[off:off + sz].reshape(gsmall[n].shape)
        off += sz
    gshard["conv_w"] = lax.dynamic_slice_in_dim(gred["conv_w"], my_chip * ccs, ccs, axis=1)
    for n in SMALL:
        if n != "conv_w":
            gshard[n] = gred[n].reshape(wts[n].shape)

    delta, new_m, new_v = {}, {}, {}
    big_out = {}
    for n, mine_h, sib_h in zip(BIG, halves, sib_halves):
        view = flat_t if n == "w_in" else (lambda a3: a3)
        res = _adamw_halves(view(local[n]), view(local["m_" + n]), view(local["v_" + n]), mine_h, sib_h, sp,
                            "adamw_" + n)
        big_out[n] = [jnp.swapaxes(a.reshape(1, cs_in, d), 1, 2) for a in res] if n == "w_in" else res
    pack = lambda src: _flat_pad([src[n] for n in SMALL], srows * LANES).reshape(srows, LANES)
    sd, sm, sv = _adamw_call(pack(wts), pack(gshard), pack(mom_m), pack(mom_v), "adamw_small")
    off = 0
    for n in SMALL:
        sz = int(np.prod(wts[n].shape))
        for dst, src in ((delta, sd), (new_m, sm), (new_v, sv)):
            dst[n] = src.reshape(-1)[off:off + sz].reshape(wts[n].shape)
        off += sz

    small_out = (gshard, delta, new_m, new_v)
    pick = lambda k: [big_out[n][k] if n in big_out else small_out[k][n][None] for n in ORDER]
    return (loss, grad_x[None], *pick(0), *pick(1), *pick(2), *pick(3))
```

```python
import functools

import numpy as np
import jax
import jax.numpy as jnp
from jax import lax
from jax.experimental import pallas as pl
from jax.experimental.pallas import tpu as pltpu

f32, bf16 = jnp.float32, jnp.bfloat16

EPS = 1e-6
V_HEAD = 128
NOPE = 128
ROPE = 64
QK_PAD = 256
Q_RANK = 512
KV_RANK = 512
ROPE_THETA = 10000.0
SSD_P = 64
SSD_G = 2
SSD_N = 128
SSD_K = 4
CHUNK = 128
ADAM_LR, ADAM_B1, ADAM_B2, ADAM_EPS, ADAM_WD, ADAM_STEP = 0.001, 0.9, 0.999, 1e-08, 0.01, 10

VMEM_LIMIT_BYTES = 48 * 1024 * 1024
LANES = 128
ATT_TILE = 1024
ATT_TILE_FWD = 1024
MM_TM, MM_TN, MM_TK = 1408, 1024, 2048
CHUNK_WHOLE_MAX = 1536
ROW_TILE = 512
ROW_BLOCK_BYTES = 2 * 1024 * 1024

NN = (((1,), (0,)), ((), ()))
NT = (((1,), (1,)), ((), ()))
TN = (((0,), (0,)), ((), ()))
MESH = pl.DeviceIdType.MESH
ANY = pl.BlockSpec(memory_space=pl.ANY)


def _tile(dim, cap, align=LANES):
    if dim <= cap:
        return dim
    t = (cap // align) * align
    while t >= align:
        if dim % t == 0:
            return t
        t -= align
    raise ValueError(f"no tile for {dim} under {cap}")


def _row_tile(rows, cols, align):
    best = None
    for tr in range(align, rows + 1, align):
        if rows % tr == 0 and tr * cols * 4 <= ROW_BLOCK_BYTES:
            best = tr
    return best or rows


def _round_up(n, m):
    return -(-n // m) * m


def _params(sem):
    return pltpu.CompilerParams(dimension_semantics=sem, vmem_limit_bytes=VMEM_LIMIT_BYTES)


def _dot(a, b, dims):
    return lax.dot_general(a.astype(bf16), b.astype(bf16), dims, preferred_element_type=f32)


def _call(body, name, out_shape, grid, in_specs, out_specs, scratch_shapes, sem, args, comm=None, at=None,
          prefetch=()):
    npf = len(prefetch)
    if comm is None:
        gs = pltpu.PrefetchScalarGridSpec(num_scalar_prefetch=npf, grid=grid, in_specs=list(in_specs),
                                          out_specs=list(out_specs), scratch_shapes=list(scratch_shapes))
        res = pl.pallas_call(body, name=name, out_shape=list(out_shape), grid_spec=gs,
                             compiler_params=_params(sem))(*prefetch, *args)
        return list(res), []
    n_in, n_out, n_sc = len(args), len(out_shape), len(scratch_shapes)
    na, no = len(comm.args), len(comm.outs)
    steps = int(np.prod(grid))

    def full(*allrefs):
        pf, refs = allrefs[:npf], allrefs[npf:]
        cin = refs[n_in:n_in + na]
        o0 = n_in + na
        cout = refs[o0 + n_out:o0 + n_out + no]
        s0 = o0 + n_out + no
        send_sems, recv_sems = refs[s0 + n_sc], refs[s0 + n_sc + 1]
        lin = pl.program_id(0)
        for dim in range(1, len(grid)):
            lin = lin * grid[dim] + pl.program_id(dim)
        for p in range(comm.nphase - 1):
            @pl.when(lin == int(round(at[p] * (steps - 1))))
            def _(p=p):
                comm.run(p, cin, cout, send_sems, recv_sems)
        body(*pf, *refs[:n_in], *refs[o0:o0 + n_out], *refs[s0:s0 + n_sc])

        @pl.when(lin == steps - 1)
        def _():
            comm.run(comm.nphase - 1, cin, cout, send_sems, recv_sems)

    gs = pltpu.PrefetchScalarGridSpec(
        num_scalar_prefetch=npf, grid=grid, in_specs=list(in_specs) + [ANY] * na,
        out_specs=list(out_specs) + [ANY] * no, scratch_shapes=list(scratch_shapes) + comm.sems())
    res = pl.pallas_call(full, name=name, out_shape=list(out_shape) + comm.outs, grid_spec=gs,
                         compiler_params=_params(("arbitrary",) * len(grid)))(*prefetch, *args, *comm.args)
    return list(res[:n_out]), list(res[n_out:])


def _chunk_tile(cs, cap):
    return cs if cs <= CHUNK_WHOLE_MAX else _tile(cs, cap)


def _matmul(pairs, mode, out_dtype, name, out_chunks=None, comm=None, at=None):
    a0, b0 = pairs[0]
    chunked = b0.ndim == 3
    cs = b0.shape[2] if chunked else None
    bcols = b0.shape[0] * b0.shape[2] if chunked else b0.shape[1]
    brows = b0.shape[1] if chunked else b0.shape[0]
    if mode == "nn":
        (m, k), n = a0.shape, bcols
    elif mode == "nt":
        (m, k), n = a0.shape, brows
    else:
        (k, m), n = a0.shape, bcols
    tm = _tile(m, MM_TM)
    if mode == "nt":
        tn = _tile(n, MM_TN)
        tk = _chunk_tile(cs, MM_TK) if chunked else _tile(k, MM_TK)
    else:
        tk = _tile(k, MM_TK)
        if chunked:
            tn = _chunk_tile(cs, MM_TN)
        elif out_chunks:
            tn = _chunk_tile(n // out_chunks, MM_TN)
        else:
            tn = _tile(n, MM_TN)
    nk = k // tk
    if mode == "nn":
        a_spec = pl.BlockSpec((tm, tk), lambda i, j, kk: (i, kk))
        if chunked:
            q = cs // tn
            b_spec = pl.BlockSpec((None, tk, tn), lambda i, j, kk: (j // q, kk, j % q))
        else:
            b_spec = pl.BlockSpec((tk, tn), lambda i, j, kk: (kk, j))
        dims = NN
    elif mode == "nt":
        a_spec = pl.BlockSpec((tm, tk), lambda i, j, kk: (i, kk))
        if chunked:
            q = cs // tk
            b_spec = pl.BlockSpec((None, tn, tk), lambda i, j, kk: (kk // q, j, kk % q))
        else:
            b_spec = pl.BlockSpec((tn, tk), lambda i, j, kk: (j, kk))
        dims = NT
    else:
        a_spec = pl.BlockSpec((tk, tm), lambda i, j, kk: (kk, i))
        b_spec = pl.BlockSpec((tk, tn), lambda i, j, kk: (kk, j))
        dims = TN
    if out_chunks:
        qo = (n // out_chunks) // tn
        out_shape = jax.ShapeDtypeStruct((out_chunks, m, n // out_chunks), out_dtype)
        o_spec = pl.BlockSpec((None, tm, tn), lambda i, j, kk: (j // qo, i, j % qo))
    else:
        out_shape = jax.ShapeDtypeStruct((m, n), out_dtype)
        o_spec = pl.BlockSpec((tm, tn), lambda i, j, kk: (i, j))
    npair = len(pairs)

    def body(*refs):
        o_ref, acc = refs[2 * npair], refs[2 * npair + 1]
        kk = pl.program_id(2)

        @pl.when(kk == 0)
        def _():
            acc[...] = jnp.zeros_like(acc)

        part = _dot(refs[0][...], refs[1][...], dims)
        for p in range(1, npair):
            part = part + _dot(refs[2 * p][...], refs[2 * p + 1][...], dims)
        acc[...] += part

        @pl.when(kk == nk - 1)
        def _():
            o_ref[...] = acc[...].astype(out_dtype)

    args = [t for pr in pairs for t in pr]
    res, cres = _call(body, name, [out_shape], (m // tm, n // tn, nk), [a_spec, b_spec] * npair, [o_spec],
                      [pltpu.VMEM((tm, tn), f32)], ("parallel", "parallel", "arbitrary"), args, comm, at)
    return res[0] if comm is None else (res[0], cres)


def _sigmoid(x):
    return 1.0 / (1.0 + jnp.exp(-x))


def _ffn_up(v, wg, wu):
    m, k = v.shape
    nchunk, _, cs = wg.shape
    n = nchunk * cs
    tm, tn = _tile(m, 512), _chunk_tile(cs, 512)
    q = cs // tn
    w_spec = pl.BlockSpec((None, k, tn), lambda j, i: (j // q, 0, j % q))

    def body(v_ref, wg_ref, wu_ref, g_ref, u_ref, act_ref):
        vb = v_ref[...]
        g = _dot(vb, wg_ref[...], NN)
        u = _dot(vb, wu_ref[...], NN)
        sg = _sigmoid(g)
        silu = g * sg
        g_ref[...] = silu.astype(bf16)
        u_ref[...] = (u * (sg * (1.0 + g * (1.0 - sg)))).astype(bf16)
        act_ref[...] = (silu * u).astype(bf16)

    out = jax.ShapeDtypeStruct((m, n), bf16)
    o_spec = pl.BlockSpec((tm, tn), lambda j, i: (i, j))
    return pl.pallas_call(
        body, name="ffn_up", out_shape=(out, out, out), grid=(n // tn, m // tm),
        in_specs=[pl.BlockSpec((tm, k), lambda j, i: (i, 0)), w_spec, w_spec],
        out_specs=(o_spec, o_spec, o_spec),
        compiler_params=_params(("parallel", "parallel")),
    )(v, wg, wu)


def _ffn_down_bwd(dffn, wd, dact_du, dact_dg):
    m, k = dffn.shape
    n = wd.shape[0]
    tm, tn = _tile(m, 1024), _tile(n, 512)

    def body(d_ref, w_ref, fu_ref, fg_ref, dg_ref, du_ref):
        dact = _dot(d_ref[...], w_ref[...], NT)
        du_ref[...] = (dact * fu_ref[...].astype(f32)).astype(bf16)
        dg_ref[...] = (dact * fg_ref[...].astype(f32)).astype(bf16)

    out = jax.ShapeDtypeStruct((m, n), bf16)
    o_spec = pl.BlockSpec((tm, tn), lambda i, j: (i, j))
    return pl.pallas_call(
        body, name="ffn_down_bwd", out_shape=(out, out), grid=(m // tm, n // tn),
        in_specs=[pl.BlockSpec((tm, k), lambda i, j: (i, 0)), pl.BlockSpec((tn, k), lambda i, j: (j, 0)),
                  o_spec, o_spec],
        out_specs=(o_spec, o_spec),
        compiler_params=_params(("parallel", "parallel")),
    )(dffn, wd, dact_du, dact_dg)


def _blocked(fn, grid, ins, outs, accs, name, sp=None, comm=None, at=None):
    n_in, n_out, n_acc = len(ins), len(outs), len(accs)
    nsp = 0 if sp is None else 1

    def body(*refs):
        refs = refs[nsp:]
        tiles = [r[...] for r in refs[:n_in]]
        res = fn(*tiles)
        if not isinstance(res, (tuple, list)):
            res = (res,)
        for r, val in zip(refs[n_in:n_in + n_out], res[:n_out]):
            r[...] = val.astype(r.dtype)
        if n_acc:
            first = pl.program_id(0) == 0
            for d in range(1, len(grid)):
                first = jnp.logical_and(first, pl.program_id(d) == 0)

            @pl.when(first)
            def _():
                for r in refs[n_in + n_out:]:
                    r[...] = jnp.zeros_like(r)

            for r, val in zip(refs[n_in + n_out:], res[n_out:]):
                r[...] += val

    def acc_map(shape):
        zeros = (0,) * len(shape)
        return lambda *idx: zeros

    in_specs = [pl.BlockSpec(bs, im) for _, bs, im in ins]
    out_specs = [pl.BlockSpec(bs, im) for _, _, bs, im in outs] + [pl.BlockSpec(s, acc_map(s)) for s in accs]
    out_shape = [jax.ShapeDtypeStruct(s, d) for s, d, _, _ in outs] + [jax.ShapeDtypeStruct(s, f32) for s in accs]
    sem = ("arbitrary",) * len(grid) if n_acc else ("parallel",) * len(grid)
    args = [a for a, _, _ in ins]
    res, cres = _call(body, name, out_shape, grid, in_specs, out_specs, [], sem, args, comm, at,
                      prefetch=() if sp is None else (sp,))
    return res if comm is None else (res, cres)


def _rows(a, tm, cols=None, cb=0):
    w = a.shape[1] if cols is None else cols
    return (a, (tm, w), lambda i: (i, cb))


def _par(a):
    zeros = (0,) * a.ndim
    return (a, a.shape, lambda i: zeros)


def _rout(t, w, dtype, tm):
    return ((t, w), dtype, (tm, w), lambda i: (i, 0))


def _rowwise(fn, t, tm, ins, outs, accs, name, comm=None, at=None):
    return _blocked(fn, (t // tm,), ins, outs, accs, name, comm=comm, at=at)


def _rms(x, w):
    r = lax.rsqrt(jnp.mean(x * x, axis=-1, keepdims=True) + EPS)
    return x * r * w


def _rms_bwd(x, w, dy):
    r = lax.rsqrt(jnp.mean(x * x, axis=-1, keepdims=True) + EPS)
    xh = x * r
    dyw = dy * w
    dx = r * (dyw - xh * jnp.mean(dyw * xh, axis=-1, keepdims=True))
    return dx, jnp.sum(dy * xh, axis=0, keepdims=True)


def _silu_grad(x):
    s = _sigmoid(x)
    return s * (1.0 + x * (1.0 - s))


def _rope(blk, cosp, sina, sinb):
    return blk * cosp + pltpu.roll(blk, 96, 1) * sina + pltpu.roll(blk, 32, 1) * sinb


def _rope_bwd(dy, cosp, sina, sinb):
    return dy * cosp + pltpu.roll(dy * sina, 32, 1) + pltpu.roll(dy * sinb, 96, 1)


HALO = 8


def _conv_taps(buf, w, tm, base):
    acc = buf[base:base + tm, :] * w[0:1]
    for k in range(1, SSD_K):
        acc = acc + buf[base + k:base + k + tm, :] * w[k:k + 1]
    return acc


def _conv_specs(t, tm, cdim):
    cur = pl.BlockSpec((tm, cdim), lambda i: (i, 0))
    prev = pl.BlockSpec((HALO, cdim), lambda i: (jnp.maximum(i * (tm // HALO) - 1, 0), 0))
    nxt = pl.BlockSpec((HALO, cdim), lambda i: (jnp.minimum((i + 1) * (tm // HALO), t // HALO - 1), 0))
    return cur, prev, nxt


def _conv_fwd(src, w, b, cdim, tm):
    t = src.shape[0]
    cur, prev, _ = _conv_specs(t, tm, cdim)

    def body(x_ref, p_ref, w_ref, b_ref, o_ref, buf):
        buf[0:HALO, :] = jnp.where(pl.program_id(0) > 0, p_ref[...], 0.0)
        buf[HALO:HALO + tm, :] = x_ref[...]
        pre = _conv_taps(buf, w_ref[...], tm, HALO - (SSD_K - 1)) + b_ref[...]
        o_ref[...] = pre * _sigmoid(pre)

    par = lambda a: pl.BlockSpec(a.shape, lambda i: (0, 0))
    return pl.pallas_call(
        body, name="conv_silu", out_shape=jax.ShapeDtypeStruct((t, cdim), f32), grid=(t // tm,),
        in_specs=[cur, prev, par(w), par(b)], out_specs=cur,
        scratch_shapes=[pltpu.VMEM((tm + HALO, cdim), f32)], compiler_params=_params(("parallel",)),
    )(src, src, w, b)


def _conv_bwd_pre(src, w, b, dact, cdim, tm):
    t = src.shape[0]
    cur, prev, _ = _conv_specs(t, tm, cdim)

    def body(x_ref, p_ref, w_ref, b_ref, d_ref, dpre_ref, dw0, dw1, dw2, dw3, db, buf):
        @pl.when(pl.program_id(0) == 0)
        def _():
            for r in (dw0, dw1, dw2, dw3, db):
                r[...] = jnp.zeros_like(r)

        buf[0:HALO, :] = jnp.where(pl.program_id(0) > 0, p_ref[...], 0.0)
        buf[HALO:HALO + tm, :] = x_ref[...]
        base = HALO - (SSD_K - 1)
        pre = _conv_taps(buf, w_ref[...], tm, base) + b_ref[...]
        dpre = d_ref[...] * _silu_grad(pre)
        dpre_ref[...] = dpre
        for k, r in enumerate((dw0, dw1, dw2, dw3)):
            r[...] += jnp.sum(dpre * buf[base + k:base + k + tm, :], axis=0, keepdims=True)
        db[...] += jnp.sum(dpre, axis=0, keepdims=True)

    par = lambda a: pl.BlockSpec(a.shape, lambda i: (0, 0))
    acc = pl.BlockSpec((1, cdim), lambda i: (0, 0))
    acc_shape = jax.ShapeDtypeStruct((1, cdim), f32)
    return pl.pallas_call(
        body, name="conv_silu_bwd", out_shape=[jax.ShapeDtypeStruct((t, cdim), f32)] + [acc_shape] * 5,
        grid=(t // tm,), in_specs=[cur, prev, par(w), par(b), cur], out_specs=[cur] + [acc] * 5,
        scratch_shapes=[pltpu.VMEM((tm + HALO, cdim), f32)], compiler_params=_params(("arbitrary",)),
    )(src, src, w, b, dact)


def _conv_bwd_dx(dpre, w, tm):
    t, cdim = dpre.shape
    cur, _, nxt = _conv_specs(t, tm, cdim)
    last = t // tm - 1

    def body(d_ref, n_ref, w_ref, o_ref, buf):
        buf[0:tm, :] = d_ref[...]
        buf[tm:tm + HALO, :] = jnp.where(pl.program_id(0) < last, n_ref[...], 0.0)
        wv = w_ref[...]
        acc = buf[0:tm, :] * wv[SSD_K - 1:SSD_K]
        for k in range(SSD_K - 1):
            s = SSD_K - 1 - k
            acc = acc + buf[s:s + tm, :] * wv[k:k + 1]
        o_ref[...] = acc.astype(bf16)

    return pl.pallas_call(
        body, name="conv_bwd_dx", out_shape=jax.ShapeDtypeStruct((t, cdim), bf16), grid=(t // tm,),
        in_specs=[cur, nxt, pl.BlockSpec(w.shape, lambda i: (0, 0))], out_specs=cur,
        scratch_shapes=[pltpu.VMEM((tm + HALO, cdim), f32)], compiler_params=_params(("parallel",)),
    )(dpre, dpre, w)


def _causal_mask(s, row0=0):
    row = lax.broadcasted_iota(jnp.int32, s.shape, 0) + row0
    col = lax.broadcasted_iota(jnp.int32, s.shape, 1)
    return jnp.where(row >= col, s, -jnp.inf)


def _causal_pairs(nq, q_major):
    pairs = [(qi, ki) for qi in range(nq) for ki in range(qi + 1)]
    if not q_major:
        pairs.sort(key=lambda p: (p[1], p[0]))
    return (jnp.asarray([p[0] for p in pairs], jnp.int32), jnp.asarray([p[1] for p in pairs], jnp.int32))


def _flash_fwd(q, kv, kr, nheads, scale, comm=None, at=None):
    t = q.shape[0]
    tq = _tile(t, ATT_TILE_FWD)
    nq = t // tq

    qtab, ktab = _causal_pairs(nq, q_major=True)
    hp = 2 if nheads % 2 == 0 else 1

    def body(qt, kt, q_ref, kv_ref, kr_ref, o_ref, lse_ref, m_sc, l_sc, acc_sc):
        qi, ki = qt[pl.program_id(1)], kt[pl.program_id(1)]

        @pl.when(ki == 0)
        def _():
            m_sc[...] = jnp.full_like(m_sc, -jnp.inf)
            l_sc[...] = jnp.zeros_like(l_sc)
            acc_sc[...] = jnp.zeros_like(acc_sc)

        def step(masked, last):
            krb = kr_ref[...]
            ss = []
            for j in range(hp):
                k = jnp.concatenate([kv_ref[:, j * QK_PAD: j * QK_PAD + NOPE], krb], axis=1)
                ss.append(lax.dot_general(q_ref[:, j * QK_PAD:(j + 1) * QK_PAD], k, NT, preferred_element_type=f32))
            soft = []
            for j in range(hp):
                s = ss[j] * scale
                if masked:
                    s = _causal_mask(s)
                m_old = m_sc[j]
                m_new = jnp.maximum(m_old, jnp.max(s, axis=1, keepdims=True))
                alpha = jnp.exp(m_old - m_new)
                p = jnp.exp(s - m_new)
                soft.append((m_new, alpha, alpha * l_sc[j] + jnp.sum(p, axis=1, keepdims=True), p.astype(bf16)))
            for j in range(hp):
                m_new, alpha, l, p = soft[j]
                v = kv_ref[:, j * QK_PAD + NOPE:(j + 1) * QK_PAD]
                acc = alpha * acc_sc[j] + lax.dot_general(p, v, NN, preferred_element_type=f32)
                if last:
                    o_ref[:, j * V_HEAD:(j + 1) * V_HEAD] = acc / l
                    lse_ref[:, j * V_HEAD:(j + 1) * V_HEAD] = jnp.broadcast_to(m_new + jnp.log(l), (tq, V_HEAD))
                else:
                    l_sc[j] = l
                    acc_sc[j] = acc
                    m_sc[j] = m_new

        @pl.when(ki < qi)
        def _():
            step(False, False)

        @pl.when(ki == qi)
        def _():
            step(True, True)

    o_spec = pl.BlockSpec((tq, hp * V_HEAD), lambda h, s, qt, kt: (qt[s], h))
    out = jax.ShapeDtypeStruct((t, nheads * V_HEAD), f32)
    (o, lse), cres = _call(
        body, "flash_fwd", [out, out], (nheads // hp, len(qtab)),
        [pl.BlockSpec((tq, hp * QK_PAD), lambda h, s, qt, kt: (qt[s], h)),
         pl.BlockSpec((tq, hp * QK_PAD), lambda h, s, qt, kt: (kt[s], h)),
         pl.BlockSpec((tq, LANES), lambda h, s, qt, kt: (kt[s], 0))],
        [o_spec, o_spec],
        [pltpu.VMEM((hp, tq, 1), f32), pltpu.VMEM((hp, tq, 1), f32), pltpu.VMEM((hp, tq, V_HEAD), f32)],
        ("parallel", "arbitrary"), [q, kv, kr], comm, at, prefetch=(qtab, ktab))
    return o, lse, cres


def _flash_bwd(q, kv, kr, do, lse, delta, nheads, scale, comm=None, at=None):
    t = q.shape[0]
    tq = _tile(t, ATT_TILE)
    nq = t // tq
    qtab, ktab = _causal_pairs(nq, q_major=False)
    nsub = 2 if tq % 32 == 0 else 1

    def body(qt, kt, q_ref, kn_ref, kr_ref, v_ref, do_ref, lse_ref, dl_ref, dkv_ref, dkr_ref, dq_ref, dk_sc, dv_sc):
        qi, ki = qt[pl.program_id(1)], kt[pl.program_id(1)]

        @pl.when(pl.program_id(1) == 0)
        def _():
            dq_ref[...] = jnp.zeros_like(dq_ref)

        @pl.when(qi == ki)
        def _():
            dk_sc[...] = jnp.zeros_like(dk_sc)
            dv_sc[...] = jnp.zeros_like(dv_sc)

        def step(masked):
            k = jnp.concatenate([kn_ref[...], kr_ref[...]], axis=1)
            vb = v_ref[...]
            parts = [slice(i * (tq // nsub), (i + 1) * (tq // nsub)) for i in range(nsub)]
            qs = [q_ref[r, :] for r in parts]
            dos = [do_ref[r, :] for r in parts]
            ss = [lax.dot_general(qb, k, NT, preferred_element_type=f32) for qb in qs]
            dps = [lax.dot_general(dob, vb, NT, preferred_element_type=f32) for dob in dos]
            ps, dss = [], []
            for r, s, dp in zip(parts, ss, dps):
                s = s * scale
                if masked:
                    s = _causal_mask(s, r.start)
                p = jnp.exp(s - lse_ref[r, 0:1])
                ps.append(p.astype(bf16))
                dss.append((p * (dp - dl_ref[r, 0:1]) * scale).astype(bf16))
            for r, qb, dob, p, ds in zip(parts, qs, dos, ps, dss):
                dv_sc[...] += lax.dot_general(p, dob, TN, preferred_element_type=f32)
                dk_sc[...] += lax.dot_general(ds, qb, TN, preferred_element_type=f32)
                rows = pl.ds(pl.multiple_of(qi * tq + r.start, tq // nsub), tq // nsub)
                dq_ref[rows, :] += lax.dot_general(ds, k, NN, preferred_element_type=f32)

        @pl.when(qi > ki)
        def _():
            step(False)

        @pl.when(qi == ki)
        def _():
            step(True)

        @pl.when(qi == nq - 1)
        def _():
            dk = dk_sc[...]
            dkv_ref[...] = jnp.concatenate([dk[:, :NOPE], dv_sc[...]], axis=1).astype(bf16)
            dkr_ref[...] = dk[:, NOPE:]

    hspec = pl.BlockSpec((tq, V_HEAD), lambda h, s, qt, kt: (qt[s], h))
    (dkv, dkr, dq), cres = _call(
        body, "flash_bwd",
        [jax.ShapeDtypeStruct((t, nheads * QK_PAD), bf16), jax.ShapeDtypeStruct((t, nheads * LANES), f32),
         jax.ShapeDtypeStruct((t, nheads * QK_PAD), f32)],
        (nheads, len(qtab)),
        [pl.BlockSpec((tq, QK_PAD), lambda h, s, qt, kt: (qt[s], h)),
         pl.BlockSpec((tq, NOPE), lambda h, s, qt, kt: (kt[s], 2 * h)),
         pl.BlockSpec((tq, LANES), lambda h, s, qt, kt: (kt[s], 0)),
         pl.BlockSpec((tq, V_HEAD), lambda h, s, qt, kt: (kt[s], 2 * h + 1)),
         hspec, hspec, hspec],
        [pl.BlockSpec((tq, QK_PAD), lambda h, s, qt, kt: (kt[s], h)),
         pl.BlockSpec((tq, LANES), lambda h, s, qt, kt: (kt[s], h)),
         pl.BlockSpec((t, QK_PAD), lambda h, s, qt, kt: (0, h))],
        [pltpu.VMEM((tq, QK_PAD), f32), pltpu.VMEM((tq, V_HEAD), f32)],
        ("parallel", "arbitrary"), [q, kv, kr, kv, do, lse, delta], comm, at, prefetch=(qtab, ktab))
    return dkv, dkr, dq, cres


def _split3(a):
    hi = a.astype(bf16)
    r1 = a - hi.astype(f32)
    mid = r1.astype(bf16)
    lo = (r1 - mid.astype(f32)).astype(bf16)
    return hi, mid, lo


def _split2(a):
    hi = a.astype(bf16)
    return hi, (a - hi.astype(f32)).astype(bf16)


def _ones_dot_left(tri, a):
    hi, mid, lo = _split3(a)
    d = lambda v: lax.dot_general(tri, v, NN, preferred_element_type=f32)
    return d(hi) + d(mid) + d(lo)


def _ones_dot_right(a, tri):
    hi, mid, lo = _split3(a)
    d = lambda v: lax.dot_general(v, tri, NN, preferred_element_type=f32)
    return d(hi) + d(mid) + d(lo)


def _softplus(x):
    return jnp.maximum(x, 0.0) + jnp.log(1.0 + jnp.exp(-jnp.abs(x)))


def _head_spread(hs, width):
    shift = SSD_P.bit_length() - 1
    return (lax.broadcasted_iota(jnp.int32, (hs, width), 0)
            == lax.shift_right_logical(lax.broadcasted_iota(jnp.int32, (hs, width), 1), shift)).astype(bf16)


def _ssd_common(dt_ref, dtT_ref, dtb_ref, dtbT_ref, alog_ref, alogT_ref):
    ii = lax.broadcasted_iota(jnp.int32, (CHUNK, CHUNK), 0)
    jj = lax.broadcasted_iota(jnp.int32, (CHUNK, CHUNK), 1)
    tri = ii >= jj
    raw = dt_ref[...] + dtb_ref[...]
    dt = _softplus(raw)
    a_neg = -jnp.exp(alog_ref[...])
    cum = _ones_dot_left(tri.astype(bf16), dt * a_neg)
    dt_t = _softplus(dtT_ref[...] + dtbT_ref[...])
    cum_t = _ones_dot_right(dt_t * (-jnp.exp(alogT_ref[...])), (ii <= jj).astype(bf16))
    return tri, raw, dt, a_neg, cum, cum_t


def _ssd_fwd(xbc, dt, dt_t, dtb, dtb_t, alog, alog_t, dskip, width, comm=None, at=None):
    t, cdim = xbc.shape
    hs = dt.shape[1]
    nc = t // CHUNK
    epg = hs // SSD_G
    gn = SSD_G * SSD_N

    def body(x_ref, dt_ref, dtT_ref, dtb_ref, dtbT_ref, alog_ref, alogT_ref, d_ref, y_ref, hp_ref, h_sc, yd_sc):
        @pl.when(pl.program_id(0) == 0)
        def _():
            h_sc[...] = jnp.zeros_like(h_sc)

        tri, _, dtv, _, cum, cum_t = _ssd_common(dt_ref, dtT_ref, dtb_ref, dtbT_ref, alog_ref, alogT_ref)
        spread = _head_spread(hs, width)
        clast = cum[CHUNK - 1:CHUNK, :]
        dec = jnp.exp(clast)
        wide = _ones_dot_right(jnp.concatenate([dtv, jnp.exp(cum), jnp.exp(clast - cum),
                                                jnp.broadcast_to(d_ref[...], (CHUNK, hs))], axis=0), spread)
        dt_x, ee_x, ff_x, dsk_x = (wide[i * CHUNK:(i + 1) * CHUNK] for i in range(4))
        xs = x_ref[:, :width]
        xdt = xs * dt_x
        xf = xdt * ff_x
        h_all = h_sc[...]
        hp_ref[0] = h_all
        h_all = h_all.reshape(hs * SSD_P, SSD_N)
        ch_parts = []
        for g in range(SSD_G):
            gsl = slice(g * epg * SSD_P, (g + 1) * epg * SSD_P)
            bb = x_ref[:, width + g * SSD_N: width + (g + 1) * SSD_N].astype(bf16)
            cb_ = x_ref[:, width + gn + g * SSD_N: width + gn + (g + 1) * SSD_N].astype(bf16)
            cbm = lax.dot_general(cb_, bb, NT, preferred_element_type=f32)
            ch_parts.append(_dot(cb_, h_all[gsl], NT))
            st = _dot(xf[:, gsl], bb, TN)
            for e in range(g * epg, (g + 1) * epg):
                esl = slice(e * SSD_P, (e + 1) * SSD_P)
                lmat = jnp.exp(jnp.where(tri, cum[:, e:e + 1] - cum_t[e:e + 1, :], -jnp.inf))
                yd_sc[:, esl] = _dot(cbm * lmat, xdt[:, esl], NN)
                j = e - g * epg
                h_sc[e] = h_sc[e] * dec[:, e:e + 1] + st[j * SSD_P:(j + 1) * SSD_P, :]
        y_ref[...] = yd_sc[...] + jnp.concatenate(ch_parts, axis=1) * ee_x + xs * dsk_x

    par = lambda a: pl.BlockSpec(a.shape, lambda i: (0,) * a.ndim)
    (y, hp), cres = _call(
        body, "ssd_fwd",
        [jax.ShapeDtypeStruct((t, width), f32), jax.ShapeDtypeStruct((nc, hs, SSD_P, SSD_N), f32)], (nc,),
        [pl.BlockSpec((CHUNK, cdim), lambda i: (i, 0)), pl.BlockSpec((CHUNK, hs), lambda i: (i, 0)),
         pl.BlockSpec((hs, CHUNK), lambda i: (0, i)), par(dtb), par(dtb_t), par(alog), par(alog_t), par(dskip)],
        [pl.BlockSpec((CHUNK, width), lambda i: (i, 0)), pl.BlockSpec((1, hs, SSD_P, SSD_N), lambda i: (i, 0, 0, 0))],
        [pltpu.VMEM((hs, SSD_P, SSD_N), f32), pltpu.VMEM((CHUNK, width), f32)], ("arbitrary",),
        [xbc, dt, dt_t, dtb, dtb_t, alog, alog_t, dskip], comm, at)
    return y, hp, cres


def _ssd_bwd(xbc, dt, dt_t, dtb, dtb_t, alog, alog_t, dskip, hprev, dy, width, comm=None, at=None):
    t, cdim = xbc.shape
    hs = dt.shape[1]
    nc = t // CHUNK
    epg = hs // SSD_G
    gn = SSD_G * SSD_N

    def body(x_ref, dt_ref, dtT_ref, dtb_ref, dtbT_ref, alog_ref, alogT_ref, d_ref, hp_ref, dy_ref,
             dx_ref, ddt_ref, dalog_ref, ddsk_ref, ddtb_ref, dh_sc, dxd_sc):
        @pl.when(pl.program_id(0) == 0)
        def _():
            dh_sc[...] = jnp.zeros_like(dh_sc)
            dalog_ref[...] = jnp.zeros_like(dalog_ref)
            ddsk_ref[...] = jnp.zeros_like(ddsk_ref)
            ddtb_ref[...] = jnp.zeros_like(ddtb_ref)

        tri, raw, dtv, a_neg, cum, cum_t = _ssd_common(dt_ref, dtT_ref, dtb_ref, dtbT_ref, alog_ref, alogT_ref)
        dsk = d_ref[...]
        head_row = lax.broadcasted_iota(jnp.int32, (1, hs), 1)
        head_col = lax.broadcasted_iota(jnp.int32, (hs, 1), 0)
        last_row = (lax.broadcasted_iota(jnp.int32, (CHUNK, 1), 0) == CHUNK - 1).astype(f32)
        shift = SSD_P.bit_length() - 1
        spread = _head_spread(hs, width)
        gather = (lax.shift_right_logical(lax.broadcasted_iota(jnp.int32, (width, hs), 0), shift)
                  == lax.broadcasted_iota(jnp.int32, (width, hs), 1)).astype(bf16)
        clast = cum[CHUNK - 1:CHUNK, :]
        ee = jnp.exp(cum)
        ff = jnp.exp(clast - cum)
        dec = jnp.exp(clast)
        wide = _ones_dot_right(jnp.concatenate([dtv, ee, ff, jnp.broadcast_to(dsk, (CHUNK, hs))], axis=0), spread)
        dt_x, ee_x, ff_x, dsk_x = (wide[i * CHUNK:(i + 1) * CHUNK] for i in range(4))
        xs = x_ref[:, :width]
        dyv = dy_ref[...]
        xdt = xs * dt_x
        dye = dyv * ee_x
        xf = xdt * ff_x
        h_all = hp_ref[0].reshape(hs * SSD_P, SSD_N)
        dh_all = dh_sc[...].reshape(hs * SSD_P, SSD_N)
        hi, lo = _split2(_ones_dot_left(spread, dh_all * h_all))
        ones8 = jnp.ones((8, SSD_N), bf16)
        hh = (lax.dot_general(ones8, hi, NT, preferred_element_type=f32)
              + lax.dot_general(ones8, lo, NT, preferred_element_type=f32))[0:1]
        rowsum_m = jnp.zeros((CHUNK, hs), f32)
        colsum_m = jnp.zeros((hs, CHUNK), f32)
        ch_parts, bds_parts = [], []
        for g in range(SSD_G):
            bsl = slice(width + g * SSD_N, width + (g + 1) * SSD_N)
            csl = slice(width + gn + g * SSD_N, width + gn + (g + 1) * SSD_N)
            gsl = slice(g * epg * SSD_P, (g + 1) * epg * SSD_P)
            bb = x_ref[:, bsl].astype(bf16)
            cb_ = x_ref[:, csl].astype(bf16)
            cbm = lax.dot_general(cb_, bb, NT, preferred_element_type=f32)
            hg = h_all[gsl].astype(bf16)
            dhg = dh_all[gsl].astype(bf16)
            ch_parts.append(_dot(cb_, hg, NT))
            bds_parts.append(_dot(bb, dhg, NT))
            dcg = _dot(dye[:, gsl], hg, NN)
            dbg = _dot(xf[:, gsl], dhg, NN)
            dh_new = _dot(dye[:, gsl], cb_, TN)
            dcb = jnp.zeros((CHUNK, CHUNK), f32)
            for e in range(g * epg, (g + 1) * epg):
                esl = slice(e * SSD_P, (e + 1) * SSD_P)
                lmat = jnp.exp(jnp.where(tri, cum[:, e:e + 1] - cum_t[e:e + 1, :], -jnp.inf))
                gmat = cbm * lmat
                dy_e = dyv[:, esl].astype(bf16)
                dgm = _dot(dy_e, xdt[:, esl], NT)
                dxd_sc[:, esl] = _dot(gmat, dy_e, TN)
                dcb = dcb + dgm * lmat
                mm = dgm * gmat
                rowsum_m = rowsum_m + jnp.sum(mm, axis=1, keepdims=True) * (head_row == e).astype(f32)
                colsum_m = colsum_m + (head_col == e).astype(f32) * jnp.sum(mm, axis=0, keepdims=True)
                j = e - g * epg
                dh_sc[e] = dh_new[j * SSD_P:(j + 1) * SSD_P, :] + dec[:, e:e + 1] * dh_sc[e]
            dx_ref[:, bsl] = dbg + _dot(dcb, cb_, TN)
            dx_ref[:, csl] = dcg + _dot(dcb, bb, NN)
        ch_all = jnp.concatenate(ch_parts, axis=1)
        bds_all = jnp.concatenate(bds_parts, axis=1)
        dxdt = bds_all * ff_x + dxd_sc[...]
        dx_ref[:, :width] = dxdt * dt_x + dyv * dsk_x
        sums = _ones_dot_right(jnp.concatenate([dxdt * xs, dyv * ch_all, bds_all * xdt, dyv * xs], axis=0), gather)
        ddtx_all = sums[0:CHUNK]
        dff = sums[2 * CHUNK:3 * CHUNK] * ff
        ddsk = jnp.sum(sums[3 * CHUNK:4 * CHUNK], axis=0, keepdims=True)
        dclast = jnp.sum(dff, axis=0, keepdims=True) + dec * hh
        eye = (lax.broadcasted_iota(jnp.int32, (hs, hs), 0) == lax.broadcasted_iota(jnp.int32, (hs, hs), 1)).astype(bf16)
        colsum_t = sum(lax.dot_general(v, eye, TN, preferred_element_type=f32) for v in _split3(colsum_m))
        dcum_all = sums[CHUNK:2 * CHUNK] * ee - dff + rowsum_m - colsum_t + dclast * last_row
        ii = lax.broadcasted_iota(jnp.int32, (CHUNK, CHUNK), 0)
        jj = lax.broadcasted_iota(jnp.int32, (CHUNK, CHUNK), 1)
        da = _ones_dot_left((jj >= ii).astype(bf16), dcum_all)
        ddt = da * a_neg + ddtx_all
        dalog_ref[...] += jnp.sum(da * dtv, axis=0, keepdims=True) * a_neg
        draw = ddt * _sigmoid(raw)
        place = (lax.broadcasted_iota(jnp.int32, (hs, LANES), 0) + ROPE
                 == lax.broadcasted_iota(jnp.int32, (hs, LANES), 1)).astype(bf16)
        ddt_ref[...] = _ones_dot_right(draw, place)
        ddtb_ref[...] += jnp.sum(draw, axis=0, keepdims=True)
        ddsk_ref[...] += ddsk

    rev = lambda i: nc - 1 - i
    par = lambda a: pl.BlockSpec(a.shape, lambda i: (0,) * a.ndim)
    acc = pl.BlockSpec((1, hs), lambda i: (0, 0))
    acc_shape = jax.ShapeDtypeStruct((1, hs), f32)
    res, cres = _call(
        body, "ssd_bwd",
        [jax.ShapeDtypeStruct((t, cdim), f32), jax.ShapeDtypeStruct((t, LANES), f32), acc_shape, acc_shape, acc_shape],
        (nc,),
        [pl.BlockSpec((CHUNK, cdim), lambda i: (rev(i), 0)), pl.BlockSpec((CHUNK, hs), lambda i: (rev(i), 0)),
         pl.BlockSpec((hs, CHUNK), lambda i: (0, rev(i))), par(dtb), par(dtb_t), par(alog), par(alog_t),
         par(dskip), pl.BlockSpec((1, hs, SSD_P, SSD_N), lambda i: (rev(i), 0, 0, 0)),
         pl.BlockSpec((CHUNK, width), lambda i: (rev(i), 0))],
        [pl.BlockSpec((CHUNK, cdim), lambda i: (rev(i), 0)), pl.BlockSpec((CHUNK, LANES), lambda i: (rev(i), 0)),
         acc, acc, acc],
        [pltpu.VMEM((hs, SSD_P, SSD_N), f32), pltpu.VMEM((CHUNK, width), f32)], ("arbitrary",),
        [xbc, dt, dt_t, dtb, dtb_t, alog, alog_t, dskip, hprev, dy], comm, at)
    return (*res, cres)


def _where_am_i():
    x, y, c = lax.axis_index("x"), lax.axis_index("y"), lax.axis_index("c")
    chips = [(1 - x, y), (x, 1 - y), (1 - x, 1 - y)]
    return x, y, c, chips


def _remote(src, dst, send_sems, recv_sems, k, to):
    return pltpu.make_async_remote_copy(src_ref=src, dst_ref=dst, send_sem=send_sems.at[k], recv_sem=recv_sems.at[k],
                                        device_id=to, device_id_type=MESH)


class _Comm:
    def __init__(self, args, outs, nsem, nphase, run):
        self.args, self.outs, self.nsem, self.nphase, self.run = list(args), list(outs), nsem, nphase, run

    def sems(self):
        return [pltpu.SemaphoreType.DMA((self.nsem,)), pltpu.SemaphoreType.DMA((self.nsem,))]


class _SemWindow:
    def __init__(self, ref, off):
        self.ref, self.off, self.at = ref, off, self

    def __getitem__(self, k):
        return self.ref.at[k + self.off]


def _both(a, b):
    assert a.nphase == b.nphase
    na, no = len(a.args), len(a.outs)

    def run(phase, srcs, outs, send_sems, recv_sems):
        a.run(phase, srcs[:na], outs[:no], send_sems, recv_sems)
        b.run(phase, srcs[na:], outs[no:], _SemWindow(send_sems, a.nsem), _SemWindow(recv_sems, a.nsem))

    return _Comm(a.args + b.args, a.outs + b.outs, a.nsem + b.nsem, a.nphase, run)


def _comm_call(comm, name):
    na, no = len(comm.args), len(comm.outs)

    def body(*refs):
        for phase in range(comm.nphase):
            comm.run(phase, refs[:na], refs[na:na + no], refs[na + no], refs[na + no + 1])

    return list(pl.pallas_call(body, name=name, out_shape=comm.outs, in_specs=[ANY] * na, out_specs=[ANY] * no,
                               scratch_shapes=comm.sems())(*comm.args))


def _half(ref, c, r2):
    return ref.at[pl.ds(c * r2, r2)]


def _gather_weights(shards, wholes):
    ns, nw = len(shards), len(wholes)
    per = 7

    def run(phase, srcs, outs, send_sems, recv_sems):
        x, y, c, chips = _where_am_i()
        me, sib = 2 * x + y, (x, y, 1 - c)
        def first():
            cps = []
            for w in range(ns + nw):
                src, out, base = srcs[w], outs[w], per * w
                halved = w < ns
                r2 = src.shape[0] // 2
                piece = _half(src, c, r2) if halved else src
                for j, (cx, cy) in enumerate(chips):
                    dst = _half(out.at[me], c, r2) if halved else out.at[me]
                    cps.append(_remote(piece, dst, send_sems, recv_sems, base + j, (cx, cy, c)))
                cps.append(_remote(src, out.at[me], send_sems, recv_sems, base + 6, sib))
            return cps

        def passed():
            cps = []
            for w in range(ns):
                r2 = srcs[w].shape[0] // 2
                for j, (cx, cy) in enumerate(chips):
                    got = _half(outs[w].at[2 * cx + cy], c, r2)
                    cps.append(_remote(got, got, send_sems, recv_sems, per * w + 3 + j, sib))
            return cps

        if phase == 0:
            for cp in first():
                cp.start()
        elif phase == 1:
            it = iter(passed())
            for w in range(ns + nw):
                src, out, base = srcs[w], outs[w], per * w
                r2 = src.shape[0] // 2
                for j, (cx, cy) in enumerate(chips):
                    got = _half(out.at[2 * cx + cy], c, r2) if w < ns else out.at[2 * cx + cy]
                    _remote(got, got, send_sems, recv_sems, base + j, sib).wait_recv()
                    if w < ns:
                        next(it).start()
        else:
            for w in range(ns + nw):
                src, out, base = srcs[w], outs[w], per * w
                r2 = src.shape[0] // 2
                if w < ns:
                    for j, (cx, cy) in enumerate(chips):
                        got = _half(out.at[2 * cx + cy], 1 - c, r2)
                        _remote(got, got, send_sems, recv_sems, base + 3 + j, sib).wait_recv()
                _remote(src, out.at[me], send_sems, recv_sems, base + 6, sib).wait_recv()
            for cp in first() + passed():
                cp.wait_send()

    args = list(shards) + list(wholes)
    outs = [jax.ShapeDtypeStruct((4,) + a.shape, a.dtype) for a in args]
    return _Comm(args, outs, per * len(args), 3, run)


def _sibling_send_halves(gs):
    n = len(gs)

    def run(phase, srcs, outs, send_sems, recv_sems):
        x, y, c, _ = _where_am_i()
        sib = (x, y, 1 - c)
        cps = []
        for w in range(n):
            r2 = srcs[w].shape[1] // 2
            for k in range(4):
                cps.append(_remote(_half(srcs[w].at[k], 1 - c, r2), outs[w].at[k], send_sems, recv_sems, 4 * w + k, sib))
        for cp in cps:
            cp.start() if phase == 0 else cp.wait()

    outs = [jax.ShapeDtypeStruct((4, g.shape[1] // 2, g.shape[2]), g.dtype) for g in gs]
    return _Comm(gs, outs, 4 * n, 2, run)


def _chips_exchange(ps):
    n = len(ps)

    def run(phase, srcs, outs, send_sems, recv_sems):
        x, y, c, chips = _where_am_i()
        me = 2 * x + y
        cps = []
        for w in range(n):
            for j, (cx, cy) in enumerate(chips):
                cps.append(_remote(srcs[w].at[2 * cx + cy], outs[w].at[me], send_sems, recv_sems, 3 * w + j, (cx, cy, c)))
        if phase == 0:
            for cp in cps:
                cp.start()
        else:
            for w in range(n):
                for j, (cx, cy) in enumerate(chips):
                    got = outs[w].at[2 * cx + cy]
                    _remote(got, got, send_sems, recv_sems, 3 * w + j, (cx, cy, c)).wait_recv()
            for cp in cps:
                cp.wait_send()

    outs = [jax.ShapeDtypeStruct(p.shape, p.dtype) for p in ps]
    return _Comm(ps, outs, 3 * n, 2, run)


def _sibling_swap(rs):
    n = len(rs)

    def run(phase, srcs, outs, send_sems, recv_sems):
        x, y, c, _ = _where_am_i()
        for w in range(n):
            cp = _remote(srcs[w], outs[w], send_sems, recv_sems, w, (x, y, 1 - c))
            cp.start() if phase == 0 else cp.wait()

    outs = [jax.ShapeDtypeStruct(r.shape, r.dtype) for r in rs]
    return _Comm(rs, outs, n, 2, run)


def _gather_all(v):
    rows = v.shape[0]

    def body(x_ref, out_ref, send_sems, recv_sems, local_sem):
        x, y, c, chips = _where_am_i()
        me, sib = (x, y, c), (x, y, 1 - c)
        blk = lambda px, py, pc: out_ref.at[4 * px + 2 * py + pc]
        mine = pltpu.make_async_copy(x_ref, blk(*me), local_sem)
        mine.start()
        first = [_remote(x_ref, blk(*me), send_sems, recv_sems, 0, sib)]
        first += [_remote(x_ref, blk(*me), send_sems, recv_sems, 1 + j, (*chip, c)) for j, chip in enumerate(chips)]
        for cp in first:
            cp.start()
        passed = [_remote(blk(*chip, c), blk(*chip, c), send_sems, recv_sems, 4 + j, sib) for j, chip in enumerate(chips)]
        for j, chip in enumerate(chips):
            _remote(blk(*chip, c), blk(*chip, c), send_sems, recv_sems, 1 + j, me).wait_recv()
            passed[j].start()
        _remote(blk(*sib), blk(*sib), send_sems, recv_sems, 0, me).wait_recv()
        for j, chip in enumerate(chips):
            _remote(blk(*chip, 1 - c), blk(*chip, 1 - c), send_sems, recv_sems, 4 + j, me).wait_recv()
        for cp in first + passed:
            cp.wait_send()
        mine.wait()

    vm = pl.BlockSpec(memory_space=pltpu.VMEM)
    return pl.pallas_call(
        body, name="gather_all", out_shape=jax.ShapeDtypeStruct((8, rows, LANES), v.dtype),
        in_specs=[vm], out_specs=vm,
        scratch_shapes=[pltpu.SemaphoreType.DMA((7,)), pltpu.SemaphoreType.DMA((7,)), pltpu.SemaphoreType.DMA],
    )(v)


def _flat_pad(parts, total):
    v = jnp.concatenate([p.reshape(-1) for p in parts])
    return jnp.pad(v, (0, total - v.shape[0]))


def _adamw(w, g, m, v):
    m = ADAM_B1 * m + (1.0 - ADAM_B1) * g
    v = ADAM_B2 * v + (1.0 - ADAM_B2) * jnp.square(g)
    m_hat = m / (1.0 - ADAM_B1 ** ADAM_STEP)
    v_hat = v / (1.0 - ADAM_B2 ** ADAM_STEP)
    delta = -ADAM_LR * (m_hat / (jnp.sqrt(v_hat) + ADAM_EPS) + ADAM_WD * w)
    return delta, m, v


def _adamw_call(w, g, m, v, name):
    r, cdim = w.shape
    tm = _tile(r, ROW_TILE, 8)
    o = _rout(r, cdim, f32, tm)
    return _rowwise(_adamw, r, tm, [_rows(w, tm), _rows(g, tm), _rows(m, tm), _rows(v, tm)], [o, o, o], [], name)


def _adamw_halves(w, m, v, g_mine, g_sib, sp, name):
    _, r, cdim = w.shape
    r2 = r // 2
    tr = _row_tile(r2, cdim, 8)
    nb = r2 // tr

    def body(sp_ref, w_ref, m_ref, v_ref, ga_ref, gb_ref, g_out, d_out, m_out, v_out):
        g = jnp.where(pl.program_id(0) == sp_ref[1], ga_ref[...], gb_ref[...])
        delta, mn, vn = _adamw(w_ref[...], g, m_ref[...], v_ref[...])
        g_out[...] = g
        d_out[...] = delta
        m_out[...] = mn
        v_out[...] = vn

    full = pl.BlockSpec((None, tr, cdim), lambda h, i, s: (0, h * nb + i, 0))
    mine = pl.BlockSpec((tr, cdim), lambda h, i, s: (jnp.where(h == s[1], i, 0), 0))
    sib = pl.BlockSpec((tr, cdim), lambda h, i, s: (jnp.where(h == s[1], 0, i), 0))
    out = jax.ShapeDtypeStruct((1, r, cdim), f32)
    gs = pltpu.PrefetchScalarGridSpec(num_scalar_prefetch=1, grid=(2, nb), in_specs=[full, full, full, mine, sib],
                                      out_specs=[full, full, full, full])
    return pl.pallas_call(body, name=name, out_shape=[out, out, out, out], grid_spec=gs,
                          compiler_params=_params(("parallel", "parallel")))(sp, w, m, v, g_mine, g_sib)


MIXER = ("w_in", "w_uq", "w_ukv", "w_out")
FFN = ("w_gate", "w_up", "w_down")
BIG = MIXER + FFN
SMALL = ("q_norm_w", "kv_norm_w", "conv_w", "conv_b", "dt_bias", "a_log", "d_skip", "ssd_norm_w", "attn_out_norm_w",
         "pre_mix_norm_w", "post_mix_norm_w", "pre_ffn_norm_w", "post_ffn_norm_w")
ORDER = ("w_in", "q_norm_w", "w_uq", "kv_norm_w", "w_ukv", "conv_w", "conv_b", "dt_bias", "a_log", "d_skip",
         "ssd_norm_w", "attn_out_norm_w", "w_out", "pre_mix_norm_w", "post_mix_norm_w", "pre_ffn_norm_w",
         "post_ffn_norm_w", "w_gate", "w_up", "w_down")


def kernel(x, positions, w_in, q_norm_w, w_uq, kv_norm_w, w_ukv, conv_w, conv_b, dt_bias, a_log, d_skip, ssd_norm_w, attn_out_norm_w, w_out, pre_mix_norm_w, post_mix_norm_w, pre_ffn_norm_w, post_ffn_norm_w, w_gate, w_up, w_down, loss_target, m_w_in, m_q_norm_w, m_w_uq, m_kv_norm_w, m_w_ukv, m_conv_w, m_conv_b, m_dt_bias, m_a_log, m_d_skip, m_ssd_norm_w, m_attn_out_norm_w, m_w_out, m_pre_mix_norm_w, m_post_mix_norm_w, m_pre_ffn_norm_w, m_post_ffn_norm_w, m_w_gate, m_w_up, m_w_down, v_w_in, v_q_norm_w, v_w_uq, v_kv_norm_w, v_w_ukv, v_conv_w, v_conv_b, v_dt_bias, v_a_log, v_d_skip, v_ssd_norm_w, v_attn_out_norm_w, v_w_out, v_pre_mix_norm_w, v_post_mix_norm_w, v_pre_ffn_norm_w, v_post_ffn_norm_w, v_w_gate, v_w_up, v_w_down):
    local = dict(locals())
    wts = {n: local[n][0] for n in ORDER}
    mom_m = {n: local["m_" + n][0] for n in ORDER}
    mom_v = {n: local["v_" + n][0] for n in ORDER}
    xs = x[0]
    tgt = loss_target[0]
    t, d = xs.shape
    nchip = 4
    my_x, my_y, my_c = lax.axis_index("x"), lax.axis_index("y"), lax.axis_index("c")
    my_chip = 2 * my_x + my_y

    mla_w = d // 2
    nh = mla_w // V_HEAD
    width = d - mla_w
    hs = width // SSD_P
    gn = SSD_G * SSD_N
    cdim = width + 2 * gn
    in_sizes = (Q_RANK, KV_RANK, ROPE, width, cdim, hs)
    d_in = sum(in_sizes)
    tail = LANES
    off_xbc = 0
    off_cq = cdim
    off_ckv = off_cq + Q_RANK
    off_z = off_ckv + KV_RANK
    off_tail = off_z + width
    d_in_p = _round_up(off_tail + tail, 256)
    gw = width // SSD_G
    assert off_cq % Q_RANK == 0 and off_ckv % KV_RANK == 0 and off_z % gw == 0 and off_tail % LANES == 0
    qk_head = NOPE + ROPE
    scale = qk_head ** -0.5
    tm = _tile(t, ROW_TILE, 8)
    tmw = _tile(t, ROW_TILE // 2, 8)

    sp = jnp.stack([my_chip, my_c]).astype(jnp.int32)

    def pair_sums(names, gl, from_sib):
        res = []
        for n, g, fs in zip(names, gl, from_sib):
            _, r2, cs = fs.shape
            tr = _row_tile(r2, cs, 16)
            nb = r2 // tr
            res.append(_blocked(
                lambda a, b: a + b, (nchip, nb),
                [(g, (None, tr, cs), lambda k, i, s, nb=nb: (k, s[1] * nb + i, 0)),
                 (fs, (None, tr, cs), lambda k, i, s: (k, i, 0))],
                [((nchip, r2, cs), bf16, (None, tr, cs), lambda k, i, s: (k, i, 0))], [], "pair_sum_" + n, sp=sp)[0])
        return res

    def chip_sums(names, gl, from_sib, from_chips):
        res = []
        for n, g, fs, fc in zip(names, gl, from_sib, from_chips):
            _, r2, cs = fs.shape
            tr = _row_tile(r2, cs, 16)
            nb = r2 // tr
            res.append(_blocked(
                lambda a, b, r1, r2_, r3: ((a + b) + r1.astype(f32)) + r2_.astype(f32) + r3.astype(f32), (nb,),
                [(g, (None, tr, cs), lambda i, s, nb=nb: (s[0], s[1] * nb + i, 0)),
                 (fs, (None, tr, cs), lambda i, s: (s[0], i, 0)),
                 (fc, (None, tr, cs), lambda i, s: (s[0] ^ 1, i, 0)),
                 (fc, (None, tr, cs), lambda i, s: (s[0] ^ 2, i, 0)),
                 (fc, (None, tr, cs), lambda i, s: (s[0] ^ 3, i, 0))],
                [((r2, cs), f32, (tr, cs), lambda i, s: (i, 0))], [], "chip_sum_" + n, sp=sp)[0])
        return res

    ck, ccs = wts["conv_w"].shape
    cs_in = wts["w_in"].shape[1]
    flat_t = lambda a3: jnp.swapaxes(a3, 1, 2).reshape(1, cs_in * d // LANES, LANES)
    (u,), (w_in_g, conv_g) = _rowwise(
        lambda a, w: _rms(a, w), t, tm, [_rows(xs, tm), _par(wts["pre_mix_norm_w"].reshape(1, -1))],
        [_rout(t, d, bf16, tm)], [], "pre_mix_norm",
        comm=_gather_weights([flat_t(w_in)[0].astype(bf16)], [wts["conv_w"]]), at=(0.0, 0.5))
    cat_cols = lambda g: jnp.concatenate([g[k] for k in range(nchip)], axis=1)
    conv_full = cat_cols(conv_g)
    mixer_gather = _gather_weights([wts[n].astype(bf16) for n in MIXER[1:]], [])
    ffn_gather = _gather_weights([wts[n].astype(bf16) for n in FFN[:2]], [])
    w_down_b = wts["w_down"].astype(bf16)
    dn2 = w_down_b.shape[0] // 2
    down_gathers = [_gather_weights([w_down_b[:dn2]], []), _gather_weights([w_down_b[dn2:]], [])]

    wi = w_in_g.reshape(nchip * cs_in, d)
    o = np.cumsum((0,) + in_sizes)
    seg = lambda i: wi[o[i]:o[i + 1]]
    w_in_pt = jnp.concatenate([seg(4), seg(0), seg(1), seg(3), seg(2), seg(5),
                               jnp.zeros((d_in_p - off_tail - ROPE - hs, d), bf16)], axis=0)

    inv_freq = ROPE_THETA ** (-jnp.arange(0, ROPE, 2, dtype=f32) / ROPE)
    ang = positions[0].astype(f32)[:, None] * inv_freq
    cos, sin = jnp.cos(ang), jnp.sin(ang)
    z32, z64, z96 = jnp.zeros((t, 32), f32), jnp.zeros((t, 64), f32), jnp.zeros((t, 96), f32)
    cosp = jnp.concatenate([cos, cos, z64], axis=1)
    sina = jnp.concatenate([-sin, z96], axis=1)
    sinb = jnp.concatenate([z32, sin, z64], axis=1)

    row = lambda a: a.reshape(1, -1)
    w_pre_mix, w_post_mix = row(wts["pre_mix_norm_w"]), row(wts["post_mix_norm_w"])
    w_pre_ffn, w_post_ffn = row(wts["pre_ffn_norm_w"]), row(wts["post_ffn_norm_w"])
    w_qn, w_kvn = row(wts["q_norm_w"]), row(wts["kv_norm_w"])
    w_attn_n, w_ssd_n = row(wts["attn_out_norm_w"]), row(wts["ssd_norm_w"])
    conv_b_r = row(wts["conv_b"])
    dtb, alog, dskip = row(wts["dt_bias"]), row(wts["a_log"]), row(wts["d_skip"])
    dtb_t, alog_t = dtb.reshape(hs, 1), alog.reshape(hs, 1)

    proj, (w_uq_g, w_ukv_f, w_out_g) = _matmul([(u, w_in_pt)], "nt", f32, "in_proj", comm=mixer_gather,
                                               at=(0.0, 0.6))
    w_uq_p = jnp.pad(cat_cols(w_uq_g).reshape(Q_RANK, nh, qk_head), ((0, 0), (0, 0), (0, QK_PAD - qk_head))
                     ).reshape(Q_RANK, nh * QK_PAD)
    w_out_f = w_out_g.reshape(-1, w_out_g.shape[2])
    cq_in = _rows(proj, tm, Q_RANK, off_cq // Q_RANK)
    ckv_in = _rows(proj, tm, KV_RANK, off_ckv // KV_RANK)
    cqn = _rowwise(lambda a, w: _rms(a, w), t, tm, [cq_in, _par(w_qn)], [_rout(t, Q_RANK, bf16, tm)], [], "q_norm")[0]
    ckvn = _rowwise(lambda a, w: _rms(a, w), t, tm, [ckv_in, _par(w_kvn)], [_rout(t, KV_RANK, bf16, tm)], [],
                    "kv_norm")[0]
    q_raw = _matmul([(cqn, w_uq_p)], "nn", f32, "q_up")
    kv = _matmul([(ckvn, w_ukv_f)], "nn", bf16, "kv_up")

    def q_rope_fn(qt, cp, sa, sb):
        parts = []
        for h in range(nh):
            parts.append(qt[:, h * QK_PAD: h * QK_PAD + NOPE])
            parts.append(_rope(qt[:, h * QK_PAD + NOPE:(h + 1) * QK_PAD], cp, sa, sb))
        return jnp.concatenate(parts, axis=1)

    tail_cb = off_tail // LANES
    q2 = _rowwise(q_rope_fn, t, tm, [_rows(q_raw, tm), _rows(cosp, tm), _rows(sina, tm), _rows(sinb, tm)],
                  [_rout(t, nh * QK_PAD, bf16, tm)], [], "q_rope")[0]
    kr2 = _rowwise(_rope, t, tm, [_rows(proj, tm, LANES, tail_cb), _rows(cosp, tm), _rows(sina, tm), _rows(sinb, tm)],
                   [_rout(t, LANES, bf16, tm)], [], "k_rope")[0]
    o_att, lse, ffn_w = _flash_fwd(q2, kv, kr2, nh, scale, ffn_gather, (0.0, 0.75))
    w_gate_f, w_up_f = ffn_w

    xbc_act = _conv_fwd(proj, conv_full, conv_b_r, cdim, tm)
    dt_raw, dt_raw_t = _blocked(
        lambda blk: (blk[:, ROPE:ROPE + hs], blk.T[ROPE:ROPE + hs, :]), (t // tm,),
        [_rows(proj, tm, LANES, tail_cb)],
        [((t, hs), f32, (tm, hs), lambda i: (i, 0)), ((hs, t), f32, (hs, tm), lambda i: (0, i))], [], "dt_split")
    y_ssd, hprev, (down_a,) = _ssd_fwd(xbc_act, dt_raw, dt_raw_t, dtb, dtb_t, alog, alog_t, dskip, width,
                                       down_gathers[0], (0.0, 0.6))
    z_ins = [_rows(proj, tm, gw, off_z // gw + i) for i in range(SSD_G)]

    def mix_norms_fn(ov, yv, *rest):
        zs, wa, ws = rest[:SSD_G], rest[SSD_G], rest[SSD_G + 1]
        outs = [_rms(ov, wa)]
        for i in range(SSD_G):
            sl = slice(i * gw, (i + 1) * gw)
            outs.append(_rms(yv[:, sl] * (zs[i] * _sigmoid(zs[i])), ws[:, sl]))
        return jnp.concatenate(outs, axis=1)

    cat = _rowwise(mix_norms_fn, t, tm, [_rows(o_att, tm), _rows(y_ssd, tm)] + z_ins + [_par(w_attn_n), _par(w_ssd_n)],
                   [_rout(t, d, bf16, tm)], [], "attn_ssd_out_norms")[0]
    mix, (down_b,) = _matmul([(cat, w_out_f)], "nn", f32, "out_proj", comm=down_gathers[1], at=(0.0, 0.6))
    w_down_f = jnp.concatenate([down_a, down_b], axis=1).reshape(-1, down_a.shape[2])

    def post_mix_fn(mx, xv, w1, w2):
        h1v = xv + _rms(mx, w1)
        return h1v, _rms(h1v, w2)

    h1, v_in = _rowwise(post_mix_fn, t, tmw, [_rows(mix, tmw), _rows(xs, tmw), _par(w_post_mix), _par(w_pre_ffn)],
                        [_rout(t, d, f32, tmw), _rout(t, d, bf16, tmw)], [], "post_mix_pre_ffn_norm")
    g_ff, u_ff, act = _ffn_up(v_in, w_gate_f, w_up_f)
    ffn = _matmul([(act, w_down_f)], "nn", f32, "ffn_down")

    def final_fn(fv, h1v, tg, w):
        h2 = h1v + _rms(fv, w)
        err = h2 - tg
        lpart = 0.5 * jnp.sum(jnp.sum(err * err, axis=1, keepdims=True), axis=0, keepdims=True) / d
        dh2 = err / d
        dff, dw = _rms_bwd(fv, w, dh2)
        return dff, dh2, jnp.broadcast_to(lpart, (1, LANES)), dw

    dffn, dh2, loss_acc, g_post_ffn = _rowwise(
        final_fn, t, tmw, [_rows(ffn, tmw), _rows(h1, tmw), _rows(tgt, tmw), _par(w_post_ffn)],
        [_rout(t, d, bf16, tmw), _rout(t, d, f32, tmw)], [(1, LANES), (1, d)], "loss_post_ffn_norm_bwd")
    loss = lax.psum(loss_acc[0, 0], ("x", "y", "c"))

    dg_ff, du_ff = _ffn_down_bwd(dffn, w_down_f, g_ff, u_ff)
    gw_down = _matmul([(act, dffn)], "tn", f32, "grad_w_down")
    gw_gate = _matmul([(v_in, dg_ff)], "tn", f32, "grad_w_gate", out_chunks=nchip)
    gw_up = _matmul([(v_in, du_ff)], "tn", f32, "grad_w_up", out_chunks=nchip)
    ffn_g = [gw_gate, gw_up, gw_down.reshape(nchip, -1, d)]
    dv_in, ffn_from_sib = _matmul([(dg_ff, w_gate_f), (du_ff, w_up_f)], "nt", f32, "ffn_up_bwd",
                                  comm=_sibling_send_halves(ffn_g), at=(0.0,))
    ffn_pairs = pair_sums(FFN, ffn_g, ffn_from_sib)

    def mid_bwd_fn(h1v, dvv, dh2v, mx, w_pf, w_pm):
        dxn, dw_pf = _rms_bwd(h1v, w_pf, dvv)
        dh1v = dh2v + dxn
        dmx, dw_pm = _rms_bwd(mx, w_pm, dh1v)
        return dh1v, dmx, dw_pf, dw_pm

    dh1, dmix, g_pre_ffn, g_post_mix = _rowwise(
        mid_bwd_fn, t, tmw, [_rows(h1, tmw), _rows(dv_in, tmw), _rows(dh2, tmw), _rows(mix, tmw),
                             _par(w_pre_ffn), _par(w_post_mix)],
        [_rout(t, d, f32, tmw), _rout(t, d, bf16, tmw)], [(1, d), (1, d)], "pre_ffn_post_mix_norm_bwd")
    dcat = _matmul([(dmix, w_out_f)], "nt", f32, "out_proj_bwd")
    gw_out = _matmul([(cat, dmix)], "tn", f32, "grad_w_out")

    def mix_norms_bwd_fn(ov, yv, *rest):
        zs, dcv, wa, ws = rest[:SSD_G], rest[SSD_G], rest[SSD_G + 1], rest[SSD_G + 2]
        dov, dwa = _rms_bwd(ov, wa, dcv[:, :mla_w])
        dl = [jnp.broadcast_to(jnp.sum(dov[:, h * V_HEAD:(h + 1) * V_HEAD] * ov[:, h * V_HEAD:(h + 1) * V_HEAD],
                                       axis=1, keepdims=True), (ov.shape[0], V_HEAD)) for h in range(nh)]
        dys, dzs, dws = [], [], []
        for i in range(SSD_G):
            sl = slice(i * gw, (i + 1) * gw)
            zv, yi = zs[i], yv[:, sl]
            sg = _sigmoid(zv)
            sz = zv * sg
            dgi, dwi = _rms_bwd(yi * sz, ws[:, sl], dcv[:, mla_w + i * gw: mla_w + (i + 1) * gw])
            dys.append(dgi * sz)
            dzs.append(dgi * yi * (sg * (1.0 + zv * (1.0 - sg))))
            dws.append(dwi)
        cc = lambda a: jnp.concatenate(a, axis=1)
        return dov, cc(dl), cc(dys), cc(dzs), dwa, cc(dws)

    do_att, delta, dy_ssd, dz, g_attn_n, g_ssd_n = _rowwise(
        mix_norms_bwd_fn, t, tm,
        [_rows(o_att, tm), _rows(y_ssd, tm)] + z_ins + [_rows(dcat, tm), _par(w_attn_n), _par(w_ssd_n)],
        [_rout(t, mla_w, bf16, tm), _rout(t, mla_w, f32, tm), _rout(t, width, f32, tm), _rout(t, width, bf16, tm)],
        [(1, mla_w), (1, width)], "attn_ssd_out_norms_bwd")
    dkv, dkr_h, dq2, gu_from_chips = _flash_bwd(q2, kv, kr2, do_att, lse, delta, nh, scale,
                                                _chips_exchange(ffn_pairs[:2]), (0.0,))

    def q_rope_bwd_fn(dqt, cp, sa, sb):
        parts = []
        for h in range(nh):
            parts.append(dqt[:, h * QK_PAD: h * QK_PAD + NOPE])
            parts.append(_rope_bwd(dqt[:, h * QK_PAD + NOPE:(h + 1) * QK_PAD], cp, sa, sb))
        return jnp.concatenate(parts, axis=1)

    dq_raw = _rowwise(q_rope_bwd_fn, t, tm, [_rows(dq2, tm), _rows(cosp, tm), _rows(sina, tm), _rows(sinb, tm)],
                      [_rout(t, nh * QK_PAD, bf16, tm)], [], "q_rope_bwd")[0]

    def k_rope_bwd_fn(dk, cp, sa, sb):
        tot = dk[:, 0:LANES]
        for h in range(1, nh):
            tot = tot + dk[:, h * LANES:(h + 1) * LANES]
        return _rope_bwd(tot, cp, sa, sb)

    dkr = _rowwise(k_rope_bwd_fn, t, tm, [_rows(dkr_h, tm), _rows(cosp, tm), _rows(sina, tm), _rows(sinb, tm)],
                   [_rout(t, LANES, f32, tm)], [], "k_rope_bwd")[0]
    gw_uq_p = _matmul([(cqn, dq_raw)], "tn", f32, "grad_w_uq")
    gw_ukv = _matmul([(ckvn, dkv)], "tn", f32, "grad_w_ukv", out_chunks=nchip)
    dcqn = _matmul([(dq_raw, w_uq_p)], "nt", f32, "q_up_bwd")
    dckvn = _matmul([(dkv, w_ukv_f)], "nt", f32, "kv_up_bwd")

    def lat_norm_bwd_fn(a, w, dyv):
        return _rms_bwd(a, w, dyv)

    dcq, g_qn = _rowwise(lat_norm_bwd_fn, t, tm, [cq_in, _par(w_qn), _rows(dcqn, tm)],
                         [_rout(t, Q_RANK, bf16, tm)], [(1, Q_RANK)], "q_norm_bwd")
    dckv, g_kvn = _rowwise(lat_norm_bwd_fn, t, tm, [ckv_in, _par(w_kvn), _rows(dckvn, tm)],
                           [_rout(t, KV_RANK, bf16, tm)], [(1, KV_RANK)], "kv_norm_bwd")

    dxbc_act, ddt_raw, g_alog, g_dskip, g_dtb, down_from_chips = _ssd_bwd(
        xbc_act, dt_raw, dt_raw_t, dtb, dtb_t, alog, alog_t, dskip, hprev, dy_ssd, width,
        _chips_exchange(ffn_pairs[2:]), (0.0,))
    ffn_halves = chip_sums(FFN, ffn_g, ffn_from_sib, gu_from_chips + down_from_chips)
    dpre, gcw0, gcw1, gcw2, gcw3, g_conv_b = _conv_bwd_pre(proj, conv_full, conv_b_r, dxbc_act, cdim, tm)
    g_conv_w = jnp.concatenate([gcw0, gcw1, gcw2, gcw3], axis=0)
    dxbc = _conv_bwd_dx(dpre, conv_full, tm)

    def dproj_fn(a, b, c, e, kr_blk, dt_blk):
        parts = [a, b, c, e, (kr_blk + dt_blk).astype(bf16)]
        if d_in_p > off_tail + tail:
            parts.append(jnp.zeros((a.shape[0], d_in_p - off_tail - tail), bf16))
        return jnp.concatenate(parts, axis=1)

    dproj = _rowwise(dproj_fn, t, tm, [_rows(dxbc, tm), _rows(dcq, tm), _rows(dckv, tm), _rows(dz, tm),
                                       _rows(dkr, tm), _rows(ddt_raw, tm)],
                     [_rout(t, d_in_p, bf16, tm)], [], "in_proj_grad_pack")[0]
    split_cols = lambda gf: jnp.stack(jnp.split(gf, nchip, axis=1))
    split_rows = lambda gf: gf.reshape(nchip, gf.shape[0] // nchip, gf.shape[1])
    gw_uq = gw_uq_p.reshape(Q_RANK, nh, QK_PAD)[:, :, :qk_head].reshape(Q_RANK, nh * qk_head)
    rest_g = [split_cols(gw_uq), gw_ukv, split_rows(gw_out)]
    gw_in_pt, both = _matmul([(dproj, u)], "tn", f32, "grad_w_in",
                             comm=_both(_sibling_send_halves(rest_g), _sibling_swap(ffn_halves)),
                             at=(0.0,))
    rest_from_sib, ffn_sib_halves = both[:len(rest_g)], both[len(rest_g):]
    gseg = lambda a, b: gw_in_pt[a:b]
    gw_in_t = jnp.concatenate([gseg(off_cq, off_ckv), gseg(off_ckv, off_z), gseg(off_tail, off_tail + ROPE),
                               gseg(off_z, off_tail), gseg(off_xbc, off_cq),
                               gseg(off_tail + ROPE, off_tail + ROPE + hs)], axis=0)
    in_g = [gw_in_t.reshape(nchip, cs_in * d // LANES, LANES)]
    rest_pairs = pair_sums(MIXER[1:], rest_g, rest_from_sib)
    du_in, both = _matmul([(dproj, w_in_pt)], "nn", f32, "in_proj_bwd",
                          comm=_both(_sibling_send_halves(in_g), _chips_exchange(rest_pairs)), at=(0.0,))
    in_from_sib, rest_from_chips = both[:1], both[1:]
    in_pairs = pair_sums(MIXER[:1], in_g, in_from_sib)

    def first_bwd_fn(xv, duv, dh1v, w):
        dxn, dw = _rms_bwd(xv, w, duv)
        return dh1v + dxn, dw

    (grad_x, g_pre_mix), in_from_chips = _rowwise(
        first_bwd_fn, t, tmw, [_rows(xs, tmw), _rows(du_in, tmw), _rows(dh1, tmw), _par(w_pre_mix)],
        [_rout(t, d, f32, tmw)], [(1, d)], "pre_mix_norm_bwd", comm=_chips_exchange(in_pairs), at=(0.0,))

    mix_halves = (chip_sums(MIXER[:1], in_g, in_from_sib, in_from_chips)
                  + chip_sums(MIXER[1:], rest_g, rest_from_sib, rest_from_chips))
    halves = mix_halves + ffn_halves
    sib_halves = _comm_call(_sibling_swap(mix_halves), "sibling_swap") + ffn_sib_halves
    gshard = {}

    gsmall = {"q_norm_w": g_qn, "kv_norm_w": g_kvn, "conv_w": g_conv_w, "conv_b": g_conv_b, "dt_bias": g_dtb,
              "a_log": g_alog, "d_skip": g_dskip, "ssd_norm_w": g_ssd_n, "attn_out_norm_w": g_attn_n,
              "pre_mix_norm_w": g_pre_mix, "post_mix_norm_w": g_post_mix, "pre_ffn_norm_w": g_pre_ffn,
              "post_ffn_norm_w": g_post_ffn}
    small_sizes = [int(np.prod(gsmall[n].shape)) for n in SMALL]
    srows = _round_up(-(-sum(small_sizes) // LANES), 8)
    spart = _flat_pad([gsmall[n] for n in SMALL], srows * LANES).reshape(srows, LANES)
    sall = _gather_all(spart)

    def sum8_fn(a):
        tot = a[0]
        for k in range(1, 8):
            tot = tot + a[k]
        return tot

    ssum = _blocked(sum8_fn, (1,), [(sall, sall.shape, lambda i: (0, 0, 0))],
                    [((srows, LANES), f32, (srows, LANES), lambda i: (0, 0))], [], "small_grad_sum")[0].reshape(-1)
    gred = {}
    off = 0
    for n, sz in zip(SMALL, small_sizes):
        gred[n] = ssum[off:off + sz].reshape(gsmall[n].shape)
        off += sz
    gshard["conv_w"] = lax.dynamic_slice_in_dim(gred["conv_w"], my_chip * ccs, ccs, axis=1)
    for n in SMALL:
        if n != "conv_w":
            gshard[n] = gred[n].reshape(wts[n].shape)

    delta, new_m, new_v = {}, {}, {}
    big_out = {}
    for n, mine_h, sib_h in zip(BIG, halves, sib_halves):
        view = flat_t if n == "w_in" else (lambda a3: a3)
        res = _adamw_halves(view(local[n]), view(local["m_" + n]), view(local["v_" + n]), mine_h, sib_h, sp,
                            "adamw_" + n)
        big_out[n] = [jnp.swapaxes(a.reshape(1, cs_in, d), 1, 2) for a in res] if n == "w_in" else res
    pack = lambda src: _flat_pad([src[n] for n in SMALL], srows * LANES).reshape(srows, LANES)
    sd, sm, sv = _adamw_call(pack(wts), pack(gshard), pack(mom_m), pack(mom_v), "adamw_small")
    off = 0
    for n in SMALL:
        sz = int(np.prod(wts[n].shape))
        for dst, src in ((delta, sd), (new_m, sm), (new_v, sv)):
            dst[n] = src.reshape(-1)[off:off + sz].reshape(wts[n].shape)
        off += sz

    small_out = (gshard, delta, new_m, new_v)
    pick = lambda k: [big_out[n][k] if n in big_out else small_out[k][n][None] for n in ORDER]
    return (loss, grad_x[None], *pick(0), *pick(1), *pick(2), *pick(3))
```

```python
import functools

import numpy as np
import jax
import jax.numpy as jnp
from jax import lax
from jax.experimental import pallas as pl
from jax.experimental.pallas import tpu as pltpu

f32, bf16 = jnp.float32, jnp.bfloat16

EPS = 1e-6
V_HEAD = 128
NOPE = 128
ROPE = 64
QK_PAD = 256
Q_RANK = 512
KV_RANK = 512
ROPE_THETA = 10000.0
SSD_P = 64
SSD_G = 2
SSD_N = 128
SSD_K = 4
CHUNK = 128
ADAM_LR, ADAM_B1, ADAM_B2, ADAM_EPS, ADAM_WD, ADAM_STEP = 0.001, 0.9, 0.999, 1e-08, 0.01, 10

VMEM_LIMIT_BYTES = 48 * 1024 * 1024
LANES = 128
ATT_TILE = 1024
ATT_TILE_FWD = 1024
MM_TM, MM_TN, MM_TK = 1408, 1024, 2048
CHUNK_WHOLE_MAX = 1536
ROW_TILE = 512
ROW_BLOCK_BYTES = 2 * 1024 * 1024

NN = (((1,), (0,)), ((), ()))
NT = (((1,), (1,)), ((), ()))
TN = (((0,), (0,)), ((), ()))
MESH = pl.DeviceIdType.MESH
ANY = pl.BlockSpec(memory_space=pl.ANY)


def _tile(dim, cap, align=LANES):
    if dim <= cap:
        return dim
    t = (cap // align) * align
    while t >= align:
        if dim % t == 0:
            return t
        t -= align
    raise ValueError(f"no tile for {dim} under {cap}")


def _row_tile(rows, cols, align):
    best = None
    for tr in range(align, rows + 1, align):
        if rows % tr == 0 and tr * cols * 4 <= ROW_BLOCK_BYTES:
            best = tr
    return best or rows


def _round_up(n, m):
    return -(-n // m) * m


def _params(sem):
    return pltpu.CompilerParams(dimension_semantics=sem, vmem_limit_bytes=VMEM_LIMIT_BYTES)


def _dot(a, b, dims):
    return lax.dot_general(a.astype(bf16), b.astype(bf16), dims, preferred_element_type=f32)


def _call(body, name, out_shape, grid, in_specs, out_specs, scratch_shapes, sem, args, comm=None, at=None,
          prefetch=()):
    npf = len(prefetch)
    if comm is None:
        gs = pltpu.PrefetchScalarGridSpec(num_scalar_prefetch=npf, grid=grid, in_specs=list(in_specs),
                                          out_specs=list(out_specs), scratch_shapes=list(scratch_shapes))
        res = pl.pallas_call(body, name=name, out_shape=list(out_shape), grid_spec=gs,
                             compiler_params=_params(sem))(*prefetch, *args)
        return list(res), []
    n_in, n_out, n_sc = len(args), len(out_shape), len(scratch_shapes)
    na, no = len(comm.args), len(comm.outs)
    steps = int(np.prod(grid))

    def full(*allrefs):
        pf, refs = allrefs[:npf], allrefs[npf:]
        cin = refs[n_in:n_in + na]
        o0 = n_in + na
        cout = refs[o0 + n_out:o0 + n_out + no]
        s0 = o0 + n_out + no
        send_sems, recv_sems = refs[s0 + n_sc], refs[s0 + n_sc + 1]
        lin = pl.program_id(0)
        for dim in range(1, len(grid)):
            lin = lin * grid[dim] + pl.program_id(dim)
        for p in range(comm.nphase - 1):
            @pl.when(lin == int(round(at[p] * (steps - 1))))
            def _(p=p):
                comm.run(p, cin, cout, send_sems, recv_sems)
        body(*pf, *refs[:n_in], *refs[o0:o0 + n_out], *refs[s0:s0 + n_sc])

        @pl.when(lin == steps - 1)
        def _():
            comm.run(comm.nphase - 1, cin, cout, send_sems, recv_sems)

    gs = pltpu.PrefetchScalarGridSpec(
        num_scalar_prefetch=npf, grid=grid, in_specs=list(in_specs) + [ANY] * na,
        out_specs=list(out_specs) + [ANY] * no, scratch_shapes=list(scratch_shapes) + comm.sems())
    res = pl.pallas_call(full, name=name, out_shape=list(out_shape) + comm.outs, grid_spec=gs,
                         compiler_params=_params(("arbitrary",) * len(grid)))(*prefetch, *args, *comm.args)
    return list(res[:n_out]), list(res[n_out:])


def _chunk_tile(cs, cap):
    return cs if cs <= CHUNK_WHOLE_MAX else _tile(cs, cap)


def _matmul(pairs, mode, out_dtype, name, out_chunks=None, comm=None, at=None):
    a0, b0 = pairs[0]
    chunked = b0.ndim == 3
    cs = b0.shape[2] if chunked else None
    bcols = b0.shape[0] * b0.shape[2] if chunked else b0.shape[1]
    brows = b0.shape[1] if chunked else b0.shape[0]
    if mode == "nn":
        (m, k), n = a0.shape, bcols
    elif mode == "nt":
        (m, k), n = a0.shape, brows
    else:
        (k, m), n = a0.shape, bcols
    tm = _tile(m, MM_TM)
    if mode == "nt":
        tn = _tile(n, MM_TN)
        tk = _chunk_tile(cs, MM_TK) if chunked else _tile(k, MM_TK)
    else:
        tk = _tile(k, MM_TK)
        if chunked:
            tn = _chunk_tile(cs, MM_TN)
        elif out_chunks:
            tn = _chunk_tile(n // out_chunks, MM_TN)
        else:
            tn = _tile(n, MM_TN)
    nk = k // tk
    if mode == "nn":
        a_spec = pl.BlockSpec((tm, tk), lambda i, j, kk: (i, kk))
        if chunked:
            q = cs // tn
            b_spec = pl.BlockSpec((None, tk, tn), lambda i, j, kk: (j // q, kk, j % q))
        else:
            b_spec = pl.BlockSpec((tk, tn), lambda i, j, kk: (kk, j))
        dims = NN
    elif mode == "nt":
        a_spec = pl.BlockSpec((tm, tk), lambda i, j, kk: (i, kk))
        if chunked:
            q = cs // tk
            b_spec = pl.BlockSpec((None, tn, tk), lambda i, j, kk: (kk // q, j, kk % q))
        else:
            b_spec = pl.BlockSpec((tn, tk), lambda i, j, kk: (j, kk))
        dims = NT
    else:
        a_spec = pl.BlockSpec((tk, tm), lambda i, j, kk: (kk, i))
        b_spec = pl.BlockSpec((tk, tn), lambda i, j, kk: (kk, j))
        dims = TN
    if out_chunks:
        qo = (n // out_chunks) // tn
        out_shape = jax.ShapeDtypeStruct((out_chunks, m, n // out_chunks), out_dtype)
        o_spec = pl.BlockSpec((None, tm, tn), lambda i, j, kk: (j // qo, i, j % qo))
    else:
        out_shape = jax.ShapeDtypeStruct((m, n), out_dtype)
        o_spec = pl.BlockSpec((tm, tn), lambda i, j, kk: (i, j))
    npair = len(pairs)

    def body(*refs):
        o_ref, acc = refs[2 * npair], refs[2 * npair + 1]
        kk = pl.program_id(2)

        @pl.when(kk == 0)
        def _():
            acc[...] = jnp.zeros_like(acc)

        part = _dot(refs[0][...], refs[1][...], dims)
        for p in range(1, npair):
            part = part + _dot(refs[2 * p][...], refs[2 * p + 1][...], dims)
        acc[...] += part

        @pl.when(kk == nk - 1)
        def _():
            o_ref[...] = acc[...].astype(out_dtype)

    args = [t for pr in pairs for t in pr]
    res, cres = _call(body, name, [out_shape], (m // tm, n // tn, nk), [a_spec, b_spec] * npair, [o_spec],
                      [pltpu.VMEM((tm, tn), f32)], ("parallel", "parallel", "arbitrary"), args, comm, at)
    return res[0] if comm is None else (res[0], cres)


def _sigmoid(x):
    return 1.0 / (1.0 + jnp.exp(-x))


def _ffn_up(v, wg, wu):
    m, k = v.shape
    nchunk, _, cs = wg.shape
    n = nchunk * cs
    tm, tn = _tile(m, 512), _chunk_tile(cs, 512)
    q = cs // tn
    w_spec = pl.BlockSpec((None, k, tn), lambda j, i: (j // q, 0, j % q))

    def body(v_ref, wg_ref, wu_ref, g_ref, u_ref, act_ref):
        vb = v_ref[...]
        g = _dot(vb, wg_ref[...], NN)
        u = _dot(vb, wu_ref[...], NN)
        sg = _sigmoid(g)
        silu = g * sg
        g_ref[...] = silu.astype(bf16)
        u_ref[...] = (u * (sg * (1.0 + g * (1.0 - sg)))).astype(bf16)
        act_ref[...] = (silu * u).astype(bf16)

    out = jax.ShapeDtypeStruct((m, n), bf16)
    o_spec = pl.BlockSpec((tm, tn), lambda j, i: (i, j))
    return pl.pallas_call(
        body, name="ffn_up", out_shape=(out, out, out), grid=(n // tn, m // tm),
        in_specs=[pl.BlockSpec((tm, k), lambda j, i: (i, 0)), w_spec, w_spec],
        out_specs=(o_spec, o_spec, o_spec),
        compiler_params=_params(("parallel", "parallel")),
    )(v, wg, wu)


def _ffn_down_bwd(dffn, wd, dact_du, dact_dg):
    m, k = dffn.shape
    n = wd.shape[0]
    tm, tn = _tile(m, 1024), _tile(n, 512)

    def body(d_ref, w_ref, fu_ref, fg_ref, dg_ref, du_ref):
        dact = _dot(d_ref[...], w_ref[...], NT)
        du_ref[...] = (dact * fu_ref[...].astype(f32)).astype(bf16)
        dg_ref[...] = (dact * fg_ref[...].astype(f32)).astype(bf16)

    out = jax.ShapeDtypeStruct((m, n), bf16)
    o_spec = pl.BlockSpec((tm, tn), lambda i, j: (i, j))
    return pl.pallas_call(
        body, name="ffn_down_bwd", out_shape=(out, out), grid=(m // tm, n // tn),
        in_specs=[pl.BlockSpec((tm, k), lambda i, j: (i, 0)), pl.BlockSpec((tn, k), lambda i, j: (j, 0)),
                  o_spec, o_spec],
        out_specs=(o_spec, o_spec),
        compiler_params=_params(("parallel", "parallel")),
    )(dffn, wd, dact_du, dact_dg)


def _blocked(fn, grid, ins, outs, accs, name, sp=None, comm=None, at=None):
    n_in, n_out, n_acc = len(ins), len(outs), len(accs)
    nsp = 0 if sp is None else 1

    def body(*refs):
        refs = refs[nsp:]
        tiles = [r[...] for r in refs[:n_in]]
        res = fn(*tiles)
        if not isinstance(res, (tuple, list)):
            res = (res,)
        for r, val in zip(refs[n_in:n_in + n_out], res[:n_out]):
            r[...] = val.astype(r.dtype)
        if n_acc:
            first = pl.program_id(0) == 0
            for d in range(1, len(grid)):
                first = jnp.logical_and(first, pl.program_id(d) == 0)

            @pl.when(first)
            def _():
                for r in refs[n_in + n_out:]:
                    r[...] = jnp.zeros_like(r)

            for r, val in zip(refs[n_in + n_out:], res[n_out:]):
                r[...] += val

    def acc_map(shape):
        zeros = (0,) * len(shape)
        return lambda *idx: zeros

    in_specs = [pl.BlockSpec(bs, im) for _, bs, im in ins]
    out_specs = [pl.BlockSpec(bs, im) for _, _, bs, im in outs] + [pl.BlockSpec(s, acc_map(s)) for s in accs]
    out_shape = [jax.ShapeDtypeStruct(s, d) for s, d, _, _ in outs] + [jax.ShapeDtypeStruct(s, f32) for s in accs]
    sem = ("arbitrary",) * len(grid) if n_acc else ("parallel",) * len(grid)
    args = [a for a, _, _ in ins]
    res, cres = _call(body, name, out_shape, grid, in_specs, out_specs, [], sem, args, comm, at,
                      prefetch=() if sp is None else (sp,))
    return res if comm is None else (res, cres)


def _rows(a, tm, cols=None, cb=0):
    w = a.shape[1] if cols is None else cols
    return (a, (tm, w), lambda i: (i, cb))


def _par(a):
    zeros = (0,) * a.ndim
    return (a, a.shape, lambda i: zeros)


def _rout(t, w, dtype, tm):
    return ((t, w), dtype, (tm, w), lambda i: (i, 0))


def _rowwise(fn, t, tm, ins, outs, accs, name, comm=None, at=None):
    return _blocked(fn, (t // tm,), ins, outs, accs, name, comm=comm, at=at)


def _rms(x, w):
    r = lax.rsqrt(jnp.mean(x * x, axis=-1, keepdims=True) + EPS)
    return x * r * w


def _rms_bwd(x, w, dy):
    r = lax.rsqrt(jnp.mean(x * x, axis=-1, keepdims=True) + EPS)
    xh = x * r
    dyw = dy * w
    dx = r * (dyw - xh * jnp.mean(dyw * xh, axis=-1, keepdims=True))
    return dx, jnp.sum(dy * xh, axis=0, keepdims=True)


def _silu_grad(x):
    s = _sigmoid(x)
    return s * (1.0 + x * (1.0 - s))


def _rope(blk, cosp, sina, sinb):
    return blk * cosp + pltpu.roll(blk, 96, 1) * sina + pltpu.roll(blk, 32, 1) * sinb


def _rope_bwd(dy, cosp, sina, sinb):
    return dy * cosp + pltpu.roll(dy * sina, 32, 1) + pltpu.roll(dy * sinb, 96, 1)


HALO = 8


def _conv_taps(buf, w, tm, base):
    acc = buf[base:base + tm, :] * w[0:1]
    for k in range(1, SSD_K):
        acc = acc + buf[base + k:base + k + tm, :] * w[k:k + 1]
    return acc


def _conv_specs(t, tm, cdim):
    cur = pl.BlockSpec((tm, cdim), lambda i: (i, 0))
    prev = pl.BlockSpec((HALO, cdim), lambda i: (jnp.maximum(i * (tm // HALO) - 1, 0), 0))
    nxt = pl.BlockSpec((HALO, cdim), lambda i: (jnp.minimum((i + 1) * (tm // HALO), t // HALO - 1), 0))
    return cur, prev, nxt


def _conv_fwd(src, w, b, cdim, tm):
    t = src.shape[0]
    cur, prev, _ = _conv_specs(t, tm, cdim)

    def body(x_ref, p_ref, w_ref, b_ref, o_ref, buf):
        buf[0:HALO, :] = jnp.where(pl.program_id(0) > 0, p_ref[...], 0.0)
        buf[HALO:HALO + tm, :] = x_ref[...]
        pre = _conv_taps(buf, w_ref[...], tm, HALO - (SSD_K - 1)) + b_ref[...]
        o_ref[...] = pre * _sigmoid(pre)

    par = lambda a: pl.BlockSpec(a.shape, lambda i: (0, 0))
    return pl.pallas_call(
        body, name="conv_silu", out_shape=jax.ShapeDtypeStruct((t, cdim), f32), grid=(t // tm,),
        in_specs=[cur, prev, par(w), par(b)], out_specs=cur,
        scratch_shapes=[pltpu.VMEM((tm + HALO, cdim), f32)], compiler_params=_params(("parallel",)),
    )(src, src, w, b)


def _conv_bwd_pre(src, w, b, dact, cdim, tm):
    t = src.shape[0]
    cur, prev, _ = _conv_specs(t, tm, cdim)

    def body(x_ref, p_ref, w_ref, b_ref, d_ref, dpre_ref, dw0, dw1, dw2, dw3, db, buf):
        @pl.when(pl.program_id(0) == 0)
        def _():
            for r in (dw0, dw1, dw2, dw3, db):
                r[...] = jnp.zeros_like(r)

        buf[0:HALO, :] = jnp.where(pl.program_id(0) > 0, p_ref[...], 0.0)
        buf[HALO:HALO + tm, :] = x_ref[...]
        base = HALO - (SSD_K - 1)
        pre = _conv_taps(buf, w_ref[...], tm, base) + b_ref[...]
        dpre = d_ref[...] * _silu_grad(pre)
        dpre_ref[...] = dpre
        for k, r in enumerate((dw0, dw1, dw2, dw3)):
            r[...] += jnp.sum(dpre * buf[base + k:base + k + tm, :], axis=0, keepdims=True)
        db[...] += jnp.sum(dpre, axis=0, keepdims=True)

    par = lambda a: pl.BlockSpec(a.shape, lambda i: (0, 0))
    acc = pl.BlockSpec((1, cdim), lambda i: (0, 0))
    acc_shape = jax.ShapeDtypeStruct((1, cdim), f32)
    return pl.pallas_call(
        body, name="conv_silu_bwd", out_shape=[jax.ShapeDtypeStruct((t, cdim), f32)] + [acc_shape] * 5,
        grid=(t // tm,), in_specs=[cur, prev, par(w), par(b), cur], out_specs=[cur] + [acc] * 5,
        scratch_shapes=[pltpu.VMEM((tm + HALO, cdim), f32)], compiler_params=_params(("arbitrary",)),
    )(src, src, w, b, dact)


def _conv_bwd_dx(dpre, w, tm):
    t, cdim = dpre.shape
    cur, _, nxt = _conv_specs(t, tm, cdim)
    last = t // tm - 1

    def body(d_ref, n_ref, w_ref, o_ref, buf):
        buf[0:tm, :] = d_ref[...]
        buf[tm:tm + HALO, :] = jnp.where(pl.program_id(0) < last, n_ref[...], 0.0)
        wv = w_ref[...]
        acc = buf[0:tm, :] * wv[SSD_K - 1:SSD_K]
        for k in range(SSD_K - 1):
            s = SSD_K - 1 - k
            acc = acc + buf[s:s + tm, :] * wv[k:k + 1]
        o_ref[...] = acc.astype(bf16)

    return pl.pallas_call(
        body, name="conv_bwd_dx", out_shape=jax.ShapeDtypeStruct((t, cdim), bf16), grid=(t // tm,),
        in_specs=[cur, nxt, pl.BlockSpec(w.shape, lambda i: (0, 0))], out_specs=cur,
        scratch_shapes=[pltpu.VMEM((tm + HALO, cdim), f32)], compiler_params=_params(("parallel",)),
    )(dpre, dpre, w)


def _causal_mask(s, row0=0):
    row = lax.broadcasted_iota(jnp.int32, s.shape, 0) + row0
    col = lax.broadcasted_iota(jnp.int32, s.shape, 1)
    return jnp.where(row >= col, s, -jnp.inf)


def _causal_pairs(nq, q_major):
    pairs = [(qi, ki) for qi in range(nq) for ki in range(qi + 1)]
    if not q_major:
        pairs.sort(key=lambda p: (p[1], p[0]))
    return (jnp.asarray([p[0] for p in pairs], jnp.int32), jnp.asarray([p[1] for p in pairs], jnp.int32))


def _flash_fwd(q, kv, kr, nheads, scale, comm=None, at=None):
    t = q.shape[0]
    tq = _tile(t, ATT_TILE_FWD)
    nq = t // tq

    qtab, ktab = _causal_pairs(nq, q_major=True)
    hp = 2 if nheads % 2 == 0 else 1

    def body(qt, kt, q_ref, kv_ref, kr_ref, o_ref, lse_ref, m_sc, l_sc, acc_sc):
        qi, ki = qt[pl.program_id(1)], kt[pl.program_id(1)]

        @pl.when(ki == 0)
        def _():
            m_sc[...] = jnp.full_like(m_sc, -jnp.inf)
            l_sc[...] = jnp.zeros_like(l_sc)
            acc_sc[...] = jnp.zeros_like(acc_sc)

        def step(masked, last):
            krb = kr_ref[...]
            ss = []
            for j in range(hp):
                k = jnp.concatenate([kv_ref[:, j * QK_PAD: j * QK_PAD + NOPE], krb], axis=1)
                ss.append(lax.dot_general(q_ref[:, j * QK_PAD:(j + 1) * QK_PAD], k, NT, preferred_element_type=f32))
            soft = []
            for j in range(hp):
                s = ss[j]
                if masked:
                    s = _causal_mask(s)
                m_old = m_sc[j]
                m_new = jnp.maximum(m_old, jnp.max(s, axis=1, keepdims=True))
                alpha = jnp.exp(m_old - m_new)
                p = jnp.exp(s - m_new)
                soft.append((m_new, alpha, alpha * l_sc[j] + jnp.sum(p, axis=1, keepdims=True), p.astype(bf16)))
            for j in range(hp):
                m_new, alpha, l, p = soft[j]
                v = kv_ref[:, j * QK_PAD + NOPE:(j + 1) * QK_PAD]
                acc = alpha * acc_sc[j] + lax.dot_general(p, v, NN, preferred_element_type=f32)
                if last:
                    o_ref[:, j * V_HEAD:(j + 1) * V_HEAD] = acc / l
                    lse_ref[:, j * V_HEAD:(j + 1) * V_HEAD] = jnp.broadcast_to(m_new + jnp.log(l), (tq, V_HEAD))
                else:
                    l_sc[j] = l
                    acc_sc[j] = acc
                    m_sc[j] = m_new

        @pl.when(ki < qi)
        def _():
            step(False, False)

        @pl.when(ki == qi)
        def _():
            step(True, True)

    o_spec = pl.BlockSpec((tq, hp * V_HEAD), lambda h, s, qt, kt: (qt[s], h))
    out = jax.ShapeDtypeStruct((t, nheads * V_HEAD), f32)
    (o, lse), cres = _call(
        body, "flash_fwd", [out, out], (nheads // hp, len(qtab)),
        [pl.BlockSpec((tq, hp * QK_PAD), lambda h, s, qt, kt: (qt[s], h)),
         pl.BlockSpec((tq, hp * QK_PAD), lambda h, s, qt, kt: (kt[s], h)),
         pl.BlockSpec((tq, LANES), lambda h, s, qt, kt: (kt[s], 0))],
        [o_spec, o_spec],
        [pltpu.VMEM((hp, tq, 1), f32), pltpu.VMEM((hp, tq, 1), f32), pltpu.VMEM((hp, tq, V_HEAD), f32)],
        ("parallel", "arbitrary"), [q, kv, kr], comm, at, prefetch=(qtab, ktab))
    return o, lse, cres


def _flash_bwd(q, kv, kr, do, lse, delta, nheads, scale, comm=None, at=None):
    t = q.shape[0]
    tq = _tile(t, ATT_TILE)
    nq = t // tq
    qtab, ktab = _causal_pairs(nq, q_major=False)
    nsub = 2 if tq % 32 == 0 else 1

    def body(qt, kt, q_ref, kn_ref, kr_ref, v_ref, do_ref, lse_ref, dl_ref, dkv_ref, dkr_ref, dq_ref, dk_sc, dv_sc):
        qi, ki = qt[pl.program_id(1)], kt[pl.program_id(1)]

        @pl.when(pl.program_id(1) == 0)
        def _():
            dq_ref[...] = jnp.zeros_like(dq_ref)

        @pl.when(qi == ki)
        def _():
            dk_sc[...] = jnp.zeros_like(dk_sc)
            dv_sc[...] = jnp.zeros_like(dv_sc)

        def step(masked):
            k = jnp.concatenate([kn_ref[...], kr_ref[...]], axis=1)
            vb = v_ref[...]
            parts = [slice(i * (tq // nsub), (i + 1) * (tq // nsub)) for i in range(nsub)]
            qs = [q_ref[r, :] for r in parts]
            dos = [do_ref[r, :] for r in parts]
            ss = [lax.dot_general(qb, k, NT, preferred_element_type=f32) for qb in qs]
            dps = [lax.dot_general(dob, vb, NT, preferred_element_type=f32) for dob in dos]
            ps, dss = [], []
            for r, s, dp in zip(parts, ss, dps):
                if masked:
                    s = _causal_mask(s, r.start)
                p = jnp.exp(s - lse_ref[r, 0:1])
                ps.append(p.astype(bf16))
                dss.append((p * (dp - dl_ref[r, 0:1])).astype(bf16))
            for r, qb, dob, p, ds in zip(parts, qs, dos, ps, dss):
                dv_sc[...] += lax.dot_general(p, dob, TN, preferred_element_type=f32)
                dk_sc[...] += lax.dot_general(ds, qb, TN, preferred_element_type=f32)
                rows = pl.ds(pl.multiple_of(qi * tq + r.start, tq // nsub), tq // nsub)
                dq_ref[rows, :] += lax.dot_general(ds, k, NN, preferred_element_type=f32)

        @pl.when(qi > ki)
        def _():
            step(False)

        @pl.when(qi == ki)
        def _():
            step(True)

        @pl.when(qi == nq - 1)
        def _():
            dk = dk_sc[...]
            dkv_ref[...] = jnp.concatenate([dk[:, :NOPE], dv_sc[...]], axis=1).astype(bf16)
            dkr_ref[...] = dk[:, NOPE:]

    hspec = pl.BlockSpec((tq, V_HEAD), lambda h, s, qt, kt: (qt[s], h))
    (dkv, dkr, dq), cres = _call(
        body, "flash_bwd",
        [jax.ShapeDtypeStruct((t, nheads * QK_PAD), bf16), jax.ShapeDtypeStruct((t, nheads * LANES), f32),
         jax.ShapeDtypeStruct((t, nheads * QK_PAD), f32)],
        (nheads, len(qtab)),
        [pl.BlockSpec((tq, QK_PAD), lambda h, s, qt, kt: (qt[s], h)),
         pl.BlockSpec((tq, NOPE), lambda h, s, qt, kt: (kt[s], 2 * h)),
         pl.BlockSpec((tq, LANES), lambda h, s, qt, kt: (kt[s], 0)),
         pl.BlockSpec((tq, V_HEAD), lambda h, s, qt, kt: (kt[s], 2 * h + 1)),
         hspec, hspec, hspec],
        [pl.BlockSpec((tq, QK_PAD), lambda h, s, qt, kt: (kt[s], h)),
         pl.BlockSpec((tq, LANES), lambda h, s, qt, kt: (kt[s], h)),
         pl.BlockSpec((t, QK_PAD), lambda h, s, qt, kt: (0, h))],
        [pltpu.VMEM((tq, QK_PAD), f32), pltpu.VMEM((tq, V_HEAD), f32)],
        ("parallel", "arbitrary"), [q, kv, kr, kv, do, lse, delta], comm, at, prefetch=(qtab, ktab))
    return dkv, dkr, dq, cres


def _split3(a):
    hi = a.astype(bf16)
    r1 = a - hi.astype(f32)
    mid = r1.astype(bf16)
    lo = (r1 - mid.astype(f32)).astype(bf16)
    return hi, mid, lo


def _split2(a):
    hi = a.astype(bf16)
    return hi, (a - hi.astype(f32)).astype(bf16)


def _ones_dot_left(tri, a):
    hi, mid, lo = _split3(a)
    d = lambda v: lax.dot_general(tri, v, NN, preferred_element_type=f32)
    return d(hi) + d(mid) + d(lo)


def _ones_dot_right(a, tri):
    hi, mid, lo = _split3(a)
    d = lambda v: lax.dot_general(v, tri, NN, preferred_element_type=f32)
    return d(hi) + d(mid) + d(lo)


def _softplus(x):
    return jnp.maximum(x, 0.0) + jnp.log(1.0 + jnp.exp(-jnp.abs(x)))


def _head_spread(hs, width):
    shift = SSD_P.bit_length() - 1
    return (lax.broadcasted_iota(jnp.int32, (hs, width), 0)
            == lax.shift_right_logical(lax.broadcasted_iota(jnp.int32, (hs, width), 1), shift)).astype(bf16)


def _ssd_common(dt_ref, dtT_ref, dtb_ref, dtbT_ref, alog_ref, alogT_ref):
    ii = lax.broadcasted_iota(jnp.int32, (CHUNK, CHUNK), 0)
    jj = lax.broadcasted_iota(jnp.int32, (CHUNK, CHUNK), 1)
    tri = ii >= jj
    raw = dt_ref[...] + dtb_ref[...]
    dt = _softplus(raw)
    a_neg = -jnp.exp(alog_ref[...])
    cum = _ones_dot_left(tri.astype(bf16), dt * a_neg)
    dt_t = _softplus(dtT_ref[...] + dtbT_ref[...])
    cum_t = _ones_dot_right(dt_t * (-jnp.exp(alogT_ref[...])), (ii <= jj).astype(bf16))
    return tri, raw, dt, a_neg, cum, cum_t


def _ssd_fwd(xbc, dt, dt_t, dtb, dtb_t, alog, alog_t, dskip, width, comm=None, at=None):
    t, cdim = xbc.shape
    hs = dt.shape[1]
    nc = t // CHUNK
    epg = hs // SSD_G
    gn = SSD_G * SSD_N

    def body(x_ref, dt_ref, dtT_ref, dtb_ref, dtbT_ref, alog_ref, alogT_ref, d_ref, y_ref, hp_ref, h_sc, yd_sc):
        @pl.when(pl.program_id(0) == 0)
        def _():
            h_sc[...] = jnp.zeros_like(h_sc)

        tri, _, dtv, _, cum, cum_t = _ssd_common(dt_ref, dtT_ref, dtb_ref, dtbT_ref, alog_ref, alogT_ref)
        spread = _head_spread(hs, width)
        clast = cum[CHUNK - 1:CHUNK, :]
        dec = jnp.exp(clast)
        wide = _ones_dot_right(jnp.concatenate([dtv, jnp.exp(cum), jnp.exp(clast - cum),
                                                jnp.broadcast_to(d_ref[...], (CHUNK, hs))], axis=0), spread)
        dt_x, ee_x, ff_x, dsk_x = (wide[i * CHUNK:(i + 1) * CHUNK] for i in range(4))
        xs = x_ref[:, :width]
        xdt = xs * dt_x
        xf = xdt * ff_x
        h_all = h_sc[...]
        hp_ref[0] = h_all
        h_all = h_all.reshape(hs * SSD_P, SSD_N)
        ch_parts = []
        for g in range(SSD_G):
            gsl = slice(g * epg * SSD_P, (g + 1) * epg * SSD_P)
            bb = x_ref[:, width + g * SSD_N: width + (g + 1) * SSD_N].astype(bf16)
            cb_ = x_ref[:, width + gn + g * SSD_N: width + gn + (g + 1) * SSD_N].astype(bf16)
            cbm = lax.dot_general(cb_, bb, NT, preferred_element_type=f32)
            ch_parts.append(_dot(cb_, h_all[gsl], NT))
            st = _dot(xf[:, gsl], bb, TN)
            for e in range(g * epg, (g + 1) * epg):
                esl = slice(e * SSD_P, (e + 1) * SSD_P)
                lmat = jnp.exp(jnp.where(tri, cum[:, e:e + 1] - cum_t[e:e + 1, :], -jnp.inf))
                yd_sc[:, esl] = _dot(cbm * lmat, xdt[:, esl], NN)
                j = e - g * epg
                h_sc[e] = h_sc[e] * dec[:, e:e + 1] + st[j * SSD_P:(j + 1) * SSD_P, :]
        y_ref[...] = yd_sc[...] + jnp.concatenate(ch_parts, axis=1) * ee_x + xs * dsk_x

    par = lambda a: pl.BlockSpec(a.shape, lambda i: (0,) * a.ndim)
    (y, hp), cres = _call(
        body, "ssd_fwd",
        [jax.ShapeDtypeStruct((t, width), f32), jax.ShapeDtypeStruct((nc, hs, SSD_P, SSD_N), f32)], (nc,),
        [pl.BlockSpec((CHUNK, cdim), lambda i: (i, 0)), pl.BlockSpec((CHUNK, hs), lambda i: (i, 0)),
         pl.BlockSpec((hs, CHUNK), lambda i: (0, i)), par(dtb), par(dtb_t), par(alog), par(alog_t), par(dskip)],
        [pl.BlockSpec((CHUNK, width), lambda i: (i, 0)), pl.BlockSpec((1, hs, SSD_P, SSD_N), lambda i: (i, 0, 0, 0))],
        [pltpu.VMEM((hs, SSD_P, SSD_N), f32), pltpu.VMEM((CHUNK, width), f32)], ("arbitrary",),
        [xbc, dt, dt_t, dtb, dtb_t, alog, alog_t, dskip], comm, at)
    return y, hp, cres


def _ssd_bwd(xbc, dt, dt_t, dtb, dtb_t, alog, alog_t, dskip, hprev, dy, width, comm=None, at=None):
    t, cdim = xbc.shape
    hs = dt.shape[1]
    nc = t // CHUNK
    epg = hs // SSD_G
    gn = SSD_G * SSD_N

    def body(x_ref, dt_ref, dtT_ref, dtb_ref, dtbT_ref, alog_ref, alogT_ref, d_ref, hp_ref, dy_ref,
             dx_ref, ddt_ref, dalog_ref, ddsk_ref, ddtb_ref, dh_sc, dxd_sc):
        @pl.when(pl.program_id(0) == 0)
        def _():
            dh_sc[...] = jnp.zeros_like(dh_sc)
            dalog_ref[...] = jnp.zeros_like(dalog_ref)
            ddsk_ref[...] = jnp.zeros_like(ddsk_ref)
            ddtb_ref[...] = jnp.zeros_like(ddtb_ref)

        tri, raw, dtv, a_neg, cum, cum_t = _ssd_common(dt_ref, dtT_ref, dtb_ref, dtbT_ref, alog_ref, alogT_ref)
        dsk = d_ref[...]
        head_row = lax.broadcasted_iota(jnp.int32, (1, hs), 1)
        head_col = lax.broadcasted_iota(jnp.int32, (hs, 1), 0)
        last_row = (lax.broadcasted_iota(jnp.int32, (CHUNK, 1), 0) == CHUNK - 1).astype(f32)
        shift = SSD_P.bit_length() - 1
        spread = _head_spread(hs, width)
        gather = (lax.shift_right_logical(lax.broadcasted_iota(jnp.int32, (width, hs), 0), shift)
                  == lax.broadcasted_iota(jnp.int32, (width, hs), 1)).astype(bf16)
        clast = cum[CHUNK - 1:CHUNK, :]
        ee = jnp.exp(cum)
        ff = jnp.exp(clast - cum)
        dec = jnp.exp(clast)
        wide = _ones_dot_right(jnp.concatenate([dtv, ee, ff, jnp.broadcast_to(dsk, (CHUNK, hs))], axis=0), spread)
        dt_x, ee_x, ff_x, dsk_x = (wide[i * CHUNK:(i + 1) * CHUNK] for i in range(4))
        xs = x_ref[:, :width]
        dyv = dy_ref[...]
        xdt = xs * dt_x
        dye = dyv * ee_x
        xf = xdt * ff_x
        h_all = hp_ref[0].reshape(hs * SSD_P, SSD_N)
        dh_all = dh_sc[...].reshape(hs * SSD_P, SSD_N)
        hi, lo = _split2(_ones_dot_left(spread, dh_all * h_all))
        ones8 = jnp.ones((8, SSD_N), bf16)
        hh = (lax.dot_general(ones8, hi, NT, preferred_element_type=f32)
              + lax.dot_general(ones8, lo, NT, preferred_element_type=f32))[0:1]
        rowsum_m = jnp.zeros((CHUNK, hs), f32)
        colsum_m = jnp.zeros((hs, CHUNK), f32)
        ch_parts, bds_parts = [], []
        for g in range(SSD_G):
            bsl = slice(width + g * SSD_N, width + (g + 1) * SSD_N)
            csl = slice(width + gn + g * SSD_N, width + gn + (g + 1) * SSD_N)
            gsl = slice(g * epg * SSD_P, (g + 1) * epg * SSD_P)
            bb = x_ref[:, bsl].astype(bf16)
            cb_ = x_ref[:, csl].astype(bf16)
            cbm = lax.dot_general(cb_, bb, NT, preferred_element_type=f32)
            hg = h_all[gsl].astype(bf16)
            dhg = dh_all[gsl].astype(bf16)
            ch_parts.append(_dot(cb_, hg, NT))
            bds_parts.append(_dot(bb, dhg, NT))
            dcg = _dot(dye[:, gsl], hg, NN)
            dbg = _dot(xf[:, gsl], dhg, NN)
            dh_new = _dot(dye[:, gsl], cb_, TN)
            dcb = jnp.zeros((CHUNK, CHUNK), f32)
            for e in range(g * epg, (g + 1) * epg):
                esl = slice(e * SSD_P, (e + 1) * SSD_P)
                lmat = jnp.exp(jnp.where(tri, cum[:, e:e + 1] - cum_t[e:e + 1, :], -jnp.inf))
                gmat = cbm * lmat
                dy_e = dyv[:, esl].astype(bf16)
                dgm = _dot(dy_e, xdt[:, esl], NT)
                dxd_sc[:, esl] = _dot(gmat, dy_e, TN)
                dcb = dcb + dgm * lmat
                mm = dgm * gmat
                rowsum_m = rowsum_m + jnp.sum(mm, axis=1, keepdims=True) * (head_row == e).astype(f32)
                colsum_m = colsum_m + (head_col == e).astype(f32) * jnp.sum(mm, axis=0, keepdims=True)
                j = e - g * epg
                dh_sc[e] = dh_new[j * SSD_P:(j + 1) * SSD_P, :] + dec[:, e:e + 1] * dh_sc[e]
            dx_ref[:, bsl] = dbg + _dot(dcb, cb_, TN)
            dx_ref[:, csl] = dcg + _dot(dcb, bb, NN)
        ch_all = jnp.concatenate(ch_parts, axis=1)
        bds_all = jnp.concatenate(bds_parts, axis=1)
        dxdt = bds_all * ff_x + dxd_sc[...]
        dx_ref[:, :width] = dxdt * dt_x + dyv * dsk_x
        sums = _ones_dot_right(jnp.concatenate([dxdt * xs, dyv * ch_all, bds_all * xdt, dyv * xs], axis=0), gather)
        ddtx_all = sums[0:CHUNK]
        dff = sums[2 * CHUNK:3 * CHUNK] * ff
        ddsk = jnp.sum(sums[3 * CHUNK:4 * CHUNK], axis=0, keepdims=True)
        dclast = jnp.sum(dff, axis=0, keepdims=True) + dec * hh
        eye = (lax.broadcasted_iota(jnp.int32, (hs, hs), 0) == lax.broadcasted_iota(jnp.int32, (hs, hs), 1)).astype(bf16)
        colsum_t = sum(lax.dot_general(v, eye, TN, preferred_element_type=f32) for v in _split3(colsum_m))
        dcum_all = sums[CHUNK:2 * CHUNK] * ee - dff + rowsum_m - colsum_t + dclast * last_row
        ii = lax.broadcasted_iota(jnp.int32, (CHUNK, CHUNK), 0)
        jj = lax.broadcasted_iota(jnp.int32, (CHUNK, CHUNK), 1)
        da = _ones_dot_left((jj >= ii).astype(bf16), dcum_all)
        ddt = da * a_neg + ddtx_all
        dalog_ref[...] += jnp.sum(da * dtv, axis=0, keepdims=True) * a_neg
        draw = ddt * _sigmoid(raw)
        place = (lax.broadcasted_iota(jnp.int32, (hs, LANES), 0) + ROPE
                 == lax.broadcasted_iota(jnp.int32, (hs, LANES), 1)).astype(bf16)
        ddt_ref[...] = _ones_dot_right(draw, place)
        ddtb_ref[...] += jnp.sum(draw, axis=0, keepdims=True)
        ddsk_ref[...] += ddsk

    rev = lambda i: nc - 1 - i
    par = lambda a: pl.BlockSpec(a.shape, lambda i: (0,) * a.ndim)
    acc = pl.BlockSpec((1, hs), lambda i: (0, 0))
    acc_shape = jax.ShapeDtypeStruct((1, hs), f32)
    res, cres = _call(
        body, "ssd_bwd",
        [jax.ShapeDtypeStruct((t, cdim), f32), jax.ShapeDtypeStruct((t, LANES), f32), acc_shape, acc_shape, acc_shape],
        (nc,),
        [pl.BlockSpec((CHUNK, cdim), lambda i: (rev(i), 0)), pl.BlockSpec((CHUNK, hs), lambda i: (rev(i), 0)),
         pl.BlockSpec((hs, CHUNK), lambda i: (0, rev(i))), par(dtb), par(dtb_t), par(alog), par(alog_t),
         par(dskip), pl.BlockSpec((1, hs, SSD_P, SSD_N), lambda i: (rev(i), 0, 0, 0)),
         pl.BlockSpec((CHUNK, width), lambda i: (rev(i), 0))],
        [pl.BlockSpec((CHUNK, cdim), lambda i: (rev(i), 0)), pl.BlockSpec((CHUNK, LANES), lambda i: (rev(i), 0)),
         acc, acc, acc],
        [pltpu.VMEM((hs, SSD_P, SSD_N), f32), pltpu.VMEM((CHUNK, width), f32)], ("arbitrary",),
        [xbc, dt, dt_t, dtb, dtb_t, alog, alog_t, dskip, hprev, dy], comm, at)
    return (*res, cres)


def _where_am_i():
    x, y, c = lax.axis_index("x"), lax.axis_index("y"), lax.axis_index("c")
    chips = [(1 - x, y), (x, 1 - y), (1 - x, 1 - y)]
    return x, y, c, chips


def _remote(src, dst, send_sems, recv_sems, k, to):
    return pltpu.make_async_remote_copy(src_ref=src, dst_ref=dst, send_sem=send_sems.at[k], recv_sem=recv_sems.at[k],
                                        device_id=to, device_id_type=MESH)


class _Comm:
    def __init__(self, args, outs, nsem, nphase, run):
        self.args, self.outs, self.nsem, self.nphase, self.run = list(args), list(outs), nsem, nphase, run

    def sems(self):
        return [pltpu.SemaphoreType.DMA((self.nsem,)), pltpu.SemaphoreType.DMA((self.nsem,))]


class _SemWindow:
    def __init__(self, ref, off):
        self.ref, self.off, self.at = ref, off, self

    def __getitem__(self, k):
        return self.ref.at[k + self.off]


def _both(a, b):
    assert a.nphase == b.nphase
    na, no = len(a.args), len(a.outs)

    def run(phase, srcs, outs, send_sems, recv_sems):
        a.run(phase, srcs[:na], outs[:no], send_sems, recv_sems)
        b.run(phase, srcs[na:], outs[no:], _SemWindow(send_sems, a.nsem), _SemWindow(recv_sems, a.nsem))

    return _Comm(a.args + b.args, a.outs + b.outs, a.nsem + b.nsem, a.nphase, run)


def _comm_call(comm, name):
    na, no = len(comm.args), len(comm.outs)

    def body(*refs):
        for phase in range(comm.nphase):
            comm.run(phase, refs[:na], refs[na:na + no], refs[na + no], refs[na + no + 1])

    return list(pl.pallas_call(body, name=name, out_shape=comm.outs, in_specs=[ANY] * na, out_specs=[ANY] * no,
                               scratch_shapes=comm.sems())(*comm.args))


def _half(ref, c, r2):
    return ref.at[pl.ds(c * r2, r2)]


def _gather_weights(shards, wholes):
    ns, nw = len(shards), len(wholes)
    per = 7

    def run(phase, srcs, outs, send_sems, recv_sems):
        x, y, c, chips = _where_am_i()
        me, sib = 2 * x + y, (x, y, 1 - c)
        def first():
            cps = []
            for w in range(ns + nw):
                src, out, base = srcs[w], outs[w], per * w
                halved = w < ns
                r2 = src.shape[0] // 2
                piece = _half(src, c, r2) if halved else src
                for j, (cx, cy) in enumerate(chips):
                    dst = _half(out.at[me], c, r2) if halved else out.at[me]
                    cps.append(_remote(piece, dst, send_sems, recv_sems, base + j, (cx, cy, c)))
                cps.append(_remote(src, out.at[me], send_sems, recv_sems, base + 6, sib))
            return cps

        def passed():
            cps = []
            for w in range(ns):
                r2 = srcs[w].shape[0] // 2
                for j, (cx, cy) in enumerate(chips):
                    got = _half(outs[w].at[2 * cx + cy], c, r2)
                    cps.append(_remote(got, got, send_sems, recv_sems, per * w + 3 + j, sib))
            return cps

        if phase == 0:
            for cp in first():
                cp.start()
        elif phase == 1:
            it = iter(passed())
            for w in range(ns + nw):
                src, out, base = srcs[w], outs[w], per * w
                r2 = src.shape[0] // 2
                for j, (cx, cy) in enumerate(chips):
                    got = _half(out.at[2 * cx + cy], c, r2) if w < ns else out.at[2 * cx + cy]
                    _remote(got, got, send_sems, recv_sems, base + j, sib).wait_recv()
                    if w < ns:
                        next(it).start()
        else:
            for w in range(ns + nw):
                src, out, base = srcs[w], outs[w], per * w
                r2 = src.shape[0] // 2
                if w < ns:
                    for j, (cx, cy) in enumerate(chips):
                        got = _half(out.at[2 * cx + cy], 1 - c, r2)
                        _remote(got, got, send_sems, recv_sems, base + 3 + j, sib).wait_recv()
                _remote(src, out.at[me], send_sems, recv_sems, base + 6, sib).wait_recv()
            for cp in first() + passed():
                cp.wait_send()

    args = list(shards) + list(wholes)
    outs = [jax.ShapeDtypeStruct((4,) + a.shape, a.dtype) for a in args]
    return _Comm(args, outs, per * len(args), 3, run)


def _sibling_send_halves(gs):
    n = len(gs)

    def run(phase, srcs, outs, send_sems, recv_sems):
        x, y, c, _ = _where_am_i()
        sib = (x, y, 1 - c)
        cps = []
        for w in range(n):
            r2 = srcs[w].shape[1] // 2
            for k in range(4):
                cps.append(_remote(_half(srcs[w].at[k], 1 - c, r2), outs[w].at[k], send_sems, recv_sems, 4 * w + k, sib))
        for cp in cps:
            cp.start() if phase == 0 else cp.wait()

    outs = [jax.ShapeDtypeStruct((4, g.shape[1] // 2, g.shape[2]), g.dtype) for g in gs]
    return _Comm(gs, outs, 4 * n, 2, run)


def _chips_exchange(ps):
    n = len(ps)

    def run(phase, srcs, outs, send_sems, recv_sems):
        x, y, c, chips = _where_am_i()
        me = 2 * x + y
        cps = []
        for w in range(n):
            for j, (cx, cy) in enumerate(chips):
                cps.append(_remote(srcs[w].at[2 * cx + cy], outs[w].at[me], send_sems, recv_sems, 3 * w + j, (cx, cy, c)))
        if phase == 0:
            for cp in cps:
                cp.start()
        else:
            for w in range(n):
                for j, (cx, cy) in enumerate(chips):
                    got = outs[w].at[2 * cx + cy]
                    _remote(got, got, send_sems, recv_sems, 3 * w + j, (cx, cy, c)).wait_recv()
            for cp in cps:
                cp.wait_send()

    outs = [jax.ShapeDtypeStruct(p.shape, p.dtype) for p in ps]
    return _Comm(ps, outs, 3 * n, 2, run)


def _sibling_swap(rs):
    n = len(rs)

    def run(phase, srcs, outs, send_sems, recv_sems):
        x, y, c, _ = _where_am_i()
        for w in range(n):
            cp = _remote(srcs[w], outs[w], send_sems, recv_sems, w, (x, y, 1 - c))
            cp.start() if phase == 0 else cp.wait()

    outs = [jax.ShapeDtypeStruct(r.shape, r.dtype) for r in rs]
    return _Comm(rs, outs, n, 2, run)


def _gather_all(v):
    rows = v.shape[0]

    def body(x_ref, out_ref, send_sems, recv_sems, local_sem):
        x, y, c, chips = _where_am_i()
        me, sib = (x, y, c), (x, y, 1 - c)
        blk = lambda px, py, pc: out_ref.at[4 * px + 2 * py + pc]
        mine = pltpu.make_async_copy(x_ref, blk(*me), local_sem)
        mine.start()
        first = [_remote(x_ref, blk(*me), send_sems, recv_sems, 0, sib)]
        first += [_remote(x_ref, blk(*me), send_sems, recv_sems, 1 + j, (*chip, c)) for j, chip in enumerate(chips)]
        for cp in first:
            cp.start()
        passed = [_remote(blk(*chip, c), blk(*chip, c), send_sems, recv_sems, 4 + j, sib) for j, chip in enumerate(chips)]
        for j, chip in enumerate(chips):
            _remote(blk(*chip, c), blk(*chip, c), send_sems, recv_sems, 1 + j, me).wait_recv()
            passed[j].start()
        _remote(blk(*sib), blk(*sib), send_sems, recv_sems, 0, me).wait_recv()
        for j, chip in enumerate(chips):
            _remote(blk(*chip, 1 - c), blk(*chip, 1 - c), send_sems, recv_sems, 4 + j, me).wait_recv()
        for cp in first + passed:
            cp.wait_send()
        mine.wait()

    vm = pl.BlockSpec(memory_space=pltpu.VMEM)
    return pl.pallas_call(
        body, name="gather_all", out_shape=jax.ShapeDtypeStruct((8, rows, LANES), v.dtype),
        in_specs=[vm], out_specs=vm,
        scratch_shapes=[pltpu.SemaphoreType.DMA((7,)), pltpu.SemaphoreType.DMA((7,)), pltpu.SemaphoreType.DMA],
    )(v)


def _flat_pad(parts, total):
    v = jnp.concatenate([p.reshape(-1) for p in parts])
    return jnp.pad(v, (0, total - v.shape[0]))


def _adamw(w, g, m, v):
    m = ADAM_B1 * m + (1.0 - ADAM_B1) * g
    v = ADAM_B2 * v + (1.0 - ADAM_B2) * jnp.square(g)
    m_hat = m / (1.0 - ADAM_B1 ** ADAM_STEP)
    v_hat = v / (1.0 - ADAM_B2 ** ADAM_STEP)
    delta = -ADAM_LR * (m_hat / (jnp.sqrt(v_hat) + ADAM_EPS) + ADAM_WD * w)
    return delta, m, v


def _adamw_call(w, g, m, v, name):
    r, cdim = w.shape
    tm = _tile(r, ROW_TILE, 8)
    o = _rout(r, cdim, f32, tm)
    return _rowwise(_adamw, r, tm, [_rows(w, tm), _rows(g, tm), _rows(m, tm), _rows(v, tm)], [o, o, o], [], name)


def _adamw_halves(w, m, v, g_mine, g_sib, sp, name):
    _, r, cdim = w.shape
    r2 = r // 2
    tr = _row_tile(r2, cdim, 8)
    nb = r2 // tr

    def body(sp_ref, w_ref, m_ref, v_ref, ga_ref, gb_ref, g_out, d_out, m_out, v_out):
        g = jnp.where(pl.program_id(0) == sp_ref[1], ga_ref[...], gb_ref[...])
        delta, mn, vn = _adamw(w_ref[...], g, m_ref[...], v_ref[...])
        g_out[...] = g
        d_out[...] = delta
        m_out[...] = mn
        v_out[...] = vn

    full = pl.BlockSpec((None, tr, cdim), lambda h, i, s: (0, h * nb + i, 0))
    mine = pl.BlockSpec((tr, cdim), lambda h, i, s: (jnp.where(h == s[1], i, 0), 0))
    sib = pl.BlockSpec((tr, cdim), lambda h, i, s: (jnp.where(h == s[1], 0, i), 0))
    out = jax.ShapeDtypeStruct((1, r, cdim), f32)
    gs = pltpu.PrefetchScalarGridSpec(num_scalar_prefetch=1, grid=(2, nb), in_specs=[full, full, full, mine, sib],
                                      out_specs=[full, full, full, full])
    return pl.pallas_call(body, name=name, out_shape=[out, out, out, out], grid_spec=gs,
                          compiler_params=_params(("parallel", "parallel")))(sp, w, m, v, g_mine, g_sib)


MIXER = ("w_in", "w_uq", "w_ukv", "w_out")
FFN = ("w_gate", "w_up", "w_down")
BIG = MIXER + FFN
SMALL = ("q_norm_w", "kv_norm_w", "conv_w", "conv_b", "dt_bias", "a_log", "d_skip", "ssd_norm_w", "attn_out_norm_w",
         "pre_mix_norm_w", "post_mix_norm_w", "pre_ffn_norm_w", "post_ffn_norm_w")
ORDER = ("w_in", "q_norm_w", "w_uq", "kv_norm_w", "w_ukv", "conv_w", "conv_b", "dt_bias", "a_log", "d_skip",
         "ssd_norm_w", "attn_out_norm_w", "w_out", "pre_mix_norm_w", "post_mix_norm_w", "pre_ffn_norm_w",
         "post_ffn_norm_w", "w_gate", "w_up", "w_down")


def kernel(x, positions, w_in, q_norm_w, w_uq, kv_norm_w, w_ukv, conv_w, conv_b, dt_bias, a_log, d_skip, ssd_norm_w, attn_out_norm_w, w_out, pre_mix_norm_w, post_mix_norm_w, pre_ffn_norm_w, post_ffn_norm_w, w_gate, w_up, w_down, loss_target, m_w_in, m_q_norm_w, m_w_uq, m_kv_norm_w, m_w_ukv, m_conv_w, m_conv_b, m_dt_bias, m_a_log, m_d_skip, m_ssd_norm_w, m_attn_out_norm_w, m_w_out, m_pre_mix_norm_w, m_post_mix_norm_w, m_pre_ffn_norm_w, m_post_ffn_norm_w, m_w_gate, m_w_up, m_w_down, v_w_in, v_q_norm_w, v_w_uq, v_kv_norm_w, v_w_ukv, v_conv_w, v_conv_b, v_dt_bias, v_a_log, v_d_skip, v_ssd_norm_w, v_attn_out_norm_w, v_w_out, v_pre_mix_norm_w, v_post_mix_norm_w, v_pre_ffn_norm_w, v_post_ffn_norm_w, v_w_gate, v_w_up, v_w_down):
    local = dict(locals())
    wts = {n: local[n][0] for n in ORDER}
    mom_m = {n: local["m_" + n][0] for n in ORDER}
    mom_v = {n: local["v_" + n][0] for n in ORDER}
    xs = x[0]
    tgt = loss_target[0]
    t, d = xs.shape
    nchip = 4
    my_x, my_y, my_c = lax.axis_index("x"), lax.axis_index("y"), lax.axis_index("c")
    my_chip = 2 * my_x + my_y

    mla_w = d // 2
    nh = mla_w // V_HEAD
    width = d - mla_w
    hs = width // SSD_P
    gn = SSD_G * SSD_N
    cdim = width + 2 * gn
    in_sizes = (Q_RANK, KV_RANK, ROPE, width, cdim, hs)
    d_in = sum(in_sizes)
    tail = LANES
    off_xbc = 0
    off_cq = cdim
    off_ckv = off_cq + Q_RANK
    off_z = off_ckv + KV_RANK
    off_tail = off_z + width
    d_in_p = _round_up(off_tail + tail, 256)
    gw = width // SSD_G
    assert off_cq % Q_RANK == 0 and off_ckv % KV_RANK == 0 and off_z % gw == 0 and off_tail % LANES == 0
    qk_head = NOPE + ROPE
    scale = qk_head ** -0.5
    tm = _tile(t, ROW_TILE, 8)
    tmw = _tile(t, ROW_TILE // 2, 8)

    sp = jnp.stack([my_chip, my_c]).astype(jnp.int32)

    def pair_sums(names, gl, from_sib):
        res = []
        for n, g, fs in zip(names, gl, from_sib):
            _, r2, cs = fs.shape
            tr = _row_tile(r2, cs, 16)
            nb = r2 // tr
            res.append(_blocked(
                lambda a, b: a + b, (nchip, nb),
                [(g, (None, tr, cs), lambda k, i, s, nb=nb: (k, s[1] * nb + i, 0)),
                 (fs, (None, tr, cs), lambda k, i, s: (k, i, 0))],
                [((nchip, r2, cs), bf16, (None, tr, cs), lambda k, i, s: (k, i, 0))], [], "pair_sum_" + n, sp=sp)[0])
        return res

    def chip_sums(names, gl, from_sib, from_chips):
        res = []
        for n, g, fs, fc in zip(names, gl, from_sib, from_chips):
            _, r2, cs = fs.shape
            tr = _row_tile(r2, cs, 16)
            nb = r2 // tr
            res.append(_blocked(
                lambda a, b, r1, r2_, r3: ((a + b) + r1.astype(f32)) + r2_.astype(f32) + r3.astype(f32), (nb,),
                [(g, (None, tr, cs), lambda i, s, nb=nb: (s[0], s[1] * nb + i, 0)),
                 (fs, (None, tr, cs), lambda i, s: (s[0], i, 0)),
                 (fc, (None, tr, cs), lambda i, s: (s[0] ^ 1, i, 0)),
                 (fc, (None, tr, cs), lambda i, s: (s[0] ^ 2, i, 0)),
                 (fc, (None, tr, cs), lambda i, s: (s[0] ^ 3, i, 0))],
                [((r2, cs), f32, (tr, cs), lambda i, s: (i, 0))], [], "chip_sum_" + n, sp=sp)[0])
        return res

    ck, ccs = wts["conv_w"].shape
    cs_in = wts["w_in"].shape[1]
    flat_t = lambda a3: jnp.swapaxes(a3, 1, 2).reshape(1, cs_in * d // LANES, LANES)
    (u,), (w_in_g, conv_g) = _rowwise(
        lambda a, w: _rms(a, w), t, tm, [_rows(xs, tm), _par(wts["pre_mix_norm_w"].reshape(1, -1))],
        [_rout(t, d, bf16, tm)], [], "pre_mix_norm",
        comm=_gather_weights([flat_t(w_in)[0].astype(bf16)], [wts["conv_w"]]), at=(0.0, 0.5))
    cat_cols = lambda g: jnp.concatenate([g[k] for k in range(nchip)], axis=1)
    conv_full = cat_cols(conv_g)
    mixer_gather = _gather_weights([wts[n].astype(bf16) for n in MIXER[1:]], [])
    ffn_gather = _gather_weights([wts[n].astype(bf16) for n in FFN[:2]], [])
    w_down_b = wts["w_down"].astype(bf16)
    dn2 = w_down_b.shape[0] // 2
    down_gathers = [_gather_weights([w_down_b[:dn2]], []), _gather_weights([w_down_b[dn2:]], [])]

    wi = w_in_g.reshape(nchip * cs_in, d)
    o = np.cumsum((0,) + in_sizes)
    seg = lambda i: wi[o[i]:o[i + 1]]
    w_in_pt = jnp.concatenate([seg(4), seg(0), seg(1), seg(3), seg(2), seg(5),
                               jnp.zeros((d_in_p - off_tail - ROPE - hs, d), bf16)], axis=0)

    inv_freq = ROPE_THETA ** (-jnp.arange(0, ROPE, 2, dtype=f32) / ROPE)
    ang = positions[0].astype(f32)[:, None] * inv_freq
    cos, sin = jnp.cos(ang), jnp.sin(ang)
    z32, z64, z96 = jnp.zeros((t, 32), f32), jnp.zeros((t, 64), f32), jnp.zeros((t, 96), f32)
    cosp = jnp.concatenate([cos, cos, z64], axis=1)
    sina = jnp.concatenate([-sin, z96], axis=1)
    sinb = jnp.concatenate([z32, sin, z64], axis=1)

    row = lambda a: a.reshape(1, -1)
    w_pre_mix, w_post_mix = row(wts["pre_mix_norm_w"]), row(wts["post_mix_norm_w"])
    w_pre_ffn, w_post_ffn = row(wts["pre_ffn_norm_w"]), row(wts["post_ffn_norm_w"])
    w_qn, w_kvn = row(wts["q_norm_w"]), row(wts["kv_norm_w"])
    w_attn_n, w_ssd_n = row(wts["attn_out_norm_w"]), row(wts["ssd_norm_w"])
    conv_b_r = row(wts["conv_b"])
    dtb, alog, dskip = row(wts["dt_bias"]), row(wts["a_log"]), row(wts["d_skip"])
    dtb_t, alog_t = dtb.reshape(hs, 1), alog.reshape(hs, 1)

    proj, (w_uq_g, w_ukv_f, w_out_g) = _matmul([(u, w_in_pt)], "nt", f32, "in_proj", comm=mixer_gather,
                                               at=(0.0, 0.6))
    w_uq_p = jnp.pad(cat_cols(w_uq_g).reshape(Q_RANK, nh, qk_head), ((0, 0), (0, 0), (0, QK_PAD - qk_head))
                     ).reshape(Q_RANK, nh * QK_PAD)
    w_out_f = w_out_g.reshape(-1, w_out_g.shape[2])
    cq_in = _rows(proj, tm, Q_RANK, off_cq // Q_RANK)
    ckv_in = _rows(proj, tm, KV_RANK, off_ckv // KV_RANK)
    cqn = _rowwise(lambda a, w: _rms(a, w), t, tm, [cq_in, _par(w_qn)], [_rout(t, Q_RANK, bf16, tm)], [], "q_norm")[0]
    ckvn = _rowwise(lambda a, w: _rms(a, w), t, tm, [ckv_in, _par(w_kvn)], [_rout(t, KV_RANK, bf16, tm)], [],
                    "kv_norm")[0]
    q_raw = _matmul([(cqn, w_uq_p)], "nn", f32, "q_up")
    kv = _matmul([(ckvn, w_ukv_f)], "nn", bf16, "kv_up")

    def q_rope_fn(qt, cp, sa, sb):
        parts = []
        for h in range(nh):
            parts.append(qt[:, h * QK_PAD: h * QK_PAD + NOPE] * scale)
            parts.append(_rope(qt[:, h * QK_PAD + NOPE:(h + 1) * QK_PAD], cp, sa, sb) * scale)
        return jnp.concatenate(parts, axis=1)

    tail_cb = off_tail // LANES
    q2 = _rowwise(q_rope_fn, t, tm, [_rows(q_raw, tm), _rows(cosp, tm), _rows(sina, tm), _rows(sinb, tm)],
                  [_rout(t, nh * QK_PAD, bf16, tm)], [], "q_rope")[0]
    kr2 = _rowwise(_rope, t, tm, [_rows(proj, tm, LANES, tail_cb), _rows(cosp, tm), _rows(sina, tm), _rows(sinb, tm)],
                   [_rout(t, LANES, bf16, tm)], [], "k_rope")[0]
    o_att, lse, ffn_w = _flash_fwd(q2, kv, kr2, nh, scale, ffn_gather, (0.0, 0.75))
    w_gate_f, w_up_f = ffn_w

    xbc_act = _conv_fwd(proj, conv_full, conv_b_r, cdim, tm)
    dt_raw, dt_raw_t = _blocked(
        lambda blk: (blk[:, ROPE:ROPE + hs], blk.T[ROPE:ROPE + hs, :]), (t // tm,),
        [_rows(proj, tm, LANES, tail_cb)],
        [((t, hs), f32, (tm, hs), lambda i: (i, 0)), ((hs, t), f32, (hs, tm), lambda i: (0, i))], [], "dt_split")
    y_ssd, hprev, (down_a,) = _ssd_fwd(xbc_act, dt_raw, dt_raw_t, dtb, dtb_t, alog, alog_t, dskip, width,
                                       down_gathers[0], (0.0, 0.6))
    z_ins = [_rows(proj, tm, gw, off_z // gw + i) for i in range(SSD_G)]

    def mix_norms_fn(ov, yv, *rest):
        zs, wa, ws = rest[:SSD_G], rest[SSD_G], rest[SSD_G + 1]
        outs = [_rms(ov, wa)]
        for i in range(SSD_G):
            sl = slice(i * gw, (i + 1) * gw)
            outs.append(_rms(yv[:, sl] * (zs[i] * _sigmoid(zs[i])), ws[:, sl]))
        return jnp.concatenate(outs, axis=1)

    cat = _rowwise(mix_norms_fn, t, tm, [_rows(o_att, tm), _rows(y_ssd, tm)] + z_ins + [_par(w_attn_n), _par(w_ssd_n)],
                   [_rout(t, d, bf16, tm)], [], "attn_ssd_out_norms")[0]
    mix, (down_b,) = _matmul([(cat, w_out_f)], "nn", f32, "out_proj", comm=down_gathers[1], at=(0.0, 0.6))
    w_down_f = jnp.concatenate([down_a, down_b], axis=1).reshape(-1, down_a.shape[2])

    def post_mix_fn(mx, xv, w1, w2):
        h1v = xv + _rms(mx, w1)
        return h1v, _rms(h1v, w2)

    h1, v_in = _rowwise(post_mix_fn, t, tmw, [_rows(mix, tmw), _rows(xs, tmw), _par(w_post_mix), _par(w_pre_ffn)],
                        [_rout(t, d, f32, tmw), _rout(t, d, bf16, tmw)], [], "post_mix_pre_ffn_norm")
    g_ff, u_ff, act = _ffn_up(v_in, w_gate_f, w_up_f)
    ffn = _matmul([(act, w_down_f)], "nn", f32, "ffn_down")

    def final_fn(fv, h1v, tg, w):
        h2 = h1v + _rms(fv, w)
        err = h2 - tg
        lpart = 0.5 * jnp.sum(jnp.sum(err * err, axis=1, keepdims=True), axis=0, keepdims=True) / d
        dh2 = err / d
        dff, dw = _rms_bwd(fv, w, dh2)
        return dff, dh2, jnp.broadcast_to(lpart, (1, LANES)), dw

    dffn, dh2, loss_acc, g_post_ffn = _rowwise(
        final_fn, t, tmw, [_rows(ffn, tmw), _rows(h1, tmw), _rows(tgt, tmw), _par(w_post_ffn)],
        [_rout(t, d, bf16, tmw), _rout(t, d, f32, tmw)], [(1, LANES), (1, d)], "loss_post_ffn_norm_bwd")
    loss = lax.psum(loss_acc[0, 0], ("x", "y", "c"))

    dg_ff, du_ff = _ffn_down_bwd(dffn, w_down_f, g_ff, u_ff)
    gw_down = _matmul([(act, dffn)], "tn", f32, "grad_w_down")
    gw_gate = _matmul([(v_in, dg_ff)], "tn", f32, "grad_w_gate", out_chunks=nchip)
    gw_up = _matmul([(v_in, du_ff)], "tn", f32, "grad_w_up", out_chunks=nchip)
    ffn_g = [gw_gate, gw_up, gw_down.reshape(nchip, -1, d)]
    dv_in, ffn_from_sib = _matmul([(dg_ff, w_gate_f), (du_ff, w_up_f)], "nt", f32, "ffn_up_bwd",
                                  comm=_sibling_send_halves(ffn_g), at=(0.0,))
    ffn_pairs = pair_sums(FFN, ffn_g, ffn_from_sib)

    def mid_bwd_fn(h1v, dvv, dh2v, mx, w_pf, w_pm):
        dxn, dw_pf = _rms_bwd(h1v, w_pf, dvv)
        dh1v = dh2v + dxn
        dmx, dw_pm = _rms_bwd(mx, w_pm, dh1v)
        return dh1v, dmx, dw_pf, dw_pm

    dh1, dmix, g_pre_ffn, g_post_mix = _rowwise(
        mid_bwd_fn, t, tmw, [_rows(h1, tmw), _rows(dv_in, tmw), _rows(dh2, tmw), _rows(mix, tmw),
                             _par(w_pre_ffn), _par(w_post_mix)],
        [_rout(t, d, f32, tmw), _rout(t, d, bf16, tmw)], [(1, d), (1, d)], "pre_ffn_post_mix_norm_bwd")
    dcat = _matmul([(dmix, w_out_f)], "nt", f32, "out_proj_bwd")
    gw_out = _matmul([(cat, dmix)], "tn", f32, "grad_w_out")

    def mix_norms_bwd_fn(ov, yv, *rest):
        zs, dcv, wa, ws = rest[:SSD_G], rest[SSD_G], rest[SSD_G + 1], rest[SSD_G + 2]
        dov, dwa = _rms_bwd(ov, wa, dcv[:, :mla_w])
        dl = [jnp.broadcast_to(jnp.sum(dov[:, h * V_HEAD:(h + 1) * V_HEAD] * ov[:, h * V_HEAD:(h + 1) * V_HEAD],
                                       axis=1, keepdims=True), (ov.shape[0], V_HEAD)) for h in range(nh)]
        dys, dzs, dws = [], [], []
        for i in range(SSD_G):
            sl = slice(i * gw, (i + 1) * gw)
            zv, yi = zs[i], yv[:, sl]
            sg = _sigmoid(zv)
            sz = zv * sg
            dgi, dwi = _rms_bwd(yi * sz, ws[:, sl], dcv[:, mla_w + i * gw: mla_w + (i + 1) * gw])
            dys.append(dgi * sz)
            dzs.append(dgi * yi * (sg * (1.0 + zv * (1.0 - sg))))
            dws.append(dwi)
        cc = lambda a: jnp.concatenate(a, axis=1)
        return dov, cc(dl), cc(dys), cc(dzs), dwa, cc(dws)

    do_att, delta, dy_ssd, dz, g_attn_n, g_ssd_n = _rowwise(
        mix_norms_bwd_fn, t, tm,
        [_rows(o_att, tm), _rows(y_ssd, tm)] + z_ins + [_rows(dcat, tm), _par(w_attn_n), _par(w_ssd_n)],
        [_rout(t, mla_w, bf16, tm), _rout(t, mla_w, f32, tm), _rout(t, width, f32, tm), _rout(t, width, bf16, tm)],
        [(1, mla_w), (1, width)], "attn_ssd_out_norms_bwd")
    dkv, dkr_h, dq2, gu_from_chips = _flash_bwd(q2, kv, kr2, do_att, lse, delta, nh, scale,
                                                _chips_exchange(ffn_pairs[:2]), (0.0,))

    def q_rope_bwd_fn(dqt, cp, sa, sb):
        parts = []
        for h in range(nh):
            parts.append(dqt[:, h * QK_PAD: h * QK_PAD + NOPE] * scale)
            parts.append(_rope_bwd(dqt[:, h * QK_PAD + NOPE:(h + 1) * QK_PAD], cp, sa, sb) * scale)
        return jnp.concatenate(parts, axis=1)

    dq_raw = _rowwise(q_rope_bwd_fn, t, tm, [_rows(dq2, tm), _rows(cosp, tm), _rows(sina, tm), _rows(sinb, tm)],
                      [_rout(t, nh * QK_PAD, bf16, tm)], [], "q_rope_bwd")[0]

    def k_rope_bwd_fn(dk, cp, sa, sb):
        tot = dk[:, 0:LANES]
        for h in range(1, nh):
            tot = tot + dk[:, h * LANES:(h + 1) * LANES]
        return _rope_bwd(tot, cp, sa, sb)

    dkr = _rowwise(k_rope_bwd_fn, t, tm, [_rows(dkr_h, tm), _rows(cosp, tm), _rows(sina, tm), _rows(sinb, tm)],
                   [_rout(t, LANES, f32, tm)], [], "k_rope_bwd")[0]
    gw_uq_p = _matmul([(cqn, dq_raw)], "tn", f32, "grad_w_uq")
    gw_ukv = _matmul([(ckvn, dkv)], "tn", f32, "grad_w_ukv", out_chunks=nchip)
    dcqn = _matmul([(dq_raw, w_uq_p)], "nt", f32, "q_up_bwd")
    dckvn = _matmul([(dkv, w_ukv_f)], "nt", f32, "kv_up_bwd")

    def lat_norm_bwd_fn(a, w, dyv):
        return _rms_bwd(a, w, dyv)

    dcq, g_qn = _rowwise(lat_norm_bwd_fn, t, tm, [cq_in, _par(w_qn), _rows(dcqn, tm)],
                         [_rout(t, Q_RANK, bf16, tm)], [(1, Q_RANK)], "q_norm_bwd")
    dckv, g_kvn = _rowwise(lat_norm_bwd_fn, t, tm, [ckv_in, _par(w_kvn), _rows(dckvn, tm)],
                           [_rout(t, KV_RANK, bf16, tm)], [(1, KV_RANK)], "kv_norm_bwd")

    dxbc_act, ddt_raw, g_alog, g_dskip, g_dtb, down_from_chips = _ssd_bwd(
        xbc_act, dt_raw, dt_raw_t, dtb, dtb_t, alog, alog_t, dskip, hprev, dy_ssd, width,
        _chips_exchange(ffn_pairs[2:]), (0.0,))
    ffn_halves = chip_sums(FFN, ffn_g, ffn_from_sib, gu_from_chips + down_from_chips)
    dpre, gcw0, gcw1, gcw2, gcw3, g_conv_b = _conv_bwd_pre(proj, conv_full, conv_b_r, dxbc_act, cdim, tm)
    g_conv_w = jnp.concatenate([gcw0, gcw1, gcw2, gcw3], axis=0)
    dxbc = _conv_bwd_dx(dpre, conv_full, tm)

    def dproj_fn(a, b, c, e, kr_blk, dt_blk):
        parts = [a, b, c, e, (kr_blk + dt_blk).astype(bf16)]
        if d_in_p > off_tail + tail:
            parts.append(jnp.zeros((a.shape[0], d_in_p - off_tail - tail), bf16))
        return jnp.concatenate(parts, axis=1)

    dproj = _rowwise(dproj_fn, t, tm, [_rows(dxbc, tm), _rows(dcq, tm), _rows(dckv, tm), _rows(dz, tm),
                                       _rows(dkr, tm), _rows(ddt_raw, tm)],
                     [_rout(t, d_in_p, bf16, tm)], [], "in_proj_grad_pack")[0]
    split_cols = lambda gf: jnp.stack(jnp.split(gf, nchip, axis=1))
    split_rows = lambda gf: gf.reshape(nchip, gf.shape[0] // nchip, gf.shape[1])
    gw_uq = gw_uq_p.reshape(Q_RANK, nh, QK_PAD)[:, :, :qk_head].reshape(Q_RANK, nh * qk_head)
    rest_g = [split_cols(gw_uq), gw_ukv, split_rows(gw_out)]
    gw_in_pt, both = _matmul([(dproj, u)], "tn", f32, "grad_w_in",
                             comm=_both(_sibling_send_halves(rest_g), _sibling_swap(ffn_halves)),
                             at=(0.0,))
    rest_from_sib, ffn_sib_halves = both[:len(rest_g)], both[len(rest_g):]
    gseg = lambda a, b: gw_in_pt[a:b]
    gw_in_t = jnp.concatenate([gseg(off_cq, off_ckv), gseg(off_ckv, off_z), gseg(off_tail, off_tail + ROPE),
                               gseg(off_z, off_tail), gseg(off_xbc, off_cq),
                               gseg(off_tail + ROPE, off_tail + ROPE + hs)], axis=0)
    in_g = [gw_in_t.reshape(nchip, cs_in * d // LANES, LANES)]
    rest_pairs = pair_sums(MIXER[1:], rest_g, rest_from_sib)
    du_in, both = _matmul([(dproj, w_in_pt)], "nn", f32, "in_proj_bwd",
                          comm=_both(_sibling_send_halves(in_g), _chips_exchange(rest_pairs)), at=(0.0,))
    in_from_sib, rest_from_chips = both[:1], both[1:]
    in_pairs = pair_sums(MIXER[:1], in_g, in_from_sib)

    def first_bwd_fn(xv, duv, dh1v, w):
        dxn, dw = _rms_bwd(xv, w, duv)
        return dh1v + dxn, dw

    (grad_x, g_pre_mix), in_from_chips = _rowwise(
        first_bwd_fn, t, tmw, [_rows(xs, tmw), _rows(du_in, tmw), _rows(dh1, tmw), _par(w_pre_mix)],
        [_rout(t, d, f32, tmw)], [(1, d)], "pre_mix_norm_bwd", comm=_chips_exchange(in_pairs), at=(0.0,))

    mix_halves = (chip_sums(MIXER[:1], in_g, in_from_sib, in_from_chips)
                  + chip_sums(MIXER[1:], rest_g, rest_from_sib, rest_from_chips))
    halves = mix_halves + ffn_halves
    sib_halves = _comm_call(_sibling_swap(mix_halves), "sibling_swap") + ffn_sib_halves
    gshard = {}

    gsmall = {"q_norm_w": g_qn, "kv_norm_w": g_kvn, "conv_w": g_conv_w, "conv_b": g_conv_b, "dt_bias": g_dtb,
              "a_log": g_alog, "d_skip": g_dskip, "ssd_norm_w": g_ssd_n, "attn_out_norm_w": g_attn_n,
              "pre_mix_norm_w": g_pre_mix, "post_mix_norm_w": g_post_mix, "pre_ffn_norm_w": g_pre_ffn,
              "post_ffn_norm_w": g_post_ffn}
    small_sizes = [int(np.prod(gsmall[n].shape)) for n in SMALL]
    srows = _round_up(-(-sum(small_sizes) // LANES), 8)
    spart = _flat_pad([gsmall[n] for n in SMALL], srows * LANES).reshape(srows, LANES)
    sall = _gather_all(spart)

    def sum8_fn(a):
        tot = a[0]
        for k in range(1, 8):
            tot = tot + a[k]
        return tot

    ssum = _blocked(sum8_fn, (1,), [(sall, sall.shape, lambda i: (0, 0, 0))],
                    [((srows, LANES), f32, (srows, LANES), lambda i: (0, 0))], [], "small_grad_sum")[0].reshape(-1)
    gred = {}
    off = 0
    for n, sz in zip(SMALL, small_sizes):
        gred[n] = ssum[off:off + sz].reshape(gsmall[n].shape)
        off += sz
    gshard["conv_w"] = lax.dynamic_slice_in_dim(gred["conv_w"], my_chip * ccs, ccs, axis=1)
    for n in SMALL:
        if n != "conv_w":
            gshard[n] = gred[n].reshape(wts[n].shape)

    delta, new_m, new_v = {}, {}, {}
    big_out = {}
    for n, mine_h, sib_h in zip(BIG, halves, sib_halves):
        view = flat_t if n == "w_in" else (lambda a3: a3)
        res = _adamw_halves(view(local[n]), view(local["m_" + n]), view(local["v_" + n]), mine_h, sib_h, sp,
                            "adamw_" + n)
        big_out[n] = [jnp.swapaxes(a.reshape(1, cs_in, d), 1, 2) for a in res] if n == "w_in" else res
    pack = lambda src: _flat_pad([src[n] for n in SMALL], srows * LANES).reshape(srows, LANES)
    sd, sm, sv = _adamw_call(pack(wts), pack(gshard), pack(mom_m), pack(mom_v), "adamw_small")
    off = 0
    for n in SMALL:
        sz = int(np.prod(wts[n].shape))
        for dst, src in ((delta, sd), (new_m, sm), (new_v, sv)):
            dst[n] = src.reshape(-1)[off:off + sz].reshape(wts[n].shape)
        off += sz

    small_out = (gshard, delta, new_m, new_v)
    pick = lambda k: [big_out[n][k] if n in big_out else small_out[k][n][None] for n in ORDER]
    return (loss, grad_x[None], *pick(0), *pick(1), *pick(2), *pick(3))
```
